```python
import math
import jax, jax.numpy as jnp
from jax import lax
import numpy as np

D_MODEL = 1024
BATCH = 8
SEQ = 4096
DEPTH = 1

CHUNK = 64
SB_HEADS = 8
SB_HEAD_DIM = D_MODEL // 16
SB_WIDTH = SB_HEADS * SB_HEAD_DIM
CA_HEADS = 8
CA_HEAD_DIM = D_MODEL // 16
CA_WIDTH = CA_HEADS * CA_HEAD_DIM
CA_PREV_CHUNKS = 8
CA_BAND = (CA_PREV_CHUNKS + 1) * CHUNK
REL_CLIP = 256
Q_BLOCK = 128
D_FF = 4 * D_MODEL
DEEPNORM_ALPHA = (2.0 * DEPTH) ** 0.25
DEEPNORM_BETA = (8.0 * DEPTH) ** -0.25
LN_EPS = 1e-5
IN_COLS = 3 * SB_WIDTH + 3 * CA_WIDTH + 2 * D_MODEL

kernel_name = "hybrid_stickbreak_chunkrel_deepnorm"


def _layer_norm(x, g, b):
    xf = x.astype(jnp.float32)
    mu = jnp.mean(xf, axis=-1, keepdims=True)
    var = jnp.mean(jnp.square(xf - mu), axis=-1, keepdims=True)
    y = (xf - mu) * lax.rsqrt(var + LN_EPS) * g.astype(jnp.float32) + b.astype(jnp.float32)
    return y.astype(x.dtype)


def _stick_breaking(q, k, v):
    b, s_len, h, dh = q.shape
    scale = dh ** -0.5
    qh = q.transpose(0, 2, 1, 3)
    kh = k.transpose(0, 2, 1, 3)
    vh = v.transpose(0, 2, 1, 3).astype(jnp.float32)
    outs = []
    for i in range(s_len // Q_BLOCK):
        start = i * Q_BLOCK
        end = start + Q_BLOCK
        z = jnp.einsum('bhqd,bhkd->bhqk', qh[:, :, start:end], kh[:, :, :end],
                       preferred_element_type=jnp.float32) * scale
        t_pos = start + jnp.arange(Q_BLOCK)[:, None]
        s_pos = jnp.arange(end)[None, :]
        strict = s_pos < t_pos
        log_keep = jnp.where(strict, jax.nn.log_sigmoid(-z), 0.0)
        between = lax.cumsum(log_keep, axis=3, reverse=True) - log_keep
        a = jnp.where(strict, jnp.exp(jax.nn.log_sigmoid(z) + between), 0.0)
        outs.append(jnp.einsum('bhqk,bhkd->bqhd', a, vh[:, :, :end]))
    o = jnp.concatenate(outs, axis=1)
    return o.reshape(b, s_len, h * dh).astype(q.dtype)


def _rel_index():
    i = np.arange(CHUNK)[:, None]
    kk = np.arange(CA_BAND)[None, :]
    dist = (CA_PREV_CHUNKS - kk // CHUNK) * CHUNK + i - kk % CHUNK
    return np.clip(dist, -REL_CLIP, REL_CLIP) + REL_CLIP


def _chunk_attention(q, k, v, rel_bias):
    b, s_len, h, dh = q.shape
    n_chunks = s_len // CHUNK
    scale = dh ** -0.5
    pad = ((0, 0), (CA_PREV_CHUNKS, 0), (0, 0), (0, 0), (0, 0))
    qc = q.reshape(b, n_chunks, CHUNK, h, dh)
    kc = jnp.pad(k.reshape(b, n_chunks, CHUNK, h, dh), pad)
    vc = jnp.pad(v.reshape(b, n_chunks, CHUNK, h, dh), pad)
    band = jnp.arange(n_chunks)[:, None] + jnp.arange(CA_PREV_CHUNKS + 1)[None, :]
    kb = kc[:, band].reshape(b, n_chunks, CA_BAND, h, dh)
    vb = vc[:, band].reshape(b, n_chunks, CA_BAND, h, dh).astype(jnp.float32)
    scores = jnp.einsum('bcqhd,bckhd->bhcqk', qc, kb,
                        preferred_element_type=jnp.float32) * scale
    bias = rel_bias.astype(jnp.float32)[:, _rel_index()]
    valid = jnp.repeat(band >= CA_PREV_CHUNKS, CHUNK, axis=1)
    scores = jnp.where(valid[None, None, :, None, :], scores + bias[:, None], -jnp.inf)
    p = jax.nn.softmax(scores, axis=-1)
    o = jnp.einsum('bhcqk,bckhd->bcqhd', p, vb)
    return o.reshape(b, s_len, h * dh).astype(q.dtype)


def _fwd_setup_inputs(seed: int = 0) -> dict:
    key = jax.random.key(seed)
    ks = jax.random.split(key, 13)
    beta = DEEPNORM_BETA
    x = jax.random.normal(ks[0], (BATCH, SEQ, D_MODEL), jnp.float32)
    col_scale = jnp.concatenate([
        jnp.ones((2 * SB_WIDTH,), jnp.float32), jnp.full((SB_WIDTH,), beta, jnp.float32),
        jnp.ones((2 * CA_WIDTH,), jnp.float32), jnp.full((CA_WIDTH,), beta, jnp.float32),
        jnp.ones((2 * D_MODEL,), jnp.float32)])
    w_in = jax.random.normal(ks[1], (D_MODEL, IN_COLS), jnp.float32) * D_MODEL ** -0.5 * col_scale
    b_gate = 0.1 * jax.random.normal(ks[2], (2 * D_MODEL,), jnp.float32)
    w_sb_proj = jax.random.normal(ks[3], (SB_WIDTH, D_MODEL), jnp.float32) * SB_WIDTH ** -0.5 * beta
    w_ca_proj = jax.random.normal(ks[4], (CA_WIDTH, D_MODEL), jnp.float32) * CA_WIDTH ** -0.5 * beta
    rel_bias = 0.2 * jax.random.normal(ks[5], (CA_HEADS, 2 * REL_CLIP + 1), jnp.float32)
    w_out = jax.random.normal(ks[6], (D_MODEL, D_MODEL), jnp.float32) * D_MODEL ** -0.5 * beta
    ln1_g = 1.0 + 0.02 * jax.random.normal(ks[7], (D_MODEL,), jnp.float32)
    ln1_b = 0.02 * jax.random.normal(ks[8], (D_MODEL,), jnp.float32)
    w_mlp_in = jax.random.normal(ks[9], (D_MODEL, D_FF), jnp.float32) * D_MODEL ** -0.5 * beta
    w_mlp_out = jax.random.normal(ks[10], (D_FF, D_MODEL), jnp.float32) * D_FF ** -0.5 * beta
    ln2_g = 1.0 + 0.02 * jax.random.normal(ks[11], (D_MODEL,), jnp.float32)
    ln2_b = 0.02 * jax.random.normal(ks[12], (D_MODEL,), jnp.float32)
    return {"x": x, "w_in": w_in, "b_gate": b_gate, "w_sb_proj": w_sb_proj,
            "w_ca_proj": w_ca_proj, "rel_bias": rel_bias, "w_out": w_out,
            "ln1_g": ln1_g, "ln1_b": ln1_b, "w_mlp_in": w_mlp_in, "w_mlp_out": w_mlp_out,
            "ln2_g": ln2_g, "ln2_b": ln2_b}


def _fwd_reference(x, w_in, b_gate, w_sb_proj, w_ca_proj, rel_bias, w_out,
              ln1_g, ln1_b, w_mlp_in, w_mlp_out, ln2_g, ln2_b):
    b, s_len, _ = x.shape
    for _layer in range(DEPTH):
        h = x @ w_in
        o = 0
        q_sb = h[..., o:o + SB_WIDTH]; o += SB_WIDTH
        k_sb = h[..., o:o + SB_WIDTH]; o += SB_WIDTH
        v_sb = h[..., o:o + SB_WIDTH]; o += SB_WIDTH
        q_ca = h[..., o:o + CA_WIDTH]; o += CA_WIDTH
        k_ca = h[..., o:o + CA_WIDTH]; o += CA_WIDTH
        v_ca = h[..., o:o + CA_WIDTH]; o += CA_WIDTH
        gate_logits = h[..., o:o + 2 * D_MODEL] + b_gate
        sb_shape = (b, s_len, SB_HEADS, SB_HEAD_DIM)
        ca_shape = (b, s_len, CA_HEADS, CA_HEAD_DIM)
        y_sb = _stick_breaking(q_sb.reshape(sb_shape), k_sb.reshape(sb_shape),
                               v_sb.reshape(sb_shape)) @ w_sb_proj
        y_ca = _chunk_attention(q_ca.reshape(ca_shape), k_ca.reshape(ca_shape),
                                v_ca.reshape(ca_shape), rel_bias) @ w_ca_proj
        gates = jax.nn.sigmoid(gate_logits.astype(jnp.float32))
        merged = (gates[..., :D_MODEL] * y_sb + gates[..., D_MODEL:] * y_ca).astype(x.dtype)
        x = _layer_norm(DEEPNORM_ALPHA * x + merged @ w_out, ln1_g, ln1_b)
        ff = jnp.square(jax.nn.relu(x @ w_mlp_in)) @ w_mlp_out
        x = _layer_norm(DEEPNORM_ALPHA * x + ff, ln2_g, ln2_b)
    return x


import jax as _jax
import jax.numpy as _jnp

TWIN_FORMAT = 'train_step'
FWD_PARAMS = ['x', 'w_in', 'b_gate', 'w_sb_proj', 'w_ca_proj', 'rel_bias', 'w_out', 'ln1_g', 'ln1_b', 'w_mlp_in', 'w_mlp_out', 'ln2_g', 'ln2_b']
TWIN_WEIGHTS = ['w_in', 'b_gate', 'w_sb_proj', 'w_ca_proj', 'rel_bias', 'w_out', 'ln1_g', 'ln1_b', 'w_mlp_in', 'w_mlp_out', 'ln2_g', 'ln2_b']
TWIN_DIFF_INPUT = 'x'
TWIN_INPUTS = ['x', 'w_in', 'b_gate', 'w_sb_proj', 'w_ca_proj', 'rel_bias', 'w_out', 'ln1_g', 'ln1_b', 'w_mlp_in', 'w_mlp_out', 'ln2_g', 'ln2_b', 'loss_target', 'm_w_in', 'm_b_gate', 'm_w_sb_proj', 'm_w_ca_proj', 'm_rel_bias', 'm_w_out', 'm_ln1_g', 'm_ln1_b', 'm_w_mlp_in', 'm_w_mlp_out', 'm_ln2_g', 'm_ln2_b', 'v_w_in', 'v_b_gate', 'v_w_sb_proj', 'v_w_ca_proj', 'v_rel_bias', 'v_w_out', 'v_ln1_g', 'v_ln1_b', 'v_w_mlp_in', 'v_w_mlp_out', 'v_ln2_g', 'v_ln2_b']
TWIN_OUTPUTS = ['loss', 'grad_x', 'grad_w_in', 'grad_b_gate', 'grad_w_sb_proj', 'grad_w_ca_proj', 'grad_rel_bias', 'grad_w_out', 'grad_ln1_g', 'grad_ln1_b', 'grad_w_mlp_in', 'grad_w_mlp_out', 'grad_ln2_g', 'grad_ln2_b', 'delta_w_in', 'delta_b_gate', 'delta_w_sb_proj', 'delta_w_ca_proj', 'delta_rel_bias', 'delta_w_out', 'delta_ln1_g', 'delta_ln1_b', 'delta_w_mlp_in', 'delta_w_mlp_out', 'delta_ln2_g', 'delta_ln2_b', 'new_m_w_in', 'new_m_b_gate', 'new_m_w_sb_proj', 'new_m_w_ca_proj', 'new_m_rel_bias', 'new_m_w_out', 'new_m_ln1_g', 'new_m_ln1_b', 'new_m_w_mlp_in', 'new_m_w_mlp_out', 'new_m_ln2_g', 'new_m_ln2_b', 'new_v_w_in', 'new_v_b_gate', 'new_v_w_sb_proj', 'new_v_w_ca_proj', 'new_v_rel_bias', 'new_v_w_out', 'new_v_ln1_g', 'new_v_ln1_b', 'new_v_w_mlp_in', 'new_v_w_mlp_out', 'new_v_ln2_g', 'new_v_ln2_b']
TWIN_LEAF_KINDS = {'loss': 'loss', 'grad_x': 'grad_x', 'grad_w_in': 'grad_w', 'grad_b_gate': 'grad_w', 'grad_w_sb_proj': 'grad_w', 'grad_w_ca_proj': 'grad_w', 'grad_rel_bias': 'grad_w', 'grad_w_out': 'grad_w', 'grad_ln1_g': 'grad_w', 'grad_ln1_b': 'grad_w', 'grad_w_mlp_in': 'grad_w', 'grad_w_mlp_out': 'grad_w', 'grad_ln2_g': 'grad_w', 'grad_ln2_b': 'grad_w', 'delta_w_in': 'delta_w', 'delta_b_gate': 'delta_w', 'delta_w_sb_proj': 'delta_w', 'delta_w_ca_proj': 'delta_w', 'delta_rel_bias': 'delta_w', 'delta_w_out': 'delta_w', 'delta_ln1_g': 'delta_w', 'delta_ln1_b': 'delta_w', 'delta_w_mlp_in': 'delta_w', 'delta_w_mlp_out': 'delta_w', 'delta_ln2_g': 'delta_w', 'delta_ln2_b': 'delta_w', 'new_m_w_in': 'new_m', 'new_m_b_gate': 'new_m', 'new_m_w_sb_proj': 'new_m', 'new_m_w_ca_proj': 'new_m', 'new_m_rel_bias': 'new_m', 'new_m_w_out': 'new_m', 'new_m_ln1_g': 'new_m', 'new_m_ln1_b': 'new_m', 'new_m_w_mlp_in': 'new_m', 'new_m_w_mlp_out': 'new_m', 'new_m_ln2_g': 'new_m', 'new_m_ln2_b': 'new_m', 'new_v_w_in': 'new_v', 'new_v_b_gate': 'new_v', 'new_v_w_sb_proj': 'new_v', 'new_v_w_ca_proj': 'new_v', 'new_v_rel_bias': 'new_v', 'new_v_w_out': 'new_v', 'new_v_ln1_g': 'new_v', 'new_v_ln1_b': 'new_v', 'new_v_w_mlp_in': 'new_v', 'new_v_w_mlp_out': 'new_v', 'new_v_ln2_g': 'new_v', 'new_v_ln2_b': 'new_v'}


def _forward(args):
    return _fwd_reference(*[args[k] for k in FWD_PARAMS])


def _output_shape():
    out = _jax.eval_shape(lambda: _forward(_fwd_setup_inputs(0)))
    return out.shape, out.dtype

N_MICROBATCH = 1
ADAM_LR = 0.001
ADAM_B1 = 0.9
ADAM_B2 = 0.999
ADAM_EPS = 1e-08
ADAM_WD = 0.01
ADAM_STEP = 10
PER_EXAMPLE_BATCH_AXIS = {'x': 0, 'loss_target': 0}
SHARED_INPUTS = []
_WEIGHT_DTYPES = {'w_in': _jnp.float32, 'b_gate': _jnp.float32, 'w_sb_proj': _jnp.float32, 'w_ca_proj': _jnp.float32, 'rel_bias': _jnp.float32, 'w_out': _jnp.float32, 'ln1_g': _jnp.float32, 'ln1_b': _jnp.float32, 'w_mlp_in': _jnp.float32, 'w_mlp_out': _jnp.float32, 'ln2_g': _jnp.float32, 'ln2_b': _jnp.float32}
MOMENT_SCALE = {'w_in': 9.391667e-03, 'b_gate': 3.269306e-03, 'w_sb_proj': 1.843282e-02, 'w_ca_proj': 4.382273e-03, 'rel_bias': 1.187738e-03, 'w_out': 1.894102e-02, 'ln1_g': 1.014275e+00, 'ln1_b': 4.489925e-01, 'w_mlp_in': 3.650054e-02, 'w_mlp_out': 8.054274e-02, 'ln2_g': 3.202518e+01, 'ln2_b': 2.526450e+00}


def _to_microbatches(a, axis):
    t = _jnp.moveaxis(a, axis, 0)
    t = t.reshape((N_MICROBATCH, t.shape[0] // N_MICROBATCH) + t.shape[1:])
    return _jnp.moveaxis(t, 1, axis + 1)


def setup_inputs(seed: int = 0) -> dict:
    inp = _fwd_setup_inputs(seed)
    key = _jax.random.fold_in(_jax.random.key(seed), 7919)
    shape, _ = _output_shape()
    out = dict(inp)
    out["loss_target"] = _jax.random.normal(_jax.random.fold_in(key, 0), shape, _jnp.float32)
    for i, name in enumerate(TWIN_WEIGHTS):
        w = inp[name].astype(_jnp.float32)
        if MOMENT_SCALE is None:
            s = _jnp.sqrt(_jnp.mean(_jnp.square(w)) + 1e-30)
        else:
            s = MOMENT_SCALE[name]
        km, kv = _jax.random.split(_jax.random.fold_in(key, i + 1))
        out[name] = w
        out["m_" + name] = s * _jax.random.normal(km, w.shape, _jnp.float32)
        out["v_" + name] = (s * s) * _jax.random.uniform(kv, w.shape, _jnp.float32, 0.5, 1.5)
    if N_MICROBATCH > 1:
        for name, axis in PER_EXAMPLE_BATCH_AXIS.items():
            out[name] = _to_microbatches(out[name], axis)
    return {'x': out['x'], 'w_in': out['w_in'], 'b_gate': out['b_gate'], 'w_sb_proj': out['w_sb_proj'], 'w_ca_proj': out['w_ca_proj'], 'rel_bias': out['rel_bias'], 'w_out': out['w_out'], 'ln1_g': out['ln1_g'], 'ln1_b': out['ln1_b'], 'w_mlp_in': out['w_mlp_in'], 'w_mlp_out': out['w_mlp_out'], 'ln2_g': out['ln2_g'], 'ln2_b': out['ln2_b'], 'loss_target': out['loss_target'], 'm_w_in': out['m_w_in'], 'm_b_gate': out['m_b_gate'], 'm_w_sb_proj': out['m_w_sb_proj'], 'm_w_ca_proj': out['m_w_ca_proj'], 'm_rel_bias': out['m_rel_bias'], 'm_w_out': out['m_w_out'], 'm_ln1_g': out['m_ln1_g'], 'm_ln1_b': out['m_ln1_b'], 'm_w_mlp_in': out['m_w_mlp_in'], 'm_w_mlp_out': out['m_w_mlp_out'], 'm_ln2_g': out['m_ln2_g'], 'm_ln2_b': out['m_ln2_b'], 'v_w_in': out['v_w_in'], 'v_b_gate': out['v_b_gate'], 'v_w_sb_proj': out['v_w_sb_proj'], 'v_w_ca_proj': out['v_w_ca_proj'], 'v_rel_bias': out['v_rel_bias'], 'v_w_out': out['v_w_out'], 'v_ln1_g': out['v_ln1_g'], 'v_ln1_b': out['v_ln1_b'], 'v_w_mlp_in': out['v_w_mlp_in'], 'v_w_mlp_out': out['v_w_mlp_out'], 'v_ln2_g': out['v_ln2_g'], 'v_ln2_b': out['v_ln2_b']}


def _loss(weights, diff, rest, loss_target):
    with _jax.named_scope("forward"):
        args = {**rest, TWIN_DIFF_INPUT: diff, **{k: w.astype(_WEIGHT_DTYPES[k]) for k, w in weights.items()}}
        y = _forward(args)
    with _jax.named_scope("loss_head"):
        err = _jnp.square(y.astype(_jnp.float32) - loss_target)
        return 0.5 * _jnp.sum(_jnp.mean(err, axis=-1)) if err.ndim else 0.5 * err


def _adamw(w, g, m, v):
    m = ADAM_B1 * m + (1.0 - ADAM_B1) * g
    v = ADAM_B2 * v + (1.0 - ADAM_B2) * _jnp.square(g)
    m_hat = m / (1.0 - ADAM_B1 ** ADAM_STEP)
    v_hat = v / (1.0 - ADAM_B2 ** ADAM_STEP)
    delta = -ADAM_LR * (m_hat / (_jnp.sqrt(v_hat) + ADAM_EPS) + ADAM_WD * w)
    return delta, m, v


def reference(x, w_in, b_gate, w_sb_proj, w_ca_proj, rel_bias, w_out, ln1_g, ln1_b, w_mlp_in, w_mlp_out, ln2_g, ln2_b, loss_target, m_w_in, m_b_gate, m_w_sb_proj, m_w_ca_proj, m_rel_bias, m_w_out, m_ln1_g, m_ln1_b, m_w_mlp_in, m_w_mlp_out, m_ln2_g, m_ln2_b, v_w_in, v_b_gate, v_w_sb_proj, v_w_ca_proj, v_rel_bias, v_w_out, v_ln1_g, v_ln1_b, v_w_mlp_in, v_w_mlp_out, v_ln2_g, v_ln2_b):
    given = dict(x=x, w_in=w_in, b_gate=b_gate, w_sb_proj=w_sb_proj, w_ca_proj=w_ca_proj, rel_bias=rel_bias, w_out=w_out, ln1_g=ln1_g, ln1_b=ln1_b, w_mlp_in=w_mlp_in, w_mlp_out=w_mlp_out, ln2_g=ln2_g, ln2_b=ln2_b, loss_target=loss_target, m_w_in=m_w_in, m_b_gate=m_b_gate, m_w_sb_proj=m_w_sb_proj, m_w_ca_proj=m_w_ca_proj, m_rel_bias=m_rel_bias, m_w_out=m_w_out, m_ln1_g=m_ln1_g, m_ln1_b=m_ln1_b, m_w_mlp_in=m_w_mlp_in, m_w_mlp_out=m_w_mlp_out, m_ln2_g=m_ln2_g, m_ln2_b=m_ln2_b, v_w_in=v_w_in, v_b_gate=v_b_gate, v_w_sb_proj=v_w_sb_proj, v_w_ca_proj=v_w_ca_proj, v_rel_bias=v_rel_bias, v_w_out=v_w_out, v_ln1_g=v_ln1_g, v_ln1_b=v_ln1_b, v_w_mlp_in=v_w_mlp_in, v_w_mlp_out=v_w_mlp_out, v_ln2_g=v_ln2_g, v_ln2_b=v_ln2_b)
    weights = {n: given[n] for n in TWIN_WEIGHTS}
    shared = {n: given[n] for n in SHARED_INPUTS}
    per_example = {n: given[n] for n in ['x']}
    grad_fn = _jax.value_and_grad(_loss, argnums=(0, 1))

    def one_microbatch(ex, loss_target):
        ex = dict(ex)
        diff = ex.pop(TWIN_DIFF_INPUT)
        return grad_fn(weights, diff, {**shared, **ex}, loss_target)

    if N_MICROBATCH == 1:
        loss, (grad_w, grad_x) = one_microbatch(per_example, given["loss_target"])
    else:
        def body(carry, xs):
            loss_sum, grad_sum = carry
            l_k, (gw_k, gx_k) = one_microbatch(xs[0], xs[1])
            with _jax.named_scope("update"):
                return (loss_sum + l_k, _jax.tree.map(_jnp.add, grad_sum, gw_k)), gx_k

        init = (_jnp.zeros((), _jnp.float32), _jax.tree.map(_jnp.zeros_like, weights))
        (loss, grad_w), grad_x = _jax.lax.scan(body, init, (per_example, given["loss_target"]))
    with _jax.named_scope("update"):
        delta_w, new_m, new_v = {}, {}, {}
        for n in TWIN_WEIGHTS:
            delta_w[n], new_m[n], new_v[n] = _adamw(weights[n], grad_w[n], given["m_" + n], given["v_" + n])
    return (loss, grad_x, *[grad_w[n] for n in TWIN_WEIGHTS], *[delta_w[n] for n in TWIN_WEIGHTS],
            *[new_m[n] for n in TWIN_WEIGHTS], *[new_v[n] for n in TWIN_WEIGHTS])
```

```python
import functools

import jax
import jax.numpy as jnp
from jax import lax
from jax.experimental import pallas as pl
from jax.experimental.pallas import tpu as pltpu

F32 = jnp.float32
BF16 = jnp.bfloat16
MESH = pl.DeviceIdType.MESH

N_DEV = 8
D_MODEL = 1024
HEAD_DIM = 64
ATT_WIDTH = 512
N_PAIRS = ATT_WIDTH // 128
QKV_COLS = 6 * ATT_WIDTH
GATE_COLS = 2 * D_MODEL
IN_COLS = QKV_COLS + GATE_COLS
IN_SHARD = IN_COLS // N_DEV
D_FF = 4 * D_MODEL
FF_SHARD = D_FF // N_DEV
PROJ_SHARD = D_MODEL // N_DEV
ATT_BLOCK = 128
CA_TILES = 5
CHUNK = 64
CA_PREV_CHUNKS = 8
REL_CLIP = 256
REL_PAD = 640
ALPHA = 2.0 ** 0.25
LN_EPS = 1e-5
QK_SCALE = HEAD_DIM ** -0.5
NEG_BIG = -1e30
VMEM_LIMIT = 56 * 1024 * 1024

ADAM_LR = 0.001
ADAM_B1 = 0.9
ADAM_B2 = 0.999
ADAM_EPS = 1e-08
ADAM_WD = 0.01
ADAM_STEP = 10

_NT = (((1,), (1,)), ((), ()))
_TN = (((0,), (0,)), ((), ()))


def _dot(a, b):
    return jnp.dot(a, b, preferred_element_type=F32)


def _dot_nt(a, b):
    return lax.dot_general(a, b, _NT, preferred_element_type=F32)


def _dot_tn(a, b):
    return lax.dot_general(a, b, _TN, preferred_element_type=F32)


def _params(semantics=None):
    return pltpu.CompilerParams(dimension_semantics=semantics, vmem_limit_bytes=VMEM_LIMIT)


def _row(v):
    return v.reshape(1, -1)


def in_proj(x, wg_in, tm=512):
    s = x.shape[0]

    def body(x_ref, w_ref, qkv_ref, lg_ref, xb_ref):
        xb = x_ref[...].astype(BF16)
        xb_ref[...] = xb
        for j in range(N_DEV):
            acc = _dot(xb, w_ref[j])
            lo, hi = IN_SHARD * j, IN_SHARD * (j + 1)
            if hi <= QKV_COLS:
                qkv_ref[:, lo:hi] = acc.astype(BF16)
            elif lo >= QKV_COLS:
                lg_ref[:, lo - QKV_COLS:hi - QKV_COLS] = acc
            else:
                qkv_ref[:, lo:QKV_COLS] = acc[:, :QKV_COLS - lo].astype(BF16)
                lg_ref[:, 0:hi - QKV_COLS] = acc[:, QKV_COLS - lo:]

    return pl.pallas_call(
        body, name="in_proj", grid=(s // tm,),
        in_specs=[pl.BlockSpec((tm, D_MODEL), lambda i: (i, 0)),
                  pl.BlockSpec((N_DEV, D_MODEL, IN_SHARD), lambda i: (0, 0, 0))],
        out_specs=[pl.BlockSpec((tm, QKV_COLS), lambda i: (i, 0)),
                   pl.BlockSpec((tm, GATE_COLS), lambda i: (i, 0)),
                   pl.BlockSpec((tm, D_MODEL), lambda i: (i, 0))],
        out_shape=[jax.ShapeDtypeStruct((s, QKV_COLS), BF16),
                   jax.ShapeDtypeStruct((s, GATE_COLS), F32),
                   jax.ShapeDtypeStruct((s, D_MODEL), BF16)],
        compiler_params=_params(("arbitrary",)),
    )(x, wg_in)


def _head_masks():
    lane = lax.broadcasted_iota(jnp.int32, (1, 128), 1)
    first = lane < HEAD_DIM
    return first, jnp.logical_not(first)


def _suffix_sum(v, tri):
    hi = v.astype(BF16)
    lo = (v - hi.astype(F32)).astype(BF16)
    return _dot(hi, tri) + _dot(lo, tri)


def _sb_tile(qh, kblk, valid, tri, run):
    z = _dot_nt(qh, kblk)
    log_keep = -(jnp.maximum(z, 0.0) + jnp.log1p(jnp.exp(-jnp.abs(z))))
    log_keep = jnp.where(valid, log_keep, 0.0)
    csum = _suffix_sum(log_keep, tri)
    a = jnp.where(valid, jnp.exp(z + csum + run), 0.0)
    return z, csum, a


def sb_fwd(qkv):
    s = qkv.shape[0]
    nq = s // ATT_BLOCK

    def body(q_ref, k_ref, v_ref, o_ref):
        i = pl.program_id(1)
        first, second = _head_masks()
        q = q_ref[...] * QK_SCALE
        zero = jnp.zeros_like(q)
        qh = (jnp.where(first, q, zero), jnp.where(second, q, zero))
        row = lax.broadcasted_iota(jnp.int32, (ATT_BLOCK, ATT_BLOCK), 0)
        col = lax.broadcasted_iota(jnp.int32, (ATT_BLOCK, ATT_BLOCK), 1)
        tri = (row >= col).astype(BF16)

        def step(it, carry):
            run0, run1, acc = carry
            kb = i - it
            start = pl.multiple_of(kb * ATT_BLOCK, ATT_BLOCK)
            kblk = k_ref[pl.ds(start, ATT_BLOCK), :]
            vblk = v_ref[pl.ds(start, ATT_BLOCK), :]
            valid = jnp.logical_or(kb < i, col < row)
            runs = []
            for h, (run, mask) in enumerate(((run0, first), (run1, second))):
                _, csum, a = _sb_tile(qh[h], kblk, valid, tri, run)
                acc = acc + _dot(a.astype(BF16), jnp.where(mask, vblk, jnp.zeros_like(vblk)))
                runs.append(run + csum[:, 0:1])
            return runs[0], runs[1], acc

        init = (jnp.zeros((ATT_BLOCK, 1), F32), jnp.zeros((ATT_BLOCK, 1), F32),
                jnp.zeros((ATT_BLOCK, 128), F32))
        _, _, acc = lax.fori_loop(0, i + 1, step, init)
        o_ref[...] = acc

    return pl.pallas_call(
        body, name="sb_fwd", grid=(N_PAIRS, nq),
        in_specs=[pl.BlockSpec((ATT_BLOCK, 128), lambda p, i: (i, p)),
                  pl.BlockSpec((s, 128), lambda p, i: (0, N_PAIRS + p)),
                  pl.BlockSpec((s, 128), lambda p, i: (0, 2 * N_PAIRS + p))],
        out_specs=pl.BlockSpec((ATT_BLOCK, 128), lambda p, i: (i, p)),
        out_shape=jax.ShapeDtypeStruct((s, ATT_WIDTH), F32),
        compiler_params=_params(("arbitrary", "arbitrary")),
    )(qkv, qkv, qkv)


def sb_bwd(qkv, o_sb, do_sb):
    s = qkv.shape[0]
    nq = s // ATT_BLOCK

    def body(q_ref, k_ref, v_ref, o_ref, do_ref, dq_ref, dk_ref, dv_ref):
        i = pl.program_id(1)

        @pl.when(i == 0)
        def _():
            dk_ref[...] = jnp.zeros_like(dk_ref)
            dv_ref[...] = jnp.zeros_like(dv_ref)

        first, second = _head_masks()
        masks = (first, second)
        q = q_ref[...] * QK_SCALE
        zero = jnp.zeros_like(q)
        qh = (jnp.where(first, q, zero), jnp.where(second, q, zero))
        do = do_ref[...]
        doh = (jnp.where(first, do, zero), jnp.where(second, do, zero))
        prod = do.astype(F32) * o_ref[...]
        e_tot = tuple(jnp.sum(jnp.where(m, prod, 0.0), axis=1, keepdims=True) for m in masks)
        row = lax.broadcasted_iota(jnp.int32, (ATT_BLOCK, ATT_BLOCK), 0)
        col = lax.broadcasted_iota(jnp.int32, (ATT_BLOCK, ATT_BLOCK), 1)
        tri = (row >= col).astype(BF16)

        def step(it, carry):
            run_l, run_e, dq = carry
            kb = i - it
            start = pl.multiple_of(kb * ATT_BLOCK, ATT_BLOCK)
            kblk = k_ref[pl.ds(start, ATT_BLOCK), :]
            vblk = v_ref[pl.ds(start, ATT_BLOCK), :]
            valid = jnp.logical_or(kb < i, col < row)
            dk_acc = jnp.zeros((ATT_BLOCK, 128), F32)
            dv_acc = jnp.zeros((ATT_BLOCK, 128), F32)
            new_l, new_e = [], []
            for h in range(2):
                z, csum, a = _sb_tile(qh[h], kblk, valid, tri, run_l[h])
                da = _dot_nt(doh[h], vblk)
                ab = a.astype(BF16)
                e = ab.astype(F32) * da
                esum = _suffix_sum(e, tri)
                prefix = e_tot[h] - (esum - e + run_e[h])
                dz = jnp.where(valid, e - jax.nn.sigmoid(z) * prefix, 0.0).astype(BF16)
                kh = jnp.where(masks[h], kblk, jnp.zeros_like(kblk))
                dq = dq + _dot(dz, kh)
                dk_acc = dk_acc + _dot_tn(dz, qh[h])
                dv_acc = dv_acc + _dot_tn(ab, doh[h])
                new_l.append(run_l[h] + csum[:, 0:1])
                new_e.append(run_e[h] + esum[:, 0:1])
            dk_ref[pl.ds(start, ATT_BLOCK), :] += dk_acc
            dv_ref[pl.ds(start, ATT_BLOCK), :] += dv_acc
            return tuple(new_l), tuple(new_e), dq

        col0 = jnp.zeros((ATT_BLOCK, 1), F32)
        init = ((col0, col0), (col0, col0), jnp.zeros((ATT_BLOCK, 128), F32))
        _, _, dq = lax.fori_loop(0, i + 1, step, init)
        dq_ref[...] = (dq * QK_SCALE).astype(BF16)

    tile = lambda p, i: (i, p)
    return pl.pallas_call(
        body, name="sb_bwd", grid=(N_PAIRS, nq),
        in_specs=[pl.BlockSpec((ATT_BLOCK, 128), tile),
                  pl.BlockSpec((s, 128), lambda p, i: (0, N_PAIRS + p)),
                  pl.BlockSpec((s, 128), lambda p, i: (0, 2 * N_PAIRS + p)),
                  pl.BlockSpec((ATT_BLOCK, 128), tile),
                  pl.BlockSpec((ATT_BLOCK, 128), tile)],
        out_specs=[pl.BlockSpec((ATT_BLOCK, 128), tile),
                   pl.BlockSpec((s, 128), lambda p, i: (0, p)),
                   pl.BlockSpec((s, 128), lambda p, i: (0, p))],
        out_shape=[jax.ShapeDtypeStruct((s, ATT_WIDTH), BF16),
                   jax.ShapeDtypeStruct((s, ATT_WIDTH), F32),
                   jax.ShapeDtypeStruct((s, ATT_WIDTH), F32)],
        compiler_params=_params(("arbitrary", "arbitrary")),
    )(qkv, qkv, qkv, o_sb, do_sb)


def ca_bias_tiles(rel_bias):
    r = jnp.arange(ATT_BLOCK)[:, None]
    c = jnp.arange(ATT_BLOCK)[None, :]
    tiles = []
    for j in range(CA_TILES):
        idx = jnp.clip(ATT_BLOCK * j + r - c, -REL_CLIP, REL_CLIP) + REL_CLIP
        back = 2 * j + r // CHUNK - c // CHUNK
        ok = jnp.logical_and(back >= 0, back <= CA_PREV_CHUNKS)
        tiles.append(jnp.where(ok[None], rel_bias[:, idx], NEG_BIG))
    return jnp.stack(tiles, axis=1)


def _ca_probs(qh, k_ref, bias_ref, h, i):
    scores, kblks = [], []
    for j in range(CA_TILES):
        kb = i - j
        start = pl.multiple_of(jnp.maximum(kb, 0) * ATT_BLOCK, ATT_BLOCK)
        kblk = k_ref[pl.ds(start, ATT_BLOCK), :]
        z = _dot_nt(qh, kblk) + bias_ref[h, j]
        scores.append(jnp.where(kb >= 0, z, NEG_BIG))
        kblks.append(kblk)
    m = functools.reduce(jnp.maximum, [jnp.max(z, axis=1, keepdims=True) for z in scores])
    ex = [jnp.exp(z - m) for z in scores]
    inv = 1.0 / functools.reduce(jnp.add, [jnp.sum(e, axis=1, keepdims=True) for e in ex])
    return [e * inv for e in ex], kblks


def ca_fwd(qkv, bias_tiles):
    s = qkv.shape[0]
    nq = s // ATT_BLOCK

    def body(q_ref, k_ref, v_ref, bias_ref, o_ref):
        i = pl.program_id(1)
        masks = _head_masks()
        q = q_ref[...] * QK_SCALE
        zero = jnp.zeros_like(q)
        acc = jnp.zeros((ATT_BLOCK, 128), F32)
        for h in range(2):
            probs, _ = _ca_probs(jnp.where(masks[h], q, zero), k_ref, bias_ref, h, i)
            for j in range(CA_TILES):
                start = pl.multiple_of(jnp.maximum(i - j, 0) * ATT_BLOCK, ATT_BLOCK)
                vblk = v_ref[pl.ds(start, ATT_BLOCK), :]
                acc = acc + _dot(probs[j].astype(BF16), jnp.where(masks[h], vblk, jnp.zeros_like(vblk)))
        o_ref[...] = acc

    return pl.pallas_call(
        body, name="ca_fwd", grid=(N_PAIRS, nq),
        in_specs=[pl.BlockSpec((ATT_BLOCK, 128), lambda p, i: (i, 3 * N_PAIRS + p)),
                  pl.BlockSpec((s, 128), lambda p, i: (0, 4 * N_PAIRS + p)),
                  pl.BlockSpec((s, 128), lambda p, i: (0, 5 * N_PAIRS + p)),
                  pl.BlockSpec((2, CA_TILES, ATT_BLOCK, ATT_BLOCK), lambda p, i: (p, 0, 0, 0))],
        out_specs=pl.BlockSpec((ATT_BLOCK, 128), lambda p, i: (i, p)),
        out_shape=jax.ShapeDtypeStruct((s, ATT_WIDTH), F32),
        compiler_params=_params(("arbitrary", "arbitrary")),
    )(qkv, qkv, qkv, bias_tiles)


def ca_bwd(qkv, bias_tiles, do_ca):
    s = qkv.shape[0]
    nq = s // ATT_BLOCK

    def body(q_ref, k_ref, v_ref, bias_ref, do_ref, dq_ref, dk_ref, dv_ref, db_ref):
        i = pl.program_id(1)

        @pl.when(i == 0)
        def _():
            dk_ref[...] = jnp.zeros_like(dk_ref)
            dv_ref[...] = jnp.zeros_like(dv_ref)
            db_ref[...] = jnp.zeros_like(db_ref)

        masks = _head_masks()
        q = q_ref[...] * QK_SCALE
        zero = jnp.zeros_like(q)
        do = do_ref[...]
        dq = jnp.zeros((ATT_BLOCK, 128), F32)
        for h in range(2):
            qh = jnp.where(masks[h], q, zero)
            doh = jnp.where(masks[h], do, zero)
            probs, kblks = _ca_probs(qh, k_ref, bias_ref, h, i)
            starts = [pl.multiple_of(jnp.maximum(i - j, 0) * ATT_BLOCK, ATT_BLOCK) for j in range(CA_TILES)]
            dps = [_dot_nt(doh, v_ref[pl.ds(starts[j], ATT_BLOCK), :]) for j in range(CA_TILES)]
            delta = functools.reduce(
                jnp.add, [jnp.sum(probs[j] * dps[j], axis=1, keepdims=True) for j in range(CA_TILES)])
            for j in range(CA_TILES):
                ds = probs[j] * (dps[j] - delta)
                db_ref[h, j] += ds
                dsb = ds.astype(BF16)
                dq = dq + _dot(dsb, jnp.where(masks[h], kblks[j], jnp.zeros_like(kblks[j])))
                dk_ref[pl.ds(starts[j], ATT_BLOCK), :] += _dot_tn(dsb, qh)
                dv_ref[pl.ds(starts[j], ATT_BLOCK), :] += _dot_tn(probs[j].astype(BF16), doh)
        dq_ref[...] = (dq * QK_SCALE).astype(BF16)

    return pl.pallas_call(
        body, name="ca_bwd", grid=(N_PAIRS, nq),
        in_specs=[pl.BlockSpec((ATT_BLOCK, 128), lambda p, i: (i, 3 * N_PAIRS + p)),
                  pl.BlockSpec((s, 128), lambda p, i: (0, 4 * N_PAIRS + p)),
                  pl.BlockSpec((s, 128), lambda p, i: (0, 5 * N_PAIRS + p)),
                  pl.BlockSpec((2, CA_TILES, ATT_BLOCK, ATT_BLOCK), lambda p, i: (p, 0, 0, 0)),
                  pl.BlockSpec((ATT_BLOCK, 128), lambda p, i: (i, p))],
        out_specs=[pl.BlockSpec((ATT_BLOCK, 128), lambda p, i: (i, p)),
                   pl.BlockSpec((s, 128), lambda p, i: (0, p)),
                   pl.BlockSpec((s, 128), lambda p, i: (0, p)),
                   pl.BlockSpec((2, CA_TILES, ATT_BLOCK, ATT_BLOCK), lambda p, i: (p, 0, 0, 0))],
        out_shape=[jax.ShapeDtypeStruct((s, ATT_WIDTH), BF16),
                   jax.ShapeDtypeStruct((s, ATT_WIDTH), F32),
                   jax.ShapeDtypeStruct((s, ATT_WIDTH), F32),
                   jax.ShapeDtypeStruct((2 * N_PAIRS, CA_TILES, ATT_BLOCK, ATT_BLOCK), F32)],
        compiler_params=_params(("arbitrary", "arbitrary")),
    )(qkv, qkv, qkv, bias_tiles, do_ca)


def rel_bias_grad(db_tiles):
    heads = db_tiles.shape[0]

    def body(db_ref, out_ref):
        row = lax.broadcasted_iota(jnp.int32, (ATT_BLOCK, ATT_BLOCK), 0)
        col = lax.broadcasted_iota(jnp.int32, (ATT_BLOCK, ATT_BLOCK), 1)
        cp = lax.broadcasted_iota(jnp.int32, (ATT_BLOCK, REL_PAD), 0)
        bins = lax.broadcasted_iota(jnp.int32, (ATT_BLOCK, REL_PAD), 1)
        total = jnp.zeros((heads, REL_PAD), F32)
        for j in range(CA_TILES):
            sums_lo, sums_up = [], []
            for h in range(heads):
                tile = db_ref[h, j]
                for bit in range(7):
                    rolled = pltpu.roll(tile, ATT_BLOCK - (1 << bit), 1)
                    tile = jnp.where((row >> bit) & 1 == 1, rolled, tile)
                was_lower = (col + row >= ATT_BLOCK) | (col == 0)
                sums_lo.append(jnp.sum(jnp.where(was_lower, tile, 0.0), axis=0, keepdims=True))
                sums_up.append(jnp.sum(jnp.where(was_lower, 0.0, tile), axis=0, keepdims=True))
            lo = jnp.concatenate(sums_lo, axis=0)
            up = jnp.concatenate(sums_up, axis=0)
            d_lo = jnp.where(cp == 0, 0, ATT_BLOCK - cp)
            hot_lo = (jnp.clip(ATT_BLOCK * j + d_lo, -REL_CLIP, REL_CLIP) + REL_CLIP == bins).astype(F32)
            hot_up = (jnp.clip(ATT_BLOCK * j - cp, -REL_CLIP, REL_CLIP) + REL_CLIP == bins).astype(F32)
            total = total + jnp.dot(lo, hot_lo, preferred_element_type=F32, precision=lax.Precision.HIGHEST)
            total = total + jnp.dot(up, hot_up, preferred_element_type=F32, precision=lax.Precision.HIGHEST)
        out_ref[...] = total

    return pl.pallas_call(
        body, name="rel_bias_grad",
        out_shape=jax.ShapeDtypeStruct((heads, REL_PAD), F32),
        compiler_params=_params(),
    )(db_tiles)


def _assemble_cols(dst_ref, src_ref):
    w = src_ref.shape[2]
    for j in range(N_DEV):
        dst_ref[:, w * j:w * (j + 1)] = src_ref[j]


def _merge(o_sb, o_ca, lg, bg, wsb, wca):
    y_sb = _dot(o_sb.astype(BF16), wsb)
    y_ca = _dot(o_ca.astype(BF16), wca)
    gates = jax.nn.sigmoid(lg + bg)
    g_sb, g_ca = gates[:, :D_MODEL], gates[:, D_MODEL:]
    return y_sb, y_ca, g_sb, g_ca, g_sb * y_sb + g_ca * y_ca


def mix_fwd(x, o_sb, o_ca, lg, b_gate, wg_sb, wg_ca, w_out, tm=256):
    s = x.shape[0]

    def body(x_ref, osb_ref, oca_ref, lg_ref, bg_ref, wsb_ref, wca_ref, wout_ref, u_ref, wsb_s, wca_s):
        @pl.when(pl.program_id(0) == 0)
        def _():
            _assemble_cols(wsb_s, wsb_ref)
            _assemble_cols(wca_s, wca_ref)

        merged = _merge(osb_ref[...], oca_ref[...], lg_ref[...], bg_ref[...], wsb_s[...], wca_s[...])[4]
        u_ref[...] = ALPHA * x_ref[...] + _dot(merged.astype(BF16), wout_ref[...])

    rows = lambda w: pl.BlockSpec((tm, w), lambda i: (i, 0))
    whole = lambda shape: pl.BlockSpec(shape, lambda i: (0,) * len(shape))
    return pl.pallas_call(
        body, name="mix_fwd", grid=(s // tm,),
        in_specs=[rows(D_MODEL), rows(ATT_WIDTH), rows(ATT_WIDTH), rows(GATE_COLS), whole((1, GATE_COLS)),
                  whole((N_DEV, ATT_WIDTH, PROJ_SHARD)), whole((N_DEV, ATT_WIDTH, PROJ_SHARD)),
                  whole((D_MODEL, D_MODEL))],
        out_specs=rows(D_MODEL),
        out_shape=jax.ShapeDtypeStruct((s, D_MODEL), F32),
        scratch_shapes=[pltpu.VMEM((ATT_WIDTH, D_MODEL), BF16), pltpu.VMEM((ATT_WIDTH, D_MODEL), BF16)],
        compiler_params=_params(("arbitrary",)),
    )(x, o_sb, o_ca, lg, _row(b_gate), wg_sb, wg_ca, w_out)


def mix_bwd(du1, o_sb, o_ca, lg, b_gate, wg_sb, wg_ca, w_out, tm=256):
    s = du1.shape[0]

    def body(du_ref, osb_ref, oca_ref, lg_ref, bg_ref, wsb_ref, wca_ref, wout_ref,
             dlg_ref, dosb_ref, doca_ref, dysb_ref, dyca_ref, mg_ref, dbg_ref, wsb_s, wca_s):
        @pl.when(pl.program_id(0) == 0)
        def _():
            _assemble_cols(wsb_s, wsb_ref)
            _assemble_cols(wca_s, wca_ref)
            dbg_ref[...] = jnp.zeros_like(dbg_ref)

        y_sb, y_ca, g_sb, g_ca, merged = _merge(
            osb_ref[...], oca_ref[...], lg_ref[...], bg_ref[...], wsb_s[...], wca_s[...])
        mg_ref[...] = merged.astype(BF16)
        dm = _dot_nt(du_ref[...].astype(BF16), wout_ref[...])
        dl_sb = dm * y_sb * g_sb * (1.0 - g_sb)
        dl_ca = dm * y_ca * g_ca * (1.0 - g_ca)
        dlg_ref[:, :D_MODEL] = dl_sb.astype(BF16)
        dlg_ref[:, D_MODEL:] = dl_ca.astype(BF16)
        dbg_ref[:, :D_MODEL] += jnp.sum(dl_sb, axis=0, keepdims=True)
        dbg_ref[:, D_MODEL:] += jnp.sum(dl_ca, axis=0, keepdims=True)
        dy_sb = (dm * g_sb).astype(BF16)
        dy_ca = (dm * g_ca).astype(BF16)
        dysb_ref[...] = dy_sb
        dyca_ref[...] = dy_ca
        dosb_ref[...] = _dot_nt(dy_sb, wsb_s[...]).astype(BF16)
        doca_ref[...] = _dot_nt(dy_ca, wca_s[...]).astype(BF16)

    rows = lambda w: pl.BlockSpec((tm, w), lambda i: (i, 0))
    whole = lambda shape: pl.BlockSpec(shape, lambda i: (0,) * len(shape))
    return pl.pallas_call(
        body, name="mix_bwd", grid=(s // tm,),
        in_specs=[rows(D_MODEL), rows(ATT_WIDTH), rows(ATT_WIDTH), rows(GATE_COLS), whole((1, GATE_COLS)),
                  whole((N_DEV, ATT_WIDTH, PROJ_SHARD)), whole((N_DEV, ATT_WIDTH, PROJ_SHARD)),
                  whole((D_MODEL, D_MODEL))],
        out_specs=[rows(GATE_COLS), rows(ATT_WIDTH), rows(ATT_WIDTH), rows(D_MODEL), rows(D_MODEL),
                   rows(D_MODEL), whole((1, GATE_COLS))],
        out_shape=[jax.ShapeDtypeStruct((s, GATE_COLS), BF16),
                   jax.ShapeDtypeStruct((s, ATT_WIDTH), BF16), jax.ShapeDtypeStruct((s, ATT_WIDTH), BF16),
                   jax.ShapeDtypeStruct((s, D_MODEL), BF16), jax.ShapeDtypeStruct((s, D_MODEL), BF16),
                   jax.ShapeDtypeStruct((s, D_MODEL), BF16), jax.ShapeDtypeStruct((1, GATE_COLS), F32)],
        scratch_shapes=[pltpu.VMEM((ATT_WIDTH, D_MODEL), BF16), pltpu.VMEM((ATT_WIDTH, D_MODEL), BF16)],
        compiler_params=_params(("arbitrary",)),
    )(du1, o_sb, o_ca, lg, _row(b_gate), wg_sb, wg_ca, w_out)


def _ln_stats(u):
    mu = jnp.mean(u, axis=1, keepdims=True)
    cen = u - mu
    var = jnp.mean(cen * cen, axis=1, keepdims=True)
    rstd = lax.rsqrt(var + LN_EPS)
    return cen * rstd, rstd


def _ln_bwd(dy, xhat, rstd, gain):
    dyg = dy * gain
    m1 = jnp.mean(dyg, axis=1, keepdims=True)
    m2 = jnp.mean(dyg * xhat, axis=1, keepdims=True)
    return rstd * (dyg - m1 - xhat * m2)


def mlp_fwd(u1, target, wg_in, wg_out, ln1_g, ln1_b, ln2_g, ln2_b, tm=512):
    s = u1.shape[0]
    nk = N_DEV

    def body(u_ref, t_ref, win_ref, wout_ref, g1_ref, b1_ref, g2_ref, b2_ref,
             hm_ref, a_ref, x1b_ref, du2_ref, loss_ref, dg2_ref, db2_ref, x1_s, acc_s):
        i, k = pl.program_id(0), pl.program_id(1)

        @pl.when(jnp.logical_and(i == 0, k == 0))
        def _():
            loss_ref[...] = jnp.zeros_like(loss_ref)
            dg2_ref[...] = jnp.zeros_like(dg2_ref)
            db2_ref[...] = jnp.zeros_like(db2_ref)

        @pl.when(k == 0)
        def _():
            xhat, _ = _ln_stats(u_ref[...])
            x1 = xhat * g1_ref[...] + b1_ref[...]
            x1_s[...] = x1
            x1b_ref[...] = x1.astype(BF16)
            acc_s[...] = jnp.zeros_like(acc_s)

        hm = _dot(x1b_ref[...], win_ref[0])
        hm_ref[...] = hm
        r = jnp.maximum(hm, 0.0)
        a = (r * r).astype(BF16)
        a_ref[...] = a
        acc_s[...] += _dot(a, wout_ref[0])

        @pl.when(k == nk - 1)
        def _():
            xhat, rstd = _ln_stats(ALPHA * x1_s[...] + acc_s[...])
            diff = xhat * g2_ref[...] + b2_ref[...] - t_ref[...]
            per_token = jnp.mean(diff * diff, axis=1, keepdims=True)
            loss_ref[...] += 0.5 * jnp.sum(per_token, axis=0, keepdims=True)
            dy = diff * (1.0 / D_MODEL)
            dg2_ref[...] += jnp.sum(dy * xhat, axis=0, keepdims=True)
            db2_ref[...] += jnp.sum(dy, axis=0, keepdims=True)
            du2_ref[...] = _ln_bwd(dy, xhat, rstd, g2_ref[...])

    rows = pl.BlockSpec((tm, D_MODEL), lambda i, k: (i, 0))
    vec = pl.BlockSpec((1, D_MODEL), lambda i, k: (0, 0))
    ff = pl.BlockSpec((tm, FF_SHARD), lambda i, k: (i, k))
    return pl.pallas_call(
        body, name="mlp_fwd", grid=(s // tm, nk),
        in_specs=[rows, rows,
                  pl.BlockSpec((1, D_MODEL, FF_SHARD), lambda i, k: (k, 0, 0)),
                  pl.BlockSpec((1, FF_SHARD, D_MODEL), lambda i, k: (k, 0, 0)),
                  vec, vec, vec, vec],
        out_specs=[ff, ff, rows, rows, pl.BlockSpec((1, 1), lambda i, k: (0, 0)), vec, vec],
        out_shape=[jax.ShapeDtypeStruct((s, D_FF), F32), jax.ShapeDtypeStruct((s, D_FF), BF16),
                   jax.ShapeDtypeStruct((s, D_MODEL), BF16), jax.ShapeDtypeStruct((s, D_MODEL), F32),
                   jax.ShapeDtypeStruct((1, 1), F32),
                   jax.ShapeDtypeStruct((1, D_MODEL), F32), jax.ShapeDtypeStruct((1, D_MODEL), F32)],
        scratch_shapes=[pltpu.VMEM((tm, D_MODEL), F32), pltpu.VMEM((tm, D_MODEL), F32)],
        compiler_params=_params(("arbitrary", "arbitrary")),
    )(u1, target, wg_in, wg_out, _row(ln1_g), _row(ln1_b), _row(ln2_g), _row(ln2_b))


def mlp_bwd(du2, hm, u1, wg_in, wg_out, ln1_g, tm=512):
    s = du2.shape[0]
    nk = N_DEV

    def body(du2_ref, hm_ref, u_ref, win_ref, wout_ref, g1_ref,
             dhm_ref, du1_ref, dg1_ref, db1_ref, du2b_s, acc_s):
        i, k = pl.program_id(0), pl.program_id(1)

        @pl.when(jnp.logical_and(i == 0, k == 0))
        def _():
            dg1_ref[...] = jnp.zeros_like(dg1_ref)
            db1_ref[...] = jnp.zeros_like(db1_ref)

        @pl.when(k == 0)
        def _():
            du2b_s[...] = du2_ref[...].astype(BF16)
            acc_s[...] = jnp.zeros_like(acc_s)

        da = _dot_nt(du2b_s[...], wout_ref[0])
        dhm = (2.0 * jnp.maximum(hm_ref[...], 0.0) * da).astype(BF16)
        dhm_ref[...] = dhm
        acc_s[...] += _dot_nt(dhm, win_ref[0])

        @pl.when(k == nk - 1)
        def _():
            dx1 = ALPHA * du2_ref[...] + acc_s[...]
            xhat, rstd = _ln_stats(u_ref[...])
            dg1_ref[...] += jnp.sum(dx1 * xhat, axis=0, keepdims=True)
            db1_ref[...] += jnp.sum(dx1, axis=0, keepdims=True)
            du1_ref[...] = _ln_bwd(dx1, xhat, rstd, g1_ref[...])

    rows = pl.BlockSpec((tm, D_MODEL), lambda i, k: (i, 0))
    vec = pl.BlockSpec((1, D_MODEL), lambda i, k: (0, 0))
    ff = pl.BlockSpec((tm, FF_SHARD), lambda i, k: (i, k))
    return pl.pallas_call(
        body, name="mlp_bwd", grid=(s // tm, nk),
        in_specs=[rows, ff, rows,
                  pl.BlockSpec((1, D_MODEL, FF_SHARD), lambda i, k: (k, 0, 0)),
                  pl.BlockSpec((1, FF_SHARD, D_MODEL), lambda i, k: (k, 0, 0)),
                  vec],
        out_specs=[ff, rows, vec, vec],
        out_shape=[jax.ShapeDtypeStruct((s, D_FF), BF16), jax.ShapeDtypeStruct((s, D_MODEL), F32),
                   jax.ShapeDtypeStruct((1, D_MODEL), F32), jax.ShapeDtypeStruct((1, D_MODEL), F32)],
        scratch_shapes=[pltpu.VMEM((tm, D_MODEL), BF16), pltpu.VMEM((tm, D_MODEL), F32)],
        compiler_params=_params(("arbitrary", "arbitrary")),
    )(du2, hm, u1, wg_in, wg_out, _row(ln1_g))


def in_bwd(du1, pieces, wg_in, tm=512):
    s = du1.shape[0]
    widths = [p.shape[1] for p in pieces]
    assert sum(widths) == IN_COLS
    n = len(pieces)

    def body(*refs):
        du_ref, piece_refs, w_ref = refs[0], refs[1:1 + n], refs[1 + n]
        gx_ref, dh_ref = refs[2 + n], refs[3 + n]
        off = 0
        for ref, w in zip(piece_refs, widths):
            dh_ref[:, off:off + w] = ref[...].astype(BF16)
            off += w
        acc = ALPHA * du_ref[...]
        for j in range(N_DEV):
            acc = acc + _dot_nt(dh_ref[:, IN_SHARD * j:IN_SHARD * (j + 1)], w_ref[j])
        gx_ref[...] = acc

    rows = lambda w: pl.BlockSpec((tm, w), lambda i: (i, 0))
    return pl.pallas_call(
        body, name="in_bwd", grid=(s // tm,),
        in_specs=[rows(D_MODEL)] + [rows(w) for w in widths]
                 + [pl.BlockSpec((N_DEV, D_MODEL, IN_SHARD), lambda i: (0, 0, 0))],
        out_specs=[rows(D_MODEL), rows(IN_COLS)],
        out_shape=[jax.ShapeDtypeStruct((s, D_MODEL), F32), jax.ShapeDtypeStruct((s, IN_COLS), BF16)],
        compiler_params=_params(("arbitrary",)),
    )(du1, *pieces, wg_in)


def wgrad(name, a, b, tm, tn, shard_cols=None, tk=512):
    kk, m = a.shape
    n = b.shape[1]
    nk = kk // tk
    w = shard_cols

    def body(a_ref, b_ref, out_ref, acc_s):
        k = pl.program_id(2)

        @pl.when(k == 0)
        def _():
            acc_s[...] = jnp.zeros_like(acc_s)

        acc_s[...] += _dot_tn(a_ref[...].astype(BF16), b_ref[...].astype(BF16))

        @pl.when(k == nk - 1)
        def _():
            if w is None:
                out_ref[...] = acc_s[...]
            else:
                for j in range(tn // w):
                    out_ref[j] = acc_s[:, w * j:w * (j + 1)]

    if w is None:
        out_spec = pl.BlockSpec((tm, tn), lambda i, j, k: (i, j))
        out_shape = jax.ShapeDtypeStruct((m, n), F32)
    else:
        out_spec = pl.BlockSpec((tn // w, tm, w), lambda i, j, k: (j, i, 0))
        out_shape = jax.ShapeDtypeStruct((n // w, m, w), F32)
    return pl.pallas_call(
        body, name=name, grid=(m // tm, n // tn, nk),
        in_specs=[pl.BlockSpec((tk, tm), lambda i, j, k: (k, i)),
                  pl.BlockSpec((tk, tn), lambda i, j, k: (k, j))],
        out_specs=out_spec, out_shape=out_shape,
        scratch_shapes=[pltpu.VMEM((tm, tn), F32)],
        compiler_params=_params(("arbitrary", "arbitrary", "arbitrary")),
    )(a, b)


def adamw(name, w, m, v, parts, tm=None):
    rows, cols = w.shape
    tm = rows if tm is None else min(tm, rows)
    n = len(parts)
    c1 = 1.0 - ADAM_B1 ** ADAM_STEP
    c2 = 1.0 - ADAM_B2 ** ADAM_STEP

    def body(*refs):
        w_ref, m_ref, v_ref = refs[:3]
        part_refs = refs[3:3 + n]
        g_ref, d_ref, nm_ref, nv_ref = refs[3 + n:]
        g = None
        for ref, (_, index) in zip(part_refs, parts):
            term = (ref[...] if index is None else ref[0]).astype(F32)
            g = term if g is None else g + term
        new_m = ADAM_B1 * m_ref[...] + (1.0 - ADAM_B1) * g
        new_v = ADAM_B2 * v_ref[...] + (1.0 - ADAM_B2) * (g * g)
        m_hat = new_m / c1
        v_hat = new_v / c2
        g_ref[...] = g
        d_ref[...] = -ADAM_LR * (m_hat / (jnp.sqrt(v_hat) + ADAM_EPS) + ADAM_WD * w_ref[...])
        nm_ref[...] = new_m
        nv_ref[...] = new_v

    spec = pl.BlockSpec((tm, cols), lambda i: (i, 0))
    shape = jax.ShapeDtypeStruct((rows, cols), F32)

    def part_spec(index):
        if index is None:
            return spec
        return pl.BlockSpec((1, tm, cols), lambda i: (index, i, 0))

    return pl.pallas_call(
        body, name=name, grid=(rows // tm,),
        in_specs=[spec] * 3 + [part_spec(index) for _, index in parts],
        out_specs=[spec] * 4, out_shape=[shape] * 4,
        compiler_params=_params(("arbitrary",)),
    )(w, m, v, *[array for array, _ in parts])


def local_grads(x, target, wg_in, wg_sb, wg_ca, w_out, wg_mi, wg_mo, b_gate, rel_bias,
                ln1_g, ln1_b, ln2_g, ln2_b):
    qkv, lg, xb = in_proj(x, wg_in)
    bias_tiles = ca_bias_tiles(rel_bias)
    o_sb = sb_fwd(qkv)
    o_ca = ca_fwd(qkv, bias_tiles)
    u1 = mix_fwd(x, o_sb, o_ca, lg, b_gate, wg_sb, wg_ca, w_out)
    hm, a, x1b, du2, loss, dg2, db2 = mlp_fwd(u1, target, wg_mi, wg_mo, ln1_g, ln1_b, ln2_g, ln2_b)
    dhm, du1, dg1, db1 = mlp_bwd(du2, hm, u1, wg_mi, wg_mo, ln1_g)
    g_mi = wgrad("wgrad_mlp_in", x1b, dhm, tm=D_MODEL, tn=FF_SHARD, shard_cols=FF_SHARD)
    g_mo = wgrad("wgrad_mlp_out", a, du2, tm=FF_SHARD, tn=D_MODEL)
    dlg, do_sb, do_ca, dy_sb, dy_ca, merged, dbg = mix_bwd(du1, o_sb, o_ca, lg, b_gate, wg_sb, wg_ca, w_out)
    g_out = wgrad("wgrad_out", merged, du1, tm=D_MODEL, tn=D_MODEL)
    g_sb = wgrad("wgrad_sb_proj", o_sb, dy_sb, tm=ATT_WIDTH, tn=D_MODEL, shard_cols=PROJ_SHARD)
    g_ca = wgrad("wgrad_ca_proj", o_ca, dy_ca, tm=ATT_WIDTH, tn=D_MODEL, shard_cols=PROJ_SHARD)
    dq_sb, dk_sb, dv_sb = sb_bwd(qkv, o_sb, do_sb)
    dq_ca, dk_ca, dv_ca, db_tiles = ca_bwd(qkv, bias_tiles, do_ca)
    g_rel = rel_bias_grad(db_tiles)
    grad_x, dh = in_bwd(du1, [dq_sb, dk_sb, dv_sb, dq_ca, dk_ca, dv_ca, dlg], wg_in)
    g_in = wgrad("wgrad_in", xb, dh, tm=D_MODEL, tn=IN_SHARD, shard_cols=IN_SHARD)
    big = dict(w_in=g_in, w_sb_proj=g_sb, w_ca_proj=g_ca, w_out=g_out, w_mlp_in=g_mi, w_mlp_out=g_mo)
    small = dict(b_gate=dbg, rel_bias=g_rel, ln1_g=dg1, ln1_b=db1, ln2_g=dg2, ln2_b=db2)
    return loss, grad_x, big, small


HBM_SPEC = pl.BlockSpec(memory_space=pl.ANY)


def _place():
    x, y, c = lax.axis_index("x"), lax.axis_index("y"), lax.axis_index("c")
    other_chips = [(1 - x, y), (x, 1 - y), (1 - x, 1 - y)]
    return x, y, c, other_chips


def all_gather(name, arrays):
    n = len(arrays)

    def body(*refs):
        ins, outs = refs[:n], refs[n:2 * n]
        send_sems, recv_sems, local_sems = refs[2 * n:]
        x, y, c, chips = _place()
        me = 4 * x + 2 * y + c
        sibling = (x, y, 1 - c)

        def copy(a, k, block, to, src=None):
            return pltpu.make_async_remote_copy(
                src_ref=outs[a].at[block] if src is None else src, dst_ref=outs[a].at[block],
                send_sem=send_sems.at[a, k], recv_sem=recv_sems.at[a, k],
                device_id=to, device_id_type=MESH)

        started = []
        local = [pltpu.make_async_copy(ins[a], outs[a].at[me], local_sems.at[a]) for a in range(n)]
        for a in range(n):
            local[a].start()
            sends = [copy(a, 0, me, sibling, src=ins[a])]
            sends += [copy(a, 1 + j, me, (cx, cy, c), src=ins[a]) for j, (cx, cy) in enumerate(chips)]
            for cp in sends:
                cp.start()
            started += sends
        for a in range(n):
            for j, (cx, cy) in enumerate(chips):
                block = 4 * cx + 2 * cy + c
                copy(a, 1 + j, block, sibling).wait_recv()
                passed = copy(a, 4 + j, block, sibling)
                passed.start()
                started.append(passed)
        for a in range(n):
            copy(a, 0, 4 * x + 2 * y + (1 - c), sibling).wait_recv()
            for j, (cx, cy) in enumerate(chips):
                copy(a, 4 + j, 4 * cx + 2 * cy + (1 - c), sibling).wait_recv()
        for cp in started:
            cp.wait_send()
        for cp in local:
            cp.wait()

    return pl.pallas_call(
        body, name=name,
        in_specs=[HBM_SPEC] * n, out_specs=[HBM_SPEC] * n,
        out_shape=[jax.ShapeDtypeStruct((N_DEV,) + a.shape, a.dtype) for a in arrays],
        scratch_shapes=[pltpu.SemaphoreType.DMA((n, 7)), pltpu.SemaphoreType.DMA((n, 7)),
                        pltpu.SemaphoreType.DMA((n,))],
    )(*arrays)


def sibling_exchange(name, grads):
    n = len(grads)

    def body(*refs):
        ins, outs = refs[:n], refs[n:2 * n]
        send_sems, recv_sems = refs[2 * n:]
        x, y, c, _ = _place()
        copies = [pltpu.make_async_remote_copy(
            src_ref=ins[a].at[:, 1 - c], dst_ref=outs[a], send_sem=send_sems.at[a], recv_sem=recv_sems.at[a],
            device_id=(x, y, 1 - c), device_id_type=MESH) for a in range(n)]
        for cp in copies:
            cp.start()
        for cp in copies:
            cp.wait()

    return pl.pallas_call(
        body, name=name,
        in_specs=[HBM_SPEC] * n, out_specs=[HBM_SPEC] * n,
        out_shape=[jax.ShapeDtypeStruct((4,) + g.shape[2:], g.dtype) for g in grads],
        scratch_shapes=[pltpu.SemaphoreType.DMA((n,)), pltpu.SemaphoreType.DMA((n,))],
    )(*grads)


def chip_sum(name, grad, landed, place, tr=256):
    _, _, rows, cols = grad.shape
    tr = min(tr, rows)

    def body(place_ref, g_ref, l_ref, own_ref, out_ref):
        total = g_ref[0, 0] + l_ref[0]
        out_ref[0] = total.astype(BF16)

        @pl.when(pl.program_id(1) == place_ref[1])
        def _():
            own_ref[...] = total

    return pl.pallas_call(
        body, name=name,
        grid_spec=pltpu.PrefetchScalarGridSpec(
            num_scalar_prefetch=1, grid=(rows // tr, 4),
            in_specs=[pl.BlockSpec((1, 1, tr, cols), lambda i, ch, pos: (ch, pos[0], i, 0)),
                      pl.BlockSpec((1, tr, cols), lambda i, ch, pos: (ch, i, 0))],
            out_specs=[pl.BlockSpec((tr, cols), lambda i, ch, pos: (i, 0)),
                       pl.BlockSpec((1, tr, cols), lambda i, ch, pos: (ch, i, 0))]),
        out_shape=[jax.ShapeDtypeStruct((rows, cols), F32), jax.ShapeDtypeStruct((4, rows, cols), BF16)],
        compiler_params=_params(("arbitrary", "arbitrary")),
    )(place, grad, landed)


def chip_exchange(name, sums):
    n = len(sums)

    def body(*refs):
        ins, outs = refs[:n], refs[n:2 * n]
        send_sems, recv_sems = refs[2 * n:]
        _, _, c, chips = _place()
        copies = [pltpu.make_async_remote_copy(
            src_ref=ins[a].at[2 * cx + cy], dst_ref=outs[a].at[j],
            send_sem=send_sems.at[a, j], recv_sem=recv_sems.at[a, j],
            device_id=(cx, cy, c), device_id_type=MESH)
            for a in range(n) for j, (cx, cy) in enumerate(chips)]
        for cp in copies:
            cp.start()
        for cp in copies:
            cp.wait()

    return pl.pallas_call(
        body, name=name,
        in_specs=[HBM_SPEC] * n, out_specs=[HBM_SPEC] * n,
        out_shape=[jax.ShapeDtypeStruct((3,) + s.shape[1:], s.dtype) for s in sums],
        scratch_shapes=[pltpu.SemaphoreType.DMA((n, 3)), pltpu.SemaphoreType.DMA((n, 3))],
    )(*sums)


BIG = ("w_in", "w_sb_proj", "w_ca_proj", "w_out", "w_mlp_in", "w_mlp_out")
SMALL = ("b_gate", "rel_bias", "ln1_g", "ln1_b", "ln2_g", "ln2_b")
NAMES = ("w_in", "b_gate", "w_sb_proj", "w_ca_proj", "rel_bias", "w_out", "ln1_g", "ln1_b",
         "w_mlp_in", "w_mlp_out", "ln2_g", "ln2_b")
REL_COLS = 2 * REL_CLIP + 1


def _pack_small(t):
    rel = t["rel_bias"]
    if rel.shape[1] != REL_PAD:
        rel = jnp.pad(rel, ((0, 0), (0, REL_PAD - rel.shape[1])))
    flat = [t["b_gate"].reshape(-1), rel.reshape(-1)] + [t[n].reshape(-1) for n in SMALL[2:]]
    return jnp.concatenate(flat).reshape(-1, 128)


def _unpack_small(p):
    flat = p.reshape(-1)
    out, off = {}, 0
    for n, size in (("b_gate", GATE_COLS), ("rel_bias", 8 * REL_PAD)) + tuple((n, D_MODEL) for n in SMALL[2:]):
        out[n] = flat[off:off + size]
        off += size
    out["rel_bias"] = out["rel_bias"].reshape(8, REL_PAD)[:, :REL_COLS]
    return out


def kernel(x, w_in, b_gate, w_sb_proj, w_ca_proj, rel_bias, w_out, ln1_g, ln1_b, w_mlp_in, w_mlp_out, ln2_g, ln2_b, loss_target, m_w_in, m_b_gate, m_w_sb_proj, m_w_ca_proj, m_rel_bias, m_w_out, m_ln1_g, m_ln1_b, m_w_mlp_in, m_w_mlp_out, m_ln2_g, m_ln2_b, v_w_in, v_b_gate, v_w_sb_proj, v_w_ca_proj, v_rel_bias, v_w_out, v_ln1_g, v_ln1_b, v_w_mlp_in, v_w_mlp_out, v_ln2_g, v_ln2_b):
    w = dict(w_in=w_in, b_gate=b_gate, w_sb_proj=w_sb_proj, w_ca_proj=w_ca_proj, rel_bias=rel_bias, w_out=w_out,
             ln1_g=ln1_g, ln1_b=ln1_b, w_mlp_in=w_mlp_in, w_mlp_out=w_mlp_out, ln2_g=ln2_g, ln2_b=ln2_b)
    m = dict(w_in=m_w_in, b_gate=m_b_gate, w_sb_proj=m_w_sb_proj, w_ca_proj=m_w_ca_proj, rel_bias=m_rel_bias,
             w_out=m_w_out, ln1_g=m_ln1_g, ln1_b=m_ln1_b, w_mlp_in=m_w_mlp_in, w_mlp_out=m_w_mlp_out,
             ln2_g=m_ln2_g, ln2_b=m_ln2_b)
    v = dict(w_in=v_w_in, b_gate=v_b_gate, w_sb_proj=v_w_sb_proj, w_ca_proj=v_w_ca_proj, rel_bias=v_rel_bias,
             w_out=v_w_out, ln1_g=v_ln1_g, ln1_b=v_ln1_b, w_mlp_in=v_w_mlp_in, w_mlp_out=v_w_mlp_out,
             ln2_g=v_ln2_g, ln2_b=v_ln2_b)

    gathered = dict(zip(BIG, all_gather("gather_weights", [w[n].astype(BF16) for n in BIG])))
    loss, grad_x, big, small = local_grads(
        x[0], loss_target[0], gathered["w_in"], gathered["w_sb_proj"], gathered["w_ca_proj"],
        gathered["w_out"].reshape(D_MODEL, D_MODEL), gathered["w_mlp_in"], gathered["w_mlp_out"],
        b_gate, rel_bias, ln1_g, ln1_b, ln2_g, ln2_b)
    loss = lax.psum(loss[0, 0], ("x", "y", "c"))

    c = lax.axis_index("c")
    place = jnp.stack([c, 2 * lax.axis_index("x") + lax.axis_index("y")]).astype(jnp.int32)
    by_owner = [big[n].reshape((4, 2) + w[n].shape) for n in BIG]
    landed = sibling_exchange("reduce_siblings", by_owner)
    own, sums = zip(*[chip_sum("chip_sum_" + n, g, l, place) for n, g, l in zip(BIG, by_owner, landed)])
    arrived = chip_exchange("reduce_chips", list(sums))

    grads, deltas, new_m, new_v = {}, {}, {}, {}
    for n, o, r in zip(BIG, own, arrived):
        grads[n], deltas[n], new_m[n], new_v[n] = adamw(
            "adamw_" + n, w[n], m[n], v[n], [(o, None), (r, 0), (r, 1), (r, 2)], tm=256)

    parts = all_gather("gather_small_grads", [_pack_small(small)])[0]
    packed = adamw("adamw_small", _pack_small(w), _pack_small(m), _pack_small(v),
                   [(parts, d) for d in range(N_DEV)])
    for out, p in zip((grads, deltas, new_m, new_v), packed):
        for n, val in _unpack_small(p).items():
            out[n] = val.reshape(w[n].shape)

    return (loss, grad_x[None], *[grads[n] for n in NAMES], *[deltas[n] for n in NAMES],
            *[new_m[n] for n in NAMES], *[new_v[n] for n in NAMES])
```

```python
import functools

import jax
import jax.numpy as jnp
from jax import lax
from jax.experimental import pallas as pl
from jax.experimental.pallas import tpu as pltpu

F32 = jnp.float32
BF16 = jnp.bfloat16
MESH = pl.DeviceIdType.MESH

N_DEV = 8
D_MODEL = 1024
HEAD_DIM = 64
ATT_WIDTH = 512
N_PAIRS = ATT_WIDTH // 128
QKV_COLS = 6 * ATT_WIDTH
GATE_COLS = 2 * D_MODEL
IN_COLS = QKV_COLS + GATE_COLS
IN_SHARD = IN_COLS // N_DEV
D_FF = 4 * D_MODEL
FF_SHARD = D_FF // N_DEV
PROJ_SHARD = D_MODEL // N_DEV
ATT_BLOCK = 128
CA_TILES = 5
CHUNK = 64
CA_PREV_CHUNKS = 8
REL_CLIP = 256
REL_PAD = 640
ALPHA = 2.0 ** 0.25
LN_EPS = 1e-5
QK_SCALE = HEAD_DIM ** -0.5
NEG_BIG = -1e30
VMEM_LIMIT = 56 * 1024 * 1024

ADAM_LR = 0.001
ADAM_B1 = 0.9
ADAM_B2 = 0.999
ADAM_EPS = 1e-08
ADAM_WD = 0.01
ADAM_STEP = 10

_NT = (((1,), (1,)), ((), ()))
_TN = (((0,), (0,)), ((), ()))


def _dot(a, b):
    return jnp.dot(a, b, preferred_element_type=F32)


def _dot_nt(a, b):
    return lax.dot_general(a, b, _NT, preferred_element_type=F32)


def _dot_tn(a, b):
    return lax.dot_general(a, b, _TN, preferred_element_type=F32)


def _params(semantics=None):
    return pltpu.CompilerParams(dimension_semantics=semantics, vmem_limit_bytes=VMEM_LIMIT)


def _row(v):
    return v.reshape(1, -1)


def in_proj(x, wg_in, tm=512):
    s = x.shape[0]

    def body(x_ref, w_ref, qkv_ref, lg_ref, xb_ref):
        xb = x_ref[...].astype(BF16)
        xb_ref[...] = xb
        for j in range(N_DEV):
            acc = _dot(xb, w_ref[j])
            lo, hi = IN_SHARD * j, IN_SHARD * (j + 1)
            if hi <= QKV_COLS:
                qkv_ref[:, lo:hi] = acc.astype(BF16)
            elif lo >= QKV_COLS:
                lg_ref[:, lo - QKV_COLS:hi - QKV_COLS] = acc
            else:
                qkv_ref[:, lo:QKV_COLS] = acc[:, :QKV_COLS - lo].astype(BF16)
                lg_ref[:, 0:hi - QKV_COLS] = acc[:, QKV_COLS - lo:]

    return pl.pallas_call(
        body, name="in_proj", grid=(s // tm,),
        in_specs=[pl.BlockSpec((tm, D_MODEL), lambda i: (i, 0)),
                  pl.BlockSpec((N_DEV, D_MODEL, IN_SHARD), lambda i: (0, 0, 0))],
        out_specs=[pl.BlockSpec((tm, QKV_COLS), lambda i: (i, 0)),
                   pl.BlockSpec((tm, GATE_COLS), lambda i: (i, 0)),
                   pl.BlockSpec((tm, D_MODEL), lambda i: (i, 0))],
        out_shape=[jax.ShapeDtypeStruct((s, QKV_COLS), BF16),
                   jax.ShapeDtypeStruct((s, GATE_COLS), F32),
                   jax.ShapeDtypeStruct((s, D_MODEL), BF16)],
        compiler_params=_params(("arbitrary",)),
    )(x, wg_in)


def _head_masks():
    lane = lax.broadcasted_iota(jnp.int32, (1, 128), 1)
    first = lane < HEAD_DIM
    return first, jnp.logical_not(first)


SB_CHUNK = 512
SB_SUB = SB_CHUNK // ATT_BLOCK


def _tri_pair():
    row = lax.broadcasted_iota(jnp.int32, (2 * ATT_BLOCK, 2 * ATT_BLOCK), 0)
    col = lax.broadcasted_iota(jnp.int32, (2 * ATT_BLOCK, 2 * ATT_BLOCK), 1)
    same = (row >= ATT_BLOCK) == (col >= ATT_BLOCK)
    return jnp.logical_and(same, row >= col).astype(BF16)


def _pair_bf16(pieces):
    zeros = jnp.zeros((ATT_BLOCK, ATT_BLOCK), BF16)
    return jnp.concatenate(
        [jnp.concatenate([zeros if p is None else p.astype(BF16) for p in row], axis=1) for row in pieces], axis=0)


def _suffix_sums(pieces, tri_pair):
    hi = [[None if p is None else p.astype(BF16) for p in row] for row in pieces]
    lo = [[None if p is None else p - h.astype(F32) for p, h in zip(row, hrow)] for row, hrow in zip(pieces, hi)]
    both = _dot(jnp.concatenate([_pair_bf16(hi), _pair_bf16(lo)], axis=0), tri_pair)
    return both[:SB_CHUNK] + both[SB_CHUNK:]


def _piece_kind(diag, r, j):
    if not diag or j < r:
        return "full"
    return "diag" if j == r else "none"


def _piece(v, r, h):
    return v[ATT_BLOCK * r:ATT_BLOCK * (r + 1), ATT_BLOCK * h:ATT_BLOCK * (h + 1)]


def _row_totals(csum, h):
    return jnp.concatenate(
        [jnp.broadcast_to(_piece(csum, r, h)[:, 0:1], (ATT_BLOCK, ATT_BLOCK)) for r in range(SB_SUB)], axis=0)


def _log_keep(z):
    return jnp.minimum(-z, 0.0) - jnp.log(1.0 + jnp.exp(jnp.minimum(z, -z)))


def _sb_weights(z, runs, diag, j, strict, tri_pair):
    keeps = []
    for r in range(SB_SUB):
        kind = _piece_kind(diag, r, j)
        row = [None if kind == "none" else _log_keep(_piece(z, r, h)) for h in range(2)]
        keeps.append([jnp.where(strict, keep, 0.0) for keep in row] if kind == "diag" else row)
    csum = _suffix_sums(keeps, tri_pair)
    weights = []
    for r in range(SB_SUB):
        kind = _piece_kind(diag, r, j)
        row = [None if kind == "none" else
               jnp.exp(_piece(z, r, h) + (_piece(csum, r, h) + runs[h][ATT_BLOCK * r:ATT_BLOCK * (r + 1)]))
               for h in range(2)]
        weights.append([jnp.where(strict, a, 0.0) for a in row] if kind == "diag" else row)
    return keeps, weights, csum


def _by_head(blk, masks):
    zero = jnp.zeros_like(blk)
    return jnp.concatenate([jnp.where(m, blk, zero) for m in masks], axis=0)


def sb_fwd(qkv):
    s = qkv.shape[0]
    nq = s // SB_CHUNK

    def body(q_ref, k_ref, v_ref, o_ref, run_s):
        i = pl.program_id(1)
        masks = _head_masks()
        q = q_ref[...] * QK_SCALE
        row = lax.broadcasted_iota(jnp.int32, (ATT_BLOCK, ATT_BLOCK), 0)
        col = lax.broadcasted_iota(jnp.int32, (ATT_BLOCK, ATT_BLOCK), 1)
        strict = col < row
        tri_pair = _tri_pair()
        run_s[...] = jnp.zeros_like(run_s)
        o_ref[...] = jnp.zeros_like(o_ref)

        def chunk(cb, diag):
            start = cb * SB_CHUNK
            runs = [run_s[0], run_s[1]]
            acc = None
            for j in reversed(range(SB_SUB)):
                keys = pl.ds(pl.multiple_of(start + ATT_BLOCK * j, ATT_BLOCK), ATT_BLOCK)
                z = _dot_nt(q, _by_head(k_ref[keys, :], masks))
                _, weights, csum = _sb_weights(z, runs, diag, j, strict, tri_pair)
                part = _dot(_pair_bf16(weights), _by_head(v_ref[keys, :], masks))
                acc = part if acc is None else acc + part
                runs = [runs[h] + _row_totals(csum, h) for h in range(2)]
            run_s[0], run_s[1] = runs
            o_ref[...] += acc

        chunk(i, True)

        def step(it, carry):
            chunk(i - 1 - it, False)
            return carry

        lax.fori_loop(0, i, step, 0)

    return pl.pallas_call(
        body, name="sb_fwd", grid=(N_PAIRS, nq),
        in_specs=[pl.BlockSpec((SB_CHUNK, 128), lambda p, i: (i, p)),
                  pl.BlockSpec((s, 128), lambda p, i: (0, N_PAIRS + p)),
                  pl.BlockSpec((s, 128), lambda p, i: (0, 2 * N_PAIRS + p))],
        out_specs=pl.BlockSpec((SB_CHUNK, 128), lambda p, i: (i, p)),
        out_shape=jax.ShapeDtypeStruct((s, ATT_WIDTH), F32),
        scratch_shapes=[pltpu.VMEM((2, SB_CHUNK, 128), F32)],
        compiler_params=_params(("arbitrary", "arbitrary")),
    )(qkv, qkv, qkv)


def sb_bwd(qkv, o_sb, do_sb):
    s = qkv.shape[0]
    nq = s // SB_CHUNK

    def body(q_ref, k_ref, v_ref, o_ref, do_ref, dq_ref, dk_ref, dv_ref, run_l_s, run_e_s, etot_s, dq_s):
        i = pl.program_id(1)

        @pl.when(i == 0)
        def _():
            dk_ref[...] = jnp.zeros_like(dk_ref)
            dv_ref[...] = jnp.zeros_like(dv_ref)

        masks = _head_masks()
        q = q_ref[...] * QK_SCALE
        do = do_ref[...]
        prod = do.astype(F32) * o_ref[...]
        for h in range(2):
            total = jnp.sum(jnp.where(masks[h], prod, 0.0), axis=1, keepdims=True)
            etot_s[h] = jnp.broadcast_to(total, (SB_CHUNK, ATT_BLOCK))
        row = lax.broadcasted_iota(jnp.int32, (ATT_BLOCK, ATT_BLOCK), 0)
        col = lax.broadcasted_iota(jnp.int32, (ATT_BLOCK, ATT_BLOCK), 1)
        strict = col < row
        tri_pair = _tri_pair()
        run_l_s[...] = jnp.zeros_like(run_l_s)
        run_e_s[...] = jnp.zeros_like(run_e_s)
        dq_s[...] = jnp.zeros_like(dq_s)

        def chunk(cb, diag):
            start = cb * SB_CHUNK
            runs_l = [run_l_s[0], run_l_s[1]]
            runs_e = [run_e_s[0], run_e_s[1]]
            e_tot = [etot_s[0], etot_s[1]]
            dq = None
            for j in reversed(range(SB_SUB)):
                keys = pl.ds(pl.multiple_of(start + ATT_BLOCK * j, ATT_BLOCK), ATT_BLOCK)
                kk = _by_head(k_ref[keys, :], masks)
                vv = _by_head(v_ref[keys, :], masks)
                z = _dot_nt(q, kk)
                da = _dot_nt(do, vv)
                keeps, weights, csum = _sb_weights(z, runs_l, diag, j, strict, tri_pair)
                ab = [[None if a is None else a.astype(BF16) for a in wrow] for wrow in weights]
                es = [[None if a is None else a.astype(F32) * _piece(da, r, h) for h, a in enumerate(arow)]
                      for r, arow in enumerate(ab)]
                esum = _suffix_sums(es, tri_pair)
                dzs = []
                for r in range(SB_SUB):
                    kind = _piece_kind(diag, r, j)
                    if kind == "none":
                        dzs.append([None, None])
                        continue
                    dz_row = []
                    for h in range(2):
                        rows = slice(ATT_BLOCK * r, ATT_BLOCK * (r + 1))
                        sig = jnp.exp(_piece(z, r, h) + keeps[r][h])
                        after = _piece(esum, r, h) + runs_e[h][rows] - es[r][h]
                        dz = es[r][h] - sig * (e_tot[h][rows] - after)
                        dz_row.append(jnp.where(strict, dz, 0.0) if kind == "diag" else dz)
                    dzs.append(dz_row)
                dz = _pair_bf16(dzs)
                part = _dot(dz, kk)
                dq = part if dq is None else dq + part
                dk = _dot_tn(dz, q)
                dv = _dot_tn(_pair_bf16(ab), do)
                dk_ref[keys, :] += jnp.where(masks[0], dk[:ATT_BLOCK], dk[ATT_BLOCK:])
                dv_ref[keys, :] += jnp.where(masks[0], dv[:ATT_BLOCK], dv[ATT_BLOCK:])
                runs_l = [runs_l[h] + _row_totals(csum, h) for h in range(2)]
                runs_e = [runs_e[h] + _row_totals(esum, h) for h in range(2)]
            run_l_s[0], run_l_s[1] = runs_l
            run_e_s[0], run_e_s[1] = runs_e
            dq_s[...] += dq

        chunk(i, True)

        def step(it, carry):
            chunk(i - 1 - it, False)
            return carry

        lax.fori_loop(0, i, step, 0)
        dq_ref[...] = (dq_s[...] * QK_SCALE).astype(BF16)

    tile = lambda p, i: (i, p)
    state = pltpu.VMEM((2, SB_CHUNK, 128), F32)
    return pl.pallas_call(
        body, name="sb_bwd", grid=(N_PAIRS, nq),
        in_specs=[pl.BlockSpec((SB_CHUNK, 128), tile),
                  pl.BlockSpec((s, 128), lambda p, i: (0, N_PAIRS + p)),
                  pl.BlockSpec((s, 128), lambda p, i: (0, 2 * N_PAIRS + p)),
                  pl.BlockSpec((SB_CHUNK, 128), tile),
                  pl.BlockSpec((SB_CHUNK, 128), tile)],
        out_specs=[pl.BlockSpec((SB_CHUNK, 128), tile),
                   pl.BlockSpec((s, 128), lambda p, i: (0, p)),
                   pl.BlockSpec((s, 128), lambda p, i: (0, p))],
        out_shape=[jax.ShapeDtypeStruct((s, ATT_WIDTH), BF16),
                   jax.ShapeDtypeStruct((s, ATT_WIDTH), F32),
                   jax.ShapeDtypeStruct((s, ATT_WIDTH), F32)],
        scratch_shapes=[state, state, state, pltpu.VMEM((SB_CHUNK, 128), F32)],
        compiler_params=_params(("arbitrary", "arbitrary")),
    )(qkv, qkv, qkv, o_sb, do_sb)


def ca_bias_tiles(rel_bias):
    heads = rel_bias.shape[0]
    wide = 2 * ATT_BLOCK

    def body(rel_ref, out_ref):
        bins = lax.broadcasted_iota(jnp.int32, (REL_PAD, wide), 0)
        pos = lax.broadcasted_iota(jnp.int32, (REL_PAD, wide), 1)
        shift = (ATT_BLOCK - 1) - lax.broadcasted_iota(jnp.int32, (ATT_BLOCK, wide), 0)
        r = lax.broadcasted_iota(jnp.int32, (ATT_BLOCK, ATT_BLOCK), 0)
        c = lax.broadcasted_iota(jnp.int32, (ATT_BLOCK, ATT_BLOCK), 1)
        for j in range(CA_TILES):
            hot = (jnp.clip(ATT_BLOCK * (j + 1) - 1 - pos, -REL_CLIP, REL_CLIP) + REL_CLIP == bins).astype(F32)
            lines = jnp.dot(rel_ref[...], hot, preferred_element_type=F32, precision=lax.Precision.HIGHEST)
            back = 2 * j + (r >> 6) - (c >> 6)
            ok = jnp.logical_and(back >= 0, back <= CA_PREV_CHUNKS)
            for h in range(heads):
                tile = jnp.broadcast_to(lines[h:h + 1, :], (ATT_BLOCK, wide))
                for bit in range(7):
                    rolled = pltpu.roll(tile, wide - (1 << bit), 1)
                    tile = jnp.where((shift >> bit) & 1 == 1, rolled, tile)
                out_ref[h, j] = jnp.where(ok, tile[:, :ATT_BLOCK], NEG_BIG)

    padded = jnp.pad(rel_bias, ((0, 0), (0, REL_PAD - rel_bias.shape[1])))
    return pl.pallas_call(
        body, name="ca_bias_tiles",
        out_shape=jax.ShapeDtypeStruct((heads, CA_TILES, ATT_BLOCK, ATT_BLOCK), F32),
        compiler_params=_params(),
    )(padded)


def _ca_probs(qh, k_ref, bias_ref, h, i):
    scores, kblks = [], []
    for j in range(CA_TILES):
        kb = i - j
        start = pl.multiple_of(jnp.maximum(kb, 0) * ATT_BLOCK, ATT_BLOCK)
        kblk = k_ref[pl.ds(start, ATT_BLOCK), :]
        z = _dot_nt(qh, kblk) + bias_ref[h, j]
        scores.append(jnp.where(kb >= 0, z, NEG_BIG))
        kblks.append(kblk)
    m = functools.reduce(jnp.maximum, [jnp.max(z, axis=1, keepdims=True) for z in scores])
    ex = [jnp.exp(z - m) for z in scores]
    inv = 1.0 / functools.reduce(jnp.add, [jnp.sum(e, axis=1, keepdims=True) for e in ex])
    return [e * inv for e in ex], kblks


def ca_fwd(qkv, bias_tiles):
    s = qkv.shape[0]
    nq = s // ATT_BLOCK

    def body(q_ref, k_ref, v_ref, bias_ref, o_ref):
        i = pl.program_id(1)
        masks = _head_masks()
        q = q_ref[...] * QK_SCALE
        zero = jnp.zeros_like(q)
        acc = jnp.zeros((ATT_BLOCK, 128), F32)
        for h in range(2):
            probs, _ = _ca_probs(jnp.where(masks[h], q, zero), k_ref, bias_ref, h, i)
            for j in range(CA_TILES):
                start = pl.multiple_of(jnp.maximum(i - j, 0) * ATT_BLOCK, ATT_BLOCK)
                vblk = v_ref[pl.ds(start, ATT_BLOCK), :]
                acc = acc + _dot(probs[j].astype(BF16), jnp.where(masks[h], vblk, jnp.zeros_like(vblk)))
        o_ref[...] = acc

    return pl.pallas_call(
        body, name="ca_fwd", grid=(N_PAIRS, nq),
        in_specs=[pl.BlockSpec((ATT_BLOCK, 128), lambda p, i: (i, 3 * N_PAIRS + p)),
                  pl.BlockSpec((s, 128), lambda p, i: (0, 4 * N_PAIRS + p)),
                  pl.BlockSpec((s, 128), lambda p, i: (0, 5 * N_PAIRS + p)),
                  pl.BlockSpec((2, CA_TILES, ATT_BLOCK, ATT_BLOCK), lambda p, i: (p, 0, 0, 0))],
        out_specs=pl.BlockSpec((ATT_BLOCK, 128), lambda p, i: (i, p)),
        out_shape=jax.ShapeDtypeStruct((s, ATT_WIDTH), F32),
        compiler_params=_params(("arbitrary", "arbitrary")),
    )(qkv, qkv, qkv, bias_tiles)


def ca_bwd(qkv, bias_tiles, do_ca):
    s = qkv.shape[0]
    nq = s // ATT_BLOCK

    def body(q_ref, k_ref, v_ref, bias_ref, do_ref, dq_ref, dk_ref, dv_ref, db_ref):
        i = pl.program_id(1)

        @pl.when(i == 0)
        def _():
            dk_ref[...] = jnp.zeros_like(dk_ref)
            dv_ref[...] = jnp.zeros_like(dv_ref)
            db_ref[...] = jnp.zeros_like(db_ref)

        masks = _head_masks()
        q = q_ref[...] * QK_SCALE
        zero = jnp.zeros_like(q)
        do = do_ref[...]
        dq = jnp.zeros((ATT_BLOCK, 128), F32)
        for h in range(2):
            qh = jnp.where(masks[h], q, zero)
            doh = jnp.where(masks[h], do, zero)
            probs, kblks = _ca_probs(qh, k_ref, bias_ref, h, i)
            starts = [pl.multiple_of(jnp.maximum(i - j, 0) * ATT_BLOCK, ATT_BLOCK) for j in range(CA_TILES)]
            dps = [_dot_nt(doh, v_ref[pl.ds(starts[j], ATT_BLOCK), :]) for j in range(CA_TILES)]
            delta = functools.reduce(
                jnp.add, [jnp.sum(probs[j] * dps[j], axis=1, keepdims=True) for j in range(CA_TILES)])
            for j in range(CA_TILES):
                ds = probs[j] * (dps[j] - delta)
                db_ref[h, j] += ds
                dsb = ds.astype(BF16)
                dq = dq + _dot(dsb, jnp.where(masks[h], kblks[j], jnp.zeros_like(kblks[j])))
                dk_ref[pl.ds(starts[j], ATT_BLOCK), :] += _dot_tn(dsb, qh)
                dv_ref[pl.ds(starts[j], ATT_BLOCK), :] += _dot_tn(probs[j].astype(BF16), doh)
        dq_ref[...] = (dq * QK_SCALE).astype(BF16)

    return pl.pallas_call(
        body, name="ca_bwd", grid=(N_PAIRS, nq),
        in_specs=[pl.BlockSpec((ATT_BLOCK, 128), lambda p, i: (i, 3 * N_PAIRS + p)),
                  pl.BlockSpec((s, 128), lambda p, i: (0, 4 * N_PAIRS + p)),
                  pl.BlockSpec((s, 128), lambda p, i: (0, 5 * N_PAIRS + p)),
                  pl.BlockSpec((2, CA_TILES, ATT_BLOCK, ATT_BLOCK), lambda p, i: (p, 0, 0, 0)),
                  pl.BlockSpec((ATT_BLOCK, 128), lambda p, i: (i, p))],
        out_specs=[pl.BlockSpec((ATT_BLOCK, 128), lambda p, i: (i, p)),
                   pl.BlockSpec((s, 128), lambda p, i: (0, p)),
                   pl.BlockSpec((s, 128), lambda p, i: (0, p)),
                   pl.BlockSpec((2, CA_TILES, ATT_BLOCK, ATT_BLOCK), lambda p, i: (p, 0, 0, 0))],
        out_shape=[jax.ShapeDtypeStruct((s, ATT_WIDTH), BF16),
                   jax.ShapeDtypeStruct((s, ATT_WIDTH), F32),
                   jax.ShapeDtypeStruct((s, ATT_WIDTH), F32),
                   jax.ShapeDtypeStruct((2 * N_PAIRS, CA_TILES, ATT_BLOCK, ATT_BLOCK), F32)],
        compiler_params=_params(("arbitrary", "arbitrary")),
    )(qkv, qkv, qkv, bias_tiles, do_ca)


def rel_bias_grad(db_tiles):
    heads = db_tiles.shape[0]

    def body(db_ref, out_ref):
        row = lax.broadcasted_iota(jnp.int32, (ATT_BLOCK, ATT_BLOCK), 0)
        col = lax.broadcasted_iota(jnp.int32, (ATT_BLOCK, ATT_BLOCK), 1)
        cp = lax.broadcasted_iota(jnp.int32, (ATT_BLOCK, REL_PAD), 0)
        bins = lax.broadcasted_iota(jnp.int32, (ATT_BLOCK, REL_PAD), 1)
        total = jnp.zeros((heads, REL_PAD), F32)
        for j in range(CA_TILES):
            sums_lo, sums_up = [], []
            for h in range(heads):
                tile = db_ref[h, j]
                for bit in range(7):
                    rolled = pltpu.roll(tile, ATT_BLOCK - (1 << bit), 1)
                    tile = jnp.where((row >> bit) & 1 == 1, rolled, tile)
                was_lower = (col + row >= ATT_BLOCK) | (col == 0)
                sums_lo.append(jnp.sum(jnp.where(was_lower, tile, 0.0), axis=0, keepdims=True))
                sums_up.append(jnp.sum(jnp.where(was_lower, 0.0, tile), axis=0, keepdims=True))
            lo = jnp.concatenate(sums_lo, axis=0)
            up = jnp.concatenate(sums_up, axis=0)
            d_lo = jnp.where(cp == 0, 0, ATT_BLOCK - cp)
            hot_lo = (jnp.clip(ATT_BLOCK * j + d_lo, -REL_CLIP, REL_CLIP) + REL_CLIP == bins).astype(F32)
            hot_up = (jnp.clip(ATT_BLOCK * j - cp, -REL_CLIP, REL_CLIP) + REL_CLIP == bins).astype(F32)
            total = total + jnp.dot(lo, hot_lo, preferred_element_type=F32, precision=lax.Precision.HIGHEST)
            total = total + jnp.dot(up, hot_up, preferred_element_type=F32, precision=lax.Precision.HIGHEST)
        out_ref[...] = total

    return pl.pallas_call(
        body, name="rel_bias_grad",
        out_shape=jax.ShapeDtypeStruct((heads, REL_PAD), F32),
        compiler_params=_params(),
    )(db_tiles)


def _assemble_cols(dst_ref, src_ref):
    w = src_ref.shape[2]
    for j in range(N_DEV):
        dst_ref[:, w * j:w * (j + 1)] = src_ref[j]


def _merge(o_sb, o_ca, lg, bg, wsb, wca):
    y_sb = _dot(o_sb.astype(BF16), wsb)
    y_ca = _dot(o_ca.astype(BF16), wca)
    gates = jax.nn.sigmoid(lg + bg)
    g_sb, g_ca = gates[:, :D_MODEL], gates[:, D_MODEL:]
    return y_sb, y_ca, g_sb, g_ca, g_sb * y_sb + g_ca * y_ca


def mix_fwd(x, o_sb, o_ca, lg, b_gate, wg_sb, wg_ca, w_out, tm=256):
    s = x.shape[0]

    def body(x_ref, osb_ref, oca_ref, lg_ref, bg_ref, wsb_ref, wca_ref, wout_ref, u_ref, wsb_s, wca_s):
        @pl.when(pl.program_id(0) == 0)
        def _():
            _assemble_cols(wsb_s, wsb_ref)
            _assemble_cols(wca_s, wca_ref)

        merged = _merge(osb_ref[...], oca_ref[...], lg_ref[...], bg_ref[...], wsb_s[...], wca_s[...])[4]
        u_ref[...] = ALPHA * x_ref[...] + _dot(merged.astype(BF16), wout_ref[...])

    rows = lambda w: pl.BlockSpec((tm, w), lambda i: (i, 0))
    whole = lambda shape: pl.BlockSpec(shape, lambda i: (0,) * len(shape))
    return pl.pallas_call(
        body, name="mix_fwd", grid=(s // tm,),
        in_specs=[rows(D_MODEL), rows(ATT_WIDTH), rows(ATT_WIDTH), rows(GATE_COLS), whole((1, GATE_COLS)),
                  whole((N_DEV, ATT_WIDTH, PROJ_SHARD)), whole((N_DEV, ATT_WIDTH, PROJ_SHARD)),
                  whole((D_MODEL, D_MODEL))],
        out_specs=rows(D_MODEL),
        out_shape=jax.ShapeDtypeStruct((s, D_MODEL), F32),
        scratch_shapes=[pltpu.VMEM((ATT_WIDTH, D_MODEL), BF16), pltpu.VMEM((ATT_WIDTH, D_MODEL), BF16)],
        compiler_params=_params(("arbitrary",)),
    )(x, o_sb, o_ca, lg, _row(b_gate), wg_sb, wg_ca, w_out)


def mix_bwd(du1, o_sb, o_ca, lg, b_gate, wg_sb, wg_ca, w_out, tm=256):
    s = du1.shape[0]

    def body(du_ref, osb_ref, oca_ref, lg_ref, bg_ref, wsb_ref, wca_ref, wout_ref,
             dlg_ref, dosb_ref, doca_ref, dysb_ref, dyca_ref, mg_ref, dbg_ref, wsb_s, wca_s):
        @pl.when(pl.program_id(0) == 0)
        def _():
            _assemble_cols(wsb_s, wsb_ref)
            _assemble_cols(wca_s, wca_ref)
            dbg_ref[...] = jnp.zeros_like(dbg_ref)

        y_sb, y_ca, g_sb, g_ca, merged = _merge(
            osb_ref[...], oca_ref[...], lg_ref[...], bg_ref[...], wsb_s[...], wca_s[...])
        mg_ref[...] = merged.astype(BF16)
        dm = _dot_nt(du_ref[...].astype(BF16), wout_ref[...])
        dl_sb = dm * y_sb * g_sb * (1.0 - g_sb)
        dl_ca = dm * y_ca * g_ca * (1.0 - g_ca)
        dlg_ref[:, :D_MODEL] = dl_sb.astype(BF16)
        dlg_ref[:, D_MODEL:] = dl_ca.astype(BF16)
        dbg_ref[:, :D_MODEL] += jnp.sum(dl_sb, axis=0, keepdims=True)
        dbg_ref[:, D_MODEL:] += jnp.sum(dl_ca, axis=0, keepdims=True)
        dy_sb = (dm * g_sb).astype(BF16)
        dy_ca = (dm * g_ca).astype(BF16)
        dysb_ref[...] = dy_sb
        dyca_ref[...] = dy_ca
        dosb_ref[...] = _dot_nt(dy_sb, wsb_s[...]).astype(BF16)
        doca_ref[...] = _dot_nt(dy_ca, wca_s[...]).astype(BF16)

    rows = lambda w: pl.BlockSpec((tm, w), lambda i: (i, 0))
    whole = lambda shape: pl.BlockSpec(shape, lambda i: (0,) * len(shape))
    return pl.pallas_call(
        body, name="mix_bwd", grid=(s // tm,),
        in_specs=[rows(D_MODEL), rows(ATT_WIDTH), rows(ATT_WIDTH), rows(GATE_COLS), whole((1, GATE_COLS)),
                  whole((N_DEV, ATT_WIDTH, PROJ_SHARD)), whole((N_DEV, ATT_WIDTH, PROJ_SHARD)),
                  whole((D_MODEL, D_MODEL))],
        out_specs=[rows(GATE_COLS), rows(ATT_WIDTH), rows(ATT_WIDTH), rows(D_MODEL), rows(D_MODEL),
                   rows(D_MODEL), whole((1, GATE_COLS))],
        out_shape=[jax.ShapeDtypeStruct((s, GATE_COLS), BF16),
                   jax.ShapeDtypeStruct((s, ATT_WIDTH), BF16), jax.ShapeDtypeStruct((s, ATT_WIDTH), BF16),
                   jax.ShapeDtypeStruct((s, D_MODEL), BF16), jax.ShapeDtypeStruct((s, D_MODEL), BF16),
                   jax.ShapeDtypeStruct((s, D_MODEL), BF16), jax.ShapeDtypeStruct((1, GATE_COLS), F32)],
        scratch_shapes=[pltpu.VMEM((ATT_WIDTH, D_MODEL), BF16), pltpu.VMEM((ATT_WIDTH, D_MODEL), BF16)],
        compiler_params=_params(("arbitrary",)),
    )(du1, o_sb, o_ca, lg, _row(b_gate), wg_sb, wg_ca, w_out)


def _ln_stats(u):
    mu = jnp.mean(u, axis=1, keepdims=True)
    cen = u - mu
    var = jnp.mean(cen * cen, axis=1, keepdims=True)
    rstd = lax.rsqrt(var + LN_EPS)
    return cen * rstd, rstd


def _ln_bwd(dy, xhat, rstd, gain):
    dyg = dy * gain
    m1 = jnp.mean(dyg, axis=1, keepdims=True)
    m2 = jnp.mean(dyg * xhat, axis=1, keepdims=True)
    return rstd * (dyg - m1 - xhat * m2)


def mlp_fwd(u1, target, wg_in, wg_out, ln1_g, ln1_b, ln2_g, ln2_b, tm=512):
    s = u1.shape[0]
    nk = N_DEV

    def body(u_ref, t_ref, win_ref, wout_ref, g1_ref, b1_ref, g2_ref, b2_ref,
             hm_ref, a_ref, x1b_ref, du2_ref, loss_ref, dg2_ref, db2_ref, x1_s, acc_s):
        i, k = pl.program_id(0), pl.program_id(1)

        @pl.when(jnp.logical_and(i == 0, k == 0))
        def _():
            loss_ref[...] = jnp.zeros_like(loss_ref)
            dg2_ref[...] = jnp.zeros_like(dg2_ref)
            db2_ref[...] = jnp.zeros_like(db2_ref)

        @pl.when(k == 0)
        def _():
            xhat, _ = _ln_stats(u_ref[...])
            x1 = xhat * g1_ref[...] + b1_ref[...]
            x1_s[...] = x1
            x1b_ref[...] = x1.astype(BF16)
            acc_s[...] = jnp.zeros_like(acc_s)

        hm = _dot(x1b_ref[...], win_ref[0])
        hm_ref[...] = hm
        r = jnp.maximum(hm, 0.0)
        a = (r * r).astype(BF16)
        a_ref[...] = a
        acc_s[...] += _dot(a, wout_ref[0])

        @pl.when(k == nk - 1)
        def _():
            xhat, rstd = _ln_stats(ALPHA * x1_s[...] + acc_s[...])
            diff = xhat * g2_ref[...] + b2_ref[...] - t_ref[...]
            per_token = jnp.mean(diff * diff, axis=1, keepdims=True)
            loss_ref[...] += 0.5 * jnp.sum(per_token, axis=0, keepdims=True)
            dy = diff * (1.0 / D_MODEL)
            dg2_ref[...] += jnp.sum(dy * xhat, axis=0, keepdims=True)
            db2_ref[...] += jnp.sum(dy, axis=0, keepdims=True)
            du2_ref[...] = _ln_bwd(dy, xhat, rstd, g2_ref[...])

    rows = pl.BlockSpec((tm, D_MODEL), lambda i, k: (i, 0))
    vec = pl.BlockSpec((1, D_MODEL), lambda i, k: (0, 0))
    ff = pl.BlockSpec((tm, FF_SHARD), lambda i, k: (i, k))
    return pl.pallas_call(
        body, name="mlp_fwd", grid=(s // tm, nk),
        in_specs=[rows, rows,
                  pl.BlockSpec((1, D_MODEL, FF_SHARD), lambda i, k: (k, 0, 0)),
                  pl.BlockSpec((1, FF_SHARD, D_MODEL), lambda i, k: (k, 0, 0)),
                  vec, vec, vec, vec],
        out_specs=[ff, ff, rows, rows, pl.BlockSpec((1, 1), lambda i, k: (0, 0)), vec, vec],
        out_shape=[jax.ShapeDtypeStruct((s, D_FF), F32), jax.ShapeDtypeStruct((s, D_FF), BF16),
                   jax.ShapeDtypeStruct((s, D_MODEL), BF16), jax.ShapeDtypeStruct((s, D_MODEL), F32),
                   jax.ShapeDtypeStruct((1, 1), F32),
                   jax.ShapeDtypeStruct((1, D_MODEL), F32), jax.ShapeDtypeStruct((1, D_MODEL), F32)],
        scratch_shapes=[pltpu.VMEM((tm, D_MODEL), F32), pltpu.VMEM((tm, D_MODEL), F32)],
        compiler_params=_params(("arbitrary", "arbitrary")),
    )(u1, target, wg_in, wg_out, _row(ln1_g), _row(ln1_b), _row(ln2_g), _row(ln2_b))


def mlp_bwd(du2, hm, u1, wg_in, wg_out, ln1_g, tm=512):
    s = du2.shape[0]
    nk = N_DEV

    def body(du2_ref, hm_ref, u_ref, win_ref, wout_ref, g1_ref,
             dhm_ref, du1_ref, dg1_ref, db1_ref, du2b_s, acc_s):
        i, k = pl.program_id(0), pl.program_id(1)

        @pl.when(jnp.logical_and(i == 0, k == 0))
        def _():
            dg1_ref[...] = jnp.zeros_like(dg1_ref)
            db1_ref[...] = jnp.zeros_like(db1_ref)

        @pl.when(k == 0)
        def _():
            du2b_s[...] = du2_ref[...].astype(BF16)
            acc_s[...] = jnp.zeros_like(acc_s)

        da = _dot_nt(du2b_s[...], wout_ref[0])
        dhm = (2.0 * jnp.maximum(hm_ref[...], 0.0) * da).astype(BF16)
        dhm_ref[...] = dhm
        acc_s[...] += _dot_nt(dhm, win_ref[0])

        @pl.when(k == nk - 1)
        def _():
            dx1 = ALPHA * du2_ref[...] + acc_s[...]
            xhat, rstd = _ln_stats(u_ref[...])
            dg1_ref[...] += jnp.sum(dx1 * xhat, axis=0, keepdims=True)
            db1_ref[...] += jnp.sum(dx1, axis=0, keepdims=True)
            du1_ref[...] = _ln_bwd(dx1, xhat, rstd, g1_ref[...])

    rows = pl.BlockSpec((tm, D_MODEL), lambda i, k: (i, 0))
    vec = pl.BlockSpec((1, D_MODEL), lambda i, k: (0, 0))
    ff = pl.BlockSpec((tm, FF_SHARD), lambda i, k: (i, k))
    return pl.pallas_call(
        body, name="mlp_bwd", grid=(s // tm, nk),
        in_specs=[rows, ff, rows,
                  pl.BlockSpec((1, D_MODEL, FF_SHARD), lambda i, k: (k, 0, 0)),
                  pl.BlockSpec((1, FF_SHARD, D_MODEL), lambda i, k: (k, 0, 0)),
                  vec],
        out_specs=[ff, rows, vec, vec],
        out_shape=[jax.ShapeDtypeStruct((s, D_FF), BF16), jax.ShapeDtypeStruct((s, D_MODEL), F32),
                   jax.ShapeDtypeStruct((1, D_MODEL), F32), jax.ShapeDtypeStruct((1, D_MODEL), F32)],
        scratch_shapes=[pltpu.VMEM((tm, D_MODEL), BF16), pltpu.VMEM((tm, D_MODEL), F32)],
        compiler_params=_params(("arbitrary", "arbitrary")),
    )(du2, hm, u1, wg_in, wg_out, _row(ln1_g))


def in_bwd(du1, pieces, wg_in, tm=512):
    s = du1.shape[0]
    widths = [p.shape[1] for p in pieces]
    assert sum(widths) == IN_COLS
    n = len(pieces)

    def body(*refs):
        du_ref, piece_refs, w_ref = refs[0], refs[1:1 + n], refs[1 + n]
        gx_ref, dh_ref = refs[2 + n], refs[3 + n]
        off = 0
        for ref, w in zip(piece_refs, widths):
            dh_ref[:, off:off + w] = ref[...].astype(BF16)
            off += w
        acc = ALPHA * du_ref[...]
        for j in range(N_DEV):
            acc = acc + _dot_nt(dh_ref[:, IN_SHARD * j:IN_SHARD * (j + 1)], w_ref[j])
        gx_ref[...] = acc

    rows = lambda w: pl.BlockSpec((tm, w), lambda i: (i, 0))
    return pl.pallas_call(
        body, name="in_bwd", grid=(s // tm,),
        in_specs=[rows(D_MODEL)] + [rows(w) for w in widths]
                 + [pl.BlockSpec((N_DEV, D_MODEL, IN_SHARD), lambda i: (0, 0, 0))],
        out_specs=[rows(D_MODEL), rows(IN_COLS)],
        out_shape=[jax.ShapeDtypeStruct((s, D_MODEL), F32), jax.ShapeDtypeStruct((s, IN_COLS), BF16)],
        compiler_params=_params(("arbitrary",)),
    )(du1, *pieces, wg_in)


def wgrad(name, a, b, tm, tn, shard_cols=None, tk=512):
    kk, m = a.shape
    n = b.shape[1]
    nk = kk // tk
    w = shard_cols

    def body(a_ref, b_ref, out_ref, acc_s):
        k = pl.program_id(2)

        @pl.when(k == 0)
        def _():
            acc_s[...] = jnp.zeros_like(acc_s)

        acc_s[...] += _dot_tn(a_ref[...].astype(BF16), b_ref[...].astype(BF16))

        @pl.when(k == nk - 1)
        def _():
            if w is None:
                out_ref[...] = acc_s[...]
            else:
                for j in range(tn // w):
                    out_ref[j] = acc_s[:, w * j:w * (j + 1)]

    if w is None:
        out_spec = pl.BlockSpec((tm, tn), lambda i, j, k: (i, j))
        out_shape = jax.ShapeDtypeStruct((m, n), F32)
    else:
        out_spec = pl.BlockSpec((tn // w, tm, w), lambda i, j, k: (j, i, 0))
        out_shape = jax.ShapeDtypeStruct((n // w, m, w), F32)
    return pl.pallas_call(
        body, name=name, grid=(m // tm, n // tn, nk),
        in_specs=[pl.BlockSpec((tk, tm), lambda i, j, k: (k, i)),
                  pl.BlockSpec((tk, tn), lambda i, j, k: (k, j))],
        out_specs=out_spec, out_shape=out_shape,
        scratch_shapes=[pltpu.VMEM((tm, tn), F32)],
        compiler_params=_params(("arbitrary", "arbitrary", "arbitrary")),
    )(a, b)


def adamw(name, w, m, v, parts, tm=None):
    rows, cols = w.shape
    tm = rows if tm is None else min(tm, rows)
    n = len(parts)
    c1 = 1.0 - ADAM_B1 ** ADAM_STEP
    c2 = 1.0 - ADAM_B2 ** ADAM_STEP

    def body(*refs):
        w_ref, m_ref, v_ref = refs[:3]
        part_refs = refs[3:3 + n]
        g_ref, d_ref, nm_ref, nv_ref = refs[3 + n:]
        g = None
        for ref, (_, index) in zip(part_refs, parts):
            term = (ref[...] if index is None else ref[0]).astype(F32)
            g = term if g is None else g + term
        new_m = ADAM_B1 * m_ref[...] + (1.0 - ADAM_B1) * g
        new_v = ADAM_B2 * v_ref[...] + (1.0 - ADAM_B2) * (g * g)
        m_hat = new_m / c1
        v_hat = new_v / c2
        g_ref[...] = g
        d_ref[...] = -ADAM_LR * (m_hat / (jnp.sqrt(v_hat) + ADAM_EPS) + ADAM_WD * w_ref[...])
        nm_ref[...] = new_m
        nv_ref[...] = new_v

    spec = pl.BlockSpec((tm, cols), lambda i: (i, 0))
    shape = jax.ShapeDtypeStruct((rows, cols), F32)

    def part_spec(index):
        if index is None:
            return spec
        return pl.BlockSpec((1, tm, cols), lambda i: (index, i, 0))

    return pl.pallas_call(
        body, name=name, grid=(rows // tm,),
        in_specs=[spec] * 3 + [part_spec(index) for _, index in parts],
        out_specs=[spec] * 4, out_shape=[shape] * 4,
        compiler_params=_params(("arbitrary",)),
    )(w, m, v, *[array for array, _ in parts])


def local_grads(x, target, wg_in, wg_sb, wg_ca, w_out, wg_mi, wg_mo, b_gate, rel_bias,
                ln1_g, ln1_b, ln2_g, ln2_b):
    qkv, lg, xb = in_proj(x, wg_in)
    bias_tiles = ca_bias_tiles(rel_bias)
    o_sb = sb_fwd(qkv)
    o_ca = ca_fwd(qkv, bias_tiles)
    u1 = mix_fwd(x, o_sb, o_ca, lg, b_gate, wg_sb, wg_ca, w_out)
    hm, a, x1b, du2, loss, dg2, db2 = mlp_fwd(u1, target, wg_mi, wg_mo, ln1_g, ln1_b, ln2_g, ln2_b)
    dhm, du1, dg1, db1 = mlp_bwd(du2, hm, u1, wg_mi, wg_mo, ln1_g)
    g_mi = wgrad("wgrad_mlp_in", x1b, dhm, tm=D_MODEL, tn=FF_SHARD, shard_cols=FF_SHARD)
    g_mo = wgrad("wgrad_mlp_out", a, du2, tm=FF_SHARD, tn=D_MODEL)
    dlg, do_sb, do_ca, dy_sb, dy_ca, merged, dbg = mix_bwd(du1, o_sb, o_ca, lg, b_gate, wg_sb, wg_ca, w_out)
    g_out = wgrad("wgrad_out", merged, du1, tm=D_MODEL, tn=D_MODEL)
    g_sb = wgrad("wgrad_sb_proj", o_sb, dy_sb, tm=ATT_WIDTH, tn=D_MODEL, shard_cols=PROJ_SHARD)
    g_ca = wgrad("wgrad_ca_proj", o_ca, dy_ca, tm=ATT_WIDTH, tn=D_MODEL, shard_cols=PROJ_SHARD)
    dq_sb, dk_sb, dv_sb = sb_bwd(qkv, o_sb, do_sb)
    dq_ca, dk_ca, dv_ca, db_tiles = ca_bwd(qkv, bias_tiles, do_ca)
    g_rel = rel_bias_grad(db_tiles)
    grad_x, dh = in_bwd(du1, [dq_sb, dk_sb, dv_sb, dq_ca, dk_ca, dv_ca, dlg], wg_in)
    g_in = wgrad("wgrad_in", xb, dh, tm=D_MODEL, tn=IN_SHARD, shard_cols=IN_SHARD)
    big = dict(w_in=g_in, w_sb_proj=g_sb, w_ca_proj=g_ca, w_out=g_out, w_mlp_in=g_mi, w_mlp_out=g_mo)
    small = dict(b_gate=dbg, rel_bias=g_rel, ln1_g=dg1, ln1_b=db1, ln2_g=dg2, ln2_b=db2)
    return loss, grad_x, big, small


HBM_SPEC = pl.BlockSpec(memory_space=pl.ANY)


def _place():
    x, y, c = lax.axis_index("x"), lax.axis_index("y"), lax.axis_index("c")
    other_chips = [(1 - x, y), (x, 1 - y), (1 - x, 1 - y)]
    return x, y, c, other_chips


def all_gather(name, arrays):
    n = len(arrays)

    def body(*refs):
        ins, outs = refs[:n], refs[n:2 * n]
        send_sems, recv_sems, local_sems = refs[2 * n:]
        x, y, c, chips = _place()
        me = 4 * x + 2 * y + c
        sibling = (x, y, 1 - c)

        def copy(a, k, block, to, src=None):
            return pltpu.make_async_remote_copy(
                src_ref=outs[a].at[block] if src is None else src, dst_ref=outs[a].at[block],
                send_sem=send_sems.at[a, k], recv_sem=recv_sems.at[a, k],
                device_id=to, device_id_type=MESH)

        started = []
        local = [pltpu.make_async_copy(ins[a], outs[a].at[me], local_sems.at[a]) for a in range(n)]
        for a in range(n):
            local[a].start()
            sends = [copy(a, 0, me, sibling, src=ins[a])]
            sends += [copy(a, 1 + j, me, (cx, cy, c), src=ins[a]) for j, (cx, cy) in enumerate(chips)]
            for cp in sends:
                cp.start()
            started += sends
        for a in range(n):
            for j, (cx, cy) in enumerate(chips):
                block = 4 * cx + 2 * cy + c
                copy(a, 1 + j, block, sibling).wait_recv()
                passed = copy(a, 4 + j, block, sibling)
                passed.start()
                started.append(passed)
        for a in range(n):
            copy(a, 0, 4 * x + 2 * y + (1 - c), sibling).wait_recv()
            for j, (cx, cy) in enumerate(chips):
                copy(a, 4 + j, 4 * cx + 2 * cy + (1 - c), sibling).wait_recv()
        for cp in started:
            cp.wait_send()
        for cp in local:
            cp.wait()

    return pl.pallas_call(
        body, name=name,
        in_specs=[HBM_SPEC] * n, out_specs=[HBM_SPEC] * n,
        out_shape=[jax.ShapeDtypeStruct((N_DEV,) + a.shape, a.dtype) for a in arrays],
        scratch_shapes=[pltpu.SemaphoreType.DMA((n, 7)), pltpu.SemaphoreType.DMA((n, 7)),
                        pltpu.SemaphoreType.DMA((n,))],
    )(*arrays)


def sibling_exchange(name, grads):
    n = len(grads)

    def body(*refs):
        ins, outs = refs[:n], refs[n:2 * n]
        send_sems, recv_sems = refs[2 * n:]
        x, y, c, _ = _place()
        copies = [pltpu.make_async_remote_copy(
            src_ref=ins[a].at[:, 1 - c], dst_ref=outs[a], send_sem=send_sems.at[a], recv_sem=recv_sems.at[a],
            device_id=(x, y, 1 - c), device_id_type=MESH) for a in range(n)]
        for cp in copies:
            cp.start()
        for cp in copies:
            cp.wait()

    return pl.pallas_call(
        body, name=name,
        in_specs=[HBM_SPEC] * n, out_specs=[HBM_SPEC] * n,
        out_shape=[jax.ShapeDtypeStruct((4,) + g.shape[2:], g.dtype) for g in grads],
        scratch_shapes=[pltpu.SemaphoreType.DMA((n,)), pltpu.SemaphoreType.DMA((n,))],
    )(*grads)


def chip_sum(name, grad, landed, place, tr=256):
    _, _, rows, cols = grad.shape
    tr = min(tr, rows)

    def body(place_ref, g_ref, l_ref, own_ref, out_ref):
        total = g_ref[0, 0] + l_ref[0]
        out_ref[0] = total.astype(BF16)

        @pl.when(pl.program_id(1) == place_ref[1])
        def _():
            own_ref[...] = total

    return pl.pallas_call(
        body, name=name,
        grid_spec=pltpu.PrefetchScalarGridSpec(
            num_scalar_prefetch=1, grid=(rows // tr, 4),
            in_specs=[pl.BlockSpec((1, 1, tr, cols), lambda i, ch, pos: (ch, pos[0], i, 0)),
                      pl.BlockSpec((1, tr, cols), lambda i, ch, pos: (ch, i, 0))],
            out_specs=[pl.BlockSpec((tr, cols), lambda i, ch, pos: (i, 0)),
                       pl.BlockSpec((1, tr, cols), lambda i, ch, pos: (ch, i, 0))]),
        out_shape=[jax.ShapeDtypeStruct((rows, cols), F32), jax.ShapeDtypeStruct((4, rows, cols), BF16)],
        compiler_params=_params(("arbitrary", "arbitrary")),
    )(place, grad, landed)


def chip_exchange(name, sums):
    n = len(sums)

    def body(*refs):
        ins, outs = refs[:n], refs[n:2 * n]
        send_sems, recv_sems = refs[2 * n:]
        _, _, c, chips = _place()
        copies = [pltpu.make_async_remote_copy(
            src_ref=ins[a].at[2 * cx + cy], dst_ref=outs[a].at[j],
            send_sem=send_sems.at[a, j], recv_sem=recv_sems.at[a, j],
            device_id=(cx, cy, c), device_id_type=MESH)
            for a in range(n) for j, (cx, cy) in enumerate(chips)]
        for cp in copies:
            cp.start()
        for cp in copies:
            cp.wait()

    return pl.pallas_call(
        body, name=name,
        in_specs=[HBM_SPEC] * n, out_specs=[HBM_SPEC] * n,
        out_shape=[jax.ShapeDtypeStruct((3,) + s.shape[1:], s.dtype) for s in sums],
        scratch_shapes=[pltpu.SemaphoreType.DMA((n, 3)), pltpu.SemaphoreType.DMA((n, 3))],
    )(*sums)


BIG = ("w_in", "w_sb_proj", "w_ca_proj", "w_out", "w_mlp_in", "w_mlp_out")
SMALL = ("b_gate", "rel_bias", "ln1_g", "ln1_b", "ln2_g", "ln2_b")
NAMES = ("w_in", "b_gate", "w_sb_proj", "w_ca_proj", "rel_bias", "w_out", "ln1_g", "ln1_b",
         "w_mlp_in", "w_mlp_out", "ln2_g", "ln2_b")
REL_COLS = 2 * REL_CLIP + 1


def _pack_small(t):
    rel = t["rel_bias"]
    if rel.shape[1] != REL_PAD:
        rel = jnp.pad(rel, ((0, 0), (0, REL_PAD - rel.shape[1])))
    flat = [t["b_gate"].reshape(-1), rel.reshape(-1)] + [t[n].reshape(-1) for n in SMALL[2:]]
    return jnp.concatenate(flat).reshape(-1, 128)


def _unpack_small(p):
    flat = p.reshape(-1)
    out, off = {}, 0
    for n, size in (("b_gate", GATE_COLS), ("rel_bias", 8 * REL_PAD)) + tuple((n, D_MODEL) for n in SMALL[2:]):
        out[n] = flat[off:off + size]
        off += size
    out["rel_bias"] = out["rel_bias"].reshape(8, REL_PAD)[:, :REL_COLS]
    return out


def kernel(x, w_in, b_gate, w_sb_proj, w_ca_proj, rel_bias, w_out, ln1_g, ln1_b, w_mlp_in, w_mlp_out, ln2_g, ln2_b, loss_target, m_w_in, m_b_gate, m_w_sb_proj, m_w_ca_proj, m_rel_bias, m_w_out, m_ln1_g, m_ln1_b, m_w_mlp_in, m_w_mlp_out, m_ln2_g, m_ln2_b, v_w_in, v_b_gate, v_w_sb_proj, v_w_ca_proj, v_rel_bias, v_w_out, v_ln1_g, v_ln1_b, v_w_mlp_in, v_w_mlp_out, v_ln2_g, v_ln2_b):
    w = dict(w_in=w_in, b_gate=b_gate, w_sb_proj=w_sb_proj, w_ca_proj=w_ca_proj, rel_bias=rel_bias, w_out=w_out,
             ln1_g=ln1_g, ln1_b=ln1_b, w_mlp_in=w_mlp_in, w_mlp_out=w_mlp_out, ln2_g=ln2_g, ln2_b=ln2_b)
    m = dict(w_in=m_w_in, b_gate=m_b_gate, w_sb_proj=m_w_sb_proj, w_ca_proj=m_w_ca_proj, rel_bias=m_rel_bias,
             w_out=m_w_out, ln1_g=m_ln1_g, ln1_b=m_ln1_b, w_mlp_in=m_w_mlp_in, w_mlp_out=m_w_mlp_out,
             ln2_g=m_ln2_g, ln2_b=m_ln2_b)
    v = dict(w_in=v_w_in, b_gate=v_b_gate, w_sb_proj=v_w_sb_proj, w_ca_proj=v_w_ca_proj, rel_bias=v_rel_bias,
             w_out=v_w_out, ln1_g=v_ln1_g, ln1_b=v_ln1_b, w_mlp_in=v_w_mlp_in, w_mlp_out=v_w_mlp_out,
             ln2_g=v_ln2_g, ln2_b=v_ln2_b)

    gathered = dict(zip(BIG, all_gather("gather_weights", [w[n].astype(BF16) for n in BIG])))
    loss, grad_x, big, small = local_grads(
        x[0], loss_target[0], gathered["w_in"], gathered["w_sb_proj"], gathered["w_ca_proj"],
        gathered["w_out"].reshape(D_MODEL, D_MODEL), gathered["w_mlp_in"], gathered["w_mlp_out"],
        b_gate, rel_bias, ln1_g, ln1_b, ln2_g, ln2_b)
    loss = lax.psum(loss[0, 0], ("x", "y", "c"))

    c = lax.axis_index("c")
    place = jnp.stack([c, 2 * lax.axis_index("x") + lax.axis_index("y")]).astype(jnp.int32)
    by_owner = [big[n].reshape((4, 2) + w[n].shape) for n in BIG]
    landed = sibling_exchange("reduce_siblings", by_owner)
    own, sums = zip(*[chip_sum("chip_sum_" + n, g, l, place) for n, g, l in zip(BIG, by_owner, landed)])
    arrived = chip_exchange("reduce_chips", list(sums))

    grads, deltas, new_m, new_v = {}, {}, {}, {}
    for n, o, r in zip(BIG, own, arrived):
        grads[n], deltas[n], new_m[n], new_v[n] = adamw(
            "adamw_" + n, w[n], m[n], v[n], [(o, None), (r, 0), (r, 1), (r, 2)], tm=256)

    parts = all_gather("gather_small_grads", [_pack_small(small)])[0]
    packed = adamw("adamw_small", _pack_small(w), _pack_small(m), _pack_small(v),
                   [(parts, d) for d in range(N_DEV)])
    for out, p in zip((grads, deltas, new_m, new_v), packed):
        for n, val in _unpack_small(p).items():
            out[n] = val.reshape(w[n].shape)

    return (loss, grad_x[None], *[grads[n] for n in NAMES], *[deltas[n] for n in NAMES],
            *[new_m[n] for n in NAMES], *[new_v[n] for n in NAMES])
```

```python
import functools

import jax
import jax.numpy as jnp
from jax import lax
from jax.experimental import pallas as pl
from jax.experimental.pallas import tpu as pltpu

F32 = jnp.float32
BF16 = jnp.bfloat16
MESH = pl.DeviceIdType.MESH

N_DEV = 8
D_MODEL = 1024
HEAD_DIM = 64
ATT_WIDTH = 512
N_PAIRS = ATT_WIDTH // 128
QKV_COLS = 6 * ATT_WIDTH
GATE_COLS = 2 * D_MODEL
IN_COLS = QKV_COLS + GATE_COLS
IN_SHARD = IN_COLS // N_DEV
D_FF = 4 * D_MODEL
FF_SHARD = D_FF // N_DEV
PROJ_SHARD = D_MODEL // N_DEV
ATT_BLOCK = 128
CA_TILES = 5
CA_ROWS = 512
CHUNK = 64
CA_PREV_CHUNKS = 8
REL_CLIP = 256
REL_PAD = 640
ALPHA = 2.0 ** 0.25
LN_EPS = 1e-5
QK_SCALE = HEAD_DIM ** -0.5
NEG_BIG = -1e30
VMEM_LIMIT = 56 * 1024 * 1024

ADAM_LR = 0.001
ADAM_B1 = 0.9
ADAM_B2 = 0.999
ADAM_EPS = 1e-08
ADAM_WD = 0.01
ADAM_STEP = 10

_NT = (((1,), (1,)), ((), ()))
_TN = (((0,), (0,)), ((), ()))


def _dot(a, b):
    return jnp.dot(a, b, preferred_element_type=F32)


def _dot_nt(a, b):
    return lax.dot_general(a, b, _NT, preferred_element_type=F32)


def _dot_tn(a, b):
    return lax.dot_general(a, b, _TN, preferred_element_type=F32)


def _params(semantics=None):
    return pltpu.CompilerParams(dimension_semantics=semantics, vmem_limit_bytes=VMEM_LIMIT)


def _row(v):
    return v.reshape(1, -1)


def in_proj(x, wg_in, tm=512):
    s = x.shape[0]

    def body(x_ref, w_ref, qkv_ref, lg_ref, xb_ref):
        xb = x_ref[...].astype(BF16)
        xb_ref[...] = xb
        for j in range(N_DEV):
            acc = _dot(xb, w_ref[j])
            lo, hi = IN_SHARD * j, IN_SHARD * (j + 1)
            if hi <= QKV_COLS:
                qkv_ref[:, lo:hi] = acc.astype(BF16)
            elif lo >= QKV_COLS:
                lg_ref[:, lo - QKV_COLS:hi - QKV_COLS] = acc
            else:
                qkv_ref[:, lo:QKV_COLS] = acc[:, :QKV_COLS - lo].astype(BF16)
                lg_ref[:, 0:hi - QKV_COLS] = acc[:, QKV_COLS - lo:]

    return pl.pallas_call(
        body, name="in_proj", grid=(s // tm,),
        in_specs=[pl.BlockSpec((tm, D_MODEL), lambda i: (i, 0)),
                  pl.BlockSpec((N_DEV, D_MODEL, IN_SHARD), lambda i: (0, 0, 0))],
        out_specs=[pl.BlockSpec((tm, QKV_COLS), lambda i: (i, 0)),
                   pl.BlockSpec((tm, GATE_COLS), lambda i: (i, 0)),
                   pl.BlockSpec((tm, D_MODEL), lambda i: (i, 0))],
        out_shape=[jax.ShapeDtypeStruct((s, QKV_COLS), BF16),
                   jax.ShapeDtypeStruct((s, GATE_COLS), F32),
                   jax.ShapeDtypeStruct((s, D_MODEL), BF16)],
        compiler_params=_params(("arbitrary",)),
    )(x, wg_in)


def _head_masks():
    lane = lax.broadcasted_iota(jnp.int32, (1, 128), 1)
    first = lane < HEAD_DIM
    return first, jnp.logical_not(first)


SB_CHUNK = 512
SB_SUB = SB_CHUNK // ATT_BLOCK


def _tri_pair():
    row = lax.broadcasted_iota(jnp.int32, (2 * ATT_BLOCK, 2 * ATT_BLOCK), 0)
    col = lax.broadcasted_iota(jnp.int32, (2 * ATT_BLOCK, 2 * ATT_BLOCK), 1)
    same = (row >= ATT_BLOCK) == (col >= ATT_BLOCK)
    return jnp.logical_and(same, row >= col).astype(BF16)


def _pair_bf16(pieces):
    zeros = jnp.zeros((ATT_BLOCK, ATT_BLOCK), BF16)
    return jnp.concatenate(
        [jnp.concatenate([zeros if p is None else p.astype(BF16) for p in row], axis=1) for row in pieces], axis=0)


def _suffix_sums(pieces, tri_pair):
    hi = [[None if p is None else p.astype(BF16) for p in row] for row in pieces]
    lo = [[None if p is None else p - h.astype(F32) for p, h in zip(row, hrow)] for row, hrow in zip(pieces, hi)]
    both = _dot(jnp.concatenate([_pair_bf16(hi), _pair_bf16(lo)], axis=0), tri_pair)
    return both[:SB_CHUNK] + both[SB_CHUNK:]


def _piece_kind(diag, r, j):
    if not diag or j < r:
        return "full"
    return "diag" if j == r else "none"


def _piece(v, r, h):
    return v[ATT_BLOCK * r:ATT_BLOCK * (r + 1), ATT_BLOCK * h:ATT_BLOCK * (h + 1)]


def _row_totals(csum, h):
    return jnp.concatenate(
        [jnp.broadcast_to(_piece(csum, r, h)[:, 0:1], (ATT_BLOCK, ATT_BLOCK)) for r in range(SB_SUB)], axis=0)


def _log_keep(z):
    return jnp.minimum(-z, 0.0) - jnp.log(1.0 + jnp.exp(jnp.minimum(z, -z)))


def _sb_weights(z, runs, diag, j, strict, tri_pair):
    keeps = []
    for r in range(SB_SUB):
        kind = _piece_kind(diag, r, j)
        row = [None if kind == "none" else _log_keep(_piece(z, r, h)) for h in range(2)]
        keeps.append([jnp.where(strict, keep, 0.0) for keep in row] if kind == "diag" else row)
    csum = _suffix_sums(keeps, tri_pair)
    weights = []
    for r in range(SB_SUB):
        kind = _piece_kind(diag, r, j)
        row = [None if kind == "none" else
               jnp.exp(_piece(z, r, h) + (_piece(csum, r, h) + runs[h][ATT_BLOCK * r:ATT_BLOCK * (r + 1)]))
               for h in range(2)]
        weights.append([jnp.where(strict, a, 0.0) for a in row] if kind == "diag" else row)
    return keeps, weights, csum


def _by_head(blk, masks):
    zero = jnp.zeros_like(blk)
    return jnp.concatenate([jnp.where(m, blk, zero) for m in masks], axis=0)


def sb_fwd(qkv):
    s = qkv.shape[0]
    nq = s // SB_CHUNK

    def body(q_ref, k_ref, v_ref, o_ref, run_s):
        i = pl.program_id(1)
        masks = _head_masks()
        q = q_ref[...] * QK_SCALE
        row = lax.broadcasted_iota(jnp.int32, (ATT_BLOCK, ATT_BLOCK), 0)
        col = lax.broadcasted_iota(jnp.int32, (ATT_BLOCK, ATT_BLOCK), 1)
        strict = col < row
        tri_pair = _tri_pair()
        run_s[...] = jnp.zeros_like(run_s)
        o_ref[...] = jnp.zeros_like(o_ref)

        def chunk(cb, diag):
            start = cb * SB_CHUNK
            runs = [run_s[0], run_s[1]]
            acc = None
            for j in reversed(range(SB_SUB)):
                keys = pl.ds(pl.multiple_of(start + ATT_BLOCK * j, ATT_BLOCK), ATT_BLOCK)
                z = _dot_nt(q, _by_head(k_ref[keys, :], masks))
                _, weights, csum = _sb_weights(z, runs, diag, j, strict, tri_pair)
                part = _dot(_pair_bf16(weights), _by_head(v_ref[keys, :], masks))
                acc = part if acc is None else acc + part
                runs = [runs[h] + _row_totals(csum, h) for h in range(2)]
            run_s[0], run_s[1] = runs
            o_ref[...] += acc

        chunk(i, True)

        def step(it, carry):
            chunk(i - 1 - it, False)
            return carry

        lax.fori_loop(0, i, step, 0)

    return pl.pallas_call(
        body, name="sb_fwd", grid=(N_PAIRS, nq),
        in_specs=[pl.BlockSpec((SB_CHUNK, 128), lambda p, i: (i, p)),
                  pl.BlockSpec((s, 128), lambda p, i: (0, N_PAIRS + p)),
                  pl.BlockSpec((s, 128), lambda p, i: (0, 2 * N_PAIRS + p))],
        out_specs=pl.BlockSpec((SB_CHUNK, 128), lambda p, i: (i, p)),
        out_shape=jax.ShapeDtypeStruct((s, ATT_WIDTH), F32),
        scratch_shapes=[pltpu.VMEM((2, SB_CHUNK, 128), F32)],
        compiler_params=_params(("arbitrary", "arbitrary")),
    )(qkv, qkv, qkv)


def sb_bwd(qkv, o_sb, do_sb):
    s = qkv.shape[0]
    nq = s // SB_CHUNK

    def body(q_ref, k_ref, v_ref, o_ref, do_ref, dq_ref, dk_ref, dv_ref, run_l_s, run_e_s, etot_s, dq_s):
        i = pl.program_id(1)

        @pl.when(i == 0)
        def _():
            dk_ref[...] = jnp.zeros_like(dk_ref)
            dv_ref[...] = jnp.zeros_like(dv_ref)

        masks = _head_masks()
        q = q_ref[...] * QK_SCALE
        do = do_ref[...]
        prod = do.astype(F32) * o_ref[...]
        for h in range(2):
            total = jnp.sum(jnp.where(masks[h], prod, 0.0), axis=1, keepdims=True)
            etot_s[h] = jnp.broadcast_to(total, (SB_CHUNK, ATT_BLOCK))
        row = lax.broadcasted_iota(jnp.int32, (ATT_BLOCK, ATT_BLOCK), 0)
        col = lax.broadcasted_iota(jnp.int32, (ATT_BLOCK, ATT_BLOCK), 1)
        strict = col < row
        tri_pair = _tri_pair()
        run_l_s[...] = jnp.zeros_like(run_l_s)
        run_e_s[...] = jnp.zeros_like(run_e_s)
        dq_s[...] = jnp.zeros_like(dq_s)

        def chunk(cb, diag):
            start = cb * SB_CHUNK
            runs_l = [run_l_s[0], run_l_s[1]]
            runs_e = [run_e_s[0], run_e_s[1]]
            e_tot = [etot_s[0], etot_s[1]]
            dq = None
            for j in reversed(range(SB_SUB)):
                keys = pl.ds(pl.multiple_of(start + ATT_BLOCK * j, ATT_BLOCK), ATT_BLOCK)
                kk = _by_head(k_ref[keys, :], masks)
                vv = _by_head(v_ref[keys, :], masks)
                z = _dot_nt(q, kk)
                da = _dot_nt(do, vv)
                keeps, weights, csum = _sb_weights(z, runs_l, diag, j, strict, tri_pair)
                ab = [[None if a is None else a.astype(BF16) for a in wrow] for wrow in weights]
                es = [[None if a is None else a.astype(F32) * _piece(da, r, h) for h, a in enumerate(arow)]
                      for r, arow in enumerate(ab)]
                esum = _suffix_sums(es, tri_pair)
                dzs = []
                for r in range(SB_SUB):
                    kind = _piece_kind(diag, r, j)
                    if kind == "none":
                        dzs.append([None, None])
                        continue
                    dz_row = []
                    for h in range(2):
                        rows = slice(ATT_BLOCK * r, ATT_BLOCK * (r + 1))
                        sig = jnp.exp(_piece(z, r, h) + keeps[r][h])
                        after = _piece(esum, r, h) + runs_e[h][rows] - es[r][h]
                        dz = es[r][h] - sig * (e_tot[h][rows] - after)
                        dz_row.append(jnp.where(strict, dz, 0.0) if kind == "diag" else dz)
                    dzs.append(dz_row)
                dz = _pair_bf16(dzs)
                part = _dot(dz, kk)
                dq = part if dq is None else dq + part
                dk = _dot_tn(dz, q)
                dv = _dot_tn(_pair_bf16(ab), do)
                dk_ref[keys, :] += jnp.where(masks[0], dk[:ATT_BLOCK], dk[ATT_BLOCK:])
                dv_ref[keys, :] += jnp.where(masks[0], dv[:ATT_BLOCK], dv[ATT_BLOCK:])
                runs_l = [runs_l[h] + _row_totals(csum, h) for h in range(2)]
                runs_e = [runs_e[h] + _row_totals(esum, h) for h in range(2)]
            run_l_s[0], run_l_s[1] = runs_l
            run_e_s[0], run_e_s[1] = runs_e
            dq_s[...] += dq

        chunk(i, True)

        def step(it, carry):
            chunk(i - 1 - it, False)
            return carry

        lax.fori_loop(0, i, step, 0)
        dq_ref[...] = (dq_s[...] * QK_SCALE).astype(BF16)

    tile = lambda p, i: (i, p)
    state = pltpu.VMEM((2, SB_CHUNK, 128), F32)
    return pl.pallas_call(
        body, name="sb_bwd", grid=(N_PAIRS, nq),
        in_specs=[pl.BlockSpec((SB_CHUNK, 128), tile),
                  pl.BlockSpec((s, 128), lambda p, i: (0, N_PAIRS + p)),
                  pl.BlockSpec((s, 128), lambda p, i: (0, 2 * N_PAIRS + p)),
                  pl.BlockSpec((SB_CHUNK, 128), tile),
                  pl.BlockSpec((SB_CHUNK, 128), tile)],
        out_specs=[pl.BlockSpec((SB_CHUNK, 128), tile),
                   pl.BlockSpec((s, 128), lambda p, i: (0, p)),
                   pl.BlockSpec((s, 128), lambda p, i: (0, p))],
        out_shape=[jax.ShapeDtypeStruct((s, ATT_WIDTH), BF16),
                   jax.ShapeDtypeStruct((s, ATT_WIDTH), F32),
                   jax.ShapeDtypeStruct((s, ATT_WIDTH), F32)],
        scratch_shapes=[state, state, state, pltpu.VMEM((SB_CHUNK, 128), F32)],
        compiler_params=_params(("arbitrary", "arbitrary")),
    )(qkv, qkv, qkv, o_sb, do_sb)


def ca_bias_tiles(rel_bias):
    heads = rel_bias.shape[0]
    wide = 2 * ATT_BLOCK

    def body(rel_ref, out_ref):
        bins = lax.broadcasted_iota(jnp.int32, (REL_PAD, wide), 0)
        pos = lax.broadcasted_iota(jnp.int32, (REL_PAD, wide), 1)
        shift = (ATT_BLOCK - 1) - lax.broadcasted_iota(jnp.int32, (ATT_BLOCK, wide), 0)
        r = lax.broadcasted_iota(jnp.int32, (ATT_BLOCK, ATT_BLOCK), 0)
        c = lax.broadcasted_iota(jnp.int32, (ATT_BLOCK, ATT_BLOCK), 1)
        for j in range(CA_TILES):
            hot = (jnp.clip(ATT_BLOCK * (j + 1) - 1 - pos, -REL_CLIP, REL_CLIP) + REL_CLIP == bins).astype(F32)
            lines = jnp.dot(rel_ref[...], hot, preferred_element_type=F32, precision=lax.Precision.HIGHEST)
            back = 2 * j + (r >> 6) - (c >> 6)
            ok = jnp.logical_and(back >= 0, back <= CA_PREV_CHUNKS)
            for h in range(heads):
                tile = jnp.broadcast_to(lines[h:h + 1, :], (ATT_BLOCK, wide))
                for bit in range(7):
                    rolled = pltpu.roll(tile, wide - (1 << bit), 1)
                    tile = jnp.where((shift >> bit) & 1 == 1, rolled, tile)
                out_ref[h, j] = jnp.where(ok, tile[:, :ATT_BLOCK], NEG_BIG)

    padded = jnp.pad(rel_bias, ((0, 0), (0, REL_PAD - rel_bias.shape[1])))
    return pl.pallas_call(
        body, name="ca_bias_tiles",
        out_shape=jax.ShapeDtypeStruct((heads, CA_TILES, ATT_BLOCK, ATT_BLOCK), F32),
        compiler_params=_params(),
    )(padded)


CA_BLOCKS = CA_ROWS // ATT_BLOCK
CA_KEY_BLOCKS = CA_BLOCKS + CA_TILES - 1


def _ca_users(t):
    return [r for r in range(CA_BLOCKS) if 0 <= r + CA_TILES - 1 - t < CA_TILES]


def _ca_rows(v, t, r):
    at = _ca_users(t).index(r)
    return v[ATT_BLOCK * at:ATT_BLOCK * (at + 1)]


def _ca_window(step, k_ref, v_ref, masks):
    kks, vvs, inside, rows = [], [], [], []
    for t in range(CA_KEY_BLOCKS):
        block = step * CA_BLOCKS - (CA_TILES - 1) + t
        keys = pl.ds(pl.multiple_of(jnp.maximum(block, 0) * ATT_BLOCK, ATT_BLOCK), ATT_BLOCK)
        kks.append(_by_head(k_ref[keys, :], masks))
        vvs.append(_by_head(v_ref[keys, :], masks))
        inside.append(block >= 0)
        rows.append(keys)
    return kks, vvs, inside, rows


def _ca_probs(q, kks, inside, bias_ref):
    scores = [[[None] * CA_TILES for _ in range(2)] for _ in range(CA_BLOCKS)]
    for t in range(CA_KEY_BLOCKS):
        users = _ca_users(t)
        z = _dot_nt(q[ATT_BLOCK * users[0]:ATT_BLOCK * (users[-1] + 1)], kks[t])
        for r in users:
            j = r + CA_TILES - 1 - t
            for h in range(2):
                zz = _ca_rows(z, t, r)[:, ATT_BLOCK * h:ATT_BLOCK * (h + 1)] + bias_ref[h, j]
                scores[r][h][j] = jnp.where(inside[t], zz, NEG_BIG)
    probs = [[None, None] for _ in range(CA_BLOCKS)]
    for r in range(CA_BLOCKS):
        for h in range(2):
            m = jnp.max(functools.reduce(jnp.maximum, scores[r][h]), axis=1, keepdims=True)
            ex = [jnp.exp(z - m) for z in scores[r][h]]
            inv = 1.0 / jnp.sum(functools.reduce(jnp.add, ex), axis=1, keepdims=True)
            probs[r][h] = [e * inv for e in ex]
    return probs


def _ca_stack(tiles, t):
    return jnp.concatenate(
        [jnp.concatenate([tiles[r][h][r + CA_TILES - 1 - t].astype(BF16) for h in range(2)], axis=1)
         for r in _ca_users(t)], axis=0)


def ca_fwd(qkv, bias_tiles):
    s = qkv.shape[0]
    nq = s // CA_ROWS

    def body(q_ref, k_ref, v_ref, bias_ref, o_ref):
        masks = _head_masks()
        kks, vvs, inside, _ = _ca_window(pl.program_id(1), k_ref, v_ref, masks)
        probs = _ca_probs(q_ref[...] * QK_SCALE, kks, inside, bias_ref)
        out = [None] * CA_BLOCKS
        for t in range(CA_KEY_BLOCKS):
            part = _dot(_ca_stack(probs, t), vvs[t])
            for r in _ca_users(t):
                piece = _ca_rows(part, t, r)
                out[r] = piece if out[r] is None else out[r] + piece
        o_ref[...] = jnp.concatenate(out, axis=0)

    return pl.pallas_call(
        body, name="ca_fwd", grid=(N_PAIRS, nq),
        in_specs=[pl.BlockSpec((CA_ROWS, 128), lambda p, i: (i, 3 * N_PAIRS + p)),
                  pl.BlockSpec((s, 128), lambda p, i: (0, 4 * N_PAIRS + p)),
                  pl.BlockSpec((s, 128), lambda p, i: (0, 5 * N_PAIRS + p)),
                  pl.BlockSpec((2, CA_TILES, ATT_BLOCK, ATT_BLOCK), lambda p, i: (p, 0, 0, 0))],
        out_specs=pl.BlockSpec((CA_ROWS, 128), lambda p, i: (i, p)),
        out_shape=jax.ShapeDtypeStruct((s, ATT_WIDTH), F32),
        compiler_params=_params(("arbitrary", "arbitrary")),
    )(qkv, qkv, qkv, bias_tiles)


def ca_bwd(qkv, bias_tiles, do_ca):
    s = qkv.shape[0]
    nq = s // CA_ROWS

    def body(q_ref, k_ref, v_ref, bias_ref, do_ref, dq_ref, dk_ref, dv_ref, db_ref):
        big = pl.program_id(1)

        @pl.when(big == 0)
        def _():
            dk_ref[...] = jnp.zeros_like(dk_ref)
            dv_ref[...] = jnp.zeros_like(dv_ref)
            db_ref[...] = jnp.zeros_like(db_ref)

        masks = _head_masks()
        kks, vvs, inside, key_rows = _ca_window(big, k_ref, v_ref, masks)
        q = q_ref[...] * QK_SCALE
        do = do_ref[...]
        probs = _ca_probs(q, kks, inside, bias_ref)
        dps = [[[None] * CA_TILES for _ in range(2)] for _ in range(CA_BLOCKS)]
        for t in range(CA_KEY_BLOCKS):
            users = _ca_users(t)
            dp = _dot_nt(do[ATT_BLOCK * users[0]:ATT_BLOCK * (users[-1] + 1)], vvs[t])
            for r in users:
                for h in range(2):
                    dps[r][h][r + CA_TILES - 1 - t] = _ca_rows(dp, t, r)[:, ATT_BLOCK * h:ATT_BLOCK * (h + 1)]
        dss = [[None, None] for _ in range(CA_BLOCKS)]
        for r in range(CA_BLOCKS):
            for h in range(2):
                delta = jnp.sum(functools.reduce(jnp.add, [p * dp for p, dp in zip(probs[r][h], dps[r][h])]),
                                axis=1, keepdims=True)
                dss[r][h] = [p * (dp - delta) for p, dp in zip(probs[r][h], dps[r][h])]
        for h in range(2):
            for j in range(CA_TILES):
                db_ref[h, j] += functools.reduce(jnp.add, [dss[r][h][j] for r in range(CA_BLOCKS)])
        dq = [None] * CA_BLOCKS
        for t in range(CA_KEY_BLOCKS):
            users = _ca_users(t)
            rows = slice(ATT_BLOCK * users[0], ATT_BLOCK * (users[-1] + 1))
            ds = _ca_stack(dss, t)
            part = _dot(ds, kks[t])
            for r in users:
                piece = _ca_rows(part, t, r)
                dq[r] = piece if dq[r] is None else dq[r] + piece
            dk = _dot_tn(ds, q[rows])
            dv = _dot_tn(_ca_stack(probs, t), do[rows])
            dk_ref[key_rows[t], :] += jnp.where(masks[0], dk[:ATT_BLOCK], dk[ATT_BLOCK:])
            dv_ref[key_rows[t], :] += jnp.where(masks[0], dv[:ATT_BLOCK], dv[ATT_BLOCK:])
        dq_ref[...] = (jnp.concatenate(dq, axis=0) * QK_SCALE).astype(BF16)

    return pl.pallas_call(
        body, name="ca_bwd", grid=(N_PAIRS, nq),
        in_specs=[pl.BlockSpec((CA_ROWS, 128), lambda p, i: (i, 3 * N_PAIRS + p)),
                  pl.BlockSpec((s, 128), lambda p, i: (0, 4 * N_PAIRS + p)),
                  pl.BlockSpec((s, 128), lambda p, i: (0, 5 * N_PAIRS + p)),
                  pl.BlockSpec((2, CA_TILES, ATT_BLOCK, ATT_BLOCK), lambda p, i: (p, 0, 0, 0)),
                  pl.BlockSpec((CA_ROWS, 128), lambda p, i: (i, p))],
        out_specs=[pl.BlockSpec((CA_ROWS, 128), lambda p, i: (i, p)),
                   pl.BlockSpec((s, 128), lambda p, i: (0, p)),
                   pl.BlockSpec((s, 128), lambda p, i: (0, p)),
                   pl.BlockSpec((2, CA_TILES, ATT_BLOCK, ATT_BLOCK), lambda p, i: (p, 0, 0, 0))],
        out_shape=[jax.ShapeDtypeStruct((s, ATT_WIDTH), BF16),
                   jax.ShapeDtypeStruct((s, ATT_WIDTH), F32),
                   jax.ShapeDtypeStruct((s, ATT_WIDTH), F32),
                   jax.ShapeDtypeStruct((2 * N_PAIRS, CA_TILES, ATT_BLOCK, ATT_BLOCK), F32)],
        compiler_params=_params(("arbitrary", "arbitrary")),
    )(qkv, qkv, qkv, bias_tiles, do_ca)


def rel_bias_grad(db_tiles):
    heads = db_tiles.shape[0]

    def body(db_ref, out_ref):
        row = lax.broadcasted_iota(jnp.int32, (ATT_BLOCK, ATT_BLOCK), 0)
        col = lax.broadcasted_iota(jnp.int32, (ATT_BLOCK, ATT_BLOCK), 1)
        cp = lax.broadcasted_iota(jnp.int32, (ATT_BLOCK, REL_PAD), 0)
        bins = lax.broadcasted_iota(jnp.int32, (ATT_BLOCK, REL_PAD), 1)
        total = jnp.zeros((heads, REL_PAD), F32)
        for j in range(CA_TILES):
            sums_lo, sums_up = [], []
            for h in range(heads):
                tile = db_ref[h, j]
                for bit in range(7):
                    rolled = pltpu.roll(tile, ATT_BLOCK - (1 << bit), 1)
                    tile = jnp.where((row >> bit) & 1 == 1, rolled, tile)
                was_lower = (col + row >= ATT_BLOCK) | (col == 0)
                sums_lo.append(jnp.sum(jnp.where(was_lower, tile, 0.0), axis=0, keepdims=True))
                sums_up.append(jnp.sum(jnp.where(was_lower, 0.0, tile), axis=0, keepdims=True))
            lo = jnp.concatenate(sums_lo, axis=0)
            up = jnp.concatenate(sums_up, axis=0)
            d_lo = jnp.where(cp == 0, 0, ATT_BLOCK - cp)
            hot_lo = (jnp.clip(ATT_BLOCK * j + d_lo, -REL_CLIP, REL_CLIP) + REL_CLIP == bins).astype(F32)
            hot_up = (jnp.clip(ATT_BLOCK * j - cp, -REL_CLIP, REL_CLIP) + REL_CLIP == bins).astype(F32)
            total = total + jnp.dot(lo, hot_lo, preferred_element_type=F32, precision=lax.Precision.HIGHEST)
            total = total + jnp.dot(up, hot_up, preferred_element_type=F32, precision=lax.Precision.HIGHEST)
        out_ref[...] = total

    return pl.pallas_call(
        body, name="rel_bias_grad",
        out_shape=jax.ShapeDtypeStruct((heads, REL_PAD), F32),
        compiler_params=_params(),
    )(db_tiles)


def _assemble_cols(dst_ref, src_ref):
    w = src_ref.shape[2]
    for j in range(N_DEV):
        dst_ref[:, w * j:w * (j + 1)] = src_ref[j]


def _merge(o_sb, o_ca, lg, bg, wsb, wca):
    y_sb = _dot(o_sb.astype(BF16), wsb)
    y_ca = _dot(o_ca.astype(BF16), wca)
    gates = jax.nn.sigmoid(lg + bg)
    g_sb, g_ca = gates[:, :D_MODEL], gates[:, D_MODEL:]
    return y_sb, y_ca, g_sb, g_ca, g_sb * y_sb + g_ca * y_ca


def mix_fwd(x, o_sb, o_ca, lg, b_gate, wg_sb, wg_ca, w_out, tm=256):
    s = x.shape[0]

    def body(x_ref, osb_ref, oca_ref, lg_ref, bg_ref, wsb_ref, wca_ref, wout_ref, u_ref, wsb_s, wca_s):
        @pl.when(pl.program_id(0) == 0)
        def _():
            _assemble_cols(wsb_s, wsb_ref)
            _assemble_cols(wca_s, wca_ref)

        merged = _merge(osb_ref[...], oca_ref[...], lg_ref[...], bg_ref[...], wsb_s[...], wca_s[...])[4]
        u_ref[...] = ALPHA * x_ref[...] + _dot(merged.astype(BF16), wout_ref[...])

    rows = lambda w: pl.BlockSpec((tm, w), lambda i: (i, 0))
    whole = lambda shape: pl.BlockSpec(shape, lambda i: (0,) * len(shape))
    return pl.pallas_call(
        body, name="mix_fwd", grid=(s // tm,),
        in_specs=[rows(D_MODEL), rows(ATT_WIDTH), rows(ATT_WIDTH), rows(GATE_COLS), whole((1, GATE_COLS)),
                  whole((N_DEV, ATT_WIDTH, PROJ_SHARD)), whole((N_DEV, ATT_WIDTH, PROJ_SHARD)),
                  whole((D_MODEL, D_MODEL))],
        out_specs=rows(D_MODEL),
        out_shape=jax.ShapeDtypeStruct((s, D_MODEL), F32),
        scratch_shapes=[pltpu.VMEM((ATT_WIDTH, D_MODEL), BF16), pltpu.VMEM((ATT_WIDTH, D_MODEL), BF16)],
        compiler_params=_params(("arbitrary",)),
    )(x, o_sb, o_ca, lg, _row(b_gate), wg_sb, wg_ca, w_out)


def mix_bwd(du1, o_sb, o_ca, lg, b_gate, wg_sb, wg_ca, w_out, tm=256):
    s = du1.shape[0]

    def body(du_ref, osb_ref, oca_ref, lg_ref, bg_ref, wsb_ref, wca_ref, wout_ref,
             dlg_ref, dosb_ref, doca_ref, dysb_ref, dyca_ref, mg_ref, dbg_ref, wsb_s, wca_s):
        @pl.when(pl.program_id(0) == 0)
        def _():
            _assemble_cols(wsb_s, wsb_ref)
            _assemble_cols(wca_s, wca_ref)
            dbg_ref[...] = jnp.zeros_like(dbg_ref)

        y_sb, y_ca, g_sb, g_ca, merged = _merge(
            osb_ref[...], oca_ref[...], lg_ref[...], bg_ref[...], wsb_s[...], wca_s[...])
        mg_ref[...] = merged.astype(BF16)
        dm = _dot_nt(du_ref[...].astype(BF16), wout_ref[...])
        dl_sb = dm * y_sb * g_sb * (1.0 - g_sb)
        dl_ca = dm * y_ca * g_ca * (1.0 - g_ca)
        dlg_ref[:, :D_MODEL] = dl_sb.astype(BF16)
        dlg_ref[:, D_MODEL:] = dl_ca.astype(BF16)
        dbg_ref[:, :D_MODEL] += jnp.sum(dl_sb, axis=0, keepdims=True)
        dbg_ref[:, D_MODEL:] += jnp.sum(dl_ca, axis=0, keepdims=True)
        dy_sb = (dm * g_sb).astype(BF16)
        dy_ca = (dm * g_ca).astype(BF16)
        dysb_ref[...] = dy_sb
        dyca_ref[...] = dy_ca
        dosb_ref[...] = _dot_nt(dy_sb, wsb_s[...]).astype(BF16)
        doca_ref[...] = _dot_nt(dy_ca, wca_s[...]).astype(BF16)

    rows = lambda w: pl.BlockSpec((tm, w), lambda i: (i, 0))
    whole = lambda shape: pl.BlockSpec(shape, lambda i: (0,) * len(shape))
    return pl.pallas_call(
        body, name="mix_bwd", grid=(s // tm,),
        in_specs=[rows(D_MODEL), rows(ATT_WIDTH), rows(ATT_WIDTH), rows(GATE_COLS), whole((1, GATE_COLS)),
                  whole((N_DEV, ATT_WIDTH, PROJ_SHARD)), whole((N_DEV, ATT_WIDTH, PROJ_SHARD)),
                  whole((D_MODEL, D_MODEL))],
        out_specs=[rows(GATE_COLS), rows(ATT_WIDTH), rows(ATT_WIDTH), rows(D_MODEL), rows(D_MODEL),
                   rows(D_MODEL), whole((1, GATE_COLS))],
        out_shape=[jax.ShapeDtypeStruct((s, GATE_COLS), BF16),
                   jax.ShapeDtypeStruct((s, ATT_WIDTH), BF16), jax.ShapeDtypeStruct((s, ATT_WIDTH), BF16),
                   jax.ShapeDtypeStruct((s, D_MODEL), BF16), jax.ShapeDtypeStruct((s, D_MODEL), BF16),
                   jax.ShapeDtypeStruct((s, D_MODEL), BF16), jax.ShapeDtypeStruct((1, GATE_COLS), F32)],
        scratch_shapes=[pltpu.VMEM((ATT_WIDTH, D_MODEL), BF16), pltpu.VMEM((ATT_WIDTH, D_MODEL), BF16)],
        compiler_params=_params(("arbitrary",)),
    )(du1, o_sb, o_ca, lg, _row(b_gate), wg_sb, wg_ca, w_out)


def _ln_stats(u):
    mu = jnp.mean(u, axis=1, keepdims=True)
    cen = u - mu
    var = jnp.mean(cen * cen, axis=1, keepdims=True)
    rstd = lax.rsqrt(var + LN_EPS)
    return cen * rstd, rstd


def _ln_bwd(dy, xhat, rstd, gain):
    dyg = dy * gain
    m1 = jnp.mean(dyg, axis=1, keepdims=True)
    m2 = jnp.mean(dyg * xhat, axis=1, keepdims=True)
    return rstd * (dyg - m1 - xhat * m2)


def mlp_fwd(u1, target, wg_in, wg_out, ln1_g, ln1_b, ln2_g, ln2_b, tm=512):
    s = u1.shape[0]
    nk = N_DEV

    def body(u_ref, t_ref, win_ref, wout_ref, g1_ref, b1_ref, g2_ref, b2_ref,
             hm_ref, a_ref, x1b_ref, du2_ref, loss_ref, dg2_ref, db2_ref, x1_s, acc_s):
        i, k = pl.program_id(0), pl.program_id(1)

        @pl.when(jnp.logical_and(i == 0, k == 0))
        def _():
            loss_ref[...] = jnp.zeros_like(loss_ref)
            dg2_ref[...] = jnp.zeros_like(dg2_ref)
            db2_ref[...] = jnp.zeros_like(db2_ref)

        @pl.when(k == 0)
        def _():
            xhat, _ = _ln_stats(u_ref[...])
            x1 = xhat * g1_ref[...] + b1_ref[...]
            x1_s[...] = x1
            x1b_ref[...] = x1.astype(BF16)
            acc_s[...] = jnp.zeros_like(acc_s)

        hm = _dot(x1b_ref[...], win_ref[0])
        hm_ref[...] = hm
        r = jnp.maximum(hm, 0.0)
        a = (r * r).astype(BF16)
        a_ref[...] = a
        acc_s[...] += _dot(a, wout_ref[0])

        @pl.when(k == nk - 1)
        def _():
            xhat, rstd = _ln_stats(ALPHA * x1_s[...] + acc_s[...])
            diff = xhat * g2_ref[...] + b2_ref[...] - t_ref[...]
            per_token = jnp.mean(diff * diff, axis=1, keepdims=True)
            loss_ref[...] += 0.5 * jnp.sum(per_token, axis=0, keepdims=True)
            dy = diff * (1.0 / D_MODEL)
            dg2_ref[...] += jnp.sum(dy * xhat, axis=0, keepdims=True)
            db2_ref[...] += jnp.sum(dy, axis=0, keepdims=True)
            du2_ref[...] = _ln_bwd(dy, xhat, rstd, g2_ref[...])

    rows = pl.BlockSpec((tm, D_MODEL), lambda i, k: (i, 0))
    vec = pl.BlockSpec((1, D_MODEL), lambda i, k: (0, 0))
    ff = pl.BlockSpec((tm, FF_SHARD), lambda i, k: (i, k))
    return pl.pallas_call(
        body, name="mlp_fwd", grid=(s // tm, nk),
        in_specs=[rows, rows,
                  pl.BlockSpec((1, D_MODEL, FF_SHARD), lambda i, k: (k, 0, 0)),
                  pl.BlockSpec((1, FF_SHARD, D_MODEL), lambda i, k: (k, 0, 0)),
                  vec, vec, vec, vec],
        out_specs=[ff, ff, rows, rows, pl.BlockSpec((1, 1), lambda i, k: (0, 0)), vec, vec],
        out_shape=[jax.ShapeDtypeStruct((s, D_FF), F32), jax.ShapeDtypeStruct((s, D_FF), BF16),
                   jax.ShapeDtypeStruct((s, D_MODEL), BF16), jax.ShapeDtypeStruct((s, D_MODEL), F32),
                   jax.ShapeDtypeStruct((1, 1), F32),
                   jax.ShapeDtypeStruct((1, D_MODEL), F32), jax.ShapeDtypeStruct((1, D_MODEL), F32)],
        scratch_shapes=[pltpu.VMEM((tm, D_MODEL), F32), pltpu.VMEM((tm, D_MODEL), F32)],
        compiler_params=_params(("arbitrary", "arbitrary")),
    )(u1, target, wg_in, wg_out, _row(ln1_g), _row(ln1_b), _row(ln2_g), _row(ln2_b))


def mlp_bwd(du2, hm, u1, wg_in, wg_out, ln1_g, tm=512):
    s = du2.shape[0]
    nk = N_DEV

    def body(du2_ref, hm_ref, u_ref, win_ref, wout_ref, g1_ref,
             dhm_ref, du1_ref, dg1_ref, db1_ref, du2b_s, acc_s):
        i, k = pl.program_id(0), pl.program_id(1)

        @pl.when(jnp.logical_and(i == 0, k == 0))
        def _():
            dg1_ref[...] = jnp.zeros_like(dg1_ref)
            db1_ref[...] = jnp.zeros_like(db1_ref)

        @pl.when(k == 0)
        def _():
            du2b_s[...] = du2_ref[...].astype(BF16)
            acc_s[...] = jnp.zeros_like(acc_s)

        da = _dot_nt(du2b_s[...], wout_ref[0])
        dhm = (2.0 * jnp.maximum(hm_ref[...], 0.0) * da).astype(BF16)
        dhm_ref[...] = dhm
        acc_s[...] += _dot_nt(dhm, win_ref[0])

        @pl.when(k == nk - 1)
        def _():
            dx1 = ALPHA * du2_ref[...] + acc_s[...]
            xhat, rstd = _ln_stats(u_ref[...])
            dg1_ref[...] += jnp.sum(dx1 * xhat, axis=0, keepdims=True)
            db1_ref[...] += jnp.sum(dx1, axis=0, keepdims=True)
            du1_ref[...] = _ln_bwd(dx1, xhat, rstd, g1_ref[...])

    rows = pl.BlockSpec((tm, D_MODEL), lambda i, k: (i, 0))
    vec = pl.BlockSpec((1, D_MODEL), lambda i, k: (0, 0))
    ff = pl.BlockSpec((tm, FF_SHARD), lambda i, k: (i, k))
    return pl.pallas_call(
        body, name="mlp_bwd", grid=(s // tm, nk),
        in_specs=[rows, ff, rows,
                  pl.BlockSpec((1, D_MODEL, FF_SHARD), lambda i, k: (k, 0, 0)),
                  pl.BlockSpec((1, FF_SHARD, D_MODEL), lambda i, k: (k, 0, 0)),
                  vec],
        out_specs=[ff, rows, vec, vec],
        out_shape=[jax.ShapeDtypeStruct((s, D_FF), BF16), jax.ShapeDtypeStruct((s, D_MODEL), F32),
                   jax.ShapeDtypeStruct((1, D_MODEL), F32), jax.ShapeDtypeStruct((1, D_MODEL), F32)],
        scratch_shapes=[pltpu.VMEM((tm, D_MODEL), BF16), pltpu.VMEM((tm, D_MODEL), F32)],
        compiler_params=_params(("arbitrary", "arbitrary")),
    )(du2, hm, u1, wg_in, wg_out, _row(ln1_g))


def in_bwd(du1, pieces, wg_in, tm=512):
    s = du1.shape[0]
    widths = [p.shape[1] for p in pieces]
    assert sum(widths) == IN_COLS
    n = len(pieces)

    def body(*refs):
        du_ref, piece_refs, w_ref = refs[0], refs[1:1 + n], refs[1 + n]
        gx_ref, dh_ref = refs[2 + n], refs[3 + n]
        off = 0
        for ref, w in zip(piece_refs, widths):
            dh_ref[:, off:off + w] = ref[...].astype(BF16)
            off += w
        acc = ALPHA * du_ref[...]
        for j in range(N_DEV):
            acc = acc + _dot_nt(dh_ref[:, IN_SHARD * j:IN_SHARD * (j + 1)], w_ref[j])
        gx_ref[...] = acc

    rows = lambda w: pl.BlockSpec((tm, w), lambda i: (i, 0))
    return pl.pallas_call(
        body, name="in_bwd", grid=(s // tm,),
        in_specs=[rows(D_MODEL)] + [rows(w) for w in widths]
                 + [pl.BlockSpec((N_DEV, D_MODEL, IN_SHARD), lambda i: (0, 0, 0))],
        out_specs=[rows(D_MODEL), rows(IN_COLS)],
        out_shape=[jax.ShapeDtypeStruct((s, D_MODEL), F32), jax.ShapeDtypeStruct((s, IN_COLS), BF16)],
        compiler_params=_params(("arbitrary",)),
    )(du1, *pieces, wg_in)


def wgrad(name, a, b, tm, tn, shard_cols=None, tk=512):
    kk, m = a.shape
    n = b.shape[1]
    nk = kk // tk
    w = shard_cols

    def body(a_ref, b_ref, out_ref, acc_s):
        k = pl.program_id(2)

        @pl.when(k == 0)
        def _():
            acc_s[...] = jnp.zeros_like(acc_s)

        acc_s[...] += _dot_tn(a_ref[...].astype(BF16), b_ref[...].astype(BF16))

        @pl.when(k == nk - 1)
        def _():
            if w is None:
                out_ref[...] = acc_s[...]
            else:
                for j in range(tn // w):
                    out_ref[j] = acc_s[:, w * j:w * (j + 1)]

    if w is None:
        out_spec = pl.BlockSpec((tm, tn), lambda i, j, k: (i, j))
        out_shape = jax.ShapeDtypeStruct((m, n), F32)
    else:
        out_spec = pl.BlockSpec((tn // w, tm, w), lambda i, j, k: (j, i, 0))
        out_shape = jax.ShapeDtypeStruct((n // w, m, w), F32)
    return pl.pallas_call(
        body, name=name, grid=(m // tm, n // tn, nk),
        in_specs=[pl.BlockSpec((tk, tm), lambda i, j, k: (k, i)),
                  pl.BlockSpec((tk, tn), lambda i, j, k: (k, j))],
        out_specs=out_spec, out_shape=out_shape,
        scratch_shapes=[pltpu.VMEM((tm, tn), F32)],
        compiler_params=_params(("arbitrary", "arbitrary", "arbitrary")),
    )(a, b)


def adamw(name, w, m, v, parts, tm=None):
    rows, cols = w.shape
    tm = rows if tm is None else min(tm, rows)
    n = len(parts)
    c1 = 1.0 - ADAM_B1 ** ADAM_STEP
    c2 = 1.0 - ADAM_B2 ** ADAM_STEP

    def body(*refs):
        w_ref, m_ref, v_ref = refs[:3]
        part_refs = refs[3:3 + n]
        g_ref, d_ref, nm_ref, nv_ref = refs[3 + n:]
        g = None
        for ref, (_, index) in zip(part_refs, parts):
            term = (ref[...] if index is None else ref[0]).astype(F32)
            g = term if g is None else g + term
        new_m = ADAM_B1 * m_ref[...] + (1.0 - ADAM_B1) * g
        new_v = ADAM_B2 * v_ref[...] + (1.0 - ADAM_B2) * (g * g)
        m_hat = new_m / c1
        v_hat = new_v / c2
        g_ref[...] = g
        d_ref[...] = -ADAM_LR * (m_hat / (jnp.sqrt(v_hat) + ADAM_EPS) + ADAM_WD * w_ref[...])
        nm_ref[...] = new_m
        nv_ref[...] = new_v

    spec = pl.BlockSpec((tm, cols), lambda i: (i, 0))
    shape = jax.ShapeDtypeStruct((rows, cols), F32)

    def part_spec(index):
        if index is None:
            return spec
        return pl.BlockSpec((1, tm, cols), lambda i: (index, i, 0))

    return pl.pallas_call(
        body, name=name, grid=(rows // tm,),
        in_specs=[spec] * 3 + [part_spec(index) for _, index in parts],
        out_specs=[spec] * 4, out_shape=[shape] * 4,
        compiler_params=_params(("arbitrary",)),
    )(w, m, v, *[array for array, _ in parts])


def local_grads(x, target, wg_in, wg_sb, wg_ca, w_out, wg_mi, wg_mo, b_gate, rel_bias,
                ln1_g, ln1_b, ln2_g, ln2_b):
    qkv, lg, xb = in_proj(x, wg_in)
    bias_tiles = ca_bias_tiles(rel_bias)
    o_sb = sb_fwd(qkv)
    o_ca = ca_fwd(qkv, bias_tiles)
    u1 = mix_fwd(x, o_sb, o_ca, lg, b_gate, wg_sb, wg_ca, w_out)
    hm, a, x1b, du2, loss, dg2, db2 = mlp_fwd(u1, target, wg_mi, wg_mo, ln1_g, ln1_b, ln2_g, ln2_b)
    dhm, du1, dg1, db1 = mlp_bwd(du2, hm, u1, wg_mi, wg_mo, ln1_g)
    g_mi = wgrad("wgrad_mlp_in", x1b, dhm, tm=D_MODEL, tn=FF_SHARD, shard_cols=FF_SHARD)
    g_mo = wgrad("wgrad_mlp_out", a, du2, tm=FF_SHARD, tn=D_MODEL)
    dlg, do_sb, do_ca, dy_sb, dy_ca, merged, dbg = mix_bwd(du1, o_sb, o_ca, lg, b_gate, wg_sb, wg_ca, w_out)
    g_out = wgrad("wgrad_out", merged, du1, tm=D_MODEL, tn=D_MODEL)
    g_sb = wgrad("wgrad_sb_proj", o_sb, dy_sb, tm=ATT_WIDTH, tn=D_MODEL, shard_cols=PROJ_SHARD)
    g_ca = wgrad("wgrad_ca_proj", o_ca, dy_ca, tm=ATT_WIDTH, tn=D_MODEL, shard_cols=PROJ_SHARD)
    dq_sb, dk_sb, dv_sb = sb_bwd(qkv, o_sb, do_sb)
    dq_ca, dk_ca, dv_ca, db_tiles = ca_bwd(qkv, bias_tiles, do_ca)
    g_rel = rel_bias_grad(db_tiles)
    grad_x, dh = in_bwd(du1, [dq_sb, dk_sb, dv_sb, dq_ca, dk_ca, dv_ca, dlg], wg_in)
    g_in = wgrad("wgrad_in", xb, dh, tm=D_MODEL, tn=IN_SHARD, shard_cols=IN_SHARD)
    big = dict(w_in=g_in, w_sb_proj=g_sb, w_ca_proj=g_ca, w_out=g_out, w_mlp_in=g_mi, w_mlp_out=g_mo)
    small = dict(b_gate=dbg, rel_bias=g_rel, ln1_g=dg1, ln1_b=db1, ln2_g=dg2, ln2_b=db2)
    return loss, grad_x, big, small


HBM_SPEC = pl.BlockSpec(memory_space=pl.ANY)


def _place():
    x, y, c = lax.axis_index("x"), lax.axis_index("y"), lax.axis_index("c")
    other_chips = [(1 - x, y), (x, 1 - y), (1 - x, 1 - y)]
    return x, y, c, other_chips


def all_gather(name, arrays):
    n = len(arrays)

    def body(*refs):
        ins, outs = refs[:n], refs[n:2 * n]
        send_sems, recv_sems, local_sems = refs[2 * n:]
        x, y, c, chips = _place()
        me = 4 * x + 2 * y + c
        sibling = (x, y, 1 - c)

        def copy(a, k, block, to, src=None):
            return pltpu.make_async_remote_copy(
                src_ref=outs[a].at[block] if src is None else src, dst_ref=outs[a].at[block],
                send_sem=send_sems.at[a, k], recv_sem=recv_sems.at[a, k],
                device_id=to, device_id_type=MESH)

        started = []
        local = [pltpu.make_async_copy(ins[a], outs[a].at[me], local_sems.at[a]) for a in range(n)]
        for a in range(n):
            local[a].start()
            sends = [copy(a, 0, me, sibling, src=ins[a])]
            sends += [copy(a, 1 + j, me, (cx, cy, c), src=ins[a]) for j, (cx, cy) in enumerate(chips)]
            for cp in sends:
                cp.start()
            started += sends
        for a in range(n):
            for j, (cx, cy) in enumerate(chips):
                block = 4 * cx + 2 * cy + c
                copy(a, 1 + j, block, sibling).wait_recv()
                passed = copy(a, 4 + j, block, sibling)
                passed.start()
                started.append(passed)
        for a in range(n):
            copy(a, 0, 4 * x + 2 * y + (1 - c), sibling).wait_recv()
            for j, (cx, cy) in enumerate(chips):
                copy(a, 4 + j, 4 * cx + 2 * cy + (1 - c), sibling).wait_recv()
        for cp in started:
            cp.wait_send()
        for cp in local:
            cp.wait()

    return pl.pallas_call(
        body, name=name,
        in_specs=[HBM_SPEC] * n, out_specs=[HBM_SPEC] * n,
        out_shape=[jax.ShapeDtypeStruct((N_DEV,) + a.shape, a.dtype) for a in arrays],
        scratch_shapes=[pltpu.SemaphoreType.DMA((n, 7)), pltpu.SemaphoreType.DMA((n, 7)),
                        pltpu.SemaphoreType.DMA((n,))],
    )(*arrays)


def sibling_exchange(name, grads):
    n = len(grads)

    def body(*refs):
        ins, outs = refs[:n], refs[n:2 * n]
        send_sems, recv_sems = refs[2 * n:]
        x, y, c, _ = _place()
        copies = [pltpu.make_async_remote_copy(
            src_ref=ins[a].at[:, 1 - c], dst_ref=outs[a], send_sem=send_sems.at[a], recv_sem=recv_sems.at[a],
            device_id=(x, y, 1 - c), device_id_type=MESH) for a in range(n)]
        for cp in copies:
            cp.start()
        for cp in copies:
            cp.wait()

    return pl.pallas_call(
        body, name=name,
        in_specs=[HBM_SPEC] * n, out_specs=[HBM_SPEC] * n,
        out_shape=[jax.ShapeDtypeStruct((4,) + g.shape[2:], g.dtype) for g in grads],
        scratch_shapes=[pltpu.SemaphoreType.DMA((n,)), pltpu.SemaphoreType.DMA((n,))],
    )(*grads)


def chip_sum(name, grad, landed, place, tr=256):
    _, _, rows, cols = grad.shape
    tr = min(tr, rows)

    def body(place_ref, g_ref, l_ref, own_ref, out_ref):
        total = g_ref[0, 0] + l_ref[0]
        out_ref[0] = total.astype(BF16)

        @pl.when(pl.program_id(1) == place_ref[1])
        def _():
            own_ref[...] = total

    return pl.pallas_call(
        body, name=name,
        grid_spec=pltpu.PrefetchScalarGridSpec(
            num_scalar_prefetch=1, grid=(rows // tr, 4),
            in_specs=[pl.BlockSpec((1, 1, tr, cols), lambda i, ch, pos: (ch, pos[0], i, 0)),
                      pl.BlockSpec((1, tr, cols), lambda i, ch, pos: (ch, i, 0))],
            out_specs=[pl.BlockSpec((tr, cols), lambda i, ch, pos: (i, 0)),
                       pl.BlockSpec((1, tr, cols), lambda i, ch, pos: (ch, i, 0))]),
        out_shape=[jax.ShapeDtypeStruct((rows, cols), F32), jax.ShapeDtypeStruct((4, rows, cols), BF16)],
        compiler_params=_params(("arbitrary", "arbitrary")),
    )(place, grad, landed)


def chip_exchange(name, sums):
    n = len(sums)

    def body(*refs):
        ins, outs = refs[:n], refs[n:2 * n]
        send_sems, recv_sems = refs[2 * n:]
        _, _, c, chips = _place()
        copies = [pltpu.make_async_remote_copy(
            src_ref=ins[a].at[2 * cx + cy], dst_ref=outs[a].at[j],
            send_sem=send_sems.at[a, j], recv_sem=recv_sems.at[a, j],
            device_id=(cx, cy, c), device_id_type=MESH)
            for a in range(n) for j, (cx, cy) in enumerate(chips)]
        for cp in copies:
            cp.start()
        for cp in copies:
            cp.wait()

    return pl.pallas_call(
        body, name=name,
        in_specs=[HBM_SPEC] * n, out_specs=[HBM_SPEC] * n,
        out_shape=[jax.ShapeDtypeStruct((3,) + s.shape[1:], s.dtype) for s in sums],
        scratch_shapes=[pltpu.SemaphoreType.DMA((n, 3)), pltpu.SemaphoreType.DMA((n, 3))],
    )(*sums)


BIG = ("w_in", "w_sb_proj", "w_ca_proj", "w_out", "w_mlp_in", "w_mlp_out")
SMALL = ("b_gate", "rel_bias", "ln1_g", "ln1_b", "ln2_g", "ln2_b")
NAMES = ("w_in", "b_gate", "w_sb_proj", "w_ca_proj", "rel_bias", "w_out", "ln1_g", "ln1_b",
         "w_mlp_in", "w_mlp_out", "ln2_g", "ln2_b")
REL_COLS = 2 * REL_CLIP + 1


def _pack_small(t):
    rel = t["rel_bias"]
    if rel.shape[1] != REL_PAD:
        rel = jnp.pad(rel, ((0, 0), (0, REL_PAD - rel.shape[1])))
    flat = [t["b_gate"].reshape(-1), rel.reshape(-1)] + [t[n].reshape(-1) for n in SMALL[2:]]
    return jnp.concatenate(flat).reshape(-1, 128)


def _unpack_small(p):
    flat = p.reshape(-1)
    out, off = {}, 0
    for n, size in (("b_gate", GATE_COLS), ("rel_bias", 8 * REL_PAD)) + tuple((n, D_MODEL) for n in SMALL[2:]):
        out[n] = flat[off:off + size]
        off += size
    out["rel_bias"] = out["rel_bias"].reshape(8, REL_PAD)[:, :REL_COLS]
    return out


def kernel(x, w_in, b_gate, w_sb_proj, w_ca_proj, rel_bias, w_out, ln1_g, ln1_b, w_mlp_in, w_mlp_out, ln2_g, ln2_b, loss_target, m_w_in, m_b_gate, m_w_sb_proj, m_w_ca_proj, m_rel_bias, m_w_out, m_ln1_g, m_ln1_b, m_w_mlp_in, m_w_mlp_out, m_ln2_g, m_ln2_b, v_w_in, v_b_gate, v_w_sb_proj, v_w_ca_proj, v_rel_bias, v_w_out, v_ln1_g, v_ln1_b, v_w_mlp_in, v_w_mlp_out, v_ln2_g, v_ln2_b):
    w = dict(w_in=w_in, b_gate=b_gate, w_sb_proj=w_sb_proj, w_ca_proj=w_ca_proj, rel_bias=rel_bias, w_out=w_out,
             ln1_g=ln1_g, ln1_b=ln1_b, w_mlp_in=w_mlp_in, w_mlp_out=w_mlp_out, ln2_g=ln2_g, ln2_b=ln2_b)
    m = dict(w_in=m_w_in, b_gate=m_b_gate, w_sb_proj=m_w_sb_proj, w_ca_proj=m_w_ca_proj, rel_bias=m_rel_bias,
             w_out=m_w_out, ln1_g=m_ln1_g, ln1_b=m_ln1_b, w_mlp_in=m_w_mlp_in, w_mlp_out=m_w_mlp_out,
             ln2_g=m_ln2_g, ln2_b=m_ln2_b)
    v = dict(w_in=v_w_in, b_gate=v_b_gate, w_sb_proj=v_w_sb_proj, w_ca_proj=v_w_ca_proj, rel_bias=v_rel_bias,
             w_out=v_w_out, ln1_g=v_ln1_g, ln1_b=v_ln1_b, w_mlp_in=v_w_mlp_in, w_mlp_out=v_w_mlp_out,
             ln2_g=v_ln2_g, ln2_b=v_ln2_b)

    gathered = dict(zip(BIG, all_gather("gather_weights", [w[n].astype(BF16) for n in BIG])))
    loss, grad_x, big, small = local_grads(
        x[0], loss_target[0], gathered["w_in"], gathered["w_sb_proj"], gathered["w_ca_proj"],
        gathered["w_out"].reshape(D_MODEL, D_MODEL), gathered["w_mlp_in"], gathered["w_mlp_out"],
        b_gate, rel_bias, ln1_g, ln1_b, ln2_g, ln2_b)
    loss = lax.psum(loss[0, 0], ("x", "y", "c"))

    c = lax.axis_index("c")
    place = jnp.stack([c, 2 * lax.axis_index("x") + lax.axis_index("y")]).astype(jnp.int32)
    by_owner = [big[n].reshape((4, 2) + w[n].shape) for n in BIG]
    landed = sibling_exchange("reduce_siblings", by_owner)
    own, sums = zip(*[chip_sum("chip_sum_" + n, g, l, place) for n, g, l in zip(BIG, by_owner, landed)])
    arrived = chip_exchange("reduce_chips", list(sums))

    grads, deltas, new_m, new_v = {}, {}, {}, {}
    for n, o, r in zip(BIG, own, arrived):
        grads[n], deltas[n], new_m[n], new_v[n] = adamw(
            "adamw_" + n, w[n], m[n], v[n], [(o, None), (r, 0), (r, 1), (r, 2)], tm=256)

    parts = all_gather("gather_small_grads", [_pack_small(small)])[0]
    packed = adamw("adamw_small", _pack_small(w), _pack_small(m), _pack_small(v),
                   [(parts, d) for d in range(N_DEV)])
    for out, p in zip((grads, deltas, new_m, new_v), packed):
        for n, val in _unpack_small(p).items():
            out[n] = val.reshape(w[n].shape)

    return (loss, grad_x[None], *[grads[n] for n in NAMES], *[deltas[n] for n in NAMES],
            *[new_m[n] for n in NAMES], *[new_v[n] for n in NAMES])
```

```python
import functools

import jax
import jax.numpy as jnp
from jax import lax
from jax.experimental import pallas as pl
from jax.experimental.pallas import tpu as pltpu

F32 = jnp.float32
BF16 = jnp.bfloat16
MESH = pl.DeviceIdType.MESH

N_DEV = 8
D_MODEL = 1024
HEAD_DIM = 64
ATT_WIDTH = 512
N_PAIRS = ATT_WIDTH // 128
QKV_COLS = 6 * ATT_WIDTH
GATE_COLS = 2 * D_MODEL
IN_COLS = QKV_COLS + GATE_COLS
IN_SHARD = IN_COLS // N_DEV
D_FF = 4 * D_MODEL
FF_SHARD = D_FF // N_DEV
PROJ_SHARD = D_MODEL // N_DEV
ATT_BLOCK = 128
CA_TILES = 5
CA_ROWS = 512
CHUNK = 64
CA_PREV_CHUNKS = 8
REL_CLIP = 256
REL_PAD = 640
ALPHA = 2.0 ** 0.25
LN_EPS = 1e-5
QK_SCALE = HEAD_DIM ** -0.5
NEG_BIG = -1e30
VMEM_LIMIT = 56 * 1024 * 1024

ADAM_LR = 0.001
ADAM_B1 = 0.9
ADAM_B2 = 0.999
ADAM_EPS = 1e-08
ADAM_WD = 0.01
ADAM_STEP = 10

_NT = (((1,), (1,)), ((), ()))
_TN = (((0,), (0,)), ((), ()))


def _dot(a, b):
    return jnp.dot(a, b, preferred_element_type=F32)


def _dot_nt(a, b):
    return lax.dot_general(a, b, _NT, preferred_element_type=F32)


def _dot_tn(a, b):
    return lax.dot_general(a, b, _TN, preferred_element_type=F32)


def _params(semantics=None):
    return pltpu.CompilerParams(dimension_semantics=semantics, vmem_limit_bytes=VMEM_LIMIT)


def _row(v):
    return v.reshape(1, -1)


HBM_SPEC = pl.BlockSpec(memory_space=pl.ANY)


def _behind(body, after):
    n = len(after)

    def wrapped(*refs):
        body(*refs[n:])

    return wrapped


def in_proj(x, wg_in, tm=512, after=()):
    s = x.shape[0]

    def body(x_ref, w_ref, qkv_ref, lg_ref, xb_ref):
        xb = x_ref[...].astype(BF16)
        xb_ref[...] = xb
        for j in range(N_DEV):
            acc = _dot(xb, w_ref[j])
            lo, hi = IN_SHARD * j, IN_SHARD * (j + 1)
            if hi <= QKV_COLS:
                qkv_ref[:, lo:hi] = acc.astype(BF16)
            elif lo >= QKV_COLS:
                lg_ref[:, lo - QKV_COLS:hi - QKV_COLS] = acc
            else:
                qkv_ref[:, lo:QKV_COLS] = acc[:, :QKV_COLS - lo].astype(BF16)
                lg_ref[:, 0:hi - QKV_COLS] = acc[:, QKV_COLS - lo:]

    return pl.pallas_call(
        _behind(body, after), name="in_proj", grid=(s // tm,),
        in_specs=[HBM_SPEC] * len(after) + [pl.BlockSpec((tm, D_MODEL), lambda i: (i, 0)),
                  pl.BlockSpec((N_DEV, D_MODEL, IN_SHARD), lambda i: (0, 0, 0))],
        out_specs=[pl.BlockSpec((tm, QKV_COLS), lambda i: (i, 0)),
                   pl.BlockSpec((tm, GATE_COLS), lambda i: (i, 0)),
                   pl.BlockSpec((tm, D_MODEL), lambda i: (i, 0))],
        out_shape=[jax.ShapeDtypeStruct((s, QKV_COLS), BF16),
                   jax.ShapeDtypeStruct((s, GATE_COLS), F32),
                   jax.ShapeDtypeStruct((s, D_MODEL), BF16)],
        compiler_params=_params(("arbitrary",)),
    )(*after, x, wg_in)


def _head_masks():
    lane = lax.broadcasted_iota(jnp.int32, (1, 128), 1)
    first = lane < HEAD_DIM
    return first, jnp.logical_not(first)


SB_CHUNK = 512
SB_SUB = SB_CHUNK // ATT_BLOCK


def _tri_pair():
    row = lax.broadcasted_iota(jnp.int32, (2 * ATT_BLOCK, 2 * ATT_BLOCK), 0)
    col = lax.broadcasted_iota(jnp.int32, (2 * ATT_BLOCK, 2 * ATT_BLOCK), 1)
    same = (row >= ATT_BLOCK) == (col >= ATT_BLOCK)
    return jnp.logical_and(same, row >= col).astype(BF16)


def _pair_bf16(pieces):
    zeros = jnp.zeros((ATT_BLOCK, ATT_BLOCK), BF16)
    return jnp.concatenate(
        [jnp.concatenate([zeros if p is None else p.astype(BF16) for p in row], axis=1) for row in pieces], axis=0)


def _suffix_sums(pieces, tri_pair):
    hi = [[None if p is None else p.astype(BF16) for p in row] for row in pieces]
    lo = [[None if p is None else p - h.astype(F32) for p, h in zip(row, hrow)] for row, hrow in zip(pieces, hi)]
    both = _dot(jnp.concatenate([_pair_bf16(hi), _pair_bf16(lo)], axis=0), tri_pair)
    return both[:SB_CHUNK] + both[SB_CHUNK:]


def _piece_kind(diag, r, j):
    if not diag or j < r:
        return "full"
    return "diag" if j == r else "none"


def _piece(v, r, h):
    return v[ATT_BLOCK * r:ATT_BLOCK * (r + 1), ATT_BLOCK * h:ATT_BLOCK * (h + 1)]


def _row_totals(csum, h):
    return jnp.concatenate(
        [jnp.broadcast_to(_piece(csum, r, h)[:, 0:1], (ATT_BLOCK, ATT_BLOCK)) for r in range(SB_SUB)], axis=0)


def _log_keep(z):
    return jnp.minimum(-z, 0.0) - jnp.log(1.0 + jnp.exp(jnp.minimum(z, -z)))


def _sb_weights(z, runs, diag, j, strict, tri_pair):
    keeps = []
    for r in range(SB_SUB):
        kind = _piece_kind(diag, r, j)
        row = [None if kind == "none" else _log_keep(_piece(z, r, h)) for h in range(2)]
        keeps.append([jnp.where(strict, keep, 0.0) for keep in row] if kind == "diag" else row)
    csum = _suffix_sums(keeps, tri_pair)
    weights = []
    for r in range(SB_SUB):
        kind = _piece_kind(diag, r, j)
        row = [None if kind == "none" else
               jnp.exp(_piece(z, r, h) + (_piece(csum, r, h) + runs[h][ATT_BLOCK * r:ATT_BLOCK * (r + 1)]))
               for h in range(2)]
        weights.append([jnp.where(strict, a, 0.0) for a in row] if kind == "diag" else row)
    return keeps, weights, csum


def _by_head(blk, masks):
    zero = jnp.zeros_like(blk)
    return jnp.concatenate([jnp.where(m, blk, zero) for m in masks], axis=0)


def sb_fwd(qkv):
    s = qkv.shape[0]
    nq = s // SB_CHUNK

    def body(q_ref, k_ref, v_ref, o_ref, run_s):
        i = pl.program_id(1)
        masks = _head_masks()
        q = q_ref[...] * QK_SCALE
        row = lax.broadcasted_iota(jnp.int32, (ATT_BLOCK, ATT_BLOCK), 0)
        col = lax.broadcasted_iota(jnp.int32, (ATT_BLOCK, ATT_BLOCK), 1)
        strict = col < row
        tri_pair = _tri_pair()
        run_s[...] = jnp.zeros_like(run_s)
        o_ref[...] = jnp.zeros_like(o_ref)

        def chunk(cb, diag):
            start = cb * SB_CHUNK
            runs = [run_s[0], run_s[1]]
            acc = None
            for j in reversed(range(SB_SUB)):
                keys = pl.ds(pl.multiple_of(start + ATT_BLOCK * j, ATT_BLOCK), ATT_BLOCK)
                z = _dot_nt(q, _by_head(k_ref[keys, :], masks))
                _, weights, csum = _sb_weights(z, runs, diag, j, strict, tri_pair)
                part = _dot(_pair_bf16(weights), _by_head(v_ref[keys, :], masks))
                acc = part if acc is None else acc + part
                runs = [runs[h] + _row_totals(csum, h) for h in range(2)]
            run_s[0], run_s[1] = runs
            o_ref[...] += acc

        chunk(i, True)

        def step(it, carry):
            chunk(i - 1 - it, False)
            return carry

        lax.fori_loop(0, i, step, 0)

    return pl.pallas_call(
        body, name="sb_fwd", grid=(N_PAIRS, nq),
        in_specs=[pl.BlockSpec((SB_CHUNK, 128), lambda p, i: (i, p)),
                  pl.BlockSpec((s, 128), lambda p, i: (0, N_PAIRS + p)),
                  pl.BlockSpec((s, 128), lambda p, i: (0, 2 * N_PAIRS + p))],
        out_specs=pl.BlockSpec((SB_CHUNK, 128), lambda p, i: (i, p)),
        out_shape=jax.ShapeDtypeStruct((s, ATT_WIDTH), F32),
        scratch_shapes=[pltpu.VMEM((2, SB_CHUNK, 128), F32)],
        compiler_params=_params(("arbitrary", "arbitrary")),
    )(qkv, qkv, qkv)


def sb_bwd(qkv, o_sb, do_sb, after=()):
    s = qkv.shape[0]
    nq = s // SB_CHUNK

    def body(q_ref, k_ref, v_ref, o_ref, do_ref, dq_ref, dk_ref, dv_ref, run_l_s, run_e_s, etot_s, dq_s):
        i = pl.program_id(1)

        @pl.when(i == 0)
        def _():
            dk_ref[...] = jnp.zeros_like(dk_ref)
            dv_ref[...] = jnp.zeros_like(dv_ref)

        masks = _head_masks()
        q = q_ref[...] * QK_SCALE
        do = do_ref[...]
        prod = do.astype(F32) * o_ref[...]
        for h in range(2):
            total = jnp.sum(jnp.where(masks[h], prod, 0.0), axis=1, keepdims=True)
            etot_s[h] = jnp.broadcast_to(total, (SB_CHUNK, ATT_BLOCK))
        row = lax.broadcasted_iota(jnp.int32, (ATT_BLOCK, ATT_BLOCK), 0)
        col = lax.broadcasted_iota(jnp.int32, (ATT_BLOCK, ATT_BLOCK), 1)
        strict = col < row
        tri_pair = _tri_pair()
        run_l_s[...] = jnp.zeros_like(run_l_s)
        run_e_s[...] = jnp.zeros_like(run_e_s)
        dq_s[...] = jnp.zeros_like(dq_s)

        def chunk(cb, diag):
            start = cb * SB_CHUNK
            runs_l = [run_l_s[0], run_l_s[1]]
            runs_e = [run_e_s[0], run_e_s[1]]
            e_tot = [etot_s[0], etot_s[1]]
            dq = None
            for j in reversed(range(SB_SUB)):
                keys = pl.ds(pl.multiple_of(start + ATT_BLOCK * j, ATT_BLOCK), ATT_BLOCK)
                kk = _by_head(k_ref[keys, :], masks)
                vv = _by_head(v_ref[keys, :], masks)
                z = _dot_nt(q, kk)
                da = _dot_nt(do, vv)
                keeps, weights, csum = _sb_weights(z, runs_l, diag, j, strict, tri_pair)
                ab = [[None if a is None else a.astype(BF16) for a in wrow] for wrow in weights]
                es = [[None if a is None else a.astype(F32) * _piece(da, r, h) for h, a in enumerate(arow)]
                      for r, arow in enumerate(ab)]
                esum = _suffix_sums(es, tri_pair)
                dzs = []
                for r in range(SB_SUB):
                    kind = _piece_kind(diag, r, j)
                    if kind == "none":
                        dzs.append([None, None])
                        continue
                    dz_row = []
                    for h in range(2):
                        rows = slice(ATT_BLOCK * r, ATT_BLOCK * (r + 1))
                        sig = jnp.exp(_piece(z, r, h) + keeps[r][h])
                        after = _piece(esum, r, h) + runs_e[h][rows] - es[r][h]
                        dz = es[r][h] - sig * (e_tot[h][rows] - after)
                        dz_row.append(jnp.where(strict, dz, 0.0) if kind == "diag" else dz)
                    dzs.append(dz_row)
                dz = _pair_bf16(dzs)
                part = _dot(dz, kk)
                dq = part if dq is None else dq + part
                dk = _dot_tn(dz, q)
                dv = _dot_tn(_pair_bf16(ab), do)
                dk_ref[keys, :] += jnp.where(masks[0], dk[:ATT_BLOCK], dk[ATT_BLOCK:])
                dv_ref[keys, :] += jnp.where(masks[0], dv[:ATT_BLOCK], dv[ATT_BLOCK:])
                runs_l = [runs_l[h] + _row_totals(csum, h) for h in range(2)]
                runs_e = [runs_e[h] + _row_totals(esum, h) for h in range(2)]
            run_l_s[0], run_l_s[1] = runs_l
            run_e_s[0], run_e_s[1] = runs_e
            dq_s[...] += dq

        chunk(i, True)

        def step(it, carry):
            chunk(i - 1 - it, False)
            return carry

        lax.fori_loop(0, i, step, 0)
        dq_ref[...] = (dq_s[...] * QK_SCALE).astype(BF16)

    tile = lambda p, i: (i, p)
    state = pltpu.VMEM((2, SB_CHUNK, 128), F32)
    return pl.pallas_call(
        _behind(body, after), name="sb_bwd", grid=(N_PAIRS, nq),
        in_specs=[HBM_SPEC] * len(after) + [pl.BlockSpec((SB_CHUNK, 128), tile),
                  pl.BlockSpec((s, 128), lambda p, i: (0, N_PAIRS + p)),
                  pl.BlockSpec((s, 128), lambda p, i: (0, 2 * N_PAIRS + p)),
                  pl.BlockSpec((SB_CHUNK, 128), tile),
                  pl.BlockSpec((SB_CHUNK, 128), tile)],
        out_specs=[pl.BlockSpec((SB_CHUNK, 128), tile),
                   pl.BlockSpec((s, 128), lambda p, i: (0, p)),
                   pl.BlockSpec((s, 128), lambda p, i: (0, p))],
        out_shape=[jax.ShapeDtypeStruct((s, ATT_WIDTH), BF16),
                   jax.ShapeDtypeStruct((s, ATT_WIDTH), F32),
                   jax.ShapeDtypeStruct((s, ATT_WIDTH), F32)],
        scratch_shapes=[state, state, state, pltpu.VMEM((SB_CHUNK, 128), F32)],
        compiler_params=_params(("arbitrary", "arbitrary")),
    )(*after, qkv, qkv, qkv, o_sb, do_sb)


def ca_bias_tiles(rel_bias):
    heads = rel_bias.shape[0]
    wide = 2 * ATT_BLOCK

    def body(rel_ref, out_ref):
        bins = lax.broadcasted_iota(jnp.int32, (REL_PAD, wide), 0)
        pos = lax.broadcasted_iota(jnp.int32, (REL_PAD, wide), 1)
        shift = (ATT_BLOCK - 1) - lax.broadcasted_iota(jnp.int32, (ATT_BLOCK, wide), 0)
        r = lax.broadcasted_iota(jnp.int32, (ATT_BLOCK, ATT_BLOCK), 0)
        c = lax.broadcasted_iota(jnp.int32, (ATT_BLOCK, ATT_BLOCK), 1)
        for j in range(CA_TILES):
            hot = (jnp.clip(ATT_BLOCK * (j + 1) - 1 - pos, -REL_CLIP, REL_CLIP) + REL_CLIP == bins).astype(F32)
            lines = jnp.dot(rel_ref[...], hot, preferred_element_type=F32, precision=lax.Precision.HIGHEST)
            back = 2 * j + (r >> 6) - (c >> 6)
            ok = jnp.logical_and(back >= 0, back <= CA_PREV_CHUNKS)
            for h in range(heads):
                tile = jnp.broadcast_to(lines[h:h + 1, :], (ATT_BLOCK, wide))
                for bit in range(7):
                    rolled = pltpu.roll(tile, wide - (1 << bit), 1)
                    tile = jnp.where((shift >> bit) & 1 == 1, rolled, tile)
                out_ref[h, j] = jnp.where(ok, tile[:, :ATT_BLOCK], NEG_BIG)

    padded = jnp.pad(rel_bias, ((0, 0), (0, REL_PAD - rel_bias.shape[1])))
    return pl.pallas_call(
        body, name="ca_bias_tiles",
        out_shape=jax.ShapeDtypeStruct((heads, CA_TILES, ATT_BLOCK, ATT_BLOCK), F32),
        compiler_params=_params(),
    )(padded)


CA_BLOCKS = CA_ROWS // ATT_BLOCK
CA_KEY_BLOCKS = CA_BLOCKS + CA_TILES - 1


def _ca_users(t):
    return [r for r in range(CA_BLOCKS) if 0 <= r + CA_TILES - 1 - t < CA_TILES]


def _ca_rows(v, t, r):
    at = _ca_users(t).index(r)
    return v[ATT_BLOCK * at:ATT_BLOCK * (at + 1)]


def _ca_window(step, k_ref, v_ref, masks):
    kks, vvs, inside, rows = [], [], [], []
    for t in range(CA_KEY_BLOCKS):
        block = step * CA_BLOCKS - (CA_TILES - 1) + t
        keys = pl.ds(pl.multiple_of(jnp.maximum(block, 0) * ATT_BLOCK, ATT_BLOCK), ATT_BLOCK)
        kks.append(_by_head(k_ref[keys, :], masks))
        vvs.append(_by_head(v_ref[keys, :], masks))
        inside.append(block >= 0)
        rows.append(keys)
    return kks, vvs, inside, rows


def _ca_probs(q, kks, inside, bias_ref):
    scores = [[[None] * CA_TILES for _ in range(2)] for _ in range(CA_BLOCKS)]
    for t in range(CA_KEY_BLOCKS):
        users = _ca_users(t)
        z = _dot_nt(q[ATT_BLOCK * users[0]:ATT_BLOCK * (users[-1] + 1)], kks[t])
        for r in users:
            j = r + CA_TILES - 1 - t
            for h in range(2):
                zz = _ca_rows(z, t, r)[:, ATT_BLOCK * h:ATT_BLOCK * (h + 1)] + bias_ref[h, j]
                scores[r][h][j] = jnp.where(inside[t], zz, NEG_BIG)
    probs = [[None, None] for _ in range(CA_BLOCKS)]
    for r in range(CA_BLOCKS):
        for h in range(2):
            m = jnp.max(functools.reduce(jnp.maximum, scores[r][h]), axis=1, keepdims=True)
            ex = [jnp.exp(z - m) for z in scores[r][h]]
            inv = 1.0 / jnp.sum(functools.reduce(jnp.add, ex), axis=1, keepdims=True)
            probs[r][h] = [e * inv for e in ex]
    return probs


def _ca_stack(tiles, t):
    return jnp.concatenate(
        [jnp.concatenate([tiles[r][h][r + CA_TILES - 1 - t].astype(BF16) for h in range(2)], axis=1)
         for r in _ca_users(t)], axis=0)


def ca_fwd(qkv, bias_tiles):
    s = qkv.shape[0]
    nq = s // CA_ROWS

    def body(q_ref, k_ref, v_ref, bias_ref, o_ref):
        masks = _head_masks()
        kks, vvs, inside, _ = _ca_window(pl.program_id(1), k_ref, v_ref, masks)
        probs = _ca_probs(q_ref[...] * QK_SCALE, kks, inside, bias_ref)
        out = [None] * CA_BLOCKS
        for t in range(CA_KEY_BLOCKS):
            part = _dot(_ca_stack(probs, t), vvs[t])
            for r in _ca_users(t):
                piece = _ca_rows(part, t, r)
                out[r] = piece if out[r] is None else out[r] + piece
        o_ref[...] = jnp.concatenate(out, axis=0)

    return pl.pallas_call(
        body, name="ca_fwd", grid=(N_PAIRS, nq),
        in_specs=[pl.BlockSpec((CA_ROWS, 128), lambda p, i: (i, 3 * N_PAIRS + p)),
                  pl.BlockSpec((s, 128), lambda p, i: (0, 4 * N_PAIRS + p)),
                  pl.BlockSpec((s, 128), lambda p, i: (0, 5 * N_PAIRS + p)),
                  pl.BlockSpec((2, CA_TILES, ATT_BLOCK, ATT_BLOCK), lambda p, i: (p, 0, 0, 0))],
        out_specs=pl.BlockSpec((CA_ROWS, 128), lambda p, i: (i, p)),
        out_shape=jax.ShapeDtypeStruct((s, ATT_WIDTH), F32),
        compiler_params=_params(("arbitrary", "arbitrary")),
    )(qkv, qkv, qkv, bias_tiles)


def ca_bwd(qkv, bias_tiles, do_ca, after=()):
    s = qkv.shape[0]
    nq = s // CA_ROWS

    def body(q_ref, k_ref, v_ref, bias_ref, do_ref, dq_ref, dk_ref, dv_ref, db_ref):
        big = pl.program_id(1)

        @pl.when(big == 0)
        def _():
            dk_ref[...] = jnp.zeros_like(dk_ref)
            dv_ref[...] = jnp.zeros_like(dv_ref)
            db_ref[...] = jnp.zeros_like(db_ref)

        masks = _head_masks()
        kks, vvs, inside, key_rows = _ca_window(big, k_ref, v_ref, masks)
        q = q_ref[...] * QK_SCALE
        do = do_ref[...]
        probs = _ca_probs(q, kks, inside, bias_ref)
        dps = [[[None] * CA_TILES for _ in range(2)] for _ in range(CA_BLOCKS)]
        for t in range(CA_KEY_BLOCKS):
            users = _ca_users(t)
            dp = _dot_nt(do[ATT_BLOCK * users[0]:ATT_BLOCK * (users[-1] + 1)], vvs[t])
            for r in users:
                for h in range(2):
                    dps[r][h][r + CA_TILES - 1 - t] = _ca_rows(dp, t, r)[:, ATT_BLOCK * h:ATT_BLOCK * (h + 1)]
        dss = [[None, None] for _ in range(CA_BLOCKS)]
        for r in range(CA_BLOCKS):
            for h in range(2):
                delta = jnp.sum(functools.reduce(jnp.add, [p * dp for p, dp in zip(probs[r][h], dps[r][h])]),
                                axis=1, keepdims=True)
                dss[r][h] = [p * (dp - delta) for p, dp in zip(probs[r][h], dps[r][h])]
        for h in range(2):
            for j in range(CA_TILES):
                db_ref[h, j] += functools.reduce(jnp.add, [dss[r][h][j] for r in range(CA_BLOCKS)])
        dq = [None] * CA_BLOCKS
        for t in range(CA_KEY_BLOCKS):
            users = _ca_users(t)
            rows = slice(ATT_BLOCK * users[0], ATT_BLOCK * (users[-1] + 1))
            ds = _ca_stack(dss, t)
            part = _dot(ds, kks[t])
            for r in users:
                piece = _ca_rows(part, t, r)
                dq[r] = piece if dq[r] is None else dq[r] + piece
            dk = _dot_tn(ds, q[rows])
            dv = _dot_tn(_ca_stack(probs, t), do[rows])
            dk_ref[key_rows[t], :] += jnp.where(masks[0], dk[:ATT_BLOCK], dk[ATT_BLOCK:])
            dv_ref[key_rows[t], :] += jnp.where(masks[0], dv[:ATT_BLOCK], dv[ATT_BLOCK:])
        dq_ref[...] = (jnp.concatenate(dq, axis=0) * QK_SCALE).astype(BF16)

    return pl.pallas_call(
        _behind(body, after), name="ca_bwd", grid=(N_PAIRS, nq),
        in_specs=[HBM_SPEC] * len(after) + [pl.BlockSpec((CA_ROWS, 128), lambda p, i: (i, 3 * N_PAIRS + p)),
                  pl.BlockSpec((s, 128), lambda p, i: (0, 4 * N_PAIRS + p)),
                  pl.BlockSpec((s, 128), lambda p, i: (0, 5 * N_PAIRS + p)),
                  pl.BlockSpec((2, CA_TILES, ATT_BLOCK, ATT_BLOCK), lambda p, i: (p, 0, 0, 0)),
                  pl.BlockSpec((CA_ROWS, 128), lambda p, i: (i, p))],
        out_specs=[pl.BlockSpec((CA_ROWS, 128), lambda p, i: (i, p)),
                   pl.BlockSpec((s, 128), lambda p, i: (0, p)),
                   pl.BlockSpec((s, 128), lambda p, i: (0, p)),
                   pl.BlockSpec((2, CA_TILES, ATT_BLOCK, ATT_BLOCK), lambda p, i: (p, 0, 0, 0))],
        out_shape=[jax.ShapeDtypeStruct((s, ATT_WIDTH), BF16),
                   jax.ShapeDtypeStruct((s, ATT_WIDTH), F32),
                   jax.ShapeDtypeStruct((s, ATT_WIDTH), F32),
                   jax.ShapeDtypeStruct((2 * N_PAIRS, CA_TILES, ATT_BLOCK, ATT_BLOCK), F32)],
        compiler_params=_params(("arbitrary", "arbitrary")),
    )(*after, qkv, qkv, qkv, bias_tiles, do_ca)


def rel_bias_grad(db_tiles):
    heads = db_tiles.shape[0]

    def body(db_ref, out_ref):
        row = lax.broadcasted_iota(jnp.int32, (ATT_BLOCK, ATT_BLOCK), 0)
        col = lax.broadcasted_iota(jnp.int32, (ATT_BLOCK, ATT_BLOCK), 1)
        cp = lax.broadcasted_iota(jnp.int32, (ATT_BLOCK, REL_PAD), 0)
        bins = lax.broadcasted_iota(jnp.int32, (ATT_BLOCK, REL_PAD), 1)
        total = jnp.zeros((heads, REL_PAD), F32)
        for j in range(CA_TILES):
            sums_lo, sums_up = [], []
            for h in range(heads):
                tile = db_ref[h, j]
                for bit in range(7):
                    rolled = pltpu.roll(tile, ATT_BLOCK - (1 << bit), 1)
                    tile = jnp.where((row >> bit) & 1 == 1, rolled, tile)
                was_lower = (col + row >= ATT_BLOCK) | (col == 0)
                sums_lo.append(jnp.sum(jnp.where(was_lower, tile, 0.0), axis=0, keepdims=True))
                sums_up.append(jnp.sum(jnp.where(was_lower, 0.0, tile), axis=0, keepdims=True))
            lo = jnp.concatenate(sums_lo, axis=0)
            up = jnp.concatenate(sums_up, axis=0)
            d_lo = jnp.where(cp == 0, 0, ATT_BLOCK - cp)
            hot_lo = (jnp.clip(ATT_BLOCK * j + d_lo, -REL_CLIP, REL_CLIP) + REL_CLIP == bins).astype(F32)
            hot_up = (jnp.clip(ATT_BLOCK * j - cp, -REL_CLIP, REL_CLIP) + REL_CLIP == bins).astype(F32)
            total = total + jnp.dot(lo, hot_lo, preferred_element_type=F32, precision=lax.Precision.HIGHEST)
            total = total + jnp.dot(up, hot_up, preferred_element_type=F32, precision=lax.Precision.HIGHEST)
        out_ref[...] = total

    return pl.pallas_call(
        body, name="rel_bias_grad",
        out_shape=jax.ShapeDtypeStruct((heads, REL_PAD), F32),
        compiler_params=_params(),
    )(db_tiles)


def _assemble_cols(dst_ref, src_ref):
    w = src_ref.shape[2]
    for j in range(N_DEV):
        dst_ref[:, w * j:w * (j + 1)] = src_ref[j]


def _merge(o_sb, o_ca, lg, bg, wsb, wca):
    y_sb = _dot(o_sb.astype(BF16), wsb)
    y_ca = _dot(o_ca.astype(BF16), wca)
    gates = jax.nn.sigmoid(lg + bg)
    g_sb, g_ca = gates[:, :D_MODEL], gates[:, D_MODEL:]
    return y_sb, y_ca, g_sb, g_ca, g_sb * y_sb + g_ca * y_ca


def mix_fwd(x, o_sb, o_ca, lg, b_gate, wg_sb, wg_ca, w_out, tm=256):
    s = x.shape[0]

    def body(x_ref, osb_ref, oca_ref, lg_ref, bg_ref, wsb_ref, wca_ref, wout_ref, u_ref, wsb_s, wca_s):
        @pl.when(pl.program_id(0) == 0)
        def _():
            _assemble_cols(wsb_s, wsb_ref)
            _assemble_cols(wca_s, wca_ref)

        merged = _merge(osb_ref[...], oca_ref[...], lg_ref[...], bg_ref[...], wsb_s[...], wca_s[...])[4]
        u_ref[...] = ALPHA * x_ref[...] + _dot(merged.astype(BF16), wout_ref[...])

    rows = lambda w: pl.BlockSpec((tm, w), lambda i: (i, 0))
    whole = lambda shape: pl.BlockSpec(shape, lambda i: (0,) * len(shape))
    return pl.pallas_call(
        body, name="mix_fwd", grid=(s // tm,),
        in_specs=[rows(D_MODEL), rows(ATT_WIDTH), rows(ATT_WIDTH), rows(GATE_COLS), whole((1, GATE_COLS)),
                  whole((N_DEV, ATT_WIDTH, PROJ_SHARD)), whole((N_DEV, ATT_WIDTH, PROJ_SHARD)),
                  whole((D_MODEL, D_MODEL))],
        out_specs=rows(D_MODEL),
        out_shape=jax.ShapeDtypeStruct((s, D_MODEL), F32),
        scratch_shapes=[pltpu.VMEM((ATT_WIDTH, D_MODEL), BF16), pltpu.VMEM((ATT_WIDTH, D_MODEL), BF16)],
        compiler_params=_params(("arbitrary",)),
    )(x, o_sb, o_ca, lg, _row(b_gate), wg_sb, wg_ca, w_out)


def mix_bwd(du1, o_sb, o_ca, lg, b_gate, wg_sb, wg_ca, w_out, tm=256, after=()):
    s = du1.shape[0]

    def body(du_ref, osb_ref, oca_ref, lg_ref, bg_ref, wsb_ref, wca_ref, wout_ref,
             dlg_ref, dosb_ref, doca_ref, dysb_ref, dyca_ref, mg_ref, dbg_ref, wsb_s, wca_s):
        @pl.when(pl.program_id(0) == 0)
        def _():
            _assemble_cols(wsb_s, wsb_ref)
            _assemble_cols(wca_s, wca_ref)
            dbg_ref[...] = jnp.zeros_like(dbg_ref)

        y_sb, y_ca, g_sb, g_ca, merged = _merge(
            osb_ref[...], oca_ref[...], lg_ref[...], bg_ref[...], wsb_s[...], wca_s[...])
        mg_ref[...] = merged.astype(BF16)
        dm = _dot_nt(du_ref[...].astype(BF16), wout_ref[...])
        dl_sb = dm * y_sb * g_sb * (1.0 - g_sb)
        dl_ca = dm * y_ca * g_ca * (1.0 - g_ca)
        dlg_ref[:, :D_MODEL] = dl_sb.astype(BF16)
        dlg_ref[:, D_MODEL:] = dl_ca.astype(BF16)
        dbg_ref[:, :D_MODEL] += jnp.sum(dl_sb, axis=0, keepdims=True)
        dbg_ref[:, D_MODEL:] += jnp.sum(dl_ca, axis=0, keepdims=True)
        dy_sb = (dm * g_sb).astype(BF16)
        dy_ca = (dm * g_ca).astype(BF16)
        dysb_ref[...] = dy_sb
        dyca_ref[...] = dy_ca
        dosb_ref[...] = _dot_nt(dy_sb, wsb_s[...]).astype(BF16)
        doca_ref[...] = _dot_nt(dy_ca, wca_s[...]).astype(BF16)

    rows = lambda w: pl.BlockSpec((tm, w), lambda i: (i, 0))
    whole = lambda shape: pl.BlockSpec(shape, lambda i: (0,) * len(shape))
    return pl.pallas_call(
        _behind(body, after), name="mix_bwd", grid=(s // tm,),
        in_specs=[HBM_SPEC] * len(after) + [rows(D_MODEL), rows(ATT_WIDTH), rows(ATT_WIDTH), rows(GATE_COLS), whole((1, GATE_COLS)),
                  whole((N_DEV, ATT_WIDTH, PROJ_SHARD)), whole((N_DEV, ATT_WIDTH, PROJ_SHARD)),
                  whole((D_MODEL, D_MODEL))],
        out_specs=[rows(GATE_COLS), rows(ATT_WIDTH), rows(ATT_WIDTH), rows(D_MODEL), rows(D_MODEL),
                   rows(D_MODEL), whole((1, GATE_COLS))],
        out_shape=[jax.ShapeDtypeStruct((s, GATE_COLS), BF16),
                   jax.ShapeDtypeStruct((s, ATT_WIDTH), BF16), jax.ShapeDtypeStruct((s, ATT_WIDTH), BF16),
                   jax.ShapeDtypeStruct((s, D_MODEL), BF16), jax.ShapeDtypeStruct((s, D_MODEL), BF16),
                   jax.ShapeDtypeStruct((s, D_MODEL), BF16), jax.ShapeDtypeStruct((1, GATE_COLS), F32)],
        scratch_shapes=[pltpu.VMEM((ATT_WIDTH, D_MODEL), BF16), pltpu.VMEM((ATT_WIDTH, D_MODEL), BF16)],
        compiler_params=_params(("arbitrary",)),
    )(*after, du1, o_sb, o_ca, lg, _row(b_gate), wg_sb, wg_ca, w_out)


def _ln_stats(u):
    mu = jnp.mean(u, axis=1, keepdims=True)
    cen = u - mu
    var = jnp.mean(cen * cen, axis=1, keepdims=True)
    rstd = lax.rsqrt(var + LN_EPS)
    return cen * rstd, rstd


def _ln_bwd(dy, xhat, rstd, gain):
    dyg = dy * gain
    m1 = jnp.mean(dyg, axis=1, keepdims=True)
    m2 = jnp.mean(dyg * xhat, axis=1, keepdims=True)
    return rstd * (dyg - m1 - xhat * m2)


def mlp_fwd(u1, target, wg_in, wg_out, ln1_g, ln1_b, ln2_g, ln2_b, tm=512):
    s = u1.shape[0]
    nk = N_DEV

    def body(u_ref, t_ref, win_ref, wout_ref, g1_ref, b1_ref, g2_ref, b2_ref,
             hm_ref, a_ref, x1b_ref, du2_ref, loss_ref, dg2_ref, db2_ref, x1_s, acc_s):
        i, k = pl.program_id(0), pl.program_id(1)

        @pl.when(jnp.logical_and(i == 0, k == 0))
        def _():
            loss_ref[...] = jnp.zeros_like(loss_ref)
            dg2_ref[...] = jnp.zeros_like(dg2_ref)
            db2_ref[...] = jnp.zeros_like(db2_ref)

        @pl.when(k == 0)
        def _():
            xhat, _ = _ln_stats(u_ref[...])
            x1 = xhat * g1_ref[...] + b1_ref[...]
            x1_s[...] = x1
            x1b_ref[...] = x1.astype(BF16)
            acc_s[...] = jnp.zeros_like(acc_s)

        hm = _dot(x1b_ref[...], win_ref[0])
        hm_ref[...] = hm
        r = jnp.maximum(hm, 0.0)
        a = (r * r).astype(BF16)
        a_ref[...] = a
        acc_s[...] += _dot(a, wout_ref[0])

        @pl.when(k == nk - 1)
        def _():
            xhat, rstd = _ln_stats(ALPHA * x1_s[...] + acc_s[...])
            diff = xhat * g2_ref[...] + b2_ref[...] - t_ref[...]
            per_token = jnp.mean(diff * diff, axis=1, keepdims=True)
            loss_ref[...] += 0.5 * jnp.sum(per_token, axis=0, keepdims=True)
            dy = diff * (1.0 / D_MODEL)
            dg2_ref[...] += jnp.sum(dy * xhat, axis=0, keepdims=True)
            db2_ref[...] += jnp.sum(dy, axis=0, keepdims=True)
            du2_ref[...] = _ln_bwd(dy, xhat, rstd, g2_ref[...])

    rows = pl.BlockSpec((tm, D_MODEL), lambda i, k: (i, 0))
    vec = pl.BlockSpec((1, D_MODEL), lambda i, k: (0, 0))
    ff = pl.BlockSpec((tm, FF_SHARD), lambda i, k: (i, k))
    return pl.pallas_call(
        body, name="mlp_fwd", grid=(s // tm, nk),
        in_specs=[rows, rows,
                  pl.BlockSpec((1, D_MODEL, FF_SHARD), lambda i, k: (k, 0, 0)),
                  pl.BlockSpec((1, FF_SHARD, D_MODEL), lambda i, k: (k, 0, 0)),
                  vec, vec, vec, vec],
        out_specs=[ff, ff, rows, rows, pl.BlockSpec((1, 1), lambda i, k: (0, 0)), vec, vec],
        out_shape=[jax.ShapeDtypeStruct((s, D_FF), F32), jax.ShapeDtypeStruct((s, D_FF), BF16),
                   jax.ShapeDtypeStruct((s, D_MODEL), BF16), jax.ShapeDtypeStruct((s, D_MODEL), F32),
                   jax.ShapeDtypeStruct((1, 1), F32),
                   jax.ShapeDtypeStruct((1, D_MODEL), F32), jax.ShapeDtypeStruct((1, D_MODEL), F32)],
        scratch_shapes=[pltpu.VMEM((tm, D_MODEL), F32), pltpu.VMEM((tm, D_MODEL), F32)],
        compiler_params=_params(("arbitrary", "arbitrary")),
    )(u1, target, wg_in, wg_out, _row(ln1_g), _row(ln1_b), _row(ln2_g), _row(ln2_b))


def mlp_bwd(du2, hm, u1, wg_in, wg_out, ln1_g, tm=512):
    s = du2.shape[0]
    nk = N_DEV

    def body(du2_ref, hm_ref, u_ref, win_ref, wout_ref, g1_ref,
             dhm_ref, du1_ref, dg1_ref, db1_ref, du2b_s, acc_s):
        i, k = pl.program_id(0), pl.program_id(1)

        @pl.when(jnp.logical_and(i == 0, k == 0))
        def _():
            dg1_ref[...] = jnp.zeros_like(dg1_ref)
            db1_ref[...] = jnp.zeros_like(db1_ref)

        @pl.when(k == 0)
        def _():
            du2b_s[...] = du2_ref[...].astype(BF16)
            acc_s[...] = jnp.zeros_like(acc_s)

        da = _dot_nt(du2b_s[...], wout_ref[0])
        dhm = (2.0 * jnp.maximum(hm_ref[...], 0.0) * da).astype(BF16)
        dhm_ref[...] = dhm
        acc_s[...] += _dot_nt(dhm, win_ref[0])

        @pl.when(k == nk - 1)
        def _():
            dx1 = ALPHA * du2_ref[...] + acc_s[...]
            xhat, rstd = _ln_stats(u_ref[...])
            dg1_ref[...] += jnp.sum(dx1 * xhat, axis=0, keepdims=True)
            db1_ref[...] += jnp.sum(dx1, axis=0, keepdims=True)
            du1_ref[...] = _ln_bwd(dx1, xhat, rstd, g1_ref[...])

    rows = pl.BlockSpec((tm, D_MODEL), lambda i, k: (i, 0))
    vec = pl.BlockSpec((1, D_MODEL), lambda i, k: (0, 0))
    ff = pl.BlockSpec((tm, FF_SHARD), lambda i, k: (i, k))
    return pl.pallas_call(
        body, name="mlp_bwd", grid=(s // tm, nk),
        in_specs=[rows, ff, rows,
                  pl.BlockSpec((1, D_MODEL, FF_SHARD), lambda i, k: (k, 0, 0)),
                  pl.BlockSpec((1, FF_SHARD, D_MODEL), lambda i, k: (k, 0, 0)),
                  vec],
        out_specs=[ff, rows, vec, vec],
        out_shape=[jax.ShapeDtypeStruct((s, D_FF), BF16), jax.ShapeDtypeStruct((s, D_MODEL), F32),
                   jax.ShapeDtypeStruct((1, D_MODEL), F32), jax.ShapeDtypeStruct((1, D_MODEL), F32)],
        scratch_shapes=[pltpu.VMEM((tm, D_MODEL), BF16), pltpu.VMEM((tm, D_MODEL), F32)],
        compiler_params=_params(("arbitrary", "arbitrary")),
    )(du2, hm, u1, wg_in, wg_out, _row(ln1_g))


def in_bwd(du1, pieces, wg_in, tm=512):
    s = du1.shape[0]
    widths = [p.shape[1] for p in pieces]
    assert sum(widths) == IN_COLS
    n = len(pieces)

    def body(*refs):
        du_ref, piece_refs, w_ref = refs[0], refs[1:1 + n], refs[1 + n]
        gx_ref, dh_ref = refs[2 + n], refs[3 + n]
        off = 0
        for ref, w in zip(piece_refs, widths):
            dh_ref[:, off:off + w] = ref[...].astype(BF16)
            off += w
        acc = ALPHA * du_ref[...]
        for j in range(N_DEV):
            acc = acc + _dot_nt(dh_ref[:, IN_SHARD * j:IN_SHARD * (j + 1)], w_ref[j])
        gx_ref[...] = acc

    rows = lambda w: pl.BlockSpec((tm, w), lambda i: (i, 0))
    return pl.pallas_call(
        body, name="in_bwd", grid=(s // tm,),
        in_specs=[rows(D_MODEL)] + [rows(w) for w in widths]
                 + [pl.BlockSpec((N_DEV, D_MODEL, IN_SHARD), lambda i: (0, 0, 0))],
        out_specs=[rows(D_MODEL), rows(IN_COLS)],
        out_shape=[jax.ShapeDtypeStruct((s, D_MODEL), F32), jax.ShapeDtypeStruct((s, IN_COLS), BF16)],
        compiler_params=_params(("arbitrary",)),
    )(du1, *pieces, wg_in)


def wgrad(name, a, b, tm, tn, shard_cols=None, tk=512, after=()):
    kk, m = a.shape
    n = b.shape[1]
    nk = kk // tk
    w = shard_cols

    def body(a_ref, b_ref, out_ref, acc_s):
        k = pl.program_id(2)

        @pl.when(k == 0)
        def _():
            acc_s[...] = jnp.zeros_like(acc_s)

        acc_s[...] += _dot_tn(a_ref[...].astype(BF16), b_ref[...].astype(BF16))

        @pl.when(k == nk - 1)
        def _():
            if w is None:
                out_ref[...] = acc_s[...]
            else:
                for j in range(tn // w):
                    out_ref[j] = acc_s[:, w * j:w * (j + 1)]

    if w is None:
        out_spec = pl.BlockSpec((tm, tn), lambda i, j, k: (i, j))
        out_shape = jax.ShapeDtypeStruct((m, n), F32)
    else:
        out_spec = pl.BlockSpec((tn // w, tm, w), lambda i, j, k: (j, i, 0))
        out_shape = jax.ShapeDtypeStruct((n // w, m, w), F32)
    return pl.pallas_call(
        _behind(body, after), name=name, grid=(m // tm, n // tn, nk),
        in_specs=[HBM_SPEC] * len(after) + [pl.BlockSpec((tk, tm), lambda i, j, k: (k, i)),
                  pl.BlockSpec((tk, tn), lambda i, j, k: (k, j))],
        out_specs=out_spec, out_shape=out_shape,
        scratch_shapes=[pltpu.VMEM((tm, tn), F32)],
        compiler_params=_params(("arbitrary", "arbitrary", "arbitrary")),
    )(*after, a, b)


def adamw(name, w, m, v, parts, tm=None):
    rows, cols = w.shape
    tm = rows if tm is None else min(tm, rows)
    n = len(parts)
    c1 = 1.0 - ADAM_B1 ** ADAM_STEP
    c2 = 1.0 - ADAM_B2 ** ADAM_STEP

    def body(*refs):
        w_ref, m_ref, v_ref = refs[:3]
        part_refs = refs[3:3 + n]
        g_ref, d_ref, nm_ref, nv_ref = refs[3 + n:]
        g = None
        for ref, (_, index) in zip(part_refs, parts):
            term = (ref[...] if index is None else ref[0]).astype(F32)
            g = term if g is None else g + term
        new_m = ADAM_B1 * m_ref[...] + (1.0 - ADAM_B1) * g
        new_v = ADAM_B2 * v_ref[...] + (1.0 - ADAM_B2) * (g * g)
        m_hat = new_m / c1
        v_hat = new_v / c2
        g_ref[...] = g
        d_ref[...] = -ADAM_LR * (m_hat / (jnp.sqrt(v_hat) + ADAM_EPS) + ADAM_WD * w_ref[...])
        nm_ref[...] = new_m
        nv_ref[...] = new_v

    spec = pl.BlockSpec((tm, cols), lambda i: (i, 0))
    shape = jax.ShapeDtypeStruct((rows, cols), F32)

    def part_spec(index):
        if index is None:
            return spec
        return pl.BlockSpec((1, tm, cols), lambda i: (index, i, 0))

    return pl.pallas_call(
        body, name=name, grid=(rows // tm,),
        in_specs=[spec] * 3 + [part_spec(index) for _, index in parts],
        out_specs=[spec] * 4, out_shape=[shape] * 4,
        compiler_params=_params(("arbitrary",)),
    )(w, m, v, *[array for array, _ in parts])


def _place():
    x, y, c = lax.axis_index("x"), lax.axis_index("y"), lax.axis_index("c")
    other_chips = [(1 - x, y), (x, 1 - y), (1 - x, 1 - y)]
    return x, y, c, other_chips


def all_gather(name, arrays, seeds=()):
    n = len(arrays)
    ns = len(seeds)

    def body(*refs):
        ins, seed_ins = refs[:n], refs[n:n + ns]
        outs, seed_outs = refs[n + ns:2 * n + ns], refs[2 * n + ns:2 * (n + ns)]
        send_sems, recv_sems, local_sems = refs[2 * (n + ns):]
        x, y, c, chips = _place()
        me = 4 * x + 2 * y + c
        sibling = (x, y, 1 - c)

        def copy(a, k, block, to, src=None):
            return pltpu.make_async_remote_copy(
                src_ref=outs[a].at[block] if src is None else src, dst_ref=outs[a].at[block],
                send_sem=send_sems.at[a, k], recv_sem=recv_sems.at[a, k],
                device_id=to, device_id_type=MESH)

        started = []
        local = [pltpu.make_async_copy(ins[a], outs[a].at[me], local_sems.at[a]) for a in range(n)]
        local += [pltpu.make_async_copy(seed_ins[a], seed_outs[a].at[me], local_sems.at[n + a]) for a in range(ns)]
        for cp in local[n:]:
            cp.start()
        for a in range(n):
            local[a].start()
            sends = [copy(a, 0, me, sibling, src=ins[a])]
            sends += [copy(a, 1 + j, me, (cx, cy, c), src=ins[a]) for j, (cx, cy) in enumerate(chips)]
            for cp in sends:
                cp.start()
            started += sends
        for a in range(n):
            for j, (cx, cy) in enumerate(chips):
                block = 4 * cx + 2 * cy + c
                copy(a, 1 + j, block, sibling).wait_recv()
                passed = copy(a, 4 + j, block, sibling)
                passed.start()
                started.append(passed)
        for a in range(n):
            copy(a, 0, 4 * x + 2 * y + (1 - c), sibling).wait_recv()
            for j, (cx, cy) in enumerate(chips):
                copy(a, 4 + j, 4 * cx + 2 * cy + (1 - c), sibling).wait_recv()
        for cp in started:
            cp.wait_send()
        for cp in local:
            cp.wait()

    return pl.pallas_call(
        body, name=name,
        in_specs=[HBM_SPEC] * (n + ns), out_specs=[HBM_SPEC] * (n + ns),
        out_shape=[jax.ShapeDtypeStruct((N_DEV,) + a.shape, a.dtype) for a in list(arrays) + list(seeds)],
        scratch_shapes=[pltpu.SemaphoreType.DMA((n, 7)), pltpu.SemaphoreType.DMA((n, 7)),
                        pltpu.SemaphoreType.DMA((n + ns,))],
    )(*arrays, *seeds)


SEM_SPEC = pl.BlockSpec(memory_space=pltpu.SEMAPHORE)
DATAFLOW = pltpu.SideEffectType.DATAFLOW_SIDE_EFFECTING


def _in_hbm(a):
    return pltpu.with_memory_space_constraint(a, pltpu.HBM)


def exchange_start(name, bufs, plan, n_copies, after=()):
    nb, na = len(bufs), len(after)

    def body(*refs):
        ins = refs[:nb]
        send_sems, recv_sems = refs[nb + na], refs[nb + na + 1]
        token = refs[-1]
        for k, (src, dst, peer, _) in enumerate(plan(ins)):
            pltpu.make_async_remote_copy(src_ref=src, dst_ref=dst, send_sem=send_sems.at[k], recv_sem=recv_sems.at[k],
                                         device_id=peer, device_id_type=MESH).start()
        token[...] = jnp.zeros_like(token)

    out = pl.pallas_call(
        body, name=name,
        out_shape=(pltpu.SemaphoreType.DMA((n_copies,)), pltpu.SemaphoreType.DMA((n_copies,)),
                   *[pltpu.HBM(b.shape, b.dtype) for b in bufs], jax.ShapeDtypeStruct((8, 128), F32)),
        in_specs=[pl.BlockSpec(memory_space=pltpu.HBM)] * nb + [HBM_SPEC] * na,
        out_specs=(SEM_SPEC, SEM_SPEC, *[pl.BlockSpec(memory_space=pltpu.HBM)] * nb,
                   pl.BlockSpec(memory_space=pltpu.VMEM)),
        input_output_aliases={i: 2 + i for i in range(nb)},
        compiler_params=pltpu.CompilerParams(has_side_effects=DATAFLOW),
    )(*[_in_hbm(b) for b in bufs], *after)
    return out[0], out[1], list(out[2:2 + nb]), out[-1]


def exchange_wait(name, send_sems, recv_sems, bufs, plan, after=()):
    nb, na = len(bufs), len(after)

    def body(*refs):
        ins = refs[:nb]
        send_ref, recv_ref = refs[nb], refs[nb + 1]
        for k, (src, dst, peer, landed) in enumerate(plan(ins)):
            pltpu.make_async_remote_copy(src_ref=src, dst_ref=dst, send_sem=send_ref.at[k], recv_sem=recv_ref.at[k],
                                         device_id=peer, device_id_type=MESH).wait_send()
            pltpu.make_async_remote_copy(src_ref=src, dst_ref=landed, send_sem=send_ref.at[k], recv_sem=recv_ref.at[k],
                                         device_id=peer, device_id_type=MESH).wait_recv()

    out = pl.pallas_call(
        body, name=name,
        out_shape=tuple(pltpu.HBM(b.shape, b.dtype) for b in bufs),
        in_specs=[pl.BlockSpec(memory_space=pltpu.HBM)] * nb + [SEM_SPEC, SEM_SPEC] + [HBM_SPEC] * na,
        out_specs=tuple([pl.BlockSpec(memory_space=pltpu.HBM)] * nb),
        input_output_aliases={i: i for i in range(nb)},
        compiler_params=pltpu.CompilerParams(has_side_effects=DATAFLOW),
    )(*bufs, send_sems, recv_sems, *after)
    return list(out)


def _gather_plan(n):
    def plan(bufs):
        x, y, c, _ = _place()
        me = 4 * x + 2 * y + c
        copies = []
        for a in range(n):
            for k in range(1, N_DEV):
                px, py, pc = x ^ (k >> 2), y ^ ((k >> 1) & 1), c ^ (k & 1)
                copies.append((bufs[a].at[me], bufs[a].at[me], (px, py, pc), bufs[a].at[4 * px + 2 * py + pc]))
        return copies
    return plan


def _sibling_plan(n):
    def plan(bufs):
        x, y, c, _ = _place()
        return [(bufs[a].at[:, 1 - c], bufs[n + a], (x, y, 1 - c), bufs[n + a]) for a in range(n)]
    return plan


def _chips_plan(n):
    def plan(bufs):
        _, _, c, chips = _place()
        return [(bufs[a].at[2 * cx + cy], bufs[n + a].at[j], (cx, cy, c), bufs[n + a].at[j])
                for a in range(n) for j, (cx, cy) in enumerate(chips)]
    return plan


def sibling_exchange(name, grads):
    n = len(grads)

    def body(*refs):
        ins, outs = refs[:n], refs[n:2 * n]
        send_sems, recv_sems = refs[2 * n:]
        x, y, c, _ = _place()
        copies = [pltpu.make_async_remote_copy(
            src_ref=ins[a].at[:, 1 - c], dst_ref=outs[a], send_sem=send_sems.at[a], recv_sem=recv_sems.at[a],
            device_id=(x, y, 1 - c), device_id_type=MESH) for a in range(n)]
        for cp in copies:
            cp.start()
        for cp in copies:
            cp.wait()

    return pl.pallas_call(
        body, name=name,
        in_specs=[HBM_SPEC] * n, out_specs=[HBM_SPEC] * n,
        out_shape=[jax.ShapeDtypeStruct((4,) + g.shape[2:], g.dtype) for g in grads],
        scratch_shapes=[pltpu.SemaphoreType.DMA((n,)), pltpu.SemaphoreType.DMA((n,))],
    )(*grads)


def chip_sum(name, grad, landed, place, tr=256):
    _, _, rows, cols = grad.shape
    tr = min(tr, rows)

    def body(place_ref, g_ref, l_ref, own_ref, out_ref):
        total = g_ref[0, 0] + l_ref[0]
        out_ref[0] = total.astype(BF16)

        @pl.when(pl.program_id(1) == place_ref[1])
        def _():
            own_ref[...] = total

    return pl.pallas_call(
        body, name=name,
        grid_spec=pltpu.PrefetchScalarGridSpec(
            num_scalar_prefetch=1, grid=(rows // tr, 4),
            in_specs=[pl.BlockSpec((1, 1, tr, cols), lambda i, ch, pos: (ch, pos[0], i, 0)),
                      pl.BlockSpec((1, tr, cols), lambda i, ch, pos: (ch, i, 0))],
            out_specs=[pl.BlockSpec((tr, cols), lambda i, ch, pos: (i, 0)),
                       pl.BlockSpec((1, tr, cols), lambda i, ch, pos: (ch, i, 0))]),
        out_shape=[jax.ShapeDtypeStruct((rows, cols), F32), jax.ShapeDtypeStruct((4, rows, cols), BF16)],
        compiler_params=_params(("arbitrary", "arbitrary")),
    )(place, grad, landed)


def chip_exchange(name, sums):
    n = len(sums)

    def body(*refs):
        ins, outs = refs[:n], refs[n:2 * n]
        send_sems, recv_sems = refs[2 * n:]
        _, _, c, chips = _place()
        copies = [pltpu.make_async_remote_copy(
            src_ref=ins[a].at[2 * cx + cy], dst_ref=outs[a].at[j],
            send_sem=send_sems.at[a, j], recv_sem=recv_sems.at[a, j],
            device_id=(cx, cy, c), device_id_type=MESH)
            for a in range(n) for j, (cx, cy) in enumerate(chips)]
        for cp in copies:
            cp.start()
        for cp in copies:
            cp.wait()

    return pl.pallas_call(
        body, name=name,
        in_specs=[HBM_SPEC] * n, out_specs=[HBM_SPEC] * n,
        out_shape=[jax.ShapeDtypeStruct((3,) + s.shape[1:], s.dtype) for s in sums],
        scratch_shapes=[pltpu.SemaphoreType.DMA((n, 3)), pltpu.SemaphoreType.DMA((n, 3))],
    )(*sums)


BIG = ("w_in", "w_sb_proj", "w_ca_proj", "w_out", "w_mlp_in", "w_mlp_out")
SMALL = ("b_gate", "rel_bias", "ln1_g", "ln1_b", "ln2_g", "ln2_b")
NAMES = ("w_in", "b_gate", "w_sb_proj", "w_ca_proj", "rel_bias", "w_out", "ln1_g", "ln1_b",
         "w_mlp_in", "w_mlp_out", "ln2_g", "ln2_b")
REL_COLS = 2 * REL_CLIP + 1


def _pack_small(t):
    rel = t["rel_bias"]
    if rel.shape[1] != REL_PAD:
        rel = jnp.pad(rel, ((0, 0), (0, REL_PAD - rel.shape[1])))
    flat = [t["b_gate"].reshape(-1), rel.reshape(-1)] + [t[n].reshape(-1) for n in SMALL[2:]]
    return jnp.concatenate(flat).reshape(-1, 128)


def _unpack_small(p):
    flat = p.reshape(-1)
    out, off = {}, 0
    for n, size in (("b_gate", GATE_COLS), ("rel_bias", 8 * REL_PAD)) + tuple((n, D_MODEL) for n in SMALL[2:]):
        out[n] = flat[off:off + size]
        off += size
    out["rel_bias"] = out["rel_bias"].reshape(8, REL_PAD)[:, :REL_COLS]
    return out


def kernel(x, w_in, b_gate, w_sb_proj, w_ca_proj, rel_bias, w_out, ln1_g, ln1_b, w_mlp_in, w_mlp_out, ln2_g, ln2_b, loss_target, m_w_in, m_b_gate, m_w_sb_proj, m_w_ca_proj, m_rel_bias, m_w_out, m_ln1_g, m_ln1_b, m_w_mlp_in, m_w_mlp_out, m_ln2_g, m_ln2_b, v_w_in, v_b_gate, v_w_sb_proj, v_w_ca_proj, v_rel_bias, v_w_out, v_ln1_g, v_ln1_b, v_w_mlp_in, v_w_mlp_out, v_ln2_g, v_ln2_b):
    w = dict(w_in=w_in, b_gate=b_gate, w_sb_proj=w_sb_proj, w_ca_proj=w_ca_proj, rel_bias=rel_bias, w_out=w_out,
             ln1_g=ln1_g, ln1_b=ln1_b, w_mlp_in=w_mlp_in, w_mlp_out=w_mlp_out, ln2_g=ln2_g, ln2_b=ln2_b)
    m = dict(w_in=m_w_in, b_gate=m_b_gate, w_sb_proj=m_w_sb_proj, w_ca_proj=m_w_ca_proj, rel_bias=m_rel_bias,
             w_out=m_w_out, ln1_g=m_ln1_g, ln1_b=m_ln1_b, w_mlp_in=m_w_mlp_in, w_mlp_out=m_w_mlp_out,
             ln2_g=m_ln2_g, ln2_b=m_ln2_b)
    v = dict(w_in=v_w_in, b_gate=v_b_gate, w_sb_proj=v_w_sb_proj, w_ca_proj=v_w_ca_proj, rel_bias=v_rel_bias,
             w_out=v_w_out, ln1_g=v_ln1_g, ln1_b=v_ln1_b, w_mlp_in=v_w_mlp_in, w_mlp_out=v_w_mlp_out,
             ln2_g=v_ln2_g, ln2_b=v_ln2_b)

    x, target = x[0], loss_target[0]
    c = lax.axis_index("c")
    place = jnp.stack([c, 2 * lax.axis_index("x") + lax.axis_index("y")]).astype(jnp.int32)
    grads, deltas, new_m, new_v = {}, {}, {}, {}

    def by_owner(n, g):
        return g.reshape((4, 2) + w[n].shape)

    def finish(n, own, arrived):
        grads[n], deltas[n], new_m[n], new_v[n] = adamw(
            "adamw_" + n, w[n], m[n], v[n], [(own, None), (arrived, 0), (arrived, 1), (arrived, 2)], tm=256)

    def reduce_start(tag, names, g):
        bufs = [by_owner(n, g[n]) for n in names]
        bufs += [_in_hbm(lax.empty((4,) + w[n].shape, F32)) for n in names]
        return exchange_start("reduce_%s_siblings_start" % tag, bufs, _sibling_plan(len(names)), len(names))

    def reduce_middle(tag, names, started, after):
        send, recv, bufs, _ = started
        k = len(names)
        bufs = exchange_wait("reduce_%s_siblings_wait" % tag, send, recv, bufs, _sibling_plan(k), after=after)
        own, sums = zip(*[chip_sum("chip_sum_" + n, bufs[a], bufs[k + a], place) for a, n in enumerate(names)])
        zones = [_in_hbm(lax.empty((3,) + w[n].shape, BF16)) for n in names]
        return own, exchange_start("reduce_%s_chips_start" % tag, list(sums) + zones, _chips_plan(k), 3 * k)

    def reduce_end(tag, names, own, started, after):
        send, recv, bufs, _ = started
        k = len(names)
        bufs = exchange_wait("reduce_%s_chips_wait" % tag, send, recv, bufs, _chips_plan(k), after=after)
        for a, n in enumerate(names):
            finish(n, own[a], bufs[k + a])

    rest = BIG[1:]
    wg_in, *zones = all_gather("gather_w_in", [w_in.astype(BF16)], seeds=[w[n].astype(BF16) for n in rest])
    send, recv, zones, token = exchange_start("gather_rest_start", zones, _gather_plan(len(rest)),
                                              (N_DEV - 1) * len(rest))
    qkv, lg, xb = in_proj(x, wg_in, after=(token,))
    bias_tiles = ca_bias_tiles(rel_bias)
    o_sb = sb_fwd(qkv)
    o_ca = ca_fwd(qkv, bias_tiles)
    zones = exchange_wait("gather_rest_wait", send, recv, zones, _gather_plan(len(rest)), after=(o_sb, o_ca))
    wg = dict(zip(rest, zones))
    wg_sb, wg_ca, wg_mi, wg_mo = wg["w_sb_proj"], wg["w_ca_proj"], wg["w_mlp_in"], wg["w_mlp_out"]
    w_out_b = wg["w_out"].reshape(D_MODEL, D_MODEL)

    u1 = mix_fwd(x, o_sb, o_ca, lg, b_gate, wg_sb, wg_ca, w_out_b)
    hm, a, x1b, du2, loss, dg2, db2 = mlp_fwd(u1, target, wg_mi, wg_mo, ln1_g, ln1_b, ln2_g, ln2_b)
    loss = lax.psum(loss[0, 0], ("x", "y", "c"))
    dhm, du1, dg1, db1 = mlp_bwd(du2, hm, u1, wg_mi, wg_mo, ln1_g)
    g = {}
    g["w_mlp_in"] = wgrad("wgrad_mlp_in", x1b, dhm, tm=D_MODEL, tn=FF_SHARD, shard_cols=FF_SHARD)
    g["w_mlp_out"] = wgrad("wgrad_mlp_out", a, du2, tm=FF_SHARD, tn=D_MODEL)

    mlp = ("w_mlp_in", "w_mlp_out")
    started = reduce_start("mlp", mlp, g)
    dlg, do_sb, do_ca, dy_sb, dy_ca, merged, dbg = mix_bwd(
        du1, o_sb, o_ca, lg, b_gate, wg_sb, wg_ca, w_out_b, after=(started[3],))
    own_mlp, started_mlp = reduce_middle("mlp", mlp, started, after=(dlg,))
    g["w_out"] = wgrad("wgrad_out", merged, du1, tm=D_MODEL, tn=D_MODEL, after=(started_mlp[3],))
    g["w_sb_proj"] = wgrad("wgrad_sb_proj", o_sb, dy_sb, tm=ATT_WIDTH, tn=D_MODEL, shard_cols=PROJ_SHARD)
    g["w_ca_proj"] = wgrad("wgrad_ca_proj", o_ca, dy_ca, tm=ATT_WIDTH, tn=D_MODEL, shard_cols=PROJ_SHARD)

    att = ("w_out", "w_sb_proj", "w_ca_proj")
    started = reduce_start("att", att, g)
    dq_sb, dk_sb, dv_sb = sb_bwd(qkv, o_sb, do_sb, after=(started[3],))
    own_att, started_att = reduce_middle("att", att, started, after=(dq_sb,))
    dq_ca, dk_ca, dv_ca, db_tiles = ca_bwd(qkv, bias_tiles, do_ca, after=(started_att[3],))
    g_rel = rel_bias_grad(db_tiles)
    grad_x, dh = in_bwd(du1, [dq_sb, dk_sb, dv_sb, dq_ca, dk_ca, dv_ca, dlg], wg_in)
    reduce_end("mlp", mlp, own_mlp, started_mlp, after=(grad_x,))
    reduce_end("att", att, own_att, started_att, after=(grad_x,))

    g_in = by_owner("w_in", wgrad("wgrad_in", xb, dh, tm=D_MODEL, tn=IN_SHARD, shard_cols=IN_SHARD))
    landed = sibling_exchange("reduce_in_siblings", [g_in])[0]
    own_in, sum_in = chip_sum("chip_sum_w_in", g_in, landed, place)
    finish("w_in", own_in, chip_exchange("reduce_in_chips", [sum_in])[0])
    small = dict(b_gate=dbg, rel_bias=g_rel, ln1_g=dg1, ln1_b=db1, ln2_g=dg2, ln2_b=db2)

    parts = all_gather("gather_small_grads", [_pack_small(small)])[0]
    packed = adamw("adamw_small", _pack_small(w), _pack_small(m), _pack_small(v),
                   [(parts, d) for d in range(N_DEV)])
    for out, p in zip((grads, deltas, new_m, new_v), packed):
        for n, val in _unpack_small(p).items():
            out[n] = val.reshape(w[n].shape)

    return (loss, grad_x[None], *[grads[n] for n in NAMES], *[deltas[n] for n in NAMES],
            *[new_m[n] for n in NAMES], *[new_v[n] for n in NAMES])
```

```python
import functools

import jax
import jax.numpy as jnp
from jax import lax
from jax.experimental import pallas as pl
from jax.experimental.pallas import tpu as pltpu

F32 = jnp.float32
BF16 = jnp.bfloat16
MESH = pl.DeviceIdType.MESH

N_DEV = 8
D_MODEL = 1024
HEAD_DIM = 64
ATT_WIDTH = 512
N_PAIRS = ATT_WIDTH // 128
QKV_COLS = 6 * ATT_WIDTH
GATE_COLS = 2 * D_MODEL
IN_COLS = QKV_COLS + GATE_COLS
IN_SHARD = IN_COLS // N_DEV
D_FF = 4 * D_MODEL
FF_SHARD = D_FF // N_DEV
PROJ_SHARD = D_MODEL // N_DEV
ATT_BLOCK = 128
CA_TILES = 5
CA_ROWS = 512
CHUNK = 64
CA_PREV_CHUNKS = 8
REL_CLIP = 256
REL_PAD = 640
ALPHA = 2.0 ** 0.25
LN_EPS = 1e-5
QK_SCALE = HEAD_DIM ** -0.5
NEG_BIG = -1e30
VMEM_LIMIT = 56 * 1024 * 1024

ADAM_LR = 0.001
ADAM_B1 = 0.9
ADAM_B2 = 0.999
ADAM_EPS = 1e-08
ADAM_WD = 0.01
ADAM_STEP = 10

_NT = (((1,), (1,)), ((), ()))
_TN = (((0,), (0,)), ((), ()))


def _dot(a, b):
    return jnp.dot(a, b, preferred_element_type=F32)


def _dot_nt(a, b):
    return lax.dot_general(a, b, _NT, preferred_element_type=F32)


def _dot_tn(a, b):
    return lax.dot_general(a, b, _TN, preferred_element_type=F32)


def _params(semantics=None):
    return pltpu.CompilerParams(dimension_semantics=semantics, vmem_limit_bytes=VMEM_LIMIT)


def _row(v):
    return v.reshape(1, -1)


HBM_SPEC = pl.BlockSpec(memory_space=pl.ANY)


def _behind(body, after):
    n = len(after)

    def wrapped(*refs):
        body(*refs[n:])

    return wrapped


def in_proj(x, wg_in, tm=512, after=()):
    s = x.shape[0]

    def body(x_ref, w_ref, qkv_ref, lg_ref, xb_ref):
        xb = x_ref[...].astype(BF16)
        xb_ref[...] = xb
        for j in range(N_DEV):
            acc = _dot(xb, w_ref[j])
            lo, hi = IN_SHARD * j, IN_SHARD * (j + 1)
            if hi <= QKV_COLS:
                qkv_ref[:, lo:hi] = acc.astype(BF16)
            elif lo >= QKV_COLS:
                lg_ref[:, lo - QKV_COLS:hi - QKV_COLS] = acc
            else:
                qkv_ref[:, lo:QKV_COLS] = acc[:, :QKV_COLS - lo].astype(BF16)
                lg_ref[:, 0:hi - QKV_COLS] = acc[:, QKV_COLS - lo:]

    return pl.pallas_call(
        _behind(body, after), name="in_proj", grid=(s // tm,),
        in_specs=[HBM_SPEC] * len(after) + [pl.BlockSpec((tm, D_MODEL), lambda i: (i, 0)),
                  pl.BlockSpec((N_DEV, D_MODEL, IN_SHARD), lambda i: (0, 0, 0))],
        out_specs=[pl.BlockSpec((tm, QKV_COLS), lambda i: (i, 0)),
                   pl.BlockSpec((tm, GATE_COLS), lambda i: (i, 0)),
                   pl.BlockSpec((tm, D_MODEL), lambda i: (i, 0))],
        out_shape=[jax.ShapeDtypeStruct((s, QKV_COLS), BF16),
                   jax.ShapeDtypeStruct((s, GATE_COLS), F32),
                   jax.ShapeDtypeStruct((s, D_MODEL), BF16)],
        compiler_params=_params(("arbitrary",)),
    )(*after, x, wg_in)


def _head_masks():
    lane = lax.broadcasted_iota(jnp.int32, (1, 128), 1)
    first = lane < HEAD_DIM
    return first, jnp.logical_not(first)


SB_CHUNK = 512
SB_SUB = SB_CHUNK // ATT_BLOCK


def _tri_pair():
    row = lax.broadcasted_iota(jnp.int32, (4 * ATT_BLOCK, 2 * ATT_BLOCK), 0) & (2 * ATT_BLOCK - 1)
    col = lax.broadcasted_iota(jnp.int32, (4 * ATT_BLOCK, 2 * ATT_BLOCK), 1)
    same = (row >= ATT_BLOCK) == (col >= ATT_BLOCK)
    return jnp.logical_and(same, row >= col).astype(BF16)


def _pair_bf16(pieces):
    zeros = jnp.zeros((ATT_BLOCK, ATT_BLOCK), BF16)
    return jnp.concatenate(
        [jnp.concatenate([zeros if p is None else p.astype(BF16) for p in row], axis=1) for row in pieces], axis=0)


def _suffix_sums(pieces, tri_pair):
    hi = [[None if p is None else p.astype(BF16) for p in row] for row in pieces]
    lo = [[None if p is None else p - h.astype(F32) for p, h in zip(row, hrow)] for row, hrow in zip(pieces, hi)]
    return _dot(jnp.concatenate([_pair_bf16(hi), _pair_bf16(lo)], axis=1), tri_pair)


def _piece_kind(diag, r, j):
    if not diag or j < r:
        return "full"
    return "diag" if j == r else "none"


def _piece(v, r, h):
    return v[ATT_BLOCK * r:ATT_BLOCK * (r + 1), ATT_BLOCK * h:ATT_BLOCK * (h + 1)]


def _row_totals(csum, h):
    return jnp.concatenate(
        [jnp.broadcast_to(_piece(csum, r, h)[:, 0:1], (ATT_BLOCK, ATT_BLOCK)) for r in range(SB_SUB)], axis=0)


SIGN_BIT = 0x80000000


def _drop(z):
    minus_abs = lax.bitcast_convert_type(lax.bitcast_convert_type(z, jnp.uint32) | jnp.uint32(SIGN_BIT), F32)
    return jnp.maximum(z, 0.0) + jnp.log(1.0 + jnp.exp(minus_abs))


def _sb_weights(z, runs, diag, j, strict, tri_pair):
    drops = []
    for r in range(SB_SUB):
        kind = _piece_kind(diag, r, j)
        row = [None if kind == "none" else _drop(_piece(z, r, h)) for h in range(2)]
        drops.append([jnp.where(strict, drop, 0.0) for drop in row] if kind == "diag" else row)
    csum = _suffix_sums(drops, tri_pair)
    weights = []
    for r in range(SB_SUB):
        kind = _piece_kind(diag, r, j)
        row = [None if kind == "none" else
               jnp.exp(_piece(z, r, h) - (_piece(csum, r, h) + runs[h][ATT_BLOCK * r:ATT_BLOCK * (r + 1)]))
               for h in range(2)]
        weights.append([jnp.where(strict, a, 0.0) for a in row] if kind == "diag" else row)
    return drops, weights, csum


def _by_head(blk, masks):
    zero = jnp.zeros_like(blk)
    return jnp.concatenate([jnp.where(m, blk, zero) for m in masks], axis=0)


def sb_fwd(qkv):
    s = qkv.shape[0]
    nq = s // SB_CHUNK

    def body(q_ref, k_ref, v_ref, o_ref, run_s):
        i = pl.program_id(1)
        masks = _head_masks()
        q = q_ref[...] * QK_SCALE
        row = lax.broadcasted_iota(jnp.int32, (ATT_BLOCK, ATT_BLOCK), 0)
        col = lax.broadcasted_iota(jnp.int32, (ATT_BLOCK, ATT_BLOCK), 1)
        strict = col < row
        tri_pair = _tri_pair()
        run_s[...] = jnp.zeros_like(run_s)
        o_ref[...] = jnp.zeros_like(o_ref)

        def chunk(cb, diag):
            start = cb * SB_CHUNK
            runs = [run_s[0], run_s[1]]
            acc = None
            for j in reversed(range(SB_SUB)):
                keys = pl.ds(pl.multiple_of(start + ATT_BLOCK * j, ATT_BLOCK), ATT_BLOCK)
                z = _dot_nt(q, _by_head(k_ref[keys, :], masks))
                _, weights, csum = _sb_weights(z, runs, diag, j, strict, tri_pair)
                part = _dot(_pair_bf16(weights), _by_head(v_ref[keys, :], masks))
                acc = part if acc is None else acc + part
                runs = [runs[h] + _row_totals(csum, h) for h in range(2)]
            run_s[0], run_s[1] = runs
            o_ref[...] += acc

        chunk(i, True)

        def step(it, carry):
            chunk(i - 1 - it, False)
            return carry

        lax.fori_loop(0, i, step, 0)

    return pl.pallas_call(
        body, name="sb_fwd", grid=(N_PAIRS, nq),
        in_specs=[pl.BlockSpec((SB_CHUNK, 128), lambda p, i: (i, p)),
                  pl.BlockSpec((s, 128), lambda p, i: (0, N_PAIRS + p)),
                  pl.BlockSpec((s, 128), lambda p, i: (0, 2 * N_PAIRS + p))],
        out_specs=pl.BlockSpec((SB_CHUNK, 128), lambda p, i: (i, p)),
        out_shape=jax.ShapeDtypeStruct((s, ATT_WIDTH), F32),
        scratch_shapes=[pltpu.VMEM((2, SB_CHUNK, 128), F32)],
        compiler_params=_params(("arbitrary", "arbitrary")),
    )(qkv, qkv, qkv)


def sb_bwd(qkv, o_sb, do_sb, after=()):
    s = qkv.shape[0]
    nq = s // SB_CHUNK

    def body(q_ref, k_ref, v_ref, o_ref, do_ref, dq_ref, dk_ref, dv_ref, run_l_s, run_e_s, etot_s, dq_s):
        i = pl.program_id(1)

        @pl.when(i == 0)
        def _():
            dk_ref[...] = jnp.zeros_like(dk_ref)
            dv_ref[...] = jnp.zeros_like(dv_ref)

        masks = _head_masks()
        q = q_ref[...] * QK_SCALE
        do = do_ref[...]
        prod = do.astype(F32) * o_ref[...]
        for h in range(2):
            total = jnp.sum(jnp.where(masks[h], prod, 0.0), axis=1, keepdims=True)
            etot_s[h] = jnp.broadcast_to(total, (SB_CHUNK, ATT_BLOCK))
        row = lax.broadcasted_iota(jnp.int32, (ATT_BLOCK, ATT_BLOCK), 0)
        col = lax.broadcasted_iota(jnp.int32, (ATT_BLOCK, ATT_BLOCK), 1)
        strict = col < row
        tri_pair = _tri_pair()
        run_l_s[...] = jnp.zeros_like(run_l_s)
        run_e_s[...] = jnp.zeros_like(run_e_s)
        dq_s[...] = jnp.zeros_like(dq_s)

        def chunk(cb, diag):
            start = cb * SB_CHUNK
            runs_l = [run_l_s[0], run_l_s[1]]
            runs_e = [run_e_s[0], run_e_s[1]]
            e_tot = [etot_s[0], etot_s[1]]
            dq = None
            for j in reversed(range(SB_SUB)):
                keys = pl.ds(pl.multiple_of(start + ATT_BLOCK * j, ATT_BLOCK), ATT_BLOCK)
                kk = _by_head(k_ref[keys, :], masks)
                vv = _by_head(v_ref[keys, :], masks)
                z = _dot_nt(q, kk)
                da = _dot_nt(do, vv)
                drops, weights, csum = _sb_weights(z, runs_l, diag, j, strict, tri_pair)
                ab = [[None if a is None else a.astype(BF16) for a in wrow] for wrow in weights]
                es = [[None if a is None else a.astype(F32) * _piece(da, r, h) for h, a in enumerate(arow)]
                      for r, arow in enumerate(ab)]
                esum = _suffix_sums(es, tri_pair)
                dzs = []
                for r in range(SB_SUB):
                    kind = _piece_kind(diag, r, j)
                    if kind == "none":
                        dzs.append([None, None])
                        continue
                    dz_row = []
                    for h in range(2):
                        rows = slice(ATT_BLOCK * r, ATT_BLOCK * (r + 1))
                        sig = jnp.exp(_piece(z, r, h) - drops[r][h])
                        after = _piece(esum, r, h) + runs_e[h][rows] - es[r][h]
                        dz = es[r][h] - sig * (e_tot[h][rows] - after)
                        dz_row.append(jnp.where(strict, dz, 0.0) if kind == "diag" else dz)
                    dzs.append(dz_row)
                dz = _pair_bf16(dzs)
                part = _dot(dz, kk)
                dq = part if dq is None else dq + part
                dk = _dot_tn(dz, q)
                dv = _dot_tn(_pair_bf16(ab), do)
                dk_ref[keys, :] += jnp.where(masks[0], dk[:ATT_BLOCK], dk[ATT_BLOCK:])
                dv_ref[keys, :] += jnp.where(masks[0], dv[:ATT_BLOCK], dv[ATT_BLOCK:])
                runs_l = [runs_l[h] + _row_totals(csum, h) for h in range(2)]
                runs_e = [runs_e[h] + _row_totals(esum, h) for h in range(2)]
            run_l_s[0], run_l_s[1] = runs_l
            run_e_s[0], run_e_s[1] = runs_e
            dq_s[...] += dq

        chunk(i, True)

        def step(it, carry):
            chunk(i - 1 - it, False)
            return carry

        lax.fori_loop(0, i, step, 0)
        dq_ref[...] = (dq_s[...] * QK_SCALE).astype(BF16)

    tile = lambda p, i: (i, p)
    state = pltpu.VMEM((2, SB_CHUNK, 128), F32)
    return pl.pallas_call(
        _behind(body, after), name="sb_bwd", grid=(N_PAIRS, nq),
        in_specs=[HBM_SPEC] * len(after) + [pl.BlockSpec((SB_CHUNK, 128), tile),
                  pl.BlockSpec((s, 128), lambda p, i: (0, N_PAIRS + p)),
                  pl.BlockSpec((s, 128), lambda p, i: (0, 2 * N_PAIRS + p)),
                  pl.BlockSpec((SB_CHUNK, 128), tile),
                  pl.BlockSpec((SB_CHUNK, 128), tile)],
        out_specs=[pl.BlockSpec((SB_CHUNK, 128), tile),
                   pl.BlockSpec((s, 128), lambda p, i: (0, p)),
                   pl.BlockSpec((s, 128), lambda p, i: (0, p))],
        out_shape=[jax.ShapeDtypeStruct((s, ATT_WIDTH), BF16),
                   jax.ShapeDtypeStruct((s, ATT_WIDTH), F32),
                   jax.ShapeDtypeStruct((s, ATT_WIDTH), F32)],
        scratch_shapes=[state, state, state, pltpu.VMEM((SB_CHUNK, 128), F32)],
        compiler_params=_params(("arbitrary", "arbitrary")),
    )(*after, qkv, qkv, qkv, o_sb, do_sb)


def ca_bias_tiles(rel_bias):
    heads = rel_bias.shape[0]
    wide = 2 * ATT_BLOCK

    def body(rel_ref, out_ref):
        bins = lax.broadcasted_iota(jnp.int32, (REL_PAD, wide), 0)
        pos = lax.broadcasted_iota(jnp.int32, (REL_PAD, wide), 1)
        shift = (ATT_BLOCK - 1) - lax.broadcasted_iota(jnp.int32, (ATT_BLOCK, wide), 0)
        r = lax.broadcasted_iota(jnp.int32, (ATT_BLOCK, ATT_BLOCK), 0)
        c = lax.broadcasted_iota(jnp.int32, (ATT_BLOCK, ATT_BLOCK), 1)
        for j in range(CA_TILES):
            hot = (jnp.clip(ATT_BLOCK * (j + 1) - 1 - pos, -REL_CLIP, REL_CLIP) + REL_CLIP == bins).astype(F32)
            lines = jnp.dot(rel_ref[...], hot, preferred_element_type=F32, precision=lax.Precision.HIGHEST)
            back = 2 * j + (r >> 6) - (c >> 6)
            ok = jnp.logical_and(back >= 0, back <= CA_PREV_CHUNKS)
            for h in range(heads):
                tile = jnp.broadcast_to(lines[h:h + 1, :], (ATT_BLOCK, wide))
                for bit in range(7):
                    rolled = pltpu.roll(tile, wide - (1 << bit), 1)
                    tile = jnp.where((shift >> bit) & 1 == 1, rolled, tile)
                out_ref[h, j] = jnp.where(ok, tile[:, :ATT_BLOCK], NEG_BIG)

    padded = jnp.pad(rel_bias, ((0, 0), (0, REL_PAD - rel_bias.shape[1])))
    return pl.pallas_call(
        body, name="ca_bias_tiles",
        out_shape=jax.ShapeDtypeStruct((heads, CA_TILES, ATT_BLOCK, ATT_BLOCK), F32),
        compiler_params=_params(),
    )(padded)


CA_BLOCKS = CA_ROWS // ATT_BLOCK
CA_KEY_BLOCKS = CA_BLOCKS + CA_TILES - 1


def _ca_users(t):
    return [r for r in range(CA_BLOCKS) if 0 <= r + CA_TILES - 1 - t < CA_TILES]


def _ca_rows(v, t, r):
    at = _ca_users(t).index(r)
    return v[ATT_BLOCK * at:ATT_BLOCK * (at + 1)]


def _ca_window(step, k_ref, v_ref, masks):
    kks, vvs, inside, rows = [], [], [], []
    for t in range(CA_KEY_BLOCKS):
        block = step * CA_BLOCKS - (CA_TILES - 1) + t
        keys = pl.ds(pl.multiple_of(jnp.maximum(block, 0) * ATT_BLOCK, ATT_BLOCK), ATT_BLOCK)
        kks.append(_by_head(k_ref[keys, :], masks))
        vvs.append(_by_head(v_ref[keys, :], masks))
        inside.append(block >= 0)
        rows.append(keys)
    return kks, vvs, inside, rows


def _ca_probs(q, kks, inside, bias_ref):
    scores = [[[None] * CA_TILES for _ in range(2)] for _ in range(CA_BLOCKS)]
    for t in range(CA_KEY_BLOCKS):
        users = _ca_users(t)
        z = _dot_nt(q[ATT_BLOCK * users[0]:ATT_BLOCK * (users[-1] + 1)], kks[t])
        for r in users:
            j = r + CA_TILES - 1 - t
            for h in range(2):
                zz = _ca_rows(z, t, r)[:, ATT_BLOCK * h:ATT_BLOCK * (h + 1)] + bias_ref[h, j]
                scores[r][h][j] = jnp.where(inside[t], zz, NEG_BIG)
    probs = [[None, None] for _ in range(CA_BLOCKS)]
    for r in range(CA_BLOCKS):
        for h in range(2):
            m = jnp.max(functools.reduce(jnp.maximum, scores[r][h]), axis=1, keepdims=True)
            ex = [jnp.exp(z - m) for z in scores[r][h]]
            inv = 1.0 / jnp.sum(functools.reduce(jnp.add, ex), axis=1, keepdims=True)
            probs[r][h] = [e * inv for e in ex]
    return probs


def _ca_stack(tiles, t):
    return jnp.concatenate(
        [jnp.concatenate([tiles[r][h][r + CA_TILES - 1 - t].astype(BF16) for h in range(2)], axis=1)
         for r in _ca_users(t)], axis=0)


def ca_fwd(qkv, bias_tiles):
    s = qkv.shape[0]
    nq = s // CA_ROWS

    def body(q_ref, k_ref, v_ref, bias_ref, o_ref):
        masks = _head_masks()
        kks, vvs, inside, _ = _ca_window(pl.program_id(1), k_ref, v_ref, masks)
        probs = _ca_probs(q_ref[...] * QK_SCALE, kks, inside, bias_ref)
        out = [None] * CA_BLOCKS
        for t in range(CA_KEY_BLOCKS):
            part = _dot(_ca_stack(probs, t), vvs[t])
            for r in _ca_users(t):
                piece = _ca_rows(part, t, r)
                out[r] = piece if out[r] is None else out[r] + piece
        o_ref[...] = jnp.concatenate(out, axis=0)

    return pl.pallas_call(
        body, name="ca_fwd", grid=(N_PAIRS, nq),
        in_specs=[pl.BlockSpec((CA_ROWS, 128), lambda p, i: (i, 3 * N_PAIRS + p)),
                  pl.BlockSpec((s, 128), lambda p, i: (0, 4 * N_PAIRS + p)),
                  pl.BlockSpec((s, 128), lambda p, i: (0, 5 * N_PAIRS + p)),
                  pl.BlockSpec((2, CA_TILES, ATT_BLOCK, ATT_BLOCK), lambda p, i: (p, 0, 0, 0))],
        out_specs=pl.BlockSpec((CA_ROWS, 128), lambda p, i: (i, p)),
        out_shape=jax.ShapeDtypeStruct((s, ATT_WIDTH), F32),
        compiler_params=_params(("arbitrary", "arbitrary")),
    )(qkv, qkv, qkv, bias_tiles)


def ca_bwd(qkv, bias_tiles, do_ca, after=()):
    s = qkv.shape[0]
    nq = s // CA_ROWS

    def body(q_ref, k_ref, v_ref, bias_ref, do_ref, dq_ref, dk_ref, dv_ref, db_ref):
        big = pl.program_id(1)

        @pl.when(big == 0)
        def _():
            dk_ref[...] = jnp.zeros_like(dk_ref)
            dv_ref[...] = jnp.zeros_like(dv_ref)
            db_ref[...] = jnp.zeros_like(db_ref)

        masks = _head_masks()
        kks, vvs, inside, key_rows = _ca_window(big, k_ref, v_ref, masks)
        q = q_ref[...] * QK_SCALE
        do = do_ref[...]
        probs = _ca_probs(q, kks, inside, bias_ref)
        dps = [[[None] * CA_TILES for _ in range(2)] for _ in range(CA_BLOCKS)]
        for t in range(CA_KEY_BLOCKS):
            users = _ca_users(t)
            dp = _dot_nt(do[ATT_BLOCK * users[0]:ATT_BLOCK * (users[-1] + 1)], vvs[t])
            for r in users:
                for h in range(2):
                    dps[r][h][r + CA_TILES - 1 - t] = _ca_rows(dp, t, r)[:, ATT_BLOCK * h:ATT_BLOCK * (h + 1)]
        dss = [[None, None] for _ in range(CA_BLOCKS)]
        for r in range(CA_BLOCKS):
            for h in range(2):
                delta = jnp.sum(functools.reduce(jnp.add, [p * dp for p, dp in zip(probs[r][h], dps[r][h])]),
                                axis=1, keepdims=True)
                dss[r][h] = [p * (dp - delta) for p, dp in zip(probs[r][h], dps[r][h])]
        for h in range(2):
            for j in range(CA_TILES):
                db_ref[h, j] += functools.reduce(jnp.add, [dss[r][h][j] for r in range(CA_BLOCKS)])
        dq = [None] * CA_BLOCKS
        for t in range(CA_KEY_BLOCKS):
            users = _ca_users(t)
            rows = slice(ATT_BLOCK * users[0], ATT_BLOCK * (users[-1] + 1))
            ds = _ca_stack(dss, t)
            part = _dot(ds, kks[t])
            for r in users:
                piece = _ca_rows(part, t, r)
                dq[r] = piece if dq[r] is None else dq[r] + piece
            dk = _dot_tn(ds, q[rows])
            dv = _dot_tn(_ca_stack(probs, t), do[rows])
            dk_ref[key_rows[t], :] += jnp.where(masks[0], dk[:ATT_BLOCK], dk[ATT_BLOCK:])
            dv_ref[key_rows[t], :] += jnp.where(masks[0], dv[:ATT_BLOCK], dv[ATT_BLOCK:])
        dq_ref[...] = (jnp.concatenate(dq, axis=0) * QK_SCALE).astype(BF16)

    return pl.pallas_call(
        _behind(body, after), name="ca_bwd", grid=(N_PAIRS, nq),
        in_specs=[HBM_SPEC] * len(after) + [pl.BlockSpec((CA_ROWS, 128), lambda p, i: (i, 3 * N_PAIRS + p)),
                  pl.BlockSpec((s, 128), lambda p, i: (0, 4 * N_PAIRS + p)),
                  pl.BlockSpec((s, 128), lambda p, i: (0, 5 * N_PAIRS + p)),
                  pl.BlockSpec((2, CA_TILES, ATT_BLOCK, ATT_BLOCK), lambda p, i: (p, 0, 0, 0)),
                  pl.BlockSpec((CA_ROWS, 128), lambda p, i: (i, p))],
        out_specs=[pl.BlockSpec((CA_ROWS, 128), lambda p, i: (i, p)),
                   pl.BlockSpec((s, 128), lambda p, i: (0, p)),
                   pl.BlockSpec((s, 128), lambda p, i: (0, p)),
                   pl.BlockSpec((2, CA_TILES, ATT_BLOCK, ATT_BLOCK), lambda p, i: (p, 0, 0, 0))],
        out_shape=[jax.ShapeDtypeStruct((s, ATT_WIDTH), BF16),
                   jax.ShapeDtypeStruct((s, ATT_WIDTH), F32),
                   jax.ShapeDtypeStruct((s, ATT_WIDTH), F32),
                   jax.ShapeDtypeStruct((2 * N_PAIRS, CA_TILES, ATT_BLOCK, ATT_BLOCK), F32)],
        compiler_params=_params(("arbitrary", "arbitrary")),
    )(*after, qkv, qkv, qkv, bias_tiles, do_ca)


def rel_bias_grad(db_tiles):
    heads = db_tiles.shape[0]

    def body(db_ref, out_ref):
        row = lax.broadcasted_iota(jnp.int32, (ATT_BLOCK, ATT_BLOCK), 0)
        col = lax.broadcasted_iota(jnp.int32, (ATT_BLOCK, ATT_BLOCK), 1)
        cp = lax.broadcasted_iota(jnp.int32, (ATT_BLOCK, REL_PAD), 0)
        bins = lax.broadcasted_iota(jnp.int32, (ATT_BLOCK, REL_PAD), 1)
        total = jnp.zeros((heads, REL_PAD), F32)
        for j in range(CA_TILES):
            sums_lo, sums_up = [], []
            for h in range(heads):
                tile = db_ref[h, j]
                for bit in range(7):
                    rolled = pltpu.roll(tile, ATT_BLOCK - (1 << bit), 1)
                    tile = jnp.where((row >> bit) & 1 == 1, rolled, tile)
                was_lower = (col + row >= ATT_BLOCK) | (col == 0)
                sums_lo.append(jnp.sum(jnp.where(was_lower, tile, 0.0), axis=0, keepdims=True))
                sums_up.append(jnp.sum(jnp.where(was_lower, 0.0, tile), axis=0, keepdims=True))
            lo = jnp.concatenate(sums_lo, axis=0)
            up = jnp.concatenate(sums_up, axis=0)
            d_lo = jnp.where(cp == 0, 0, ATT_BLOCK - cp)
            hot_lo = (jnp.clip(ATT_BLOCK * j + d_lo, -REL_CLIP, REL_CLIP) + REL_CLIP == bins).astype(F32)
            hot_up = (jnp.clip(ATT_BLOCK * j - cp, -REL_CLIP, REL_CLIP) + REL_CLIP == bins).astype(F32)
            total = total + jnp.dot(lo, hot_lo, preferred_element_type=F32, precision=lax.Precision.HIGHEST)
            total = total + jnp.dot(up, hot_up, preferred_element_type=F32, precision=lax.Precision.HIGHEST)
        out_ref[...] = total

    return pl.pallas_call(
        body, name="rel_bias_grad",
        out_shape=jax.ShapeDtypeStruct((heads, REL_PAD), F32),
        compiler_params=_params(),
    )(db_tiles)


def _assemble_cols(dst_ref, src_ref):
    w = src_ref.shape[2]
    for j in range(N_DEV):
        dst_ref[:, w * j:w * (j + 1)] = src_ref[j]


def _merge(o_sb, o_ca, lg, bg, wsb, wca):
    y_sb = _dot(o_sb.astype(BF16), wsb)
    y_ca = _dot(o_ca.astype(BF16), wca)
    gates = jax.nn.sigmoid(lg + bg)
    g_sb, g_ca = gates[:, :D_MODEL], gates[:, D_MODEL:]
    return y_sb, y_ca, g_sb, g_ca, g_sb * y_sb + g_ca * y_ca


def mix_fwd(x, o_sb, o_ca, lg, b_gate, wg_sb, wg_ca, w_out, tm=256):
    s = x.shape[0]

    def body(x_ref, osb_ref, oca_ref, lg_ref, bg_ref, wsb_ref, wca_ref, wout_ref, u_ref, wsb_s, wca_s):
        @pl.when(pl.program_id(0) == 0)
        def _():
            _assemble_cols(wsb_s, wsb_ref)
            _assemble_cols(wca_s, wca_ref)

        merged = _merge(osb_ref[...], oca_ref[...], lg_ref[...], bg_ref[...], wsb_s[...], wca_s[...])[4]
        u_ref[...] = ALPHA * x_ref[...] + _dot(merged.astype(BF16), wout_ref[...])

    rows = lambda w: pl.BlockSpec((tm, w), lambda i: (i, 0))
    whole = lambda shape: pl.BlockSpec(shape, lambda i: (0,) * len(shape))
    return pl.pallas_call(
        body, name="mix_fwd", grid=(s // tm,),
        in_specs=[rows(D_MODEL), rows(ATT_WIDTH), rows(ATT_WIDTH), rows(GATE_COLS), whole((1, GATE_COLS)),
                  whole((N_DEV, ATT_WIDTH, PROJ_SHARD)), whole((N_DEV, ATT_WIDTH, PROJ_SHARD)),
                  whole((D_MODEL, D_MODEL))],
        out_specs=rows(D_MODEL),
        out_shape=jax.ShapeDtypeStruct((s, D_MODEL), F32),
        scratch_shapes=[pltpu.VMEM((ATT_WIDTH, D_MODEL), BF16), pltpu.VMEM((ATT_WIDTH, D_MODEL), BF16)],
        compiler_params=_params(("arbitrary",)),
    )(x, o_sb, o_ca, lg, _row(b_gate), wg_sb, wg_ca, w_out)


def mix_bwd(du1, o_sb, o_ca, lg, b_gate, wg_sb, wg_ca, w_out, tm=256, after=()):
    s = du1.shape[0]

    def body(du_ref, osb_ref, oca_ref, lg_ref, bg_ref, wsb_ref, wca_ref, wout_ref,
             dlg_ref, dosb_ref, doca_ref, dysb_ref, dyca_ref, mg_ref, dbg_ref, wsb_s, wca_s):
        @pl.when(pl.program_id(0) == 0)
        def _():
            _assemble_cols(wsb_s, wsb_ref)
            _assemble_cols(wca_s, wca_ref)
            dbg_ref[...] = jnp.zeros_like(dbg_ref)

        y_sb, y_ca, g_sb, g_ca, merged = _merge(
            osb_ref[...], oca_ref[...], lg_ref[...], bg_ref[...], wsb_s[...], wca_s[...])
        mg_ref[...] = merged.astype(BF16)
        dm = _dot_nt(du_ref[...].astype(BF16), wout_ref[...])
        dl_sb = dm * y_sb * g_sb * (1.0 - g_sb)
        dl_ca = dm * y_ca * g_ca * (1.0 - g_ca)
        dlg_ref[:, :D_MODEL] = dl_sb.astype(BF16)
        dlg_ref[:, D_MODEL:] = dl_ca.astype(BF16)
        dbg_ref[:, :D_MODEL] += jnp.sum(dl_sb, axis=0, keepdims=True)
        dbg_ref[:, D_MODEL:] += jnp.sum(dl_ca, axis=0, keepdims=True)
        dy_sb = (dm * g_sb).astype(BF16)
        dy_ca = (dm * g_ca).astype(BF16)
        dysb_ref[...] = dy_sb
        dyca_ref[...] = dy_ca
        dosb_ref[...] = _dot_nt(dy_sb, wsb_s[...]).astype(BF16)
        doca_ref[...] = _dot_nt(dy_ca, wca_s[...]).astype(BF16)

    rows = lambda w: pl.BlockSpec((tm, w), lambda i: (i, 0))
    whole = lambda shape: pl.BlockSpec(shape, lambda i: (0,) * len(shape))
    return pl.pallas_call(
        _behind(body, after), name="mix_bwd", grid=(s // tm,),
        in_specs=[HBM_SPEC] * len(after) + [rows(D_MODEL), rows(ATT_WIDTH), rows(ATT_WIDTH), rows(GATE_COLS), whole((1, GATE_COLS)),
                  whole((N_DEV, ATT_WIDTH, PROJ_SHARD)), whole((N_DEV, ATT_WIDTH, PROJ_SHARD)),
                  whole((D_MODEL, D_MODEL))],
        out_specs=[rows(GATE_COLS), rows(ATT_WIDTH), rows(ATT_WIDTH), rows(D_MODEL), rows(D_MODEL),
                   rows(D_MODEL), whole((1, GATE_COLS))],
        out_shape=[jax.ShapeDtypeStruct((s, GATE_COLS), BF16),
                   jax.ShapeDtypeStruct((s, ATT_WIDTH), BF16), jax.ShapeDtypeStruct((s, ATT_WIDTH), BF16),
                   jax.ShapeDtypeStruct((s, D_MODEL), BF16), jax.ShapeDtypeStruct((s, D_MODEL), BF16),
                   jax.ShapeDtypeStruct((s, D_MODEL), BF16), jax.ShapeDtypeStruct((1, GATE_COLS), F32)],
        scratch_shapes=[pltpu.VMEM((ATT_WIDTH, D_MODEL), BF16), pltpu.VMEM((ATT_WIDTH, D_MODEL), BF16)],
        compiler_params=_params(("arbitrary",)),
    )(*after, du1, o_sb, o_ca, lg, _row(b_gate), wg_sb, wg_ca, w_out)


def _ln_stats(u):
    mu = jnp.mean(u, axis=1, keepdims=True)
    cen = u - mu
    var = jnp.mean(cen * cen, axis=1, keepdims=True)
    rstd = lax.rsqrt(var + LN_EPS)
    return cen * rstd, rstd


def _ln_bwd(dy, xhat, rstd, gain):
    dyg = dy * gain
    m1 = jnp.mean(dyg, axis=1, keepdims=True)
    m2 = jnp.mean(dyg * xhat, axis=1, keepdims=True)
    return rstd * (dyg - m1 - xhat * m2)


def mlp_fwd(u1, target, wg_in, wg_out, ln1_g, ln1_b, ln2_g, ln2_b, tm=512):
    s = u1.shape[0]
    nk = N_DEV

    def body(u_ref, t_ref, win_ref, wout_ref, g1_ref, b1_ref, g2_ref, b2_ref,
             hm_ref, a_ref, x1b_ref, du2_ref, loss_ref, dg2_ref, db2_ref, x1_s, acc_s):
        i, k = pl.program_id(0), pl.program_id(1)

        @pl.when(jnp.logical_and(i == 0, k == 0))
        def _():
            loss_ref[...] = jnp.zeros_like(loss_ref)
            dg2_ref[...] = jnp.zeros_like(dg2_ref)
            db2_ref[...] = jnp.zeros_like(db2_ref)

        @pl.when(k == 0)
        def _():
            xhat, _ = _ln_stats(u_ref[...])
            x1 = xhat * g1_ref[...] + b1_ref[...]
            x1_s[...] = x1
            x1b_ref[...] = x1.astype(BF16)
            acc_s[...] = jnp.zeros_like(acc_s)

        hm = _dot(x1b_ref[...], win_ref[0])
        hm_ref[...] = hm
        r = jnp.maximum(hm, 0.0)
        a = (r * r).astype(BF16)
        a_ref[...] = a
        acc_s[...] += _dot(a, wout_ref[0])

        @pl.when(k == nk - 1)
        def _():
            xhat, rstd = _ln_stats(ALPHA * x1_s[...] + acc_s[...])
            diff = xhat * g2_ref[...] + b2_ref[...] - t_ref[...]
            per_token = jnp.mean(diff * diff, axis=1, keepdims=True)
            loss_ref[...] += 0.5 * jnp.sum(per_token, axis=0, keepdims=True)
            dy = diff * (1.0 / D_MODEL)
            dg2_ref[...] += jnp.sum(dy * xhat, axis=0, keepdims=True)
            db2_ref[...] += jnp.sum(dy, axis=0, keepdims=True)
            du2_ref[...] = _ln_bwd(dy, xhat, rstd, g2_ref[...])

    rows = pl.BlockSpec((tm, D_MODEL), lambda i, k: (i, 0))
    vec = pl.BlockSpec((1, D_MODEL), lambda i, k: (0, 0))
    ff = pl.BlockSpec((tm, FF_SHARD), lambda i, k: (i, k))
    return pl.pallas_call(
        body, name="mlp_fwd", grid=(s // tm, nk),
        in_specs=[rows, rows,
                  pl.BlockSpec((1, D_MODEL, FF_SHARD), lambda i, k: (k, 0, 0)),
                  pl.BlockSpec((1, FF_SHARD, D_MODEL), lambda i, k: (k, 0, 0)),
                  vec, vec, vec, vec],
        out_specs=[ff, ff, rows, rows, pl.BlockSpec((1, 1), lambda i, k: (0, 0)), vec, vec],
        out_shape=[jax.ShapeDtypeStruct((s, D_FF), F32), jax.ShapeDtypeStruct((s, D_FF), BF16),
                   jax.ShapeDtypeStruct((s, D_MODEL), BF16), jax.ShapeDtypeStruct((s, D_MODEL), F32),
                   jax.ShapeDtypeStruct((1, 1), F32),
                   jax.ShapeDtypeStruct((1, D_MODEL), F32), jax.ShapeDtypeStruct((1, D_MODEL), F32)],
        scratch_shapes=[pltpu.VMEM((tm, D_MODEL), F32), pltpu.VMEM((tm, D_MODEL), F32)],
        compiler_params=_params(("arbitrary", "arbitrary")),
    )(u1, target, wg_in, wg_out, _row(ln1_g), _row(ln1_b), _row(ln2_g), _row(ln2_b))


def mlp_bwd(du2, hm, u1, wg_in, wg_out, ln1_g, tm=512):
    s = du2.shape[0]
    nk = N_DEV

    def body(du2_ref, hm_ref, u_ref, win_ref, wout_ref, g1_ref,
             dhm_ref, du1_ref, dg1_ref, db1_ref, du2b_s, acc_s):
        i, k = pl.program_id(0), pl.program_id(1)

        @pl.when(jnp.logical_and(i == 0, k == 0))
        def _():
            dg1_ref[...] = jnp.zeros_like(dg1_ref)
            db1_ref[...] = jnp.zeros_like(db1_ref)

        @pl.when(k == 0)
        def _():
            du2b_s[...] = du2_ref[...].astype(BF16)
            acc_s[...] = jnp.zeros_like(acc_s)

        da = _dot_nt(du2b_s[...], wout_ref[0])
        dhm = (2.0 * jnp.maximum(hm_ref[...], 0.0) * da).astype(BF16)
        dhm_ref[...] = dhm
        acc_s[...] += _dot_nt(dhm, win_ref[0])

        @pl.when(k == nk - 1)
        def _():
            dx1 = ALPHA * du2_ref[...] + acc_s[...]
            xhat, rstd = _ln_stats(u_ref[...])
            dg1_ref[...] += jnp.sum(dx1 * xhat, axis=0, keepdims=True)
            db1_ref[...] += jnp.sum(dx1, axis=0, keepdims=True)
            du1_ref[...] = _ln_bwd(dx1, xhat, rstd, g1_ref[...])

    rows = pl.BlockSpec((tm, D_MODEL), lambda i, k: (i, 0))
    vec = pl.BlockSpec((1, D_MODEL), lambda i, k: (0, 0))
    ff = pl.BlockSpec((tm, FF_SHARD), lambda i, k: (i, k))
    return pl.pallas_call(
        body, name="mlp_bwd", grid=(s // tm, nk),
        in_specs=[rows, ff, rows,
                  pl.BlockSpec((1, D_MODEL, FF_SHARD), lambda i, k: (k, 0, 0)),
                  pl.BlockSpec((1, FF_SHARD, D_MODEL), lambda i, k: (k, 0, 0)),
                  vec],
        out_specs=[ff, rows, vec, vec],
        out_shape=[jax.ShapeDtypeStruct((s, D_FF), BF16), jax.ShapeDtypeStruct((s, D_MODEL), F32),
                   jax.ShapeDtypeStruct((1, D_MODEL), F32), jax.ShapeDtypeStruct((1, D_MODEL), F32)],
        scratch_shapes=[pltpu.VMEM((tm, D_MODEL), BF16), pltpu.VMEM((tm, D_MODEL), F32)],
        compiler_params=_params(("arbitrary", "arbitrary")),
    )(du2, hm, u1, wg_in, wg_out, _row(ln1_g))


def dh_slab(pieces, tm=512):
    s = pieces[0].shape[0]
    widths = [p.shape[1] for p in pieces]
    assert sum(widths) == IN_COLS
    n = len(pieces)

    def body(*refs):
        dh_ref = refs[n]
        off = 0
        for ref, w in zip(refs[:n], widths):
            dh_ref[:, off:off + w] = ref[...].astype(BF16)
            off += w

    rows = lambda w: pl.BlockSpec((tm, w), lambda i: (i, 0))
    return pl.pallas_call(
        body, name="dh_slab", grid=(s // tm,),
        in_specs=[rows(w) for w in widths], out_specs=rows(IN_COLS),
        out_shape=jax.ShapeDtypeStruct((s, IN_COLS), BF16),
        compiler_params=_params(("arbitrary",)),
    )(*pieces)


def in_bwd(du1, dh, wg_in, tm=512, after=()):
    s = du1.shape[0]

    def body(du_ref, dh_ref, w_ref, gx_ref):
        acc = ALPHA * du_ref[...]
        for j in range(N_DEV):
            acc = acc + _dot_nt(dh_ref[:, IN_SHARD * j:IN_SHARD * (j + 1)], w_ref[j])
        gx_ref[...] = acc

    rows = lambda w: pl.BlockSpec((tm, w), lambda i: (i, 0))
    return pl.pallas_call(
        _behind(body, after), name="in_bwd", grid=(s // tm,),
        in_specs=[HBM_SPEC] * len(after) + [rows(D_MODEL), rows(IN_COLS),
                                            pl.BlockSpec((N_DEV, D_MODEL, IN_SHARD), lambda i: (0, 0, 0))],
        out_specs=rows(D_MODEL),
        out_shape=jax.ShapeDtypeStruct((s, D_MODEL), F32),
        compiler_params=_params(("arbitrary",)),
    )(*after, du1, dh, wg_in)


def wgrad(name, a, b, tm, tn, shard_cols=None, tk=2048, after=()):
    kk, m = a.shape
    n = b.shape[1]
    tk = min(tk, kk)
    nk = kk // tk
    w = shard_cols

    def body(a_ref, b_ref, out_ref, acc_s):
        k = pl.program_id(2)

        @pl.when(k == 0)
        def _():
            acc_s[...] = jnp.zeros_like(acc_s)

        acc_s[...] += _dot_tn(a_ref[...].astype(BF16), b_ref[...].astype(BF16))

        @pl.when(k == nk - 1)
        def _():
            if w is None:
                out_ref[...] = acc_s[...]
            else:
                for j in range(tn // w):
                    out_ref[j] = acc_s[:, w * j:w * (j + 1)]

    if w is None:
        out_spec = pl.BlockSpec((tm, tn), lambda i, j, k: (i, j))
        out_shape = jax.ShapeDtypeStruct((m, n), F32)
    else:
        out_spec = pl.BlockSpec((tn // w, tm, w), lambda i, j, k: (j, i, 0))
        out_shape = jax.ShapeDtypeStruct((n // w, m, w), F32)
    return pl.pallas_call(
        _behind(body, after), name=name, grid=(m // tm, n // tn, nk),
        in_specs=[HBM_SPEC] * len(after) + [pl.BlockSpec((tk, tm), lambda i, j, k: (k, i)),
                  pl.BlockSpec((tk, tn), lambda i, j, k: (k, j))],
        out_specs=out_spec, out_shape=out_shape,
        scratch_shapes=[pltpu.VMEM((tm, tn), F32)],
        compiler_params=_params(("arbitrary", "arbitrary", "arbitrary")),
    )(*after, a, b)


def adamw(name, w, m, v, parts, tm=None):
    rows, cols = w.shape
    tm = rows if tm is None else min(tm, rows)
    n = len(parts)
    c1 = 1.0 - ADAM_B1 ** ADAM_STEP
    c2 = 1.0 - ADAM_B2 ** ADAM_STEP

    def body(*refs):
        w_ref, m_ref, v_ref = refs[:3]
        part_refs = refs[3:3 + n]
        g_ref, d_ref, nm_ref, nv_ref = refs[3 + n:]
        g = None
        for ref, (_, index) in zip(part_refs, parts):
            term = (ref[...] if index is None else ref[0]).astype(F32)
            g = term if g is None else g + term
        new_m = ADAM_B1 * m_ref[...] + (1.0 - ADAM_B1) * g
        new_v = ADAM_B2 * v_ref[...] + (1.0 - ADAM_B2) * (g * g)
        m_hat = new_m / c1
        v_hat = new_v / c2
        g_ref[...] = g
        d_ref[...] = -ADAM_LR * (m_hat / (jnp.sqrt(v_hat) + ADAM_EPS) + ADAM_WD * w_ref[...])
        nm_ref[...] = new_m
        nv_ref[...] = new_v

    spec = pl.BlockSpec((tm, cols), lambda i: (i, 0))
    shape = jax.ShapeDtypeStruct((rows, cols), F32)

    def part_spec(index):
        if index is None:
            return spec
        return pl.BlockSpec((1, tm, cols), lambda i: (index, i, 0))

    return pl.pallas_call(
        body, name=name, grid=(rows // tm,),
        in_specs=[spec] * 3 + [part_spec(index) for _, index in parts],
        out_specs=[spec] * 4, out_shape=[shape] * 4,
        compiler_params=_params(("arbitrary",)),
    )(w, m, v, *[array for array, _ in parts])


def _place():
    x, y, c = lax.axis_index("x"), lax.axis_index("y"), lax.axis_index("c")
    other_chips = [(1 - x, y), (x, 1 - y), (1 - x, 1 - y)]
    return x, y, c, other_chips


def all_gather(name, arrays, seeds=()):
    n = len(arrays)
    ns = len(seeds)

    def body(*refs):
        ins, seed_ins = refs[:n], refs[n:n + ns]
        outs, seed_outs = refs[n + ns:2 * n + ns], refs[2 * n + ns:2 * (n + ns)]
        send_sems, recv_sems, local_sems = refs[2 * (n + ns):]
        x, y, c, chips = _place()
        me = 4 * x + 2 * y + c
        sibling = (x, y, 1 - c)

        def copy(a, k, block, to, src=None):
            return pltpu.make_async_remote_copy(
                src_ref=outs[a].at[block] if src is None else src, dst_ref=outs[a].at[block],
                send_sem=send_sems.at[a, k], recv_sem=recv_sems.at[a, k],
                device_id=to, device_id_type=MESH)

        started = []
        local = [pltpu.make_async_copy(ins[a], outs[a].at[me], local_sems.at[a]) for a in range(n)]
        local += [pltpu.make_async_copy(seed_ins[a], seed_outs[a].at[me], local_sems.at[n + a]) for a in range(ns)]
        for cp in local[n:]:
            cp.start()
        for a in range(n):
            local[a].start()
            sends = [copy(a, 0, me, sibling, src=ins[a])]
            sends += [copy(a, 1 + j, me, (cx, cy, c), src=ins[a]) for j, (cx, cy) in enumerate(chips)]
            for cp in sends:
                cp.start()
            started += sends
        for a in range(n):
            for j, (cx, cy) in enumerate(chips):
                block = 4 * cx + 2 * cy + c
                copy(a, 1 + j, block, sibling).wait_recv()
                passed = copy(a, 4 + j, block, sibling)
                passed.start()
                started.append(passed)
        for a in range(n):
            copy(a, 0, 4 * x + 2 * y + (1 - c), sibling).wait_recv()
            for j, (cx, cy) in enumerate(chips):
                copy(a, 4 + j, 4 * cx + 2 * cy + (1 - c), sibling).wait_recv()
        for cp in started:
            cp.wait_send()
        for cp in local:
            cp.wait()

    return pl.pallas_call(
        body, name=name,
        in_specs=[HBM_SPEC] * (n + ns), out_specs=[HBM_SPEC] * (n + ns),
        out_shape=[jax.ShapeDtypeStruct((N_DEV,) + a.shape, a.dtype) for a in list(arrays) + list(seeds)],
        scratch_shapes=[pltpu.SemaphoreType.DMA((n, 7)), pltpu.SemaphoreType.DMA((n, 7)),
                        pltpu.SemaphoreType.DMA((n + ns,))],
    )(*arrays, *seeds)


SEM_SPEC = pl.BlockSpec(memory_space=pltpu.SEMAPHORE)
DATAFLOW = pltpu.SideEffectType.DATAFLOW_SIDE_EFFECTING


def _in_hbm(a):
    return pltpu.with_memory_space_constraint(a, pltpu.HBM)


def exchange_start(name, bufs, plan, n_copies, after=()):
    nb, na = len(bufs), len(after)

    def body(*refs):
        ins = refs[:nb]
        send_sems, recv_sems = refs[nb + na], refs[nb + na + 1]
        token = refs[-1]
        for k, (src, dst, peer, _) in enumerate(plan(ins)):
            pltpu.make_async_remote_copy(src_ref=src, dst_ref=dst, send_sem=send_sems.at[k], recv_sem=recv_sems.at[k],
                                         device_id=peer, device_id_type=MESH).start()
        token[...] = jnp.zeros_like(token)

    out = pl.pallas_call(
        body, name=name,
        out_shape=(pltpu.SemaphoreType.DMA((n_copies,)), pltpu.SemaphoreType.DMA((n_copies,)),
                   *[pltpu.HBM(b.shape, b.dtype) for b in bufs], jax.ShapeDtypeStruct((8, 128), F32)),
        in_specs=[pl.BlockSpec(memory_space=pltpu.HBM)] * nb + [HBM_SPEC] * na,
        out_specs=(SEM_SPEC, SEM_SPEC, *[pl.BlockSpec(memory_space=pltpu.HBM)] * nb,
                   pl.BlockSpec(memory_space=pltpu.VMEM)),
        input_output_aliases={i: 2 + i for i in range(nb)},
        compiler_params=pltpu.CompilerParams(has_side_effects=DATAFLOW),
    )(*[_in_hbm(b) for b in bufs], *after)
    return out[0], out[1], list(out[2:2 + nb]), out[-1]


def exchange_wait(name, send_sems, recv_sems, bufs, plan, after=()):
    nb, na = len(bufs), len(after)

    def body(*refs):
        ins = refs[:nb]
        send_ref, recv_ref = refs[nb], refs[nb + 1]
        for k, (src, dst, peer, landed) in enumerate(plan(ins)):
            pltpu.make_async_remote_copy(src_ref=src, dst_ref=dst, send_sem=send_ref.at[k], recv_sem=recv_ref.at[k],
                                         device_id=peer, device_id_type=MESH).wait_send()
            pltpu.make_async_remote_copy(src_ref=src, dst_ref=landed, send_sem=send_ref.at[k], recv_sem=recv_ref.at[k],
                                         device_id=peer, device_id_type=MESH).wait_recv()

    out = pl.pallas_call(
        body, name=name,
        out_shape=tuple(pltpu.HBM(b.shape, b.dtype) for b in bufs),
        in_specs=[pl.BlockSpec(memory_space=pltpu.HBM)] * nb + [SEM_SPEC, SEM_SPEC] + [HBM_SPEC] * na,
        out_specs=tuple([pl.BlockSpec(memory_space=pltpu.HBM)] * nb),
        input_output_aliases={i: i for i in range(nb)},
        compiler_params=pltpu.CompilerParams(has_side_effects=DATAFLOW),
    )(*bufs, send_sems, recv_sems, *after)
    return list(out)


def _gather_plan(n):
    def plan(bufs):
        x, y, c, _ = _place()
        me = 4 * x + 2 * y + c
        copies = []
        for a in range(n):
            for k in range(1, N_DEV):
                px, py, pc = x ^ (k >> 2), y ^ ((k >> 1) & 1), c ^ (k & 1)
                copies.append((bufs[a].at[me], bufs[a].at[me], (px, py, pc), bufs[a].at[4 * px + 2 * py + pc]))
        return copies
    return plan


def _sibling_plan(n):
    def plan(bufs):
        x, y, c, _ = _place()
        return [(bufs[a].at[:, 1 - c], bufs[n + a], (x, y, 1 - c), bufs[n + a]) for a in range(n)]
    return plan


def _chips_plan(n):
    def plan(bufs):
        _, _, c, chips = _place()
        return [(bufs[a].at[2 * cx + cy], bufs[n + a].at[j], (cx, cy, c), bufs[n + a].at[j])
                for a in range(n) for j, (cx, cy) in enumerate(chips)]
    return plan


def chip_sum(name, grad, landed, place, tr=256):
    _, _, rows, cols = grad.shape
    tr = min(tr, rows)

    def body(place_ref, g_ref, l_ref, own_ref, out_ref):
        total = g_ref[0, 0] + l_ref[0]
        out_ref[0] = total.astype(BF16)

        @pl.when(pl.program_id(1) == place_ref[1])
        def _():
            own_ref[...] = total

    return pl.pallas_call(
        body, name=name,
        grid_spec=pltpu.PrefetchScalarGridSpec(
            num_scalar_prefetch=1, grid=(rows // tr, 4),
            in_specs=[pl.BlockSpec((1, 1, tr, cols), lambda i, ch, pos: (ch, pos[0], i, 0)),
                      pl.BlockSpec((1, tr, cols), lambda i, ch, pos: (ch, i, 0))],
            out_specs=[pl.BlockSpec((tr, cols), lambda i, ch, pos: (i, 0)),
                       pl.BlockSpec((1, tr, cols), lambda i, ch, pos: (ch, i, 0))]),
        out_shape=[jax.ShapeDtypeStruct((rows, cols), F32), jax.ShapeDtypeStruct((4, rows, cols), BF16)],
        compiler_params=_params(("arbitrary", "arbitrary")),
    )(place, grad, landed)


BIG =("w_in", "w_sb_proj", "w_ca_proj", "w_out", "w_mlp_in", "w_mlp_out")
SMALL = ("b_gate", "rel_bias", "ln1_g", "ln1_b", "ln2_g", "ln2_b")
NAMES = ("w_in", "b_gate", "w_sb_proj", "w_ca_proj", "rel_bias", "w_out", "ln1_g", "ln1_b",
         "w_mlp_in", "w_mlp_out", "ln2_g", "ln2_b")
REL_COLS = 2 * REL_CLIP + 1


def _pack_small(t):
    rel = t["rel_bias"]
    if rel.shape[1] != REL_PAD:
        rel = jnp.pad(rel, ((0, 0), (0, REL_PAD - rel.shape[1])))
    flat = [t["b_gate"].reshape(-1), rel.reshape(-1)] + [t[n].reshape(-1) for n in SMALL[2:]]
    return jnp.concatenate(flat).reshape(-1, 128)


def _unpack_small(p):
    flat = p.reshape(-1)
    out, off = {}, 0
    for n, size in (("b_gate", GATE_COLS), ("rel_bias", 8 * REL_PAD)) + tuple((n, D_MODEL) for n in SMALL[2:]):
        out[n] = flat[off:off + size]
        off += size
    out["rel_bias"] = out["rel_bias"].reshape(8, REL_PAD)[:, :REL_COLS]
    return out


def kernel(x, w_in, b_gate, w_sb_proj, w_ca_proj, rel_bias, w_out, ln1_g, ln1_b, w_mlp_in, w_mlp_out, ln2_g, ln2_b, loss_target, m_w_in, m_b_gate, m_w_sb_proj, m_w_ca_proj, m_rel_bias, m_w_out, m_ln1_g, m_ln1_b, m_w_mlp_in, m_w_mlp_out, m_ln2_g, m_ln2_b, v_w_in, v_b_gate, v_w_sb_proj, v_w_ca_proj, v_rel_bias, v_w_out, v_ln1_g, v_ln1_b, v_w_mlp_in, v_w_mlp_out, v_ln2_g, v_ln2_b):
    w = dict(w_in=w_in, b_gate=b_gate, w_sb_proj=w_sb_proj, w_ca_proj=w_ca_proj, rel_bias=rel_bias, w_out=w_out,
             ln1_g=ln1_g, ln1_b=ln1_b, w_mlp_in=w_mlp_in, w_mlp_out=w_mlp_out, ln2_g=ln2_g, ln2_b=ln2_b)
    m = dict(w_in=m_w_in, b_gate=m_b_gate, w_sb_proj=m_w_sb_proj, w_ca_proj=m_w_ca_proj, rel_bias=m_rel_bias,
             w_out=m_w_out, ln1_g=m_ln1_g, ln1_b=m_ln1_b, w_mlp_in=m_w_mlp_in, w_mlp_out=m_w_mlp_out,
             ln2_g=m_ln2_g, ln2_b=m_ln2_b)
    v = dict(w_in=v_w_in, b_gate=v_b_gate, w_sb_proj=v_w_sb_proj, w_ca_proj=v_w_ca_proj, rel_bias=v_rel_bias,
             w_out=v_w_out, ln1_g=v_ln1_g, ln1_b=v_ln1_b, w_mlp_in=v_w_mlp_in, w_mlp_out=v_w_mlp_out,
             ln2_g=v_ln2_g, ln2_b=v_ln2_b)

    x, target = x[0], loss_target[0]
    c = lax.axis_index("c")
    place = jnp.stack([c, 2 * lax.axis_index("x") + lax.axis_index("y")]).astype(jnp.int32)
    grads, deltas, new_m, new_v = {}, {}, {}, {}

    def by_owner(n, g):
        return g.reshape((4, 2) + w[n].shape)

    def finish(n, own, arrived):
        grads[n], deltas[n], new_m[n], new_v[n] = adamw(
            "adamw_" + n, w[n], m[n], v[n], [(own, None), (arrived, 0), (arrived, 1), (arrived, 2)], tm=256)

    def reduce_start(tag, names, g):
        bufs = [by_owner(n, g[n]) for n in names]
        bufs += [_in_hbm(lax.empty((4,) + w[n].shape, F32)) for n in names]
        return exchange_start("reduce_%s_siblings_start" % tag, bufs, _sibling_plan(len(names)), len(names))

    def reduce_middle(tag, names, started, after):
        send, recv, bufs, _ = started
        k = len(names)
        bufs = exchange_wait("reduce_%s_siblings_wait" % tag, send, recv, bufs, _sibling_plan(k), after=after)
        own, sums = zip(*[chip_sum("chip_sum_" + n, bufs[a], bufs[k + a], place) for a, n in enumerate(names)])
        zones = [_in_hbm(lax.empty((3,) + w[n].shape, BF16)) for n in names]
        return own, exchange_start("reduce_%s_chips_start" % tag, list(sums) + zones, _chips_plan(k), 3 * k)

    def reduce_end(tag, names, own, started, after):
        send, recv, bufs, _ = started
        k = len(names)
        bufs = exchange_wait("reduce_%s_chips_wait" % tag, send, recv, bufs, _chips_plan(k), after=after)
        for a, n in enumerate(names):
            finish(n, own[a], bufs[k + a])

    rest = BIG[1:]
    wg_in, *zones = all_gather("gather_w_in", [w_in.astype(BF16)], seeds=[w[n].astype(BF16) for n in rest])
    send, recv, zones, token = exchange_start("gather_rest_start", zones, _gather_plan(len(rest)),
                                              (N_DEV - 1) * len(rest))
    qkv, lg, xb = in_proj(x, wg_in, after=(token,))
    bias_tiles = ca_bias_tiles(rel_bias)
    o_sb = sb_fwd(qkv)
    o_ca = ca_fwd(qkv, bias_tiles)
    zones = exchange_wait("gather_rest_wait", send, recv, zones, _gather_plan(len(rest)), after=(o_sb, o_ca))
    wg = dict(zip(rest, zones))
    wg_sb, wg_ca, wg_mi, wg_mo = wg["w_sb_proj"], wg["w_ca_proj"], wg["w_mlp_in"], wg["w_mlp_out"]
    w_out_b = wg["w_out"].reshape(D_MODEL, D_MODEL)

    u1 = mix_fwd(x, o_sb, o_ca, lg, b_gate, wg_sb, wg_ca, w_out_b)
    hm, a, x1b, du2, loss, dg2, db2 = mlp_fwd(u1, target, wg_mi, wg_mo, ln1_g, ln1_b, ln2_g, ln2_b)
    loss = lax.psum(loss[0, 0], ("x", "y", "c"))
    dhm, du1, dg1, db1 = mlp_bwd(du2, hm, u1, wg_mi, wg_mo, ln1_g)
    g = {}
    g["w_mlp_in"] = wgrad("wgrad_mlp_in", x1b, dhm, tm=D_MODEL, tn=FF_SHARD, shard_cols=FF_SHARD)
    g["w_mlp_out"] = wgrad("wgrad_mlp_out", a, du2, tm=FF_SHARD, tn=D_MODEL)

    mlp = ("w_mlp_in", "w_mlp_out")
    started = reduce_start("mlp", mlp, g)
    dlg, do_sb, do_ca, dy_sb, dy_ca, merged, dbg = mix_bwd(
        du1, o_sb, o_ca, lg, b_gate, wg_sb, wg_ca, w_out_b, after=(started[3],))
    own_mlp, started_mlp = reduce_middle("mlp", mlp, started, after=(dlg,))
    g["w_out"] = wgrad("wgrad_out", merged, du1, tm=D_MODEL, tn=D_MODEL, after=(started_mlp[3],))
    g["w_sb_proj"] = wgrad("wgrad_sb_proj", o_sb, dy_sb, tm=ATT_WIDTH, tn=D_MODEL, shard_cols=PROJ_SHARD)
    g["w_ca_proj"] = wgrad("wgrad_ca_proj", o_ca, dy_ca, tm=ATT_WIDTH, tn=D_MODEL, shard_cols=PROJ_SHARD)

    att = ("w_out", "w_sb_proj", "w_ca_proj")
    started = reduce_start("att", att, g)
    dq_sb, dk_sb, dv_sb = sb_bwd(qkv, o_sb, do_sb, after=(started[3],))
    own_att, started_att = reduce_middle("att", att, started, after=(dq_sb,))
    dq_ca, dk_ca, dv_ca, db_tiles = ca_bwd(qkv, bias_tiles, do_ca, after=(started_att[3],))
    g_rel = rel_bias_grad(db_tiles)

    dh = dh_slab([dq_sb, dk_sb, dv_sb, dq_ca, dk_ca, dv_ca, dlg])
    g["w_in"] = wgrad("wgrad_in", xb, dh, tm=D_MODEL, tn=IN_SHARD, shard_cols=IN_SHARD)
    last = ("w_in",)
    started = reduce_start("in", last, g)
    grad_x = in_bwd(du1, dh, wg_in, after=(started[3],))
    own_in, started_in = reduce_middle("in", last, started, after=(grad_x,))
    reduce_end("mlp", mlp, own_mlp, started_mlp, after=(started_in[3],))
    reduce_end("att", att, own_att, started_att, after=(started_in[3],))

    small = dict(b_gate=dbg, rel_bias=g_rel, ln1_g=dg1, ln1_b=db1, ln2_g=dg2, ln2_b=db2)
    parts = all_gather("gather_small_grads", [_pack_small(small)])[0]
    packed = adamw("adamw_small", _pack_small(w), _pack_small(m), _pack_small(v),
                   [(parts, d) for d in range(N_DEV)])
    for out, p in zip((grads, deltas, new_m, new_v), packed):
        for n, val in _unpack_small(p).items():
            out[n] = val.reshape(w[n].shape)
    reduce_end("in", last, own_in, started_in, after=(packed[0], deltas["w_mlp_in"], deltas["w_out"]))

    return (loss, grad_x[None], *[grads[n] for n in NAMES], *[deltas[n] for n in NAMES],
            *[new_m[n] for n in NAMES], *[new_v[n] for n in NAMES])
```

```python
import functools

import jax
import jax.numpy as jnp
from jax import lax
from jax.experimental import pallas as pl
from jax.experimental.pallas import tpu as pltpu

F32 = jnp.float32
BF16 = jnp.bfloat16
MESH = pl.DeviceIdType.MESH

N_DEV = 8
D_MODEL = 1024
HEAD_DIM = 64
ATT_WIDTH = 512
N_PAIRS = ATT_WIDTH // 128
QKV_COLS = 6 * ATT_WIDTH
GATE_COLS = 2 * D_MODEL
IN_COLS = QKV_COLS + GATE_COLS
IN_SHARD = IN_COLS // N_DEV
D_FF = 4 * D_MODEL
FF_SHARD = D_FF // N_DEV
PROJ_SHARD = D_MODEL // N_DEV
ATT_BLOCK = 128
CA_TILES = 5
CA_ROWS = 512
CHUNK = 64
CA_PREV_CHUNKS = 8
REL_CLIP = 256
REL_PAD = 640
ALPHA = 2.0 ** 0.25
LN_EPS = 1e-5
QK_SCALE = HEAD_DIM ** -0.5
NEG_BIG = -1e30
VMEM_LIMIT = 56 * 1024 * 1024

ADAM_LR = 0.001
ADAM_B1 = 0.9
ADAM_B2 = 0.999
ADAM_EPS = 1e-08
ADAM_WD = 0.01
ADAM_STEP = 10

_NT = (((1,), (1,)), ((), ()))
_TN = (((0,), (0,)), ((), ()))


def _dot(a, b):
    return jnp.dot(a, b, preferred_element_type=F32)


def _dot_nt(a, b):
    return lax.dot_general(a, b, _NT, preferred_element_type=F32)


def _dot_tn(a, b):
    return lax.dot_general(a, b, _TN, preferred_element_type=F32)


def _params(semantics=None):
    return pltpu.CompilerParams(dimension_semantics=semantics, vmem_limit_bytes=VMEM_LIMIT)


def _row(v):
    return v.reshape(1, -1)


HBM_SPEC = pl.BlockSpec(memory_space=pl.ANY)


def _behind(body, after):
    n = len(after)

    def wrapped(*refs):
        body(*refs[n:])

    return wrapped


def in_proj(x, wg_in, tm=512, after=()):
    s = x.shape[0]

    def body(x_ref, w_ref, qkv_ref, lg_ref, xb_ref):
        xb = x_ref[...].astype(BF16)
        xb_ref[...] = xb
        for j in range(N_DEV):
            acc = _dot(xb, w_ref[j])
            lo, hi = IN_SHARD * j, IN_SHARD * (j + 1)
            if hi <= QKV_COLS:
                qkv_ref[:, lo:hi] = acc.astype(BF16)
            elif lo >= QKV_COLS:
                lg_ref[:, lo - QKV_COLS:hi - QKV_COLS] = acc
            else:
                qkv_ref[:, lo:QKV_COLS] = acc[:, :QKV_COLS - lo].astype(BF16)
                lg_ref[:, 0:hi - QKV_COLS] = acc[:, QKV_COLS - lo:]

    return pl.pallas_call(
        _behind(body, after), name="in_proj", grid=(s // tm,),
        in_specs=[HBM_SPEC] * len(after) + [pl.BlockSpec((tm, D_MODEL), lambda i: (i, 0)),
                  pl.BlockSpec((N_DEV, D_MODEL, IN_SHARD), lambda i: (0, 0, 0))],
        out_specs=[pl.BlockSpec((tm, QKV_COLS), lambda i: (i, 0)),
                   pl.BlockSpec((tm, GATE_COLS), lambda i: (i, 0)),
                   pl.BlockSpec((tm, D_MODEL), lambda i: (i, 0))],
        out_shape=[jax.ShapeDtypeStruct((s, QKV_COLS), BF16),
                   jax.ShapeDtypeStruct((s, GATE_COLS), F32),
                   jax.ShapeDtypeStruct((s, D_MODEL), BF16)],
        compiler_params=_params(("arbitrary",)),
    )(*after, x, wg_in)


def _head_masks():
    lane = lax.broadcasted_iota(jnp.int32, (1, 128), 1)
    first = lane < HEAD_DIM
    return first, jnp.logical_not(first)


SB_CHUNK = 512
SB_SUB = SB_CHUNK // ATT_BLOCK


def _tri_pair():
    row = lax.broadcasted_iota(jnp.int32, (4 * ATT_BLOCK, 2 * ATT_BLOCK), 0) & (2 * ATT_BLOCK - 1)
    col = lax.broadcasted_iota(jnp.int32, (4 * ATT_BLOCK, 2 * ATT_BLOCK), 1)
    same = (row >= ATT_BLOCK) == (col >= ATT_BLOCK)
    return jnp.logical_and(same, row >= col).astype(BF16)


def _pair_bf16(pieces):
    return jnp.concatenate(
        [jnp.concatenate([p.astype(BF16) for p in row], axis=1) for row in pieces], axis=0)


def _suffix_sums(pieces, tri_pair):
    hi = [[p.astype(BF16) for p in row] for row in pieces]
    lo = [[p - h.astype(F32) for p, h in zip(row, hrow)] for row, hrow in zip(pieces, hi)]
    return _dot(jnp.concatenate([_pair_bf16(hi), _pair_bf16(lo)], axis=1), tri_pair)


def _first_row_piece(diag, j):
    return j if diag else 0


def _piece(v, r, h):
    return v[ATT_BLOCK * r:ATT_BLOCK * (r + 1), ATT_BLOCK * h:ATT_BLOCK * (h + 1)]


def _all_row_pieces(slabs):
    return [[slab[ATT_BLOCK * r:ATT_BLOCK * (r + 1)] for r in range(SB_SUB)] for slab in slabs]


def _add_rows(total, part, r0):
    if r0 == 0:
        return total + part
    return jnp.concatenate([total[:ATT_BLOCK * r0], total[ATT_BLOCK * r0:] + part], axis=0)


def _row_totals(csum, r0, h):
    return jnp.concatenate(
        [jnp.broadcast_to(_piece(csum, at, h)[:, 0:1], (ATT_BLOCK, ATT_BLOCK)) for at in range(SB_SUB - r0)], axis=0)


SIGN_BIT = 0x80000000


def _drop(z):
    minus_abs = lax.bitcast_convert_type(lax.bitcast_convert_type(z, jnp.uint32) | jnp.uint32(SIGN_BIT), F32)
    return jnp.maximum(z, 0.0) + jnp.log(1.0 + jnp.exp(minus_abs))


def _sb_weights(z, runs, r0, masked, strict, tri_pair):
    drops = []
    for r in range(r0, SB_SUB):
        row = [_drop(_piece(z, r - r0, h)) for h in range(2)]
        drops.append([jnp.where(strict, drop, 0.0) for drop in row] if r == masked else row)
    csum = _suffix_sums(drops, tri_pair)
    weights = []
    for r in range(r0, SB_SUB):
        row = [jnp.exp(_piece(z, r - r0, h) - (_piece(csum, r - r0, h) + runs[h][r])) for h in range(2)]
        weights.append([jnp.where(strict, a, 0.0) for a in row] if r == masked else row)
    return drops, weights, csum


def _by_head(blk, masks):
    zero = jnp.zeros_like(blk)
    return jnp.concatenate([jnp.where(m, blk, zero) for m in masks], axis=0)


def sb_fwd(qkv):
    s = qkv.shape[0]
    nq = s // SB_CHUNK

    def body(q_ref, k_ref, v_ref, o_ref, run_s):
        i = pl.program_id(1)
        masks = _head_masks()
        q = q_ref[...] * QK_SCALE
        row = lax.broadcasted_iota(jnp.int32, (ATT_BLOCK, ATT_BLOCK), 0)
        col = lax.broadcasted_iota(jnp.int32, (ATT_BLOCK, ATT_BLOCK), 1)
        strict = col < row
        tri_pair = _tri_pair()
        run_s[...] = jnp.zeros_like(run_s)
        o_ref[...] = jnp.zeros_like(o_ref)

        def chunk(cb, diag):
            start = cb * SB_CHUNK
            runs = [run_s[0], run_s[1]]
            acc = jnp.zeros((SB_CHUNK, ATT_BLOCK), F32)
            for j in reversed(range(SB_SUB)):
                r0 = _first_row_piece(diag, j)
                keys = pl.ds(pl.multiple_of(start + ATT_BLOCK * j, ATT_BLOCK), ATT_BLOCK)
                z = _dot_nt(q[ATT_BLOCK * r0:], _by_head(k_ref[keys, :], masks))
                _, weights, csum = _sb_weights(z, _all_row_pieces(runs), r0, j if diag else None, strict, tri_pair)
                part = _dot(_pair_bf16(weights), _by_head(v_ref[keys, :], masks))
                acc = _add_rows(acc, part, r0)
                runs = [_add_rows(runs[h], _row_totals(csum, r0, h), r0) for h in range(2)]
            run_s[0], run_s[1] = runs
            o_ref[...] += acc

        chunk(i, True)

        def step(it, carry):
            chunk(i - 1 - it, False)
            return carry

        lax.fori_loop(0, i, step, 0)

    return pl.pallas_call(
        body, name="sb_fwd", grid=(N_PAIRS, nq),
        in_specs=[pl.BlockSpec((SB_CHUNK, 128), lambda p, i: (i, p)),
                  pl.BlockSpec((s, 128), lambda p, i: (0, N_PAIRS + p)),
                  pl.BlockSpec((s, 128), lambda p, i: (0, 2 * N_PAIRS + p))],
        out_specs=pl.BlockSpec((SB_CHUNK, 128), lambda p, i: (i, p)),
        out_shape=jax.ShapeDtypeStruct((s, ATT_WIDTH), F32),
        scratch_shapes=[pltpu.VMEM((2, SB_CHUNK, 128), F32)],
        compiler_params=_params(("arbitrary", "arbitrary")),
    )(qkv, qkv, qkv)


def sb_bwd(qkv, o_sb, do_sb, after=()):
    s = qkv.shape[0]
    nq = s // SB_CHUNK

    def body(q_ref, k_ref, v_ref, o_ref, do_ref, dq_ref, dk_ref, dv_ref, run_l_s, run_e_s, etot_s, dq_s):
        i = pl.program_id(1)

        @pl.when(i == 0)
        def _():
            dk_ref[...] = jnp.zeros_like(dk_ref)
            dv_ref[...] = jnp.zeros_like(dv_ref)

        masks = _head_masks()
        q = q_ref[...] * QK_SCALE
        do = do_ref[...]
        prod = do.astype(F32) * o_ref[...]
        for h in range(2):
            total = jnp.sum(jnp.where(masks[h], prod, 0.0), axis=1, keepdims=True)
            etot_s[h] = jnp.broadcast_to(total, (SB_CHUNK, ATT_BLOCK))
        row = lax.broadcasted_iota(jnp.int32, (ATT_BLOCK, ATT_BLOCK), 0)
        col = lax.broadcasted_iota(jnp.int32, (ATT_BLOCK, ATT_BLOCK), 1)
        strict = col < row
        tri_pair = _tri_pair()
        run_l_s[...] = jnp.zeros_like(run_l_s)
        run_e_s[...] = jnp.zeros_like(run_e_s)
        dq_s[...] = jnp.zeros_like(dq_s)

        def chunk(cb, diag):
            start = cb * SB_CHUNK
            slabs_l = [run_l_s[0], run_l_s[1]]
            slabs_e = [run_e_s[0], run_e_s[1]]
            e_tot = _all_row_pieces([etot_s[0], etot_s[1]])
            dq = jnp.zeros((SB_CHUNK, ATT_BLOCK), F32)
            for j in reversed(range(SB_SUB)):
                runs_l, runs_e = _all_row_pieces(slabs_l), _all_row_pieces(slabs_e)
                r0 = _first_row_piece(diag, j)
                masked = j if diag else None
                keys = pl.ds(pl.multiple_of(start + ATT_BLOCK * j, ATT_BLOCK), ATT_BLOCK)
                kk = _by_head(k_ref[keys, :], masks)
                vv = _by_head(v_ref[keys, :], masks)
                q_rows, do_rows = q[ATT_BLOCK * r0:], do[ATT_BLOCK * r0:]
                z = _dot_nt(q_rows, kk)
                da = _dot_nt(do_rows, vv)
                drops, weights, csum = _sb_weights(z, runs_l, r0, masked, strict, tri_pair)
                ab = [[a.astype(BF16) for a in wrow] for wrow in weights]
                es = [[a.astype(F32) * _piece(da, at, h) for h, a in enumerate(arow)] for at, arow in enumerate(ab)]
                esum = _suffix_sums(es, tri_pair)
                dzs = []
                for at, r in enumerate(range(r0, SB_SUB)):
                    dz_row = []
                    for h in range(2):
                        sig = jnp.exp(_piece(z, at, h) - drops[at][h])
                        after = _piece(esum, at, h) + runs_e[h][r] - es[at][h]
                        dz = es[at][h] - sig * (e_tot[h][r] - after)
                        dz_row.append(jnp.where(strict, dz, 0.0) if r == masked else dz)
                    dzs.append(dz_row)
                dz = _pair_bf16(dzs)
                dq = _add_rows(dq, _dot(dz, kk), r0)
                dk = _dot_tn(dz, q_rows)
                dv = _dot_tn(_pair_bf16(ab), do_rows)
                dk_ref[keys, :] += jnp.where(masks[0], dk[:ATT_BLOCK], dk[ATT_BLOCK:])
                dv_ref[keys, :] += jnp.where(masks[0], dv[:ATT_BLOCK], dv[ATT_BLOCK:])
                slabs_l = [_add_rows(slabs_l[h], _row_totals(csum, r0, h), r0) for h in range(2)]
                slabs_e = [_add_rows(slabs_e[h], _row_totals(esum, r0, h), r0) for h in range(2)]
            run_l_s[0], run_l_s[1] = slabs_l
            run_e_s[0], run_e_s[1] = slabs_e
            dq_s[...] += dq

        chunk(i, True)

        def step(it, carry):
            chunk(i - 1 - it, False)
            return carry

        lax.fori_loop(0, i, step, 0)
        dq_ref[...] = (dq_s[...] * QK_SCALE).astype(BF16)

    tile = lambda p, i: (i, p)
    state = pltpu.VMEM((2, SB_CHUNK, 128), F32)
    return pl.pallas_call(
        _behind(body, after), name="sb_bwd", grid=(N_PAIRS, nq),
        in_specs=[HBM_SPEC] * len(after) + [pl.BlockSpec((SB_CHUNK, 128), tile),
                  pl.BlockSpec((s, 128), lambda p, i: (0, N_PAIRS + p)),
                  pl.BlockSpec((s, 128), lambda p, i: (0, 2 * N_PAIRS + p)),
                  pl.BlockSpec((SB_CHUNK, 128), tile),
                  pl.BlockSpec((SB_CHUNK, 128), tile)],
        out_specs=[pl.BlockSpec((SB_CHUNK, 128), tile),
                   pl.BlockSpec((s, 128), lambda p, i: (0, p)),
                   pl.BlockSpec((s, 128), lambda p, i: (0, p))],
        out_shape=[jax.ShapeDtypeStruct((s, ATT_WIDTH), BF16),
                   jax.ShapeDtypeStruct((s, ATT_WIDTH), F32),
                   jax.ShapeDtypeStruct((s, ATT_WIDTH), F32)],
        scratch_shapes=[state, state, state, pltpu.VMEM((SB_CHUNK, 128), F32)],
        compiler_params=_params(("arbitrary", "arbitrary")),
    )(*after, qkv, qkv, qkv, o_sb, do_sb)


def ca_bias_tiles(rel_bias):
    heads = rel_bias.shape[0]
    wide = 2 * ATT_BLOCK

    def body(rel_ref, out_ref):
        bins = lax.broadcasted_iota(jnp.int32, (REL_PAD, wide), 0)
        pos = lax.broadcasted_iota(jnp.int32, (REL_PAD, wide), 1)
        shift = (ATT_BLOCK - 1) - lax.broadcasted_iota(jnp.int32, (ATT_BLOCK, wide), 0)
        r = lax.broadcasted_iota(jnp.int32, (ATT_BLOCK, ATT_BLOCK), 0)
        c = lax.broadcasted_iota(jnp.int32, (ATT_BLOCK, ATT_BLOCK), 1)
        for j in range(CA_TILES):
            hot = (jnp.clip(ATT_BLOCK * (j + 1) - 1 - pos, -REL_CLIP, REL_CLIP) + REL_CLIP == bins).astype(F32)
            lines = jnp.dot(rel_ref[...], hot, preferred_element_type=F32, precision=lax.Precision.HIGHEST)
            back = 2 * j + (r >> 6) - (c >> 6)
            ok = jnp.logical_and(back >= 0, back <= CA_PREV_CHUNKS)
            for h in range(heads):
                tile = jnp.broadcast_to(lines[h:h + 1, :], (ATT_BLOCK, wide))
                for bit in range(7):
                    rolled = pltpu.roll(tile, wide - (1 << bit), 1)
                    tile = jnp.where((shift >> bit) & 1 == 1, rolled, tile)
                out_ref[h, j] = jnp.where(ok, tile[:, :ATT_BLOCK], NEG_BIG)

    padded = jnp.pad(rel_bias, ((0, 0), (0, REL_PAD - rel_bias.shape[1])))
    return pl.pallas_call(
        body, name="ca_bias_tiles",
        out_shape=jax.ShapeDtypeStruct((heads, CA_TILES, ATT_BLOCK, ATT_BLOCK), F32),
        compiler_params=_params(),
    )(padded)


CA_BLOCKS = CA_ROWS // ATT_BLOCK
CA_KEY_BLOCKS = CA_BLOCKS + CA_TILES - 1


def _ca_users(t):
    return [r for r in range(CA_BLOCKS) if 0 <= r + CA_TILES - 1 - t < CA_TILES]


def _ca_rows(v, t, r):
    at = _ca_users(t).index(r)
    return v[ATT_BLOCK * at:ATT_BLOCK * (at + 1)]


def _ca_window(step, k_ref, v_ref, masks):
    kks, vvs, inside, rows = [], [], [], []
    for t in range(CA_KEY_BLOCKS):
        block = step * CA_BLOCKS - (CA_TILES - 1) + t
        keys = pl.ds(pl.multiple_of(jnp.maximum(block, 0) * ATT_BLOCK, ATT_BLOCK), ATT_BLOCK)
        kks.append(_by_head(k_ref[keys, :], masks))
        vvs.append(_by_head(v_ref[keys, :], masks))
        inside.append(block >= 0)
        rows.append(keys)
    return kks, vvs, inside, rows


def _ca_probs(q, kks, inside, bias_ref):
    scores = [[[None] * CA_TILES for _ in range(2)] for _ in range(CA_BLOCKS)]
    for t in range(CA_KEY_BLOCKS):
        users = _ca_users(t)
        z = _dot_nt(q[ATT_BLOCK * users[0]:ATT_BLOCK * (users[-1] + 1)], kks[t])
        for r in users:
            j = r + CA_TILES - 1 - t
            for h in range(2):
                zz = _ca_rows(z, t, r)[:, ATT_BLOCK * h:ATT_BLOCK * (h + 1)] + bias_ref[h, j]
                scores[r][h][j] = jnp.where(inside[t], zz, NEG_BIG)
    probs = [[None, None] for _ in range(CA_BLOCKS)]
    for r in range(CA_BLOCKS):
        for h in range(2):
            m = jnp.max(functools.reduce(jnp.maximum, scores[r][h]), axis=1, keepdims=True)
            ex = [jnp.exp(z - m) for z in scores[r][h]]
            inv = 1.0 / jnp.sum(functools.reduce(jnp.add, ex), axis=1, keepdims=True)
            probs[r][h] = [e * inv for e in ex]
    return probs


def _ca_stack(tiles, t):
    return jnp.concatenate(
        [jnp.concatenate([tiles[r][h][r + CA_TILES - 1 - t].astype(BF16) for h in range(2)], axis=1)
         for r in _ca_users(t)], axis=0)


def ca_fwd(qkv, bias_tiles):
    s = qkv.shape[0]
    nq = s // CA_ROWS

    def body(q_ref, k_ref, v_ref, bias_ref, o_ref):
        masks = _head_masks()
        kks, vvs, inside, _ = _ca_window(pl.program_id(1), k_ref, v_ref, masks)
        probs = _ca_probs(q_ref[...] * QK_SCALE, kks, inside, bias_ref)
        out = [None] * CA_BLOCKS
        for t in range(CA_KEY_BLOCKS):
            part = _dot(_ca_stack(probs, t), vvs[t])
            for r in _ca_users(t):
                piece = _ca_rows(part, t, r)
                out[r] = piece if out[r] is None else out[r] + piece
        o_ref[...] = jnp.concatenate(out, axis=0)

    return pl.pallas_call(
        body, name="ca_fwd", grid=(N_PAIRS, nq),
        in_specs=[pl.BlockSpec((CA_ROWS, 128), lambda p, i: (i, 3 * N_PAIRS + p)),
                  pl.BlockSpec((s, 128), lambda p, i: (0, 4 * N_PAIRS + p)),
                  pl.BlockSpec((s, 128), lambda p, i: (0, 5 * N_PAIRS + p)),
                  pl.BlockSpec((2, CA_TILES, ATT_BLOCK, ATT_BLOCK), lambda p, i: (p, 0, 0, 0))],
        out_specs=pl.BlockSpec((CA_ROWS, 128), lambda p, i: (i, p)),
        out_shape=jax.ShapeDtypeStruct((s, ATT_WIDTH), F32),
        compiler_params=_params(("arbitrary", "arbitrary")),
    )(qkv, qkv, qkv, bias_tiles)


def ca_bwd(qkv, bias_tiles, do_ca, after=()):
    s = qkv.shape[0]
    nq = s // CA_ROWS

    def body(q_ref, k_ref, v_ref, bias_ref, do_ref, dq_ref, dk_ref, dv_ref, db_ref):
        big = pl.program_id(1)

        @pl.when(big == 0)
        def _():
            dk_ref[...] = jnp.zeros_like(dk_ref)
            dv_ref[...] = jnp.zeros_like(dv_ref)
            db_ref[...] = jnp.zeros_like(db_ref)

        masks = _head_masks()
        kks, vvs, inside, key_rows = _ca_window(big, k_ref, v_ref, masks)
        q = q_ref[...] * QK_SCALE
        do = do_ref[...]
        probs = _ca_probs(q, kks, inside, bias_ref)
        dps = [[[None] * CA_TILES for _ in range(2)] for _ in range(CA_BLOCKS)]
        for t in range(CA_KEY_BLOCKS):
            users = _ca_users(t)
            dp = _dot_nt(do[ATT_BLOCK * users[0]:ATT_BLOCK * (users[-1] + 1)], vvs[t])
            for r in users:
                for h in range(2):
                    dps[r][h][r + CA_TILES - 1 - t] = _ca_rows(dp, t, r)[:, ATT_BLOCK * h:ATT_BLOCK * (h + 1)]
        dss = [[None, None] for _ in range(CA_BLOCKS)]
        for r in range(CA_BLOCKS):
            for h in range(2):
                delta = jnp.sum(functools.reduce(jnp.add, [p * dp for p, dp in zip(probs[r][h], dps[r][h])]),
                                axis=1, keepdims=True)
                dss[r][h] = [p * (dp - delta) for p, dp in zip(probs[r][h], dps[r][h])]
        for h in range(2):
            for j in range(CA_TILES):
                db_ref[h, j] += functools.reduce(jnp.add, [dss[r][h][j] for r in range(CA_BLOCKS)])
        dq = [None] * CA_BLOCKS
        for t in range(CA_KEY_BLOCKS):
            users = _ca_users(t)
            rows = slice(ATT_BLOCK * users[0], ATT_BLOCK * (users[-1] + 1))
            ds = _ca_stack(dss, t)
            part = _dot(ds, kks[t])
            for r in users:
                piece = _ca_rows(part, t, r)
                dq[r] = piece if dq[r] is None else dq[r] + piece
            dk = _dot_tn(ds, q[rows])
            dv = _dot_tn(_ca_stack(probs, t), do[rows])
            dk_ref[key_rows[t], :] += jnp.where(masks[0], dk[:ATT_BLOCK], dk[ATT_BLOCK:])
            dv_ref[key_rows[t], :] += jnp.where(masks[0], dv[:ATT_BLOCK], dv[ATT_BLOCK:])
        dq_ref[...] = (jnp.concatenate(dq, axis=0) * QK_SCALE).astype(BF16)

    return pl.pallas_call(
        _behind(body, after), name="ca_bwd", grid=(N_PAIRS, nq),
        in_specs=[HBM_SPEC] * len(after) + [pl.BlockSpec((CA_ROWS, 128), lambda p, i: (i, 3 * N_PAIRS + p)),
                  pl.BlockSpec((s, 128), lambda p, i: (0, 4 * N_PAIRS + p)),
                  pl.BlockSpec((s, 128), lambda p, i: (0, 5 * N_PAIRS + p)),
                  pl.BlockSpec((2, CA_TILES, ATT_BLOCK, ATT_BLOCK), lambda p, i: (p, 0, 0, 0)),
                  pl.BlockSpec((CA_ROWS, 128), lambda p, i: (i, p))],
        out_specs=[pl.BlockSpec((CA_ROWS, 128), lambda p, i: (i, p)),
                   pl.BlockSpec((s, 128), lambda p, i: (0, p)),
                   pl.BlockSpec((s, 128), lambda p, i: (0, p)),
                   pl.BlockSpec((2, CA_TILES, ATT_BLOCK, ATT_BLOCK), lambda p, i: (p, 0, 0, 0))],
        out_shape=[jax.ShapeDtypeStruct((s, ATT_WIDTH), BF16),
                   jax.ShapeDtypeStruct((s, ATT_WIDTH), F32),
                   jax.ShapeDtypeStruct((s, ATT_WIDTH), F32),
                   jax.ShapeDtypeStruct((2 * N_PAIRS, CA_TILES, ATT_BLOCK, ATT_BLOCK), F32)],
        compiler_params=_params(("arbitrary", "arbitrary")),
    )(*after, qkv, qkv, qkv, bias_tiles, do_ca)


def rel_bias_grad(db_tiles):
    heads = db_tiles.shape[0]

    def body(db_ref, out_ref):
        row = lax.broadcasted_iota(jnp.int32, (ATT_BLOCK, ATT_BLOCK), 0)
        col = lax.broadcasted_iota(jnp.int32, (ATT_BLOCK, ATT_BLOCK), 1)
        cp = lax.broadcasted_iota(jnp.int32, (ATT_BLOCK, REL_PAD), 0)
        bins = lax.broadcasted_iota(jnp.int32, (ATT_BLOCK, REL_PAD), 1)
        total = jnp.zeros((heads, REL_PAD), F32)
        for j in range(CA_TILES):
            sums_lo, sums_up = [], []
            for h in range(heads):
                tile = db_ref[h, j]
                for bit in range(7):
                    rolled = pltpu.roll(tile, ATT_BLOCK - (1 << bit), 1)
                    tile = jnp.where((row >> bit) & 1 == 1, rolled, tile)
                was_lower = (col + row >= ATT_BLOCK) | (col == 0)
                sums_lo.append(jnp.sum(jnp.where(was_lower, tile, 0.0), axis=0, keepdims=True))
                sums_up.append(jnp.sum(jnp.where(was_lower, 0.0, tile), axis=0, keepdims=True))
            lo = jnp.concatenate(sums_lo, axis=0)
            up = jnp.concatenate(sums_up, axis=0)
            d_lo = jnp.where(cp == 0, 0, ATT_BLOCK - cp)
            hot_lo = (jnp.clip(ATT_BLOCK * j + d_lo, -REL_CLIP, REL_CLIP) + REL_CLIP == bins).astype(F32)
            hot_up = (jnp.clip(ATT_BLOCK * j - cp, -REL_CLIP, REL_CLIP) + REL_CLIP == bins).astype(F32)
            total = total + jnp.dot(lo, hot_lo, preferred_element_type=F32, precision=lax.Precision.HIGHEST)
            total = total + jnp.dot(up, hot_up, preferred_element_type=F32, precision=lax.Precision.HIGHEST)
        out_ref[...] = total

    return pl.pallas_call(
        body, name="rel_bias_grad",
        out_shape=jax.ShapeDtypeStruct((heads, REL_PAD), F32),
        compiler_params=_params(),
    )(db_tiles)


def _assemble_cols(dst_ref, src_ref):
    w = src_ref.shape[2]
    for j in range(N_DEV):
        dst_ref[:, w * j:w * (j + 1)] = src_ref[j]


def _merge(o_sb, o_ca, lg, bg, wsb, wca):
    y_sb = _dot(o_sb.astype(BF16), wsb)
    y_ca = _dot(o_ca.astype(BF16), wca)
    gates = jax.nn.sigmoid(lg + bg)
    g_sb, g_ca = gates[:, :D_MODEL], gates[:, D_MODEL:]
    return y_sb, y_ca, g_sb, g_ca, g_sb * y_sb + g_ca * y_ca


def mix_fwd(x, o_sb, o_ca, lg, b_gate, wg_sb, wg_ca, w_out, tm=256):
    s = x.shape[0]

    def body(x_ref, osb_ref, oca_ref, lg_ref, bg_ref, wsb_ref, wca_ref, wout_ref, u_ref, wsb_s, wca_s):
        @pl.when(pl.program_id(0) == 0)
        def _():
            _assemble_cols(wsb_s, wsb_ref)
            _assemble_cols(wca_s, wca_ref)

        merged = _merge(osb_ref[...], oca_ref[...], lg_ref[...], bg_ref[...], wsb_s[...], wca_s[...])[4]
        u_ref[...] = ALPHA * x_ref[...] + _dot(merged.astype(BF16), wout_ref[...])

    rows = lambda w: pl.BlockSpec((tm, w), lambda i: (i, 0))
    whole = lambda shape: pl.BlockSpec(shape, lambda i: (0,) * len(shape))
    return pl.pallas_call(
        body, name="mix_fwd", grid=(s // tm,),
        in_specs=[rows(D_MODEL), rows(ATT_WIDTH), rows(ATT_WIDTH), rows(GATE_COLS), whole((1, GATE_COLS)),
                  whole((N_DEV, ATT_WIDTH, PROJ_SHARD)), whole((N_DEV, ATT_WIDTH, PROJ_SHARD)),
                  whole((D_MODEL, D_MODEL))],
        out_specs=rows(D_MODEL),
        out_shape=jax.ShapeDtypeStruct((s, D_MODEL), F32),
        scratch_shapes=[pltpu.VMEM((ATT_WIDTH, D_MODEL), BF16), pltpu.VMEM((ATT_WIDTH, D_MODEL), BF16)],
        compiler_params=_params(("arbitrary",)),
    )(x, o_sb, o_ca, lg, _row(b_gate), wg_sb, wg_ca, w_out)


def mix_bwd(du1, o_sb, o_ca, lg, b_gate, wg_sb, wg_ca, w_out, tm=256, after=()):
    s = du1.shape[0]

    def body(du_ref, osb_ref, oca_ref, lg_ref, bg_ref, wsb_ref, wca_ref, wout_ref,
             dlg_ref, dosb_ref, doca_ref, dysb_ref, dyca_ref, mg_ref, dbg_ref, wsb_s, wca_s):
        @pl.when(pl.program_id(0) == 0)
        def _():
            _assemble_cols(wsb_s, wsb_ref)
            _assemble_cols(wca_s, wca_ref)
            dbg_ref[...] = jnp.zeros_like(dbg_ref)

        y_sb, y_ca, g_sb, g_ca, merged = _merge(
            osb_ref[...], oca_ref[...], lg_ref[...], bg_ref[...], wsb_s[...], wca_s[...])
        mg_ref[...] = merged.astype(BF16)
        dm = _dot_nt(du_ref[...].astype(BF16), wout_ref[...])
        dl_sb = dm * y_sb * g_sb * (1.0 - g_sb)
        dl_ca = dm * y_ca * g_ca * (1.0 - g_ca)
        dlg_ref[:, :D_MODEL] = dl_sb.astype(BF16)
        dlg_ref[:, D_MODEL:] = dl_ca.astype(BF16)
        dbg_ref[:, :D_MODEL] += jnp.sum(dl_sb, axis=0, keepdims=True)
        dbg_ref[:, D_MODEL:] += jnp.sum(dl_ca, axis=0, keepdims=True)
        dy_sb = (dm * g_sb).astype(BF16)
        dy_ca = (dm * g_ca).astype(BF16)
        dysb_ref[...] = dy_sb
        dyca_ref[...] = dy_ca
        dosb_ref[...] = _dot_nt(dy_sb, wsb_s[...]).astype(BF16)
        doca_ref[...] = _dot_nt(dy_ca, wca_s[...]).astype(BF16)

    rows = lambda w: pl.BlockSpec((tm, w), lambda i: (i, 0))
    whole = lambda shape: pl.BlockSpec(shape, lambda i: (0,) * len(shape))
    return pl.pallas_call(
        _behind(body, after), name="mix_bwd", grid=(s // tm,),
        in_specs=[HBM_SPEC] * len(after) + [rows(D_MODEL), rows(ATT_WIDTH), rows(ATT_WIDTH), rows(GATE_COLS), whole((1, GATE_COLS)),
                  whole((N_DEV, ATT_WIDTH, PROJ_SHARD)), whole((N_DEV, ATT_WIDTH, PROJ_SHARD)),
                  whole((D_MODEL, D_MODEL))],
        out_specs=[rows(GATE_COLS), rows(ATT_WIDTH), rows(ATT_WIDTH), rows(D_MODEL), rows(D_MODEL),
                   rows(D_MODEL), whole((1, GATE_COLS))],
        out_shape=[jax.ShapeDtypeStruct((s, GATE_COLS), BF16),
                   jax.ShapeDtypeStruct((s, ATT_WIDTH), BF16), jax.ShapeDtypeStruct((s, ATT_WIDTH), BF16),
                   jax.ShapeDtypeStruct((s, D_MODEL), BF16), jax.ShapeDtypeStruct((s, D_MODEL), BF16),
                   jax.ShapeDtypeStruct((s, D_MODEL), BF16), jax.ShapeDtypeStruct((1, GATE_COLS), F32)],
        scratch_shapes=[pltpu.VMEM((ATT_WIDTH, D_MODEL), BF16), pltpu.VMEM((ATT_WIDTH, D_MODEL), BF16)],
        compiler_params=_params(("arbitrary",)),
    )(*after, du1, o_sb, o_ca, lg, _row(b_gate), wg_sb, wg_ca, w_out)


def _ln_stats(u):
    mu = jnp.mean(u, axis=1, keepdims=True)
    cen = u - mu
    var = jnp.mean(cen * cen, axis=1, keepdims=True)
    rstd = lax.rsqrt(var + LN_EPS)
    return cen * rstd, rstd


def _ln_bwd(dy, xhat, rstd, gain):
    dyg = dy * gain
    m1 = jnp.mean(dyg, axis=1, keepdims=True)
    m2 = jnp.mean(dyg * xhat, axis=1, keepdims=True)
    return rstd * (dyg - m1 - xhat * m2)


def _load_mlp_weights(win_hbm, wout_hbm, win_s, wout_s, sems):
    copies = [pltpu.make_async_copy(win_hbm.at[k], win_s.at[:, pl.ds(FF_SHARD * k, FF_SHARD)], sems.at[k])
              for k in range(N_DEV)]
    copies.append(pltpu.make_async_copy(wout_hbm, wout_s, sems.at[N_DEV]))
    for cp in copies:
        cp.start()
    return copies


MLP_WEIGHT_SCRATCH = [pltpu.VMEM((D_MODEL, D_FF), BF16), pltpu.VMEM((D_FF, D_MODEL), BF16),
                      pltpu.SemaphoreType.DMA((N_DEV + 1,))]


def mlp_fwd(u1, target, wg_in, w_out, ln1_g, ln1_b, ln2_g, ln2_b, tm=256):
    s = u1.shape[0]

    def body(u_ref, t_ref, win_hbm, wout_hbm, g1_ref, b1_ref, g2_ref, b2_ref,
             hm_ref, a_ref, x1b_ref, du2_ref, loss_ref, dg2_ref, db2_ref, win_s, wout_s, sems):
        first = pl.program_id(0) == 0

        @pl.when(first)
        def _():
            copies = _load_mlp_weights(win_hbm, wout_hbm, win_s, wout_s, sems)
            loss_ref[...] = jnp.zeros_like(loss_ref)
            dg2_ref[...] = jnp.zeros_like(dg2_ref)
            db2_ref[...] = jnp.zeros_like(db2_ref)
            for cp in copies:
                cp.wait()

        xhat, _ = _ln_stats(u_ref[...])
        x1 = xhat * g1_ref[...] + b1_ref[...]
        x1b = x1.astype(BF16)
        x1b_ref[...] = x1b
        for k in range(N_DEV):
            cols = slice(FF_SHARD * k, FF_SHARD * (k + 1))
            hm = _dot(x1b, win_s[:, cols])
            hm_ref[:, cols] = hm
            r = jnp.maximum(hm, 0.0)
            a_ref[:, cols] = (r * r).astype(BF16)
        xhat, rstd = _ln_stats(ALPHA * x1 + _dot(a_ref[...], wout_s[...]))
        diff = xhat * g2_ref[...] + b2_ref[...] - t_ref[...]
        per_token = jnp.mean(diff * diff, axis=1, keepdims=True)
        loss_ref[...] += 0.5 * jnp.sum(per_token, axis=0, keepdims=True)
        dy = diff * (1.0 / D_MODEL)
        dg2_ref[...] += jnp.sum(dy * xhat, axis=0, keepdims=True)
        db2_ref[...] += jnp.sum(dy, axis=0, keepdims=True)
        du2_ref[...] = _ln_bwd(dy, xhat, rstd, g2_ref[...])

    rows = pl.BlockSpec((tm, D_MODEL), lambda i: (i, 0))
    vec = pl.BlockSpec((1, D_MODEL), lambda i: (0, 0))
    ff = pl.BlockSpec((tm, D_FF), lambda i: (i, 0))
    return pl.pallas_call(
        body, name="mlp_fwd", grid=(s // tm,),
        in_specs=[rows, rows, HBM_SPEC, HBM_SPEC, vec, vec, vec, vec],
        out_specs=[ff, ff, rows, rows, pl.BlockSpec((1, 1), lambda i: (0, 0)), vec, vec],
        out_shape=[jax.ShapeDtypeStruct((s, D_FF), F32), jax.ShapeDtypeStruct((s, D_FF), BF16),
                   jax.ShapeDtypeStruct((s, D_MODEL), BF16), jax.ShapeDtypeStruct((s, D_MODEL), F32),
                   jax.ShapeDtypeStruct((1, 1), F32),
                   jax.ShapeDtypeStruct((1, D_MODEL), F32), jax.ShapeDtypeStruct((1, D_MODEL), F32)],
        scratch_shapes=MLP_WEIGHT_SCRATCH,
        compiler_params=_params(("arbitrary",)),
    )(u1, target, wg_in, w_out, _row(ln1_g), _row(ln1_b), _row(ln2_g), _row(ln2_b))


def mlp_bwd(du2, hm, u1, wg_in, w_out, ln1_g, tm=256):
    s = du2.shape[0]

    def body(du2_ref, hm_ref, u_ref, win_hbm, wout_hbm, g1_ref,
             dhm_ref, du1_ref, dg1_ref, db1_ref, win_s, wout_s, sems):
        first = pl.program_id(0) == 0

        @pl.when(first)
        def _():
            copies = _load_mlp_weights(win_hbm, wout_hbm, win_s, wout_s, sems)
            dg1_ref[...] = jnp.zeros_like(dg1_ref)
            db1_ref[...] = jnp.zeros_like(db1_ref)
            for cp in copies:
                cp.wait()

        du2 = du2_ref[...]
        du2b = du2.astype(BF16)
        for k in range(N_DEV):
            cols = slice(FF_SHARD * k, FF_SHARD * (k + 1))
            da = _dot_nt(du2b, wout_s[cols, :])
            dhm_ref[:, cols] = (2.0 * jnp.maximum(hm_ref[:, cols], 0.0) * da).astype(BF16)
        dx1 = ALPHA * du2 + _dot_nt(dhm_ref[...], win_s[...])
        xhat, rstd = _ln_stats(u_ref[...])
        dg1_ref[...] += jnp.sum(dx1 * xhat, axis=0, keepdims=True)
        db1_ref[...] += jnp.sum(dx1, axis=0, keepdims=True)
        du1_ref[...] = _ln_bwd(dx1, xhat, rstd, g1_ref[...])

    rows = pl.BlockSpec((tm, D_MODEL), lambda i: (i, 0))
    vec = pl.BlockSpec((1, D_MODEL), lambda i: (0, 0))
    ff = pl.BlockSpec((tm, D_FF), lambda i: (i, 0))
    return pl.pallas_call(
        body, name="mlp_bwd", grid=(s // tm,),
        in_specs=[rows, ff, rows, HBM_SPEC, HBM_SPEC, vec],
        out_specs=[ff, rows, vec, vec],
        out_shape=[jax.ShapeDtypeStruct((s, D_FF), BF16), jax.ShapeDtypeStruct((s, D_MODEL), F32),
                   jax.ShapeDtypeStruct((1, D_MODEL), F32), jax.ShapeDtypeStruct((1, D_MODEL), F32)],
        scratch_shapes=MLP_WEIGHT_SCRATCH,
        compiler_params=_params(("arbitrary",)),
    )(du2, hm, u1, wg_in, w_out, _row(ln1_g))


def dh_slab(pieces, tm=512):
    s = pieces[0].shape[0]
    widths = [p.shape[1] for p in pieces]
    assert sum(widths) == IN_COLS
    n = len(pieces)

    def body(*refs):
        dh_ref = refs[n]
        off = 0
        for ref, w in zip(refs[:n], widths):
            dh_ref[:, off:off + w] = ref[...].astype(BF16)
            off += w

    rows = lambda w: pl.BlockSpec((tm, w), lambda i: (i, 0))
    return pl.pallas_call(
        body, name="dh_slab", grid=(s // tm,),
        in_specs=[rows(w) for w in widths], out_specs=rows(IN_COLS),
        out_shape=jax.ShapeDtypeStruct((s, IN_COLS), BF16),
        compiler_params=_params(("arbitrary",)),
    )(*pieces)


def in_bwd(du1, dh, wg_in, tm=512, after=()):
    s = du1.shape[0]

    def body(du_ref, dh_ref, w_ref, gx_ref):
        acc = ALPHA * du_ref[...]
        for j in range(N_DEV):
            acc = acc + _dot_nt(dh_ref[:, IN_SHARD * j:IN_SHARD * (j + 1)], w_ref[j])
        gx_ref[...] = acc

    rows = lambda w: pl.BlockSpec((tm, w), lambda i: (i, 0))
    return pl.pallas_call(
        _behind(body, after), name="in_bwd", grid=(s // tm,),
        in_specs=[HBM_SPEC] * len(after) + [rows(D_MODEL), rows(IN_COLS),
                                            pl.BlockSpec((N_DEV, D_MODEL, IN_SHARD), lambda i: (0, 0, 0))],
        out_specs=rows(D_MODEL),
        out_shape=jax.ShapeDtypeStruct((s, D_MODEL), F32),
        compiler_params=_params(("arbitrary",)),
    )(*after, du1, dh, wg_in)


def wgrad(name, a, b, tm, tn, shard_cols=None, tk=2048, after=()):
    kk, m = a.shape
    n = b.shape[1]
    tk = min(tk, kk)
    nk = kk // tk
    w = shard_cols

    def body(a_ref, b_ref, out_ref, acc_s):
        k = pl.program_id(2)

        @pl.when(k == 0)
        def _():
            acc_s[...] = jnp.zeros_like(acc_s)

        acc_s[...] += _dot_tn(a_ref[...].astype(BF16), b_ref[...].astype(BF16))

        @pl.when(k == nk - 1)
        def _():
            if w is None:
                out_ref[...] = acc_s[...]
            else:
                for j in range(tn // w):
                    out_ref[j] = acc_s[:, w * j:w * (j + 1)]

    if w is None:
        out_spec = pl.BlockSpec((tm, tn), lambda i, j, k: (i, j))
        out_shape = jax.ShapeDtypeStruct((m, n), F32)
    else:
        out_spec = pl.BlockSpec((tn // w, tm, w), lambda i, j, k: (j, i, 0))
        out_shape = jax.ShapeDtypeStruct((n // w, m, w), F32)
    return pl.pallas_call(
        _behind(body, after), name=name, grid=(m // tm, n // tn, nk),
        in_specs=[HBM_SPEC] * len(after) + [pl.BlockSpec((tk, tm), lambda i, j, k: (k, i)),
                  pl.BlockSpec((tk, tn), lambda i, j, k: (k, j))],
        out_specs=out_spec, out_shape=out_shape,
        scratch_shapes=[pltpu.VMEM((tm, tn), F32)],
        compiler_params=_params(("arbitrary", "arbitrary", "arbitrary")),
    )(*after, a, b)


def adamw(name, w, m, v, parts, tm=None):
    rows, cols = w.shape
    tm = rows if tm is None else min(tm, rows)
    n = len(parts)
    c1 = 1.0 - ADAM_B1 ** ADAM_STEP
    c2 = 1.0 - ADAM_B2 ** ADAM_STEP

    def body(*refs):
        w_ref, m_ref, v_ref = refs[:3]
        part_refs = refs[3:3 + n]
        g_ref, d_ref, nm_ref, nv_ref = refs[3 + n:]
        g = None
        for ref, (_, index) in zip(part_refs, parts):
            term = (ref[...] if index is None else ref[0]).astype(F32)
            g = term if g is None else g + term
        new_m = ADAM_B1 * m_ref[...] + (1.0 - ADAM_B1) * g
        new_v = ADAM_B2 * v_ref[...] + (1.0 - ADAM_B2) * (g * g)
        m_hat = new_m / c1
        v_hat = new_v / c2
        g_ref[...] = g
        d_ref[...] = -ADAM_LR * (m_hat / (jnp.sqrt(v_hat) + ADAM_EPS) + ADAM_WD * w_ref[...])
        nm_ref[...] = new_m
        nv_ref[...] = new_v

    spec = pl.BlockSpec((tm, cols), lambda i: (i, 0))
    shape = jax.ShapeDtypeStruct((rows, cols), F32)

    def part_spec(index):
        if index is None:
            return spec
        return pl.BlockSpec((1, tm, cols), lambda i: (index, i, 0))

    return pl.pallas_call(
        body, name=name, grid=(rows // tm,),
        in_specs=[spec] * 3 + [part_spec(index) for _, index in parts],
        out_specs=[spec] * 4, out_shape=[shape] * 4,
        compiler_params=_params(("arbitrary",)),
    )(w, m, v, *[array for array, _ in parts])


def _place():
    x, y, c = lax.axis_index("x"), lax.axis_index("y"), lax.axis_index("c")
    other_chips = [(1 - x, y), (x, 1 - y), (1 - x, 1 - y)]
    return x, y, c, other_chips


def all_gather(name, arrays, seeds=()):
    n = len(arrays)
    ns = len(seeds)

    def body(*refs):
        ins, seed_ins = refs[:n], refs[n:n + ns]
        outs, seed_outs = refs[n + ns:2 * n + ns], refs[2 * n + ns:2 * (n + ns)]
        send_sems, recv_sems, local_sems = refs[2 * (n + ns):]
        x, y, c, chips = _place()
        me = 4 * x + 2 * y + c
        sibling = (x, y, 1 - c)

        def copy(a, k, block, to, src=None):
            return pltpu.make_async_remote_copy(
                src_ref=outs[a].at[block] if src is None else src, dst_ref=outs[a].at[block],
                send_sem=send_sems.at[a, k], recv_sem=recv_sems.at[a, k],
                device_id=to, device_id_type=MESH)

        started = []
        local = [pltpu.make_async_copy(ins[a], outs[a].at[me], local_sems.at[a]) for a in range(n)]
        local += [pltpu.make_async_copy(seed_ins[a], seed_outs[a].at[me], local_sems.at[n + a]) for a in range(ns)]
        for cp in local[n:]:
            cp.start()
        for a in range(n):
            local[a].start()
            sends = [copy(a, 0, me, sibling, src=ins[a])]
            sends += [copy(a, 1 + j, me, (cx, cy, c), src=ins[a]) for j, (cx, cy) in enumerate(chips)]
            for cp in sends:
                cp.start()
            started += sends
        for a in range(n):
            for j, (cx, cy) in enumerate(chips):
                block = 4 * cx + 2 * cy + c
                copy(a, 1 + j, block, sibling).wait_recv()
                passed = copy(a, 4 + j, block, sibling)
                passed.start()
                started.append(passed)
        for a in range(n):
            copy(a, 0, 4 * x + 2 * y + (1 - c), sibling).wait_recv()
            for j, (cx, cy) in enumerate(chips):
                copy(a, 4 + j, 4 * cx + 2 * cy + (1 - c), sibling).wait_recv()
        for cp in started:
            cp.wait_send()
        for cp in local:
            cp.wait()

    return pl.pallas_call(
        body, name=name,
        in_specs=[HBM_SPEC] * (n + ns), out_specs=[HBM_SPEC] * (n + ns),
        out_shape=[jax.ShapeDtypeStruct((N_DEV,) + a.shape, a.dtype) for a in list(arrays) + list(seeds)],
        scratch_shapes=[pltpu.SemaphoreType.DMA((n, 7)), pltpu.SemaphoreType.DMA((n, 7)),
                        pltpu.SemaphoreType.DMA((n + ns,))],
    )(*arrays, *seeds)


SEM_SPEC = pl.BlockSpec(memory_space=pltpu.SEMAPHORE)
DATAFLOW = pltpu.SideEffectType.DATAFLOW_SIDE_EFFECTING


def _in_hbm(a):
    return pltpu.with_memory_space_constraint(a, pltpu.HBM)


def exchange_start(name, bufs, plan, n_copies, after=()):
    nb, na = len(bufs), len(after)

    def body(*refs):
        ins = refs[:nb]
        send_sems, recv_sems = refs[nb + na], refs[nb + na + 1]
        token = refs[-1]
        for k, (src, dst, peer, _) in enumerate(plan(ins)):
            pltpu.make_async_remote_copy(src_ref=src, dst_ref=dst, send_sem=send_sems.at[k], recv_sem=recv_sems.at[k],
                                         device_id=peer, device_id_type=MESH).start()
        token[...] = jnp.zeros_like(token)

    out = pl.pallas_call(
        body, name=name,
        out_shape=(pltpu.SemaphoreType.DMA((n_copies,)), pltpu.SemaphoreType.DMA((n_copies,)),
                   *[pltpu.HBM(b.shape, b.dtype) for b in bufs], jax.ShapeDtypeStruct((8, 128), F32)),
        in_specs=[pl.BlockSpec(memory_space=pltpu.HBM)] * nb + [HBM_SPEC] * na,
        out_specs=(SEM_SPEC, SEM_SPEC, *[pl.BlockSpec(memory_space=pltpu.HBM)] * nb,
                   pl.BlockSpec(memory_space=pltpu.VMEM)),
        input_output_aliases={i: 2 + i for i in range(nb)},
        compiler_params=pltpu.CompilerParams(has_side_effects=DATAFLOW),
    )(*[_in_hbm(b) for b in bufs], *after)
    return out[0], out[1], list(out[2:2 + nb]), out[-1]


def exchange_wait(name, send_sems, recv_sems, bufs, plan, after=()):
    nb, na = len(bufs), len(after)

    def body(*refs):
        ins = refs[:nb]
        send_ref, recv_ref = refs[nb], refs[nb + 1]
        for k, (src, dst, peer, landed) in enumerate(plan(ins)):
            pltpu.make_async_remote_copy(src_ref=src, dst_ref=dst, send_sem=send_ref.at[k], recv_sem=recv_ref.at[k],
                                         device_id=peer, device_id_type=MESH).wait_send()
            pltpu.make_async_remote_copy(src_ref=src, dst_ref=landed, send_sem=send_ref.at[k], recv_sem=recv_ref.at[k],
                                         device_id=peer, device_id_type=MESH).wait_recv()

    out = pl.pallas_call(
        body, name=name,
        out_shape=tuple(pltpu.HBM(b.shape, b.dtype) for b in bufs),
        in_specs=[pl.BlockSpec(memory_space=pltpu.HBM)] * nb + [SEM_SPEC, SEM_SPEC] + [HBM_SPEC] * na,
        out_specs=tuple([pl.BlockSpec(memory_space=pltpu.HBM)] * nb),
        input_output_aliases={i: i for i in range(nb)},
        compiler_params=pltpu.CompilerParams(has_side_effects=DATAFLOW),
    )(*bufs, send_sems, recv_sems, *after)
    return list(out)


def _gather_plan(n):
    def plan(bufs):
        x, y, c, _ = _place()
        me = 4 * x + 2 * y + c
        copies = []
        for a in range(n):
            for k in range(1, N_DEV):
                px, py, pc = x ^ (k >> 2), y ^ ((k >> 1) & 1), c ^ (k & 1)
                copies.append((bufs[a].at[me], bufs[a].at[me], (px, py, pc), bufs[a].at[4 * px + 2 * py + pc]))
        return copies
    return plan


def _sibling_plan(n):
    def plan(bufs):
        x, y, c, _ = _place()
        return [(bufs[a].at[:, 1 - c], bufs[n + a], (x, y, 1 - c), bufs[n + a]) for a in range(n)]
    return plan


def _chips_plan(n):
    def plan(bufs):
        _, _, c, chips = _place()
        return [(bufs[a].at[2 * cx + cy], bufs[n + a].at[j], (cx, cy, c), bufs[n + a].at[j])
                for a in range(n) for j, (cx, cy) in enumerate(chips)]
    return plan


def chip_sum(name, grad, landed, place, tr=256):
    _, _, rows, cols = grad.shape
    tr = min(tr, rows)

    def body(place_ref, g_ref, l_ref, own_ref, out_ref):
        total = g_ref[0, 0] + l_ref[0]
        out_ref[0] = total.astype(BF16)

        @pl.when(pl.program_id(1) == place_ref[1])
        def _():
            own_ref[...] = total

    return pl.pallas_call(
        body, name=name,
        grid_spec=pltpu.PrefetchScalarGridSpec(
            num_scalar_prefetch=1, grid=(rows // tr, 4),
            in_specs=[pl.BlockSpec((1, 1, tr, cols), lambda i, ch, pos: (ch, pos[0], i, 0)),
                      pl.BlockSpec((1, tr, cols), lambda i, ch, pos: (ch, i, 0))],
            out_specs=[pl.BlockSpec((tr, cols), lambda i, ch, pos: (i, 0)),
                       pl.BlockSpec((1, tr, cols), lambda i, ch, pos: (ch, i, 0))]),
        out_shape=[jax.ShapeDtypeStruct((rows, cols), F32), jax.ShapeDtypeStruct((4, rows, cols), BF16)],
        compiler_params=_params(("arbitrary", "arbitrary")),
    )(place, grad, landed)


BIG =("w_in", "w_sb_proj", "w_ca_proj", "w_out", "w_mlp_in", "w_mlp_out")
SMALL = ("b_gate", "rel_bias", "ln1_g", "ln1_b", "ln2_g", "ln2_b")
NAMES = ("w_in", "b_gate", "w_sb_proj", "w_ca_proj", "rel_bias", "w_out", "ln1_g", "ln1_b",
         "w_mlp_in", "w_mlp_out", "ln2_g", "ln2_b")
REL_COLS = 2 * REL_CLIP + 1


def _pack_small(t):
    rel = t["rel_bias"]
    if rel.shape[1] != REL_PAD:
        rel = jnp.pad(rel, ((0, 0), (0, REL_PAD - rel.shape[1])))
    flat = [t["b_gate"].reshape(-1), rel.reshape(-1)] + [t[n].reshape(-1) for n in SMALL[2:]]
    return jnp.concatenate(flat).reshape(-1, 128)


def _unpack_small(p):
    flat = p.reshape(-1)
    out, off = {}, 0
    for n, size in (("b_gate", GATE_COLS), ("rel_bias", 8 * REL_PAD)) + tuple((n, D_MODEL) for n in SMALL[2:]):
        out[n] = flat[off:off + size]
        off += size
    out["rel_bias"] = out["rel_bias"].reshape(8, REL_PAD)[:, :REL_COLS]
    return out


def kernel(x, w_in, b_gate, w_sb_proj, w_ca_proj, rel_bias, w_out, ln1_g, ln1_b, w_mlp_in, w_mlp_out, ln2_g, ln2_b, loss_target, m_w_in, m_b_gate, m_w_sb_proj, m_w_ca_proj, m_rel_bias, m_w_out, m_ln1_g, m_ln1_b, m_w_mlp_in, m_w_mlp_out, m_ln2_g, m_ln2_b, v_w_in, v_b_gate, v_w_sb_proj, v_w_ca_proj, v_rel_bias, v_w_out, v_ln1_g, v_ln1_b, v_w_mlp_in, v_w_mlp_out, v_ln2_g, v_ln2_b):
    w = dict(w_in=w_in, b_gate=b_gate, w_sb_proj=w_sb_proj, w_ca_proj=w_ca_proj, rel_bias=rel_bias, w_out=w_out,
             ln1_g=ln1_g, ln1_b=ln1_b, w_mlp_in=w_mlp_in, w_mlp_out=w_mlp_out, ln2_g=ln2_g, ln2_b=ln2_b)
    m = dict(w_in=m_w_in, b_gate=m_b_gate, w_sb_proj=m_w_sb_proj, w_ca_proj=m_w_ca_proj, rel_bias=m_rel_bias,
             w_out=m_w_out, ln1_g=m_ln1_g, ln1_b=m_ln1_b, w_mlp_in=m_w_mlp_in, w_mlp_out=m_w_mlp_out,
             ln2_g=m_ln2_g, ln2_b=m_ln2_b)
    v = dict(w_in=v_w_in, b_gate=v_b_gate, w_sb_proj=v_w_sb_proj, w_ca_proj=v_w_ca_proj, rel_bias=v_rel_bias,
             w_out=v_w_out, ln1_g=v_ln1_g, ln1_b=v_ln1_b, w_mlp_in=v_w_mlp_in, w_mlp_out=v_w_mlp_out,
             ln2_g=v_ln2_g, ln2_b=v_ln2_b)

    x, target = x[0], loss_target[0]
    c = lax.axis_index("c")
    place = jnp.stack([c, 2 * lax.axis_index("x") + lax.axis_index("y")]).astype(jnp.int32)
    grads, deltas, new_m, new_v = {}, {}, {}, {}

    def by_owner(n, g):
        return g.reshape((4, 2) + w[n].shape)

    def finish(n, own, arrived):
        grads[n], deltas[n], new_m[n], new_v[n] = adamw(
            "adamw_" + n, w[n], m[n], v[n], [(own, None), (arrived, 0), (arrived, 1), (arrived, 2)], tm=256)

    def reduce_start(tag, names, g):
        bufs = [by_owner(n, g[n]) for n in names]
        bufs += [_in_hbm(lax.empty((4,) + w[n].shape, F32)) for n in names]
        return exchange_start("reduce_%s_siblings_start" % tag, bufs, _sibling_plan(len(names)), len(names))

    def reduce_middle(tag, names, started, after):
        send, recv, bufs, _ = started
        k = len(names)
        bufs = exchange_wait("reduce_%s_siblings_wait" % tag, send, recv, bufs, _sibling_plan(k), after=after)
        own, sums = zip(*[chip_sum("chip_sum_" + n, bufs[a], bufs[k + a], place) for a, n in enumerate(names)])
        zones = [_in_hbm(lax.empty((3,) + w[n].shape, BF16)) for n in names]
        return own, exchange_start("reduce_%s_chips_start" % tag, list(sums) + zones, _chips_plan(k), 3 * k)

    def reduce_end(tag, names, own, started, after):
        send, recv, bufs, _ = started
        k = len(names)
        bufs = exchange_wait("reduce_%s_chips_wait" % tag, send, recv, bufs, _chips_plan(k), after=after)
        for a, n in enumerate(names):
            finish(n, own[a], bufs[k + a])

    rest = BIG[1:]
    wg_in, *zones = all_gather("gather_w_in", [w_in.astype(BF16)], seeds=[w[n].astype(BF16) for n in rest])
    send, recv, zones, token = exchange_start("gather_rest_start", zones, _gather_plan(len(rest)),
                                              (N_DEV - 1) * len(rest))
    qkv, lg, xb = in_proj(x, wg_in, after=(token,))
    bias_tiles = ca_bias_tiles(rel_bias)
    o_sb = sb_fwd(qkv)
    o_ca = ca_fwd(qkv, bias_tiles)
    zones = exchange_wait("gather_rest_wait", send, recv, zones, _gather_plan(len(rest)), after=(o_sb, o_ca))
    wg = dict(zip(rest, zones))
    wg_sb, wg_ca, wg_mi = wg["w_sb_proj"], wg["w_ca_proj"], wg["w_mlp_in"]
    wg_mo = wg["w_mlp_out"].reshape(D_FF, D_MODEL)
    w_out_b = wg["w_out"].reshape(D_MODEL, D_MODEL)

    u1 = mix_fwd(x, o_sb, o_ca, lg, b_gate, wg_sb, wg_ca, w_out_b)
    hm, a, x1b, du2, loss, dg2, db2 = mlp_fwd(u1, target, wg_mi, wg_mo, ln1_g, ln1_b, ln2_g, ln2_b)
    loss = lax.psum(loss[0, 0], ("x", "y", "c"))
    dhm, du1, dg1, db1 = mlp_bwd(du2, hm, u1, wg_mi, wg_mo, ln1_g)
    g = {}
    g["w_mlp_in"] = wgrad("wgrad_mlp_in", x1b, dhm, tm=D_MODEL, tn=FF_SHARD, shard_cols=FF_SHARD)
    g["w_mlp_out"] = wgrad("wgrad_mlp_out", a, du2, tm=FF_SHARD, tn=D_MODEL)

    mlp = ("w_mlp_in", "w_mlp_out")
    started = reduce_start("mlp", mlp, g)
    dlg, do_sb, do_ca, dy_sb, dy_ca, merged, dbg = mix_bwd(
        du1, o_sb, o_ca, lg, b_gate, wg_sb, wg_ca, w_out_b, after=(started[3],))
    own_mlp, started_mlp = reduce_middle("mlp", mlp, started, after=(dlg,))
    g["w_out"] = wgrad("wgrad_out", merged, du1, tm=D_MODEL, tn=D_MODEL, after=(started_mlp[3],))
    g["w_sb_proj"] = wgrad("wgrad_sb_proj", o_sb, dy_sb, tm=ATT_WIDTH, tn=D_MODEL, shard_cols=PROJ_SHARD)
    g["w_ca_proj"] = wgrad("wgrad_ca_proj", o_ca, dy_ca, tm=ATT_WIDTH, tn=D_MODEL, shard_cols=PROJ_SHARD)

    att = ("w_out", "w_sb_proj", "w_ca_proj")
    started = reduce_start("att", att, g)
    dq_sb, dk_sb, dv_sb = sb_bwd(qkv, o_sb, do_sb, after=(started[3],))
    own_att, started_att = reduce_middle("att", att, started, after=(dq_sb,))
    dq_ca, dk_ca, dv_ca, db_tiles = ca_bwd(qkv, bias_tiles, do_ca, after=(started_att[3],))
    g_rel = rel_bias_grad(db_tiles)

    dh = dh_slab([dq_sb, dk_sb, dv_sb, dq_ca, dk_ca, dv_ca, dlg])
    g["w_in"] = wgrad("wgrad_in", xb, dh, tm=D_MODEL, tn=IN_SHARD, shard_cols=IN_SHARD)
    last = ("w_in",)
    started = reduce_start("in", last, g)
    grad_x = in_bwd(du1, dh, wg_in, after=(started[3],))
    own_in, started_in = reduce_middle("in", last, started, after=(grad_x,))
    reduce_end("mlp", mlp, own_mlp, started_mlp, after=(started_in[3],))
    reduce_end("att", att, own_att, started_att, after=(started_in[3],))

    small = dict(b_gate=dbg, rel_bias=g_rel, ln1_g=dg1, ln1_b=db1, ln2_g=dg2, ln2_b=db2)
    parts = all_gather("gather_small_grads", [_pack_small(small)])[0]
    packed = adamw("adamw_small", _pack_small(w), _pack_small(m), _pack_small(v),
                   [(parts, d) for d in range(N_DEV)])
    for out, p in zip((grads, deltas, new_m, new_v), packed):
        for n, val in _unpack_small(p).items():
            out[n] = val.reshape(w[n].shape)
    reduce_end("in", last, own_in, started_in, after=(packed[0], deltas["w_mlp_in"], deltas["w_out"]))

    return (loss, grad_x[None], *[grads[n] for n in NAMES], *[deltas[n] for n in NAMES],
            *[new_m[n] for n in NAMES], *[new_v[n] for n in NAMES])
```

```python
import functools

import jax
import jax.numpy as jnp
from jax import lax
from jax.experimental import pallas as pl
from jax.experimental.pallas import tpu as pltpu

F32 = jnp.float32
BF16 = jnp.bfloat16
MESH = pl.DeviceIdType.MESH

N_DEV = 8
D_MODEL = 1024
HEAD_DIM = 64
ATT_WIDTH = 512
N_PAIRS = ATT_WIDTH // 128
QKV_COLS = 6 * ATT_WIDTH
GATE_COLS = 2 * D_MODEL
IN_COLS = QKV_COLS + GATE_COLS
IN_SHARD = IN_COLS // N_DEV
D_FF = 4 * D_MODEL
FF_SHARD = D_FF // N_DEV
PROJ_SHARD = D_MODEL // N_DEV
ATT_BLOCK = 128
CA_TILES = 5
CA_ROWS = 512
CHUNK = 64
CA_PREV_CHUNKS = 8
REL_CLIP = 256
REL_PAD = 640
ALPHA = 2.0 ** 0.25
LN_EPS = 1e-5
QK_SCALE = HEAD_DIM ** -0.5
NEG_BIG = -1e30
VMEM_LIMIT = 56 * 1024 * 1024

ADAM_LR = 0.001
ADAM_B1 = 0.9
ADAM_B2 = 0.999
ADAM_EPS = 1e-08
ADAM_WD = 0.01
ADAM_STEP = 10

_NT = (((1,), (1,)), ((), ()))
_TN = (((0,), (0,)), ((), ()))


def _dot(a, b):
    return jnp.dot(a, b, preferred_element_type=F32)


def _dot_nt(a, b):
    return lax.dot_general(a, b, _NT, preferred_element_type=F32)


def _dot_tn(a, b):
    return lax.dot_general(a, b, _TN, preferred_element_type=F32)


def _params(semantics=None):
    return pltpu.CompilerParams(dimension_semantics=semantics, vmem_limit_bytes=VMEM_LIMIT)


def _row(v):
    return v.reshape(1, -1)


HBM_SPEC = pl.BlockSpec(memory_space=pl.ANY)


def _behind(body, after):
    n = len(after)

    def wrapped(*refs):
        body(*refs[n:])

    return wrapped


def in_proj(x, wg_in, tm=512, after=()):
    s = x.shape[0]

    def body(x_ref, w_ref, qkv_ref, lg_ref, xb_ref):
        xb = x_ref[...].astype(BF16)
        xb_ref[...] = xb
        for j in range(N_DEV):
            acc = _dot(xb, w_ref[j])
            lo, hi = IN_SHARD * j, IN_SHARD * (j + 1)
            if hi <= QKV_COLS:
                qkv_ref[:, lo:hi] = acc.astype(BF16)
            elif lo >= QKV_COLS:
                lg_ref[:, lo - QKV_COLS:hi - QKV_COLS] = acc
            else:
                qkv_ref[:, lo:QKV_COLS] = acc[:, :QKV_COLS - lo].astype(BF16)
                lg_ref[:, 0:hi - QKV_COLS] = acc[:, QKV_COLS - lo:]

    return pl.pallas_call(
        _behind(body, after), name="in_proj", grid=(s // tm,),
        in_specs=[HBM_SPEC] * len(after) + [pl.BlockSpec((tm, D_MODEL), lambda i: (i, 0)),
                  pl.BlockSpec((N_DEV, D_MODEL, IN_SHARD), lambda i: (0, 0, 0))],
        out_specs=[pl.BlockSpec((tm, QKV_COLS), lambda i: (i, 0)),
                   pl.BlockSpec((tm, GATE_COLS), lambda i: (i, 0)),
                   pl.BlockSpec((tm, D_MODEL), lambda i: (i, 0))],
        out_shape=[jax.ShapeDtypeStruct((s, QKV_COLS), BF16),
                   jax.ShapeDtypeStruct((s, GATE_COLS), F32),
                   jax.ShapeDtypeStruct((s, D_MODEL), BF16)],
        compiler_params=_params(("arbitrary",)),
    )(*after, x, wg_in)


def _head_masks():
    lane = lax.broadcasted_iota(jnp.int32, (1, 128), 1)
    first = lane < HEAD_DIM
    return first, jnp.logical_not(first)


SB_CHUNK = 512
SB_SUB = SB_CHUNK // ATT_BLOCK


def _tri_pair():
    row = lax.broadcasted_iota(jnp.int32, (4 * ATT_BLOCK, 2 * ATT_BLOCK), 0) & (2 * ATT_BLOCK - 1)
    col = lax.broadcasted_iota(jnp.int32, (4 * ATT_BLOCK, 2 * ATT_BLOCK), 1)
    same = (row >= ATT_BLOCK) == (col >= ATT_BLOCK)
    return jnp.logical_and(same, row >= col).astype(BF16)


def _pair_bf16(pieces):
    return jnp.concatenate(
        [jnp.concatenate([p.astype(BF16) for p in row], axis=1) for row in pieces], axis=0)


def _suffix_sums(pieces, tri_pair):
    hi = [[p.astype(BF16) for p in row] for row in pieces]
    lo = [[p - h.astype(F32) for p, h in zip(row, hrow)] for row, hrow in zip(pieces, hi)]
    return _dot(jnp.concatenate([_pair_bf16(hi), _pair_bf16(lo)], axis=1), tri_pair)


def _first_row_piece(diag, j):
    return j if diag else 0


def _piece(v, r, h):
    return v[ATT_BLOCK * r:ATT_BLOCK * (r + 1), ATT_BLOCK * h:ATT_BLOCK * (h + 1)]


def _all_row_pieces(slabs):
    return [[slab[ATT_BLOCK * r:ATT_BLOCK * (r + 1)] for r in range(SB_SUB)] for slab in slabs]


def _add_rows(total, part, r0):
    if r0 == 0:
        return total + part
    return jnp.concatenate([total[:ATT_BLOCK * r0], total[ATT_BLOCK * r0:] + part], axis=0)


def _row_totals(csum, r0, h):
    return jnp.concatenate(
        [jnp.broadcast_to(_piece(csum, at, h)[:, 0:1], (ATT_BLOCK, ATT_BLOCK)) for at in range(SB_SUB - r0)], axis=0)


SIGN_BIT = 0x80000000


def _drop(z):
    minus_abs = lax.bitcast_convert_type(lax.bitcast_convert_type(z, jnp.uint32) | jnp.uint32(SIGN_BIT), F32)
    return jnp.maximum(z, 0.0) + jnp.log(1.0 + jnp.exp(minus_abs))


def _sb_weights(z, runs, r0, masked, strict, tri_pair):
    drops = []
    for r in range(r0, SB_SUB):
        row = [_drop(_piece(z, r - r0, h)) for h in range(2)]
        drops.append([jnp.where(strict, drop, 0.0) for drop in row] if r == masked else row)
    csum = _suffix_sums(drops, tri_pair)
    weights = []
    for r in range(r0, SB_SUB):
        row = [jnp.exp(_piece(z, r - r0, h) - (_piece(csum, r - r0, h) + runs[h][r])) for h in range(2)]
        weights.append([jnp.where(strict, a, 0.0) for a in row] if r == masked else row)
    return drops, weights, csum


def _by_head(blk, masks):
    zero = jnp.zeros_like(blk)
    return jnp.concatenate([jnp.where(m, blk, zero) for m in masks], axis=0)


def sb_fwd(qkv):
    s = qkv.shape[0]
    nq = s // SB_CHUNK

    def body(q_ref, k_ref, v_ref, o_ref, run_s):
        i = pl.program_id(1)
        masks = _head_masks()
        q = q_ref[...] * QK_SCALE
        row = lax.broadcasted_iota(jnp.int32, (ATT_BLOCK, ATT_BLOCK), 0)
        col = lax.broadcasted_iota(jnp.int32, (ATT_BLOCK, ATT_BLOCK), 1)
        strict = col < row
        tri_pair = _tri_pair()
        run_s[...] = jnp.zeros_like(run_s)
        o_ref[...] = jnp.zeros_like(o_ref)

        def chunk(cb, diag):
            start = cb * SB_CHUNK
            runs = [run_s[0], run_s[1]]
            acc = jnp.zeros((SB_CHUNK, ATT_BLOCK), F32)
            for j in reversed(range(SB_SUB)):
                r0 = _first_row_piece(diag, j)
                keys = pl.ds(pl.multiple_of(start + ATT_BLOCK * j, ATT_BLOCK), ATT_BLOCK)
                z = _dot_nt(q[ATT_BLOCK * r0:], _by_head(k_ref[keys, :], masks))
                _, weights, csum = _sb_weights(z, _all_row_pieces(runs), r0, j if diag else None, strict, tri_pair)
                part = _dot(_pair_bf16(weights), _by_head(v_ref[keys, :], masks))
                acc = _add_rows(acc, part, r0)
                runs = [_add_rows(runs[h], _row_totals(csum, r0, h), r0) for h in range(2)]
            run_s[0], run_s[1] = runs
            o_ref[...] += acc

        chunk(i, True)

        def step(it, carry):
            chunk(i - 1 - it, False)
            return carry

        lax.fori_loop(0, i, step, 0)

    return pl.pallas_call(
        body, name="sb_fwd", grid=(N_PAIRS, nq),
        in_specs=[pl.BlockSpec((SB_CHUNK, 128), lambda p, i: (i, p)),
                  pl.BlockSpec((s, 128), lambda p, i: (0, N_PAIRS + p)),
                  pl.BlockSpec((s, 128), lambda p, i: (0, 2 * N_PAIRS + p))],
        out_specs=pl.BlockSpec((SB_CHUNK, 128), lambda p, i: (i, p)),
        out_shape=jax.ShapeDtypeStruct((s, ATT_WIDTH), F32),
        scratch_shapes=[pltpu.VMEM((2, SB_CHUNK, 128), F32)],
        compiler_params=_params(("arbitrary", "arbitrary")),
    )(qkv, qkv, qkv)


def sb_bwd(qkv, o_sb, do_sb, after=()):
    s = qkv.shape[0]
    nq = s // SB_CHUNK

    def body(q_ref, k_ref, v_ref, o_ref, do_ref, dq_ref, dk_ref, dv_ref, run_l_s, run_e_s, etot_s, dq_s):
        i = pl.program_id(1)

        @pl.when(i == 0)
        def _():
            dk_ref[...] = jnp.zeros_like(dk_ref)
            dv_ref[...] = jnp.zeros_like(dv_ref)

        masks = _head_masks()
        q = q_ref[...] * QK_SCALE
        do = do_ref[...]
        prod = do.astype(F32) * o_ref[...]
        for h in range(2):
            total = jnp.sum(jnp.where(masks[h], prod, 0.0), axis=1, keepdims=True)
            etot_s[h] = jnp.broadcast_to(total, (SB_CHUNK, ATT_BLOCK))
        row = lax.broadcasted_iota(jnp.int32, (ATT_BLOCK, ATT_BLOCK), 0)
        col = lax.broadcasted_iota(jnp.int32, (ATT_BLOCK, ATT_BLOCK), 1)
        strict = col < row
        tri_pair = _tri_pair()
        run_l_s[...] = jnp.zeros_like(run_l_s)
        run_e_s[...] = jnp.zeros_like(run_e_s)
        dq_s[...] = jnp.zeros_like(dq_s)

        def chunk(cb, diag):
            start = cb * SB_CHUNK
            slabs_l = [run_l_s[0], run_l_s[1]]
            slabs_e = [run_e_s[0], run_e_s[1]]
            e_tot = _all_row_pieces([etot_s[0], etot_s[1]])
            dq = jnp.zeros((SB_CHUNK, ATT_BLOCK), F32)
            for j in reversed(range(SB_SUB)):
                runs_l, runs_e = _all_row_pieces(slabs_l), _all_row_pieces(slabs_e)
                r0 = _first_row_piece(diag, j)
                masked = j if diag else None
                keys = pl.ds(pl.multiple_of(start + ATT_BLOCK * j, ATT_BLOCK), ATT_BLOCK)
                kk = _by_head(k_ref[keys, :], masks)
                vv = _by_head(v_ref[keys, :], masks)
                q_rows, do_rows = q[ATT_BLOCK * r0:], do[ATT_BLOCK * r0:]
                z = _dot_nt(q_rows, kk)
                da = _dot_nt(do_rows, vv)
                drops, weights, csum = _sb_weights(z, runs_l, r0, masked, strict, tri_pair)
                ab = [[a.astype(BF16) for a in wrow] for wrow in weights]
                es = [[a.astype(F32) * _piece(da, at, h) for h, a in enumerate(arow)] for at, arow in enumerate(ab)]
                esum = _suffix_sums(es, tri_pair)
                dzs = []
                for at, r in enumerate(range(r0, SB_SUB)):
                    dz_row = []
                    for h in range(2):
                        sig = jnp.exp(_piece(z, at, h) - drops[at][h])
                        after = _piece(esum, at, h) + runs_e[h][r] - es[at][h]
                        dz = es[at][h] - sig * (e_tot[h][r] - after)
                        dz_row.append(jnp.where(strict, dz, 0.0) if r == masked else dz)
                    dzs.append(dz_row)
                dz = _pair_bf16(dzs)
                dq = _add_rows(dq, _dot(dz, kk), r0)
                dk = _dot_tn(dz, q_rows)
                dv = _dot_tn(_pair_bf16(ab), do_rows)
                dk_ref[keys, :] += jnp.where(masks[0], dk[:ATT_BLOCK], dk[ATT_BLOCK:])
                dv_ref[keys, :] += jnp.where(masks[0], dv[:ATT_BLOCK], dv[ATT_BLOCK:])
                slabs_l = [_add_rows(slabs_l[h], _row_totals(csum, r0, h), r0) for h in range(2)]
                slabs_e = [_add_rows(slabs_e[h], _row_totals(esum, r0, h), r0) for h in range(2)]
            run_l_s[0], run_l_s[1] = slabs_l
            run_e_s[0], run_e_s[1] = slabs_e
            dq_s[...] += dq

        chunk(i, True)

        def step(it, carry):
            chunk(i - 1 - it, False)
            return carry

        lax.fori_loop(0, i, step, 0)
        dq_ref[...] = (dq_s[...] * QK_SCALE).astype(BF16)

    tile = lambda p, i: (i, p)
    state = pltpu.VMEM((2, SB_CHUNK, 128), F32)
    return pl.pallas_call(
        _behind(body, after), name="sb_bwd", grid=(N_PAIRS, nq),
        in_specs=[HBM_SPEC] * len(after) + [pl.BlockSpec((SB_CHUNK, 128), tile),
                  pl.BlockSpec((s, 128), lambda p, i: (0, N_PAIRS + p)),
                  pl.BlockSpec((s, 128), lambda p, i: (0, 2 * N_PAIRS + p)),
                  pl.BlockSpec((SB_CHUNK, 128), tile),
                  pl.BlockSpec((SB_CHUNK, 128), tile)],
        out_specs=[pl.BlockSpec((SB_CHUNK, 128), tile),
                   pl.BlockSpec((s, 128), lambda p, i: (0, p)),
                   pl.BlockSpec((s, 128), lambda p, i: (0, p))],
        out_shape=[jax.ShapeDtypeStruct((s, ATT_WIDTH), BF16),
                   jax.ShapeDtypeStruct((s, ATT_WIDTH), F32),
                   jax.ShapeDtypeStruct((s, ATT_WIDTH), F32)],
        scratch_shapes=[state, state, state, pltpu.VMEM((SB_CHUNK, 128), F32)],
        compiler_params=_params(("arbitrary", "arbitrary")),
    )(*after, qkv, qkv, qkv, o_sb, do_sb)


def ca_bias_tiles(rel_bias):
    heads = rel_bias.shape[0]
    wide = 2 * ATT_BLOCK

    def body(rel_ref, out_ref):
        bins = lax.broadcasted_iota(jnp.int32, (REL_PAD, wide), 0)
        pos = lax.broadcasted_iota(jnp.int32, (REL_PAD, wide), 1)
        r = lax.broadcasted_iota(jnp.int32, (ATT_BLOCK, ATT_BLOCK), 0)
        c = lax.broadcasted_iota(jnp.int32, (ATT_BLOCK, ATT_BLOCK), 1)
        for j in range(CA_TILES):
            hot = (jnp.clip(ATT_BLOCK * (j + 1) - 1 - pos, -REL_CLIP, REL_CLIP) + REL_CLIP == bins).astype(F32)
            lines = jnp.dot(rel_ref[...], hot, preferred_element_type=F32, precision=lax.Precision.HIGHEST)
            back = 2 * j + (r >> 6) - (c >> 6)
            ok = jnp.logical_and(back >= 0, back <= CA_PREV_CHUNKS)
            for h in range(heads):
                tile = jnp.broadcast_to(lines[h:h + 1, :], (ATT_BLOCK, wide))
                tile = pltpu.roll(tile, wide - (ATT_BLOCK - 1), 1, stride=1, stride_axis=0)
                out_ref[h, j] = jnp.where(ok, tile[:, :ATT_BLOCK], NEG_BIG)

    padded = jnp.pad(rel_bias, ((0, 0), (0, REL_PAD - rel_bias.shape[1])))
    return pl.pallas_call(
        body, name="ca_bias_tiles",
        out_shape=jax.ShapeDtypeStruct((heads, CA_TILES, ATT_BLOCK, ATT_BLOCK), F32),
        compiler_params=_params(),
    )(padded)


CA_BLOCKS = CA_ROWS // ATT_BLOCK
CA_KEY_BLOCKS = CA_BLOCKS + CA_TILES - 1


def _ca_users(t):
    return [r for r in range(CA_BLOCKS) if 0 <= r + CA_TILES - 1 - t < CA_TILES]


def _ca_rows(v, t, r):
    at = _ca_users(t).index(r)
    return v[ATT_BLOCK * at:ATT_BLOCK * (at + 1)]


def _ca_window(step, k_ref, v_ref, masks):
    kks, vvs, inside, rows = [], [], [], []
    for t in range(CA_KEY_BLOCKS):
        block = step * CA_BLOCKS - (CA_TILES - 1) + t
        keys = pl.ds(pl.multiple_of(jnp.maximum(block, 0) * ATT_BLOCK, ATT_BLOCK), ATT_BLOCK)
        kks.append(_by_head(k_ref[keys, :], masks))
        vvs.append(_by_head(v_ref[keys, :], masks))
        inside.append(block >= 0)
        rows.append(keys)
    return kks, vvs, inside, rows


def _ca_probs(q, kks, inside, bias_ref):
    scores = [[[None] * CA_TILES for _ in range(2)] for _ in range(CA_BLOCKS)]
    for t in range(CA_KEY_BLOCKS):
        users = _ca_users(t)
        z = _dot_nt(q[ATT_BLOCK * users[0]:ATT_BLOCK * (users[-1] + 1)], kks[t])
        for r in users:
            j = r + CA_TILES - 1 - t
            for h in range(2):
                zz = _ca_rows(z, t, r)[:, ATT_BLOCK * h:ATT_BLOCK * (h + 1)] + bias_ref[h, j]
                scores[r][h][j] = jnp.where(inside[t], zz, NEG_BIG)
    probs = [[None, None] for _ in range(CA_BLOCKS)]
    for r in range(CA_BLOCKS):
        for h in range(2):
            m = jnp.max(functools.reduce(jnp.maximum, scores[r][h]), axis=1, keepdims=True)
            ex = [jnp.exp(z - m) for z in scores[r][h]]
            inv = 1.0 / jnp.sum(functools.reduce(jnp.add, ex), axis=1, keepdims=True)
            probs[r][h] = [e * inv for e in ex]
    return probs


def _ca_stack(tiles, t):
    return jnp.concatenate(
        [jnp.concatenate([tiles[r][h][r + CA_TILES - 1 - t].astype(BF16) for h in range(2)], axis=1)
         for r in _ca_users(t)], axis=0)


def ca_fwd(qkv, bias_tiles):
    s = qkv.shape[0]
    nq = s // CA_ROWS

    def body(q_ref, k_ref, v_ref, bias_ref, o_ref):
        masks = _head_masks()
        kks, vvs, inside, _ = _ca_window(pl.program_id(1), k_ref, v_ref, masks)
        probs = _ca_probs(q_ref[...] * QK_SCALE, kks, inside, bias_ref)
        out = [None] * CA_BLOCKS
        for t in range(CA_KEY_BLOCKS):
            part = _dot(_ca_stack(probs, t), vvs[t])
            for r in _ca_users(t):
                piece = _ca_rows(part, t, r)
                out[r] = piece if out[r] is None else out[r] + piece
        o_ref[...] = jnp.concatenate(out, axis=0)

    return pl.pallas_call(
        body, name="ca_fwd", grid=(N_PAIRS, nq),
        in_specs=[pl.BlockSpec((CA_ROWS, 128), lambda p, i: (i, 3 * N_PAIRS + p)),
                  pl.BlockSpec((s, 128), lambda p, i: (0, 4 * N_PAIRS + p)),
                  pl.BlockSpec((s, 128), lambda p, i: (0, 5 * N_PAIRS + p)),
                  pl.BlockSpec((2, CA_TILES, ATT_BLOCK, ATT_BLOCK), lambda p, i: (p, 0, 0, 0))],
        out_specs=pl.BlockSpec((CA_ROWS, 128), lambda p, i: (i, p)),
        out_shape=jax.ShapeDtypeStruct((s, ATT_WIDTH), F32),
        compiler_params=_params(("arbitrary", "arbitrary")),
    )(qkv, qkv, qkv, bias_tiles)


def ca_bwd(qkv, bias_tiles, do_ca, after=()):
    s = qkv.shape[0]
    nq = s // CA_ROWS

    def body(q_ref, k_ref, v_ref, bias_ref, do_ref, dq_ref, dk_ref, dv_ref, db_ref):
        big = pl.program_id(1)

        @pl.when(big == 0)
        def _():
            dk_ref[...] = jnp.zeros_like(dk_ref)
            dv_ref[...] = jnp.zeros_like(dv_ref)
            db_ref[...] = jnp.zeros_like(db_ref)

        masks = _head_masks()
        kks, vvs, inside, key_rows = _ca_window(big, k_ref, v_ref, masks)
        q = q_ref[...] * QK_SCALE
        do = do_ref[...]
        probs = _ca_probs(q, kks, inside, bias_ref)
        dps = [[[None] * CA_TILES for _ in range(2)] for _ in range(CA_BLOCKS)]
        for t in range(CA_KEY_BLOCKS):
            users = _ca_users(t)
            dp = _dot_nt(do[ATT_BLOCK * users[0]:ATT_BLOCK * (users[-1] + 1)], vvs[t])
            for r in users:
                for h in range(2):
                    dps[r][h][r + CA_TILES - 1 - t] = _ca_rows(dp, t, r)[:, ATT_BLOCK * h:ATT_BLOCK * (h + 1)]
        dss = [[None, None] for _ in range(CA_BLOCKS)]
        for r in range(CA_BLOCKS):
            for h in range(2):
                delta = jnp.sum(functools.reduce(jnp.add, [p * dp for p, dp in zip(probs[r][h], dps[r][h])]),
                                axis=1, keepdims=True)
                dss[r][h] = [p * (dp - delta) for p, dp in zip(probs[r][h], dps[r][h])]
        for h in range(2):
            for j in range(CA_TILES):
                db_ref[h, j] += functools.reduce(jnp.add, [dss[r][h][j] for r in range(CA_BLOCKS)])
        dq = [None] * CA_BLOCKS
        for t in range(CA_KEY_BLOCKS):
            users = _ca_users(t)
            rows = slice(ATT_BLOCK * users[0], ATT_BLOCK * (users[-1] + 1))
            ds = _ca_stack(dss, t)
            part = _dot(ds, kks[t])
            for r in users:
                piece = _ca_rows(part, t, r)
                dq[r] = piece if dq[r] is None else dq[r] + piece
            dk = _dot_tn(ds, q[rows])
            dv = _dot_tn(_ca_stack(probs, t), do[rows])
            dk_ref[key_rows[t], :] += jnp.where(masks[0], dk[:ATT_BLOCK], dk[ATT_BLOCK:])
            dv_ref[key_rows[t], :] += jnp.where(masks[0], dv[:ATT_BLOCK], dv[ATT_BLOCK:])
        dq_ref[...] = (jnp.concatenate(dq, axis=0) * QK_SCALE).astype(BF16)

    return pl.pallas_call(
        _behind(body, after), name="ca_bwd", grid=(N_PAIRS, nq),
        in_specs=[HBM_SPEC] * len(after) + [pl.BlockSpec((CA_ROWS, 128), lambda p, i: (i, 3 * N_PAIRS + p)),
                  pl.BlockSpec((s, 128), lambda p, i: (0, 4 * N_PAIRS + p)),
                  pl.BlockSpec((s, 128), lambda p, i: (0, 5 * N_PAIRS + p)),
                  pl.BlockSpec((2, CA_TILES, ATT_BLOCK, ATT_BLOCK), lambda p, i: (p, 0, 0, 0)),
                  pl.BlockSpec((CA_ROWS, 128), lambda p, i: (i, p))],
        out_specs=[pl.BlockSpec((CA_ROWS, 128), lambda p, i: (i, p)),
                   pl.BlockSpec((s, 128), lambda p, i: (0, p)),
                   pl.BlockSpec((s, 128), lambda p, i: (0, p)),
                   pl.BlockSpec((2, CA_TILES, ATT_BLOCK, ATT_BLOCK), lambda p, i: (p, 0, 0, 0))],
        out_shape=[jax.ShapeDtypeStruct((s, ATT_WIDTH), BF16),
                   jax.ShapeDtypeStruct((s, ATT_WIDTH), F32),
                   jax.ShapeDtypeStruct((s, ATT_WIDTH), F32),
                   jax.ShapeDtypeStruct((2 * N_PAIRS, CA_TILES, ATT_BLOCK, ATT_BLOCK), F32)],
        compiler_params=_params(("arbitrary", "arbitrary")),
    )(*after, qkv, qkv, qkv, bias_tiles, do_ca)


def rel_bias_grad(db_tiles):
    heads = db_tiles.shape[0]

    def body(db_ref, out_ref):
        row = lax.broadcasted_iota(jnp.int32, (ATT_BLOCK, ATT_BLOCK), 0)
        col = lax.broadcasted_iota(jnp.int32, (ATT_BLOCK, ATT_BLOCK), 1)
        flip = (row + col == ATT_BLOCK - 1).astype(F32)
        wrapped = col < row
        cp = lax.broadcasted_iota(jnp.int32, (ATT_BLOCK, REL_PAD), 0)
        bins = lax.broadcasted_iota(jnp.int32, (ATT_BLOCK, REL_PAD), 1)
        total = jnp.zeros((heads, REL_PAD), F32)
        for j in range(CA_TILES):
            sums_neg, sums_pos = [], []
            for h in range(heads):
                tile = jnp.dot(db_ref[h, j], flip, preferred_element_type=F32, precision=lax.Precision.HIGHEST)
                tile = pltpu.roll(tile, 0, 1, stride=1, stride_axis=0)
                sums_neg.append(jnp.sum(jnp.where(wrapped, 0.0, tile), axis=0, keepdims=True))
                sums_pos.append(jnp.sum(jnp.where(wrapped, tile, 0.0), axis=0, keepdims=True))
            neg = jnp.concatenate(sums_neg, axis=0)
            pos = jnp.concatenate(sums_pos, axis=0)
            hot_neg = (jnp.clip(ATT_BLOCK * j + cp - (ATT_BLOCK - 1), -REL_CLIP, REL_CLIP) + REL_CLIP == bins)
            hot_pos = (jnp.clip(ATT_BLOCK * j + cp + 1, -REL_CLIP, REL_CLIP) + REL_CLIP == bins)
            total = total + jnp.dot(neg, hot_neg.astype(F32), preferred_element_type=F32,
                                    precision=lax.Precision.HIGHEST)
            total = total + jnp.dot(pos, hot_pos.astype(F32), preferred_element_type=F32,
                                    precision=lax.Precision.HIGHEST)
        out_ref[...] = total

    return pl.pallas_call(
        body, name="rel_bias_grad",
        out_shape=jax.ShapeDtypeStruct((heads, REL_PAD), F32),
        compiler_params=_params(),
    )(db_tiles)


def _assemble_cols(dst_ref, src_ref):
    w = src_ref.shape[2]
    for j in range(N_DEV):
        dst_ref[:, w * j:w * (j + 1)] = src_ref[j]


def _merge(o_sb, o_ca, lg, bg, wsb, wca):
    y_sb = _dot(o_sb.astype(BF16), wsb)
    y_ca = _dot(o_ca.astype(BF16), wca)
    gates = jax.nn.sigmoid(lg + bg)
    g_sb, g_ca = gates[:, :D_MODEL], gates[:, D_MODEL:]
    return y_sb, y_ca, g_sb, g_ca, g_sb * y_sb + g_ca * y_ca


def mix_fwd(x, o_sb, o_ca, lg, b_gate, wg_sb, wg_ca, w_out, tm=256):
    s = x.shape[0]

    def body(x_ref, osb_ref, oca_ref, lg_ref, bg_ref, wsb_ref, wca_ref, wout_ref, u_ref, wsb_s, wca_s):
        @pl.when(pl.program_id(0) == 0)
        def _():
            _assemble_cols(wsb_s, wsb_ref)
            _assemble_cols(wca_s, wca_ref)

        merged = _merge(osb_ref[...], oca_ref[...], lg_ref[...], bg_ref[...], wsb_s[...], wca_s[...])[4]
        u_ref[...] = ALPHA * x_ref[...] + _dot(merged.astype(BF16), wout_ref[...])

    rows = lambda w: pl.BlockSpec((tm, w), lambda i: (i, 0))
    whole = lambda shape: pl.BlockSpec(shape, lambda i: (0,) * len(shape))
    return pl.pallas_call(
        body, name="mix_fwd", grid=(s // tm,),
        in_specs=[rows(D_MODEL), rows(ATT_WIDTH), rows(ATT_WIDTH), rows(GATE_COLS), whole((1, GATE_COLS)),
                  whole((N_DEV, ATT_WIDTH, PROJ_SHARD)), whole((N_DEV, ATT_WIDTH, PROJ_SHARD)),
                  whole((D_MODEL, D_MODEL))],
        out_specs=rows(D_MODEL),
        out_shape=jax.ShapeDtypeStruct((s, D_MODEL), F32),
        scratch_shapes=[pltpu.VMEM((ATT_WIDTH, D_MODEL), BF16), pltpu.VMEM((ATT_WIDTH, D_MODEL), BF16)],
        compiler_params=_params(("arbitrary",)),
    )(x, o_sb, o_ca, lg, _row(b_gate), wg_sb, wg_ca, w_out)


def mix_bwd(du1, o_sb, o_ca, lg, b_gate, wg_sb, wg_ca, w_out, tm=256, after=()):
    s = du1.shape[0]

    def body(du_ref, osb_ref, oca_ref, lg_ref, bg_ref, wsb_ref, wca_ref, wout_ref,
             dlg_ref, dosb_ref, doca_ref, dysb_ref, dyca_ref, mg_ref, dbg_ref, wsb_s, wca_s):
        @pl.when(pl.program_id(0) == 0)
        def _():
            _assemble_cols(wsb_s, wsb_ref)
            _assemble_cols(wca_s, wca_ref)
            dbg_ref[...] = jnp.zeros_like(dbg_ref)

        y_sb, y_ca, g_sb, g_ca, merged = _merge(
            osb_ref[...], oca_ref[...], lg_ref[...], bg_ref[...], wsb_s[...], wca_s[...])
        mg_ref[...] = merged.astype(BF16)
        dm = _dot_nt(du_ref[...].astype(BF16), wout_ref[...])
        dl_sb = dm * y_sb * g_sb * (1.0 - g_sb)
        dl_ca = dm * y_ca * g_ca * (1.0 - g_ca)
        dlg_ref[:, :D_MODEL] = dl_sb.astype(BF16)
        dlg_ref[:, D_MODEL:] = dl_ca.astype(BF16)
        dbg_ref[:, :D_MODEL] += jnp.sum(dl_sb, axis=0, keepdims=True)
        dbg_ref[:, D_MODEL:] += jnp.sum(dl_ca, axis=0, keepdims=True)
        dy_sb = (dm * g_sb).astype(BF16)
        dy_ca = (dm * g_ca).astype(BF16)
        dysb_ref[...] = dy_sb
        dyca_ref[...] = dy_ca
        dosb_ref[...] = _dot_nt(dy_sb, wsb_s[...]).astype(BF16)
        doca_ref[...] = _dot_nt(dy_ca, wca_s[...]).astype(BF16)

    rows = lambda w: pl.BlockSpec((tm, w), lambda i: (i, 0))
    whole = lambda shape: pl.BlockSpec(shape, lambda i: (0,) * len(shape))
    return pl.pallas_call(
        _behind(body, after), name="mix_bwd", grid=(s // tm,),
        in_specs=[HBM_SPEC] * len(after) + [rows(D_MODEL), rows(ATT_WIDTH), rows(ATT_WIDTH), rows(GATE_COLS), whole((1, GATE_COLS)),
                  whole((N_DEV, ATT_WIDTH, PROJ_SHARD)), whole((N_DEV, ATT_WIDTH, PROJ_SHARD)),
                  whole((D_MODEL, D_MODEL))],
        out_specs=[rows(GATE_COLS), rows(ATT_WIDTH), rows(ATT_WIDTH), rows(D_MODEL), rows(D_MODEL),
                   rows(D_MODEL), whole((1, GATE_COLS))],
        out_shape=[jax.ShapeDtypeStruct((s, GATE_COLS), BF16),
                   jax.ShapeDtypeStruct((s, ATT_WIDTH), BF16), jax.ShapeDtypeStruct((s, ATT_WIDTH), BF16),
                   jax.ShapeDtypeStruct((s, D_MODEL), BF16), jax.ShapeDtypeStruct((s, D_MODEL), BF16),
                   jax.ShapeDtypeStruct((s, D_MODEL), BF16), jax.ShapeDtypeStruct((1, GATE_COLS), F32)],
        scratch_shapes=[pltpu.VMEM((ATT_WIDTH, D_MODEL), BF16), pltpu.VMEM((ATT_WIDTH, D_MODEL), BF16)],
        compiler_params=_params(("arbitrary",)),
    )(*after, du1, o_sb, o_ca, lg, _row(b_gate), wg_sb, wg_ca, w_out)


def _ln_stats(u):
    mu = jnp.mean(u, axis=1, keepdims=True)
    cen = u - mu
    var = jnp.mean(cen * cen, axis=1, keepdims=True)
    rstd = lax.rsqrt(var + LN_EPS)
    return cen * rstd, rstd


def _ln_bwd(dy, xhat, rstd, gain):
    dyg = dy * gain
    m1 = jnp.mean(dyg, axis=1, keepdims=True)
    m2 = jnp.mean(dyg * xhat, axis=1, keepdims=True)
    return rstd * (dyg - m1 - xhat * m2)


def _load_mlp_weights(win_hbm, wout_hbm, win_s, wout_s, sems):
    copies = [pltpu.make_async_copy(win_hbm.at[k], win_s.at[:, pl.ds(FF_SHARD * k, FF_SHARD)], sems.at[k])
              for k in range(N_DEV)]
    copies.append(pltpu.make_async_copy(wout_hbm, wout_s, sems.at[N_DEV]))
    for cp in copies:
        cp.start()
    return copies


MLP_WEIGHT_SCRATCH = [pltpu.VMEM((D_MODEL, D_FF), BF16), pltpu.VMEM((D_FF, D_MODEL), BF16),
                      pltpu.SemaphoreType.DMA((N_DEV + 1,))]


def mlp_fwd(u1, target, wg_in, w_out, ln1_g, ln1_b, ln2_g, ln2_b, tm=256):
    s = u1.shape[0]

    def body(u_ref, t_ref, win_hbm, wout_hbm, g1_ref, b1_ref, g2_ref, b2_ref,
             hm_ref, a_ref, x1b_ref, du2_ref, loss_ref, dg2_ref, db2_ref, win_s, wout_s, sems):
        first = pl.program_id(0) == 0

        @pl.when(first)
        def _():
            copies = _load_mlp_weights(win_hbm, wout_hbm, win_s, wout_s, sems)
            loss_ref[...] = jnp.zeros_like(loss_ref)
            dg2_ref[...] = jnp.zeros_like(dg2_ref)
            db2_ref[...] = jnp.zeros_like(db2_ref)
            for cp in copies:
                cp.wait()

        xhat, _ = _ln_stats(u_ref[...])
        x1 = xhat * g1_ref[...] + b1_ref[...]
        x1b = x1.astype(BF16)
        x1b_ref[...] = x1b
        for k in range(N_DEV):
            cols = slice(FF_SHARD * k, FF_SHARD * (k + 1))
            hm = _dot(x1b, win_s[:, cols])
            hm_ref[:, cols] = hm
            r = jnp.maximum(hm, 0.0)
            a_ref[:, cols] = (r * r).astype(BF16)
        xhat, rstd = _ln_stats(ALPHA * x1 + _dot(a_ref[...], wout_s[...]))
        diff = xhat * g2_ref[...] + b2_ref[...] - t_ref[...]
        per_token = jnp.mean(diff * diff, axis=1, keepdims=True)
        loss_ref[...] += 0.5 * jnp.sum(per_token, axis=0, keepdims=True)
        dy = diff * (1.0 / D_MODEL)
        dg2_ref[...] += jnp.sum(dy * xhat, axis=0, keepdims=True)
        db2_ref[...] += jnp.sum(dy, axis=0, keepdims=True)
        du2_ref[...] = _ln_bwd(dy, xhat, rstd, g2_ref[...])

    rows = pl.BlockSpec((tm, D_MODEL), lambda i: (i, 0))
    vec = pl.BlockSpec((1, D_MODEL), lambda i: (0, 0))
    ff = pl.BlockSpec((tm, D_FF), lambda i: (i, 0))
    return pl.pallas_call(
        body, name="mlp_fwd", grid=(s // tm,),
        in_specs=[rows, rows, HBM_SPEC, HBM_SPEC, vec, vec, vec, vec],
        out_specs=[ff, ff, rows, rows, pl.BlockSpec((1, 1), lambda i: (0, 0)), vec, vec],
        out_shape=[jax.ShapeDtypeStruct((s, D_FF), F32), jax.ShapeDtypeStruct((s, D_FF), BF16),
                   jax.ShapeDtypeStruct((s, D_MODEL), BF16), jax.ShapeDtypeStruct((s, D_MODEL), F32),
                   jax.ShapeDtypeStruct((1, 1), F32),
                   jax.ShapeDtypeStruct((1, D_MODEL), F32), jax.ShapeDtypeStruct((1, D_MODEL), F32)],
        scratch_shapes=MLP_WEIGHT_SCRATCH,
        compiler_params=_params(("arbitrary",)),
    )(u1, target, wg_in, w_out, _row(ln1_g), _row(ln1_b), _row(ln2_g), _row(ln2_b))


def mlp_bwd(du2, hm, u1, wg_in, w_out, ln1_g, tm=256):
    s = du2.shape[0]

    def body(du2_ref, hm_ref, u_ref, win_hbm, wout_hbm, g1_ref,
             dhm_ref, du1_ref, dg1_ref, db1_ref, win_s, wout_s, sems):
        first = pl.program_id(0) == 0

        @pl.when(first)
        def _():
            copies = _load_mlp_weights(win_hbm, wout_hbm, win_s, wout_s, sems)
            dg1_ref[...] = jnp.zeros_like(dg1_ref)
            db1_ref[...] = jnp.zeros_like(db1_ref)
            for cp in copies:
                cp.wait()

        du2 = du2_ref[...]
        du2b = du2.astype(BF16)
        for k in range(N_DEV):
            cols = slice(FF_SHARD * k, FF_SHARD * (k + 1))
            da = _dot_nt(du2b, wout_s[cols, :])
            dhm_ref[:, cols] = (2.0 * jnp.maximum(hm_ref[:, cols], 0.0) * da).astype(BF16)
        dx1 = ALPHA * du2 + _dot_nt(dhm_ref[...], win_s[...])
        xhat, rstd = _ln_stats(u_ref[...])
        dg1_ref[...] += jnp.sum(dx1 * xhat, axis=0, keepdims=True)
        db1_ref[...] += jnp.sum(dx1, axis=0, keepdims=True)
        du1_ref[...] = _ln_bwd(dx1, xhat, rstd, g1_ref[...])

    rows = pl.BlockSpec((tm, D_MODEL), lambda i: (i, 0))
    vec = pl.BlockSpec((1, D_MODEL), lambda i: (0, 0))
    ff = pl.BlockSpec((tm, D_FF), lambda i: (i, 0))
    return pl.pallas_call(
        body, name="mlp_bwd", grid=(s // tm,),
        in_specs=[rows, ff, rows, HBM_SPEC, HBM_SPEC, vec],
        out_specs=[ff, rows, vec, vec],
        out_shape=[jax.ShapeDtypeStruct((s, D_FF), BF16), jax.ShapeDtypeStruct((s, D_MODEL), F32),
                   jax.ShapeDtypeStruct((1, D_MODEL), F32), jax.ShapeDtypeStruct((1, D_MODEL), F32)],
        scratch_shapes=MLP_WEIGHT_SCRATCH,
        compiler_params=_params(("arbitrary",)),
    )(du2, hm, u1, wg_in, w_out, _row(ln1_g))


def dh_slab(pieces, tm=512):
    s = pieces[0].shape[0]
    widths = [p.shape[1] for p in pieces]
    assert sum(widths) == IN_COLS
    n = len(pieces)

    def body(*refs):
        dh_ref = refs[n]
        off = 0
        for ref, w in zip(refs[:n], widths):
            dh_ref[:, off:off + w] = ref[...].astype(BF16)
            off += w

    rows = lambda w: pl.BlockSpec((tm, w), lambda i: (i, 0))
    return pl.pallas_call(
        body, name="dh_slab", grid=(s // tm,),
        in_specs=[rows(w) for w in widths], out_specs=rows(IN_COLS),
        out_shape=jax.ShapeDtypeStruct((s, IN_COLS), BF16),
        compiler_params=_params(("arbitrary",)),
    )(*pieces)


def in_bwd(du1, dh, wg_in, tm=512, after=()):
    s = du1.shape[0]

    def body(du_ref, dh_ref, w_ref, gx_ref):
        acc = ALPHA * du_ref[...]
        for j in range(N_DEV):
            acc = acc + _dot_nt(dh_ref[:, IN_SHARD * j:IN_SHARD * (j + 1)], w_ref[j])
        gx_ref[...] = acc

    rows = lambda w: pl.BlockSpec((tm, w), lambda i: (i, 0))
    return pl.pallas_call(
        _behind(body, after), name="in_bwd", grid=(s // tm,),
        in_specs=[HBM_SPEC] * len(after) + [rows(D_MODEL), rows(IN_COLS),
                                            pl.BlockSpec((N_DEV, D_MODEL, IN_SHARD), lambda i: (0, 0, 0))],
        out_specs=rows(D_MODEL),
        out_shape=jax.ShapeDtypeStruct((s, D_MODEL), F32),
        compiler_params=_params(("arbitrary",)),
    )(*after, du1, dh, wg_in)


def wgrad(name, a, b, tm, tn, shard_cols=None, tk=2048, after=()):
    kk, m = a.shape
    n = b.shape[1]
    tk = min(tk, kk)
    nk = kk // tk
    w = shard_cols

    def body(a_ref, b_ref, out_ref, acc_s):
        k = pl.program_id(2)

        @pl.when(k == 0)
        def _():
            acc_s[...] = jnp.zeros_like(acc_s)

        acc_s[...] += _dot_tn(a_ref[...].astype(BF16), b_ref[...].astype(BF16))

        @pl.when(k == nk - 1)
        def _():
            if w is None:
                out_ref[...] = acc_s[...]
            else:
                for j in range(tn // w):
                    out_ref[j] = acc_s[:, w * j:w * (j + 1)]

    if w is None:
        out_spec = pl.BlockSpec((tm, tn), lambda i, j, k: (i, j))
        out_shape = jax.ShapeDtypeStruct((m, n), F32)
    else:
        out_spec = pl.BlockSpec((tn // w, tm, w), lambda i, j, k: (j, i, 0))
        out_shape = jax.ShapeDtypeStruct((n // w, m, w), F32)
    return pl.pallas_call(
        _behind(body, after), name=name, grid=(m // tm, n // tn, nk),
        in_specs=[HBM_SPEC] * len(after) + [pl.BlockSpec((tk, tm), lambda i, j, k: (k, i)),
                  pl.BlockSpec((tk, tn), lambda i, j, k: (k, j))],
        out_specs=out_spec, out_shape=out_shape,
        scratch_shapes=[pltpu.VMEM((tm, tn), F32)],
        compiler_params=_params(("arbitrary", "arbitrary", "arbitrary")),
    )(*after, a, b)


def adamw(name, w, m, v, parts, tm=None):
    rows, cols = w.shape
    tm = rows if tm is None else min(tm, rows)
    n = len(parts)
    c1 = 1.0 - ADAM_B1 ** ADAM_STEP
    c2 = 1.0 - ADAM_B2 ** ADAM_STEP

    def body(*refs):
        w_ref, m_ref, v_ref = refs[:3]
        part_refs = refs[3:3 + n]
        g_ref, d_ref, nm_ref, nv_ref = refs[3 + n:]
        g = None
        for ref, (_, index) in zip(part_refs, parts):
            term = (ref[...] if index is None else ref[0]).astype(F32)
            g = term if g is None else g + term
        new_m = ADAM_B1 * m_ref[...] + (1.0 - ADAM_B1) * g
        new_v = ADAM_B2 * v_ref[...] + (1.0 - ADAM_B2) * (g * g)
        m_hat = new_m / c1
        v_hat = new_v / c2
        g_ref[...] = g
        d_ref[...] = -ADAM_LR * (m_hat / (jnp.sqrt(v_hat) + ADAM_EPS) + ADAM_WD * w_ref[...])
        nm_ref[...] = new_m
        nv_ref[...] = new_v

    spec = pl.BlockSpec((tm, cols), lambda i: (i, 0))
    shape = jax.ShapeDtypeStruct((rows, cols), F32)

    def part_spec(index):
        if index is None:
            return spec
        return pl.BlockSpec((1, tm, cols), lambda i: (index, i, 0))

    return pl.pallas_call(
        body, name=name, grid=(rows // tm,),
        in_specs=[spec] * 3 + [part_spec(index) for _, index in parts],
        out_specs=[spec] * 4, out_shape=[shape] * 4,
        compiler_params=_params(("arbitrary",)),
    )(w, m, v, *[array for array, _ in parts])


def _place():
    x, y, c = lax.axis_index("x"), lax.axis_index("y"), lax.axis_index("c")
    other_chips = [(1 - x, y), (x, 1 - y), (1 - x, 1 - y)]
    return x, y, c, other_chips


def place_shards(arrays):
    n = len(arrays)

    def body(*refs):
        ins, outs, stage, sems = refs[:n], refs[n:2 * n], refs[2 * n:3 * n], refs[3 * n]
        x, y, c, _ = _place()
        copies = []
        for a in range(n):
            stage[a][...] = ins[a][...].astype(BF16)
            copies.append(pltpu.make_async_copy(stage[a], outs[a].at[4 * x + 2 * y + c], sems.at[a]))
            copies[-1].start()
        for cp in copies:
            cp.wait()

    return pl.pallas_call(
        body, name="place_shards",
        in_specs=[pl.BlockSpec(memory_space=pltpu.VMEM)] * n, out_specs=[HBM_SPEC] * n,
        out_shape=[jax.ShapeDtypeStruct((N_DEV,) + a.shape, BF16) for a in arrays],
        scratch_shapes=[pltpu.VMEM(a.shape, BF16) for a in arrays] + [pltpu.SemaphoreType.DMA((n,))],
        compiler_params=_params(),
    )(*arrays)


def all_gather(name, arrays):
    n = len(arrays)

    def body(*refs):
        ins, outs = refs[:n], refs[n:2 * n]
        send_sems, recv_sems, local_sems = refs[2 * n:]
        x, y, c, chips = _place()
        me = 4 * x + 2 * y + c
        sibling = (x, y, 1 - c)

        def copy(a, k, block, to, src=None):
            return pltpu.make_async_remote_copy(
                src_ref=outs[a].at[block] if src is None else src, dst_ref=outs[a].at[block],
                send_sem=send_sems.at[a, k], recv_sem=recv_sems.at[a, k],
                device_id=to, device_id_type=MESH)

        started = []
        local = [pltpu.make_async_copy(ins[a], outs[a].at[me], local_sems.at[a]) for a in range(n)]
        for a in range(n):
            local[a].start()
            sends = [copy(a, 0, me, sibling, src=ins[a])]
            sends += [copy(a, 1 + j, me, (cx, cy, c), src=ins[a]) for j, (cx, cy) in enumerate(chips)]
            for cp in sends:
                cp.start()
            started += sends
        for a in range(n):
            for j, (cx, cy) in enumerate(chips):
                block = 4 * cx + 2 * cy + c
                copy(a, 1 + j, block, sibling).wait_recv()
                passed = copy(a, 4 + j, block, sibling)
                passed.start()
                started.append(passed)
        for a in range(n):
            copy(a, 0, 4 * x + 2 * y + (1 - c), sibling).wait_recv()
            for j, (cx, cy) in enumerate(chips):
                copy(a, 4 + j, 4 * cx + 2 * cy + (1 - c), sibling).wait_recv()
        for cp in started:
            cp.wait_send()
        for cp in local:
            cp.wait()

    return pl.pallas_call(
        body, name=name,
        in_specs=[pl.BlockSpec(memory_space=pltpu.VMEM)] * n, out_specs=[HBM_SPEC] * n,
        out_shape=[jax.ShapeDtypeStruct((N_DEV,) + a.shape, a.dtype) for a in arrays],
        scratch_shapes=[pltpu.SemaphoreType.DMA((n, 7)), pltpu.SemaphoreType.DMA((n, 7)),
                        pltpu.SemaphoreType.DMA((n,))],
    )(*arrays)


SEM_SPEC = pl.BlockSpec(memory_space=pltpu.SEMAPHORE)
DATAFLOW = pltpu.SideEffectType.DATAFLOW_SIDE_EFFECTING


def _in_hbm(a):
    return pltpu.with_memory_space_constraint(a, pltpu.HBM)


def exchange_start(name, bufs, plan, n_copies, after=()):
    nb, na = len(bufs), len(after)

    def body(*refs):
        ins = refs[:nb]
        send_sems, recv_sems = refs[nb + na], refs[nb + na + 1]
        token = refs[-1]
        for k, (src, dst, peer, _) in enumerate(plan(ins)):
            pltpu.make_async_remote_copy(src_ref=src, dst_ref=dst, send_sem=send_sems.at[k], recv_sem=recv_sems.at[k],
                                         device_id=peer, device_id_type=MESH).start()
        token[...] = jnp.zeros_like(token)

    out = pl.pallas_call(
        body, name=name,
        out_shape=(pltpu.SemaphoreType.DMA((n_copies,)), pltpu.SemaphoreType.DMA((n_copies,)),
                   *[pltpu.HBM(b.shape, b.dtype) for b in bufs], jax.ShapeDtypeStruct((8, 128), F32)),
        in_specs=[pl.BlockSpec(memory_space=pltpu.HBM)] * nb + [HBM_SPEC] * na,
        out_specs=(SEM_SPEC, SEM_SPEC, *[pl.BlockSpec(memory_space=pltpu.HBM)] * nb,
                   pl.BlockSpec(memory_space=pltpu.VMEM)),
        input_output_aliases={i: 2 + i for i in range(nb)},
        compiler_params=pltpu.CompilerParams(has_side_effects=DATAFLOW),
    )(*[_in_hbm(b) for b in bufs], *after)
    return out[0], out[1], list(out[2:2 + nb]), out[-1]


def exchange_wait(name, send_sems, recv_sems, bufs, plan, after=()):
    nb, na = len(bufs), len(after)

    def body(*refs):
        ins = refs[:nb]
        send_ref, recv_ref = refs[nb], refs[nb + 1]
        for k, (src, dst, peer, landed) in enumerate(plan(ins)):
            pltpu.make_async_remote_copy(src_ref=src, dst_ref=dst, send_sem=send_ref.at[k], recv_sem=recv_ref.at[k],
                                         device_id=peer, device_id_type=MESH).wait_send()
            pltpu.make_async_remote_copy(src_ref=src, dst_ref=landed, send_sem=send_ref.at[k], recv_sem=recv_ref.at[k],
                                         device_id=peer, device_id_type=MESH).wait_recv()

    out = pl.pallas_call(
        body, name=name,
        out_shape=tuple(pltpu.HBM(b.shape, b.dtype) for b in bufs),
        in_specs=[pl.BlockSpec(memory_space=pltpu.HBM)] * nb + [SEM_SPEC, SEM_SPEC] + [HBM_SPEC] * na,
        out_specs=tuple([pl.BlockSpec(memory_space=pltpu.HBM)] * nb),
        input_output_aliases={i: i for i in range(nb)},
        compiler_params=pltpu.CompilerParams(has_side_effects=DATAFLOW),
    )(*bufs, send_sems, recv_sems, *after)
    return list(out)


def _gather_plan(n):
    def plan(bufs):
        x, y, c, _ = _place()
        me = 4 * x + 2 * y + c
        copies = []
        for a in range(n):
            for k in range(1, N_DEV):
                px, py, pc = x ^ (k >> 2), y ^ ((k >> 1) & 1), c ^ (k & 1)
                copies.append((bufs[a].at[me], bufs[a].at[me], (px, py, pc), bufs[a].at[4 * px + 2 * py + pc]))
        return copies
    return plan


def _sibling_plan(n):
    def plan(bufs):
        x, y, c, _ = _place()
        return [(bufs[a].at[:, 1 - c], bufs[n + a], (x, y, 1 - c), bufs[n + a]) for a in range(n)]
    return plan


def _chips_plan(n):
    def plan(bufs):
        _, _, c, chips = _place()
        return [(bufs[a].at[2 * cx + cy], bufs[n + a].at[j], (cx, cy, c), bufs[n + a].at[j])
                for a in range(n) for j, (cx, cy) in enumerate(chips)]
    return plan


def chip_sum(name, grad, landed, place, tr=256):
    _, _, rows, cols = grad.shape
    tr = min(tr, rows)

    def body(place_ref, g_ref, l_ref, own_ref, out_ref):
        total = g_ref[0, 0] + l_ref[0]
        out_ref[0] = total.astype(BF16)

        @pl.when(pl.program_id(1) == place_ref[1])
        def _():
            own_ref[...] = total

    return pl.pallas_call(
        body, name=name,
        grid_spec=pltpu.PrefetchScalarGridSpec(
            num_scalar_prefetch=1, grid=(rows // tr, 4),
            in_specs=[pl.BlockSpec((1, 1, tr, cols), lambda i, ch, pos: (ch, pos[0], i, 0)),
                      pl.BlockSpec((1, tr, cols), lambda i, ch, pos: (ch, i, 0))],
            out_specs=[pl.BlockSpec((tr, cols), lambda i, ch, pos: (i, 0)),
                       pl.BlockSpec((1, tr, cols), lambda i, ch, pos: (ch, i, 0))]),
        out_shape=[jax.ShapeDtypeStruct((rows, cols), F32), jax.ShapeDtypeStruct((4, rows, cols), BF16)],
        compiler_params=_params(("arbitrary", "arbitrary")),
    )(place, grad, landed)


BIG =("w_in", "w_sb_proj", "w_ca_proj", "w_out", "w_mlp_in", "w_mlp_out")
SMALL = ("b_gate", "rel_bias", "ln1_g", "ln1_b", "ln2_g", "ln2_b")
NAMES = ("w_in", "b_gate", "w_sb_proj", "w_ca_proj", "rel_bias", "w_out", "ln1_g", "ln1_b",
         "w_mlp_in", "w_mlp_out", "ln2_g", "ln2_b")
REL_COLS = 2 * REL_CLIP + 1


def _pack_small(t, scalar=None):
    rel = t["rel_bias"]
    if rel.shape[1] != REL_PAD:
        rel = jnp.pad(rel, ((0, 0), (0, REL_PAD - rel.shape[1])))
    last = jnp.zeros((128,), F32) if scalar is None else jnp.pad(scalar.reshape(1), (0, 127))
    flat = [t["b_gate"].reshape(-1), rel.reshape(-1)] + [t[n].reshape(-1) for n in SMALL[2:]] + [last]
    return jnp.concatenate(flat).reshape(-1, 128)


def _unpack_small(p):
    flat = p.reshape(-1)
    out, off = {}, 0
    for n, size in (("b_gate", GATE_COLS), ("rel_bias", 8 * REL_PAD)) + tuple((n, D_MODEL) for n in SMALL[2:]):
        out[n] = flat[off:off + size]
        off += size
    out["rel_bias"] = out["rel_bias"].reshape(8, REL_PAD)[:, :REL_COLS]
    return out


def kernel(x, w_in, b_gate, w_sb_proj, w_ca_proj, rel_bias, w_out, ln1_g, ln1_b, w_mlp_in, w_mlp_out, ln2_g, ln2_b, loss_target, m_w_in, m_b_gate, m_w_sb_proj, m_w_ca_proj, m_rel_bias, m_w_out, m_ln1_g, m_ln1_b, m_w_mlp_in, m_w_mlp_out, m_ln2_g, m_ln2_b, v_w_in, v_b_gate, v_w_sb_proj, v_w_ca_proj, v_rel_bias, v_w_out, v_ln1_g, v_ln1_b, v_w_mlp_in, v_w_mlp_out, v_ln2_g, v_ln2_b):
    w = dict(w_in=w_in, b_gate=b_gate, w_sb_proj=w_sb_proj, w_ca_proj=w_ca_proj, rel_bias=rel_bias, w_out=w_out,
             ln1_g=ln1_g, ln1_b=ln1_b, w_mlp_in=w_mlp_in, w_mlp_out=w_mlp_out, ln2_g=ln2_g, ln2_b=ln2_b)
    m = dict(w_in=m_w_in, b_gate=m_b_gate, w_sb_proj=m_w_sb_proj, w_ca_proj=m_w_ca_proj, rel_bias=m_rel_bias,
             w_out=m_w_out, ln1_g=m_ln1_g, ln1_b=m_ln1_b, w_mlp_in=m_w_mlp_in, w_mlp_out=m_w_mlp_out,
             ln2_g=m_ln2_g, ln2_b=m_ln2_b)
    v = dict(w_in=v_w_in, b_gate=v_b_gate, w_sb_proj=v_w_sb_proj, w_ca_proj=v_w_ca_proj, rel_bias=v_rel_bias,
             w_out=v_w_out, ln1_g=v_ln1_g, ln1_b=v_ln1_b, w_mlp_in=v_w_mlp_in, w_mlp_out=v_w_mlp_out,
             ln2_g=v_ln2_g, ln2_b=v_ln2_b)

    x, target = x[0], loss_target[0]
    c = lax.axis_index("c")
    place = jnp.stack([c, 2 * lax.axis_index("x") + lax.axis_index("y")]).astype(jnp.int32)
    grads, deltas, new_m, new_v = {}, {}, {}, {}

    def by_owner(n, g):
        return g.reshape((4, 2) + w[n].shape)

    def finish(n, own, arrived):
        grads[n], deltas[n], new_m[n], new_v[n] = adamw(
            "adamw_" + n, w[n], m[n], v[n], [(own, None), (arrived, 0), (arrived, 1), (arrived, 2)], tm=256)

    def reduce_start(tag, names, g):
        bufs = [by_owner(n, g[n]) for n in names]
        bufs += [_in_hbm(lax.empty((4,) + w[n].shape, F32)) for n in names]
        return exchange_start("reduce_%s_siblings_start" % tag, bufs, _sibling_plan(len(names)), len(names))

    def reduce_middle(tag, names, started, after):
        send, recv, bufs, _ = started
        k = len(names)
        bufs = exchange_wait("reduce_%s_siblings_wait" % tag, send, recv, bufs, _sibling_plan(k), after=after)
        own, sums = zip(*[chip_sum("chip_sum_" + n, bufs[a], bufs[k + a], place) for a, n in enumerate(names)])
        zones = [_in_hbm(lax.empty((3,) + w[n].shape, BF16)) for n in names]
        return own, exchange_start("reduce_%s_chips_start" % tag, list(sums) + zones, _chips_plan(k), 3 * k)

    def reduce_end(tag, names, own, started, after):
        send, recv, bufs, _ = started
        k = len(names)
        bufs = exchange_wait("reduce_%s_chips_wait" % tag, send, recv, bufs, _chips_plan(k), after=after)
        for a, n in enumerate(names):
            finish(n, own[a], bufs[k + a])

    rest = BIG[1:]
    zones = place_shards([w[n] for n in rest])
    wg_in = all_gather("gather_w_in", [w_in.astype(BF16)])[0]
    send, recv, zones, token = exchange_start("gather_rest_start", zones, _gather_plan(len(rest)),
                                              (N_DEV - 1) * len(rest), after=(wg_in,))
    qkv, lg, xb = in_proj(x, wg_in, after=(token,))
    bias_tiles = ca_bias_tiles(rel_bias)
    o_sb = sb_fwd(qkv)
    o_ca = ca_fwd(qkv, bias_tiles)
    zones = exchange_wait("gather_rest_wait", send, recv, zones, _gather_plan(len(rest)), after=(o_sb, o_ca))
    wg = dict(zip(rest, zones))
    wg_sb, wg_ca, wg_mi = wg["w_sb_proj"], wg["w_ca_proj"], wg["w_mlp_in"]
    wg_mo = wg["w_mlp_out"].reshape(D_FF, D_MODEL)
    w_out_b = wg["w_out"].reshape(D_MODEL, D_MODEL)

    u1 = mix_fwd(x, o_sb, o_ca, lg, b_gate, wg_sb, wg_ca, w_out_b)
    hm, a, x1b, du2, loss, dg2, db2 = mlp_fwd(u1, target, wg_mi, wg_mo, ln1_g, ln1_b, ln2_g, ln2_b)
    dhm, du1, dg1, db1 = mlp_bwd(du2, hm, u1, wg_mi, wg_mo, ln1_g)
    g = {}
    g["w_mlp_in"] = wgrad("wgrad_mlp_in", x1b, dhm, tm=D_MODEL, tn=FF_SHARD, shard_cols=FF_SHARD)
    g["w_mlp_out"] = wgrad("wgrad_mlp_out", a, du2, tm=FF_SHARD, tn=D_MODEL)

    mlp = ("w_mlp_in", "w_mlp_out")
    started = reduce_start("mlp", mlp, g)
    dlg, do_sb, do_ca, dy_sb, dy_ca, merged, dbg = mix_bwd(
        du1, o_sb, o_ca, lg, b_gate, wg_sb, wg_ca, w_out_b, after=(started[3],))
    own_mlp, started_mlp = reduce_middle("mlp", mlp, started, after=(dlg,))
    g["w_out"] = wgrad("wgrad_out", merged, du1, tm=D_MODEL, tn=D_MODEL, after=(started_mlp[3],))
    g["w_sb_proj"] = wgrad("wgrad_sb_proj", o_sb, dy_sb, tm=ATT_WIDTH, tn=D_MODEL, shard_cols=PROJ_SHARD)
    g["w_ca_proj"] = wgrad("wgrad_ca_proj", o_ca, dy_ca, tm=ATT_WIDTH, tn=D_MODEL, shard_cols=PROJ_SHARD)

    att = ("w_out", "w_sb_proj", "w_ca_proj")
    started = reduce_start("att", att, g)
    dq_sb, dk_sb, dv_sb = sb_bwd(qkv, o_sb, do_sb, after=(started[3],))
    own_att, started_att = reduce_middle("att", att, started, after=(dq_sb,))
    dq_ca, dk_ca, dv_ca, db_tiles = ca_bwd(qkv, bias_tiles, do_ca, after=(started_att[3],))
    g_rel = rel_bias_grad(db_tiles)

    dh = dh_slab([dq_sb, dk_sb, dv_sb, dq_ca, dk_ca, dv_ca, dlg])
    g["w_in"] = wgrad("wgrad_in", xb, dh, tm=D_MODEL, tn=IN_SHARD, shard_cols=IN_SHARD)
    last = ("w_in",)
    started = reduce_start("in", last, g)
    grad_x = in_bwd(du1, dh, wg_in, after=(started[3],))
    own_in, started_in = reduce_middle("in", last, started, after=(grad_x,))
    reduce_end("mlp", mlp, own_mlp, started_mlp, after=(started_in[3],))
    reduce_end("att", att, own_att, started_att, after=(started_in[3],))

    small = dict(b_gate=dbg, rel_bias=g_rel, ln1_g=dg1, ln1_b=db1, ln2_g=dg2, ln2_b=db2)
    parts = all_gather("gather_small_grads", [_pack_small(small, loss[0, 0])])[0]
    packed = adamw("adamw_small", _pack_small(w), _pack_small(m), _pack_small(v),
                   [(parts, d) for d in range(N_DEV)])
    loss = packed[0][-1, 0]
    for out, p in zip((grads, deltas, new_m, new_v), packed):
        for n, val in _unpack_small(p).items():
            out[n] = val.reshape(w[n].shape)
    reduce_end("in", last, own_in, started_in, after=(packed[0], deltas["w_mlp_in"], deltas["w_out"]))

    return (loss, grad_x[None], *[grads[n] for n in NAMES], *[deltas[n] for n in NAMES],
            *[new_m[n] for n in NAMES], *[new_v[n] for n in NAMES])
```

```python
import functools

import jax
import jax.numpy as jnp
from jax import lax
from jax.experimental import pallas as pl
from jax.experimental.pallas import tpu as pltpu

F32 = jnp.float32
BF16 = jnp.bfloat16
MESH = pl.DeviceIdType.MESH

N_DEV = 8
D_MODEL = 1024
HEAD_DIM = 64
ATT_WIDTH = 512
N_PAIRS = ATT_WIDTH // 128
QKV_COLS = 6 * ATT_WIDTH
GATE_COLS = 2 * D_MODEL
IN_COLS = QKV_COLS + GATE_COLS
IN_SHARD = IN_COLS // N_DEV
D_FF = 4 * D_MODEL
FF_SHARD = D_FF // N_DEV
PROJ_SHARD = D_MODEL // N_DEV
ATT_BLOCK = 128
CA_TILES = 5
CA_ROWS = 512
CHUNK = 64
CA_PREV_CHUNKS = 8
REL_CLIP = 256
REL_PAD = 640
ALPHA = 2.0 ** 0.25
LN_EPS = 1e-5
QK_SCALE = HEAD_DIM ** -0.5
NEG_BIG = -1e30
VMEM_LIMIT = 56 * 1024 * 1024

ADAM_LR = 0.001
ADAM_B1 = 0.9
ADAM_B2 = 0.999
ADAM_EPS = 1e-08
ADAM_WD = 0.01
ADAM_STEP = 10

_NT = (((1,), (1,)), ((), ()))
_TN = (((0,), (0,)), ((), ()))


def _dot(a, b):
    return jnp.dot(a, b, preferred_element_type=F32)


def _dot_nt(a, b):
    return lax.dot_general(a, b, _NT, preferred_element_type=F32)


def _dot_tn(a, b):
    return lax.dot_general(a, b, _TN, preferred_element_type=F32)


def _params(semantics=None):
    return pltpu.CompilerParams(dimension_semantics=semantics, vmem_limit_bytes=VMEM_LIMIT)


def _row(v):
    return v.reshape(1, -1)


HBM_SPEC = pl.BlockSpec(memory_space=pl.ANY)


def _behind(body, after):
    n = len(after)

    def wrapped(*refs):
        body(*refs[n:])

    return wrapped


def in_proj(x, wg_in, tm=512, after=()):
    s = x.shape[0]

    def body(x_ref, w_ref, qkv_ref, lg_ref, xb_ref):
        xb = x_ref[...].astype(BF16)
        xb_ref[...] = xb
        for j in range(N_DEV):
            acc = _dot(xb, w_ref[j])
            lo, hi = IN_SHARD * j, IN_SHARD * (j + 1)
            if hi <= QKV_COLS:
                qkv_ref[:, lo:hi] = acc.astype(BF16)
            elif lo >= QKV_COLS:
                lg_ref[:, lo - QKV_COLS:hi - QKV_COLS] = acc
            else:
                qkv_ref[:, lo:QKV_COLS] = acc[:, :QKV_COLS - lo].astype(BF16)
                lg_ref[:, 0:hi - QKV_COLS] = acc[:, QKV_COLS - lo:]

    return pl.pallas_call(
        _behind(body, after), name="in_proj", grid=(s // tm,),
        in_specs=[HBM_SPEC] * len(after) + [pl.BlockSpec((tm, D_MODEL), lambda i: (i, 0)),
                  pl.BlockSpec((N_DEV, D_MODEL, IN_SHARD), lambda i: (0, 0, 0))],
        out_specs=[pl.BlockSpec((tm, QKV_COLS), lambda i: (i, 0)),
                   pl.BlockSpec((tm, GATE_COLS), lambda i: (i, 0)),
                   pl.BlockSpec((tm, D_MODEL), lambda i: (i, 0))],
        out_shape=[jax.ShapeDtypeStruct((s, QKV_COLS), BF16),
                   jax.ShapeDtypeStruct((s, GATE_COLS), F32),
                   jax.ShapeDtypeStruct((s, D_MODEL), BF16)],
        compiler_params=_params(("arbitrary",)),
    )(*after, x, wg_in)


def _head_masks():
    lane = lax.broadcasted_iota(jnp.int32, (1, 128), 1)
    first = lane < HEAD_DIM
    return first, jnp.logical_not(first)


SB_CHUNK = 512
SB_SUB = SB_CHUNK // ATT_BLOCK


def _tri_pair():
    row = lax.broadcasted_iota(jnp.int32, (4 * ATT_BLOCK, 2 * ATT_BLOCK), 0) & (2 * ATT_BLOCK - 1)
    col = lax.broadcasted_iota(jnp.int32, (4 * ATT_BLOCK, 2 * ATT_BLOCK), 1)
    same = (row >= ATT_BLOCK) == (col >= ATT_BLOCK)
    return jnp.logical_and(same, row >= col).astype(BF16)


def _pair_bf16(pieces):
    return jnp.concatenate(
        [jnp.concatenate([p.astype(BF16) for p in row], axis=1) for row in pieces], axis=0)


def _suffix_sums(pieces, tri_pair):
    hi = [[p.astype(BF16) for p in row] for row in pieces]
    lo = [[p - h.astype(F32) for p, h in zip(row, hrow)] for row, hrow in zip(pieces, hi)]
    return _dot(jnp.concatenate([_pair_bf16(hi), _pair_bf16(lo)], axis=1), tri_pair)


def _first_row_piece(diag, j):
    return j if diag else 0


def _piece(v, r, h):
    return v[ATT_BLOCK * r:ATT_BLOCK * (r + 1), ATT_BLOCK * h:ATT_BLOCK * (h + 1)]


def _all_row_pieces(slabs):
    return [[slab[ATT_BLOCK * r:ATT_BLOCK * (r + 1)] for r in range(SB_SUB)] for slab in slabs]


def _add_rows(total, part, r0):
    if r0 == 0:
        return total + part
    return jnp.concatenate([total[:ATT_BLOCK * r0], total[ATT_BLOCK * r0:] + part], axis=0)


def _row_totals(csum, r0, h):
    return jnp.concatenate(
        [jnp.broadcast_to(_piece(csum, at, h)[:, 0:1], (ATT_BLOCK, ATT_BLOCK)) for at in range(SB_SUB - r0)], axis=0)


SIGN_BIT = 0x80000000


def _drop(z):
    minus_abs = lax.bitcast_convert_type(lax.bitcast_convert_type(z, jnp.uint32) | jnp.uint32(SIGN_BIT), F32)
    return jnp.maximum(z, 0.0) + jnp.log(1.0 + jnp.exp(minus_abs))


def _sb_weights(z, runs, r0, masked, strict, tri_pair):
    drops = []
    for r in range(r0, SB_SUB):
        row = [_drop(_piece(z, r - r0, h)) for h in range(2)]
        drops.append([jnp.where(strict, drop, 0.0) for drop in row] if r == masked else row)
    csum = _suffix_sums(drops, tri_pair)
    weights = []
    for r in range(r0, SB_SUB):
        row = [jnp.exp(_piece(z, r - r0, h) - (_piece(csum, r - r0, h) + runs[h][r])) for h in range(2)]
        weights.append([jnp.where(strict, a, 0.0) for a in row] if r == masked else row)
    return drops, weights, csum


def _sweep_left(chunk, i, per_step):
    def step(it, carry):
        for u in range(per_step):
            chunk(i - 1 - per_step * it - u, False)
        return carry

    lax.fori_loop(0, i // per_step, step, 0)
    for left in range(1, per_step):
        @pl.when(i % per_step == left)
        def _():
            for cb in reversed(range(left)):
                chunk(i - i + cb, False)


def _by_head(blk, masks):
    zero = jnp.zeros_like(blk)
    return jnp.concatenate([jnp.where(m, blk, zero) for m in masks], axis=0)


def sb_fwd(qkv):
    s = qkv.shape[0]
    nq = s // SB_CHUNK

    def body(q_ref, k_ref, v_ref, o_ref, run_s):
        i = pl.program_id(1)
        masks = _head_masks()
        q = q_ref[...] * QK_SCALE
        row = lax.broadcasted_iota(jnp.int32, (ATT_BLOCK, ATT_BLOCK), 0)
        col = lax.broadcasted_iota(jnp.int32, (ATT_BLOCK, ATT_BLOCK), 1)
        strict = col < row
        tri_pair = _tri_pair()
        run_s[...] = jnp.zeros_like(run_s)
        o_ref[...] = jnp.zeros_like(o_ref)

        def chunk(cb, diag):
            start = cb * SB_CHUNK
            runs = [run_s[0], run_s[1]]
            acc = jnp.zeros((SB_CHUNK, ATT_BLOCK), F32)
            for j in reversed(range(SB_SUB)):
                r0 = _first_row_piece(diag, j)
                keys = pl.ds(pl.multiple_of(start + ATT_BLOCK * j, ATT_BLOCK), ATT_BLOCK)
                z = _dot_nt(q[ATT_BLOCK * r0:], _by_head(k_ref[keys, :], masks))
                _, weights, csum = _sb_weights(z, _all_row_pieces(runs), r0, j if diag else None, strict, tri_pair)
                part = _dot(_pair_bf16(weights), _by_head(v_ref[keys, :], masks))
                acc = _add_rows(acc, part, r0)
                runs = [_add_rows(runs[h], _row_totals(csum, r0, h), r0) for h in range(2)]
            run_s[0], run_s[1] = runs
            o_ref[...] += acc

        chunk(i, True)
        _sweep_left(chunk, i, per_step=4)

    return pl.pallas_call(
        body, name="sb_fwd", grid=(N_PAIRS, nq),
        in_specs=[pl.BlockSpec((SB_CHUNK, 128), lambda p, i: (i, p)),
                  pl.BlockSpec((s, 128), lambda p, i: (0, N_PAIRS + p)),
                  pl.BlockSpec((s, 128), lambda p, i: (0, 2 * N_PAIRS + p))],
        out_specs=pl.BlockSpec((SB_CHUNK, 128), lambda p, i: (i, p)),
        out_shape=jax.ShapeDtypeStruct((s, ATT_WIDTH), F32),
        scratch_shapes=[pltpu.VMEM((2, SB_CHUNK, 128), F32)],
        compiler_params=_params(("arbitrary", "arbitrary")),
    )(qkv, qkv, qkv)


def sb_bwd(qkv, o_sb, do_sb, after=()):
    s = qkv.shape[0]
    nq = s // SB_CHUNK

    def body(q_ref, k_ref, v_ref, o_ref, do_ref, dq_ref, dk_ref, dv_ref, run_l_s, run_e_s, etot_s, dq_s):
        i = pl.program_id(1)

        @pl.when(i == 0)
        def _():
            dk_ref[...] = jnp.zeros_like(dk_ref)
            dv_ref[...] = jnp.zeros_like(dv_ref)

        masks = _head_masks()
        q = q_ref[...] * QK_SCALE
        do = do_ref[...]
        prod = do.astype(F32) * o_ref[...]
        for h in range(2):
            total = jnp.sum(jnp.where(masks[h], prod, 0.0), axis=1, keepdims=True)
            etot_s[h] = jnp.broadcast_to(total, (SB_CHUNK, ATT_BLOCK))
        row = lax.broadcasted_iota(jnp.int32, (ATT_BLOCK, ATT_BLOCK), 0)
        col = lax.broadcasted_iota(jnp.int32, (ATT_BLOCK, ATT_BLOCK), 1)
        strict = col < row
        tri_pair = _tri_pair()
        run_l_s[...] = jnp.zeros_like(run_l_s)
        run_e_s[...] = jnp.zeros_like(run_e_s)
        dq_s[...] = jnp.zeros_like(dq_s)

        def chunk(cb, diag):
            start = cb * SB_CHUNK
            slabs_l = [run_l_s[0], run_l_s[1]]
            slabs_e = [run_e_s[0], run_e_s[1]]
            e_tot = _all_row_pieces([etot_s[0], etot_s[1]])
            dq = jnp.zeros((SB_CHUNK, ATT_BLOCK), F32)
            for j in reversed(range(SB_SUB)):
                runs_l, runs_e = _all_row_pieces(slabs_l), _all_row_pieces(slabs_e)
                r0 = _first_row_piece(diag, j)
                masked = j if diag else None
                keys = pl.ds(pl.multiple_of(start + ATT_BLOCK * j, ATT_BLOCK), ATT_BLOCK)
                kk = _by_head(k_ref[keys, :], masks)
                vv = _by_head(v_ref[keys, :], masks)
                q_rows, do_rows = q[ATT_BLOCK * r0:], do[ATT_BLOCK * r0:]
                z = _dot_nt(q_rows, kk)
                da = _dot_nt(do_rows, vv)
                drops, weights, csum = _sb_weights(z, runs_l, r0, masked, strict, tri_pair)
                ab = [[a.astype(BF16) for a in wrow] for wrow in weights]
                es = [[a.astype(F32) * _piece(da, at, h) for h, a in enumerate(arow)] for at, arow in enumerate(ab)]
                esum = _suffix_sums(es, tri_pair)
                dzs = []
                for at, r in enumerate(range(r0, SB_SUB)):
                    dz_row = []
                    for h in range(2):
                        sig = jnp.exp(_piece(z, at, h) - drops[at][h])
                        after = _piece(esum, at, h) + runs_e[h][r] - es[at][h]
                        dz = es[at][h] - sig * (e_tot[h][r] - after)
                        dz_row.append(jnp.where(strict, dz, 0.0) if r == masked else dz)
                    dzs.append(dz_row)
                dz = _pair_bf16(dzs)
                dq = _add_rows(dq, _dot(dz, kk), r0)
                dk = _dot_tn(dz, q_rows)
                dv = _dot_tn(_pair_bf16(ab), do_rows)
                dk_ref[keys, :] += jnp.where(masks[0], dk[:ATT_BLOCK], dk[ATT_BLOCK:])
                dv_ref[keys, :] += jnp.where(masks[0], dv[:ATT_BLOCK], dv[ATT_BLOCK:])
                slabs_l = [_add_rows(slabs_l[h], _row_totals(csum, r0, h), r0) for h in range(2)]
                slabs_e = [_add_rows(slabs_e[h], _row_totals(esum, r0, h), r0) for h in range(2)]
            run_l_s[0], run_l_s[1] = slabs_l
            run_e_s[0], run_e_s[1] = slabs_e
            dq_s[...] += dq

        chunk(i, True)
        _sweep_left(chunk, i, per_step=2)
        dq_ref[...] = (dq_s[...] * QK_SCALE).astype(BF16)

    tile = lambda p, i: (i, p)
    state = pltpu.VMEM((2, SB_CHUNK, 128), F32)
    return pl.pallas_call(
        _behind(body, after), name="sb_bwd", grid=(N_PAIRS, nq),
        in_specs=[HBM_SPEC] * len(after) + [pl.BlockSpec((SB_CHUNK, 128), tile),
                  pl.BlockSpec((s, 128), lambda p, i: (0, N_PAIRS + p)),
                  pl.BlockSpec((s, 128), lambda p, i: (0, 2 * N_PAIRS + p)),
                  pl.BlockSpec((SB_CHUNK, 128), tile),
                  pl.BlockSpec((SB_CHUNK, 128), tile)],
        out_specs=[pl.BlockSpec((SB_CHUNK, 128), tile),
                   pl.BlockSpec((s, 128), lambda p, i: (0, p)),
                   pl.BlockSpec((s, 128), lambda p, i: (0, p))],
        out_shape=[jax.ShapeDtypeStruct((s, ATT_WIDTH), BF16),
                   jax.ShapeDtypeStruct((s, ATT_WIDTH), F32),
                   jax.ShapeDtypeStruct((s, ATT_WIDTH), F32)],
        scratch_shapes=[state, state, state, pltpu.VMEM((SB_CHUNK, 128), F32)],
        compiler_params=_params(("arbitrary", "arbitrary")),
    )(*after, qkv, qkv, qkv, o_sb, do_sb)


def ca_bias_tiles(rel_bias):
    heads = rel_bias.shape[0]
    wide = 2 * ATT_BLOCK

    def body(rel_ref, out_ref):
        bins = lax.broadcasted_iota(jnp.int32, (REL_PAD, wide), 0)
        pos = lax.broadcasted_iota(jnp.int32, (REL_PAD, wide), 1)
        r = lax.broadcasted_iota(jnp.int32, (ATT_BLOCK, ATT_BLOCK), 0)
        c = lax.broadcasted_iota(jnp.int32, (ATT_BLOCK, ATT_BLOCK), 1)
        for j in range(CA_TILES):
            hot = (jnp.clip(ATT_BLOCK * (j + 1) - 1 - pos, -REL_CLIP, REL_CLIP) + REL_CLIP == bins).astype(F32)
            lines = jnp.dot(rel_ref[...], hot, preferred_element_type=F32, precision=lax.Precision.HIGHEST)
            back = 2 * j + (r >> 6) - (c >> 6)
            ok = jnp.logical_and(back >= 0, back <= CA_PREV_CHUNKS)
            for h in range(heads):
                tile = jnp.broadcast_to(lines[h:h + 1, :], (ATT_BLOCK, wide))
                tile = pltpu.roll(tile, wide - (ATT_BLOCK - 1), 1, stride=1, stride_axis=0)
                out_ref[h, j] = jnp.where(ok, tile[:, :ATT_BLOCK], NEG_BIG)

    padded = jnp.pad(rel_bias, ((0, 0), (0, REL_PAD - rel_bias.shape[1])))
    return pl.pallas_call(
        body, name="ca_bias_tiles",
        out_shape=jax.ShapeDtypeStruct((heads, CA_TILES, ATT_BLOCK, ATT_BLOCK), F32),
        compiler_params=_params(),
    )(padded)


CA_BLOCKS = CA_ROWS // ATT_BLOCK
CA_KEY_BLOCKS = CA_BLOCKS + CA_TILES - 1


def _ca_users(t):
    return [r for r in range(CA_BLOCKS) if 0 <= r + CA_TILES - 1 - t < CA_TILES]


def _ca_rows(v, t, r):
    at = _ca_users(t).index(r)
    return v[ATT_BLOCK * at:ATT_BLOCK * (at + 1)]


def _ca_window(step, k_ref, v_ref, masks):
    kks, vvs, inside, rows = [], [], [], []
    for t in range(CA_KEY_BLOCKS):
        block = step * CA_BLOCKS - (CA_TILES - 1) + t
        keys = pl.ds(pl.multiple_of(jnp.maximum(block, 0) * ATT_BLOCK, ATT_BLOCK), ATT_BLOCK)
        kks.append(_by_head(k_ref[keys, :], masks))
        vvs.append(_by_head(v_ref[keys, :], masks))
        inside.append(block >= 0)
        rows.append(keys)
    return kks, vvs, inside, rows


def _ca_probs(q, kks, inside, bias_ref):
    scores = [[[None] * CA_TILES for _ in range(2)] for _ in range(CA_BLOCKS)]
    for t in range(CA_KEY_BLOCKS):
        users = _ca_users(t)
        z = _dot_nt(q[ATT_BLOCK * users[0]:ATT_BLOCK * (users[-1] + 1)], kks[t])
        for r in users:
            j = r + CA_TILES - 1 - t
            for h in range(2):
                zz = _ca_rows(z, t, r)[:, ATT_BLOCK * h:ATT_BLOCK * (h + 1)] + bias_ref[h, j]
                scores[r][h][j] = jnp.where(inside[t], zz, NEG_BIG)
    probs = [[None, None] for _ in range(CA_BLOCKS)]
    for r in range(CA_BLOCKS):
        for h in range(2):
            m = jnp.max(functools.reduce(jnp.maximum, scores[r][h]), axis=1, keepdims=True)
            ex = [jnp.exp(z - m) for z in scores[r][h]]
            inv = 1.0 / jnp.sum(functools.reduce(jnp.add, ex), axis=1, keepdims=True)
            probs[r][h] = [e * inv for e in ex]
    return probs


def _ca_stack(tiles, t):
    return jnp.concatenate(
        [jnp.concatenate([tiles[r][h][r + CA_TILES - 1 - t].astype(BF16) for h in range(2)], axis=1)
         for r in _ca_users(t)], axis=0)


def ca_fwd(qkv, bias_tiles):
    s = qkv.shape[0]
    nq = s // CA_ROWS

    def body(q_ref, k_ref, v_ref, bias_ref, o_ref):
        masks = _head_masks()
        kks, vvs, inside, _ = _ca_window(pl.program_id(1), k_ref, v_ref, masks)
        probs = _ca_probs(q_ref[...] * QK_SCALE, kks, inside, bias_ref)
        out = [None] * CA_BLOCKS
        for t in range(CA_KEY_BLOCKS):
            part = _dot(_ca_stack(probs, t), vvs[t])
            for r in _ca_users(t):
                piece = _ca_rows(part, t, r)
                out[r] = piece if out[r] is None else out[r] + piece
        o_ref[...] = jnp.concatenate(out, axis=0)

    return pl.pallas_call(
        body, name="ca_fwd", grid=(N_PAIRS, nq),
        in_specs=[pl.BlockSpec((CA_ROWS, 128), lambda p, i: (i, 3 * N_PAIRS + p)),
                  pl.BlockSpec((s, 128), lambda p, i: (0, 4 * N_PAIRS + p)),
                  pl.BlockSpec((s, 128), lambda p, i: (0, 5 * N_PAIRS + p)),
                  pl.BlockSpec((2, CA_TILES, ATT_BLOCK, ATT_BLOCK), lambda p, i: (p, 0, 0, 0))],
        out_specs=pl.BlockSpec((CA_ROWS, 128), lambda p, i: (i, p)),
        out_shape=jax.ShapeDtypeStruct((s, ATT_WIDTH), F32),
        compiler_params=_params(("arbitrary", "arbitrary")),
    )(qkv, qkv, qkv, bias_tiles)


def ca_bwd(qkv, bias_tiles, do_ca, after=()):
    s = qkv.shape[0]
    nq = s // CA_ROWS

    def body(q_ref, k_ref, v_ref, bias_ref, do_ref, dq_ref, dk_ref, dv_ref, db_ref):
        big = pl.program_id(1)

        @pl.when(big == 0)
        def _():
            dk_ref[...] = jnp.zeros_like(dk_ref)
            dv_ref[...] = jnp.zeros_like(dv_ref)
            db_ref[...] = jnp.zeros_like(db_ref)

        masks = _head_masks()
        kks, vvs, inside, key_rows = _ca_window(big, k_ref, v_ref, masks)
        q = q_ref[...] * QK_SCALE
        do = do_ref[...]
        probs = _ca_probs(q, kks, inside, bias_ref)
        dps = [[[None] * CA_TILES for _ in range(2)] for _ in range(CA_BLOCKS)]
        for t in range(CA_KEY_BLOCKS):
            users = _ca_users(t)
            dp = _dot_nt(do[ATT_BLOCK * users[0]:ATT_BLOCK * (users[-1] + 1)], vvs[t])
            for r in users:
                for h in range(2):
                    dps[r][h][r + CA_TILES - 1 - t] = _ca_rows(dp, t, r)[:, ATT_BLOCK * h:ATT_BLOCK * (h + 1)]
        dss = [[None, None] for _ in range(CA_BLOCKS)]
        for r in range(CA_BLOCKS):
            for h in range(2):
                delta = jnp.sum(functools.reduce(jnp.add, [p * dp for p, dp in zip(probs[r][h], dps[r][h])]),
                                axis=1, keepdims=True)
                dss[r][h] = [p * (dp - delta) for p, dp in zip(probs[r][h], dps[r][h])]
        for h in range(2):
            for j in range(CA_TILES):
                db_ref[h, j] += functools.reduce(jnp.add, [dss[r][h][j] for r in range(CA_BLOCKS)])
        dq = [None] * CA_BLOCKS
        for t in range(CA_KEY_BLOCKS):
            users = _ca_users(t)
            rows = slice(ATT_BLOCK * users[0], ATT_BLOCK * (users[-1] + 1))
            ds = _ca_stack(dss, t)
            part = _dot(ds, kks[t])
            for r in users:
                piece = _ca_rows(part, t, r)
                dq[r] = piece if dq[r] is None else dq[r] + piece
            dk = _dot_tn(ds, q[rows])
            dv = _dot_tn(_ca_stack(probs, t), do[rows])
            dk_ref[key_rows[t], :] += jnp.where(masks[0], dk[:ATT_BLOCK], dk[ATT_BLOCK:])
            dv_ref[key_rows[t], :] += jnp.where(masks[0], dv[:ATT_BLOCK], dv[ATT_BLOCK:])
        dq_ref[...] = (jnp.concatenate(dq, axis=0) * QK_SCALE).astype(BF16)

    return pl.pallas_call(
        _behind(body, after), name="ca_bwd", grid=(N_PAIRS, nq),
        in_specs=[HBM_SPEC] * len(after) + [pl.BlockSpec((CA_ROWS, 128), lambda p, i: (i, 3 * N_PAIRS + p)),
                  pl.BlockSpec((s, 128), lambda p, i: (0, 4 * N_PAIRS + p)),
                  pl.BlockSpec((s, 128), lambda p, i: (0, 5 * N_PAIRS + p)),
                  pl.BlockSpec((2, CA_TILES, ATT_BLOCK, ATT_BLOCK), lambda p, i: (p, 0, 0, 0)),
                  pl.BlockSpec((CA_ROWS, 128), lambda p, i: (i, p))],
        out_specs=[pl.BlockSpec((CA_ROWS, 128), lambda p, i: (i, p)),
                   pl.BlockSpec((s, 128), lambda p, i: (0, p)),
                   pl.BlockSpec((s, 128), lambda p, i: (0, p)),
                   pl.BlockSpec((2, CA_TILES, ATT_BLOCK, ATT_BLOCK), lambda p, i: (p, 0, 0, 0))],
        out_shape=[jax.ShapeDtypeStruct((s, ATT_WIDTH), BF16),
                   jax.ShapeDtypeStruct((s, ATT_WIDTH), F32),
                   jax.ShapeDtypeStruct((s, ATT_WIDTH), F32),
                   jax.ShapeDtypeStruct((2 * N_PAIRS, CA_TILES, ATT_BLOCK, ATT_BLOCK), F32)],
        compiler_params=_params(("arbitrary", "arbitrary")),
    )(*after, qkv, qkv, qkv, bias_tiles, do_ca)


def rel_bias_grad(db_tiles):
    heads = db_tiles.shape[0]

    def body(db_ref, out_ref):
        row = lax.broadcasted_iota(jnp.int32, (ATT_BLOCK, ATT_BLOCK), 0)
        col = lax.broadcasted_iota(jnp.int32, (ATT_BLOCK, ATT_BLOCK), 1)
        flip = (row + col == ATT_BLOCK - 1).astype(F32)
        wrapped = col < row
        cp = lax.broadcasted_iota(jnp.int32, (ATT_BLOCK, REL_PAD), 0)
        bins = lax.broadcasted_iota(jnp.int32, (ATT_BLOCK, REL_PAD), 1)
        total = jnp.zeros((heads, REL_PAD), F32)
        for j in range(CA_TILES):
            sums_neg, sums_pos = [], []
            for h in range(heads):
                tile = jnp.dot(db_ref[h, j], flip, preferred_element_type=F32, precision=lax.Precision.HIGHEST)
                tile = pltpu.roll(tile, 0, 1, stride=1, stride_axis=0)
                sums_neg.append(jnp.sum(jnp.where(wrapped, 0.0, tile), axis=0, keepdims=True))
                sums_pos.append(jnp.sum(jnp.where(wrapped, tile, 0.0), axis=0, keepdims=True))
            neg = jnp.concatenate(sums_neg, axis=0)
            pos = jnp.concatenate(sums_pos, axis=0)
            hot_neg = (jnp.clip(ATT_BLOCK * j + cp - (ATT_BLOCK - 1), -REL_CLIP, REL_CLIP) + REL_CLIP == bins)
            hot_pos = (jnp.clip(ATT_BLOCK * j + cp + 1, -REL_CLIP, REL_CLIP) + REL_CLIP == bins)
            total = total + jnp.dot(neg, hot_neg.astype(F32), preferred_element_type=F32,
                                    precision=lax.Precision.HIGHEST)
            total = total + jnp.dot(pos, hot_pos.astype(F32), preferred_element_type=F32,
                                    precision=lax.Precision.HIGHEST)
        out_ref[...] = total

    return pl.pallas_call(
        body, name="rel_bias_grad",
        out_shape=jax.ShapeDtypeStruct((heads, REL_PAD), F32),
        compiler_params=_params(),
    )(db_tiles)


def _assemble_cols(dst_ref, src_ref):
    w = src_ref.shape[2]
    for j in range(N_DEV):
        dst_ref[:, w * j:w * (j + 1)] = src_ref[j]


def _merge(o_sb, o_ca, lg, bg, wsb, wca):
    y_sb = _dot(o_sb.astype(BF16), wsb)
    y_ca = _dot(o_ca.astype(BF16), wca)
    gates = jax.nn.sigmoid(lg + bg)
    g_sb, g_ca = gates[:, :D_MODEL], gates[:, D_MODEL:]
    return y_sb, y_ca, g_sb, g_ca, g_sb * y_sb + g_ca * y_ca


def mix_fwd(x, o_sb, o_ca, lg, b_gate, wg_sb, wg_ca, w_out, tm=256):
    s = x.shape[0]

    def body(x_ref, osb_ref, oca_ref, lg_ref, bg_ref, wsb_ref, wca_ref, wout_ref, u_ref, wsb_s, wca_s):
        @pl.when(pl.program_id(0) == 0)
        def _():
            _assemble_cols(wsb_s, wsb_ref)
            _assemble_cols(wca_s, wca_ref)

        merged = _merge(osb_ref[...], oca_ref[...], lg_ref[...], bg_ref[...], wsb_s[...], wca_s[...])[4]
        u_ref[...] = ALPHA * x_ref[...] + _dot(merged.astype(BF16), wout_ref[...])

    rows = lambda w: pl.BlockSpec((tm, w), lambda i: (i, 0))
    whole = lambda shape: pl.BlockSpec(shape, lambda i: (0,) * len(shape))
    return pl.pallas_call(
        body, name="mix_fwd", grid=(s // tm,),
        in_specs=[rows(D_MODEL), rows(ATT_WIDTH), rows(ATT_WIDTH), rows(GATE_COLS), whole((1, GATE_COLS)),
                  whole((N_DEV, ATT_WIDTH, PROJ_SHARD)), whole((N_DEV, ATT_WIDTH, PROJ_SHARD)),
                  whole((D_MODEL, D_MODEL))],
        out_specs=rows(D_MODEL),
        out_shape=jax.ShapeDtypeStruct((s, D_MODEL), F32),
        scratch_shapes=[pltpu.VMEM((ATT_WIDTH, D_MODEL), BF16), pltpu.VMEM((ATT_WIDTH, D_MODEL), BF16)],
        compiler_params=_params(("arbitrary",)),
    )(x, o_sb, o_ca, lg, _row(b_gate), wg_sb, wg_ca, w_out)


def mix_bwd(du1, o_sb, o_ca, lg, b_gate, wg_sb, wg_ca, w_out, tm=256, after=()):
    s = du1.shape[0]

    def body(du_ref, osb_ref, oca_ref, lg_ref, bg_ref, wsb_ref, wca_ref, wout_ref,
             dlg_ref, dosb_ref, doca_ref, dysb_ref, dyca_ref, mg_ref, dbg_ref, wsb_s, wca_s):
        @pl.when(pl.program_id(0) == 0)
        def _():
            _assemble_cols(wsb_s, wsb_ref)
            _assemble_cols(wca_s, wca_ref)
            dbg_ref[...] = jnp.zeros_like(dbg_ref)

        y_sb, y_ca, g_sb, g_ca, merged = _merge(
            osb_ref[...], oca_ref[...], lg_ref[...], bg_ref[...], wsb_s[...], wca_s[...])
        mg_ref[...] = merged.astype(BF16)
        dm = _dot_nt(du_ref[...].astype(BF16), wout_ref[...])
        dl_sb = dm * y_sb * g_sb * (1.0 - g_sb)
        dl_ca = dm * y_ca * g_ca * (1.0 - g_ca)
        dlg_ref[:, :D_MODEL] = dl_sb.astype(BF16)
        dlg_ref[:, D_MODEL:] = dl_ca.astype(BF16)
        dbg_ref[:, :D_MODEL] += jnp.sum(dl_sb, axis=0, keepdims=True)
        dbg_ref[:, D_MODEL:] += jnp.sum(dl_ca, axis=0, keepdims=True)
        dy_sb = (dm * g_sb).astype(BF16)
        dy_ca = (dm * g_ca).astype(BF16)
        dysb_ref[...] = dy_sb
        dyca_ref[...] = dy_ca
        dosb_ref[...] = _dot_nt(dy_sb, wsb_s[...]).astype(BF16)
        doca_ref[...] = _dot_nt(dy_ca, wca_s[...]).astype(BF16)

    rows = lambda w: pl.BlockSpec((tm, w), lambda i: (i, 0))
    whole = lambda shape: pl.BlockSpec(shape, lambda i: (0,) * len(shape))
    return pl.pallas_call(
        _behind(body, after), name="mix_bwd", grid=(s // tm,),
        in_specs=[HBM_SPEC] * len(after) + [rows(D_MODEL), rows(ATT_WIDTH), rows(ATT_WIDTH), rows(GATE_COLS), whole((1, GATE_COLS)),
                  whole((N_DEV, ATT_WIDTH, PROJ_SHARD)), whole((N_DEV, ATT_WIDTH, PROJ_SHARD)),
                  whole((D_MODEL, D_MODEL))],
        out_specs=[rows(GATE_COLS), rows(ATT_WIDTH), rows(ATT_WIDTH), rows(D_MODEL), rows(D_MODEL),
                   rows(D_MODEL), whole((1, GATE_COLS))],
        out_shape=[jax.ShapeDtypeStruct((s, GATE_COLS), BF16),
                   jax.ShapeDtypeStruct((s, ATT_WIDTH), BF16), jax.ShapeDtypeStruct((s, ATT_WIDTH), BF16),
                   jax.ShapeDtypeStruct((s, D_MODEL), BF16), jax.ShapeDtypeStruct((s, D_MODEL), BF16),
                   jax.ShapeDtypeStruct((s, D_MODEL), BF16), jax.ShapeDtypeStruct((1, GATE_COLS), F32)],
        scratch_shapes=[pltpu.VMEM((ATT_WIDTH, D_MODEL), BF16), pltpu.VMEM((ATT_WIDTH, D_MODEL), BF16)],
        compiler_params=_params(("arbitrary",)),
    )(*after, du1, o_sb, o_ca, lg, _row(b_gate), wg_sb, wg_ca, w_out)


def _ln_stats(u):
    mu = jnp.mean(u, axis=1, keepdims=True)
    cen = u - mu
    var = jnp.mean(cen * cen, axis=1, keepdims=True)
    rstd = lax.rsqrt(var + LN_EPS)
    return cen * rstd, rstd


def _ln_bwd(dy, xhat, rstd, gain):
    dyg = dy * gain
    m1 = jnp.mean(dyg, axis=1, keepdims=True)
    m2 = jnp.mean(dyg * xhat, axis=1, keepdims=True)
    return rstd * (dyg - m1 - xhat * m2)


def _load_mlp_weights(win_hbm, wout_hbm, win_s, wout_s, sems):
    copies = [pltpu.make_async_copy(win_hbm.at[k], win_s.at[:, pl.ds(FF_SHARD * k, FF_SHARD)], sems.at[k])
              for k in range(N_DEV)]
    copies.append(pltpu.make_async_copy(wout_hbm, wout_s, sems.at[N_DEV]))
    for cp in copies:
        cp.start()
    return copies


MLP_WEIGHT_SCRATCH = [pltpu.VMEM((D_MODEL, D_FF), BF16), pltpu.VMEM((D_FF, D_MODEL), BF16),
                      pltpu.SemaphoreType.DMA((N_DEV + 1,))]


def mlp_fwd(u1, target, wg_in, w_out, ln1_g, ln1_b, ln2_g, ln2_b, tm=256):
    s = u1.shape[0]

    def body(u_ref, t_ref, win_hbm, wout_hbm, g1_ref, b1_ref, g2_ref, b2_ref,
             hm_ref, a_ref, x1b_ref, du2_ref, loss_ref, dg2_ref, db2_ref, win_s, wout_s, sems):
        first = pl.program_id(0) == 0

        @pl.when(first)
        def _():
            copies = _load_mlp_weights(win_hbm, wout_hbm, win_s, wout_s, sems)
            loss_ref[...] = jnp.zeros_like(loss_ref)
            dg2_ref[...] = jnp.zeros_like(dg2_ref)
            db2_ref[...] = jnp.zeros_like(db2_ref)
            for cp in copies:
                cp.wait()

        xhat, _ = _ln_stats(u_ref[...])
        x1 = xhat * g1_ref[...] + b1_ref[...]
        x1b = x1.astype(BF16)
        x1b_ref[...] = x1b
        for k in range(N_DEV):
            cols = slice(FF_SHARD * k, FF_SHARD * (k + 1))
            hm = _dot(x1b, win_s[:, cols])
            hm_ref[:, cols] = hm
            r = jnp.maximum(hm, 0.0)
            a_ref[:, cols] = (r * r).astype(BF16)
        xhat, rstd = _ln_stats(ALPHA * x1 + _dot(a_ref[...], wout_s[...]))
        diff = xhat * g2_ref[...] + b2_ref[...] - t_ref[...]
        per_token = jnp.mean(diff * diff, axis=1, keepdims=True)
        loss_ref[...] += 0.5 * jnp.sum(per_token, axis=0, keepdims=True)
        dy = diff * (1.0 / D_MODEL)
        dg2_ref[...] += jnp.sum(dy * xhat, axis=0, keepdims=True)
        db2_ref[...] += jnp.sum(dy, axis=0, keepdims=True)
        du2_ref[...] = _ln_bwd(dy, xhat, rstd, g2_ref[...])

    rows = pl.BlockSpec((tm, D_MODEL), lambda i: (i, 0))
    vec = pl.BlockSpec((1, D_MODEL), lambda i: (0, 0))
    ff = pl.BlockSpec((tm, D_FF), lambda i: (i, 0))
    return pl.pallas_call(
        body, name="mlp_fwd", grid=(s // tm,),
        in_specs=[rows, rows, HBM_SPEC, HBM_SPEC, vec, vec, vec, vec],
        out_specs=[ff, ff, rows, rows, pl.BlockSpec((1, 1), lambda i: (0, 0)), vec, vec],
        out_shape=[jax.ShapeDtypeStruct((s, D_FF), F32), jax.ShapeDtypeStruct((s, D_FF), BF16),
                   jax.ShapeDtypeStruct((s, D_MODEL), BF16), jax.ShapeDtypeStruct((s, D_MODEL), F32),
                   jax.ShapeDtypeStruct((1, 1), F32),
                   jax.ShapeDtypeStruct((1, D_MODEL), F32), jax.ShapeDtypeStruct((1, D_MODEL), F32)],
        scratch_shapes=MLP_WEIGHT_SCRATCH,
        compiler_params=_params(("arbitrary",)),
    )(u1, target, wg_in, w_out, _row(ln1_g), _row(ln1_b), _row(ln2_g), _row(ln2_b))


def mlp_bwd(du2, hm, u1, wg_in, w_out, ln1_g, tm=256):
    s = du2.shape[0]

    def body(du2_ref, hm_ref, u_ref, win_hbm, wout_hbm, g1_ref,
             dhm_ref, du1_ref, dg1_ref, db1_ref, win_s, wout_s, sems):
        first = pl.program_id(0) == 0

        @pl.when(first)
        def _():
            copies = _load_mlp_weights(win_hbm, wout_hbm, win_s, wout_s, sems)
            dg1_ref[...] = jnp.zeros_like(dg1_ref)
            db1_ref[...] = jnp.zeros_like(db1_ref)
            for cp in copies:
                cp.wait()

        du2 = du2_ref[...]
        du2b = du2.astype(BF16)
        for k in range(N_DEV):
            cols = slice(FF_SHARD * k, FF_SHARD * (k + 1))
            da = _dot_nt(du2b, wout_s[cols, :])
            dhm_ref[:, cols] = (2.0 * jnp.maximum(hm_ref[:, cols], 0.0) * da).astype(BF16)
        dx1 = ALPHA * du2 + _dot_nt(dhm_ref[...], win_s[...])
        xhat, rstd = _ln_stats(u_ref[...])
        dg1_ref[...] += jnp.sum(dx1 * xhat, axis=0, keepdims=True)
        db1_ref[...] += jnp.sum(dx1, axis=0, keepdims=True)
        du1_ref[...] = _ln_bwd(dx1, xhat, rstd, g1_ref[...])

    rows = pl.BlockSpec((tm, D_MODEL), lambda i: (i, 0))
    vec = pl.BlockSpec((1, D_MODEL), lambda i: (0, 0))
    ff = pl.BlockSpec((tm, D_FF), lambda i: (i, 0))
    return pl.pallas_call(
        body, name="mlp_bwd", grid=(s // tm,),
        in_specs=[rows, ff, rows, HBM_SPEC, HBM_SPEC, vec],
        out_specs=[ff, rows, vec, vec],
        out_shape=[jax.ShapeDtypeStruct((s, D_FF), BF16), jax.ShapeDtypeStruct((s, D_MODEL), F32),
                   jax.ShapeDtypeStruct((1, D_MODEL), F32), jax.ShapeDtypeStruct((1, D_MODEL), F32)],
        scratch_shapes=MLP_WEIGHT_SCRATCH,
        compiler_params=_params(("arbitrary",)),
    )(du2, hm, u1, wg_in, w_out, _row(ln1_g))


def dh_slab(pieces, tm=512):
    s = pieces[0].shape[0]
    widths = [p.shape[1] for p in pieces]
    assert sum(widths) == IN_COLS
    n = len(pieces)

    def body(*refs):
        dh_ref = refs[n]
        off = 0
        for ref, w in zip(refs[:n], widths):
            dh_ref[:, off:off + w] = ref[...].astype(BF16)
            off += w

    rows = lambda w: pl.BlockSpec((tm, w), lambda i: (i, 0))
    return pl.pallas_call(
        body, name="dh_slab", grid=(s // tm,),
        in_specs=[rows(w) for w in widths], out_specs=rows(IN_COLS),
        out_shape=jax.ShapeDtypeStruct((s, IN_COLS), BF16),
        compiler_params=_params(("arbitrary",)),
    )(*pieces)


def in_bwd(du1, dh, wg_in, tm=512, after=()):
    s = du1.shape[0]

    def body(du_ref, dh_ref, w_ref, gx_ref):
        acc = ALPHA * du_ref[...]
        for j in range(N_DEV):
            acc = acc + _dot_nt(dh_ref[:, IN_SHARD * j:IN_SHARD * (j + 1)], w_ref[j])
        gx_ref[...] = acc

    rows = lambda w: pl.BlockSpec((tm, w), lambda i: (i, 0))
    return pl.pallas_call(
        _behind(body, after), name="in_bwd", grid=(s // tm,),
        in_specs=[HBM_SPEC] * len(after) + [rows(D_MODEL), rows(IN_COLS),
                                            pl.BlockSpec((N_DEV, D_MODEL, IN_SHARD), lambda i: (0, 0, 0))],
        out_specs=rows(D_MODEL),
        out_shape=jax.ShapeDtypeStruct((s, D_MODEL), F32),
        compiler_params=_params(("arbitrary",)),
    )(*after, du1, dh, wg_in)


def wgrad(name, a, b, tm, tn, shard_cols=None, tk=2048, after=()):
    kk, m = a.shape
    n = b.shape[1]
    while kk % tk:
        tk //= 2
    nk = kk // tk
    w = shard_cols

    def body(a_ref, b_ref, out_ref, acc_s):
        k = pl.program_id(2)

        @pl.when(k == 0)
        def _():
            acc_s[...] = jnp.zeros_like(acc_s)

        acc_s[...] += _dot_tn(a_ref[...].astype(BF16), b_ref[...].astype(BF16))

        @pl.when(k == nk - 1)
        def _():
            if w is None:
                out_ref[...] = acc_s[...]
            else:
                for j in range(tn // w):
                    out_ref[j] = acc_s[:, w * j:w * (j + 1)]

    if w is None:
        out_spec = pl.BlockSpec((tm, tn), lambda i, j, k: (i, j))
        out_shape = jax.ShapeDtypeStruct((m, n), F32)
    else:
        out_spec = pl.BlockSpec((tn // w, tm, w), lambda i, j, k: (j, i, 0))
        out_shape = jax.ShapeDtypeStruct((n // w, m, w), F32)
    return pl.pallas_call(
        _behind(body, after), name=name, grid=(m // tm, n // tn, nk),
        in_specs=[HBM_SPEC] * len(after) + [pl.BlockSpec((tk, tm), lambda i, j, k: (k, i)),
                  pl.BlockSpec((tk, tn), lambda i, j, k: (k, j))],
        out_specs=out_spec, out_shape=out_shape,
        scratch_shapes=[pltpu.VMEM((tm, tn), F32)],
        compiler_params=_params(("arbitrary", "arbitrary", "arbitrary")),
    )(*after, a, b)


def adamw(name, w, m, v, parts, tm=None):
    rows, cols = w.shape
    tm = rows if tm is None else min(tm, rows)
    n = len(parts)
    c1 = 1.0 - ADAM_B1 ** ADAM_STEP
    c2 = 1.0 - ADAM_B2 ** ADAM_STEP

    def body(*refs):
        w_ref, m_ref, v_ref = refs[:3]
        part_refs = refs[3:3 + n]
        g_ref, d_ref, nm_ref, nv_ref = refs[3 + n:]
        g = None
        for ref, (_, index) in zip(part_refs, parts):
            term = (ref[...] if index is None else ref[0]).astype(F32)
            g = term if g is None else g + term
        new_m = ADAM_B1 * m_ref[...] + (1.0 - ADAM_B1) * g
        new_v = ADAM_B2 * v_ref[...] + (1.0 - ADAM_B2) * (g * g)
        m_hat = new_m / c1
        v_hat = new_v / c2
        g_ref[...] = g
        d_ref[...] = -ADAM_LR * (m_hat / (jnp.sqrt(v_hat) + ADAM_EPS) + ADAM_WD * w_ref[...])
        nm_ref[...] = new_m
        nv_ref[...] = new_v

    spec = pl.BlockSpec((tm, cols), lambda i: (i, 0))
    shape = jax.ShapeDtypeStruct((rows, cols), F32)

    def part_spec(index):
        if index is None:
            return spec
        return pl.BlockSpec((1, tm, cols), lambda i: (index, i, 0))

    return pl.pallas_call(
        body, name=name, grid=(rows // tm,),
        in_specs=[spec] * 3 + [part_spec(index) for _, index in parts],
        out_specs=[spec] * 4, out_shape=[shape] * 4,
        compiler_params=_params(("arbitrary",)),
    )(w, m, v, *[array for array, _ in parts])


def _place():
    x, y, c = lax.axis_index("x"), lax.axis_index("y"), lax.axis_index("c")
    other_chips = [(1 - x, y), (x, 1 - y), (1 - x, 1 - y)]
    return x, y, c, other_chips


def place_shards(arrays):
    n = len(arrays)

    def body(*refs):
        ins, outs, stage, sems = refs[:n], refs[n:2 * n], refs[2 * n:3 * n], refs[3 * n]
        x, y, c, _ = _place()
        copies = []
        for a in range(n):
            stage[a][...] = ins[a][...].astype(BF16)
            copies.append(pltpu.make_async_copy(stage[a], outs[a].at[4 * x + 2 * y + c], sems.at[a]))
            copies[-1].start()
        for cp in copies:
            cp.wait()

    return pl.pallas_call(
        body, name="place_shards",
        in_specs=[pl.BlockSpec(memory_space=pltpu.VMEM)] * n, out_specs=[HBM_SPEC] * n,
        out_shape=[jax.ShapeDtypeStruct((N_DEV,) + a.shape, BF16) for a in arrays],
        scratch_shapes=[pltpu.VMEM(a.shape, BF16) for a in arrays] + [pltpu.SemaphoreType.DMA((n,))],
        compiler_params=_params(),
    )(*arrays)


def all_gather(name, arrays):
    n = len(arrays)

    def body(*refs):
        ins, outs = refs[:n], refs[n:2 * n]
        send_sems, recv_sems, local_sems = refs[2 * n:]
        x, y, c, chips = _place()
        me = 4 * x + 2 * y + c
        sibling = (x, y, 1 - c)

        def copy(a, k, block, to, src=None):
            return pltpu.make_async_remote_copy(
                src_ref=outs[a].at[block] if src is None else src, dst_ref=outs[a].at[block],
                send_sem=send_sems.at[a, k], recv_sem=recv_sems.at[a, k],
                device_id=to, device_id_type=MESH)

        started = []
        local = [pltpu.make_async_copy(ins[a], outs[a].at[me], local_sems.at[a]) for a in range(n)]
        for a in range(n):
            local[a].start()
            sends = [copy(a, 0, me, sibling, src=ins[a])]
            sends += [copy(a, 1 + j, me, (cx, cy, c), src=ins[a]) for j, (cx, cy) in enumerate(chips)]
            for cp in sends:
                cp.start()
            started += sends
        for a in range(n):
            for j, (cx, cy) in enumerate(chips):
                block = 4 * cx + 2 * cy + c
                copy(a, 1 + j, block, sibling).wait_recv()
                passed = copy(a, 4 + j, block, sibling)
                passed.start()
                started.append(passed)
        for a in range(n):
            copy(a, 0, 4 * x + 2 * y + (1 - c), sibling).wait_recv()
            for j, (cx, cy) in enumerate(chips):
                copy(a, 4 + j, 4 * cx + 2 * cy + (1 - c), sibling).wait_recv()
        for cp in started:
            cp.wait_send()
        for cp in local:
            cp.wait()

    return pl.pallas_call(
        body, name=name,
        in_specs=[pl.BlockSpec(memory_space=pltpu.VMEM)] * n, out_specs=[HBM_SPEC] * n,
        out_shape=[jax.ShapeDtypeStruct((N_DEV,) + a.shape, a.dtype) for a in arrays],
        scratch_shapes=[pltpu.SemaphoreType.DMA((n, 7)), pltpu.SemaphoreType.DMA((n, 7)),
                        pltpu.SemaphoreType.DMA((n,))],
    )(*arrays)


SEM_SPEC = pl.BlockSpec(memory_space=pltpu.SEMAPHORE)
DATAFLOW = pltpu.SideEffectType.DATAFLOW_SIDE_EFFECTING


def _in_hbm(a):
    return pltpu.with_memory_space_constraint(a, pltpu.HBM)


def exchange_start(name, bufs, plan, n_copies, after=()):
    nb, na = len(bufs), len(after)

    def body(*refs):
        ins = refs[:nb]
        send_sems, recv_sems = refs[nb + na], refs[nb + na + 1]
        token = refs[-1]
        for k, (src, dst, peer, _) in enumerate(plan(ins)):
            pltpu.make_async_remote_copy(src_ref=src, dst_ref=dst, send_sem=send_sems.at[k], recv_sem=recv_sems.at[k],
                                         device_id=peer, device_id_type=MESH).start()
        token[...] = jnp.zeros_like(token)

    out = pl.pallas_call(
        body, name=name,
        out_shape=(pltpu.SemaphoreType.DMA((n_copies,)), pltpu.SemaphoreType.DMA((n_copies,)),
                   *[pltpu.HBM(b.shape, b.dtype) for b in bufs], jax.ShapeDtypeStruct((8, 128), F32)),
        in_specs=[pl.BlockSpec(memory_space=pltpu.HBM)] * nb + [HBM_SPEC] * na,
        out_specs=(SEM_SPEC, SEM_SPEC, *[pl.BlockSpec(memory_space=pltpu.HBM)] * nb,
                   pl.BlockSpec(memory_space=pltpu.VMEM)),
        input_output_aliases={i: 2 + i for i in range(nb)},
        compiler_params=pltpu.CompilerParams(has_side_effects=DATAFLOW),
    )(*[_in_hbm(b) for b in bufs], *after)
    return out[0], out[1], list(out[2:2 + nb]), out[-1]


def exchange_wait(name, send_sems, recv_sems, bufs, plan, after=()):
    nb, na = len(bufs), len(after)

    def body(*refs):
        ins = refs[:nb]
        send_ref, recv_ref = refs[nb], refs[nb + 1]
        for k, (src, dst, peer, landed) in enumerate(plan(ins)):
            pltpu.make_async_remote_copy(src_ref=src, dst_ref=dst, send_sem=send_ref.at[k], recv_sem=recv_ref.at[k],
                                         device_id=peer, device_id_type=MESH).wait_send()
            pltpu.make_async_remote_copy(src_ref=src, dst_ref=landed, send_sem=send_ref.at[k], recv_sem=recv_ref.at[k],
                                         device_id=peer, device_id_type=MESH).wait_recv()

    out = pl.pallas_call(
        body, name=name,
        out_shape=tuple(pltpu.HBM(b.shape, b.dtype) for b in bufs),
        in_specs=[pl.BlockSpec(memory_space=pltpu.HBM)] * nb + [SEM_SPEC, SEM_SPEC] + [HBM_SPEC] * na,
        out_specs=tuple([pl.BlockSpec(memory_space=pltpu.HBM)] * nb),
        input_output_aliases={i: i for i in range(nb)},
        compiler_params=pltpu.CompilerParams(has_side_effects=DATAFLOW),
    )(*bufs, send_sems, recv_sems, *after)
    return list(out)


def _gather_plan(n):
    def plan(bufs):
        x, y, c, _ = _place()
        me = 4 * x + 2 * y + c
        copies = []
        for a in range(n):
            for k in range(1, N_DEV):
                px, py, pc = x ^ (k >> 2), y ^ ((k >> 1) & 1), c ^ (k & 1)
                copies.append((bufs[a].at[me], bufs[a].at[me], (px, py, pc), bufs[a].at[4 * px + 2 * py + pc]))
        return copies
    return plan


def _sibling_plan(n):
    def plan(bufs):
        x, y, c, _ = _place()
        return [(bufs[a].at[:, 1 - c], bufs[n + a], (x, y, 1 - c), bufs[n + a]) for a in range(n)]
    return plan


def _chips_plan(n):
    def plan(bufs):
        _, _, c, chips = _place()
        return [(bufs[a].at[2 * cx + cy], bufs[n + a].at[j], (cx, cy, c), bufs[n + a].at[j])
                for a in range(n) for j, (cx, cy) in enumerate(chips)]
    return plan


def chip_sum(name, grad, landed, place, tr=256):
    _, _, rows, cols = grad.shape
    tr = min(tr, rows)

    def body(place_ref, g_ref, l_ref, own_ref, out_ref):
        total = g_ref[0, 0] + l_ref[0]
        out_ref[0] = total.astype(BF16)

        @pl.when(pl.program_id(1) == place_ref[1])
        def _():
            own_ref[...] = total

    return pl.pallas_call(
        body, name=name,
        grid_spec=pltpu.PrefetchScalarGridSpec(
            num_scalar_prefetch=1, grid=(rows // tr, 4),
            in_specs=[pl.BlockSpec((1, 1, tr, cols), lambda i, ch, pos: (ch, pos[0], i, 0)),
                      pl.BlockSpec((1, tr, cols), lambda i, ch, pos: (ch, i, 0))],
            out_specs=[pl.BlockSpec((tr, cols), lambda i, ch, pos: (i, 0)),
                       pl.BlockSpec((1, tr, cols), lambda i, ch, pos: (ch, i, 0))]),
        out_shape=[jax.ShapeDtypeStruct((rows, cols), F32), jax.ShapeDtypeStruct((4, rows, cols), BF16)],
        compiler_params=_params(("arbitrary", "arbitrary")),
    )(place, grad, landed)


BIG =("w_in", "w_sb_proj", "w_ca_proj", "w_out", "w_mlp_in", "w_mlp_out")
SMALL = ("b_gate", "rel_bias", "ln1_g", "ln1_b", "ln2_g", "ln2_b")
NAMES = ("w_in", "b_gate", "w_sb_proj", "w_ca_proj", "rel_bias", "w_out", "ln1_g", "ln1_b",
         "w_mlp_in", "w_mlp_out", "ln2_g", "ln2_b")
REL_COLS = 2 * REL_CLIP + 1


def _pack_small(t, scalar=None):
    rel = t["rel_bias"]
    if rel.shape[1] != REL_PAD:
        rel = jnp.pad(rel, ((0, 0), (0, REL_PAD - rel.shape[1])))
    last = jnp.zeros((128,), F32) if scalar is None else jnp.pad(scalar.reshape(1), (0, 127))
    flat = [t["b_gate"].reshape(-1), rel.reshape(-1)] + [t[n].reshape(-1) for n in SMALL[2:]] + [last]
    return jnp.concatenate(flat).reshape(-1, 128)


def _unpack_small(p):
    flat = p.reshape(-1)
    out, off = {}, 0
    for n, size in (("b_gate", GATE_COLS), ("rel_bias", 8 * REL_PAD)) + tuple((n, D_MODEL) for n in SMALL[2:]):
        out[n] = flat[off:off + size]
        off += size
    out["rel_bias"] = out["rel_bias"].reshape(8, REL_PAD)[:, :REL_COLS]
    return out


def kernel(x, w_in, b_gate, w_sb_proj, w_ca_proj, rel_bias, w_out, ln1_g, ln1_b, w_mlp_in, w_mlp_out, ln2_g, ln2_b, loss_target, m_w_in, m_b_gate, m_w_sb_proj, m_w_ca_proj, m_rel_bias, m_w_out, m_ln1_g, m_ln1_b, m_w_mlp_in, m_w_mlp_out, m_ln2_g, m_ln2_b, v_w_in, v_b_gate, v_w_sb_proj, v_w_ca_proj, v_rel_bias, v_w_out, v_ln1_g, v_ln1_b, v_w_mlp_in, v_w_mlp_out, v_ln2_g, v_ln2_b):
    w = dict(w_in=w_in, b_gate=b_gate, w_sb_proj=w_sb_proj, w_ca_proj=w_ca_proj, rel_bias=rel_bias, w_out=w_out,
             ln1_g=ln1_g, ln1_b=ln1_b, w_mlp_in=w_mlp_in, w_mlp_out=w_mlp_out, ln2_g=ln2_g, ln2_b=ln2_b)
    m = dict(w_in=m_w_in, b_gate=m_b_gate, w_sb_proj=m_w_sb_proj, w_ca_proj=m_w_ca_proj, rel_bias=m_rel_bias,
             w_out=m_w_out, ln1_g=m_ln1_g, ln1_b=m_ln1_b, w_mlp_in=m_w_mlp_in, w_mlp_out=m_w_mlp_out,
             ln2_g=m_ln2_g, ln2_b=m_ln2_b)
    v = dict(w_in=v_w_in, b_gate=v_b_gate, w_sb_proj=v_w_sb_proj, w_ca_proj=v_w_ca_proj, rel_bias=v_rel_bias,
             w_out=v_w_out, ln1_g=v_ln1_g, ln1_b=v_ln1_b, w_mlp_in=v_w_mlp_in, w_mlp_out=v_w_mlp_out,
             ln2_g=v_ln2_g, ln2_b=v_ln2_b)

    x, target = x[0], loss_target[0]
    c = lax.axis_index("c")
    place = jnp.stack([c, 2 * lax.axis_index("x") + lax.axis_index("y")]).astype(jnp.int32)
    grads, deltas, new_m, new_v = {}, {}, {}, {}

    def by_owner(n, g):
        return g.reshape((4, 2) + w[n].shape)

    def finish(n, own, arrived):
        grads[n], deltas[n], new_m[n], new_v[n] = adamw(
            "adamw_" + n, w[n], m[n], v[n], [(own, None), (arrived, 0), (arrived, 1), (arrived, 2)], tm=256)

    def reduce_start(tag, names, g):
        bufs = [by_owner(n, g[n]) for n in names]
        bufs += [_in_hbm(lax.empty((4,) + w[n].shape, F32)) for n in names]
        return exchange_start("reduce_%s_siblings_start" % tag, bufs, _sibling_plan(len(names)), len(names))

    def reduce_middle(tag, names, started, after):
        send, recv, bufs, _ = started
        k = len(names)
        bufs = exchange_wait("reduce_%s_siblings_wait" % tag, send, recv, bufs, _sibling_plan(k), after=after)
        own, sums = zip(*[chip_sum("chip_sum_" + n, bufs[a], bufs[k + a], place) for a, n in enumerate(names)])
        zones = [_in_hbm(lax.empty((3,) + w[n].shape, BF16)) for n in names]
        return own, exchange_start("reduce_%s_chips_start" % tag, list(sums) + zones, _chips_plan(k), 3 * k)

    def reduce_end(tag, names, own, started, after):
        send, recv, bufs, _ = started
        k = len(names)
        bufs = exchange_wait("reduce_%s_chips_wait" % tag, send, recv, bufs, _chips_plan(k), after=after)
        for a, n in enumerate(names):
            finish(n, own[a], bufs[k + a])

    rest = BIG[1:]
    zones = place_shards([w[n] for n in rest])
    wg_in = all_gather("gather_w_in", [w_in.astype(BF16)])[0]
    send, recv, zones, token = exchange_start("gather_rest_start", zones, _gather_plan(len(rest)),
                                              (N_DEV - 1) * len(rest), after=(wg_in,))
    qkv, lg, xb = in_proj(x, wg_in, after=(token,))
    bias_tiles = ca_bias_tiles(rel_bias)
    o_sb = sb_fwd(qkv)
    o_ca = ca_fwd(qkv, bias_tiles)
    zones = exchange_wait("gather_rest_wait", send, recv, zones, _gather_plan(len(rest)), after=(o_sb, o_ca))
    wg = dict(zip(rest, zones))
    wg_sb, wg_ca, wg_mi = wg["w_sb_proj"], wg["w_ca_proj"], wg["w_mlp_in"]
    wg_mo = wg["w_mlp_out"].reshape(D_FF, D_MODEL)
    w_out_b = wg["w_out"].reshape(D_MODEL, D_MODEL)

    u1 = mix_fwd(x, o_sb, o_ca, lg, b_gate, wg_sb, wg_ca, w_out_b)
    hm, a, x1b, du2, loss, dg2, db2 = mlp_fwd(u1, target, wg_mi, wg_mo, ln1_g, ln1_b, ln2_g, ln2_b)
    dhm, du1, dg1, db1 = mlp_bwd(du2, hm, u1, wg_mi, wg_mo, ln1_g)
    g = {}
    g["w_mlp_in"] = wgrad("wgrad_mlp_in", x1b, dhm, tm=D_MODEL, tn=FF_SHARD, shard_cols=FF_SHARD)
    g["w_mlp_out"] = wgrad("wgrad_mlp_out", a, du2, tm=FF_SHARD, tn=D_MODEL)

    mlp = ("w_mlp_in", "w_mlp_out")
    started = reduce_start("mlp", mlp, g)
    dlg, do_sb, do_ca, dy_sb, dy_ca, merged, dbg = mix_bwd(
        du1, o_sb, o_ca, lg, b_gate, wg_sb, wg_ca, w_out_b, after=(started[3],))
    own_mlp, started_mlp = reduce_middle("mlp", mlp, started, after=(dlg,))
    g["w_out"] = wgrad("wgrad_out", merged, du1, tm=D_MODEL, tn=D_MODEL, after=(started_mlp[3],))
    g["w_sb_proj"] = wgrad("wgrad_sb_proj", o_sb, dy_sb, tm=ATT_WIDTH, tn=D_MODEL, shard_cols=PROJ_SHARD)
    g["w_ca_proj"] = wgrad("wgrad_ca_proj", o_ca, dy_ca, tm=ATT_WIDTH, tn=D_MODEL, shard_cols=PROJ_SHARD)

    att = ("w_out", "w_sb_proj", "w_ca_proj")
    started = reduce_start("att", att, g)
    dq_sb, dk_sb, dv_sb = sb_bwd(qkv, o_sb, do_sb, after=(started[3],))
    own_att, started_att = reduce_middle("att", att, started, after=(dq_sb,))
    dq_ca, dk_ca, dv_ca, db_tiles = ca_bwd(qkv, bias_tiles, do_ca, after=(started_att[3],))
    g_rel = rel_bias_grad(db_tiles)

    dh = dh_slab([dq_sb, dk_sb, dv_sb, dq_ca, dk_ca, dv_ca, dlg])
    g["w_in"] = wgrad("wgrad_in", xb, dh, tm=D_MODEL, tn=IN_SHARD, shard_cols=IN_SHARD)
    last = ("w_in",)
    started = reduce_start("in", last, g)
    reduce_end("mlp", mlp, own_mlp, started_mlp, after=(started[3],))
    reduce_end("att", att, own_att, started_att, after=(started[3],))
    own_in, started_in = reduce_middle("in", last, started, after=(deltas["w_mlp_in"], deltas["w_out"]))
    grad_x = in_bwd(du1, dh, wg_in, after=(started_in[3],))

    small = dict(b_gate=dbg, rel_bias=g_rel, ln1_g=dg1, ln1_b=db1, ln2_g=dg2, ln2_b=db2)
    parts = all_gather("gather_small_grads", [_pack_small(small, loss[0, 0])])[0]
    packed = adamw("adamw_small", _pack_small(w), _pack_small(m), _pack_small(v),
                   [(parts, d) for d in range(N_DEV)])
    loss = packed[0][-1, 0]
    for out, p in zip((grads, deltas, new_m, new_v), packed):
        for n, val in _unpack_small(p).items():
            out[n] = val.reshape(w[n].shape)
    reduce_end("in", last, own_in, started_in, after=(packed[0], grad_x))

    return (loss, grad_x[None], *[grads[n] for n in NAMES], *[deltas[n] for n in NAMES],
            *[new_m[n] for n in NAMES], *[new_v[n] for n in NAMES])
```

```python
import functools
from typing import NamedTuple

import jax
import jax.numpy as jnp
from jax import lax
from jax.experimental import pallas as pl
from jax.experimental.pallas import tpu as pltpu

F32 = jnp.float32
BF16 = jnp.bfloat16
MESH = pl.DeviceIdType.MESH

N_DEV = 8
D_MODEL = 1024
HEAD_DIM = 64
ATT_WIDTH = 512
N_PAIRS = ATT_WIDTH // 128
QKV_COLS = 6 * ATT_WIDTH
GATE_COLS = 2 * D_MODEL
IN_COLS = QKV_COLS + GATE_COLS
IN_SHARD = IN_COLS // N_DEV
D_FF = 4 * D_MODEL
FF_SHARD = D_FF // N_DEV
PROJ_SHARD = D_MODEL // N_DEV
ATT_BLOCK = 128
CA_TILES = 5
CA_ROWS = 512
CHUNK = 64
CA_PREV_CHUNKS = 8
REL_CLIP = 256
REL_PAD = 640
ALPHA = 2.0 ** 0.25
LN_EPS = 1e-5
QK_SCALE = HEAD_DIM ** -0.5
NEG_BIG = -1e30
VMEM_LIMIT = 56 * 1024 * 1024

ADAM_LR = 0.001
ADAM_B1 = 0.9
ADAM_B2 = 0.999
ADAM_EPS = 1e-08
ADAM_WD = 0.01
ADAM_STEP = 10

_NT = (((1,), (1,)), ((), ()))
_TN = (((0,), (0,)), ((), ()))


def _dot(a, b):
    return jnp.dot(a, b, preferred_element_type=F32)


def _dot_nt(a, b):
    return lax.dot_general(a, b, _NT, preferred_element_type=F32)


def _dot_tn(a, b):
    return lax.dot_general(a, b, _TN, preferred_element_type=F32)


def _params(semantics=None):
    return pltpu.CompilerParams(dimension_semantics=semantics, vmem_limit_bytes=VMEM_LIMIT)


def _row(v):
    return v.reshape(1, -1)


HBM_SPEC = pl.BlockSpec(memory_space=pl.ANY)


class Exchange(NamedTuple):
    bufs: tuple
    plan: object
    copies: int


def _pallas(body, guest, args, *, grid, in_specs, out_specs, out_shape, scratch_shapes=(), **kw):
    out_specs, out_shape, scratch_shapes = list(out_specs), list(out_shape), list(scratch_shapes)
    if guest is None:
        return pl.pallas_call(body, grid=grid, in_specs=in_specs, out_specs=out_specs, out_shape=out_shape,
                              scratch_shapes=scratch_shapes, **kw)(*args)
    nb, n_own = len(guest.bufs), len(in_specs) + len(out_specs)

    def hosting(*refs):
        zones = refs[nb + n_own:2 * nb + n_own]
        send_sems, recv_sems = refs[-2], refs[-1]
        steps = [pl.program_id(axis) for axis in range(len(grid))]
        first = functools.reduce(jnp.logical_and, [step == 0 for step in steps])
        last = functools.reduce(jnp.logical_and, [step == n - 1 for step, n in zip(steps, grid)])

        def copy(k, src, dst, peer):
            return pltpu.make_async_remote_copy(src_ref=src, dst_ref=dst, send_sem=send_sems.at[k],
                                                recv_sem=recv_sems.at[k], device_id=peer, device_id_type=MESH)

        @pl.when(first)
        def _():
            for k, (src, dst, peer, _) in enumerate(guest.plan(zones)):
                copy(k, src, dst, peer).start()

        body(*refs[nb:nb + n_own], *refs[2 * nb + n_own:-2])

        @pl.when(last)
        def _():
            for k, (src, dst, peer, landed) in enumerate(guest.plan(zones)):
                copy(k, src, dst, peer).wait_send()
                copy(k, src, landed, peer).wait_recv()

    sems = pltpu.SemaphoreType.DMA((guest.copies,))
    results = pl.pallas_call(
        hosting, grid=grid, in_specs=[HBM_SPEC] * nb + list(in_specs), out_specs=out_specs + [HBM_SPEC] * nb,
        out_shape=out_shape + [jax.ShapeDtypeStruct(b.shape, b.dtype) for b in guest.bufs],
        input_output_aliases={k: len(out_specs) + k for k in range(nb)},
        scratch_shapes=scratch_shapes + [sems, sems], **kw)(*guest.bufs, *args)
    return results[:len(out_specs)], results[len(out_specs):]


def in_proj(x, wg_in, tm=512):
    s = x.shape[0]

    def body(x_ref, w_ref, qkv_ref, lg_ref, xb_ref):
        xb = x_ref[...].astype(BF16)
        xb_ref[...] = xb
        for j in range(N_DEV):
            acc = _dot(xb, w_ref[j])
            lo, hi = IN_SHARD * j, IN_SHARD * (j + 1)
            if hi <= QKV_COLS:
                qkv_ref[:, lo:hi] = acc.astype(BF16)
            elif lo >= QKV_COLS:
                lg_ref[:, lo - QKV_COLS:hi - QKV_COLS] = acc
            else:
                qkv_ref[:, lo:QKV_COLS] = acc[:, :QKV_COLS - lo].astype(BF16)
                lg_ref[:, 0:hi - QKV_COLS] = acc[:, QKV_COLS - lo:]

    return pl.pallas_call(
        body, name="in_proj", grid=(s // tm,),
        in_specs=[pl.BlockSpec((tm, D_MODEL), lambda i: (i, 0)),
                  pl.BlockSpec((N_DEV, D_MODEL, IN_SHARD), lambda i: (0, 0, 0))],
        out_specs=[pl.BlockSpec((tm, QKV_COLS), lambda i: (i, 0)),
                   pl.BlockSpec((tm, GATE_COLS), lambda i: (i, 0)),
                   pl.BlockSpec((tm, D_MODEL), lambda i: (i, 0))],
        out_shape=[jax.ShapeDtypeStruct((s, QKV_COLS), BF16),
                   jax.ShapeDtypeStruct((s, GATE_COLS), F32),
                   jax.ShapeDtypeStruct((s, D_MODEL), BF16)],
        compiler_params=_params(("arbitrary",)),
    )(x, wg_in)


def _head_masks():
    lane = lax.broadcasted_iota(jnp.int32, (1, 128), 1)
    first = lane < HEAD_DIM
    return first, jnp.logical_not(first)


SB_CHUNK = 512
SB_SUB = SB_CHUNK // ATT_BLOCK


def _tri_pair():
    row = lax.broadcasted_iota(jnp.int32, (4 * ATT_BLOCK, 2 * ATT_BLOCK), 0) & (2 * ATT_BLOCK - 1)
    col = lax.broadcasted_iota(jnp.int32, (4 * ATT_BLOCK, 2 * ATT_BLOCK), 1)
    same = (row >= ATT_BLOCK) == (col >= ATT_BLOCK)
    return jnp.logical_and(same, row >= col).astype(BF16)


def _pair_bf16(pieces):
    return jnp.concatenate(
        [jnp.concatenate([p.astype(BF16) for p in row], axis=1) for row in pieces], axis=0)


def _suffix_sums(pieces, tri_pair):
    hi = [[p.astype(BF16) for p in row] for row in pieces]
    lo = [[p - h.astype(F32) for p, h in zip(row, hrow)] for row, hrow in zip(pieces, hi)]
    return _dot(jnp.concatenate([_pair_bf16(hi), _pair_bf16(lo)], axis=1), tri_pair)


def _first_row_piece(diag, j):
    return j if diag else 0


def _piece(v, r, h):
    return v[ATT_BLOCK * r:ATT_BLOCK * (r + 1), ATT_BLOCK * h:ATT_BLOCK * (h + 1)]


def _all_row_pieces(slabs):
    return [[slab[ATT_BLOCK * r:ATT_BLOCK * (r + 1)] for r in range(SB_SUB)] for slab in slabs]


def _add_rows(total, part, r0):
    if r0 == 0:
        return total + part
    return jnp.concatenate([total[:ATT_BLOCK * r0], total[ATT_BLOCK * r0:] + part], axis=0)


def _row_totals(csum, r0, h):
    return jnp.concatenate(
        [jnp.broadcast_to(_piece(csum, at, h)[:, 0:1], (ATT_BLOCK, ATT_BLOCK)) for at in range(SB_SUB - r0)], axis=0)


SIGN_BIT = 0x80000000


def _drop(z):
    minus_abs = lax.bitcast_convert_type(lax.bitcast_convert_type(z, jnp.uint32) | jnp.uint32(SIGN_BIT), F32)
    return jnp.maximum(z, 0.0) + jnp.log(1.0 + jnp.exp(minus_abs))


def _sb_weights(z, runs, r0, masked, strict, tri_pair):
    drops = []
    for r in range(r0, SB_SUB):
        row = [_drop(_piece(z, r - r0, h)) for h in range(2)]
        drops.append([jnp.where(strict, drop, 0.0) for drop in row] if r == masked else row)
    csum = _suffix_sums(drops, tri_pair)
    weights = []
    for r in range(r0, SB_SUB):
        row = [jnp.exp(_piece(z, r - r0, h) - (_piece(csum, r - r0, h) + runs[h][r])) for h in range(2)]
        weights.append([jnp.where(strict, a, 0.0) for a in row] if r == masked else row)
    return drops, weights, csum


EXP_UNDERFLOW = 104.0


def _sweep_left(chunk, i, dropped_ref):
    def more(cb):
        return jnp.logical_and(cb >= 0, jnp.min(dropped_ref[...]) < EXP_UNDERFLOW)

    def step(cb):
        chunk(cb, False)
        return cb - 1

    lax.while_loop(more, step, i - 1)


def _by_head(blk, masks):
    zero = jnp.zeros_like(blk)
    return jnp.concatenate([jnp.where(m, blk, zero) for m in masks], axis=0)


def sb_fwd(qkv, guest=None):
    s = qkv.shape[0]
    nq = s // SB_CHUNK

    def body(q_ref, k_ref, v_ref, o_ref, run_s):
        i = pl.program_id(1)
        masks = _head_masks()
        q = q_ref[...] * QK_SCALE
        row = lax.broadcasted_iota(jnp.int32, (ATT_BLOCK, ATT_BLOCK), 0)
        col = lax.broadcasted_iota(jnp.int32, (ATT_BLOCK, ATT_BLOCK), 1)
        strict = col < row
        tri_pair = _tri_pair()
        run_s[...] = jnp.zeros_like(run_s)
        o_ref[...] = jnp.zeros_like(o_ref)

        def chunk(cb, diag):
            start = cb * SB_CHUNK
            runs = [run_s[0], run_s[1]]
            acc = jnp.zeros((SB_CHUNK, ATT_BLOCK), F32)
            for j in reversed(range(SB_SUB)):
                r0 = _first_row_piece(diag, j)
                keys = pl.ds(pl.multiple_of(start + ATT_BLOCK * j, ATT_BLOCK), ATT_BLOCK)
                z = _dot_nt(q[ATT_BLOCK * r0:], _by_head(k_ref[keys, :], masks))
                _, weights, csum = _sb_weights(z, _all_row_pieces(runs), r0, j if diag else None, strict, tri_pair)
                part = _dot(_pair_bf16(weights), _by_head(v_ref[keys, :], masks))
                acc = _add_rows(acc, part, r0)
                runs = [_add_rows(runs[h], _row_totals(csum, r0, h), r0) for h in range(2)]
            run_s[0], run_s[1] = runs
            o_ref[...] += acc

        chunk(i, True)
        _sweep_left(chunk, i, run_s)

    return _pallas(
        body, guest, (qkv, qkv, qkv), name="sb_fwd", grid=(N_PAIRS, nq),
        in_specs=[pl.BlockSpec((SB_CHUNK, 128), lambda p, i: (i, p)),
                  pl.BlockSpec((s, 128), lambda p, i: (0, N_PAIRS + p)),
                  pl.BlockSpec((s, 128), lambda p, i: (0, 2 * N_PAIRS + p))],
        out_specs=[pl.BlockSpec((SB_CHUNK, 128), lambda p, i: (i, p))],
        out_shape=[jax.ShapeDtypeStruct((s, ATT_WIDTH), F32)],
        scratch_shapes=[pltpu.VMEM((2, SB_CHUNK, 128), F32)],
        compiler_params=_params(("arbitrary", "arbitrary")),
    )


def sb_bwd(qkv, o_sb, do_sb, guest=None):
    s = qkv.shape[0]
    nq = s // SB_CHUNK

    def body(q_ref, k_ref, v_ref, o_ref, do_ref, dq_ref, dk_ref, dv_ref, run_l_s, run_e_s, etot_s, dq_s):
        i = pl.program_id(1)

        @pl.when(i == 0)
        def _():
            dk_ref[...] = jnp.zeros_like(dk_ref)
            dv_ref[...] = jnp.zeros_like(dv_ref)

        masks = _head_masks()
        q = q_ref[...] * QK_SCALE
        do = do_ref[...]
        prod = do.astype(F32) * o_ref[...]
        for h in range(2):
            total = jnp.sum(jnp.where(masks[h], prod, 0.0), axis=1, keepdims=True)
            etot_s[h] = jnp.broadcast_to(total, (SB_CHUNK, ATT_BLOCK))
        row = lax.broadcasted_iota(jnp.int32, (ATT_BLOCK, ATT_BLOCK), 0)
        col = lax.broadcasted_iota(jnp.int32, (ATT_BLOCK, ATT_BLOCK), 1)
        strict = col < row
        tri_pair = _tri_pair()
        run_l_s[...] = jnp.zeros_like(run_l_s)
        run_e_s[...] = jnp.zeros_like(run_e_s)
        dq_s[...] = jnp.zeros_like(dq_s)

        def chunk(cb, diag):
            start = cb * SB_CHUNK
            slabs_l = [run_l_s[0], run_l_s[1]]
            slabs_e = [run_e_s[0], run_e_s[1]]
            e_tot = _all_row_pieces([etot_s[0], etot_s[1]])
            dq = jnp.zeros((SB_CHUNK, ATT_BLOCK), F32)
            for j in reversed(range(SB_SUB)):
                runs_l, runs_e = _all_row_pieces(slabs_l), _all_row_pieces(slabs_e)
                r0 = _first_row_piece(diag, j)
                masked = j if diag else None
                keys = pl.ds(pl.multiple_of(start + ATT_BLOCK * j, ATT_BLOCK), ATT_BLOCK)
                kk = _by_head(k_ref[keys, :], masks)
                vv = _by_head(v_ref[keys, :], masks)
                q_rows, do_rows = q[ATT_BLOCK * r0:], do[ATT_BLOCK * r0:]
                z = _dot_nt(q_rows, kk)
                da = _dot_nt(do_rows, vv)
                drops, weights, csum = _sb_weights(z, runs_l, r0, masked, strict, tri_pair)
                ab = [[a.astype(BF16) for a in wrow] for wrow in weights]
                es = [[a.astype(F32) * _piece(da, at, h) for h, a in enumerate(arow)] for at, arow in enumerate(ab)]
                esum = _suffix_sums(es, tri_pair)
                dzs = []
                for at, r in enumerate(range(r0, SB_SUB)):
                    dz_row = []
                    for h in range(2):
                        sig = jnp.exp(_piece(z, at, h) - drops[at][h])
                        after = _piece(esum, at, h) + runs_e[h][r] - es[at][h]
                        dz = es[at][h] - sig * (e_tot[h][r] - after)
                        dz_row.append(jnp.where(strict, dz, 0.0) if r == masked else dz)
                    dzs.append(dz_row)
                dz = _pair_bf16(dzs)
                dq = _add_rows(dq, _dot(dz, kk), r0)
                dk = _dot_tn(dz, q_rows)
                dv = _dot_tn(_pair_bf16(ab), do_rows)
                dk_ref[keys, :] += jnp.where(masks[0], dk[:ATT_BLOCK], dk[ATT_BLOCK:])
                dv_ref[keys, :] += jnp.where(masks[0], dv[:ATT_BLOCK], dv[ATT_BLOCK:])
                slabs_l = [_add_rows(slabs_l[h], _row_totals(csum, r0, h), r0) for h in range(2)]
                slabs_e = [_add_rows(slabs_e[h], _row_totals(esum, r0, h), r0) for h in range(2)]
            run_l_s[0], run_l_s[1] = slabs_l
            run_e_s[0], run_e_s[1] = slabs_e
            dq_s[...] += dq

        chunk(i, True)
        _sweep_left(chunk, i, run_l_s)
        dq_ref[...] = (dq_s[...] * QK_SCALE).astype(BF16)

    tile = lambda p, i: (i, p)
    state = pltpu.VMEM((2, SB_CHUNK, 128), F32)
    return _pallas(
        body, guest, (qkv, qkv, qkv, o_sb, do_sb), name="sb_bwd", grid=(N_PAIRS, nq),
        in_specs=[pl.BlockSpec((SB_CHUNK, 128), tile),
                  pl.BlockSpec((s, 128), lambda p, i: (0, N_PAIRS + p)),
                  pl.BlockSpec((s, 128), lambda p, i: (0, 2 * N_PAIRS + p)),
                  pl.BlockSpec((SB_CHUNK, 128), tile),
                  pl.BlockSpec((SB_CHUNK, 128), tile)],
        out_specs=[pl.BlockSpec((SB_CHUNK, 128), tile),
                   pl.BlockSpec((s, 128), lambda p, i: (0, p)),
                   pl.BlockSpec((s, 128), lambda p, i: (0, p))],
        out_shape=[jax.ShapeDtypeStruct((s, ATT_WIDTH), BF16),
                   jax.ShapeDtypeStruct((s, ATT_WIDTH), F32),
                   jax.ShapeDtypeStruct((s, ATT_WIDTH), F32)],
        scratch_shapes=[state, state, state, pltpu.VMEM((SB_CHUNK, 128), F32)],
        compiler_params=_params(("arbitrary", "arbitrary")),
    )


def ca_bias_tiles(rel_bias):
    heads = rel_bias.shape[0]
    wide = 2 * ATT_BLOCK

    def body(rel_ref, out_ref):
        bins = lax.broadcasted_iota(jnp.int32, (REL_PAD, wide), 0)
        pos = lax.broadcasted_iota(jnp.int32, (REL_PAD, wide), 1)
        r = lax.broadcasted_iota(jnp.int32, (ATT_BLOCK, ATT_BLOCK), 0)
        c = lax.broadcasted_iota(jnp.int32, (ATT_BLOCK, ATT_BLOCK), 1)
        for j in range(CA_TILES):
            hot = (jnp.clip(ATT_BLOCK * (j + 1) - 1 - pos, -REL_CLIP, REL_CLIP) + REL_CLIP == bins).astype(F32)
            lines = jnp.dot(rel_ref[...], hot, preferred_element_type=F32, precision=lax.Precision.HIGHEST)
            back = 2 * j + (r >> 6) - (c >> 6)
            ok = jnp.logical_and(back >= 0, back <= CA_PREV_CHUNKS)
            for h in range(heads):
                tile = jnp.broadcast_to(lines[h:h + 1, :], (ATT_BLOCK, wide))
                tile = pltpu.roll(tile, wide - (ATT_BLOCK - 1), 1, stride=1, stride_axis=0)
                out_ref[h, j] = jnp.where(ok, tile[:, :ATT_BLOCK], NEG_BIG)

    padded = jnp.pad(rel_bias, ((0, 0), (0, REL_PAD - rel_bias.shape[1])))
    return pl.pallas_call(
        body, name="ca_bias_tiles",
        out_shape=jax.ShapeDtypeStruct((heads, CA_TILES, ATT_BLOCK, ATT_BLOCK), F32),
        compiler_params=_params(),
    )(padded)


CA_BLOCKS = CA_ROWS // ATT_BLOCK
CA_KEY_BLOCKS = CA_BLOCKS + CA_TILES - 1


def _ca_users(t):
    return [r for r in range(CA_BLOCKS) if 0 <= r + CA_TILES - 1 - t < CA_TILES]


def _ca_rows(v, t, r):
    at = _ca_users(t).index(r)
    return v[ATT_BLOCK * at:ATT_BLOCK * (at + 1)]


def _ca_window(step, k_ref, v_ref, masks):
    kks, vvs, inside, rows = [], [], [], []
    for t in range(CA_KEY_BLOCKS):
        block = step * CA_BLOCKS - (CA_TILES - 1) + t
        keys = pl.ds(pl.multiple_of(jnp.maximum(block, 0) * ATT_BLOCK, ATT_BLOCK), ATT_BLOCK)
        kks.append(_by_head(k_ref[keys, :], masks))
        vvs.append(_by_head(v_ref[keys, :], masks))
        inside.append(block >= 0)
        rows.append(keys)
    return kks, vvs, inside, rows


def _ca_probs(q, kks, inside, bias_ref):
    scores = [[[None] * CA_TILES for _ in range(2)] for _ in range(CA_BLOCKS)]
    for t in range(CA_KEY_BLOCKS):
        users = _ca_users(t)
        z = _dot_nt(q[ATT_BLOCK * users[0]:ATT_BLOCK * (users[-1] + 1)], kks[t])
        for r in users:
            j = r + CA_TILES - 1 - t
            for h in range(2):
                zz = _ca_rows(z, t, r)[:, ATT_BLOCK * h:ATT_BLOCK * (h + 1)] + bias_ref[h, j]
                scores[r][h][j] = jnp.where(inside[t], zz, NEG_BIG)
    probs = [[None, None] for _ in range(CA_BLOCKS)]
    for r in range(CA_BLOCKS):
        for h in range(2):
            m = jnp.max(functools.reduce(jnp.maximum, scores[r][h]), axis=1, keepdims=True)
            ex = [jnp.exp(z - m) for z in scores[r][h]]
            inv = 1.0 / jnp.sum(functools.reduce(jnp.add, ex), axis=1, keepdims=True)
            probs[r][h] = [e * inv for e in ex]
    return probs


def _ca_stack(tiles, t):
    return jnp.concatenate(
        [jnp.concatenate([tiles[r][h][r + CA_TILES - 1 - t].astype(BF16) for h in range(2)], axis=1)
         for r in _ca_users(t)], axis=0)


def ca_fwd(qkv, bias_tiles, guest=None):
    s = qkv.shape[0]
    nq = s // CA_ROWS

    def body(q_ref, k_ref, v_ref, bias_ref, o_ref):
        masks = _head_masks()
        kks, vvs, inside, _ = _ca_window(pl.program_id(1), k_ref, v_ref, masks)
        probs = _ca_probs(q_ref[...] * QK_SCALE, kks, inside, bias_ref)
        out = [None] * CA_BLOCKS
        for t in range(CA_KEY_BLOCKS):
            part = _dot(_ca_stack(probs, t), vvs[t])
            for r in _ca_users(t):
                piece = _ca_rows(part, t, r)
                out[r] = piece if out[r] is None else out[r] + piece
        o_ref[...] = jnp.concatenate(out, axis=0)

    return _pallas(
        body, guest, (qkv, qkv, qkv, bias_tiles), name="ca_fwd", grid=(N_PAIRS, nq),
        in_specs=[pl.BlockSpec((CA_ROWS, 128), lambda p, i: (i, 3 * N_PAIRS + p)),
                  pl.BlockSpec((s, 128), lambda p, i: (0, 4 * N_PAIRS + p)),
                  pl.BlockSpec((s, 128), lambda p, i: (0, 5 * N_PAIRS + p)),
                  pl.BlockSpec((2, CA_TILES, ATT_BLOCK, ATT_BLOCK), lambda p, i: (p, 0, 0, 0))],
        out_specs=[pl.BlockSpec((CA_ROWS, 128), lambda p, i: (i, p))],
        out_shape=[jax.ShapeDtypeStruct((s, ATT_WIDTH), F32)],
        compiler_params=_params(("arbitrary", "arbitrary")),
    )


def ca_bwd(qkv, bias_tiles, do_ca, guest=None):
    s = qkv.shape[0]
    nq = s // CA_ROWS

    def body(q_ref, k_ref, v_ref, bias_ref, do_ref, dq_ref, dk_ref, dv_ref, db_ref):
        big = pl.program_id(1)

        @pl.when(big == 0)
        def _():
            dk_ref[...] = jnp.zeros_like(dk_ref)
            dv_ref[...] = jnp.zeros_like(dv_ref)
            db_ref[...] = jnp.zeros_like(db_ref)

        masks = _head_masks()
        kks, vvs, inside, key_rows = _ca_window(big, k_ref, v_ref, masks)
        q = q_ref[...] * QK_SCALE
        do = do_ref[...]
        probs = _ca_probs(q, kks, inside, bias_ref)
        dps = [[[None] * CA_TILES for _ in range(2)] for _ in range(CA_BLOCKS)]
        for t in range(CA_KEY_BLOCKS):
            users = _ca_users(t)
            dp = _dot_nt(do[ATT_BLOCK * users[0]:ATT_BLOCK * (users[-1] + 1)], vvs[t])
            for r in users:
                for h in range(2):
                    dps[r][h][r + CA_TILES - 1 - t] = _ca_rows(dp, t, r)[:, ATT_BLOCK * h:ATT_BLOCK * (h + 1)]
        dss = [[None, None] for _ in range(CA_BLOCKS)]
        for r in range(CA_BLOCKS):
            for h in range(2):
                delta = jnp.sum(functools.reduce(jnp.add, [p * dp for p, dp in zip(probs[r][h], dps[r][h])]),
                                axis=1, keepdims=True)
                dss[r][h] = [p * (dp - delta) for p, dp in zip(probs[r][h], dps[r][h])]
        for h in range(2):
            for j in range(CA_TILES):
                db_ref[h, j] += functools.reduce(jnp.add, [dss[r][h][j] for r in range(CA_BLOCKS)])
        dq = [None] * CA_BLOCKS
        for t in range(CA_KEY_BLOCKS):
            users = _ca_users(t)
            rows = slice(ATT_BLOCK * users[0], ATT_BLOCK * (users[-1] + 1))
            ds = _ca_stack(dss, t)
            part = _dot(ds, kks[t])
            for r in users:
                piece = _ca_rows(part, t, r)
                dq[r] = piece if dq[r] is None else dq[r] + piece
            dk = _dot_tn(ds, q[rows])
            dv = _dot_tn(_ca_stack(probs, t), do[rows])
            dk_ref[key_rows[t], :] += jnp.where(masks[0], dk[:ATT_BLOCK], dk[ATT_BLOCK:])
            dv_ref[key_rows[t], :] += jnp.where(masks[0], dv[:ATT_BLOCK], dv[ATT_BLOCK:])
        dq_ref[...] = (jnp.concatenate(dq, axis=0) * QK_SCALE).astype(BF16)

    return _pallas(
        body, guest, (qkv, qkv, qkv, bias_tiles, do_ca), name="ca_bwd", grid=(N_PAIRS, nq),
        in_specs=[pl.BlockSpec((CA_ROWS, 128), lambda p, i: (i, 3 * N_PAIRS + p)),
                  pl.BlockSpec((s, 128), lambda p, i: (0, 4 * N_PAIRS + p)),
                  pl.BlockSpec((s, 128), lambda p, i: (0, 5 * N_PAIRS + p)),
                  pl.BlockSpec((2, CA_TILES, ATT_BLOCK, ATT_BLOCK), lambda p, i: (p, 0, 0, 0)),
                  pl.BlockSpec((CA_ROWS, 128), lambda p, i: (i, p))],
        out_specs=[pl.BlockSpec((CA_ROWS, 128), lambda p, i: (i, p)),
                   pl.BlockSpec((s, 128), lambda p, i: (0, p)),
                   pl.BlockSpec((s, 128), lambda p, i: (0, p)),
                   pl.BlockSpec((2, CA_TILES, ATT_BLOCK, ATT_BLOCK), lambda p, i: (p, 0, 0, 0))],
        out_shape=[jax.ShapeDtypeStruct((s, ATT_WIDTH), BF16),
                   jax.ShapeDtypeStruct((s, ATT_WIDTH), F32),
                   jax.ShapeDtypeStruct((s, ATT_WIDTH), F32),
                   jax.ShapeDtypeStruct((2 * N_PAIRS, CA_TILES, ATT_BLOCK, ATT_BLOCK), F32)],
        compiler_params=_params(("arbitrary", "arbitrary")),
    )


def rel_bias_grad(db_tiles):
    heads = db_tiles.shape[0]

    def body(db_ref, out_ref):
        row = lax.broadcasted_iota(jnp.int32, (ATT_BLOCK, ATT_BLOCK), 0)
        col = lax.broadcasted_iota(jnp.int32, (ATT_BLOCK, ATT_BLOCK), 1)
        flip = (row + col == ATT_BLOCK - 1).astype(F32)
        wrapped = col < row
        cp = lax.broadcasted_iota(jnp.int32, (ATT_BLOCK, REL_PAD), 0)
        bins = lax.broadcasted_iota(jnp.int32, (ATT_BLOCK, REL_PAD), 1)
        total = jnp.zeros((heads, REL_PAD), F32)
        for j in range(CA_TILES):
            sums_neg, sums_pos = [], []
            for h in range(heads):
                tile = jnp.dot(db_ref[h, j], flip, preferred_element_type=F32, precision=lax.Precision.HIGHEST)
                tile = pltpu.roll(tile, 0, 1, stride=1, stride_axis=0)
                sums_neg.append(jnp.sum(jnp.where(wrapped, 0.0, tile), axis=0, keepdims=True))
                sums_pos.append(jnp.sum(jnp.where(wrapped, tile, 0.0), axis=0, keepdims=True))
            neg = jnp.concatenate(sums_neg, axis=0)
            pos = jnp.concatenate(sums_pos, axis=0)
            hot_neg = (jnp.clip(ATT_BLOCK * j + cp - (ATT_BLOCK - 1), -REL_CLIP, REL_CLIP) + REL_CLIP == bins)
            hot_pos = (jnp.clip(ATT_BLOCK * j + cp + 1, -REL_CLIP, REL_CLIP) + REL_CLIP == bins)
            total = total + jnp.dot(neg, hot_neg.astype(F32), preferred_element_type=F32,
                                    precision=lax.Precision.HIGHEST)
            total = total + jnp.dot(pos, hot_pos.astype(F32), preferred_element_type=F32,
                                    precision=lax.Precision.HIGHEST)
        out_ref[...] = total

    return pl.pallas_call(
        body, name="rel_bias_grad",
        out_shape=jax.ShapeDtypeStruct((heads, REL_PAD), F32),
        compiler_params=_params(),
    )(db_tiles)


def _assemble_cols(dst_ref, src_ref):
    w = src_ref.shape[2]
    for j in range(N_DEV):
        dst_ref[:, w * j:w * (j + 1)] = src_ref[j]


def _merge(o_sb, o_ca, lg, bg, wsb, wca):
    y_sb = _dot(o_sb.astype(BF16), wsb)
    y_ca = _dot(o_ca.astype(BF16), wca)
    gates = jax.nn.sigmoid(lg + bg)
    g_sb, g_ca = gates[:, :D_MODEL], gates[:, D_MODEL:]
    return y_sb, y_ca, g_sb, g_ca, g_sb * y_sb + g_ca * y_ca


def mix_fwd(x, o_sb, o_ca, lg, b_gate, wg_sb, wg_ca, w_out, tm=256):
    s = x.shape[0]

    def body(x_ref, osb_ref, oca_ref, lg_ref, bg_ref, wsb_ref, wca_ref, wout_ref, u_ref, wsb_s, wca_s):
        @pl.when(pl.program_id(0) == 0)
        def _():
            _assemble_cols(wsb_s, wsb_ref)
            _assemble_cols(wca_s, wca_ref)

        merged = _merge(osb_ref[...], oca_ref[...], lg_ref[...], bg_ref[...], wsb_s[...], wca_s[...])[4]
        u_ref[...] = ALPHA * x_ref[...] + _dot(merged.astype(BF16), wout_ref[...])

    rows = lambda w: pl.BlockSpec((tm, w), lambda i: (i, 0))
    whole = lambda shape: pl.BlockSpec(shape, lambda i: (0,) * len(shape))
    return pl.pallas_call(
        body, name="mix_fwd", grid=(s // tm,),
        in_specs=[rows(D_MODEL), rows(ATT_WIDTH), rows(ATT_WIDTH), rows(GATE_COLS), whole((1, GATE_COLS)),
                  whole((N_DEV, ATT_WIDTH, PROJ_SHARD)), whole((N_DEV, ATT_WIDTH, PROJ_SHARD)),
                  whole((D_MODEL, D_MODEL))],
        out_specs=rows(D_MODEL),
        out_shape=jax.ShapeDtypeStruct((s, D_MODEL), F32),
        scratch_shapes=[pltpu.VMEM((ATT_WIDTH, D_MODEL), BF16), pltpu.VMEM((ATT_WIDTH, D_MODEL), BF16)],
        compiler_params=_params(("arbitrary",)),
    )(x, o_sb, o_ca, lg, _row(b_gate), wg_sb, wg_ca, w_out)


def mix_bwd(du1, o_sb, o_ca, lg, b_gate, wg_sb, wg_ca, w_out, tm=256, guest=None):
    s = du1.shape[0]

    def body(du_ref, osb_ref, oca_ref, lg_ref, bg_ref, wsb_ref, wca_ref, wout_ref,
             dlg_ref, dosb_ref, doca_ref, dysb_ref, dyca_ref, mg_ref, dbg_ref, wsb_s, wca_s):
        @pl.when(pl.program_id(0) == 0)
        def _():
            _assemble_cols(wsb_s, wsb_ref)
            _assemble_cols(wca_s, wca_ref)
            dbg_ref[...] = jnp.zeros_like(dbg_ref)

        y_sb, y_ca, g_sb, g_ca, merged = _merge(
            osb_ref[...], oca_ref[...], lg_ref[...], bg_ref[...], wsb_s[...], wca_s[...])
        mg_ref[...] = merged.astype(BF16)
        dm = _dot_nt(du_ref[...].astype(BF16), wout_ref[...])
        dl_sb = dm * y_sb * g_sb * (1.0 - g_sb)
        dl_ca = dm * y_ca * g_ca * (1.0 - g_ca)
        dlg_ref[:, :D_MODEL] = dl_sb.astype(BF16)
        dlg_ref[:, D_MODEL:] = dl_ca.astype(BF16)
        dbg_ref[:, :D_MODEL] += jnp.sum(dl_sb, axis=0, keepdims=True)
        dbg_ref[:, D_MODEL:] += jnp.sum(dl_ca, axis=0, keepdims=True)
        dy_sb = (dm * g_sb).astype(BF16)
        dy_ca = (dm * g_ca).astype(BF16)
        dysb_ref[...] = dy_sb
        dyca_ref[...] = dy_ca
        dosb_ref[...] = _dot_nt(dy_sb, wsb_s[...]).astype(BF16)
        doca_ref[...] = _dot_nt(dy_ca, wca_s[...]).astype(BF16)

    rows = lambda w: pl.BlockSpec((tm, w), lambda i: (i, 0))
    whole = lambda shape: pl.BlockSpec(shape, lambda i: (0,) * len(shape))
    return _pallas(
        body, guest, (du1, o_sb, o_ca, lg, _row(b_gate), wg_sb, wg_ca, w_out), name="mix_bwd", grid=(s // tm,),
        in_specs=[rows(D_MODEL), rows(ATT_WIDTH), rows(ATT_WIDTH), rows(GATE_COLS), whole((1, GATE_COLS)),
                  whole((N_DEV, ATT_WIDTH, PROJ_SHARD)), whole((N_DEV, ATT_WIDTH, PROJ_SHARD)),
                  whole((D_MODEL, D_MODEL))],
        out_specs=[rows(GATE_COLS), rows(ATT_WIDTH), rows(ATT_WIDTH), rows(D_MODEL), rows(D_MODEL),
                   rows(D_MODEL), whole((1, GATE_COLS))],
        out_shape=[jax.ShapeDtypeStruct((s, GATE_COLS), BF16),
                   jax.ShapeDtypeStruct((s, ATT_WIDTH), BF16), jax.ShapeDtypeStruct((s, ATT_WIDTH), BF16),
                   jax.ShapeDtypeStruct((s, D_MODEL), BF16), jax.ShapeDtypeStruct((s, D_MODEL), BF16),
                   jax.ShapeDtypeStruct((s, D_MODEL), BF16), jax.ShapeDtypeStruct((1, GATE_COLS), F32)],
        scratch_shapes=[pltpu.VMEM((ATT_WIDTH, D_MODEL), BF16), pltpu.VMEM((ATT_WIDTH, D_MODEL), BF16)],
        compiler_params=_params(("arbitrary",)),
    )


def _ln_stats(u):
    mu = jnp.mean(u, axis=1, keepdims=True)
    cen = u - mu
    var = jnp.mean(cen * cen, axis=1, keepdims=True)
    rstd = lax.rsqrt(var + LN_EPS)
    return cen * rstd, rstd


def _ln_bwd(dy, xhat, rstd, gain):
    dyg = dy * gain
    m1 = jnp.mean(dyg, axis=1, keepdims=True)
    m2 = jnp.mean(dyg * xhat, axis=1, keepdims=True)
    return rstd * (dyg - m1 - xhat * m2)


def _load_mlp_weights(win_hbm, wout_hbm, win_s, wout_s, sems):
    copies = [pltpu.make_async_copy(win_hbm.at[k], win_s.at[:, pl.ds(FF_SHARD * k, FF_SHARD)], sems.at[k])
              for k in range(N_DEV)]
    copies.append(pltpu.make_async_copy(wout_hbm, wout_s, sems.at[N_DEV]))
    for cp in copies:
        cp.start()
    return copies


MLP_WEIGHT_SCRATCH = [pltpu.VMEM((D_MODEL, D_FF), BF16), pltpu.VMEM((D_FF, D_MODEL), BF16),
                      pltpu.SemaphoreType.DMA((N_DEV + 1,))]


def mlp_fwd(u1, target, wg_in, w_out, ln1_g, ln1_b, ln2_g, ln2_b, tm=256):
    s = u1.shape[0]

    def body(u_ref, t_ref, win_hbm, wout_hbm, g1_ref, b1_ref, g2_ref, b2_ref,
             hm_ref, a_ref, x1b_ref, du2_ref, loss_ref, dg2_ref, db2_ref, win_s, wout_s, sems):
        first = pl.program_id(0) == 0

        @pl.when(first)
        def _():
            copies = _load_mlp_weights(win_hbm, wout_hbm, win_s, wout_s, sems)
            loss_ref[...] = jnp.zeros_like(loss_ref)
            dg2_ref[...] = jnp.zeros_like(dg2_ref)
            db2_ref[...] = jnp.zeros_like(db2_ref)
            for cp in copies:
                cp.wait()

        xhat, _ = _ln_stats(u_ref[...])
        x1 = xhat * g1_ref[...] + b1_ref[...]
        x1b = x1.astype(BF16)
        x1b_ref[...] = x1b
        for k in range(N_DEV):
            cols = slice(FF_SHARD * k, FF_SHARD * (k + 1))
            hm = _dot(x1b, win_s[:, cols])
            hm_ref[:, cols] = hm
            r = jnp.maximum(hm, 0.0)
            a_ref[:, cols] = (r * r).astype(BF16)
        xhat, rstd = _ln_stats(ALPHA * x1 + _dot(a_ref[...], wout_s[...]))
        diff = xhat * g2_ref[...] + b2_ref[...] - t_ref[...]
        per_token = jnp.mean(diff * diff, axis=1, keepdims=True)
        loss_ref[...] += 0.5 * jnp.sum(per_token, axis=0, keepdims=True)
        dy = diff * (1.0 / D_MODEL)
        dg2_ref[...] += jnp.sum(dy * xhat, axis=0, keepdims=True)
        db2_ref[...] += jnp.sum(dy, axis=0, keepdims=True)
        du2_ref[...] = _ln_bwd(dy, xhat, rstd, g2_ref[...])

    rows = pl.BlockSpec((tm, D_MODEL), lambda i: (i, 0))
    vec = pl.BlockSpec((1, D_MODEL), lambda i: (0, 0))
    ff = pl.BlockSpec((tm, D_FF), lambda i: (i, 0))
    return pl.pallas_call(
        body, name="mlp_fwd", grid=(s // tm,),
        in_specs=[rows, rows, HBM_SPEC, HBM_SPEC, vec, vec, vec, vec],
        out_specs=[ff, ff, rows, rows, pl.BlockSpec((1, 1), lambda i: (0, 0)), vec, vec],
        out_shape=[jax.ShapeDtypeStruct((s, D_FF), F32), jax.ShapeDtypeStruct((s, D_FF), BF16),
                   jax.ShapeDtypeStruct((s, D_MODEL), BF16), jax.ShapeDtypeStruct((s, D_MODEL), F32),
                   jax.ShapeDtypeStruct((1, 1), F32),
                   jax.ShapeDtypeStruct((1, D_MODEL), F32), jax.ShapeDtypeStruct((1, D_MODEL), F32)],
        scratch_shapes=MLP_WEIGHT_SCRATCH,
        compiler_params=_params(("arbitrary",)),
    )(u1, target, wg_in, w_out, _row(ln1_g), _row(ln1_b), _row(ln2_g), _row(ln2_b))


def mlp_bwd(du2, hm, u1, wg_in, w_out, ln1_g, tm=256):
    s = du2.shape[0]

    def body(du2_ref, hm_ref, u_ref, win_hbm, wout_hbm, g1_ref,
             dhm_ref, du1_ref, dg1_ref, db1_ref, win_s, wout_s, sems):
        first = pl.program_id(0) == 0

        @pl.when(first)
        def _():
            copies = _load_mlp_weights(win_hbm, wout_hbm, win_s, wout_s, sems)
            dg1_ref[...] = jnp.zeros_like(dg1_ref)
            db1_ref[...] = jnp.zeros_like(db1_ref)
            for cp in copies:
                cp.wait()

        du2 = du2_ref[...]
        du2b = du2.astype(BF16)
        for k in range(N_DEV):
            cols = slice(FF_SHARD * k, FF_SHARD * (k + 1))
            da = _dot_nt(du2b, wout_s[cols, :])
            dhm_ref[:, cols] = (2.0 * jnp.maximum(hm_ref[:, cols], 0.0) * da).astype(BF16)
        dx1 = ALPHA * du2 + _dot_nt(dhm_ref[...], win_s[...])
        xhat, rstd = _ln_stats(u_ref[...])
        dg1_ref[...] += jnp.sum(dx1 * xhat, axis=0, keepdims=True)
        db1_ref[...] += jnp.sum(dx1, axis=0, keepdims=True)
        du1_ref[...] = _ln_bwd(dx1, xhat, rstd, g1_ref[...])

    rows = pl.BlockSpec((tm, D_MODEL), lambda i: (i, 0))
    vec = pl.BlockSpec((1, D_MODEL), lambda i: (0, 0))
    ff = pl.BlockSpec((tm, D_FF), lambda i: (i, 0))
    return pl.pallas_call(
        body, name="mlp_bwd", grid=(s // tm,),
        in_specs=[rows, ff, rows, HBM_SPEC, HBM_SPEC, vec],
        out_specs=[ff, rows, vec, vec],
        out_shape=[jax.ShapeDtypeStruct((s, D_FF), BF16), jax.ShapeDtypeStruct((s, D_MODEL), F32),
                   jax.ShapeDtypeStruct((1, D_MODEL), F32), jax.ShapeDtypeStruct((1, D_MODEL), F32)],
        scratch_shapes=MLP_WEIGHT_SCRATCH,
        compiler_params=_params(("arbitrary",)),
    )(du2, hm, u1, wg_in, w_out, _row(ln1_g))


def dh_slab(pieces, tm=512):
    s = pieces[0].shape[0]
    widths = [p.shape[1] for p in pieces]
    assert sum(widths) == IN_COLS
    n = len(pieces)

    def body(*refs):
        dh_ref = refs[n]
        off = 0
        for ref, w in zip(refs[:n], widths):
            dh_ref[:, off:off + w] = ref[...].astype(BF16)
            off += w

    rows = lambda w: pl.BlockSpec((tm, w), lambda i: (i, 0))
    return pl.pallas_call(
        body, name="dh_slab", grid=(s // tm,),
        in_specs=[rows(w) for w in widths], out_specs=rows(IN_COLS),
        out_shape=jax.ShapeDtypeStruct((s, IN_COLS), BF16),
        compiler_params=_params(("arbitrary",)),
    )(*pieces)


def in_bwd(du1, dh, wg_in, tm=512, guest=None):
    s = du1.shape[0]

    def body(du_ref, dh_ref, w_ref, gx_ref):
        acc = ALPHA * du_ref[...]
        for j in range(N_DEV):
            acc = acc + _dot_nt(dh_ref[:, IN_SHARD * j:IN_SHARD * (j + 1)], w_ref[j])
        gx_ref[...] = acc

    rows = lambda w: pl.BlockSpec((tm, w), lambda i: (i, 0))
    return _pallas(
        body, guest, (du1, dh, wg_in), name="in_bwd", grid=(s // tm,),
        in_specs=[rows(D_MODEL), rows(IN_COLS), pl.BlockSpec((N_DEV, D_MODEL, IN_SHARD), lambda i: (0, 0, 0))],
        out_specs=[rows(D_MODEL)],
        out_shape=[jax.ShapeDtypeStruct((s, D_MODEL), F32)],
        compiler_params=_params(("arbitrary",)),
    )


def wgrad(name, a, b, tm, tn, shard_cols=None, tk=2048):
    kk, m = a.shape
    n = b.shape[1]
    while kk % tk:
        tk //= 2
    nk = kk // tk
    w = shard_cols

    def body(a_ref, b_ref, out_ref, acc_s):
        k = pl.program_id(2)

        @pl.when(k == 0)
        def _():
            acc_s[...] = jnp.zeros_like(acc_s)

        acc_s[...] += _dot_tn(a_ref[...].astype(BF16), b_ref[...].astype(BF16))

        @pl.when(k == nk - 1)
        def _():
            if w is None:
                out_ref[...] = acc_s[...]
            else:
                for j in range(tn // w):
                    out_ref[j] = acc_s[:, w * j:w * (j + 1)]

    if w is None:
        out_spec = pl.BlockSpec((tm, tn), lambda i, j, k: (i, j))
        out_shape = jax.ShapeDtypeStruct((m, n), F32)
    else:
        out_spec = pl.BlockSpec((tn // w, tm, w), lambda i, j, k: (j, i, 0))
        out_shape = jax.ShapeDtypeStruct((n // w, m, w), F32)
    return pl.pallas_call(
        body, name=name, grid=(m // tm, n // tn, nk),
        in_specs=[pl.BlockSpec((tk, tm), lambda i, j, k: (k, i)),
                  pl.BlockSpec((tk, tn), lambda i, j, k: (k, j))],
        out_specs=out_spec, out_shape=out_shape,
        scratch_shapes=[pltpu.VMEM((tm, tn), F32)],
        compiler_params=_params(("arbitrary", "arbitrary", "arbitrary")),
    )(a, b)


def adamw(name, w, m, v, parts, tm=None, guest=None):
    rows, cols = w.shape
    tm = rows if tm is None else min(tm, rows)
    n = len(parts)
    c1 = 1.0 - ADAM_B1 ** ADAM_STEP
    c2 = 1.0 - ADAM_B2 ** ADAM_STEP

    def body(*refs):
        w_ref, m_ref, v_ref = refs[:3]
        part_refs = refs[3:3 + n]
        g_ref, d_ref, nm_ref, nv_ref = refs[3 + n:]
        g = None
        for ref, (_, index) in zip(part_refs, parts):
            term = (ref[...] if index is None else ref[0]).astype(F32)
            g = term if g is None else g + term
        new_m = ADAM_B1 * m_ref[...] + (1.0 - ADAM_B1) * g
        new_v = ADAM_B2 * v_ref[...] + (1.0 - ADAM_B2) * (g * g)
        m_hat = new_m / c1
        v_hat = new_v / c2
        g_ref[...] = g
        d_ref[...] = -ADAM_LR * (m_hat / (jnp.sqrt(v_hat) + ADAM_EPS) + ADAM_WD * w_ref[...])
        nm_ref[...] = new_m
        nv_ref[...] = new_v

    spec = pl.BlockSpec((tm, cols), lambda i: (i, 0))
    shape = jax.ShapeDtypeStruct((rows, cols), F32)

    def part_spec(index):
        if index is None:
            return spec
        return pl.BlockSpec((1, tm, cols), lambda i: (index, i, 0))

    return _pallas(
        body, guest, (w, m, v, *[array for array, _ in parts]), name=name, grid=(rows // tm,),
        in_specs=[spec] * 3 + [part_spec(index) for _, index in parts],
        out_specs=[spec] * 4, out_shape=[shape] * 4,
        compiler_params=_params(("arbitrary",)),
    )


def _place():
    x, y, c = lax.axis_index("x"), lax.axis_index("y"), lax.axis_index("c")
    other_chips = [(1 - x, y), (x, 1 - y), (1 - x, 1 - y)]
    return x, y, c, other_chips


def place_shards(arrays):
    n = len(arrays)

    def body(*refs):
        ins, outs, stage, sems = refs[:n], refs[n:2 * n], refs[2 * n:3 * n], refs[3 * n]
        x, y, c, _ = _place()
        copies = []
        for a in range(n):
            stage[a][...] = ins[a][...].astype(BF16)
            copies.append(pltpu.make_async_copy(stage[a], outs[a].at[4 * x + 2 * y + c], sems.at[a]))
            copies[-1].start()
        for cp in copies:
            cp.wait()

    return pl.pallas_call(
        body, name="place_shards",
        in_specs=[pl.BlockSpec(memory_space=pltpu.VMEM)] * n, out_specs=[HBM_SPEC] * n,
        out_shape=[jax.ShapeDtypeStruct((N_DEV,) + a.shape, BF16) for a in arrays],
        scratch_shapes=[pltpu.VMEM(a.shape, BF16) for a in arrays] + [pltpu.SemaphoreType.DMA((n,))],
        compiler_params=_params(),
    )(*arrays)


def all_gather(name, arrays):
    n = len(arrays)

    def body(*refs):
        ins, outs = refs[:n], refs[n:2 * n]
        send_sems, recv_sems, local_sems = refs[2 * n:]
        x, y, c, chips = _place()
        me = 4 * x + 2 * y + c
        sibling = (x, y, 1 - c)

        def copy(a, k, block, to, src=None):
            return pltpu.make_async_remote_copy(
                src_ref=outs[a].at[block] if src is None else src, dst_ref=outs[a].at[block],
                send_sem=send_sems.at[a, k], recv_sem=recv_sems.at[a, k],
                device_id=to, device_id_type=MESH)

        started = []
        local = [pltpu.make_async_copy(ins[a], outs[a].at[me], local_sems.at[a]) for a in range(n)]
        for a in range(n):
            local[a].start()
            sends = [copy(a, 0, me, sibling, src=ins[a])]
            sends += [copy(a, 1 + j, me, (cx, cy, c), src=ins[a]) for j, (cx, cy) in enumerate(chips)]
            for cp in sends:
                cp.start()
            started += sends
        for a in range(n):
            for j, (cx, cy) in enumerate(chips):
                block = 4 * cx + 2 * cy + c
                copy(a, 1 + j, block, sibling).wait_recv()
                passed = copy(a, 4 + j, block, sibling)
                passed.start()
                started.append(passed)
        for a in range(n):
            copy(a, 0, 4 * x + 2 * y + (1 - c), sibling).wait_recv()
            for j, (cx, cy) in enumerate(chips):
                copy(a, 4 + j, 4 * cx + 2 * cy + (1 - c), sibling).wait_recv()
        for cp in started:
            cp.wait_send()
        for cp in local:
            cp.wait()

    return pl.pallas_call(
        body, name=name,
        in_specs=[pl.BlockSpec(memory_space=pltpu.VMEM)] * n, out_specs=[HBM_SPEC] * n,
        out_shape=[jax.ShapeDtypeStruct((N_DEV,) + a.shape, a.dtype) for a in arrays],
        scratch_shapes=[pltpu.SemaphoreType.DMA((n, 7)), pltpu.SemaphoreType.DMA((n, 7)),
                        pltpu.SemaphoreType.DMA((n,))],
    )(*arrays)


def _gather_first_plan(n):
    def plan(zones):
        x, y, c, chips = _place()
        me = 4 * x + 2 * y + c
        copies = []
        for zone in zones[:n]:
            copies.append((zone.at[me], zone.at[me], (x, y, 1 - c), zone.at[4 * x + 2 * y + (1 - c)]))
            copies += [(zone.at[me], zone.at[me], (cx, cy, c), zone.at[4 * cx + 2 * cy + c]) for cx, cy in chips]
        return copies
    return plan


def _gather_second_plan(n):
    def plan(zones):
        x, y, c, chips = _place()
        return [(zone.at[4 * cx + 2 * cy + c], zone.at[4 * cx + 2 * cy + c], (x, y, 1 - c),
                 zone.at[4 * cx + 2 * cy + (1 - c)]) for zone in zones[:n] for cx, cy in chips]
    return plan


def _both_plans(first, n_first, second):
    return lambda bufs: first(bufs[:n_first]) + second(bufs[n_first:])


def _sibling_plan(n):
    def plan(bufs):
        x, y, c, _ = _place()
        return [(bufs[a].at[:, 1 - c], bufs[n + a], (x, y, 1 - c), bufs[n + a]) for a in range(n)]
    return plan


def _chips_plan(n):
    def plan(bufs):
        _, _, c, chips = _place()
        return [(bufs[a].at[2 * cx + cy], bufs[n + a].at[j], (cx, cy, c), bufs[n + a].at[j])
                for a in range(n) for j, (cx, cy) in enumerate(chips)]
    return plan


def chip_sum(name, grad, landed, place, tr=256):
    _, _, rows, cols = grad.shape
    tr = min(tr, rows)

    def body(place_ref, g_ref, l_ref, own_ref, out_ref):
        total = g_ref[0, 0] + l_ref[0]
        out_ref[0] = total.astype(BF16)

        @pl.when(pl.program_id(1) == place_ref[1])
        def _():
            own_ref[...] = total

    return pl.pallas_call(
        body, name=name,
        grid_spec=pltpu.PrefetchScalarGridSpec(
            num_scalar_prefetch=1, grid=(rows // tr, 4),
            in_specs=[pl.BlockSpec((1, 1, tr, cols), lambda i, ch, pos: (ch, pos[0], i, 0)),
                      pl.BlockSpec((1, tr, cols), lambda i, ch, pos: (ch, i, 0))],
            out_specs=[pl.BlockSpec((tr, cols), lambda i, ch, pos: (i, 0)),
                       pl.BlockSpec((1, tr, cols), lambda i, ch, pos: (ch, i, 0))]),
        out_shape=[jax.ShapeDtypeStruct((rows, cols), F32), jax.ShapeDtypeStruct((4, rows, cols), BF16)],
        compiler_params=_params(("arbitrary", "arbitrary")),
    )(place, grad, landed)


BIG =("w_in", "w_sb_proj", "w_ca_proj", "w_out", "w_mlp_in", "w_mlp_out")
SMALL = ("b_gate", "rel_bias", "ln1_g", "ln1_b", "ln2_g", "ln2_b")
NAMES = ("w_in", "b_gate", "w_sb_proj", "w_ca_proj", "rel_bias", "w_out", "ln1_g", "ln1_b",
         "w_mlp_in", "w_mlp_out", "ln2_g", "ln2_b")
REL_COLS = 2 * REL_CLIP + 1


def _pack_small(t, scalar=None):
    rel = t["rel_bias"]
    if rel.shape[1] != REL_PAD:
        rel = jnp.pad(rel, ((0, 0), (0, REL_PAD - rel.shape[1])))
    last = jnp.zeros((128,), F32) if scalar is None else jnp.pad(scalar.reshape(1), (0, 127))
    flat = [t["b_gate"].reshape(-1), rel.reshape(-1)] + [t[n].reshape(-1) for n in SMALL[2:]] + [last]
    return jnp.concatenate(flat).reshape(-1, 128)


def _unpack_small(p):
    flat = p.reshape(-1)
    out, off = {}, 0
    for n, size in (("b_gate", GATE_COLS), ("rel_bias", 8 * REL_PAD)) + tuple((n, D_MODEL) for n in SMALL[2:]):
        out[n] = flat[off:off + size]
        off += size
    out["rel_bias"] = out["rel_bias"].reshape(8, REL_PAD)[:, :REL_COLS]
    return out


def kernel(x, w_in, b_gate, w_sb_proj, w_ca_proj, rel_bias, w_out, ln1_g, ln1_b, w_mlp_in, w_mlp_out, ln2_g, ln2_b, loss_target, m_w_in, m_b_gate, m_w_sb_proj, m_w_ca_proj, m_rel_bias, m_w_out, m_ln1_g, m_ln1_b, m_w_mlp_in, m_w_mlp_out, m_ln2_g, m_ln2_b, v_w_in, v_b_gate, v_w_sb_proj, v_w_ca_proj, v_rel_bias, v_w_out, v_ln1_g, v_ln1_b, v_w_mlp_in, v_w_mlp_out, v_ln2_g, v_ln2_b):
    w = dict(w_in=w_in, b_gate=b_gate, w_sb_proj=w_sb_proj, w_ca_proj=w_ca_proj, rel_bias=rel_bias, w_out=w_out,
             ln1_g=ln1_g, ln1_b=ln1_b, w_mlp_in=w_mlp_in, w_mlp_out=w_mlp_out, ln2_g=ln2_g, ln2_b=ln2_b)
    m = dict(w_in=m_w_in, b_gate=m_b_gate, w_sb_proj=m_w_sb_proj, w_ca_proj=m_w_ca_proj, rel_bias=m_rel_bias,
             w_out=m_w_out, ln1_g=m_ln1_g, ln1_b=m_ln1_b, w_mlp_in=m_w_mlp_in, w_mlp_out=m_w_mlp_out,
             ln2_g=m_ln2_g, ln2_b=m_ln2_b)
    v = dict(w_in=v_w_in, b_gate=v_b_gate, w_sb_proj=v_w_sb_proj, w_ca_proj=v_w_ca_proj, rel_bias=v_rel_bias,
             w_out=v_w_out, ln1_g=v_ln1_g, ln1_b=v_ln1_b, w_mlp_in=v_w_mlp_in, w_mlp_out=v_w_mlp_out,
             ln2_g=v_ln2_g, ln2_b=v_ln2_b)

    x, target = x[0], loss_target[0]
    c = lax.axis_index("c")
    place = jnp.stack([c, 2 * lax.axis_index("x") + lax.axis_index("y")]).astype(jnp.int32)
    grads, deltas, new_m, new_v = {}, {}, {}, {}

    def by_owner(n, g):
        return g.reshape((4, 2) + w[n].shape)

    def finish(n, own, arrived, guest=None):
        out = adamw("adamw_" + n, w[n], m[n], v[n], [(own, None), (arrived, 0), (arrived, 1), (arrived, 2)],
                    tm=256, guest=guest)
        out, bufs = (out, None) if guest is None else out
        grads[n], deltas[n], new_m[n], new_v[n] = out
        return bufs

    def siblings(names, g):
        bufs = [by_owner(n, g[n]) for n in names] + [lax.empty((4,) + w[n].shape, F32) for n in names]
        return Exchange(tuple(bufs), _sibling_plan(len(names)), len(names))

    def chip_sums(names, bufs):
        k = len(names)
        own, sums = zip(*[chip_sum("chip_sum_" + n, bufs[a], bufs[k + a], place) for a, n in enumerate(names)])
        zones = [lax.empty((3,) + w[n].shape, BF16) for n in names]
        return own, Exchange(tuple(sums) + tuple(zones), _chips_plan(k), 3 * k)

    rest = BIG[1:]
    zones = place_shards([w[n] for n in rest])
    wg_in = all_gather("gather_w_in", [w_in.astype(BF16)])[0]
    qkv, lg, xb = in_proj(x, wg_in)
    bias_tiles = ca_bias_tiles(rel_bias)
    (o_sb,), zones = sb_fwd(qkv, Exchange(tuple(zones), _gather_first_plan(len(rest)), 4 * len(rest)))
    (o_ca,), zones = ca_fwd(qkv, bias_tiles, Exchange(tuple(zones), _gather_second_plan(len(rest)), 3 * len(rest)))
    wg = dict(zip(rest, zones))
    wg_sb, wg_ca, wg_mi = wg["w_sb_proj"], wg["w_ca_proj"], wg["w_mlp_in"]
    wg_mo = wg["w_mlp_out"].reshape(D_FF, D_MODEL)
    w_out_b = wg["w_out"].reshape(D_MODEL, D_MODEL)

    u1 = mix_fwd(x, o_sb, o_ca, lg, b_gate, wg_sb, wg_ca, w_out_b)
    hm, a, x1b, du2, loss, dg2, db2 = mlp_fwd(u1, target, wg_mi, wg_mo, ln1_g, ln1_b, ln2_g, ln2_b)
    dhm, du1, dg1, db1 = mlp_bwd(du2, hm, u1, wg_mi, wg_mo, ln1_g)
    g = {}
    g["w_mlp_in"] = wgrad("wgrad_mlp_in", x1b, dhm, tm=D_MODEL, tn=FF_SHARD, shard_cols=FF_SHARD)
    g["w_mlp_out"] = wgrad("wgrad_mlp_out", a, du2, tm=FF_SHARD, tn=D_MODEL)

    mlp = ("w_mlp_in", "w_mlp_out")
    (dlg, do_sb, do_ca, dy_sb, dy_ca, merged, dbg), bufs = mix_bwd(
        du1, o_sb, o_ca, lg, b_gate, wg_sb, wg_ca, w_out_b, guest=siblings(mlp, g))
    own_mlp, mlp_chips = chip_sums(mlp, bufs)
    g["w_out"] = wgrad("wgrad_out", merged, du1, tm=D_MODEL, tn=D_MODEL)
    g["w_sb_proj"] = wgrad("wgrad_sb_proj", o_sb, dy_sb, tm=ATT_WIDTH, tn=D_MODEL, shard_cols=PROJ_SHARD)
    g["w_ca_proj"] = wgrad("wgrad_ca_proj", o_ca, dy_ca, tm=ATT_WIDTH, tn=D_MODEL, shard_cols=PROJ_SHARD)

    att = ("w_out", "w_sb_proj", "w_ca_proj")
    att_siblings = siblings(att, g)
    both = Exchange(mlp_chips.bufs + att_siblings.bufs,
                    _both_plans(mlp_chips.plan, len(mlp_chips.bufs), att_siblings.plan),
                    mlp_chips.copies + att_siblings.copies)
    (dq_sb, dk_sb, dv_sb), bufs = sb_bwd(qkv, o_sb, do_sb, guest=both)
    arrived_mlp, bufs = bufs[len(mlp):len(mlp_chips.bufs)], bufs[len(mlp_chips.bufs):]
    own_att, att_chips = chip_sums(att, bufs)
    (dq_ca, dk_ca, dv_ca, db_tiles), bufs = ca_bwd(qkv, bias_tiles, do_ca, guest=att_chips)
    arrived_att = bufs[len(att):]
    g_rel = rel_bias_grad(db_tiles)

    dh = dh_slab([dq_sb, dk_sb, dv_sb, dq_ca, dk_ca, dv_ca, dlg])
    g["w_in"] = wgrad("wgrad_in", xb, dh, tm=D_MODEL, tn=IN_SHARD, shard_cols=IN_SHARD)
    last = ("w_in",)
    bufs = finish(mlp[0], own_mlp[0], arrived_mlp[0], guest=siblings(last, g))
    finish(mlp[1], own_mlp[1], arrived_mlp[1])
    for n, own, arrived in zip(att, own_att, arrived_att):
        finish(n, own, arrived)
    own_in, in_chips = chip_sums(last, bufs)
    (grad_x,), bufs = in_bwd(du1, dh, wg_in, guest=in_chips)
    finish(last[0], own_in[0], bufs[1])

    small = dict(b_gate=dbg, rel_bias=g_rel, ln1_g=dg1, ln1_b=db1, ln2_g=dg2, ln2_b=db2)
    parts = all_gather("gather_small_grads", [_pack_small(small, loss[0, 0])])[0]
    packed = adamw("adamw_small", _pack_small(w), _pack_small(m), _pack_small(v),
                   [(parts, d) for d in range(N_DEV)])
    loss = packed[0][-1, 0]
    for out, p in zip((grads, deltas, new_m, new_v), packed):
        for n, val in _unpack_small(p).items():
            out[n] = val.reshape(w[n].shape)

    return (loss, grad_x[None], *[grads[n] for n in NAMES], *[deltas[n] for n in NAMES],
            *[new_m[n] for n in NAMES], *[new_v[n] for n in NAMES])
```

```python
import functools
from typing import NamedTuple

import jax
import jax.numpy as jnp
from jax import lax
from jax.experimental import pallas as pl
from jax.experimental.pallas import tpu as pltpu

F32 = jnp.float32
BF16 = jnp.bfloat16
MESH = pl.DeviceIdType.MESH

N_DEV = 8
D_MODEL = 1024
HEAD_DIM = 64
ATT_WIDTH = 512
N_PAIRS = ATT_WIDTH // 128
QKV_COLS = 6 * ATT_WIDTH
GATE_COLS = 2 * D_MODEL
IN_COLS = QKV_COLS + GATE_COLS
IN_SHARD = IN_COLS // N_DEV
D_FF = 4 * D_MODEL
FF_SHARD = D_FF // N_DEV
PROJ_SHARD = D_MODEL // N_DEV
ATT_BLOCK = 128
CA_TILES = 5
CA_ROWS = 1024
CHUNK = 64
CA_PREV_CHUNKS = 8
REL_CLIP = 256
REL_PAD = 640
ALPHA = 2.0 ** 0.25
LN_EPS = 1e-5
QK_SCALE = HEAD_DIM ** -0.5
NEG_BIG = -1e30
VMEM_LIMIT = 56 * 1024 * 1024

ADAM_LR = 0.001
ADAM_B1 = 0.9
ADAM_B2 = 0.999
ADAM_EPS = 1e-08
ADAM_WD = 0.01
ADAM_STEP = 10

_NT = (((1,), (1,)), ((), ()))
_TN = (((0,), (0,)), ((), ()))


def _dot(a, b):
    return jnp.dot(a, b, preferred_element_type=F32)


def _dot_nt(a, b):
    return lax.dot_general(a, b, _NT, preferred_element_type=F32)


def _dot_tn(a, b):
    return lax.dot_general(a, b, _TN, preferred_element_type=F32)


def _params(semantics=None):
    return pltpu.CompilerParams(dimension_semantics=semantics, vmem_limit_bytes=VMEM_LIMIT)


def _row(v):
    return v.reshape(1, -1)


HBM_SPEC = pl.BlockSpec(memory_space=pl.ANY)


class Exchange(NamedTuple):
    bufs: tuple
    plan: object
    copies: int


def _pallas(body, guest, args, *, grid, in_specs, out_specs, out_shape, scratch_shapes=(), **kw):
    out_specs, out_shape, scratch_shapes = list(out_specs), list(out_shape), list(scratch_shapes)
    if guest is None:
        return pl.pallas_call(body, grid=grid, in_specs=in_specs, out_specs=out_specs, out_shape=out_shape,
                              scratch_shapes=scratch_shapes, **kw)(*args)
    nb, n_own = len(guest.bufs), len(in_specs) + len(out_specs)

    def hosting(*refs):
        zones = refs[nb + n_own:2 * nb + n_own]
        send_sems, recv_sems = refs[-2], refs[-1]
        steps = [pl.program_id(axis) for axis in range(len(grid))]
        first = functools.reduce(jnp.logical_and, [step == 0 for step in steps])
        last = functools.reduce(jnp.logical_and, [step == n - 1 for step, n in zip(steps, grid)])

        def copy(k, src, dst, peer):
            return pltpu.make_async_remote_copy(src_ref=src, dst_ref=dst, send_sem=send_sems.at[k],
                                                recv_sem=recv_sems.at[k], device_id=peer, device_id_type=MESH)

        @pl.when(first)
        def _():
            for k, (src, dst, peer, _) in enumerate(guest.plan(zones)):
                copy(k, src, dst, peer).start()

        body(*refs[nb:nb + n_own], *refs[2 * nb + n_own:-2])

        @pl.when(last)
        def _():
            for k, (src, dst, peer, landed) in enumerate(guest.plan(zones)):
                copy(k, src, dst, peer).wait_send()
                copy(k, src, landed, peer).wait_recv()

    sems = pltpu.SemaphoreType.DMA((guest.copies,))
    results = pl.pallas_call(
        hosting, grid=grid, in_specs=[HBM_SPEC] * nb + list(in_specs), out_specs=out_specs + [HBM_SPEC] * nb,
        out_shape=out_shape + [jax.ShapeDtypeStruct(b.shape, b.dtype) for b in guest.bufs],
        input_output_aliases={k: len(out_specs) + k for k in range(nb)},
        scratch_shapes=scratch_shapes + [sems, sems], **kw)(*guest.bufs, *args)
    return results[:len(out_specs)], results[len(out_specs):]


def in_proj(x, wg_in, tm=512):
    s = x.shape[0]

    def body(x_ref, w_ref, qkv_ref, lg_ref, xb_ref):
        xb = x_ref[...].astype(BF16)
        xb_ref[...] = xb
        for j in range(N_DEV):
            acc = _dot(xb, w_ref[j])
            lo, hi = IN_SHARD * j, IN_SHARD * (j + 1)
            if hi <= QKV_COLS:
                qkv_ref[:, lo:hi] = acc.astype(BF16)
            elif lo >= QKV_COLS:
                lg_ref[:, lo - QKV_COLS:hi - QKV_COLS] = acc
            else:
                qkv_ref[:, lo:QKV_COLS] = acc[:, :QKV_COLS - lo].astype(BF16)
                lg_ref[:, 0:hi - QKV_COLS] = acc[:, QKV_COLS - lo:]

    return pl.pallas_call(
        body, name="in_proj", grid=(s // tm,),
        in_specs=[pl.BlockSpec((tm, D_MODEL), lambda i: (i, 0)),
                  pl.BlockSpec((N_DEV, D_MODEL, IN_SHARD), lambda i: (0, 0, 0))],
        out_specs=[pl.BlockSpec((tm, QKV_COLS), lambda i: (i, 0)),
                   pl.BlockSpec((tm, GATE_COLS), lambda i: (i, 0)),
                   pl.BlockSpec((tm, D_MODEL), lambda i: (i, 0))],
        out_shape=[jax.ShapeDtypeStruct((s, QKV_COLS), BF16),
                   jax.ShapeDtypeStruct((s, GATE_COLS), F32),
                   jax.ShapeDtypeStruct((s, D_MODEL), BF16)],
        compiler_params=_params(("arbitrary",)),
    )(x, wg_in)


def _head_masks():
    lane = lax.broadcasted_iota(jnp.int32, (1, 128), 1)
    first = lane < HEAD_DIM
    return first, jnp.logical_not(first)


SB_CHUNK = 512
SB_SUB = SB_CHUNK // ATT_BLOCK


def _tri_pair():
    row = lax.broadcasted_iota(jnp.int32, (4 * ATT_BLOCK, 2 * ATT_BLOCK), 0) & (2 * ATT_BLOCK - 1)
    col = lax.broadcasted_iota(jnp.int32, (4 * ATT_BLOCK, 2 * ATT_BLOCK), 1)
    same = (row >= ATT_BLOCK) == (col >= ATT_BLOCK)
    return jnp.logical_and(same, row >= col).astype(BF16)


def _pair_bf16(pieces):
    return jnp.concatenate(
        [jnp.concatenate([p.astype(BF16) for p in row], axis=1) for row in pieces], axis=0)


def _suffix_sums(pieces, tri_pair):
    hi = [[p.astype(BF16) for p in row] for row in pieces]
    lo = [[p - h.astype(F32) for p, h in zip(row, hrow)] for row, hrow in zip(pieces, hi)]
    return _dot(jnp.concatenate([_pair_bf16(hi), _pair_bf16(lo)], axis=1), tri_pair)


def _first_row_piece(diag, j):
    return j if diag else 0


def _piece(v, r, h):
    return v[ATT_BLOCK * r:ATT_BLOCK * (r + 1), ATT_BLOCK * h:ATT_BLOCK * (h + 1)]


def _all_row_pieces(slabs):
    return [[slab[ATT_BLOCK * r:ATT_BLOCK * (r + 1)] for r in range(SB_SUB)] for slab in slabs]


def _add_rows(total, part, r0):
    if r0 == 0:
        return total + part
    return jnp.concatenate([total[:ATT_BLOCK * r0], total[ATT_BLOCK * r0:] + part], axis=0)


def _row_totals(csum, r0, h):
    return jnp.concatenate(
        [jnp.broadcast_to(_piece(csum, at, h)[:, 0:1], (ATT_BLOCK, ATT_BLOCK)) for at in range(SB_SUB - r0)], axis=0)


SIGN_BIT = 0x80000000


def _drop(z):
    minus_abs = lax.bitcast_convert_type(lax.bitcast_convert_type(z, jnp.uint32) | jnp.uint32(SIGN_BIT), F32)
    return jnp.maximum(z, 0.0) + jnp.log(1.0 + jnp.exp(minus_abs))


def _sb_weights(z, runs, r0, masked, strict, tri_pair):
    drops = []
    for r in range(r0, SB_SUB):
        row = [_drop(_piece(z, r - r0, h)) for h in range(2)]
        drops.append([jnp.where(strict, drop, 0.0) for drop in row] if r == masked else row)
    csum = _suffix_sums(drops, tri_pair)
    weights = []
    for r in range(r0, SB_SUB):
        row = [jnp.exp(_piece(z, r - r0, h) - (_piece(csum, r - r0, h) + runs[h][r])) for h in range(2)]
        weights.append([jnp.where(strict, a, 0.0) for a in row] if r == masked else row)
    return drops, weights, csum


EXP_UNDERFLOW = 104.0


SB_STEP = 2


def _sweep_left(span, i, dropped_ref):
    def more(at):
        return jnp.logical_and(at >= 0, jnp.min(dropped_ref[...]) < EXP_UNDERFLOW)

    def step(at):
        span(at * (SB_STEP * ATT_BLOCK), SB_STEP, False)
        return at - 1

    lax.while_loop(more, step, i * (SB_SUB // SB_STEP) - 1)


def _by_head(blk, masks):
    zero = jnp.zeros_like(blk)
    return jnp.concatenate([jnp.where(m, blk, zero) for m in masks], axis=0)


def sb_fwd(qkv, guest=None):
    s = qkv.shape[0]
    nq = s // SB_CHUNK

    def body(q_ref, k_ref, v_ref, o_ref, run_s):
        i = pl.program_id(1)
        masks = _head_masks()
        q = q_ref[...] * QK_SCALE
        row = lax.broadcasted_iota(jnp.int32, (ATT_BLOCK, ATT_BLOCK), 0)
        col = lax.broadcasted_iota(jnp.int32, (ATT_BLOCK, ATT_BLOCK), 1)
        strict = col < row
        tri_pair = _tri_pair()
        run_s[...] = jnp.zeros_like(run_s)
        o_ref[...] = jnp.zeros_like(o_ref)

        def span(start, pieces, diag):
            runs = [run_s[0], run_s[1]]
            acc = jnp.zeros((SB_CHUNK, ATT_BLOCK), F32)
            for j in reversed(range(pieces)):
                r0 = _first_row_piece(diag, j)
                keys = pl.ds(pl.multiple_of(start + ATT_BLOCK * j, ATT_BLOCK), ATT_BLOCK)
                z = _dot_nt(q[ATT_BLOCK * r0:], _by_head(k_ref[keys, :], masks))
                _, weights, csum = _sb_weights(z, _all_row_pieces(runs), r0, j if diag else None, strict, tri_pair)
                part = _dot(_pair_bf16(weights), _by_head(v_ref[keys, :], masks))
                acc = _add_rows(acc, part, r0)
                runs = [_add_rows(runs[h], _row_totals(csum, r0, h), r0) for h in range(2)]
            run_s[0], run_s[1] = runs
            o_ref[...] += acc

        span(i * SB_CHUNK, SB_SUB, True)
        _sweep_left(span, i, run_s)

    return _pallas(
        body, guest, (qkv, qkv, qkv), name="sb_fwd", grid=(N_PAIRS, nq),
        in_specs=[pl.BlockSpec((SB_CHUNK, 128), lambda p, i: (i, p)),
                  pl.BlockSpec((s, 128), lambda p, i: (0, N_PAIRS + p)),
                  pl.BlockSpec((s, 128), lambda p, i: (0, 2 * N_PAIRS + p))],
        out_specs=[pl.BlockSpec((SB_CHUNK, 128), lambda p, i: (i, p))],
        out_shape=[jax.ShapeDtypeStruct((s, ATT_WIDTH), F32)],
        scratch_shapes=[pltpu.VMEM((2, SB_CHUNK, 128), F32)],
        compiler_params=_params(("arbitrary", "arbitrary")),
    )


def sb_bwd(qkv, o_sb, do_sb, guest=None):
    s = qkv.shape[0]
    nq = s // SB_CHUNK

    def body(q_ref, k_ref, v_ref, o_ref, do_ref, dq_ref, dk_ref, dv_ref, run_l_s, run_e_s, etot_s, dq_s):
        i = pl.program_id(1)

        @pl.when(i == 0)
        def _():
            dk_ref[...] = jnp.zeros_like(dk_ref)
            dv_ref[...] = jnp.zeros_like(dv_ref)

        masks = _head_masks()
        q = q_ref[...] * QK_SCALE
        do = do_ref[...]
        prod = do.astype(F32) * o_ref[...]
        for h in range(2):
            total = jnp.sum(jnp.where(masks[h], prod, 0.0), axis=1, keepdims=True)
            etot_s[h] = jnp.broadcast_to(total, (SB_CHUNK, ATT_BLOCK))
        row = lax.broadcasted_iota(jnp.int32, (ATT_BLOCK, ATT_BLOCK), 0)
        col = lax.broadcasted_iota(jnp.int32, (ATT_BLOCK, ATT_BLOCK), 1)
        strict = col < row
        tri_pair = _tri_pair()
        run_l_s[...] = jnp.zeros_like(run_l_s)
        run_e_s[...] = jnp.zeros_like(run_e_s)
        dq_s[...] = jnp.zeros_like(dq_s)

        def span(start, pieces, diag):
            slabs_l = [run_l_s[0], run_l_s[1]]
            slabs_e = [run_e_s[0], run_e_s[1]]
            e_tot = _all_row_pieces([etot_s[0], etot_s[1]])
            dq = jnp.zeros((SB_CHUNK, ATT_BLOCK), F32)
            for j in reversed(range(pieces)):
                runs_l, runs_e = _all_row_pieces(slabs_l), _all_row_pieces(slabs_e)
                r0 = _first_row_piece(diag, j)
                masked = j if diag else None
                keys = pl.ds(pl.multiple_of(start + ATT_BLOCK * j, ATT_BLOCK), ATT_BLOCK)
                kk = _by_head(k_ref[keys, :], masks)
                vv = _by_head(v_ref[keys, :], masks)
                q_rows, do_rows = q[ATT_BLOCK * r0:], do[ATT_BLOCK * r0:]
                z = _dot_nt(q_rows, kk)
                da = _dot_nt(do_rows, vv)
                drops, weights, csum = _sb_weights(z, runs_l, r0, masked, strict, tri_pair)
                ab = [[a.astype(BF16) for a in wrow] for wrow in weights]
                es = [[a.astype(F32) * _piece(da, at, h) for h, a in enumerate(arow)] for at, arow in enumerate(ab)]
                esum = _suffix_sums(es, tri_pair)
                dzs = []
                for at, r in enumerate(range(r0, SB_SUB)):
                    dz_row = []
                    for h in range(2):
                        sig = jnp.exp(_piece(z, at, h) - drops[at][h])
                        after = _piece(esum, at, h) + runs_e[h][r] - es[at][h]
                        dz = es[at][h] - sig * (e_tot[h][r] - after)
                        dz_row.append(jnp.where(strict, dz, 0.0) if r == masked else dz)
                    dzs.append(dz_row)
                dz = _pair_bf16(dzs)
                dq = _add_rows(dq, _dot(dz, kk), r0)
                dk = _dot_tn(dz, q_rows)
                dv = _dot_tn(_pair_bf16(ab), do_rows)
                dk_ref[keys, :] += jnp.where(masks[0], dk[:ATT_BLOCK], dk[ATT_BLOCK:])
                dv_ref[keys, :] += jnp.where(masks[0], dv[:ATT_BLOCK], dv[ATT_BLOCK:])
                slabs_l = [_add_rows(slabs_l[h], _row_totals(csum, r0, h), r0) for h in range(2)]
                slabs_e = [_add_rows(slabs_e[h], _row_totals(esum, r0, h), r0) for h in range(2)]
            run_l_s[0], run_l_s[1] = slabs_l
            run_e_s[0], run_e_s[1] = slabs_e
            dq_s[...] += dq

        span(i * SB_CHUNK, SB_SUB, True)
        _sweep_left(span, i, run_l_s)
        dq_ref[...] = (dq_s[...] * QK_SCALE).astype(BF16)

    tile = lambda p, i: (i, p)
    state = pltpu.VMEM((2, SB_CHUNK, 128), F32)
    return _pallas(
        body, guest, (qkv, qkv, qkv, o_sb, do_sb), name="sb_bwd", grid=(N_PAIRS, nq),
        in_specs=[pl.BlockSpec((SB_CHUNK, 128), tile),
                  pl.BlockSpec((s, 128), lambda p, i: (0, N_PAIRS + p)),
                  pl.BlockSpec((s, 128), lambda p, i: (0, 2 * N_PAIRS + p)),
                  pl.BlockSpec((SB_CHUNK, 128), tile),
                  pl.BlockSpec((SB_CHUNK, 128), tile)],
        out_specs=[pl.BlockSpec((SB_CHUNK, 128), tile),
                   pl.BlockSpec((s, 128), lambda p, i: (0, p)),
                   pl.BlockSpec((s, 128), lambda p, i: (0, p))],
        out_shape=[jax.ShapeDtypeStruct((s, ATT_WIDTH), BF16),
                   jax.ShapeDtypeStruct((s, ATT_WIDTH), F32),
                   jax.ShapeDtypeStruct((s, ATT_WIDTH), F32)],
        scratch_shapes=[state, state, state, pltpu.VMEM((SB_CHUNK, 128), F32)],
        compiler_params=_params(("arbitrary", "arbitrary")),
    )


def ca_bias_tiles(rel_bias):
    heads = rel_bias.shape[0]
    wide = 2 * ATT_BLOCK

    def body(rel_ref, out_ref):
        bins = lax.broadcasted_iota(jnp.int32, (REL_PAD, wide), 0)
        pos = lax.broadcasted_iota(jnp.int32, (REL_PAD, wide), 1)
        r = lax.broadcasted_iota(jnp.int32, (ATT_BLOCK, ATT_BLOCK), 0)
        c = lax.broadcasted_iota(jnp.int32, (ATT_BLOCK, ATT_BLOCK), 1)
        for j in range(CA_TILES):
            hot = (jnp.clip(ATT_BLOCK * (j + 1) - 1 - pos, -REL_CLIP, REL_CLIP) + REL_CLIP == bins).astype(F32)
            lines = jnp.dot(rel_ref[...], hot, preferred_element_type=F32, precision=lax.Precision.HIGHEST)
            back = 2 * j + (r >> 6) - (c >> 6)
            ok = jnp.logical_and(back >= 0, back <= CA_PREV_CHUNKS)
            for h in range(heads):
                tile = jnp.broadcast_to(lines[h:h + 1, :], (ATT_BLOCK, wide))
                tile = pltpu.roll(tile, wide - (ATT_BLOCK - 1), 1, stride=1, stride_axis=0)
                out_ref[h, j] = jnp.where(ok, tile[:, :ATT_BLOCK], NEG_BIG)

    padded = jnp.pad(rel_bias, ((0, 0), (0, REL_PAD - rel_bias.shape[1])))
    return pl.pallas_call(
        body, name="ca_bias_tiles",
        out_shape=jax.ShapeDtypeStruct((heads, CA_TILES, ATT_BLOCK, ATT_BLOCK), F32),
        compiler_params=_params(),
    )(padded)


CA_BLOCKS = CA_ROWS // ATT_BLOCK
CA_KEY_BLOCKS = CA_BLOCKS + CA_TILES - 1


def _ca_users(t):
    return [r for r in range(CA_BLOCKS) if 0 <= r + CA_TILES - 1 - t < CA_TILES]


def _ca_rows(v, t, r):
    at = _ca_users(t).index(r)
    return v[ATT_BLOCK * at:ATT_BLOCK * (at + 1)]


def _ca_window(step, k_ref, v_ref, masks):
    kks, vvs, inside, rows = [], [], [], []
    for t in range(CA_KEY_BLOCKS):
        block = step * CA_BLOCKS - (CA_TILES - 1) + t
        keys = pl.ds(pl.multiple_of(jnp.maximum(block, 0) * ATT_BLOCK, ATT_BLOCK), ATT_BLOCK)
        kks.append(_by_head(k_ref[keys, :], masks))
        vvs.append(_by_head(v_ref[keys, :], masks))
        inside.append(block >= 0)
        rows.append(keys)
    return kks, vvs, inside, rows


def _ca_probs(q, kks, inside, bias_ref):
    scores = [[[None] * CA_TILES for _ in range(2)] for _ in range(CA_BLOCKS)]
    for t in range(CA_KEY_BLOCKS):
        users = _ca_users(t)
        z = _dot_nt(q[ATT_BLOCK * users[0]:ATT_BLOCK * (users[-1] + 1)], kks[t])
        for r in users:
            j = r + CA_TILES - 1 - t
            for h in range(2):
                zz = _ca_rows(z, t, r)[:, ATT_BLOCK * h:ATT_BLOCK * (h + 1)] + bias_ref[h, j]
                scores[r][h][j] = jnp.where(inside[t], zz, NEG_BIG)
    probs = [[None, None] for _ in range(CA_BLOCKS)]
    for r in range(CA_BLOCKS):
        for h in range(2):
            m = jnp.max(functools.reduce(jnp.maximum, scores[r][h]), axis=1, keepdims=True)
            ex = [jnp.exp(z - m) for z in scores[r][h]]
            inv = 1.0 / jnp.sum(functools.reduce(jnp.add, ex), axis=1, keepdims=True)
            probs[r][h] = [e * inv for e in ex]
    return probs


def _ca_stack(tiles, t):
    return jnp.concatenate(
        [jnp.concatenate([tiles[r][h][r + CA_TILES - 1 - t].astype(BF16) for h in range(2)], axis=1)
         for r in _ca_users(t)], axis=0)


def ca_fwd(qkv, bias_tiles, guest=None):
    s = qkv.shape[0]
    nq = s // CA_ROWS

    def body(q_ref, k_ref, v_ref, bias_ref, o_ref):
        masks = _head_masks()
        kks, vvs, inside, _ = _ca_window(pl.program_id(1), k_ref, v_ref, masks)
        probs = _ca_probs(q_ref[...] * QK_SCALE, kks, inside, bias_ref)
        out = [None] * CA_BLOCKS
        for t in range(CA_KEY_BLOCKS):
            part = _dot(_ca_stack(probs, t), vvs[t])
            for r in _ca_users(t):
                piece = _ca_rows(part, t, r)
                out[r] = piece if out[r] is None else out[r] + piece
        o_ref[...] = jnp.concatenate(out, axis=0)

    return _pallas(
        body, guest, (qkv, qkv, qkv, bias_tiles), name="ca_fwd", grid=(N_PAIRS, nq),
        in_specs=[pl.BlockSpec((CA_ROWS, 128), lambda p, i: (i, 3 * N_PAIRS + p)),
                  pl.BlockSpec((s, 128), lambda p, i: (0, 4 * N_PAIRS + p)),
                  pl.BlockSpec((s, 128), lambda p, i: (0, 5 * N_PAIRS + p)),
                  pl.BlockSpec((2, CA_TILES, ATT_BLOCK, ATT_BLOCK), lambda p, i: (p, 0, 0, 0))],
        out_specs=[pl.BlockSpec((CA_ROWS, 128), lambda p, i: (i, p))],
        out_shape=[jax.ShapeDtypeStruct((s, ATT_WIDTH), F32)],
        compiler_params=_params(("arbitrary", "arbitrary")),
    )


def ca_bwd(qkv, bias_tiles, do_ca, guest=None):
    s = qkv.shape[0]
    nq = s // CA_ROWS

    def body(q_ref, k_ref, v_ref, bias_ref, do_ref, dq_ref, dk_ref, dv_ref, db_ref):
        big = pl.program_id(1)

        @pl.when(big == 0)
        def _():
            dk_ref[...] = jnp.zeros_like(dk_ref)
            dv_ref[...] = jnp.zeros_like(dv_ref)
            db_ref[...] = jnp.zeros_like(db_ref)

        masks = _head_masks()
        kks, vvs, inside, key_rows = _ca_window(big, k_ref, v_ref, masks)
        q = q_ref[...] * QK_SCALE
        do = do_ref[...]
        probs = _ca_probs(q, kks, inside, bias_ref)
        dps = [[[None] * CA_TILES for _ in range(2)] for _ in range(CA_BLOCKS)]
        for t in range(CA_KEY_BLOCKS):
            users = _ca_users(t)
            dp = _dot_nt(do[ATT_BLOCK * users[0]:ATT_BLOCK * (users[-1] + 1)], vvs[t])
            for r in users:
                for h in range(2):
                    dps[r][h][r + CA_TILES - 1 - t] = _ca_rows(dp, t, r)[:, ATT_BLOCK * h:ATT_BLOCK * (h + 1)]
        dss = [[None, None] for _ in range(CA_BLOCKS)]
        for r in range(CA_BLOCKS):
            for h in range(2):
                delta = jnp.sum(functools.reduce(jnp.add, [p * dp for p, dp in zip(probs[r][h], dps[r][h])]),
                                axis=1, keepdims=True)
                dss[r][h] = [p * (dp - delta) for p, dp in zip(probs[r][h], dps[r][h])]
        for h in range(2):
            for j in range(CA_TILES):
                db_ref[h, j] += functools.reduce(jnp.add, [dss[r][h][j] for r in range(CA_BLOCKS)])
        dq = [None] * CA_BLOCKS
        for t in range(CA_KEY_BLOCKS):
            users = _ca_users(t)
            rows = slice(ATT_BLOCK * users[0], ATT_BLOCK * (users[-1] + 1))
            ds = _ca_stack(dss, t)
            part = _dot(ds, kks[t])
            for r in users:
                piece = _ca_rows(part, t, r)
                dq[r] = piece if dq[r] is None else dq[r] + piece
            dk = _dot_tn(ds, q[rows])
            dv = _dot_tn(_ca_stack(probs, t), do[rows])
            dk_ref[key_rows[t], :] += jnp.where(masks[0], dk[:ATT_BLOCK], dk[ATT_BLOCK:])
            dv_ref[key_rows[t], :] += jnp.where(masks[0], dv[:ATT_BLOCK], dv[ATT_BLOCK:])
        dq_ref[...] = (jnp.concatenate(dq, axis=0) * QK_SCALE).astype(BF16)

    return _pallas(
        body, guest, (qkv, qkv, qkv, bias_tiles, do_ca), name="ca_bwd", grid=(N_PAIRS, nq),
        in_specs=[pl.BlockSpec((CA_ROWS, 128), lambda p, i: (i, 3 * N_PAIRS + p)),
                  pl.BlockSpec((s, 128), lambda p, i: (0, 4 * N_PAIRS + p)),
                  pl.BlockSpec((s, 128), lambda p, i: (0, 5 * N_PAIRS + p)),
                  pl.BlockSpec((2, CA_TILES, ATT_BLOCK, ATT_BLOCK), lambda p, i: (p, 0, 0, 0)),
                  pl.BlockSpec((CA_ROWS, 128), lambda p, i: (i, p))],
        out_specs=[pl.BlockSpec((CA_ROWS, 128), lambda p, i: (i, p)),
                   pl.BlockSpec((s, 128), lambda p, i: (0, p)),
                   pl.BlockSpec((s, 128), lambda p, i: (0, p)),
                   pl.BlockSpec((2, CA_TILES, ATT_BLOCK, ATT_BLOCK), lambda p, i: (p, 0, 0, 0))],
        out_shape=[jax.ShapeDtypeStruct((s, ATT_WIDTH), BF16),
                   jax.ShapeDtypeStruct((s, ATT_WIDTH), F32),
                   jax.ShapeDtypeStruct((s, ATT_WIDTH), F32),
                   jax.ShapeDtypeStruct((2 * N_PAIRS, CA_TILES, ATT_BLOCK, ATT_BLOCK), F32)],
        compiler_params=_params(("arbitrary", "arbitrary")),
    )


def rel_bias_grad(db_tiles):
    heads = db_tiles.shape[0]

    def body(db_ref, out_ref):
        row = lax.broadcasted_iota(jnp.int32, (ATT_BLOCK, ATT_BLOCK), 0)
        col = lax.broadcasted_iota(jnp.int32, (ATT_BLOCK, ATT_BLOCK), 1)
        flip = (row + col == ATT_BLOCK - 1).astype(F32)
        wrapped = col < row
        cp = lax.broadcasted_iota(jnp.int32, (ATT_BLOCK, REL_PAD), 0)
        bins = lax.broadcasted_iota(jnp.int32, (ATT_BLOCK, REL_PAD), 1)
        total = jnp.zeros((heads, REL_PAD), F32)
        for j in range(CA_TILES):
            sums_neg, sums_pos = [], []
            for h in range(heads):
                tile = jnp.dot(db_ref[h, j], flip, preferred_element_type=F32, precision=lax.Precision.HIGHEST)
                tile = pltpu.roll(tile, 0, 1, stride=1, stride_axis=0)
                sums_neg.append(jnp.sum(jnp.where(wrapped, 0.0, tile), axis=0, keepdims=True))
                sums_pos.append(jnp.sum(jnp.where(wrapped, tile, 0.0), axis=0, keepdims=True))
            neg = jnp.concatenate(sums_neg, axis=0)
            pos = jnp.concatenate(sums_pos, axis=0)
            hot_neg = (jnp.clip(ATT_BLOCK * j + cp - (ATT_BLOCK - 1), -REL_CLIP, REL_CLIP) + REL_CLIP == bins)
            hot_pos = (jnp.clip(ATT_BLOCK * j + cp + 1, -REL_CLIP, REL_CLIP) + REL_CLIP == bins)
            total = total + jnp.dot(neg, hot_neg.astype(F32), preferred_element_type=F32,
                                    precision=lax.Precision.HIGHEST)
            total = total + jnp.dot(pos, hot_pos.astype(F32), preferred_element_type=F32,
                                    precision=lax.Precision.HIGHEST)
        out_ref[...] = total

    return pl.pallas_call(
        body, name="rel_bias_grad",
        out_shape=jax.ShapeDtypeStruct((heads, REL_PAD), F32),
        compiler_params=_params(),
    )(db_tiles)


def _assemble_cols(dst_ref, src_ref):
    w = src_ref.shape[2]
    for j in range(N_DEV):
        dst_ref[:, w * j:w * (j + 1)] = src_ref[j]


def _merge(o_sb, o_ca, lg, bg, wsb, wca):
    y_sb = _dot(o_sb.astype(BF16), wsb)
    y_ca = _dot(o_ca.astype(BF16), wca)
    gates = jax.nn.sigmoid(lg + bg)
    g_sb, g_ca = gates[:, :D_MODEL], gates[:, D_MODEL:]
    return y_sb, y_ca, g_sb, g_ca, g_sb * y_sb + g_ca * y_ca


def mix_fwd(x, o_sb, o_ca, lg, b_gate, wg_sb, wg_ca, w_out, tm=256):
    s = x.shape[0]

    def body(x_ref, osb_ref, oca_ref, lg_ref, bg_ref, wsb_ref, wca_ref, wout_ref, u_ref, wsb_s, wca_s):
        @pl.when(pl.program_id(0) == 0)
        def _():
            _assemble_cols(wsb_s, wsb_ref)
            _assemble_cols(wca_s, wca_ref)

        merged = _merge(osb_ref[...], oca_ref[...], lg_ref[...], bg_ref[...], wsb_s[...], wca_s[...])[4]
        u_ref[...] = ALPHA * x_ref[...] + _dot(merged.astype(BF16), wout_ref[...])

    rows = lambda w: pl.BlockSpec((tm, w), lambda i: (i, 0))
    whole = lambda shape: pl.BlockSpec(shape, lambda i: (0,) * len(shape))
    return pl.pallas_call(
        body, name="mix_fwd", grid=(s // tm,),
        in_specs=[rows(D_MODEL), rows(ATT_WIDTH), rows(ATT_WIDTH), rows(GATE_COLS), whole((1, GATE_COLS)),
                  whole((N_DEV, ATT_WIDTH, PROJ_SHARD)), whole((N_DEV, ATT_WIDTH, PROJ_SHARD)),
                  whole((D_MODEL, D_MODEL))],
        out_specs=rows(D_MODEL),
        out_shape=jax.ShapeDtypeStruct((s, D_MODEL), F32),
        scratch_shapes=[pltpu.VMEM((ATT_WIDTH, D_MODEL), BF16), pltpu.VMEM((ATT_WIDTH, D_MODEL), BF16)],
        compiler_params=_params(("arbitrary",)),
    )(x, o_sb, o_ca, lg, _row(b_gate), wg_sb, wg_ca, w_out)


def mix_bwd(du1, o_sb, o_ca, lg, b_gate, wg_sb, wg_ca, w_out, tm=256, guest=None):
    s = du1.shape[0]

    def body(du_ref, osb_ref, oca_ref, lg_ref, bg_ref, wsb_ref, wca_ref, wout_ref,
             dlg_ref, dosb_ref, doca_ref, dysb_ref, dyca_ref, mg_ref, dbg_ref, wsb_s, wca_s):
        @pl.when(pl.program_id(0) == 0)
        def _():
            _assemble_cols(wsb_s, wsb_ref)
            _assemble_cols(wca_s, wca_ref)
            dbg_ref[...] = jnp.zeros_like(dbg_ref)

        y_sb, y_ca, g_sb, g_ca, merged = _merge(
            osb_ref[...], oca_ref[...], lg_ref[...], bg_ref[...], wsb_s[...], wca_s[...])
        mg_ref[...] = merged.astype(BF16)
        dm = _dot_nt(du_ref[...].astype(BF16), wout_ref[...])
        dl_sb = dm * y_sb * g_sb * (1.0 - g_sb)
        dl_ca = dm * y_ca * g_ca * (1.0 - g_ca)
        dlg_ref[:, :D_MODEL] = dl_sb.astype(BF16)
        dlg_ref[:, D_MODEL:] = dl_ca.astype(BF16)
        dbg_ref[:, :D_MODEL] += jnp.sum(dl_sb, axis=0, keepdims=True)
        dbg_ref[:, D_MODEL:] += jnp.sum(dl_ca, axis=0, keepdims=True)
        dy_sb = (dm * g_sb).astype(BF16)
        dy_ca = (dm * g_ca).astype(BF16)
        dysb_ref[...] = dy_sb
        dyca_ref[...] = dy_ca
        dosb_ref[...] = _dot_nt(dy_sb, wsb_s[...]).astype(BF16)
        doca_ref[...] = _dot_nt(dy_ca, wca_s[...]).astype(BF16)

    rows = lambda w: pl.BlockSpec((tm, w), lambda i: (i, 0))
    whole = lambda shape: pl.BlockSpec(shape, lambda i: (0,) * len(shape))
    return _pallas(
        body, guest, (du1, o_sb, o_ca, lg, _row(b_gate), wg_sb, wg_ca, w_out), name="mix_bwd", grid=(s // tm,),
        in_specs=[rows(D_MODEL), rows(ATT_WIDTH), rows(ATT_WIDTH), rows(GATE_COLS), whole((1, GATE_COLS)),
                  whole((N_DEV, ATT_WIDTH, PROJ_SHARD)), whole((N_DEV, ATT_WIDTH, PROJ_SHARD)),
                  whole((D_MODEL, D_MODEL))],
        out_specs=[rows(GATE_COLS), rows(ATT_WIDTH), rows(ATT_WIDTH), rows(D_MODEL), rows(D_MODEL),
                   rows(D_MODEL), whole((1, GATE_COLS))],
        out_shape=[jax.ShapeDtypeStruct((s, GATE_COLS), BF16),
                   jax.ShapeDtypeStruct((s, ATT_WIDTH), BF16), jax.ShapeDtypeStruct((s, ATT_WIDTH), BF16),
                   jax.ShapeDtypeStruct((s, D_MODEL), BF16), jax.ShapeDtypeStruct((s, D_MODEL), BF16),
                   jax.ShapeDtypeStruct((s, D_MODEL), BF16), jax.ShapeDtypeStruct((1, GATE_COLS), F32)],
        scratch_shapes=[pltpu.VMEM((ATT_WIDTH, D_MODEL), BF16), pltpu.VMEM((ATT_WIDTH, D_MODEL), BF16)],
        compiler_params=_params(("arbitrary",)),
    )


def _ln_stats(u):
    mu = jnp.mean(u, axis=1, keepdims=True)
    cen = u - mu
    var = jnp.mean(cen * cen, axis=1, keepdims=True)
    rstd = lax.rsqrt(var + LN_EPS)
    return cen * rstd, rstd


def _ln_bwd(dy, xhat, rstd, gain):
    dyg = dy * gain
    m1 = jnp.mean(dyg, axis=1, keepdims=True)
    m2 = jnp.mean(dyg * xhat, axis=1, keepdims=True)
    return rstd * (dyg - m1 - xhat * m2)


def _load_mlp_weights(win_hbm, wout_hbm, win_s, wout_s, sems):
    copies = [pltpu.make_async_copy(win_hbm.at[k], win_s.at[:, pl.ds(FF_SHARD * k, FF_SHARD)], sems.at[k])
              for k in range(N_DEV)]
    copies.append(pltpu.make_async_copy(wout_hbm, wout_s, sems.at[N_DEV]))
    for cp in copies:
        cp.start()
    return copies


MLP_WEIGHT_SCRATCH = [pltpu.VMEM((D_MODEL, D_FF), BF16), pltpu.VMEM((D_FF, D_MODEL), BF16),
                      pltpu.SemaphoreType.DMA((N_DEV + 1,))]


def mlp_fwd(u1, target, wg_in, w_out, ln1_g, ln1_b, ln2_g, ln2_b, tm=256):
    s = u1.shape[0]

    def body(u_ref, t_ref, win_hbm, wout_hbm, g1_ref, b1_ref, g2_ref, b2_ref,
             hm_ref, a_ref, x1b_ref, du2_ref, loss_ref, dg2_ref, db2_ref, win_s, wout_s, sems):
        first = pl.program_id(0) == 0

        @pl.when(first)
        def _():
            copies = _load_mlp_weights(win_hbm, wout_hbm, win_s, wout_s, sems)
            loss_ref[...] = jnp.zeros_like(loss_ref)
            dg2_ref[...] = jnp.zeros_like(dg2_ref)
            db2_ref[...] = jnp.zeros_like(db2_ref)
            for cp in copies:
                cp.wait()

        xhat, _ = _ln_stats(u_ref[...])
        x1 = xhat * g1_ref[...] + b1_ref[...]
        x1b = x1.astype(BF16)
        x1b_ref[...] = x1b
        for k in range(N_DEV):
            cols = slice(FF_SHARD * k, FF_SHARD * (k + 1))
            hm = _dot(x1b, win_s[:, cols])
            hm_ref[:, cols] = hm
            r = jnp.maximum(hm, 0.0)
            a_ref[:, cols] = (r * r).astype(BF16)
        xhat, rstd = _ln_stats(ALPHA * x1 + _dot(a_ref[...], wout_s[...]))
        diff = xhat * g2_ref[...] + b2_ref[...] - t_ref[...]
        per_token = jnp.mean(diff * diff, axis=1, keepdims=True)
        loss_ref[...] += 0.5 * jnp.sum(per_token, axis=0, keepdims=True)
        dy = diff * (1.0 / D_MODEL)
        dg2_ref[...] += jnp.sum(dy * xhat, axis=0, keepdims=True)
        db2_ref[...] += jnp.sum(dy, axis=0, keepdims=True)
        du2_ref[...] = _ln_bwd(dy, xhat, rstd, g2_ref[...])

    rows = pl.BlockSpec((tm, D_MODEL), lambda i: (i, 0))
    vec = pl.BlockSpec((1, D_MODEL), lambda i: (0, 0))
    ff = pl.BlockSpec((tm, D_FF), lambda i: (i, 0))
    return pl.pallas_call(
        body, name="mlp_fwd", grid=(s // tm,),
        in_specs=[rows, rows, HBM_SPEC, HBM_SPEC, vec, vec, vec, vec],
        out_specs=[ff, ff, rows, rows, pl.BlockSpec((1, 1), lambda i: (0, 0)), vec, vec],
        out_shape=[jax.ShapeDtypeStruct((s, D_FF), F32), jax.ShapeDtypeStruct((s, D_FF), BF16),
                   jax.ShapeDtypeStruct((s, D_MODEL), BF16), jax.ShapeDtypeStruct((s, D_MODEL), F32),
                   jax.ShapeDtypeStruct((1, 1), F32),
                   jax.ShapeDtypeStruct((1, D_MODEL), F32), jax.ShapeDtypeStruct((1, D_MODEL), F32)],
        scratch_shapes=MLP_WEIGHT_SCRATCH,
        compiler_params=_params(("arbitrary",)),
    )(u1, target, wg_in, w_out, _row(ln1_g), _row(ln1_b), _row(ln2_g), _row(ln2_b))


def mlp_bwd(du2, hm, u1, wg_in, w_out, ln1_g, tm=256):
    s = du2.shape[0]

    def body(du2_ref, hm_ref, u_ref, win_hbm, wout_hbm, g1_ref,
             dhm_ref, du1_ref, dg1_ref, db1_ref, win_s, wout_s, sems):
        first = pl.program_id(0) == 0

        @pl.when(first)
        def _():
            copies = _load_mlp_weights(win_hbm, wout_hbm, win_s, wout_s, sems)
            dg1_ref[...] = jnp.zeros_like(dg1_ref)
            db1_ref[...] = jnp.zeros_like(db1_ref)
            for cp in copies:
                cp.wait()

        du2 = du2_ref[...]
        du2b = du2.astype(BF16)
        for k in range(N_DEV):
            cols = slice(FF_SHARD * k, FF_SHARD * (k + 1))
            da = _dot_nt(du2b, wout_s[cols, :])
            dhm_ref[:, cols] = (2.0 * jnp.maximum(hm_ref[:, cols], 0.0) * da).astype(BF16)
        dx1 = ALPHA * du2 + _dot_nt(dhm_ref[...], win_s[...])
        xhat, rstd = _ln_stats(u_ref[...])
        dg1_ref[...] += jnp.sum(dx1 * xhat, axis=0, keepdims=True)
        db1_ref[...] += jnp.sum(dx1, axis=0, keepdims=True)
        du1_ref[...] = _ln_bwd(dx1, xhat, rstd, g1_ref[...])

    rows = pl.BlockSpec((tm, D_MODEL), lambda i: (i, 0))
    vec = pl.BlockSpec((1, D_MODEL), lambda i: (0, 0))
    ff = pl.BlockSpec((tm, D_FF), lambda i: (i, 0))
    return pl.pallas_call(
        body, name="mlp_bwd", grid=(s // tm,),
        in_specs=[rows, ff, rows, HBM_SPEC, HBM_SPEC, vec],
        out_specs=[ff, rows, vec, vec],
        out_shape=[jax.ShapeDtypeStruct((s, D_FF), BF16), jax.ShapeDtypeStruct((s, D_MODEL), F32),
                   jax.ShapeDtypeStruct((1, D_MODEL), F32), jax.ShapeDtypeStruct((1, D_MODEL), F32)],
        scratch_shapes=MLP_WEIGHT_SCRATCH,
        compiler_params=_params(("arbitrary",)),
    )(du2, hm, u1, wg_in, w_out, _row(ln1_g))


def dh_slab(pieces, tm=512):
    s = pieces[0].shape[0]
    widths = [p.shape[1] for p in pieces]
    assert sum(widths) == IN_COLS
    n = len(pieces)

    def body(*refs):
        dh_ref = refs[n]
        off = 0
        for ref, w in zip(refs[:n], widths):
            dh_ref[:, off:off + w] = ref[...].astype(BF16)
            off += w

    rows = lambda w: pl.BlockSpec((tm, w), lambda i: (i, 0))
    return pl.pallas_call(
        body, name="dh_slab", grid=(s // tm,),
        in_specs=[rows(w) for w in widths], out_specs=rows(IN_COLS),
        out_shape=jax.ShapeDtypeStruct((s, IN_COLS), BF16),
        compiler_params=_params(("arbitrary",)),
    )(*pieces)


def in_bwd(du1, dh, wg_in, tm=512, guest=None):
    s = du1.shape[0]

    def body(du_ref, dh_ref, w_ref, gx_ref):
        acc = ALPHA * du_ref[...]
        for j in range(N_DEV):
            acc = acc + _dot_nt(dh_ref[:, IN_SHARD * j:IN_SHARD * (j + 1)], w_ref[j])
        gx_ref[...] = acc

    rows = lambda w: pl.BlockSpec((tm, w), lambda i: (i, 0))
    return _pallas(
        body, guest, (du1, dh, wg_in), name="in_bwd", grid=(s // tm,),
        in_specs=[rows(D_MODEL), rows(IN_COLS), pl.BlockSpec((N_DEV, D_MODEL, IN_SHARD), lambda i: (0, 0, 0))],
        out_specs=[rows(D_MODEL)],
        out_shape=[jax.ShapeDtypeStruct((s, D_MODEL), F32)],
        compiler_params=_params(("arbitrary",)),
    )


def wgrad(name, a, b, tm, tn, shard_cols=None, tk=2048):
    kk, m = a.shape
    n = b.shape[1]
    while kk % tk:
        tk //= 2
    nk = kk // tk
    w = shard_cols

    def body(a_ref, b_ref, out_ref, acc_s):
        k = pl.program_id(2)

        @pl.when(k == 0)
        def _():
            acc_s[...] = jnp.zeros_like(acc_s)

        acc_s[...] += _dot_tn(a_ref[...].astype(BF16), b_ref[...].astype(BF16))

        @pl.when(k == nk - 1)
        def _():
            if w is None:
                out_ref[...] = acc_s[...]
            else:
                for j in range(tn // w):
                    out_ref[j] = acc_s[:, w * j:w * (j + 1)]

    if w is None:
        out_spec = pl.BlockSpec((tm, tn), lambda i, j, k: (i, j))
        out_shape = jax.ShapeDtypeStruct((m, n), F32)
    else:
        out_spec = pl.BlockSpec((tn // w, tm, w), lambda i, j, k: (j, i, 0))
        out_shape = jax.ShapeDtypeStruct((n // w, m, w), F32)
    return pl.pallas_call(
        body, name=name, grid=(m // tm, n // tn, nk),
        in_specs=[pl.BlockSpec((tk, tm), lambda i, j, k: (k, i)),
                  pl.BlockSpec((tk, tn), lambda i, j, k: (k, j))],
        out_specs=out_spec, out_shape=out_shape,
        scratch_shapes=[pltpu.VMEM((tm, tn), F32)],
        compiler_params=_params(("arbitrary", "arbitrary", "arbitrary")),
    )(a, b)


def adamw(name, w, m, v, parts, tm=None, guest=None):
    rows, cols = w.shape
    tm = rows if tm is None else min(tm, rows)
    n = len(parts)
    c1 = 1.0 - ADAM_B1 ** ADAM_STEP
    c2 = 1.0 - ADAM_B2 ** ADAM_STEP

    def body(*refs):
        w_ref, m_ref, v_ref = refs[:3]
        part_refs = refs[3:3 + n]
        g_ref, d_ref, nm_ref, nv_ref = refs[3 + n:]
        g = None
        for ref, (_, index) in zip(part_refs, parts):
            term = (ref[...] if index is None else ref[0]).astype(F32)
            g = term if g is None else g + term
        new_m = ADAM_B1 * m_ref[...] + (1.0 - ADAM_B1) * g
        new_v = ADAM_B2 * v_ref[...] + (1.0 - ADAM_B2) * (g * g)
        m_hat = new_m / c1
        v_hat = new_v / c2
        g_ref[...] = g
        d_ref[...] = -ADAM_LR * (m_hat / (jnp.sqrt(v_hat) + ADAM_EPS) + ADAM_WD * w_ref[...])
        nm_ref[...] = new_m
        nv_ref[...] = new_v

    spec = pl.BlockSpec((tm, cols), lambda i: (i, 0))
    shape = jax.ShapeDtypeStruct((rows, cols), F32)

    def part_spec(index):
        if index is None:
            return spec
        return pl.BlockSpec((1, tm, cols), lambda i: (index, i, 0))

    return _pallas(
        body, guest, (w, m, v, *[array for array, _ in parts]), name=name, grid=(rows // tm,),
        in_specs=[spec] * 3 + [part_spec(index) for _, index in parts],
        out_specs=[spec] * 4, out_shape=[shape] * 4,
        compiler_params=_params(("arbitrary",)),
    )


def _place():
    x, y, c = lax.axis_index("x"), lax.axis_index("y"), lax.axis_index("c")
    other_chips = [(1 - x, y), (x, 1 - y), (1 - x, 1 - y)]
    return x, y, c, other_chips


def place_shards(arrays):
    n = len(arrays)

    def body(*refs):
        ins, outs, stage, sems = refs[:n], refs[n:2 * n], refs[2 * n:3 * n], refs[3 * n]
        x, y, c, _ = _place()
        copies = []
        for a in range(n):
            stage[a][...] = ins[a][...].astype(BF16)
            copies.append(pltpu.make_async_copy(stage[a], outs[a].at[4 * x + 2 * y + c], sems.at[a]))
            copies[-1].start()
        for cp in copies:
            cp.wait()

    return pl.pallas_call(
        body, name="place_shards",
        in_specs=[pl.BlockSpec(memory_space=pltpu.VMEM)] * n, out_specs=[HBM_SPEC] * n,
        out_shape=[jax.ShapeDtypeStruct((N_DEV,) + a.shape, BF16) for a in arrays],
        scratch_shapes=[pltpu.VMEM(a.shape, BF16) for a in arrays] + [pltpu.SemaphoreType.DMA((n,))],
        compiler_params=_params(),
    )(*arrays)


def all_gather(name, arrays):
    n = len(arrays)

    def body(*refs):
        ins, outs = refs[:n], refs[n:2 * n]
        send_sems, recv_sems, local_sems = refs[2 * n:]
        x, y, c, chips = _place()
        me = 4 * x + 2 * y + c
        sibling = (x, y, 1 - c)

        def copy(a, k, block, to, src=None):
            return pltpu.make_async_remote_copy(
                src_ref=outs[a].at[block] if src is None else src, dst_ref=outs[a].at[block],
                send_sem=send_sems.at[a, k], recv_sem=recv_sems.at[a, k],
                device_id=to, device_id_type=MESH)

        started = []
        local = [pltpu.make_async_copy(ins[a], outs[a].at[me], local_sems.at[a]) for a in range(n)]
        for a in range(n):
            local[a].start()
            sends = [copy(a, 0, me, sibling, src=ins[a])]
            sends += [copy(a, 1 + j, me, (cx, cy, c), src=ins[a]) for j, (cx, cy) in enumerate(chips)]
            for cp in sends:
                cp.start()
            started += sends
        for a in range(n):
            for j, (cx, cy) in enumerate(chips):
                block = 4 * cx + 2 * cy + c
                copy(a, 1 + j, block, sibling).wait_recv()
                passed = copy(a, 4 + j, block, sibling)
                passed.start()
                started.append(passed)
        for a in range(n):
            copy(a, 0, 4 * x + 2 * y + (1 - c), sibling).wait_recv()
            for j, (cx, cy) in enumerate(chips):
                copy(a, 4 + j, 4 * cx + 2 * cy + (1 - c), sibling).wait_recv()
        for cp in started:
            cp.wait_send()
        for cp in local:
            cp.wait()

    return pl.pallas_call(
        body, name=name,
        in_specs=[pl.BlockSpec(memory_space=pltpu.VMEM)] * n, out_specs=[HBM_SPEC] * n,
        out_shape=[jax.ShapeDtypeStruct((N_DEV,) + a.shape, a.dtype) for a in arrays],
        scratch_shapes=[pltpu.SemaphoreType.DMA((n, 7)), pltpu.SemaphoreType.DMA((n, 7)),
                        pltpu.SemaphoreType.DMA((n,))],
    )(*arrays)


def _gather_first_plan(n):
    def plan(zones):
        x, y, c, chips = _place()
        me = 4 * x + 2 * y + c
        copies = []
        for zone in zones[:n]:
            copies.append((zone.at[me], zone.at[me], (x, y, 1 - c), zone.at[4 * x + 2 * y + (1 - c)]))
            copies += [(zone.at[me], zone.at[me], (cx, cy, c), zone.at[4 * cx + 2 * cy + c]) for cx, cy in chips]
        return copies
    return plan


def _gather_second_plan(n):
    def plan(zones):
        x, y, c, chips = _place()
        return [(zone.at[4 * cx + 2 * cy + c], zone.at[4 * cx + 2 * cy + c], (x, y, 1 - c),
                 zone.at[4 * cx + 2 * cy + (1 - c)]) for zone in zones[:n] for cx, cy in chips]
    return plan


def _both_plans(first, n_first, second):
    return lambda bufs: first(bufs[:n_first]) + second(bufs[n_first:])


def _sibling_plan(n):
    def plan(bufs):
        x, y, c, _ = _place()
        return [(bufs[a].at[:, 1 - c], bufs[n + a], (x, y, 1 - c), bufs[n + a]) for a in range(n)]
    return plan


def _chips_plan(n):
    def plan(bufs):
        _, _, c, chips = _place()
        return [(bufs[a].at[2 * cx + cy], bufs[n + a].at[j], (cx, cy, c), bufs[n + a].at[j])
                for a in range(n) for j, (cx, cy) in enumerate(chips)]
    return plan


def chip_sum(name, grad, landed, place, tr=256):
    _, _, rows, cols = grad.shape
    tr = min(tr, rows)

    def body(place_ref, g_ref, l_ref, own_ref, out_ref):
        total = g_ref[0, 0] + l_ref[0]
        out_ref[0] = total.astype(BF16)

        @pl.when(pl.program_id(1) == place_ref[1])
        def _():
            own_ref[...] = total

    return pl.pallas_call(
        body, name=name,
        grid_spec=pltpu.PrefetchScalarGridSpec(
            num_scalar_prefetch=1, grid=(rows // tr, 4),
            in_specs=[pl.BlockSpec((1, 1, tr, cols), lambda i, ch, pos: (ch, pos[0], i, 0)),
                      pl.BlockSpec((1, tr, cols), lambda i, ch, pos: (ch, i, 0))],
            out_specs=[pl.BlockSpec((tr, cols), lambda i, ch, pos: (i, 0)),
                       pl.BlockSpec((1, tr, cols), lambda i, ch, pos: (ch, i, 0))]),
        out_shape=[jax.ShapeDtypeStruct((rows, cols), F32), jax.ShapeDtypeStruct((4, rows, cols), BF16)],
        compiler_params=_params(("arbitrary", "arbitrary")),
    )(place, grad, landed)


BIG =("w_in", "w_sb_proj", "w_ca_proj", "w_out", "w_mlp_in", "w_mlp_out")
SMALL = ("b_gate", "rel_bias", "ln1_g", "ln1_b", "ln2_g", "ln2_b")
NAMES = ("w_in", "b_gate", "w_sb_proj", "w_ca_proj", "rel_bias", "w_out", "ln1_g", "ln1_b",
         "w_mlp_in", "w_mlp_out", "ln2_g", "ln2_b")
REL_COLS = 2 * REL_CLIP + 1


def _pack_small(t, scalar=None):
    rel = t["rel_bias"]
    if rel.shape[1] != REL_PAD:
        rel = jnp.pad(rel, ((0, 0), (0, REL_PAD - rel.shape[1])))
    last = jnp.zeros((128,), F32) if scalar is None else jnp.pad(scalar.reshape(1), (0, 127))
    flat = [t["b_gate"].reshape(-1), rel.reshape(-1)] + [t[n].reshape(-1) for n in SMALL[2:]] + [last]
    return jnp.concatenate(flat).reshape(-1, 128)


def _unpack_small(p):
    flat = p.reshape(-1)
    out, off = {}, 0
    for n, size in (("b_gate", GATE_COLS), ("rel_bias", 8 * REL_PAD)) + tuple((n, D_MODEL) for n in SMALL[2:]):
        out[n] = flat[off:off + size]
        off += size
    out["rel_bias"] = out["rel_bias"].reshape(8, REL_PAD)[:, :REL_COLS]
    return out


def kernel(x, w_in, b_gate, w_sb_proj, w_ca_proj, rel_bias, w_out, ln1_g, ln1_b, w_mlp_in, w_mlp_out, ln2_g, ln2_b, loss_target, m_w_in, m_b_gate, m_w_sb_proj, m_w_ca_proj, m_rel_bias, m_w_out, m_ln1_g, m_ln1_b, m_w_mlp_in, m_w_mlp_out, m_ln2_g, m_ln2_b, v_w_in, v_b_gate, v_w_sb_proj, v_w_ca_proj, v_rel_bias, v_w_out, v_ln1_g, v_ln1_b, v_w_mlp_in, v_w_mlp_out, v_ln2_g, v_ln2_b):
    w = dict(w_in=w_in, b_gate=b_gate, w_sb_proj=w_sb_proj, w_ca_proj=w_ca_proj, rel_bias=rel_bias, w_out=w_out,
             ln1_g=ln1_g, ln1_b=ln1_b, w_mlp_in=w_mlp_in, w_mlp_out=w_mlp_out, ln2_g=ln2_g, ln2_b=ln2_b)
    m = dict(w_in=m_w_in, b_gate=m_b_gate, w_sb_proj=m_w_sb_proj, w_ca_proj=m_w_ca_proj, rel_bias=m_rel_bias,
             w_out=m_w_out, ln1_g=m_ln1_g, ln1_b=m_ln1_b, w_mlp_in=m_w_mlp_in, w_mlp_out=m_w_mlp_out,
             ln2_g=m_ln2_g, ln2_b=m_ln2_b)
    v = dict(w_in=v_w_in, b_gate=v_b_gate, w_sb_proj=v_w_sb_proj, w_ca_proj=v_w_ca_proj, rel_bias=v_rel_bias,
             w_out=v_w_out, ln1_g=v_ln1_g, ln1_b=v_ln1_b, w_mlp_in=v_w_mlp_in, w_mlp_out=v_w_mlp_out,
             ln2_g=v_ln2_g, ln2_b=v_ln2_b)

    x, target = x[0], loss_target[0]
    c = lax.axis_index("c")
    place = jnp.stack([c, 2 * lax.axis_index("x") + lax.axis_index("y")]).astype(jnp.int32)
    grads, deltas, new_m, new_v = {}, {}, {}, {}

    def by_owner(n, g):
        return g.reshape((4, 2) + w[n].shape)

    def finish(n, own, arrived, guest=None):
        out = adamw("adamw_" + n, w[n], m[n], v[n], [(own, None), (arrived, 0), (arrived, 1), (arrived, 2)],
                    tm=256, guest=guest)
        out, bufs = (out, None) if guest is None else out
        grads[n], deltas[n], new_m[n], new_v[n] = out
        return bufs

    def siblings(names, g):
        bufs = [by_owner(n, g[n]) for n in names] + [lax.empty((4,) + w[n].shape, F32) for n in names]
        return Exchange(tuple(bufs), _sibling_plan(len(names)), len(names))

    def chip_sums(names, bufs):
        k = len(names)
        own, sums = zip(*[chip_sum("chip_sum_" + n, bufs[a], bufs[k + a], place) for a, n in enumerate(names)])
        zones = [lax.empty((3,) + w[n].shape, BF16) for n in names]
        return own, Exchange(tuple(sums) + tuple(zones), _chips_plan(k), 3 * k)

    rest = BIG[1:]
    zones = place_shards([w[n] for n in rest])
    wg_in = all_gather("gather_w_in", [w_in.astype(BF16)])[0]
    qkv, lg, xb = in_proj(x, wg_in)
    bias_tiles = ca_bias_tiles(rel_bias)
    (o_sb,), zones = sb_fwd(qkv, Exchange(tuple(zones), _gather_first_plan(len(rest)), 4 * len(rest)))
    (o_ca,), zones = ca_fwd(qkv, bias_tiles, Exchange(tuple(zones), _gather_second_plan(len(rest)), 3 * len(rest)))
    wg = dict(zip(rest, zones))
    wg_sb, wg_ca, wg_mi = wg["w_sb_proj"], wg["w_ca_proj"], wg["w_mlp_in"]
    wg_mo = wg["w_mlp_out"].reshape(D_FF, D_MODEL)
    w_out_b = wg["w_out"].reshape(D_MODEL, D_MODEL)

    u1 = mix_fwd(x, o_sb, o_ca, lg, b_gate, wg_sb, wg_ca, w_out_b)
    hm, a, x1b, du2, loss, dg2, db2 = mlp_fwd(u1, target, wg_mi, wg_mo, ln1_g, ln1_b, ln2_g, ln2_b)
    dhm, du1, dg1, db1 = mlp_bwd(du2, hm, u1, wg_mi, wg_mo, ln1_g)
    g = {}
    g["w_mlp_in"] = wgrad("wgrad_mlp_in", x1b, dhm, tm=D_MODEL, tn=FF_SHARD, shard_cols=FF_SHARD)
    g["w_mlp_out"] = wgrad("wgrad_mlp_out", a, du2, tm=FF_SHARD, tn=D_MODEL)

    mlp = ("w_mlp_in", "w_mlp_out")
    (dlg, do_sb, do_ca, dy_sb, dy_ca, merged, dbg), bufs = mix_bwd(
        du1, o_sb, o_ca, lg, b_gate, wg_sb, wg_ca, w_out_b, guest=siblings(mlp, g))
    own_mlp, mlp_chips = chip_sums(mlp, bufs)
    g["w_out"] = wgrad("wgrad_out", merged, du1, tm=D_MODEL, tn=D_MODEL)
    g["w_sb_proj"] = wgrad("wgrad_sb_proj", o_sb, dy_sb, tm=ATT_WIDTH, tn=D_MODEL, shard_cols=PROJ_SHARD)
    g["w_ca_proj"] = wgrad("wgrad_ca_proj", o_ca, dy_ca, tm=ATT_WIDTH, tn=D_MODEL, shard_cols=PROJ_SHARD)

    att = ("w_out", "w_sb_proj", "w_ca_proj")
    att_siblings = siblings(att, g)
    both = Exchange(mlp_chips.bufs + att_siblings.bufs,
                    _both_plans(mlp_chips.plan, len(mlp_chips.bufs), att_siblings.plan),
                    mlp_chips.copies + att_siblings.copies)
    (dq_sb, dk_sb, dv_sb), bufs = sb_bwd(qkv, o_sb, do_sb, guest=both)
    arrived_mlp, bufs = bufs[len(mlp):len(mlp_chips.bufs)], bufs[len(mlp_chips.bufs):]
    own_att, att_chips = chip_sums(att, bufs)
    (dq_ca, dk_ca, dv_ca, db_tiles), bufs = ca_bwd(qkv, bias_tiles, do_ca, guest=att_chips)
    arrived_att = bufs[len(att):]
    g_rel = rel_bias_grad(db_tiles)

    dh = dh_slab([dq_sb, dk_sb, dv_sb, dq_ca, dk_ca, dv_ca, dlg])
    g["w_in"] = wgrad("wgrad_in", xb, dh, tm=D_MODEL, tn=IN_SHARD, shard_cols=IN_SHARD)
    last = ("w_in",)
    bufs = finish(mlp[0], own_mlp[0], arrived_mlp[0], guest=siblings(last, g))
    finish(mlp[1], own_mlp[1], arrived_mlp[1])
    for n, own, arrived in zip(att, own_att, arrived_att):
        finish(n, own, arrived)
    own_in, in_chips = chip_sums(last, bufs)
    (grad_x,), bufs = in_bwd(du1, dh, wg_in, guest=in_chips)
    finish(last[0], own_in[0], bufs[1])

    small = dict(b_gate=dbg, rel_bias=g_rel, ln1_g=dg1, ln1_b=db1, ln2_g=dg2, ln2_b=db2)
    parts = all_gather("gather_small_grads", [_pack_small(small, loss[0, 0])])[0]
    packed = adamw("adamw_small", _pack_small(w), _pack_small(m), _pack_small(v),
                   [(parts, d) for d in range(N_DEV)])
    loss = packed[0][-1, 0]
    for out, p in zip((grads, deltas, new_m, new_v), packed):
        for n, val in _unpack_small(p).items():
            out[n] = val.reshape(w[n].shape)

    return (loss, grad_x[None], *[grads[n] for n in NAMES], *[deltas[n] for n in NAMES],
            *[new_m[n] for n in NAMES], *[new_v[n] for n in NAMES])
```

```python
import functools
from typing import NamedTuple

import jax
import jax.numpy as jnp
from jax import lax
from jax.experimental import pallas as pl
from jax.experimental.pallas import tpu as pltpu

F32 = jnp.float32
BF16 = jnp.bfloat16
MESH = pl.DeviceIdType.MESH

N_DEV = 8
D_MODEL = 1024
HEAD_DIM = 64
ATT_WIDTH = 512
N_PAIRS = ATT_WIDTH // 128
QKV_COLS = 6 * ATT_WIDTH
GATE_COLS = 2 * D_MODEL
IN_COLS = QKV_COLS + GATE_COLS
IN_SHARD = IN_COLS // N_DEV
D_FF = 4 * D_MODEL
FF_SHARD = D_FF // N_DEV
PROJ_SHARD = D_MODEL // N_DEV
ATT_BLOCK = 128
CA_TILES = 5
CA_ROWS = 1024
CHUNK = 64
CA_PREV_CHUNKS = 8
REL_CLIP = 256
REL_PAD = 640
ALPHA = 2.0 ** 0.25
LN_EPS = 1e-5
QK_SCALE = HEAD_DIM ** -0.5
NEG_BIG = -1e30
VMEM_LIMIT = 56 * 1024 * 1024

ADAM_LR = 0.001
ADAM_B1 = 0.9
ADAM_B2 = 0.999
ADAM_EPS = 1e-08
ADAM_WD = 0.01
ADAM_STEP = 10

_NT = (((1,), (1,)), ((), ()))
_TN = (((0,), (0,)), ((), ()))


def _dot(a, b):
    return jnp.dot(a, b, preferred_element_type=F32)


def _dot_nt(a, b):
    return lax.dot_general(a, b, _NT, preferred_element_type=F32)


def _dot_tn(a, b):
    return lax.dot_general(a, b, _TN, preferred_element_type=F32)


def _params(semantics=None):
    return pltpu.CompilerParams(dimension_semantics=semantics, vmem_limit_bytes=VMEM_LIMIT)


def _row(v):
    return v.reshape(1, -1)


HBM_SPEC = pl.BlockSpec(memory_space=pl.ANY)


class Exchange(NamedTuple):
    bufs: tuple
    plan: object
    copies: int


def _pallas(body, guest, args, *, grid, in_specs, out_specs, out_shape, scratch_shapes=(), **kw):
    out_specs, out_shape, scratch_shapes = list(out_specs), list(out_shape), list(scratch_shapes)
    if guest is None:
        return pl.pallas_call(body, grid=grid, in_specs=in_specs, out_specs=out_specs, out_shape=out_shape,
                              scratch_shapes=scratch_shapes, **kw)(*args)
    nb, n_own = len(guest.bufs), len(in_specs) + len(out_specs)

    def hosting(*refs):
        zones = refs[nb + n_own:2 * nb + n_own]
        send_sems, recv_sems = refs[-2], refs[-1]
        steps = [pl.program_id(axis) for axis in range(len(grid))]
        first = functools.reduce(jnp.logical_and, [step == 0 for step in steps])
        last = functools.reduce(jnp.logical_and, [step == n - 1 for step, n in zip(steps, grid)])

        def copy(k, src, dst, peer):
            return pltpu.make_async_remote_copy(src_ref=src, dst_ref=dst, send_sem=send_sems.at[k],
                                                recv_sem=recv_sems.at[k], device_id=peer, device_id_type=MESH)

        @pl.when(first)
        def _():
            for k, (src, dst, peer, _) in enumerate(guest.plan(zones)):
                copy(k, src, dst, peer).start()

        body(*refs[nb:nb + n_own], *refs[2 * nb + n_own:-2])

        @pl.when(last)
        def _():
            for k, (src, dst, peer, landed) in enumerate(guest.plan(zones)):
                copy(k, src, dst, peer).wait_send()
                copy(k, src, landed, peer).wait_recv()

    sems = pltpu.SemaphoreType.DMA((guest.copies,))
    results = pl.pallas_call(
        hosting, grid=grid, in_specs=[HBM_SPEC] * nb + list(in_specs), out_specs=out_specs + [HBM_SPEC] * nb,
        out_shape=out_shape + [jax.ShapeDtypeStruct(b.shape, b.dtype) for b in guest.bufs],
        input_output_aliases={k: len(out_specs) + k for k in range(nb)},
        scratch_shapes=scratch_shapes + [sems, sems], **kw)(*guest.bufs, *args)
    return results[:len(out_specs)], results[len(out_specs):]


def in_proj(x, wg_in, tm=512, guest=None):
    s = x.shape[0]

    def body(x_ref, w_ref, qkv_ref, lg_ref, xb_ref):
        xb = x_ref[...].astype(BF16)
        xb_ref[...] = xb
        for j in range(N_DEV):
            acc = _dot(xb, w_ref[j])
            lo, hi = IN_SHARD * j, IN_SHARD * (j + 1)
            if hi <= QKV_COLS:
                qkv_ref[:, lo:hi] = acc.astype(BF16)
            elif lo >= QKV_COLS:
                lg_ref[:, lo - QKV_COLS:hi - QKV_COLS] = acc
            else:
                qkv_ref[:, lo:QKV_COLS] = acc[:, :QKV_COLS - lo].astype(BF16)
                lg_ref[:, 0:hi - QKV_COLS] = acc[:, QKV_COLS - lo:]

    return _pallas(
        body, guest, (x, wg_in), name="in_proj", grid=(s // tm,),
        in_specs=[pl.BlockSpec((tm, D_MODEL), lambda i: (i, 0)),
                  pl.BlockSpec((N_DEV, D_MODEL, IN_SHARD), lambda i: (0, 0, 0))],
        out_specs=[pl.BlockSpec((tm, QKV_COLS), lambda i: (i, 0)),
                   pl.BlockSpec((tm, GATE_COLS), lambda i: (i, 0)),
                   pl.BlockSpec((tm, D_MODEL), lambda i: (i, 0))],
        out_shape=[jax.ShapeDtypeStruct((s, QKV_COLS), BF16),
                   jax.ShapeDtypeStruct((s, GATE_COLS), F32),
                   jax.ShapeDtypeStruct((s, D_MODEL), BF16)],
        compiler_params=_params(("arbitrary",)),
    )


def _head_masks():
    lane = lax.broadcasted_iota(jnp.int32, (1, 128), 1)
    first = lane < HEAD_DIM
    return first, jnp.logical_not(first)


SB_CHUNK = 512
SB_SUB = SB_CHUNK // ATT_BLOCK


def _tri_pair():
    row = lax.broadcasted_iota(jnp.int32, (4 * ATT_BLOCK, 2 * ATT_BLOCK), 0) & (2 * ATT_BLOCK - 1)
    col = lax.broadcasted_iota(jnp.int32, (4 * ATT_BLOCK, 2 * ATT_BLOCK), 1)
    same = (row >= ATT_BLOCK) == (col >= ATT_BLOCK)
    return jnp.logical_and(same, row >= col).astype(BF16)


def _pair_bf16(pieces):
    return jnp.concatenate(
        [jnp.concatenate([p.astype(BF16) for p in row], axis=1) for row in pieces], axis=0)


def _suffix_sums(pieces, tri_pair):
    hi = [[p.astype(BF16) for p in row] for row in pieces]
    lo = [[p - h.astype(F32) for p, h in zip(row, hrow)] for row, hrow in zip(pieces, hi)]
    return _dot(jnp.concatenate([_pair_bf16(hi), _pair_bf16(lo)], axis=1), tri_pair)


def _first_row_piece(diag, j):
    return j if diag else 0


def _piece(v, r, h):
    return v[ATT_BLOCK * r:ATT_BLOCK * (r + 1), ATT_BLOCK * h:ATT_BLOCK * (h + 1)]


def _all_row_pieces(slabs):
    return [[slab[ATT_BLOCK * r:ATT_BLOCK * (r + 1)] for r in range(SB_SUB)] for slab in slabs]


def _add_rows(total, part, r0):
    if r0 == 0:
        return total + part
    return jnp.concatenate([total[:ATT_BLOCK * r0], total[ATT_BLOCK * r0:] + part], axis=0)


def _row_totals(csum, r0, h):
    return jnp.concatenate(
        [jnp.broadcast_to(_piece(csum, at, h)[:, 0:1], (ATT_BLOCK, ATT_BLOCK)) for at in range(SB_SUB - r0)], axis=0)


SIGN_BIT = 0x80000000


def _drop(z):
    minus_abs = lax.bitcast_convert_type(lax.bitcast_convert_type(z, jnp.uint32) | jnp.uint32(SIGN_BIT), F32)
    return jnp.maximum(z, 0.0) + jnp.log(1.0 + jnp.exp(minus_abs))


def _sb_weights(z, runs, r0, masked, strict, tri_pair):
    drops = []
    for r in range(r0, SB_SUB):
        row = [_drop(_piece(z, r - r0, h)) for h in range(2)]
        drops.append([jnp.where(strict, drop, 0.0) for drop in row] if r == masked else row)
    csum = _suffix_sums(drops, tri_pair)
    weights = []
    for r in range(r0, SB_SUB):
        row = [jnp.exp(_piece(z, r - r0, h) - (_piece(csum, r - r0, h) + runs[h][r])) for h in range(2)]
        weights.append([jnp.where(strict, a, 0.0) for a in row] if r == masked else row)
    return drops, weights, csum


EXP_UNDERFLOW = 104.0


SB_STEP = 2


def _sweep_left(span, i, dropped_ref):
    def more(at):
        return jnp.logical_and(at >= 0, jnp.min(dropped_ref[...]) < EXP_UNDERFLOW)

    def step(at):
        span(at * (SB_STEP * ATT_BLOCK), SB_STEP, False)
        return at - 1

    lax.while_loop(more, step, i * (SB_SUB // SB_STEP) - 1)


def _by_head(blk, masks):
    zero = jnp.zeros_like(blk)
    return jnp.concatenate([jnp.where(m, blk, zero) for m in masks], axis=0)


def sb_fwd(qkv, guest=None):
    s = qkv.shape[0]
    nq = s // SB_CHUNK

    def body(q_ref, k_ref, v_ref, o_ref, run_s):
        i = pl.program_id(1)
        masks = _head_masks()
        q = q_ref[...] * QK_SCALE
        row = lax.broadcasted_iota(jnp.int32, (ATT_BLOCK, ATT_BLOCK), 0)
        col = lax.broadcasted_iota(jnp.int32, (ATT_BLOCK, ATT_BLOCK), 1)
        strict = col < row
        tri_pair = _tri_pair()
        run_s[...] = jnp.zeros_like(run_s)
        o_ref[...] = jnp.zeros_like(o_ref)

        def span(start, pieces, diag):
            runs = [run_s[0], run_s[1]]
            acc = jnp.zeros((SB_CHUNK, ATT_BLOCK), F32)
            for j in reversed(range(pieces)):
                r0 = _first_row_piece(diag, j)
                keys = pl.ds(pl.multiple_of(start + ATT_BLOCK * j, ATT_BLOCK), ATT_BLOCK)
                z = _dot_nt(q[ATT_BLOCK * r0:], _by_head(k_ref[keys, :], masks))
                _, weights, csum = _sb_weights(z, _all_row_pieces(runs), r0, j if diag else None, strict, tri_pair)
                part = _dot(_pair_bf16(weights), _by_head(v_ref[keys, :], masks))
                acc = _add_rows(acc, part, r0)
                runs = [_add_rows(runs[h], _row_totals(csum, r0, h), r0) for h in range(2)]
            run_s[0], run_s[1] = runs
            o_ref[...] += acc

        span(i * SB_CHUNK, SB_SUB, True)
        _sweep_left(span, i, run_s)

    return _pallas(
        body, guest, (qkv, qkv, qkv), name="sb_fwd", grid=(N_PAIRS, nq),
        in_specs=[pl.BlockSpec((SB_CHUNK, 128), lambda p, i: (i, p)),
                  pl.BlockSpec((s, 128), lambda p, i: (0, N_PAIRS + p)),
                  pl.BlockSpec((s, 128), lambda p, i: (0, 2 * N_PAIRS + p))],
        out_specs=[pl.BlockSpec((SB_CHUNK, 128), lambda p, i: (i, p))],
        out_shape=[jax.ShapeDtypeStruct((s, ATT_WIDTH), F32)],
        scratch_shapes=[pltpu.VMEM((2, SB_CHUNK, 128), F32)],
        compiler_params=_params(("arbitrary", "arbitrary")),
    )


def sb_bwd(qkv, o_sb, do_sb, guest=None):
    s = qkv.shape[0]
    nq = s // SB_CHUNK

    def body(q_ref, k_ref, v_ref, o_ref, do_ref, dq_ref, dk_ref, dv_ref, run_l_s, run_e_s, etot_s, dq_s):
        i = pl.program_id(1)

        @pl.when(i == 0)
        def _():
            dk_ref[...] = jnp.zeros_like(dk_ref)
            dv_ref[...] = jnp.zeros_like(dv_ref)

        masks = _head_masks()
        q = q_ref[...] * QK_SCALE
        do = do_ref[...]
        prod = do.astype(F32) * o_ref[...]
        for h in range(2):
            total = jnp.sum(jnp.where(masks[h], prod, 0.0), axis=1, keepdims=True)
            etot_s[h] = jnp.broadcast_to(total, (SB_CHUNK, ATT_BLOCK))
        row = lax.broadcasted_iota(jnp.int32, (ATT_BLOCK, ATT_BLOCK), 0)
        col = lax.broadcasted_iota(jnp.int32, (ATT_BLOCK, ATT_BLOCK), 1)
        strict = col < row
        tri_pair = _tri_pair()
        run_l_s[...] = jnp.zeros_like(run_l_s)
        run_e_s[...] = jnp.zeros_like(run_e_s)
        dq_s[...] = jnp.zeros_like(dq_s)

        def span(start, pieces, diag):
            slabs_l = [run_l_s[0], run_l_s[1]]
            slabs_e = [run_e_s[0], run_e_s[1]]
            e_tot = _all_row_pieces([etot_s[0], etot_s[1]])
            dq = jnp.zeros((SB_CHUNK, ATT_BLOCK), F32)
            for j in reversed(range(pieces)):
                runs_l, runs_e = _all_row_pieces(slabs_l), _all_row_pieces(slabs_e)
                r0 = _first_row_piece(diag, j)
                masked = j if diag else None
                keys = pl.ds(pl.multiple_of(start + ATT_BLOCK * j, ATT_BLOCK), ATT_BLOCK)
                kk = _by_head(k_ref[keys, :], masks)
                vv = _by_head(v_ref[keys, :], masks)
                q_rows, do_rows = q[ATT_BLOCK * r0:], do[ATT_BLOCK * r0:]
                z = _dot_nt(q_rows, kk)
                da = _dot_nt(do_rows, vv)
                drops, weights, csum = _sb_weights(z, runs_l, r0, masked, strict, tri_pair)
                ab = [[a.astype(BF16) for a in wrow] for wrow in weights]
                es = [[a.astype(F32) * _piece(da, at, h) for h, a in enumerate(arow)] for at, arow in enumerate(ab)]
                esum = _suffix_sums(es, tri_pair)
                dzs = []
                for at, r in enumerate(range(r0, SB_SUB)):
                    dz_row = []
                    for h in range(2):
                        sig = jnp.exp(_piece(z, at, h) - drops[at][h])
                        after = _piece(esum, at, h) + runs_e[h][r] - es[at][h]
                        dz = es[at][h] - sig * (e_tot[h][r] - after)
                        dz_row.append(jnp.where(strict, dz, 0.0) if r == masked else dz)
                    dzs.append(dz_row)
                dz = _pair_bf16(dzs)
                dq = _add_rows(dq, _dot(dz, kk), r0)
                dk = _dot_tn(dz, q_rows)
                dv = _dot_tn(_pair_bf16(ab), do_rows)
                dk_ref[keys, :] += jnp.where(masks[0], dk[:ATT_BLOCK], dk[ATT_BLOCK:])
                dv_ref[keys, :] += jnp.where(masks[0], dv[:ATT_BLOCK], dv[ATT_BLOCK:])
                slabs_l = [_add_rows(slabs_l[h], _row_totals(csum, r0, h), r0) for h in range(2)]
                slabs_e = [_add_rows(slabs_e[h], _row_totals(esum, r0, h), r0) for h in range(2)]
            run_l_s[0], run_l_s[1] = slabs_l
            run_e_s[0], run_e_s[1] = slabs_e
            dq_s[...] += dq

        span(i * SB_CHUNK, SB_SUB, True)
        _sweep_left(span, i, run_l_s)
        dq_ref[...] = (dq_s[...] * QK_SCALE).astype(BF16)

    tile = lambda p, i: (i, p)
    state = pltpu.VMEM((2, SB_CHUNK, 128), F32)
    return _pallas(
        body, guest, (qkv, qkv, qkv, o_sb, do_sb), name="sb_bwd", grid=(N_PAIRS, nq),
        in_specs=[pl.BlockSpec((SB_CHUNK, 128), tile),
                  pl.BlockSpec((s, 128), lambda p, i: (0, N_PAIRS + p)),
                  pl.BlockSpec((s, 128), lambda p, i: (0, 2 * N_PAIRS + p)),
                  pl.BlockSpec((SB_CHUNK, 128), tile),
                  pl.BlockSpec((SB_CHUNK, 128), tile)],
        out_specs=[pl.BlockSpec((SB_CHUNK, 128), tile),
                   pl.BlockSpec((s, 128), lambda p, i: (0, p)),
                   pl.BlockSpec((s, 128), lambda p, i: (0, p))],
        out_shape=[jax.ShapeDtypeStruct((s, ATT_WIDTH), BF16),
                   jax.ShapeDtypeStruct((s, ATT_WIDTH), F32),
                   jax.ShapeDtypeStruct((s, ATT_WIDTH), F32)],
        scratch_shapes=[state, state, state, pltpu.VMEM((SB_CHUNK, 128), F32)],
        compiler_params=_params(("arbitrary", "arbitrary")),
    )


def ca_bias_tiles(rel_bias):
    heads = rel_bias.shape[0]
    wide = 2 * ATT_BLOCK

    def body(rel_ref, out_ref):
        bins = lax.broadcasted_iota(jnp.int32, (REL_PAD, wide), 0)
        pos = lax.broadcasted_iota(jnp.int32, (REL_PAD, wide), 1)
        r = lax.broadcasted_iota(jnp.int32, (ATT_BLOCK, ATT_BLOCK), 0)
        c = lax.broadcasted_iota(jnp.int32, (ATT_BLOCK, ATT_BLOCK), 1)
        for j in range(CA_TILES):
            hot = (jnp.clip(ATT_BLOCK * (j + 1) - 1 - pos, -REL_CLIP, REL_CLIP) + REL_CLIP == bins).astype(F32)
            lines = jnp.dot(rel_ref[...], hot, preferred_element_type=F32, precision=lax.Precision.HIGHEST)
            back = 2 * j + (r >> 6) - (c >> 6)
            ok = jnp.logical_and(back >= 0, back <= CA_PREV_CHUNKS)
            for h in range(heads):
                tile = jnp.broadcast_to(lines[h:h + 1, :], (ATT_BLOCK, wide))
                tile = pltpu.roll(tile, wide - (ATT_BLOCK - 1), 1, stride=1, stride_axis=0)
                out_ref[h, j] = jnp.where(ok, tile[:, :ATT_BLOCK], NEG_BIG)

    padded = jnp.pad(rel_bias, ((0, 0), (0, REL_PAD - rel_bias.shape[1])))
    return pl.pallas_call(
        body, name="ca_bias_tiles",
        out_shape=jax.ShapeDtypeStruct((heads, CA_TILES, ATT_BLOCK, ATT_BLOCK), F32),
        compiler_params=_params(),
    )(padded)


CA_BLOCKS = CA_ROWS // ATT_BLOCK
CA_KEY_BLOCKS = CA_BLOCKS + CA_TILES - 1


def _ca_users(t):
    return [r for r in range(CA_BLOCKS) if 0 <= r + CA_TILES - 1 - t < CA_TILES]


def _ca_rows(v, t, r):
    at = _ca_users(t).index(r)
    return v[ATT_BLOCK * at:ATT_BLOCK * (at + 1)]


def _ca_window(step, k_ref, v_ref, masks):
    kks, vvs, inside, rows = [], [], [], []
    for t in range(CA_KEY_BLOCKS):
        block = step * CA_BLOCKS - (CA_TILES - 1) + t
        keys = pl.ds(pl.multiple_of(jnp.maximum(block, 0) * ATT_BLOCK, ATT_BLOCK), ATT_BLOCK)
        kks.append(_by_head(k_ref[keys, :], masks))
        vvs.append(_by_head(v_ref[keys, :], masks))
        inside.append(block >= 0)
        rows.append(keys)
    return kks, vvs, inside, rows


def _ca_probs(q, kks, inside, bias_ref):
    scores = [[[None] * CA_TILES for _ in range(2)] for _ in range(CA_BLOCKS)]
    for t in range(CA_KEY_BLOCKS):
        users = _ca_users(t)
        z = _dot_nt(q[ATT_BLOCK * users[0]:ATT_BLOCK * (users[-1] + 1)], kks[t])
        for r in users:
            j = r + CA_TILES - 1 - t
            for h in range(2):
                zz = _ca_rows(z, t, r)[:, ATT_BLOCK * h:ATT_BLOCK * (h + 1)] + bias_ref[h, j]
                scores[r][h][j] = jnp.where(inside[t], zz, NEG_BIG)
    probs = [[None, None] for _ in range(CA_BLOCKS)]
    for r in range(CA_BLOCKS):
        for h in range(2):
            m = jnp.max(functools.reduce(jnp.maximum, scores[r][h]), axis=1, keepdims=True)
            ex = [jnp.exp(z - m) for z in scores[r][h]]
            inv = 1.0 / jnp.sum(functools.reduce(jnp.add, ex), axis=1, keepdims=True)
            probs[r][h] = [e * inv for e in ex]
    return probs


def _ca_stack(tiles, t):
    return jnp.concatenate(
        [jnp.concatenate([tiles[r][h][r + CA_TILES - 1 - t].astype(BF16) for h in range(2)], axis=1)
         for r in _ca_users(t)], axis=0)


def ca_fwd(qkv, bias_tiles, guest=None):
    s = qkv.shape[0]
    nq = s // CA_ROWS

    def body(q_ref, k_ref, v_ref, bias_ref, o_ref):
        masks = _head_masks()
        kks, vvs, inside, _ = _ca_window(pl.program_id(1), k_ref, v_ref, masks)
        probs = _ca_probs(q_ref[...] * QK_SCALE, kks, inside, bias_ref)
        out = [None] * CA_BLOCKS
        for t in range(CA_KEY_BLOCKS):
            part = _dot(_ca_stack(probs, t), vvs[t])
            for r in _ca_users(t):
                piece = _ca_rows(part, t, r)
                out[r] = piece if out[r] is None else out[r] + piece
        o_ref[...] = jnp.concatenate(out, axis=0)

    return _pallas(
        body, guest, (qkv, qkv, qkv, bias_tiles), name="ca_fwd", grid=(N_PAIRS, nq),
        in_specs=[pl.BlockSpec((CA_ROWS, 128), lambda p, i: (i, 3 * N_PAIRS + p)),
                  pl.BlockSpec((s, 128), lambda p, i: (0, 4 * N_PAIRS + p)),
                  pl.BlockSpec((s, 128), lambda p, i: (0, 5 * N_PAIRS + p)),
                  pl.BlockSpec((2, CA_TILES, ATT_BLOCK, ATT_BLOCK), lambda p, i: (p, 0, 0, 0))],
        out_specs=[pl.BlockSpec((CA_ROWS, 128), lambda p, i: (i, p))],
        out_shape=[jax.ShapeDtypeStruct((s, ATT_WIDTH), F32)],
        compiler_params=_params(("arbitrary", "arbitrary")),
    )


def ca_bwd(qkv, bias_tiles, do_ca, guest=None):
    s = qkv.shape[0]
    nq = s // CA_ROWS

    def body(q_ref, k_ref, v_ref, bias_ref, do_ref, dq_ref, dk_ref, dv_ref, db_ref):
        big = pl.program_id(1)

        @pl.when(big == 0)
        def _():
            dk_ref[...] = jnp.zeros_like(dk_ref)
            dv_ref[...] = jnp.zeros_like(dv_ref)
            db_ref[...] = jnp.zeros_like(db_ref)

        masks = _head_masks()
        kks, vvs, inside, key_rows = _ca_window(big, k_ref, v_ref, masks)
        q = q_ref[...] * QK_SCALE
        do = do_ref[...]
        probs = _ca_probs(q, kks, inside, bias_ref)
        dps = [[[None] * CA_TILES for _ in range(2)] for _ in range(CA_BLOCKS)]
        for t in range(CA_KEY_BLOCKS):
            users = _ca_users(t)
            dp = _dot_nt(do[ATT_BLOCK * users[0]:ATT_BLOCK * (users[-1] + 1)], vvs[t])
            for r in users:
                for h in range(2):
                    dps[r][h][r + CA_TILES - 1 - t] = _ca_rows(dp, t, r)[:, ATT_BLOCK * h:ATT_BLOCK * (h + 1)]
        dss = [[None, None] for _ in range(CA_BLOCKS)]
        for r in range(CA_BLOCKS):
            for h in range(2):
                delta = jnp.sum(functools.reduce(jnp.add, [p * dp for p, dp in zip(probs[r][h], dps[r][h])]),
                                axis=1, keepdims=True)
                dss[r][h] = [p * (dp - delta) for p, dp in zip(probs[r][h], dps[r][h])]
        for h in range(2):
            for j in range(CA_TILES):
                db_ref[h, j] += functools.reduce(jnp.add, [dss[r][h][j] for r in range(CA_BLOCKS)])
        dq = [None] * CA_BLOCKS
        for t in range(CA_KEY_BLOCKS):
            users = _ca_users(t)
            rows = slice(ATT_BLOCK * users[0], ATT_BLOCK * (users[-1] + 1))
            ds = _ca_stack(dss, t)
            part = _dot(ds, kks[t])
            for r in users:
                piece = _ca_rows(part, t, r)
                dq[r] = piece if dq[r] is None else dq[r] + piece
            dk = _dot_tn(ds, q[rows])
            dv = _dot_tn(_ca_stack(probs, t), do[rows])
            dk_ref[key_rows[t], :] += jnp.where(masks[0], dk[:ATT_BLOCK], dk[ATT_BLOCK:])
            dv_ref[key_rows[t], :] += jnp.where(masks[0], dv[:ATT_BLOCK], dv[ATT_BLOCK:])
        dq_ref[...] = (jnp.concatenate(dq, axis=0) * QK_SCALE).astype(BF16)

    return _pallas(
        body, guest, (qkv, qkv, qkv, bias_tiles, do_ca), name="ca_bwd", grid=(N_PAIRS, nq),
        in_specs=[pl.BlockSpec((CA_ROWS, 128), lambda p, i: (i, 3 * N_PAIRS + p)),
                  pl.BlockSpec((s, 128), lambda p, i: (0, 4 * N_PAIRS + p)),
                  pl.BlockSpec((s, 128), lambda p, i: (0, 5 * N_PAIRS + p)),
                  pl.BlockSpec((2, CA_TILES, ATT_BLOCK, ATT_BLOCK), lambda p, i: (p, 0, 0, 0)),
                  pl.BlockSpec((CA_ROWS, 128), lambda p, i: (i, p))],
        out_specs=[pl.BlockSpec((CA_ROWS, 128), lambda p, i: (i, p)),
                   pl.BlockSpec((s, 128), lambda p, i: (0, p)),
                   pl.BlockSpec((s, 128), lambda p, i: (0, p)),
                   pl.BlockSpec((2, CA_TILES, ATT_BLOCK, ATT_BLOCK), lambda p, i: (p, 0, 0, 0))],
        out_shape=[jax.ShapeDtypeStruct((s, ATT_WIDTH), BF16),
                   jax.ShapeDtypeStruct((s, ATT_WIDTH), F32),
                   jax.ShapeDtypeStruct((s, ATT_WIDTH), F32),
                   jax.ShapeDtypeStruct((2 * N_PAIRS, CA_TILES, ATT_BLOCK, ATT_BLOCK), F32)],
        compiler_params=_params(("arbitrary", "arbitrary")),
    )


def rel_bias_grad(db_tiles):
    heads = db_tiles.shape[0]

    def body(db_ref, out_ref):
        row = lax.broadcasted_iota(jnp.int32, (ATT_BLOCK, ATT_BLOCK), 0)
        col = lax.broadcasted_iota(jnp.int32, (ATT_BLOCK, ATT_BLOCK), 1)
        flip = (row + col == ATT_BLOCK - 1).astype(F32)
        wrapped = col < row
        cp = lax.broadcasted_iota(jnp.int32, (ATT_BLOCK, REL_PAD), 0)
        bins = lax.broadcasted_iota(jnp.int32, (ATT_BLOCK, REL_PAD), 1)
        total = jnp.zeros((heads, REL_PAD), F32)
        for j in range(CA_TILES):
            sums_neg, sums_pos = [], []
            for h in range(heads):
                tile = jnp.dot(db_ref[h, j], flip, preferred_element_type=F32, precision=lax.Precision.HIGHEST)
                tile = pltpu.roll(tile, 0, 1, stride=1, stride_axis=0)
                sums_neg.append(jnp.sum(jnp.where(wrapped, 0.0, tile), axis=0, keepdims=True))
                sums_pos.append(jnp.sum(jnp.where(wrapped, tile, 0.0), axis=0, keepdims=True))
            neg = jnp.concatenate(sums_neg, axis=0)
            pos = jnp.concatenate(sums_pos, axis=0)
            hot_neg = (jnp.clip(ATT_BLOCK * j + cp - (ATT_BLOCK - 1), -REL_CLIP, REL_CLIP) + REL_CLIP == bins)
            hot_pos = (jnp.clip(ATT_BLOCK * j + cp + 1, -REL_CLIP, REL_CLIP) + REL_CLIP == bins)
            total = total + jnp.dot(neg, hot_neg.astype(F32), preferred_element_type=F32,
                                    precision=lax.Precision.HIGHEST)
            total = total + jnp.dot(pos, hot_pos.astype(F32), preferred_element_type=F32,
                                    precision=lax.Precision.HIGHEST)
        out_ref[...] = total

    return pl.pallas_call(
        body, name="rel_bias_grad",
        out_shape=jax.ShapeDtypeStruct((heads, REL_PAD), F32),
        compiler_params=_params(),
    )(db_tiles)


def _assemble_cols(dst_ref, src_ref):
    w = src_ref.shape[2]
    for j in range(N_DEV):
        dst_ref[:, w * j:w * (j + 1)] = src_ref[j]


def _merge(o_sb, o_ca, lg, bg, wsb, wca):
    y_sb = _dot(o_sb.astype(BF16), wsb)
    y_ca = _dot(o_ca.astype(BF16), wca)
    gates = jax.nn.sigmoid(lg + bg)
    g_sb, g_ca = gates[:, :D_MODEL], gates[:, D_MODEL:]
    return y_sb, y_ca, g_sb, g_ca, g_sb * y_sb + g_ca * y_ca


def mix_fwd(x, o_sb, o_ca, lg, b_gate, wg_sb, wg_ca, w_out, tm=256):
    s = x.shape[0]

    def body(x_ref, osb_ref, oca_ref, lg_ref, bg_ref, wsb_ref, wca_ref, wout_ref, u_ref, wsb_s, wca_s):
        @pl.when(pl.program_id(0) == 0)
        def _():
            _assemble_cols(wsb_s, wsb_ref)
            _assemble_cols(wca_s, wca_ref)

        merged = _merge(osb_ref[...], oca_ref[...], lg_ref[...], bg_ref[...], wsb_s[...], wca_s[...])[4]
        u_ref[...] = ALPHA * x_ref[...] + _dot(merged.astype(BF16), wout_ref[...])

    rows = lambda w: pl.BlockSpec((tm, w), lambda i: (i, 0))
    whole = lambda shape: pl.BlockSpec(shape, lambda i: (0,) * len(shape))
    return pl.pallas_call(
        body, name="mix_fwd", grid=(s // tm,),
        in_specs=[rows(D_MODEL), rows(ATT_WIDTH), rows(ATT_WIDTH), rows(GATE_COLS), whole((1, GATE_COLS)),
                  whole((N_DEV, ATT_WIDTH, PROJ_SHARD)), whole((N_DEV, ATT_WIDTH, PROJ_SHARD)),
                  whole((D_MODEL, D_MODEL))],
        out_specs=rows(D_MODEL),
        out_shape=jax.ShapeDtypeStruct((s, D_MODEL), F32),
        scratch_shapes=[pltpu.VMEM((ATT_WIDTH, D_MODEL), BF16), pltpu.VMEM((ATT_WIDTH, D_MODEL), BF16)],
        compiler_params=_params(("arbitrary",)),
    )(x, o_sb, o_ca, lg, _row(b_gate), wg_sb, wg_ca, w_out)


def mix_bwd(du1, o_sb, o_ca, lg, b_gate, wg_sb, wg_ca, w_out, tm=256, guest=None):
    s = du1.shape[0]

    def body(du_ref, osb_ref, oca_ref, lg_ref, bg_ref, wsb_ref, wca_ref, wout_ref,
             dlg_ref, dosb_ref, doca_ref, dysb_ref, dyca_ref, mg_ref, dbg_ref, wsb_s, wca_s):
        @pl.when(pl.program_id(0) == 0)
        def _():
            _assemble_cols(wsb_s, wsb_ref)
            _assemble_cols(wca_s, wca_ref)
            dbg_ref[...] = jnp.zeros_like(dbg_ref)

        y_sb, y_ca, g_sb, g_ca, merged = _merge(
            osb_ref[...], oca_ref[...], lg_ref[...], bg_ref[...], wsb_s[...], wca_s[...])
        mg_ref[...] = merged.astype(BF16)
        dm = _dot_nt(du_ref[...].astype(BF16), wout_ref[...])
        dl_sb = dm * y_sb * g_sb * (1.0 - g_sb)
        dl_ca = dm * y_ca * g_ca * (1.0 - g_ca)
        dlg_ref[:, :D_MODEL] = dl_sb.astype(BF16)
        dlg_ref[:, D_MODEL:] = dl_ca.astype(BF16)
        dbg_ref[:, :D_MODEL] += jnp.sum(dl_sb, axis=0, keepdims=True)
        dbg_ref[:, D_MODEL:] += jnp.sum(dl_ca, axis=0, keepdims=True)
        dy_sb = (dm * g_sb).astype(BF16)
        dy_ca = (dm * g_ca).astype(BF16)
        dysb_ref[...] = dy_sb
        dyca_ref[...] = dy_ca
        dosb_ref[...] = _dot_nt(dy_sb, wsb_s[...]).astype(BF16)
        doca_ref[...] = _dot_nt(dy_ca, wca_s[...]).astype(BF16)

    rows = lambda w: pl.BlockSpec((tm, w), lambda i: (i, 0))
    whole = lambda shape: pl.BlockSpec(shape, lambda i: (0,) * len(shape))
    return _pallas(
        body, guest, (du1, o_sb, o_ca, lg, _row(b_gate), wg_sb, wg_ca, w_out), name="mix_bwd", grid=(s // tm,),
        in_specs=[rows(D_MODEL), rows(ATT_WIDTH), rows(ATT_WIDTH), rows(GATE_COLS), whole((1, GATE_COLS)),
                  whole((N_DEV, ATT_WIDTH, PROJ_SHARD)), whole((N_DEV, ATT_WIDTH, PROJ_SHARD)),
                  whole((D_MODEL, D_MODEL))],
        out_specs=[rows(GATE_COLS), rows(ATT_WIDTH), rows(ATT_WIDTH), rows(D_MODEL), rows(D_MODEL),
                   rows(D_MODEL), whole((1, GATE_COLS))],
        out_shape=[jax.ShapeDtypeStruct((s, GATE_COLS), BF16),
                   jax.ShapeDtypeStruct((s, ATT_WIDTH), BF16), jax.ShapeDtypeStruct((s, ATT_WIDTH), BF16),
                   jax.ShapeDtypeStruct((s, D_MODEL), BF16), jax.ShapeDtypeStruct((s, D_MODEL), BF16),
                   jax.ShapeDtypeStruct((s, D_MODEL), BF16), jax.ShapeDtypeStruct((1, GATE_COLS), F32)],
        scratch_shapes=[pltpu.VMEM((ATT_WIDTH, D_MODEL), BF16), pltpu.VMEM((ATT_WIDTH, D_MODEL), BF16)],
        compiler_params=_params(("arbitrary",)),
    )


def _ln_stats(u):
    mu = jnp.mean(u, axis=1, keepdims=True)
    cen = u - mu
    var = jnp.mean(cen * cen, axis=1, keepdims=True)
    rstd = lax.rsqrt(var + LN_EPS)
    return cen * rstd, rstd


def _ln_bwd(dy, xhat, rstd, gain):
    dyg = dy * gain
    m1 = jnp.mean(dyg, axis=1, keepdims=True)
    m2 = jnp.mean(dyg * xhat, axis=1, keepdims=True)
    return rstd * (dyg - m1 - xhat * m2)


def _load_mlp_weights(win_hbm, wout_hbm, win_s, wout_s, sems):
    copies = [pltpu.make_async_copy(win_hbm.at[k], win_s.at[:, pl.ds(FF_SHARD * k, FF_SHARD)], sems.at[k])
              for k in range(N_DEV)]
    copies.append(pltpu.make_async_copy(wout_hbm, wout_s, sems.at[N_DEV]))
    for cp in copies:
        cp.start()
    return copies


MLP_WEIGHT_SCRATCH = [pltpu.VMEM((D_MODEL, D_FF), BF16), pltpu.VMEM((D_FF, D_MODEL), BF16),
                      pltpu.SemaphoreType.DMA((N_DEV + 1,))]


def mlp_fwd(u1, target, wg_in, w_out, ln1_g, ln1_b, ln2_g, ln2_b, tm=256):
    s = u1.shape[0]

    def body(u_ref, t_ref, win_hbm, wout_hbm, g1_ref, b1_ref, g2_ref, b2_ref,
             hm_ref, a_ref, x1b_ref, du2_ref, loss_ref, dg2_ref, db2_ref, win_s, wout_s, sems):
        first = pl.program_id(0) == 0

        @pl.when(first)
        def _():
            copies = _load_mlp_weights(win_hbm, wout_hbm, win_s, wout_s, sems)
            loss_ref[...] = jnp.zeros_like(loss_ref)
            dg2_ref[...] = jnp.zeros_like(dg2_ref)
            db2_ref[...] = jnp.zeros_like(db2_ref)
            for cp in copies:
                cp.wait()

        xhat, _ = _ln_stats(u_ref[...])
        x1 = xhat * g1_ref[...] + b1_ref[...]
        x1b = x1.astype(BF16)
        x1b_ref[...] = x1b
        for k in range(N_DEV):
            cols = slice(FF_SHARD * k, FF_SHARD * (k + 1))
            hm = _dot(x1b, win_s[:, cols])
            hm_ref[:, cols] = hm
            r = jnp.maximum(hm, 0.0)
            a_ref[:, cols] = (r * r).astype(BF16)
        xhat, rstd = _ln_stats(ALPHA * x1 + _dot(a_ref[...], wout_s[...]))
        diff = xhat * g2_ref[...] + b2_ref[...] - t_ref[...]
        per_token = jnp.mean(diff * diff, axis=1, keepdims=True)
        loss_ref[...] += 0.5 * jnp.sum(per_token, axis=0, keepdims=True)
        dy = diff * (1.0 / D_MODEL)
        dg2_ref[...] += jnp.sum(dy * xhat, axis=0, keepdims=True)
        db2_ref[...] += jnp.sum(dy, axis=0, keepdims=True)
        du2_ref[...] = _ln_bwd(dy, xhat, rstd, g2_ref[...])

    rows = pl.BlockSpec((tm, D_MODEL), lambda i: (i, 0))
    vec = pl.BlockSpec((1, D_MODEL), lambda i: (0, 0))
    ff = pl.BlockSpec((tm, D_FF), lambda i: (i, 0))
    return pl.pallas_call(
        body, name="mlp_fwd", grid=(s // tm,),
        in_specs=[rows, rows, HBM_SPEC, HBM_SPEC, vec, vec, vec, vec],
        out_specs=[ff, ff, rows, rows, pl.BlockSpec((1, 1), lambda i: (0, 0)), vec, vec],
        out_shape=[jax.ShapeDtypeStruct((s, D_FF), F32), jax.ShapeDtypeStruct((s, D_FF), BF16),
                   jax.ShapeDtypeStruct((s, D_MODEL), BF16), jax.ShapeDtypeStruct((s, D_MODEL), F32),
                   jax.ShapeDtypeStruct((1, 1), F32),
                   jax.ShapeDtypeStruct((1, D_MODEL), F32), jax.ShapeDtypeStruct((1, D_MODEL), F32)],
        scratch_shapes=MLP_WEIGHT_SCRATCH,
        compiler_params=_params(("arbitrary",)),
    )(u1, target, wg_in, w_out, _row(ln1_g), _row(ln1_b), _row(ln2_g), _row(ln2_b))


def mlp_bwd(du2, hm, u1, wg_in, w_out, ln1_g, tm=256):
    s = du2.shape[0]

    def body(du2_ref, hm_ref, u_ref, win_hbm, wout_hbm, g1_ref,
             dhm_ref, du1_ref, dg1_ref, db1_ref, win_s, wout_s, sems):
        first = pl.program_id(0) == 0

        @pl.when(first)
        def _():
            copies = _load_mlp_weights(win_hbm, wout_hbm, win_s, wout_s, sems)
            dg1_ref[...] = jnp.zeros_like(dg1_ref)
            db1_ref[...] = jnp.zeros_like(db1_ref)
            for cp in copies:
                cp.wait()

        du2 = du2_ref[...]
        du2b = du2.astype(BF16)
        for k in range(N_DEV):
            cols = slice(FF_SHARD * k, FF_SHARD * (k + 1))
            da = _dot_nt(du2b, wout_s[cols, :])
            dhm_ref[:, cols] = (2.0 * jnp.maximum(hm_ref[:, cols], 0.0) * da).astype(BF16)
        dx1 = ALPHA * du2 + _dot_nt(dhm_ref[...], win_s[...])
        xhat, rstd = _ln_stats(u_ref[...])
        dg1_ref[...] += jnp.sum(dx1 * xhat, axis=0, keepdims=True)
        db1_ref[...] += jnp.sum(dx1, axis=0, keepdims=True)
        du1_ref[...] = _ln_bwd(dx1, xhat, rstd, g1_ref[...])

    rows = pl.BlockSpec((tm, D_MODEL), lambda i: (i, 0))
    vec = pl.BlockSpec((1, D_MODEL), lambda i: (0, 0))
    ff = pl.BlockSpec((tm, D_FF), lambda i: (i, 0))
    return pl.pallas_call(
        body, name="mlp_bwd", grid=(s // tm,),
        in_specs=[rows, ff, rows, HBM_SPEC, HBM_SPEC, vec],
        out_specs=[ff, rows, vec, vec],
        out_shape=[jax.ShapeDtypeStruct((s, D_FF), BF16), jax.ShapeDtypeStruct((s, D_MODEL), F32),
                   jax.ShapeDtypeStruct((1, D_MODEL), F32), jax.ShapeDtypeStruct((1, D_MODEL), F32)],
        scratch_shapes=MLP_WEIGHT_SCRATCH,
        compiler_params=_params(("arbitrary",)),
    )(du2, hm, u1, wg_in, w_out, _row(ln1_g))


def dh_slab(pieces, tm=512):
    s = pieces[0].shape[0]
    widths = [p.shape[1] for p in pieces]
    assert sum(widths) == IN_COLS
    n = len(pieces)

    def body(*refs):
        dh_ref = refs[n]
        off = 0
        for ref, w in zip(refs[:n], widths):
            dh_ref[:, off:off + w] = ref[...].astype(BF16)
            off += w

    rows = lambda w: pl.BlockSpec((tm, w), lambda i: (i, 0))
    return pl.pallas_call(
        body, name="dh_slab", grid=(s // tm,),
        in_specs=[rows(w) for w in widths], out_specs=rows(IN_COLS),
        out_shape=jax.ShapeDtypeStruct((s, IN_COLS), BF16),
        compiler_params=_params(("arbitrary",)),
    )(*pieces)


def in_bwd(du1, dh, wg_in, tm=512, guest=None):
    s = du1.shape[0]

    def body(du_ref, dh_ref, w_ref, gx_ref):
        acc = ALPHA * du_ref[...]
        for j in range(N_DEV):
            acc = acc + _dot_nt(dh_ref[:, IN_SHARD * j:IN_SHARD * (j + 1)], w_ref[j])
        gx_ref[...] = acc

    rows = lambda w: pl.BlockSpec((tm, w), lambda i: (i, 0))
    return _pallas(
        body, guest, (du1, dh, wg_in), name="in_bwd", grid=(s // tm,),
        in_specs=[rows(D_MODEL), rows(IN_COLS), pl.BlockSpec((N_DEV, D_MODEL, IN_SHARD), lambda i: (0, 0, 0))],
        out_specs=[rows(D_MODEL)],
        out_shape=[jax.ShapeDtypeStruct((s, D_MODEL), F32)],
        compiler_params=_params(("arbitrary",)),
    )


def wgrad(name, a, b, tm, tn, shard_cols=None, tk=2048):
    kk, m = a.shape
    n = b.shape[1]
    while kk % tk:
        tk //= 2
    nk = kk // tk
    w = shard_cols

    def body(a_ref, b_ref, out_ref, acc_s):
        k = pl.program_id(2)

        @pl.when(k == 0)
        def _():
            acc_s[...] = jnp.zeros_like(acc_s)

        acc_s[...] += _dot_tn(a_ref[...].astype(BF16), b_ref[...].astype(BF16))

        @pl.when(k == nk - 1)
        def _():
            if w is None:
                out_ref[...] = acc_s[...]
            else:
                for j in range(tn // w):
                    out_ref[j] = acc_s[:, w * j:w * (j + 1)]

    if w is None:
        out_spec = pl.BlockSpec((tm, tn), lambda i, j, k: (i, j))
        out_shape = jax.ShapeDtypeStruct((m, n), F32)
    else:
        out_spec = pl.BlockSpec((tn // w, tm, w), lambda i, j, k: (j, i, 0))
        out_shape = jax.ShapeDtypeStruct((n // w, m, w), F32)
    return pl.pallas_call(
        body, name=name, grid=(m // tm, n // tn, nk),
        in_specs=[pl.BlockSpec((tk, tm), lambda i, j, k: (k, i)),
                  pl.BlockSpec((tk, tn), lambda i, j, k: (k, j))],
        out_specs=out_spec, out_shape=out_shape,
        scratch_shapes=[pltpu.VMEM((tm, tn), F32)],
        compiler_params=_params(("arbitrary", "arbitrary", "arbitrary")),
    )(a, b)


def adamw(name, w, m, v, parts, tm=None, guest=None):
    rows, cols = w.shape
    tm = rows if tm is None else min(tm, rows)
    n = len(parts)
    c1 = 1.0 - ADAM_B1 ** ADAM_STEP
    c2 = 1.0 - ADAM_B2 ** ADAM_STEP

    def body(*refs):
        w_ref, m_ref, v_ref = refs[:3]
        part_refs = refs[3:3 + n]
        g_ref, d_ref, nm_ref, nv_ref = refs[3 + n:]
        g = None
        for ref, (_, index) in zip(part_refs, parts):
            term = (ref[...] if index is None else ref[0]).astype(F32)
            g = term if g is None else g + term
        new_m = ADAM_B1 * m_ref[...] + (1.0 - ADAM_B1) * g
        new_v = ADAM_B2 * v_ref[...] + (1.0 - ADAM_B2) * (g * g)
        m_hat = new_m / c1
        v_hat = new_v / c2
        g_ref[...] = g
        d_ref[...] = -ADAM_LR * (m_hat / (jnp.sqrt(v_hat) + ADAM_EPS) + ADAM_WD * w_ref[...])
        nm_ref[...] = new_m
        nv_ref[...] = new_v

    spec = pl.BlockSpec((tm, cols), lambda i: (i, 0))
    shape = jax.ShapeDtypeStruct((rows, cols), F32)

    def part_spec(index):
        if index is None:
            return spec
        return pl.BlockSpec((1, tm, cols), lambda i: (index, i, 0))

    return _pallas(
        body, guest, (w, m, v, *[array for array, _ in parts]), name=name, grid=(rows // tm,),
        in_specs=[spec] * 3 + [part_spec(index) for _, index in parts],
        out_specs=[spec] * 4, out_shape=[shape] * 4,
        compiler_params=_params(("arbitrary",)),
    )


def _place():
    x, y, c = lax.axis_index("x"), lax.axis_index("y"), lax.axis_index("c")
    other_chips = [(1 - x, y), (x, 1 - y), (1 - x, 1 - y)]
    return x, y, c, other_chips


def place_shards(arrays):
    n = len(arrays)

    def body(*refs):
        ins, outs, stage, sems = refs[:n], refs[n:2 * n], refs[2 * n:3 * n], refs[3 * n]
        x, y, c, _ = _place()
        copies = []
        for a in range(n):
            stage[a][...] = ins[a][...].astype(BF16)
            copies.append(pltpu.make_async_copy(stage[a], outs[a].at[4 * x + 2 * y + c], sems.at[a]))
            copies[-1].start()
        for cp in copies:
            cp.wait()

    return pl.pallas_call(
        body, name="place_shards",
        in_specs=[pl.BlockSpec(memory_space=pltpu.VMEM)] * n, out_specs=[HBM_SPEC] * n,
        out_shape=[jax.ShapeDtypeStruct((N_DEV,) + a.shape, BF16) for a in arrays],
        scratch_shapes=[pltpu.VMEM(a.shape, BF16) for a in arrays] + [pltpu.SemaphoreType.DMA((n,))],
        compiler_params=_params(),
    )(*arrays)


def all_gather(name, arrays):
    n = len(arrays)

    def body(*refs):
        ins, outs = refs[:n], refs[n:2 * n]
        send_sems, recv_sems, local_sems = refs[2 * n:]
        x, y, c, chips = _place()
        me = 4 * x + 2 * y + c
        sibling = (x, y, 1 - c)

        def copy(a, k, block, to, src=None):
            return pltpu.make_async_remote_copy(
                src_ref=outs[a].at[block] if src is None else src, dst_ref=outs[a].at[block],
                send_sem=send_sems.at[a, k], recv_sem=recv_sems.at[a, k],
                device_id=to, device_id_type=MESH)

        started = []
        local = [pltpu.make_async_copy(ins[a], outs[a].at[me], local_sems.at[a]) for a in range(n)]
        for a in range(n):
            local[a].start()
            sends = [copy(a, 0, me, sibling, src=ins[a])]
            sends += [copy(a, 1 + j, me, (cx, cy, c), src=ins[a]) for j, (cx, cy) in enumerate(chips)]
            for cp in sends:
                cp.start()
            started += sends
        for a in range(n):
            for j, (cx, cy) in enumerate(chips):
                block = 4 * cx + 2 * cy + c
                copy(a, 1 + j, block, sibling).wait_recv()
                passed = copy(a, 4 + j, block, sibling)
                passed.start()
                started.append(passed)
        for a in range(n):
            copy(a, 0, 4 * x + 2 * y + (1 - c), sibling).wait_recv()
            for j, (cx, cy) in enumerate(chips):
                copy(a, 4 + j, 4 * cx + 2 * cy + (1 - c), sibling).wait_recv()
        for cp in started:
            cp.wait_send()
        for cp in local:
            cp.wait()

    return pl.pallas_call(
        body, name=name,
        in_specs=[pl.BlockSpec(memory_space=pltpu.VMEM)] * n, out_specs=[HBM_SPEC] * n,
        out_shape=[jax.ShapeDtypeStruct((N_DEV,) + a.shape, a.dtype) for a in arrays],
        scratch_shapes=[pltpu.SemaphoreType.DMA((n, 7)), pltpu.SemaphoreType.DMA((n, 7)),
                        pltpu.SemaphoreType.DMA((n,))],
    )(*arrays)


def _gather_first_plan(n):
    def plan(zones):
        x, y, c, chips = _place()
        me = 4 * x + 2 * y + c
        copies = []
        for zone in zones[:n]:
            copies.append((zone.at[me], zone.at[me], (x, y, 1 - c), zone.at[4 * x + 2 * y + (1 - c)]))
            copies += [(zone.at[me], zone.at[me], (cx, cy, c), zone.at[4 * cx + 2 * cy + c]) for cx, cy in chips]
        return copies
    return plan


def _gather_second_plan(n):
    def plan(zones):
        x, y, c, chips = _place()
        return [(zone.at[4 * cx + 2 * cy + c], zone.at[4 * cx + 2 * cy + c], (x, y, 1 - c),
                 zone.at[4 * cx + 2 * cy + (1 - c)]) for zone in zones[:n] for cx, cy in chips]
    return plan


def _both_plans(first, n_first, second):
    return lambda bufs: first(bufs[:n_first]) + second(bufs[n_first:])


def _sibling_plan(n):
    def plan(bufs):
        x, y, c, _ = _place()
        return [(bufs[a].at[:, 1 - c], bufs[n + a], (x, y, 1 - c), bufs[n + a]) for a in range(n)]
    return plan


def _chips_plan(n):
    def plan(bufs):
        _, _, c, chips = _place()
        return [(bufs[a].at[2 * cx + cy], bufs[n + a].at[j], (cx, cy, c), bufs[n + a].at[j])
                for a in range(n) for j, (cx, cy) in enumerate(chips)]
    return plan


def chip_sum(name, grad, landed, place, tr=256):
    _, _, rows, cols = grad.shape
    tr = min(tr, rows)

    def body(place_ref, g_ref, l_ref, own_ref, out_ref):
        total = g_ref[0, 0] + l_ref[0]
        out_ref[0] = total.astype(BF16)

        @pl.when(pl.program_id(1) == place_ref[1])
        def _():
            own_ref[...] = total

    return pl.pallas_call(
        body, name=name,
        grid_spec=pltpu.PrefetchScalarGridSpec(
            num_scalar_prefetch=1, grid=(rows // tr, 4),
            in_specs=[pl.BlockSpec((1, 1, tr, cols), lambda i, ch, pos: (ch, pos[0], i, 0)),
                      pl.BlockSpec((1, tr, cols), lambda i, ch, pos: (ch, i, 0))],
            out_specs=[pl.BlockSpec((tr, cols), lambda i, ch, pos: (i, 0)),
                       pl.BlockSpec((1, tr, cols), lambda i, ch, pos: (ch, i, 0))]),
        out_shape=[jax.ShapeDtypeStruct((rows, cols), F32), jax.ShapeDtypeStruct((4, rows, cols), BF16)],
        compiler_params=_params(("arbitrary", "arbitrary")),
    )(place, grad, landed)


BIG =("w_in", "w_sb_proj", "w_ca_proj", "w_out", "w_mlp_in", "w_mlp_out")
SMALL = ("b_gate", "rel_bias", "ln1_g", "ln1_b", "ln2_g", "ln2_b")
NAMES = ("w_in", "b_gate", "w_sb_proj", "w_ca_proj", "rel_bias", "w_out", "ln1_g", "ln1_b",
         "w_mlp_in", "w_mlp_out", "ln2_g", "ln2_b")
REL_COLS = 2 * REL_CLIP + 1


def _pack_small(t, scalar=None):
    rel = t["rel_bias"]
    if rel.shape[1] != REL_PAD:
        rel = jnp.pad(rel, ((0, 0), (0, REL_PAD - rel.shape[1])))
    last = jnp.zeros((128,), F32) if scalar is None else jnp.pad(scalar.reshape(1), (0, 127))
    flat = [t["b_gate"].reshape(-1), rel.reshape(-1)] + [t[n].reshape(-1) for n in SMALL[2:]] + [last]
    return jnp.concatenate(flat).reshape(-1, 128)


def _unpack_small(p):
    flat = p.reshape(-1)
    out, off = {}, 0
    for n, size in (("b_gate", GATE_COLS), ("rel_bias", 8 * REL_PAD)) + tuple((n, D_MODEL) for n in SMALL[2:]):
        out[n] = flat[off:off + size]
        off += size
    out["rel_bias"] = out["rel_bias"].reshape(8, REL_PAD)[:, :REL_COLS]
    return out


def kernel(x, w_in, b_gate, w_sb_proj, w_ca_proj, rel_bias, w_out, ln1_g, ln1_b, w_mlp_in, w_mlp_out, ln2_g, ln2_b, loss_target, m_w_in, m_b_gate, m_w_sb_proj, m_w_ca_proj, m_rel_bias, m_w_out, m_ln1_g, m_ln1_b, m_w_mlp_in, m_w_mlp_out, m_ln2_g, m_ln2_b, v_w_in, v_b_gate, v_w_sb_proj, v_w_ca_proj, v_rel_bias, v_w_out, v_ln1_g, v_ln1_b, v_w_mlp_in, v_w_mlp_out, v_ln2_g, v_ln2_b):
    w = dict(w_in=w_in, b_gate=b_gate, w_sb_proj=w_sb_proj, w_ca_proj=w_ca_proj, rel_bias=rel_bias, w_out=w_out,
             ln1_g=ln1_g, ln1_b=ln1_b, w_mlp_in=w_mlp_in, w_mlp_out=w_mlp_out, ln2_g=ln2_g, ln2_b=ln2_b)
    m = dict(w_in=m_w_in, b_gate=m_b_gate, w_sb_proj=m_w_sb_proj, w_ca_proj=m_w_ca_proj, rel_bias=m_rel_bias,
             w_out=m_w_out, ln1_g=m_ln1_g, ln1_b=m_ln1_b, w_mlp_in=m_w_mlp_in, w_mlp_out=m_w_mlp_out,
             ln2_g=m_ln2_g, ln2_b=m_ln2_b)
    v = dict(w_in=v_w_in, b_gate=v_b_gate, w_sb_proj=v_w_sb_proj, w_ca_proj=v_w_ca_proj, rel_bias=v_rel_bias,
             w_out=v_w_out, ln1_g=v_ln1_g, ln1_b=v_ln1_b, w_mlp_in=v_w_mlp_in, w_mlp_out=v_w_mlp_out,
             ln2_g=v_ln2_g, ln2_b=v_ln2_b)

    x, target = x[0], loss_target[0]
    c = lax.axis_index("c")
    place = jnp.stack([c, 2 * lax.axis_index("x") + lax.axis_index("y")]).astype(jnp.int32)
    grads, deltas, new_m, new_v = {}, {}, {}, {}

    def by_owner(n, g):
        return g.reshape((4, 2) + w[n].shape)

    def finish(n, own, arrived, guest=None):
        out = adamw("adamw_" + n, w[n], m[n], v[n], [(own, None), (arrived, 0), (arrived, 1), (arrived, 2)],
                    tm=256, guest=guest)
        out, bufs = (out, None) if guest is None else out
        grads[n], deltas[n], new_m[n], new_v[n] = out
        return bufs

    def siblings(names, g):
        bufs = [by_owner(n, g[n]) for n in names] + [lax.empty((4,) + w[n].shape, F32) for n in names]
        return Exchange(tuple(bufs), _sibling_plan(len(names)), len(names))

    def chip_sums(names, bufs):
        k = len(names)
        own, sums = zip(*[chip_sum("chip_sum_" + n, bufs[a], bufs[k + a], place) for a, n in enumerate(names)])
        zones = [lax.empty((3,) + w[n].shape, BF16) for n in names]
        return own, Exchange(tuple(sums) + tuple(zones), _chips_plan(k), 3 * k)

    proj, ffn = ("w_sb_proj", "w_ca_proj", "w_out"), ("w_mlp_in", "w_mlp_out")
    zones = place_shards([w[n] for n in proj + ffn])
    proj_zones, ffn_zones = tuple(zones[:len(proj)]), tuple(zones[len(proj):])
    wg_in = all_gather("gather_w_in", [w_in.astype(BF16)])[0]
    (qkv, lg, xb), proj_zones = in_proj(
        x, wg_in, guest=Exchange(proj_zones, _gather_first_plan(len(proj)), 4 * len(proj)))
    bias_tiles = ca_bias_tiles(rel_bias)
    both = Exchange(ffn_zones + tuple(proj_zones),
                    _both_plans(_gather_first_plan(len(ffn)), len(ffn), _gather_second_plan(len(proj))),
                    4 * len(ffn) + 3 * len(proj))
    (o_sb,), zones = sb_fwd(qkv, both)
    ffn_zones, proj_zones = tuple(zones[:len(ffn)]), zones[len(ffn):]
    (o_ca,), ffn_zones = ca_fwd(qkv, bias_tiles, Exchange(ffn_zones, _gather_second_plan(len(ffn)), 3 * len(ffn)))
    wg = dict(zip(proj + ffn, list(proj_zones) + list(ffn_zones)))
    wg_sb, wg_ca, wg_mi = wg["w_sb_proj"], wg["w_ca_proj"], wg["w_mlp_in"]
    wg_mo = wg["w_mlp_out"].reshape(D_FF, D_MODEL)
    w_out_b = wg["w_out"].reshape(D_MODEL, D_MODEL)

    u1 = mix_fwd(x, o_sb, o_ca, lg, b_gate, wg_sb, wg_ca, w_out_b)
    hm, a, x1b, du2, loss, dg2, db2 = mlp_fwd(u1, target, wg_mi, wg_mo, ln1_g, ln1_b, ln2_g, ln2_b)
    dhm, du1, dg1, db1 = mlp_bwd(du2, hm, u1, wg_mi, wg_mo, ln1_g)
    g = {}
    g["w_mlp_in"] = wgrad("wgrad_mlp_in", x1b, dhm, tm=D_MODEL, tn=FF_SHARD, shard_cols=FF_SHARD)
    g["w_mlp_out"] = wgrad("wgrad_mlp_out", a, du2, tm=FF_SHARD, tn=D_MODEL)

    mlp = ("w_mlp_in", "w_mlp_out")
    (dlg, do_sb, do_ca, dy_sb, dy_ca, merged, dbg), bufs = mix_bwd(
        du1, o_sb, o_ca, lg, b_gate, wg_sb, wg_ca, w_out_b, guest=siblings(mlp, g))
    own_mlp, mlp_chips = chip_sums(mlp, bufs)
    g["w_out"] = wgrad("wgrad_out", merged, du1, tm=D_MODEL, tn=D_MODEL)
    g["w_sb_proj"] = wgrad("wgrad_sb_proj", o_sb, dy_sb, tm=ATT_WIDTH, tn=D_MODEL, shard_cols=PROJ_SHARD)
    g["w_ca_proj"] = wgrad("wgrad_ca_proj", o_ca, dy_ca, tm=ATT_WIDTH, tn=D_MODEL, shard_cols=PROJ_SHARD)

    att = ("w_out", "w_sb_proj", "w_ca_proj")
    att_siblings = siblings(att, g)
    both = Exchange(mlp_chips.bufs + att_siblings.bufs,
                    _both_plans(mlp_chips.plan, len(mlp_chips.bufs), att_siblings.plan),
                    mlp_chips.copies + att_siblings.copies)
    (dq_sb, dk_sb, dv_sb), bufs = sb_bwd(qkv, o_sb, do_sb, guest=both)
    arrived_mlp, bufs = bufs[len(mlp):len(mlp_chips.bufs)], bufs[len(mlp_chips.bufs):]
    own_att, att_chips = chip_sums(att, bufs)
    (dq_ca, dk_ca, dv_ca, db_tiles), bufs = ca_bwd(qkv, bias_tiles, do_ca, guest=att_chips)
    arrived_att = bufs[len(att):]
    g_rel = rel_bias_grad(db_tiles)

    dh = dh_slab([dq_sb, dk_sb, dv_sb, dq_ca, dk_ca, dv_ca, dlg])
    g["w_in"] = wgrad("wgrad_in", xb, dh, tm=D_MODEL, tn=IN_SHARD, shard_cols=IN_SHARD)
    last = ("w_in",)
    bufs = finish(mlp[0], own_mlp[0], arrived_mlp[0], guest=siblings(last, g))
    finish(mlp[1], own_mlp[1], arrived_mlp[1])
    for n, own, arrived in zip(att, own_att, arrived_att):
        finish(n, own, arrived)
    own_in, in_chips = chip_sums(last, bufs)
    (grad_x,), bufs = in_bwd(du1, dh, wg_in, guest=in_chips)
    finish(last[0], own_in[0], bufs[1])

    small = dict(b_gate=dbg, rel_bias=g_rel, ln1_g=dg1, ln1_b=db1, ln2_g=dg2, ln2_b=db2)
    parts = all_gather("gather_small_grads", [_pack_small(small, loss[0, 0])])[0]
    packed = adamw("adamw_small", _pack_small(w), _pack_small(m), _pack_small(v),
                   [(parts, d) for d in range(N_DEV)])
    loss = packed[0][-1, 0]
    for out, p in zip((grads, deltas, new_m, new_v), packed):
        for n, val in _unpack_small(p).items():
            out[n] = val.reshape(w[n].shape)

    return (loss, grad_x[None], *[grads[n] for n in NAMES], *[deltas[n] for n in NAMES],
            *[new_m[n] for n in NAMES], *[new_v[n] for n in NAMES])
```

```python
import functools
from typing import NamedTuple

import jax
import jax.numpy as jnp
from jax import lax
from jax.experimental import pallas as pl
from jax.experimental.pallas import tpu as pltpu

F32 = jnp.float32
BF16 = jnp.bfloat16
MESH = pl.DeviceIdType.MESH

N_DEV = 8
D_MODEL = 1024
HEAD_DIM = 64
ATT_WIDTH = 512
N_PAIRS = ATT_WIDTH // 128
QKV_COLS = 6 * ATT_WIDTH
GATE_COLS = 2 * D_MODEL
IN_COLS = QKV_COLS + GATE_COLS
IN_SHARD = IN_COLS // N_DEV
D_FF = 4 * D_MODEL
FF_SHARD = D_FF // N_DEV
PROJ_SHARD = D_MODEL // N_DEV
ATT_BLOCK = 128
CA_TILES = 5
CA_ROWS = 1024
CHUNK = 64
CA_PREV_CHUNKS = 8
REL_CLIP = 256
REL_PAD = 640
ALPHA = 2.0 ** 0.25
LN_EPS = 1e-5
QK_SCALE = HEAD_DIM ** -0.5
NEG_BIG = -1e30
VMEM_LIMIT = 56 * 1024 * 1024

ADAM_LR = 0.001
ADAM_B1 = 0.9
ADAM_B2 = 0.999
ADAM_EPS = 1e-08
ADAM_WD = 0.01
ADAM_STEP = 10

_NT = (((1,), (1,)), ((), ()))
_TN = (((0,), (0,)), ((), ()))


def _dot(a, b):
    return jnp.dot(a, b, preferred_element_type=F32)


def _dot_nt(a, b):
    return lax.dot_general(a, b, _NT, preferred_element_type=F32)


def _dot_tn(a, b):
    return lax.dot_general(a, b, _TN, preferred_element_type=F32)


def _params(semantics=None):
    return pltpu.CompilerParams(dimension_semantics=semantics, vmem_limit_bytes=VMEM_LIMIT)


def _row(v):
    return v.reshape(1, -1)


HBM_SPEC = pl.BlockSpec(memory_space=pl.ANY)


class Exchange(NamedTuple):
    bufs: tuple
    plan: object
    copies: int


def _pallas(body, guest, args, *, grid, in_specs, out_specs, out_shape, scratch_shapes=(), **kw):
    out_specs, out_shape, scratch_shapes = list(out_specs), list(out_shape), list(scratch_shapes)
    if guest is None:
        return pl.pallas_call(body, grid=grid, in_specs=in_specs, out_specs=out_specs, out_shape=out_shape,
                              scratch_shapes=scratch_shapes, **kw)(*args)
    nb, n_own = len(guest.bufs), len(in_specs) + len(out_specs)

    def hosting(*refs):
        zones = refs[nb + n_own:2 * nb + n_own]
        send_sems, recv_sems = refs[-2], refs[-1]
        steps = [pl.program_id(axis) for axis in range(len(grid))]
        first = functools.reduce(jnp.logical_and, [step == 0 for step in steps])
        last = functools.reduce(jnp.logical_and, [step == n - 1 for step, n in zip(steps, grid)])

        def copy(k, src, dst, peer):
            return pltpu.make_async_remote_copy(src_ref=src, dst_ref=dst, send_sem=send_sems.at[k],
                                                recv_sem=recv_sems.at[k], device_id=peer, device_id_type=MESH)

        @pl.when(first)
        def _():
            for k, (src, dst, peer, _) in enumerate(guest.plan(zones)):
                copy(k, src, dst, peer).start()

        body(*refs[nb:nb + n_own], *refs[2 * nb + n_own:-2])

        @pl.when(last)
        def _():
            for k, (src, dst, peer, landed) in enumerate(guest.plan(zones)):
                copy(k, src, dst, peer).wait_send()
                copy(k, src, landed, peer).wait_recv()

    sems = pltpu.SemaphoreType.DMA((guest.copies,))
    results = pl.pallas_call(
        hosting, grid=grid, in_specs=[HBM_SPEC] * nb + list(in_specs), out_specs=out_specs + [HBM_SPEC] * nb,
        out_shape=out_shape + [jax.ShapeDtypeStruct(b.shape, b.dtype) for b in guest.bufs],
        input_output_aliases={k: len(out_specs) + k for k in range(nb)},
        scratch_shapes=scratch_shapes + [sems, sems], **kw)(*guest.bufs, *args)
    return results[:len(out_specs)], results[len(out_specs):]


IN_PAIR = 2 * IN_SHARD
IN_PAIR_SCRATCH = [pltpu.VMEM((N_DEV // 2, D_MODEL, IN_PAIR), BF16), pltpu.SemaphoreType.DMA((N_DEV,))]


def _load_in_pairs(w_hbm, pairs_s, sems):
    copies = [pltpu.make_async_copy(w_hbm.at[j], pairs_s.at[j // 2, :, pl.ds(IN_SHARD * (j % 2), IN_SHARD)],
                                    sems.at[j]) for j in range(N_DEV)]
    for cp in copies:
        cp.start()
    for cp in copies:
        cp.wait()


def in_proj(x, wg_in, tm=512, guest=None):
    s = x.shape[0]

    def body(x_ref, w_hbm, qkv_ref, lg_ref, xb_ref, pairs_s, sems):
        @pl.when(pl.program_id(0) == 0)
        def _():
            _load_in_pairs(w_hbm, pairs_s, sems)

        xb = x_ref[...].astype(BF16)
        xb_ref[...] = xb
        for j in range(N_DEV // 2):
            acc = _dot(xb, pairs_s[j])
            lo, hi = IN_PAIR * j, IN_PAIR * (j + 1)
            if hi <= QKV_COLS:
                qkv_ref[:, lo:hi] = acc.astype(BF16)
            elif lo >= QKV_COLS:
                lg_ref[:, lo - QKV_COLS:hi - QKV_COLS] = acc
            else:
                qkv_ref[:, lo:QKV_COLS] = acc[:, :QKV_COLS - lo].astype(BF16)
                lg_ref[:, 0:hi - QKV_COLS] = acc[:, QKV_COLS - lo:]

    return _pallas(
        body, guest, (x, wg_in), name="in_proj", grid=(s // tm,), scratch_shapes=IN_PAIR_SCRATCH,
        in_specs=[pl.BlockSpec((tm, D_MODEL), lambda i: (i, 0)), HBM_SPEC],
        out_specs=[pl.BlockSpec((tm, QKV_COLS), lambda i: (i, 0)),
                   pl.BlockSpec((tm, GATE_COLS), lambda i: (i, 0)),
                   pl.BlockSpec((tm, D_MODEL), lambda i: (i, 0))],
        out_shape=[jax.ShapeDtypeStruct((s, QKV_COLS), BF16),
                   jax.ShapeDtypeStruct((s, GATE_COLS), F32),
                   jax.ShapeDtypeStruct((s, D_MODEL), BF16)],
        compiler_params=_params(("arbitrary",)),
    )


def _head_masks():
    lane = lax.broadcasted_iota(jnp.int32, (1, 128), 1)
    first = lane < HEAD_DIM
    return first, jnp.logical_not(first)


SB_CHUNK = 512
SB_SUB = SB_CHUNK // ATT_BLOCK


def _tri_pair():
    row = lax.broadcasted_iota(jnp.int32, (4 * ATT_BLOCK, 2 * ATT_BLOCK), 0) & (2 * ATT_BLOCK - 1)
    col = lax.broadcasted_iota(jnp.int32, (4 * ATT_BLOCK, 2 * ATT_BLOCK), 1)
    same = (row >= ATT_BLOCK) == (col >= ATT_BLOCK)
    return jnp.logical_and(same, row >= col).astype(BF16)


def _pair_bf16(pieces):
    return jnp.concatenate(
        [jnp.concatenate([p.astype(BF16) for p in row], axis=1) for row in pieces], axis=0)


def _suffix_sums(pieces, tri_pair):
    hi = [[p.astype(BF16) for p in row] for row in pieces]
    lo = [[p - h.astype(F32) for p, h in zip(row, hrow)] for row, hrow in zip(pieces, hi)]
    return _dot(jnp.concatenate([_pair_bf16(hi), _pair_bf16(lo)], axis=1), tri_pair)


def _first_row_piece(diag, j):
    return j if diag else 0


def _piece(v, r, h):
    return v[ATT_BLOCK * r:ATT_BLOCK * (r + 1), ATT_BLOCK * h:ATT_BLOCK * (h + 1)]


def _all_row_pieces(slabs):
    return [[slab[ATT_BLOCK * r:ATT_BLOCK * (r + 1)] for r in range(SB_SUB)] for slab in slabs]


def _add_rows(total, part, r0):
    if r0 == 0:
        return total + part
    return jnp.concatenate([total[:ATT_BLOCK * r0], total[ATT_BLOCK * r0:] + part], axis=0)


def _row_totals(csum, r0, h):
    return jnp.concatenate(
        [jnp.broadcast_to(_piece(csum, at, h)[:, 0:1], (ATT_BLOCK, ATT_BLOCK)) for at in range(SB_SUB - r0)], axis=0)


SIGN_BIT = 0x80000000


def _drop(z):
    minus_abs = lax.bitcast_convert_type(lax.bitcast_convert_type(z, jnp.uint32) | jnp.uint32(SIGN_BIT), F32)
    return jnp.maximum(z, 0.0) + jnp.log(1.0 + jnp.exp(minus_abs))


def _sb_weights(z, runs, r0, masked, strict, tri_pair):
    drops = []
    for r in range(r0, SB_SUB):
        row = [_drop(_piece(z, r - r0, h)) for h in range(2)]
        drops.append([jnp.where(strict, drop, 0.0) for drop in row] if r == masked else row)
    csum = _suffix_sums(drops, tri_pair)
    weights = []
    for r in range(r0, SB_SUB):
        row = [jnp.exp(_piece(z, r - r0, h) - (_piece(csum, r - r0, h) + runs[h][r])) for h in range(2)]
        weights.append([jnp.where(strict, a, 0.0) for a in row] if r == masked else row)
    return drops, weights, csum


EXP_UNDERFLOW = 104.0


SB_STEP = 2


def _sweep_left(span, i, dropped_ref):
    def more(at):
        return jnp.logical_and(at >= 0, jnp.min(dropped_ref[...]) < EXP_UNDERFLOW)

    def step(at):
        span(at * (SB_STEP * ATT_BLOCK), SB_STEP, False)
        return at - 1

    lax.while_loop(more, step, i * (SB_SUB // SB_STEP) - 1)


def _by_head(blk, masks):
    zero = jnp.zeros_like(blk)
    return jnp.concatenate([jnp.where(m, blk, zero) for m in masks], axis=0)


def sb_fwd(qkv, guest=None):
    s = qkv.shape[0]
    nq = s // SB_CHUNK

    def body(q_ref, k_ref, v_ref, o_ref, run_s):
        i = pl.program_id(1)
        masks = _head_masks()
        q = q_ref[...] * QK_SCALE
        row = lax.broadcasted_iota(jnp.int32, (ATT_BLOCK, ATT_BLOCK), 0)
        col = lax.broadcasted_iota(jnp.int32, (ATT_BLOCK, ATT_BLOCK), 1)
        strict = col < row
        tri_pair = _tri_pair()
        run_s[...] = jnp.zeros_like(run_s)
        o_ref[...] = jnp.zeros_like(o_ref)

        def span(start, pieces, diag):
            runs = [run_s[0], run_s[1]]
            acc = jnp.zeros((SB_CHUNK, ATT_BLOCK), F32)
            for j in reversed(range(pieces)):
                r0 = _first_row_piece(diag, j)
                keys = pl.ds(pl.multiple_of(start + ATT_BLOCK * j, ATT_BLOCK), ATT_BLOCK)
                z = _dot_nt(q[ATT_BLOCK * r0:], _by_head(k_ref[keys, :], masks))
                _, weights, csum = _sb_weights(z, _all_row_pieces(runs), r0, j if diag else None, strict, tri_pair)
                part = _dot(_pair_bf16(weights), _by_head(v_ref[keys, :], masks))
                acc = _add_rows(acc, part, r0)
                runs = [_add_rows(runs[h], _row_totals(csum, r0, h), r0) for h in range(2)]
            run_s[0], run_s[1] = runs
            o_ref[...] += acc

        span(i * SB_CHUNK, SB_SUB, True)
        _sweep_left(span, i, run_s)

    return _pallas(
        body, guest, (qkv, qkv, qkv), name="sb_fwd", grid=(N_PAIRS, nq),
        in_specs=[pl.BlockSpec((SB_CHUNK, 128), lambda p, i: (i, p)),
                  pl.BlockSpec((s, 128), lambda p, i: (0, N_PAIRS + p)),
                  pl.BlockSpec((s, 128), lambda p, i: (0, 2 * N_PAIRS + p))],
        out_specs=[pl.BlockSpec((SB_CHUNK, 128), lambda p, i: (i, p))],
        out_shape=[jax.ShapeDtypeStruct((s, ATT_WIDTH), F32)],
        scratch_shapes=[pltpu.VMEM((2, SB_CHUNK, 128), F32)],
        compiler_params=_params(("arbitrary", "arbitrary")),
    )


def sb_bwd(qkv, o_sb, do_sb, guest=None):
    s = qkv.shape[0]
    nq = s // SB_CHUNK

    def body(q_ref, k_ref, v_ref, o_ref, do_ref, dq_ref, dk_ref, dv_ref, run_l_s, run_e_s, etot_s, dq_s):
        i = pl.program_id(1)

        @pl.when(i == 0)
        def _():
            dk_ref[...] = jnp.zeros_like(dk_ref)
            dv_ref[...] = jnp.zeros_like(dv_ref)

        masks = _head_masks()
        q = q_ref[...] * QK_SCALE
        do = do_ref[...]
        prod = do.astype(F32) * o_ref[...]
        for h in range(2):
            total = jnp.sum(jnp.where(masks[h], prod, 0.0), axis=1, keepdims=True)
            etot_s[h] = jnp.broadcast_to(total, (SB_CHUNK, ATT_BLOCK))
        row = lax.broadcasted_iota(jnp.int32, (ATT_BLOCK, ATT_BLOCK), 0)
        col = lax.broadcasted_iota(jnp.int32, (ATT_BLOCK, ATT_BLOCK), 1)
        strict = col < row
        tri_pair = _tri_pair()
        run_l_s[...] = jnp.zeros_like(run_l_s)
        run_e_s[...] = jnp.zeros_like(run_e_s)
        dq_s[...] = jnp.zeros_like(dq_s)

        def span(start, pieces, diag):
            slabs_l = [run_l_s[0], run_l_s[1]]
            slabs_e = [run_e_s[0], run_e_s[1]]
            e_tot = _all_row_pieces([etot_s[0], etot_s[1]])
            dq = jnp.zeros((SB_CHUNK, ATT_BLOCK), F32)
            for j in reversed(range(pieces)):
                runs_l, runs_e = _all_row_pieces(slabs_l), _all_row_pieces(slabs_e)
                r0 = _first_row_piece(diag, j)
                masked = j if diag else None
                keys = pl.ds(pl.multiple_of(start + ATT_BLOCK * j, ATT_BLOCK), ATT_BLOCK)
                kk = _by_head(k_ref[keys, :], masks)
                vv = _by_head(v_ref[keys, :], masks)
                q_rows, do_rows = q[ATT_BLOCK * r0:], do[ATT_BLOCK * r0:]
                z = _dot_nt(q_rows, kk)
                da = _dot_nt(do_rows, vv)
                drops, weights, csum = _sb_weights(z, runs_l, r0, masked, strict, tri_pair)
                ab = [[a.astype(BF16) for a in wrow] for wrow in weights]
                es = [[a.astype(F32) * _piece(da, at, h) for h, a in enumerate(arow)] for at, arow in enumerate(ab)]
                esum = _suffix_sums(es, tri_pair)
                dzs = []
                for at, r in enumerate(range(r0, SB_SUB)):
                    dz_row = []
                    for h in range(2):
                        sig = jnp.exp(_piece(z, at, h) - drops[at][h])
                        after = _piece(esum, at, h) + runs_e[h][r] - es[at][h]
                        dz = es[at][h] - sig * (e_tot[h][r] - after)
                        dz_row.append(jnp.where(strict, dz, 0.0) if r == masked else dz)
                    dzs.append(dz_row)
                dz = _pair_bf16(dzs)
                dq = _add_rows(dq, _dot(dz, kk), r0)
                dk = _dot_tn(dz, q_rows)
                dv = _dot_tn(_pair_bf16(ab), do_rows)
                dk_ref[keys, :] += jnp.where(masks[0], dk[:ATT_BLOCK], dk[ATT_BLOCK:])
                dv_ref[keys, :] += jnp.where(masks[0], dv[:ATT_BLOCK], dv[ATT_BLOCK:])
                slabs_l = [_add_rows(slabs_l[h], _row_totals(csum, r0, h), r0) for h in range(2)]
                slabs_e = [_add_rows(slabs_e[h], _row_totals(esum, r0, h), r0) for h in range(2)]
            run_l_s[0], run_l_s[1] = slabs_l
            run_e_s[0], run_e_s[1] = slabs_e
            dq_s[...] += dq

        span(i * SB_CHUNK, SB_SUB, True)
        _sweep_left(span, i, run_l_s)
        dq_ref[...] = (dq_s[...] * QK_SCALE).astype(BF16)

    tile = lambda p, i: (i, p)
    state = pltpu.VMEM((2, SB_CHUNK, 128), F32)
    return _pallas(
        body, guest, (qkv, qkv, qkv, o_sb, do_sb), name="sb_bwd", grid=(N_PAIRS, nq),
        in_specs=[pl.BlockSpec((SB_CHUNK, 128), tile),
                  pl.BlockSpec((s, 128), lambda p, i: (0, N_PAIRS + p)),
                  pl.BlockSpec((s, 128), lambda p, i: (0, 2 * N_PAIRS + p)),
                  pl.BlockSpec((SB_CHUNK, 128), tile),
                  pl.BlockSpec((SB_CHUNK, 128), tile)],
        out_specs=[pl.BlockSpec((SB_CHUNK, 128), tile),
                   pl.BlockSpec((s, 128), lambda p, i: (0, p)),
                   pl.BlockSpec((s, 128), lambda p, i: (0, p))],
        out_shape=[jax.ShapeDtypeStruct((s, ATT_WIDTH), BF16),
                   jax.ShapeDtypeStruct((s, ATT_WIDTH), F32),
                   jax.ShapeDtypeStruct((s, ATT_WIDTH), F32)],
        scratch_shapes=[state, state, state, pltpu.VMEM((SB_CHUNK, 128), F32)],
        compiler_params=_params(("arbitrary", "arbitrary")),
    )


def ca_bias_tiles(rel_bias):
    heads = rel_bias.shape[0]
    wide = 2 * ATT_BLOCK

    def body(rel_ref, out_ref):
        bins = lax.broadcasted_iota(jnp.int32, (REL_PAD, wide), 0)
        pos = lax.broadcasted_iota(jnp.int32, (REL_PAD, wide), 1)
        r = lax.broadcasted_iota(jnp.int32, (ATT_BLOCK, ATT_BLOCK), 0)
        c = lax.broadcasted_iota(jnp.int32, (ATT_BLOCK, ATT_BLOCK), 1)
        for j in range(CA_TILES):
            hot = (jnp.clip(ATT_BLOCK * (j + 1) - 1 - pos, -REL_CLIP, REL_CLIP) + REL_CLIP == bins).astype(F32)
            lines = jnp.dot(rel_ref[...], hot, preferred_element_type=F32, precision=lax.Precision.HIGHEST)
            back = 2 * j + (r >> 6) - (c >> 6)
            ok = jnp.logical_and(back >= 0, back <= CA_PREV_CHUNKS)
            for h in range(heads):
                tile = jnp.broadcast_to(lines[h:h + 1, :], (ATT_BLOCK, wide))
                tile = pltpu.roll(tile, wide - (ATT_BLOCK - 1), 1, stride=1, stride_axis=0)
                out_ref[h, j] = jnp.where(ok, tile[:, :ATT_BLOCK], NEG_BIG)

    padded = jnp.pad(rel_bias, ((0, 0), (0, REL_PAD - rel_bias.shape[1])))
    return pl.pallas_call(
        body, name="ca_bias_tiles",
        out_shape=jax.ShapeDtypeStruct((heads, CA_TILES, ATT_BLOCK, ATT_BLOCK), F32),
        compiler_params=_params(),
    )(padded)


CA_BLOCKS = CA_ROWS // ATT_BLOCK
CA_KEY_BLOCKS = CA_BLOCKS + CA_TILES - 1


def _ca_users(t):
    return [r for r in range(CA_BLOCKS) if 0 <= r + CA_TILES - 1 - t < CA_TILES]


def _ca_rows(v, t, r):
    at = _ca_users(t).index(r)
    return v[ATT_BLOCK * at:ATT_BLOCK * (at + 1)]


def _ca_window(step, k_ref, v_ref, masks):
    kks, vvs, inside, rows = [], [], [], []
    for t in range(CA_KEY_BLOCKS):
        block = step * CA_BLOCKS - (CA_TILES - 1) + t
        keys = pl.ds(pl.multiple_of(jnp.maximum(block, 0) * ATT_BLOCK, ATT_BLOCK), ATT_BLOCK)
        kks.append(_by_head(k_ref[keys, :], masks))
        vvs.append(_by_head(v_ref[keys, :], masks))
        inside.append(block >= 0)
        rows.append(keys)
    return kks, vvs, inside, rows


def _ca_probs(q, kks, inside, bias_ref):
    scores = [[[None] * CA_TILES for _ in range(2)] for _ in range(CA_BLOCKS)]
    for t in range(CA_KEY_BLOCKS):
        users = _ca_users(t)
        z = _dot_nt(q[ATT_BLOCK * users[0]:ATT_BLOCK * (users[-1] + 1)], kks[t])
        for r in users:
            j = r + CA_TILES - 1 - t
            for h in range(2):
                zz = _ca_rows(z, t, r)[:, ATT_BLOCK * h:ATT_BLOCK * (h + 1)] + bias_ref[h, j]
                scores[r][h][j] = jnp.where(inside[t], zz, NEG_BIG)
    probs = [[None, None] for _ in range(CA_BLOCKS)]
    for r in range(CA_BLOCKS):
        for h in range(2):
            m = jnp.max(functools.reduce(jnp.maximum, scores[r][h]), axis=1, keepdims=True)
            ex = [jnp.exp(z - m) for z in scores[r][h]]
            inv = 1.0 / jnp.sum(functools.reduce(jnp.add, ex), axis=1, keepdims=True)
            probs[r][h] = [e * inv for e in ex]
    return probs


def _ca_stack(tiles, t):
    return jnp.concatenate(
        [jnp.concatenate([tiles[r][h][r + CA_TILES - 1 - t].astype(BF16) for h in range(2)], axis=1)
         for r in _ca_users(t)], axis=0)


def ca_fwd(qkv, bias_tiles, guest=None):
    s = qkv.shape[0]
    nq = s // CA_ROWS

    def body(q_ref, k_ref, v_ref, bias_ref, o_ref):
        masks = _head_masks()
        kks, vvs, inside, _ = _ca_window(pl.program_id(1), k_ref, v_ref, masks)
        probs = _ca_probs(q_ref[...] * QK_SCALE, kks, inside, bias_ref)
        out = [None] * CA_BLOCKS
        for t in range(CA_KEY_BLOCKS):
            part = _dot(_ca_stack(probs, t), vvs[t])
            for r in _ca_users(t):
                piece = _ca_rows(part, t, r)
                out[r] = piece if out[r] is None else out[r] + piece
        o_ref[...] = jnp.concatenate(out, axis=0)

    return _pallas(
        body, guest, (qkv, qkv, qkv, bias_tiles), name="ca_fwd", grid=(N_PAIRS, nq),
        in_specs=[pl.BlockSpec((CA_ROWS, 128), lambda p, i: (i, 3 * N_PAIRS + p)),
                  pl.BlockSpec((s, 128), lambda p, i: (0, 4 * N_PAIRS + p)),
                  pl.BlockSpec((s, 128), lambda p, i: (0, 5 * N_PAIRS + p)),
                  pl.BlockSpec((2, CA_TILES, ATT_BLOCK, ATT_BLOCK), lambda p, i: (p, 0, 0, 0))],
        out_specs=[pl.BlockSpec((CA_ROWS, 128), lambda p, i: (i, p))],
        out_shape=[jax.ShapeDtypeStruct((s, ATT_WIDTH), F32)],
        compiler_params=_params(("arbitrary", "arbitrary")),
    )


def ca_bwd(qkv, bias_tiles, do_ca, guest=None):
    s = qkv.shape[0]
    nq = s // CA_ROWS

    def body(q_ref, k_ref, v_ref, bias_ref, do_ref, dq_ref, dk_ref, dv_ref, db_ref):
        big = pl.program_id(1)

        @pl.when(big == 0)
        def _():
            dk_ref[...] = jnp.zeros_like(dk_ref)
            dv_ref[...] = jnp.zeros_like(dv_ref)
            db_ref[...] = jnp.zeros_like(db_ref)

        masks = _head_masks()
        kks, vvs, inside, key_rows = _ca_window(big, k_ref, v_ref, masks)
        q = q_ref[...] * QK_SCALE
        do = do_ref[...]
        probs = _ca_probs(q, kks, inside, bias_ref)
        dps = [[[None] * CA_TILES for _ in range(2)] for _ in range(CA_BLOCKS)]
        for t in range(CA_KEY_BLOCKS):
            users = _ca_users(t)
            dp = _dot_nt(do[ATT_BLOCK * users[0]:ATT_BLOCK * (users[-1] + 1)], vvs[t])
            for r in users:
                for h in range(2):
                    dps[r][h][r + CA_TILES - 1 - t] = _ca_rows(dp, t, r)[:, ATT_BLOCK * h:ATT_BLOCK * (h + 1)]
        dss = [[None, None] for _ in range(CA_BLOCKS)]
        for r in range(CA_BLOCKS):
            for h in range(2):
                delta = jnp.sum(functools.reduce(jnp.add, [p * dp for p, dp in zip(probs[r][h], dps[r][h])]),
                                axis=1, keepdims=True)
                dss[r][h] = [p * (dp - delta) for p, dp in zip(probs[r][h], dps[r][h])]
        for h in range(2):
            for j in range(CA_TILES):
                db_ref[h, j] += functools.reduce(jnp.add, [dss[r][h][j] for r in range(CA_BLOCKS)])
        dq = [None] * CA_BLOCKS
        for t in range(CA_KEY_BLOCKS):
            users = _ca_users(t)
            rows = slice(ATT_BLOCK * users[0], ATT_BLOCK * (users[-1] + 1))
            ds = _ca_stack(dss, t)
            part = _dot(ds, kks[t])
            for r in users:
                piece = _ca_rows(part, t, r)
                dq[r] = piece if dq[r] is None else dq[r] + piece
            dk = _dot_tn(ds, q[rows])
            dv = _dot_tn(_ca_stack(probs, t), do[rows])
            dk_ref[key_rows[t], :] += jnp.where(masks[0], dk[:ATT_BLOCK], dk[ATT_BLOCK:])
            dv_ref[key_rows[t], :] += jnp.where(masks[0], dv[:ATT_BLOCK], dv[ATT_BLOCK:])
        dq_ref[...] = (jnp.concatenate(dq, axis=0) * QK_SCALE).astype(BF16)

    return _pallas(
        body, guest, (qkv, qkv, qkv, bias_tiles, do_ca), name="ca_bwd", grid=(N_PAIRS, nq),
        in_specs=[pl.BlockSpec((CA_ROWS, 128), lambda p, i: (i, 3 * N_PAIRS + p)),
                  pl.BlockSpec((s, 128), lambda p, i: (0, 4 * N_PAIRS + p)),
                  pl.BlockSpec((s, 128), lambda p, i: (0, 5 * N_PAIRS + p)),
                  pl.BlockSpec((2, CA_TILES, ATT_BLOCK, ATT_BLOCK), lambda p, i: (p, 0, 0, 0)),
                  pl.BlockSpec((CA_ROWS, 128), lambda p, i: (i, p))],
        out_specs=[pl.BlockSpec((CA_ROWS, 128), lambda p, i: (i, p)),
                   pl.BlockSpec((s, 128), lambda p, i: (0, p)),
                   pl.BlockSpec((s, 128), lambda p, i: (0, p)),
                   pl.BlockSpec((2, CA_TILES, ATT_BLOCK, ATT_BLOCK), lambda p, i: (p, 0, 0, 0))],
        out_shape=[jax.ShapeDtypeStruct((s, ATT_WIDTH), BF16),
                   jax.ShapeDtypeStruct((s, ATT_WIDTH), F32),
                   jax.ShapeDtypeStruct((s, ATT_WIDTH), F32),
                   jax.ShapeDtypeStruct((2 * N_PAIRS, CA_TILES, ATT_BLOCK, ATT_BLOCK), F32)],
        compiler_params=_params(("arbitrary", "arbitrary")),
    )


def rel_bias_grad(db_tiles):
    heads = db_tiles.shape[0]

    def body(db_ref, out_ref):
        row = lax.broadcasted_iota(jnp.int32, (ATT_BLOCK, ATT_BLOCK), 0)
        col = lax.broadcasted_iota(jnp.int32, (ATT_BLOCK, ATT_BLOCK), 1)
        flip = (row + col == ATT_BLOCK - 1).astype(F32)
        wrapped = col < row
        cp = lax.broadcasted_iota(jnp.int32, (ATT_BLOCK, REL_PAD), 0)
        bins = lax.broadcasted_iota(jnp.int32, (ATT_BLOCK, REL_PAD), 1)
        total = jnp.zeros((heads, REL_PAD), F32)
        for j in range(CA_TILES):
            sums_neg, sums_pos = [], []
            for h in range(heads):
                tile = jnp.dot(db_ref[h, j], flip, preferred_element_type=F32, precision=lax.Precision.HIGHEST)
                tile = pltpu.roll(tile, 0, 1, stride=1, stride_axis=0)
                sums_neg.append(jnp.sum(jnp.where(wrapped, 0.0, tile), axis=0, keepdims=True))
                sums_pos.append(jnp.sum(jnp.where(wrapped, tile, 0.0), axis=0, keepdims=True))
            neg = jnp.concatenate(sums_neg, axis=0)
            pos = jnp.concatenate(sums_pos, axis=0)
            hot_neg = (jnp.clip(ATT_BLOCK * j + cp - (ATT_BLOCK - 1), -REL_CLIP, REL_CLIP) + REL_CLIP == bins)
            hot_pos = (jnp.clip(ATT_BLOCK * j + cp + 1, -REL_CLIP, REL_CLIP) + REL_CLIP == bins)
            total = total + jnp.dot(neg, hot_neg.astype(F32), preferred_element_type=F32,
                                    precision=lax.Precision.HIGHEST)
            total = total + jnp.dot(pos, hot_pos.astype(F32), preferred_element_type=F32,
                                    precision=lax.Precision.HIGHEST)
        out_ref[...] = total

    return pl.pallas_call(
        body, name="rel_bias_grad",
        out_shape=jax.ShapeDtypeStruct((heads, REL_PAD), F32),
        compiler_params=_params(),
    )(db_tiles)


def _assemble_cols(dst_ref, src_ref):
    w = src_ref.shape[2]
    for j in range(N_DEV):
        dst_ref[:, w * j:w * (j + 1)] = src_ref[j]


def _merge(o_sb, o_ca, lg, bg, wsb, wca):
    y_sb = _dot(o_sb.astype(BF16), wsb)
    y_ca = _dot(o_ca.astype(BF16), wca)
    gates = jax.nn.sigmoid(lg + bg)
    g_sb, g_ca = gates[:, :D_MODEL], gates[:, D_MODEL:]
    return y_sb, y_ca, g_sb, g_ca, g_sb * y_sb + g_ca * y_ca


def mix_fwd(x, o_sb, o_ca, lg, b_gate, wg_sb, wg_ca, w_out, tm=256):
    s = x.shape[0]

    def body(x_ref, osb_ref, oca_ref, lg_ref, bg_ref, wsb_ref, wca_ref, wout_ref, u_ref, wsb_s, wca_s):
        @pl.when(pl.program_id(0) == 0)
        def _():
            _assemble_cols(wsb_s, wsb_ref)
            _assemble_cols(wca_s, wca_ref)

        merged = _merge(osb_ref[...], oca_ref[...], lg_ref[...], bg_ref[...], wsb_s[...], wca_s[...])[4]
        u_ref[...] = ALPHA * x_ref[...] + _dot(merged.astype(BF16), wout_ref[...])

    rows = lambda w: pl.BlockSpec((tm, w), lambda i: (i, 0))
    whole = lambda shape: pl.BlockSpec(shape, lambda i: (0,) * len(shape))
    return pl.pallas_call(
        body, name="mix_fwd", grid=(s // tm,),
        in_specs=[rows(D_MODEL), rows(ATT_WIDTH), rows(ATT_WIDTH), rows(GATE_COLS), whole((1, GATE_COLS)),
                  whole((N_DEV, ATT_WIDTH, PROJ_SHARD)), whole((N_DEV, ATT_WIDTH, PROJ_SHARD)),
                  whole((D_MODEL, D_MODEL))],
        out_specs=rows(D_MODEL),
        out_shape=jax.ShapeDtypeStruct((s, D_MODEL), F32),
        scratch_shapes=[pltpu.VMEM((ATT_WIDTH, D_MODEL), BF16), pltpu.VMEM((ATT_WIDTH, D_MODEL), BF16)],
        compiler_params=_params(("arbitrary",)),
    )(x, o_sb, o_ca, lg, _row(b_gate), wg_sb, wg_ca, w_out)


def mix_bwd(du1, o_sb, o_ca, lg, b_gate, wg_sb, wg_ca, w_out, tm=256, guest=None):
    s = du1.shape[0]

    def body(du_ref, osb_ref, oca_ref, lg_ref, bg_ref, wsb_ref, wca_ref, wout_ref,
             dlg_ref, dosb_ref, doca_ref, dysb_ref, dyca_ref, mg_ref, dbg_ref, wsb_s, wca_s):
        @pl.when(pl.program_id(0) == 0)
        def _():
            _assemble_cols(wsb_s, wsb_ref)
            _assemble_cols(wca_s, wca_ref)
            dbg_ref[...] = jnp.zeros_like(dbg_ref)

        y_sb, y_ca, g_sb, g_ca, merged = _merge(
            osb_ref[...], oca_ref[...], lg_ref[...], bg_ref[...], wsb_s[...], wca_s[...])
        mg_ref[...] = merged.astype(BF16)
        dm = _dot_nt(du_ref[...].astype(BF16), wout_ref[...])
        dl_sb = dm * y_sb * g_sb * (1.0 - g_sb)
        dl_ca = dm * y_ca * g_ca * (1.0 - g_ca)
        dlg_ref[:, :D_MODEL] = dl_sb.astype(BF16)
        dlg_ref[:, D_MODEL:] = dl_ca.astype(BF16)
        dbg_ref[:, :D_MODEL] += jnp.sum(dl_sb, axis=0, keepdims=True)
        dbg_ref[:, D_MODEL:] += jnp.sum(dl_ca, axis=0, keepdims=True)
        dy_sb = (dm * g_sb).astype(BF16)
        dy_ca = (dm * g_ca).astype(BF16)
        dysb_ref[...] = dy_sb
        dyca_ref[...] = dy_ca
        dosb_ref[...] = _dot_nt(dy_sb, wsb_s[...]).astype(BF16)
        doca_ref[...] = _dot_nt(dy_ca, wca_s[...]).astype(BF16)

    rows = lambda w: pl.BlockSpec((tm, w), lambda i: (i, 0))
    whole = lambda shape: pl.BlockSpec(shape, lambda i: (0,) * len(shape))
    return _pallas(
        body, guest, (du1, o_sb, o_ca, lg, _row(b_gate), wg_sb, wg_ca, w_out), name="mix_bwd", grid=(s // tm,),
        in_specs=[rows(D_MODEL), rows(ATT_WIDTH), rows(ATT_WIDTH), rows(GATE_COLS), whole((1, GATE_COLS)),
                  whole((N_DEV, ATT_WIDTH, PROJ_SHARD)), whole((N_DEV, ATT_WIDTH, PROJ_SHARD)),
                  whole((D_MODEL, D_MODEL))],
        out_specs=[rows(GATE_COLS), rows(ATT_WIDTH), rows(ATT_WIDTH), rows(D_MODEL), rows(D_MODEL),
                   rows(D_MODEL), whole((1, GATE_COLS))],
        out_shape=[jax.ShapeDtypeStruct((s, GATE_COLS), BF16),
                   jax.ShapeDtypeStruct((s, ATT_WIDTH), BF16), jax.ShapeDtypeStruct((s, ATT_WIDTH), BF16),
                   jax.ShapeDtypeStruct((s, D_MODEL), BF16), jax.ShapeDtypeStruct((s, D_MODEL), BF16),
                   jax.ShapeDtypeStruct((s, D_MODEL), BF16), jax.ShapeDtypeStruct((1, GATE_COLS), F32)],
        scratch_shapes=[pltpu.VMEM((ATT_WIDTH, D_MODEL), BF16), pltpu.VMEM((ATT_WIDTH, D_MODEL), BF16)],
        compiler_params=_params(("arbitrary",)),
    )


def _ln_stats(u):
    mu = jnp.mean(u, axis=1, keepdims=True)
    cen = u - mu
    var = jnp.mean(cen * cen, axis=1, keepdims=True)
    rstd = lax.rsqrt(var + LN_EPS)
    return cen * rstd, rstd


def _ln_bwd(dy, xhat, rstd, gain):
    dyg = dy * gain
    m1 = jnp.mean(dyg, axis=1, keepdims=True)
    m2 = jnp.mean(dyg * xhat, axis=1, keepdims=True)
    return rstd * (dyg - m1 - xhat * m2)


def _load_mlp_weights(win_hbm, wout_hbm, win_s, wout_s, sems):
    copies = [pltpu.make_async_copy(win_hbm.at[k], win_s.at[:, pl.ds(FF_SHARD * k, FF_SHARD)], sems.at[k])
              for k in range(N_DEV)]
    copies.append(pltpu.make_async_copy(wout_hbm, wout_s, sems.at[N_DEV]))
    for cp in copies:
        cp.start()
    return copies


MLP_WEIGHT_SCRATCH = [pltpu.VMEM((D_MODEL, D_FF), BF16), pltpu.VMEM((D_FF, D_MODEL), BF16),
                      pltpu.SemaphoreType.DMA((N_DEV + 1,))]


def mlp_fwd(u1, target, wg_in, w_out, ln1_g, ln1_b, ln2_g, ln2_b, tm=256):
    s = u1.shape[0]

    def body(u_ref, t_ref, win_hbm, wout_hbm, g1_ref, b1_ref, g2_ref, b2_ref,
             hm_ref, a_ref, x1b_ref, du2_ref, du2b_ref, loss_ref, dg2_ref, db2_ref, win_s, wout_s, sems):
        first = pl.program_id(0) == 0

        @pl.when(first)
        def _():
            copies = _load_mlp_weights(win_hbm, wout_hbm, win_s, wout_s, sems)
            loss_ref[...] = jnp.zeros_like(loss_ref)
            dg2_ref[...] = jnp.zeros_like(dg2_ref)
            db2_ref[...] = jnp.zeros_like(db2_ref)
            for cp in copies:
                cp.wait()

        xhat, _ = _ln_stats(u_ref[...])
        x1 = xhat * g1_ref[...] + b1_ref[...]
        x1b = x1.astype(BF16)
        x1b_ref[...] = x1b
        for k in range(N_DEV):
            cols = slice(FF_SHARD * k, FF_SHARD * (k + 1))
            hm = _dot(x1b, win_s[:, cols])
            hm_ref[:, cols] = hm
            r = jnp.maximum(hm, 0.0)
            a_ref[:, cols] = (r * r).astype(BF16)
        xhat, rstd = _ln_stats(ALPHA * x1 + _dot(a_ref[...], wout_s[...]))
        diff = xhat * g2_ref[...] + b2_ref[...] - t_ref[...]
        per_token = jnp.mean(diff * diff, axis=1, keepdims=True)
        loss_ref[...] += 0.5 * jnp.sum(per_token, axis=0, keepdims=True)
        dy = diff * (1.0 / D_MODEL)
        dg2_ref[...] += jnp.sum(dy * xhat, axis=0, keepdims=True)
        db2_ref[...] += jnp.sum(dy, axis=0, keepdims=True)
        du2 = _ln_bwd(dy, xhat, rstd, g2_ref[...])
        du2_ref[...] = du2
        du2b_ref[...] = du2.astype(BF16)

    rows = pl.BlockSpec((tm, D_MODEL), lambda i: (i, 0))
    vec = pl.BlockSpec((1, D_MODEL), lambda i: (0, 0))
    ff = pl.BlockSpec((tm, D_FF), lambda i: (i, 0))
    return pl.pallas_call(
        body, name="mlp_fwd", grid=(s // tm,),
        in_specs=[rows, rows, HBM_SPEC, HBM_SPEC, vec, vec, vec, vec],
        out_specs=[ff, ff, rows, rows, rows, pl.BlockSpec((1, 1), lambda i: (0, 0)), vec, vec],
        out_shape=[jax.ShapeDtypeStruct((s, D_FF), F32), jax.ShapeDtypeStruct((s, D_FF), BF16),
                   jax.ShapeDtypeStruct((s, D_MODEL), BF16), jax.ShapeDtypeStruct((s, D_MODEL), F32),
                   jax.ShapeDtypeStruct((s, D_MODEL), BF16), jax.ShapeDtypeStruct((1, 1), F32),
                   jax.ShapeDtypeStruct((1, D_MODEL), F32), jax.ShapeDtypeStruct((1, D_MODEL), F32)],
        scratch_shapes=MLP_WEIGHT_SCRATCH,
        compiler_params=_params(("arbitrary",)),
    )(u1, target, wg_in, w_out, _row(ln1_g), _row(ln1_b), _row(ln2_g), _row(ln2_b))


def mlp_bwd(du2, hm, u1, wg_in, w_out, ln1_g, tm=256):
    s = du2.shape[0]

    def body(du2_ref, hm_ref, u_ref, win_hbm, wout_hbm, g1_ref,
             dhm_ref, du1_ref, dg1_ref, db1_ref, win_s, wout_s, sems):
        first = pl.program_id(0) == 0

        @pl.when(first)
        def _():
            copies = _load_mlp_weights(win_hbm, wout_hbm, win_s, wout_s, sems)
            dg1_ref[...] = jnp.zeros_like(dg1_ref)
            db1_ref[...] = jnp.zeros_like(db1_ref)
            for cp in copies:
                cp.wait()

        du2 = du2_ref[...]
        du2b = du2.astype(BF16)
        for k in range(N_DEV):
            cols = slice(FF_SHARD * k, FF_SHARD * (k + 1))
            da = _dot_nt(du2b, wout_s[cols, :])
            dhm_ref[:, cols] = (2.0 * jnp.maximum(hm_ref[:, cols], 0.0) * da).astype(BF16)
        dx1 = ALPHA * du2 + _dot_nt(dhm_ref[...], win_s[...])
        xhat, rstd = _ln_stats(u_ref[...])
        dg1_ref[...] += jnp.sum(dx1 * xhat, axis=0, keepdims=True)
        db1_ref[...] += jnp.sum(dx1, axis=0, keepdims=True)
        du1_ref[...] = _ln_bwd(dx1, xhat, rstd, g1_ref[...])

    rows = pl.BlockSpec((tm, D_MODEL), lambda i: (i, 0))
    vec = pl.BlockSpec((1, D_MODEL), lambda i: (0, 0))
    ff = pl.BlockSpec((tm, D_FF), lambda i: (i, 0))
    return pl.pallas_call(
        body, name="mlp_bwd", grid=(s // tm,),
        in_specs=[rows, ff, rows, HBM_SPEC, HBM_SPEC, vec],
        out_specs=[ff, rows, vec, vec],
        out_shape=[jax.ShapeDtypeStruct((s, D_FF), BF16), jax.ShapeDtypeStruct((s, D_MODEL), F32),
                   jax.ShapeDtypeStruct((1, D_MODEL), F32), jax.ShapeDtypeStruct((1, D_MODEL), F32)],
        scratch_shapes=MLP_WEIGHT_SCRATCH,
        compiler_params=_params(("arbitrary",)),
    )(du2, hm, u1, wg_in, w_out, _row(ln1_g))


def dh_slab(pieces, tm=512):
    s = pieces[0].shape[0]
    widths = [p.shape[1] for p in pieces]
    assert sum(widths) == IN_COLS
    n = len(pieces)

    def body(*refs):
        dh_ref = refs[n]
        off = 0
        for ref, w in zip(refs[:n], widths):
            dh_ref[:, off:off + w] = ref[...].astype(BF16)
            off += w

    rows = lambda w: pl.BlockSpec((tm, w), lambda i: (i, 0))
    return pl.pallas_call(
        body, name="dh_slab", grid=(s // tm,),
        in_specs=[rows(w) for w in widths], out_specs=rows(IN_COLS),
        out_shape=jax.ShapeDtypeStruct((s, IN_COLS), BF16),
        compiler_params=_params(("arbitrary",)),
    )(*pieces)


def in_bwd(du1, dh, wg_in, tm=512, guest=None):
    s = du1.shape[0]

    def body(du_ref, dh_ref, w_hbm, gx_ref, pairs_s, sems):
        @pl.when(pl.program_id(0) == 0)
        def _():
            _load_in_pairs(w_hbm, pairs_s, sems)

        acc = ALPHA * du_ref[...]
        for j in range(N_DEV // 2):
            acc = acc + _dot_nt(dh_ref[:, IN_PAIR * j:IN_PAIR * (j + 1)], pairs_s[j])
        gx_ref[...] = acc

    rows = lambda w: pl.BlockSpec((tm, w), lambda i: (i, 0))
    return _pallas(
        body, guest, (du1, dh, wg_in), name="in_bwd", grid=(s // tm,), scratch_shapes=IN_PAIR_SCRATCH,
        in_specs=[rows(D_MODEL), rows(IN_COLS), HBM_SPEC],
        out_specs=[rows(D_MODEL)],
        out_shape=[jax.ShapeDtypeStruct((s, D_MODEL), F32)],
        compiler_params=_params(("arbitrary",)),
    )


def wgrad(name, a, b, tm, tn, shard_cols=None, tk=2048):
    kk, m = a.shape
    n = b.shape[1]
    while kk % tk:
        tk //= 2
    nk = kk // tk
    w = shard_cols

    def body(a_ref, b_ref, out_ref, acc_s):
        k = pl.program_id(2)

        @pl.when(k == 0)
        def _():
            acc_s[...] = jnp.zeros_like(acc_s)

        acc_s[...] += _dot_tn(a_ref[...].astype(BF16), b_ref[...].astype(BF16))

        @pl.when(k == nk - 1)
        def _():
            if w is None:
                out_ref[...] = acc_s[...]
            else:
                for j in range(tn // w):
                    out_ref[j] = acc_s[:, w * j:w * (j + 1)]

    if w is None:
        out_spec = pl.BlockSpec((tm, tn), lambda i, j, k: (i, j))
        out_shape = jax.ShapeDtypeStruct((m, n), F32)
    else:
        out_spec = pl.BlockSpec((tn // w, tm, w), lambda i, j, k: (j, i, 0))
        out_shape = jax.ShapeDtypeStruct((n // w, m, w), F32)
    return pl.pallas_call(
        body, name=name, grid=(m // tm, n // tn, nk),
        in_specs=[pl.BlockSpec((tk, tm), lambda i, j, k: (k, i)),
                  pl.BlockSpec((tk, tn), lambda i, j, k: (k, j))],
        out_specs=out_spec, out_shape=out_shape,
        scratch_shapes=[pltpu.VMEM((tm, tn), F32)],
        compiler_params=_params(("arbitrary", "arbitrary", "arbitrary")),
    )(a, b)


def adamw(name, w, m, v, parts, tm=None, guest=None):
    rows, cols = w.shape
    tm = rows if tm is None else min(tm, rows)
    n = len(parts)
    c1 = 1.0 - ADAM_B1 ** ADAM_STEP
    c2 = 1.0 - ADAM_B2 ** ADAM_STEP

    def body(*refs):
        w_ref, m_ref, v_ref = refs[:3]
        part_refs = refs[3:3 + n]
        g_ref, d_ref, nm_ref, nv_ref = refs[3 + n:]
        g = None
        for ref, (_, index) in zip(part_refs, parts):
            term = (ref[...] if index is None else ref[0]).astype(F32)
            g = term if g is None else g + term
        new_m = ADAM_B1 * m_ref[...] + (1.0 - ADAM_B1) * g
        new_v = ADAM_B2 * v_ref[...] + (1.0 - ADAM_B2) * (g * g)
        m_hat = new_m / c1
        v_hat = new_v / c2
        g_ref[...] = g
        d_ref[...] = -ADAM_LR * (m_hat / (jnp.sqrt(v_hat) + ADAM_EPS) + ADAM_WD * w_ref[...])
        nm_ref[...] = new_m
        nv_ref[...] = new_v

    spec = pl.BlockSpec((tm, cols), lambda i: (i, 0))
    shape = jax.ShapeDtypeStruct((rows, cols), F32)

    def part_spec(index):
        if index is None:
            return spec
        return pl.BlockSpec((1, tm, cols), lambda i: (index, i, 0))

    return _pallas(
        body, guest, (w, m, v, *[array for array, _ in parts]), name=name, grid=(rows // tm,),
        in_specs=[spec] * 3 + [part_spec(index) for _, index in parts],
        out_specs=[spec] * 4, out_shape=[shape] * 4,
        compiler_params=_params(("arbitrary",)),
    )


def _place():
    x, y, c = lax.axis_index("x"), lax.axis_index("y"), lax.axis_index("c")
    other_chips = [(1 - x, y), (x, 1 - y), (1 - x, 1 - y)]
    return x, y, c, other_chips


def place_shards(arrays):
    n = len(arrays)

    def body(*refs):
        ins, outs, stage, sems = refs[:n], refs[n:2 * n], refs[2 * n:3 * n], refs[3 * n]
        x, y, c, _ = _place()
        copies = []
        for a in range(n):
            stage[a][...] = ins[a][...].astype(BF16)
            copies.append(pltpu.make_async_copy(stage[a], outs[a].at[4 * x + 2 * y + c], sems.at[a]))
            copies[-1].start()
        for cp in copies:
            cp.wait()

    return pl.pallas_call(
        body, name="place_shards",
        in_specs=[pl.BlockSpec(memory_space=pltpu.VMEM)] * n, out_specs=[HBM_SPEC] * n,
        out_shape=[jax.ShapeDtypeStruct((N_DEV,) + a.shape, BF16) for a in arrays],
        scratch_shapes=[pltpu.VMEM(a.shape, BF16) for a in arrays] + [pltpu.SemaphoreType.DMA((n,))],
        compiler_params=_params(),
    )(*arrays)


def all_gather(name, arrays):
    n = len(arrays)

    def body(*refs):
        ins, outs = refs[:n], refs[n:2 * n]
        send_sems, recv_sems, local_sems = refs[2 * n:]
        x, y, c, chips = _place()
        me = 4 * x + 2 * y + c
        sibling = (x, y, 1 - c)

        def copy(a, k, block, to, src=None):
            return pltpu.make_async_remote_copy(
                src_ref=outs[a].at[block] if src is None else src, dst_ref=outs[a].at[block],
                send_sem=send_sems.at[a, k], recv_sem=recv_sems.at[a, k],
                device_id=to, device_id_type=MESH)

        started = []
        local = [pltpu.make_async_copy(ins[a], outs[a].at[me], local_sems.at[a]) for a in range(n)]
        for a in range(n):
            local[a].start()
            sends = [copy(a, 0, me, sibling, src=ins[a])]
            sends += [copy(a, 1 + j, me, (cx, cy, c), src=ins[a]) for j, (cx, cy) in enumerate(chips)]
            for cp in sends:
                cp.start()
            started += sends
        for a in range(n):
            for j, (cx, cy) in enumerate(chips):
                block = 4 * cx + 2 * cy + c
                copy(a, 1 + j, block, sibling).wait_recv()
                passed = copy(a, 4 + j, block, sibling)
                passed.start()
                started.append(passed)
        for a in range(n):
            copy(a, 0, 4 * x + 2 * y + (1 - c), sibling).wait_recv()
            for j, (cx, cy) in enumerate(chips):
                copy(a, 4 + j, 4 * cx + 2 * cy + (1 - c), sibling).wait_recv()
        for cp in started:
            cp.wait_send()
        for cp in local:
            cp.wait()

    return pl.pallas_call(
        body, name=name,
        in_specs=[pl.BlockSpec(memory_space=pltpu.VMEM)] * n, out_specs=[HBM_SPEC] * n,
        out_shape=[jax.ShapeDtypeStruct((N_DEV,) + a.shape, a.dtype) for a in arrays],
        scratch_shapes=[pltpu.SemaphoreType.DMA((n, 7)), pltpu.SemaphoreType.DMA((n, 7)),
                        pltpu.SemaphoreType.DMA((n,))],
    )(*arrays)


def _gather_first_plan(n):
    def plan(zones):
        x, y, c, chips = _place()
        me = 4 * x + 2 * y + c
        copies = []
        for zone in zones[:n]:
            copies.append((zone.at[me], zone.at[me], (x, y, 1 - c), zone.at[4 * x + 2 * y + (1 - c)]))
            copies += [(zone.at[me], zone.at[me], (cx, cy, c), zone.at[4 * cx + 2 * cy + c]) for cx, cy in chips]
        return copies
    return plan


def _gather_second_plan(n):
    def plan(zones):
        x, y, c, chips = _place()
        return [(zone.at[4 * cx + 2 * cy + c], zone.at[4 * cx + 2 * cy + c], (x, y, 1 - c),
                 zone.at[4 * cx + 2 * cy + (1 - c)]) for zone in zones[:n] for cx, cy in chips]
    return plan


def _both_plans(first, n_first, second):
    return lambda bufs: first(bufs[:n_first]) + second(bufs[n_first:])


def _sibling_plan(n):
    def plan(bufs):
        x, y, c, _ = _place()
        return [(bufs[a].at[:, 1 - c], bufs[n + a], (x, y, 1 - c), bufs[n + a]) for a in range(n)]
    return plan


def _chips_plan(n):
    def plan(bufs):
        _, _, c, chips = _place()
        return [(bufs[a].at[2 * cx + cy], bufs[n + a].at[j], (cx, cy, c), bufs[n + a].at[j])
                for a in range(n) for j, (cx, cy) in enumerate(chips)]
    return plan


def chip_sum(name, grad, landed, place, tr=256):
    _, _, rows, cols = grad.shape
    tr = min(tr, rows)

    def body(place_ref, g_ref, l_ref, own_ref, out_ref):
        total = g_ref[0, 0] + l_ref[0]
        out_ref[0] = total.astype(BF16)

        @pl.when(pl.program_id(1) == place_ref[1])
        def _():
            own_ref[...] = total

    return pl.pallas_call(
        body, name=name,
        grid_spec=pltpu.PrefetchScalarGridSpec(
            num_scalar_prefetch=1, grid=(rows // tr, 4),
            in_specs=[pl.BlockSpec((1, 1, tr, cols), lambda i, ch, pos: (ch, pos[0], i, 0)),
                      pl.BlockSpec((1, tr, cols), lambda i, ch, pos: (ch, i, 0))],
            out_specs=[pl.BlockSpec((tr, cols), lambda i, ch, pos: (i, 0)),
                       pl.BlockSpec((1, tr, cols), lambda i, ch, pos: (ch, i, 0))]),
        out_shape=[jax.ShapeDtypeStruct((rows, cols), F32), jax.ShapeDtypeStruct((4, rows, cols), BF16)],
        compiler_params=_params(("arbitrary", "arbitrary")),
    )(place, grad, landed)


BIG =("w_in", "w_sb_proj", "w_ca_proj", "w_out", "w_mlp_in", "w_mlp_out")
SMALL = ("b_gate", "rel_bias", "ln1_g", "ln1_b", "ln2_g", "ln2_b")
NAMES = ("w_in", "b_gate", "w_sb_proj", "w_ca_proj", "rel_bias", "w_out", "ln1_g", "ln1_b",
         "w_mlp_in", "w_mlp_out", "ln2_g", "ln2_b")
REL_COLS = 2 * REL_CLIP + 1


def _pack_small(t, scalar=None):
    rel = t["rel_bias"]
    if rel.shape[1] != REL_PAD:
        rel = jnp.pad(rel, ((0, 0), (0, REL_PAD - rel.shape[1])))
    last = jnp.zeros((128,), F32) if scalar is None else jnp.pad(scalar.reshape(1), (0, 127))
    flat = [t["b_gate"].reshape(-1), rel.reshape(-1)] + [t[n].reshape(-1) for n in SMALL[2:]] + [last]
    return jnp.concatenate(flat).reshape(-1, 128)


def _unpack_small(p):
    flat = p.reshape(-1)
    out, off = {}, 0
    for n, size in (("b_gate", GATE_COLS), ("rel_bias", 8 * REL_PAD)) + tuple((n, D_MODEL) for n in SMALL[2:]):
        out[n] = flat[off:off + size]
        off += size
    out["rel_bias"] = out["rel_bias"].reshape(8, REL_PAD)[:, :REL_COLS]
    return out


def kernel(x, w_in, b_gate, w_sb_proj, w_ca_proj, rel_bias, w_out, ln1_g, ln1_b, w_mlp_in, w_mlp_out, ln2_g, ln2_b, loss_target, m_w_in, m_b_gate, m_w_sb_proj, m_w_ca_proj, m_rel_bias, m_w_out, m_ln1_g, m_ln1_b, m_w_mlp_in, m_w_mlp_out, m_ln2_g, m_ln2_b, v_w_in, v_b_gate, v_w_sb_proj, v_w_ca_proj, v_rel_bias, v_w_out, v_ln1_g, v_ln1_b, v_w_mlp_in, v_w_mlp_out, v_ln2_g, v_ln2_b):
    w = dict(w_in=w_in, b_gate=b_gate, w_sb_proj=w_sb_proj, w_ca_proj=w_ca_proj, rel_bias=rel_bias, w_out=w_out,
             ln1_g=ln1_g, ln1_b=ln1_b, w_mlp_in=w_mlp_in, w_mlp_out=w_mlp_out, ln2_g=ln2_g, ln2_b=ln2_b)
    m = dict(w_in=m_w_in, b_gate=m_b_gate, w_sb_proj=m_w_sb_proj, w_ca_proj=m_w_ca_proj, rel_bias=m_rel_bias,
             w_out=m_w_out, ln1_g=m_ln1_g, ln1_b=m_ln1_b, w_mlp_in=m_w_mlp_in, w_mlp_out=m_w_mlp_out,
             ln2_g=m_ln2_g, ln2_b=m_ln2_b)
    v = dict(w_in=v_w_in, b_gate=v_b_gate, w_sb_proj=v_w_sb_proj, w_ca_proj=v_w_ca_proj, rel_bias=v_rel_bias,
             w_out=v_w_out, ln1_g=v_ln1_g, ln1_b=v_ln1_b, w_mlp_in=v_w_mlp_in, w_mlp_out=v_w_mlp_out,
             ln2_g=v_ln2_g, ln2_b=v_ln2_b)

    x, target = x[0], loss_target[0]
    c = lax.axis_index("c")
    place = jnp.stack([c, 2 * lax.axis_index("x") + lax.axis_index("y")]).astype(jnp.int32)
    grads, deltas, new_m, new_v = {}, {}, {}, {}

    def by_owner(n, g):
        return g.reshape((4, 2) + w[n].shape)

    def finish(n, own, arrived, guest=None):
        out = adamw("adamw_" + n, w[n], m[n], v[n], [(own, None), (arrived, 0), (arrived, 1), (arrived, 2)],
                    tm=256, guest=guest)
        out, bufs = (out, None) if guest is None else out
        grads[n], deltas[n], new_m[n], new_v[n] = out
        return bufs

    def siblings(names, g):
        bufs = [by_owner(n, g[n]) for n in names] + [lax.empty((4,) + w[n].shape, F32) for n in names]
        return Exchange(tuple(bufs), _sibling_plan(len(names)), len(names))

    def chip_sums(names, bufs):
        k = len(names)
        own, sums = zip(*[chip_sum("chip_sum_" + n, bufs[a], bufs[k + a], place) for a, n in enumerate(names)])
        zones = [lax.empty((3,) + w[n].shape, BF16) for n in names]
        return own, Exchange(tuple(sums) + tuple(zones), _chips_plan(k), 3 * k)

    proj, ffn = ("w_sb_proj", "w_ca_proj", "w_out"), ("w_mlp_in", "w_mlp_out")
    zones = place_shards([w[n] for n in proj + ffn])
    proj_zones, ffn_zones = tuple(zones[:len(proj)]), tuple(zones[len(proj):])
    wg_in = all_gather("gather_w_in", [w_in.astype(BF16)])[0]
    (qkv, lg, xb), proj_zones = in_proj(
        x, wg_in, guest=Exchange(proj_zones, _gather_first_plan(len(proj)), 4 * len(proj)))
    bias_tiles = ca_bias_tiles(rel_bias)
    both = Exchange(ffn_zones + tuple(proj_zones),
                    _both_plans(_gather_first_plan(len(ffn)), len(ffn), _gather_second_plan(len(proj))),
                    4 * len(ffn) + 3 * len(proj))
    (o_sb,), zones = sb_fwd(qkv, both)
    ffn_zones, proj_zones = tuple(zones[:len(ffn)]), zones[len(ffn):]
    (o_ca,), ffn_zones = ca_fwd(qkv, bias_tiles, Exchange(ffn_zones, _gather_second_plan(len(ffn)), 3 * len(ffn)))
    wg = dict(zip(proj + ffn, list(proj_zones) + list(ffn_zones)))
    wg_sb, wg_ca, wg_mi = wg["w_sb_proj"], wg["w_ca_proj"], wg["w_mlp_in"]
    wg_mo = wg["w_mlp_out"].reshape(D_FF, D_MODEL)
    w_out_b = wg["w_out"].reshape(D_MODEL, D_MODEL)

    u1 = mix_fwd(x, o_sb, o_ca, lg, b_gate, wg_sb, wg_ca, w_out_b)
    hm, a, x1b, du2, du2b, loss, dg2, db2 = mlp_fwd(u1, target, wg_mi, wg_mo, ln1_g, ln1_b, ln2_g, ln2_b)
    dhm, du1, dg1, db1 = mlp_bwd(du2, hm, u1, wg_mi, wg_mo, ln1_g)
    g = {}
    g["w_mlp_in"] = wgrad("wgrad_mlp_in", x1b, dhm, tm=D_MODEL, tn=FF_SHARD, shard_cols=FF_SHARD)
    g["w_mlp_out"] = wgrad("wgrad_mlp_out", a, du2b, tm=FF_SHARD, tn=D_MODEL)

    mlp = ("w_mlp_in", "w_mlp_out")
    (dlg, do_sb, do_ca, dy_sb, dy_ca, merged, dbg), bufs = mix_bwd(
        du1, o_sb, o_ca, lg, b_gate, wg_sb, wg_ca, w_out_b, guest=siblings(mlp, g))
    own_mlp, mlp_chips = chip_sums(mlp, bufs)
    g["w_out"] = wgrad("wgrad_out", merged, du1, tm=D_MODEL, tn=D_MODEL)
    g["w_sb_proj"] = wgrad("wgrad_sb_proj", o_sb, dy_sb, tm=ATT_WIDTH, tn=D_MODEL, shard_cols=PROJ_SHARD)
    g["w_ca_proj"] = wgrad("wgrad_ca_proj", o_ca, dy_ca, tm=ATT_WIDTH, tn=D_MODEL, shard_cols=PROJ_SHARD)

    att = ("w_out", "w_sb_proj", "w_ca_proj")
    att_siblings = siblings(att, g)
    both = Exchange(mlp_chips.bufs + att_siblings.bufs,
                    _both_plans(mlp_chips.plan, len(mlp_chips.bufs), att_siblings.plan),
                    mlp_chips.copies + att_siblings.copies)
    (dq_sb, dk_sb, dv_sb), bufs = sb_bwd(qkv, o_sb, do_sb, guest=both)
    arrived_mlp, bufs = bufs[len(mlp):len(mlp_chips.bufs)], bufs[len(mlp_chips.bufs):]
    own_att, att_chips = chip_sums(att, bufs)
    (dq_ca, dk_ca, dv_ca, db_tiles), bufs = ca_bwd(qkv, bias_tiles, do_ca, guest=att_chips)
    arrived_att = bufs[len(att):]
    g_rel = rel_bias_grad(db_tiles)

    dh = dh_slab([dq_sb, dk_sb, dv_sb, dq_ca, dk_ca, dv_ca, dlg])
    g["w_in"] = wgrad("wgrad_in", xb, dh, tm=D_MODEL, tn=2 * IN_SHARD, shard_cols=IN_SHARD)
    last = ("w_in",)
    bufs = finish(mlp[0], own_mlp[0], arrived_mlp[0], guest=siblings(last, g))
    finish(mlp[1], own_mlp[1], arrived_mlp[1])
    for n, own, arrived in zip(att, own_att, arrived_att):
        finish(n, own, arrived)
    own_in, in_chips = chip_sums(last, bufs)
    (grad_x,), bufs = in_bwd(du1, dh, wg_in, guest=in_chips)
    finish(last[0], own_in[0], bufs[1])

    small = dict(b_gate=dbg, rel_bias=g_rel, ln1_g=dg1, ln1_b=db1, ln2_g=dg2, ln2_b=db2)
    parts = all_gather("gather_small_grads", [_pack_small(small, loss[0, 0])])[0]
    packed = adamw("adamw_small", _pack_small(w), _pack_small(m), _pack_small(v),
                   [(parts, d) for d in range(N_DEV)])
    loss = packed[0][-1, 0]
    for out, p in zip((grads, deltas, new_m, new_v), packed):
        for n, val in _unpack_small(p).items():
            out[n] = val.reshape(w[n].shape)

    return (loss, grad_x[None], *[grads[n] for n in NAMES], *[deltas[n] for n in NAMES],
            *[new_m[n] for n in NAMES], *[new_v[n] for n in NAMES])
```

```python
import functools
from typing import NamedTuple

import jax
import jax.numpy as jnp
from jax import lax
from jax.experimental import pallas as pl
from jax.experimental.pallas import tpu as pltpu

F32 = jnp.float32
BF16 = jnp.bfloat16
MESH = pl.DeviceIdType.MESH

N_DEV = 8
D_MODEL = 1024
HEAD_DIM = 64
ATT_WIDTH = 512
N_PAIRS = ATT_WIDTH // 128
QKV_COLS = 6 * ATT_WIDTH
GATE_COLS = 2 * D_MODEL
IN_COLS = QKV_COLS + GATE_COLS
IN_SHARD = IN_COLS // N_DEV
D_FF = 4 * D_MODEL
FF_SHARD = D_FF // N_DEV
PROJ_SHARD = D_MODEL // N_DEV
ATT_BLOCK = 128
CA_TILES = 5
CA_ROWS = 1024
CHUNK = 64
CA_PREV_CHUNKS = 8
REL_CLIP = 256
REL_PAD = 640
ALPHA = 2.0 ** 0.25
LN_EPS = 1e-5
QK_SCALE = HEAD_DIM ** -0.5
NEG_BIG = -1e30
VMEM_LIMIT = 56 * 1024 * 1024

ADAM_LR = 0.001
ADAM_B1 = 0.9
ADAM_B2 = 0.999
ADAM_EPS = 1e-08
ADAM_WD = 0.01
ADAM_STEP = 10

_NT = (((1,), (1,)), ((), ()))
_TN = (((0,), (0,)), ((), ()))


def _dot(a, b):
    return jnp.dot(a, b, preferred_element_type=F32)


def _dot_nt(a, b):
    return lax.dot_general(a, b, _NT, preferred_element_type=F32)


def _dot_tn(a, b):
    return lax.dot_general(a, b, _TN, preferred_element_type=F32)


def _params(semantics=None):
    return pltpu.CompilerParams(dimension_semantics=semantics, vmem_limit_bytes=VMEM_LIMIT)


def _row(v):
    return v.reshape(1, -1)


HBM_SPEC = pl.BlockSpec(memory_space=pl.ANY)


class Exchange(NamedTuple):
    bufs: tuple
    plan: object
    copies: int


def _pallas(body, guest, args, *, grid, in_specs, out_specs, out_shape, scratch_shapes=(), **kw):
    out_specs, out_shape, scratch_shapes = list(out_specs), list(out_shape), list(scratch_shapes)
    if guest is None:
        return pl.pallas_call(body, grid=grid, in_specs=in_specs, out_specs=out_specs, out_shape=out_shape,
                              scratch_shapes=scratch_shapes, **kw)(*args)
    nb, n_own = len(guest.bufs), len(in_specs) + len(out_specs)

    def hosting(*refs):
        zones = refs[nb + n_own:2 * nb + n_own]
        send_sems, recv_sems = refs[-2], refs[-1]
        steps = [pl.program_id(axis) for axis in range(len(grid))]
        first = functools.reduce(jnp.logical_and, [step == 0 for step in steps])
        last = functools.reduce(jnp.logical_and, [step == n - 1 for step, n in zip(steps, grid)])

        def copy(k, src, dst, peer):
            return pltpu.make_async_remote_copy(src_ref=src, dst_ref=dst, send_sem=send_sems.at[k],
                                                recv_sem=recv_sems.at[k], device_id=peer, device_id_type=MESH)

        @pl.when(first)
        def _():
            for k, (src, dst, peer, _) in enumerate(guest.plan(zones)):
                copy(k, src, dst, peer).start()

        body(*refs[nb:nb + n_own], *refs[2 * nb + n_own:-2])

        @pl.when(last)
        def _():
            for k, (src, dst, peer, landed) in enumerate(guest.plan(zones)):
                copy(k, src, dst, peer).wait_send()
                copy(k, src, landed, peer).wait_recv()

    sems = pltpu.SemaphoreType.DMA((guest.copies,))
    results = pl.pallas_call(
        hosting, grid=grid, in_specs=[HBM_SPEC] * nb + list(in_specs), out_specs=out_specs + [HBM_SPEC] * nb,
        out_shape=out_shape + [jax.ShapeDtypeStruct(b.shape, b.dtype) for b in guest.bufs],
        input_output_aliases={k: len(out_specs) + k for k in range(nb)},
        scratch_shapes=scratch_shapes + [sems, sems], **kw)(*guest.bufs, *args)
    return results[:len(out_specs)], results[len(out_specs):]


IN_PAIR = 2 * IN_SHARD
IN_PAIR_SCRATCH = [pltpu.VMEM((N_DEV // 2, D_MODEL, IN_PAIR), BF16), pltpu.SemaphoreType.DMA((N_DEV,))]


def _load_in_pairs(w_hbm, pairs_s, sems):
    copies = [pltpu.make_async_copy(w_hbm.at[j], pairs_s.at[j // 2, :, pl.ds(IN_SHARD * (j % 2), IN_SHARD)],
                                    sems.at[j]) for j in range(N_DEV)]
    for cp in copies:
        cp.start()
    for cp in copies:
        cp.wait()


def in_proj(x, wg_in, tm=512, guest=None):
    s = x.shape[0]

    def body(x_ref, w_hbm, qkv_ref, lg_ref, xb_ref, pairs_s, sems):
        @pl.when(pl.program_id(0) == 0)
        def _():
            _load_in_pairs(w_hbm, pairs_s, sems)

        xb = x_ref[...].astype(BF16)
        xb_ref[...] = xb
        for j in range(N_DEV // 2):
            acc = _dot(xb, pairs_s[j])
            lo, hi = IN_PAIR * j, IN_PAIR * (j + 1)
            if hi <= QKV_COLS:
                qkv_ref[:, lo:hi] = acc.astype(BF16)
            elif lo >= QKV_COLS:
                lg_ref[:, lo - QKV_COLS:hi - QKV_COLS] = acc
            else:
                qkv_ref[:, lo:QKV_COLS] = acc[:, :QKV_COLS - lo].astype(BF16)
                lg_ref[:, 0:hi - QKV_COLS] = acc[:, QKV_COLS - lo:]

    return _pallas(
        body, guest, (x, wg_in), name="in_proj", grid=(s // tm,), scratch_shapes=IN_PAIR_SCRATCH,
        in_specs=[pl.BlockSpec((tm, D_MODEL), lambda i: (i, 0)), HBM_SPEC],
        out_specs=[pl.BlockSpec((tm, QKV_COLS), lambda i: (i, 0)),
                   pl.BlockSpec((tm, GATE_COLS), lambda i: (i, 0)),
                   pl.BlockSpec((tm, D_MODEL), lambda i: (i, 0))],
        out_shape=[jax.ShapeDtypeStruct((s, QKV_COLS), BF16),
                   jax.ShapeDtypeStruct((s, GATE_COLS), F32),
                   jax.ShapeDtypeStruct((s, D_MODEL), BF16)],
        compiler_params=_params(("arbitrary",)),
    )


def _head_masks():
    lane = lax.broadcasted_iota(jnp.int32, (1, 128), 1)
    first = lane < HEAD_DIM
    return first, jnp.logical_not(first)


SB_CHUNK = 512
SB_SUB = SB_CHUNK // ATT_BLOCK


def _tri_pair():
    row = lax.broadcasted_iota(jnp.int32, (4 * ATT_BLOCK, 2 * ATT_BLOCK), 0) & (2 * ATT_BLOCK - 1)
    col = lax.broadcasted_iota(jnp.int32, (4 * ATT_BLOCK, 2 * ATT_BLOCK), 1)
    same = (row >= ATT_BLOCK) == (col >= ATT_BLOCK)
    return jnp.logical_and(same, row >= col).astype(BF16)


def _pair_bf16(pieces):
    return jnp.concatenate(
        [jnp.concatenate([p.astype(BF16) for p in row], axis=1) for row in pieces], axis=0)


def _suffix_sums(pieces, tri_pair):
    hi = [[p.astype(BF16) for p in row] for row in pieces]
    lo = [[p - h.astype(F32) for p, h in zip(row, hrow)] for row, hrow in zip(pieces, hi)]
    return _dot(jnp.concatenate([_pair_bf16(hi), _pair_bf16(lo)], axis=1), tri_pair)


def _first_row_piece(diag, j):
    return j if diag else 0


def _piece(v, r, h):
    return v[ATT_BLOCK * r:ATT_BLOCK * (r + 1), ATT_BLOCK * h:ATT_BLOCK * (h + 1)]


def _all_row_pieces(slabs):
    return [[slab[ATT_BLOCK * r:ATT_BLOCK * (r + 1)] for r in range(SB_SUB)] for slab in slabs]


def _add_rows(total, part, r0):
    if r0 == 0:
        return total + part
    return jnp.concatenate([total[:ATT_BLOCK * r0], total[ATT_BLOCK * r0:] + part], axis=0)


def _row_totals(csum, r0, h):
    return jnp.concatenate(
        [jnp.broadcast_to(_piece(csum, at, h)[:, 0:1], (ATT_BLOCK, ATT_BLOCK)) for at in range(SB_SUB - r0)], axis=0)


SIGN_BIT = 0x80000000


def _drop(z):
    minus_abs = lax.bitcast_convert_type(lax.bitcast_convert_type(z, jnp.uint32) | jnp.uint32(SIGN_BIT), F32)
    return jnp.maximum(z, 0.0) + jnp.log(1.0 + jnp.exp(minus_abs))


def _sb_weights(z, runs, r0, masked, strict, tri_pair):
    drops = []
    for r in range(r0, SB_SUB):
        row = [_drop(_piece(z, r - r0, h)) for h in range(2)]
        drops.append([jnp.where(strict, drop, 0.0) for drop in row] if r == masked else row)
    csum = _suffix_sums(drops, tri_pair)
    weights = []
    for r in range(r0, SB_SUB):
        row = [jnp.exp(_piece(z, r - r0, h) - (_piece(csum, r - r0, h) + runs[h][r])) for h in range(2)]
        weights.append([jnp.where(strict, a, 0.0) for a in row] if r == masked else row)
    return drops, weights, csum


EXP_UNDERFLOW = 104.0


SB_STEP = 2


def _sweep_left(span, i, dropped_ref):
    def more(at):
        return jnp.logical_and(at >= 0, jnp.min(dropped_ref[...]) < EXP_UNDERFLOW)

    def step(at):
        span(at * (SB_STEP * ATT_BLOCK), SB_STEP, False)
        return at - 1

    lax.while_loop(more, step, i * (SB_SUB // SB_STEP) - 1)


def _by_head(blk, masks):
    zero = jnp.zeros_like(blk)
    return jnp.concatenate([jnp.where(m, blk, zero) for m in masks], axis=0)


def sb_fwd(qkv, guest=None):
    s = qkv.shape[0]
    nq = s // SB_CHUNK

    def body(q_ref, k_ref, v_ref, o_ref, run_s):
        i = pl.program_id(1)
        masks = _head_masks()
        q = q_ref[...] * QK_SCALE
        row = lax.broadcasted_iota(jnp.int32, (ATT_BLOCK, ATT_BLOCK), 0)
        col = lax.broadcasted_iota(jnp.int32, (ATT_BLOCK, ATT_BLOCK), 1)
        strict = col < row
        tri_pair = _tri_pair()
        run_s[...] = jnp.zeros_like(run_s)
        o_ref[...] = jnp.zeros_like(o_ref)

        def span(start, pieces, diag):
            runs = [run_s[0], run_s[1]]
            acc = jnp.zeros((SB_CHUNK, ATT_BLOCK), F32)
            for j in reversed(range(pieces)):
                r0 = _first_row_piece(diag, j)
                keys = pl.ds(pl.multiple_of(start + ATT_BLOCK * j, ATT_BLOCK), ATT_BLOCK)
                z = _dot_nt(q[ATT_BLOCK * r0:], _by_head(k_ref[keys, :], masks))
                _, weights, csum = _sb_weights(z, _all_row_pieces(runs), r0, j if diag else None, strict, tri_pair)
                part = _dot(_pair_bf16(weights), _by_head(v_ref[keys, :], masks))
                acc = _add_rows(acc, part, r0)
                runs = [_add_rows(runs[h], _row_totals(csum, r0, h), r0) for h in range(2)]
            run_s[0], run_s[1] = runs
            o_ref[...] += acc

        span(i * SB_CHUNK, SB_SUB, True)
        _sweep_left(span, i, run_s)

    return _pallas(
        body, guest, (qkv, qkv, qkv), name="sb_fwd", grid=(N_PAIRS, nq),
        in_specs=[pl.BlockSpec((SB_CHUNK, 128), lambda p, i: (i, p)),
                  pl.BlockSpec((s, 128), lambda p, i: (0, N_PAIRS + p)),
                  pl.BlockSpec((s, 128), lambda p, i: (0, 2 * N_PAIRS + p))],
        out_specs=[pl.BlockSpec((SB_CHUNK, 128), lambda p, i: (i, p))],
        out_shape=[jax.ShapeDtypeStruct((s, ATT_WIDTH), F32)],
        scratch_shapes=[pltpu.VMEM((2, SB_CHUNK, 128), F32)],
        compiler_params=_params(("arbitrary", "arbitrary")),
    )


def sb_bwd(qkv, o_sb, do_sb, guest=None):
    s = qkv.shape[0]
    nq = s // SB_CHUNK

    def body(q_ref, k_ref, v_ref, o_ref, do_ref, dq_ref, dk_ref, dv_ref, run_l_s, run_e_s, etot_s, dq_s, dk_s, dv_s):
        i = pl.program_id(1)

        @pl.when(i == 0)
        def _():
            dk_s[...] = jnp.zeros_like(dk_s)
            dv_s[...] = jnp.zeros_like(dv_s)

        masks = _head_masks()
        q = q_ref[...] * QK_SCALE
        do = do_ref[...]
        prod = do.astype(F32) * o_ref[...]
        for h in range(2):
            total = jnp.sum(jnp.where(masks[h], prod, 0.0), axis=1, keepdims=True)
            etot_s[h] = jnp.broadcast_to(total, (SB_CHUNK, ATT_BLOCK))
        row = lax.broadcasted_iota(jnp.int32, (ATT_BLOCK, ATT_BLOCK), 0)
        col = lax.broadcasted_iota(jnp.int32, (ATT_BLOCK, ATT_BLOCK), 1)
        strict = col < row
        tri_pair = _tri_pair()
        run_l_s[...] = jnp.zeros_like(run_l_s)
        run_e_s[...] = jnp.zeros_like(run_e_s)
        dq_s[...] = jnp.zeros_like(dq_s)

        def span(start, pieces, diag):
            slabs_l = [run_l_s[0], run_l_s[1]]
            slabs_e = [run_e_s[0], run_e_s[1]]
            e_tot = _all_row_pieces([etot_s[0], etot_s[1]])
            dq = jnp.zeros((SB_CHUNK, ATT_BLOCK), F32)
            for j in reversed(range(pieces)):
                runs_l, runs_e = _all_row_pieces(slabs_l), _all_row_pieces(slabs_e)
                r0 = _first_row_piece(diag, j)
                masked = j if diag else None
                keys = pl.ds(pl.multiple_of(start + ATT_BLOCK * j, ATT_BLOCK), ATT_BLOCK)
                kk = _by_head(k_ref[keys, :], masks)
                vv = _by_head(v_ref[keys, :], masks)
                q_rows, do_rows = q[ATT_BLOCK * r0:], do[ATT_BLOCK * r0:]
                z = _dot_nt(q_rows, kk)
                da = _dot_nt(do_rows, vv)
                drops, weights, csum = _sb_weights(z, runs_l, r0, masked, strict, tri_pair)
                ab = [[a.astype(BF16) for a in wrow] for wrow in weights]
                es = [[a.astype(F32) * _piece(da, at, h) for h, a in enumerate(arow)] for at, arow in enumerate(ab)]
                esum = _suffix_sums(es, tri_pair)
                dzs = []
                for at, r in enumerate(range(r0, SB_SUB)):
                    dz_row = []
                    for h in range(2):
                        sig = jnp.exp(_piece(z, at, h) - drops[at][h])
                        after = _piece(esum, at, h) + runs_e[h][r] - es[at][h]
                        dz = es[at][h] - sig * (e_tot[h][r] - after)
                        dz_row.append(jnp.where(strict, dz, 0.0) if r == masked else dz)
                    dzs.append(dz_row)
                dz = _pair_bf16(dzs)
                dq = _add_rows(dq, _dot(dz, kk), r0)
                dk = _dot_tn(dz, q_rows)
                dv = _dot_tn(_pair_bf16(ab), do_rows)
                dk_s[keys, :] += jnp.where(masks[0], dk[:ATT_BLOCK], dk[ATT_BLOCK:])
                dv_s[keys, :] += jnp.where(masks[0], dv[:ATT_BLOCK], dv[ATT_BLOCK:])
                slabs_l =[_add_rows(slabs_l[h], _row_totals(csum, r0, h), r0) for h in range(2)]
                slabs_e = [_add_rows(slabs_e[h], _row_totals(esum, r0, h), r0) for h in range(2)]
            run_l_s[0], run_l_s[1] = slabs_l
            run_e_s[0], run_e_s[1] = slabs_e
            dq_s[...] += dq

        span(i * SB_CHUNK, SB_SUB, True)
        _sweep_left(span, i, run_l_s)
        dq_ref[...] = (dq_s[...] * QK_SCALE).astype(BF16)

        @pl.when(i == nq - 1)
        def _():
            dk_ref[...] = dk_s[...].astype(BF16)
            dv_ref[...] = dv_s[...].astype(BF16)

    tile = lambda p, i: (i, p)
    state = pltpu.VMEM((2, SB_CHUNK, 128), F32)
    sums = pltpu.VMEM((s, 128), F32)
    return _pallas(
        body, guest, (qkv, qkv, qkv, o_sb, do_sb), name="sb_bwd", grid=(N_PAIRS, nq),
        in_specs=[pl.BlockSpec((SB_CHUNK, 128), tile),
                  pl.BlockSpec((s, 128), lambda p, i: (0, N_PAIRS + p)),
                  pl.BlockSpec((s, 128), lambda p, i: (0, 2 * N_PAIRS + p)),
                  pl.BlockSpec((SB_CHUNK, 128), tile),
                  pl.BlockSpec((SB_CHUNK, 128), tile)],
        out_specs=[pl.BlockSpec((SB_CHUNK, 128), tile),
                   pl.BlockSpec((s, 128), lambda p, i: (0, p)),
                   pl.BlockSpec((s, 128), lambda p, i: (0, p))],
        out_shape=[jax.ShapeDtypeStruct((s, ATT_WIDTH), BF16)] * 3,
        scratch_shapes=[state, state, state, pltpu.VMEM((SB_CHUNK, 128), F32), sums, sums],
        compiler_params=_params(("arbitrary", "arbitrary")),
    )


def ca_bias_tiles(rel_bias):
    heads = rel_bias.shape[0]
    wide = 2 * ATT_BLOCK

    def body(rel_ref, out_ref):
        bins = lax.broadcasted_iota(jnp.int32, (REL_PAD, wide), 0)
        pos = lax.broadcasted_iota(jnp.int32, (REL_PAD, wide), 1)
        r = lax.broadcasted_iota(jnp.int32, (ATT_BLOCK, ATT_BLOCK), 0)
        c = lax.broadcasted_iota(jnp.int32, (ATT_BLOCK, ATT_BLOCK), 1)
        for j in range(CA_TILES):
            hot = (jnp.clip(ATT_BLOCK * (j + 1) - 1 - pos, -REL_CLIP, REL_CLIP) + REL_CLIP == bins).astype(F32)
            lines = jnp.dot(rel_ref[...], hot, preferred_element_type=F32, precision=lax.Precision.HIGHEST)
            back = 2 * j + (r >> 6) - (c >> 6)
            ok = jnp.logical_and(back >= 0, back <= CA_PREV_CHUNKS)
            for h in range(heads):
                tile = jnp.broadcast_to(lines[h:h + 1, :], (ATT_BLOCK, wide))
                tile = pltpu.roll(tile, wide - (ATT_BLOCK - 1), 1, stride=1, stride_axis=0)
                out_ref[h, j] = jnp.where(ok, tile[:, :ATT_BLOCK], NEG_BIG)

    padded = jnp.pad(rel_bias, ((0, 0), (0, REL_PAD - rel_bias.shape[1])))
    return pl.pallas_call(
        body, name="ca_bias_tiles",
        out_shape=jax.ShapeDtypeStruct((heads, CA_TILES, ATT_BLOCK, ATT_BLOCK), F32),
        compiler_params=_params(),
    )(padded)


CA_BLOCKS = CA_ROWS // ATT_BLOCK
CA_KEY_BLOCKS = CA_BLOCKS + CA_TILES - 1


def _ca_users(t):
    return [r for r in range(CA_BLOCKS) if 0 <= r + CA_TILES - 1 - t < CA_TILES]


def _ca_rows(v, t, r):
    at = _ca_users(t).index(r)
    return v[ATT_BLOCK * at:ATT_BLOCK * (at + 1)]


def _ca_window(step, k_ref, v_ref, masks):
    kks, vvs, inside, rows = [], [], [], []
    for t in range(CA_KEY_BLOCKS):
        block = step * CA_BLOCKS - (CA_TILES - 1) + t
        keys = pl.ds(pl.multiple_of(jnp.maximum(block, 0) * ATT_BLOCK, ATT_BLOCK), ATT_BLOCK)
        kks.append(_by_head(k_ref[keys, :], masks))
        vvs.append(_by_head(v_ref[keys, :], masks))
        inside.append(block >= 0)
        rows.append(keys)
    return kks, vvs, inside, rows


def _ca_probs(q, kks, inside, bias_ref):
    scores = [[[None] * CA_TILES for _ in range(2)] for _ in range(CA_BLOCKS)]
    for t in range(CA_KEY_BLOCKS):
        users = _ca_users(t)
        z = _dot_nt(q[ATT_BLOCK * users[0]:ATT_BLOCK * (users[-1] + 1)], kks[t])
        for r in users:
            j = r + CA_TILES - 1 - t
            for h in range(2):
                zz = _ca_rows(z, t, r)[:, ATT_BLOCK * h:ATT_BLOCK * (h + 1)] + bias_ref[h, j]
                scores[r][h][j] = jnp.where(inside[t], zz, NEG_BIG)
    probs = [[None, None] for _ in range(CA_BLOCKS)]
    for r in range(CA_BLOCKS):
        for h in range(2):
            m = jnp.max(functools.reduce(jnp.maximum, scores[r][h]), axis=1, keepdims=True)
            ex = [jnp.exp(z - m) for z in scores[r][h]]
            inv = 1.0 / jnp.sum(functools.reduce(jnp.add, ex), axis=1, keepdims=True)
            probs[r][h] = [e * inv for e in ex]
    return probs


def _ca_stack(tiles, t):
    return jnp.concatenate(
        [jnp.concatenate([tiles[r][h][r + CA_TILES - 1 - t].astype(BF16) for h in range(2)], axis=1)
         for r in _ca_users(t)], axis=0)


def ca_fwd(qkv, bias_tiles, guest=None):
    s = qkv.shape[0]
    nq = s // CA_ROWS

    def body(q_ref, k_ref, v_ref, bias_ref, o_ref):
        masks = _head_masks()
        kks, vvs, inside, _ = _ca_window(pl.program_id(1), k_ref, v_ref, masks)
        probs = _ca_probs(q_ref[...] * QK_SCALE, kks, inside, bias_ref)
        out = [None] * CA_BLOCKS
        for t in range(CA_KEY_BLOCKS):
            part = _dot(_ca_stack(probs, t), vvs[t])
            for r in _ca_users(t):
                piece = _ca_rows(part, t, r)
                out[r] = piece if out[r] is None else out[r] + piece
        o_ref[...] = jnp.concatenate(out, axis=0).astype(BF16)

    return _pallas(
        body, guest, (qkv, qkv, qkv, bias_tiles), name="ca_fwd", grid=(N_PAIRS, nq),
        in_specs=[pl.BlockSpec((CA_ROWS, 128), lambda p, i: (i, 3 * N_PAIRS + p)),
                  pl.BlockSpec((s, 128), lambda p, i: (0, 4 * N_PAIRS + p)),
                  pl.BlockSpec((s, 128), lambda p, i: (0, 5 * N_PAIRS + p)),
                  pl.BlockSpec((2, CA_TILES, ATT_BLOCK, ATT_BLOCK), lambda p, i: (p, 0, 0, 0))],
        out_specs=[pl.BlockSpec((CA_ROWS, 128), lambda p, i: (i, p))],
        out_shape=[jax.ShapeDtypeStruct((s, ATT_WIDTH), BF16)],
        compiler_params=_params(("arbitrary", "arbitrary")),
    )


def ca_bwd(qkv, bias_tiles, do_ca, guest=None):
    s = qkv.shape[0]
    nq = s // CA_ROWS

    def body(q_ref, k_ref, v_ref, bias_ref, do_ref, dq_ref, dk_ref, dv_ref, db_ref, dk_s, dv_s):
        big = pl.program_id(1)

        @pl.when(big == 0)
        def _():
            dk_s[...] = jnp.zeros_like(dk_s)
            dv_s[...] = jnp.zeros_like(dv_s)
            db_ref[...] = jnp.zeros_like(db_ref)

        masks = _head_masks()
        kks, vvs, inside, key_rows = _ca_window(big, k_ref, v_ref, masks)
        q = q_ref[...] * QK_SCALE
        do = do_ref[...]
        probs = _ca_probs(q, kks, inside, bias_ref)
        dps = [[[None] * CA_TILES for _ in range(2)] for _ in range(CA_BLOCKS)]
        for t in range(CA_KEY_BLOCKS):
            users = _ca_users(t)
            dp = _dot_nt(do[ATT_BLOCK * users[0]:ATT_BLOCK * (users[-1] + 1)], vvs[t])
            for r in users:
                for h in range(2):
                    dps[r][h][r + CA_TILES - 1 - t] = _ca_rows(dp, t, r)[:, ATT_BLOCK * h:ATT_BLOCK * (h + 1)]
        dss = [[None, None] for _ in range(CA_BLOCKS)]
        for r in range(CA_BLOCKS):
            for h in range(2):
                delta = jnp.sum(functools.reduce(jnp.add, [p * dp for p, dp in zip(probs[r][h], dps[r][h])]),
                                axis=1, keepdims=True)
                dss[r][h] = [p * (dp - delta) for p, dp in zip(probs[r][h], dps[r][h])]
        for h in range(2):
            for j in range(CA_TILES):
                db_ref[h, j] += functools.reduce(jnp.add, [dss[r][h][j] for r in range(CA_BLOCKS)])
        dq = [None] * CA_BLOCKS
        for t in range(CA_KEY_BLOCKS):
            users = _ca_users(t)
            rows = slice(ATT_BLOCK * users[0], ATT_BLOCK * (users[-1] + 1))
            ds = _ca_stack(dss, t)
            part = _dot(ds, kks[t])
            for r in users:
                piece = _ca_rows(part, t, r)
                dq[r] = piece if dq[r] is None else dq[r] + piece
            dk = _dot_tn(ds, q[rows])
            dv = _dot_tn(_ca_stack(probs, t), do[rows])
            dk_s[key_rows[t], :] += jnp.where(masks[0], dk[:ATT_BLOCK], dk[ATT_BLOCK:])
            dv_s[key_rows[t], :] += jnp.where(masks[0], dv[:ATT_BLOCK], dv[ATT_BLOCK:])
        dq_ref[...] = (jnp.concatenate(dq, axis=0) * QK_SCALE).astype(BF16)

        @pl.when(big == nq - 1)
        def _():
            dk_ref[...] = dk_s[...].astype(BF16)
            dv_ref[...] = dv_s[...].astype(BF16)

    return _pallas(
        body, guest, (qkv, qkv, qkv, bias_tiles, do_ca), name="ca_bwd", grid=(N_PAIRS, nq),
        in_specs=[pl.BlockSpec((CA_ROWS, 128), lambda p, i: (i, 3 * N_PAIRS + p)),
                  pl.BlockSpec((s, 128), lambda p, i: (0, 4 * N_PAIRS + p)),
                  pl.BlockSpec((s, 128), lambda p, i: (0, 5 * N_PAIRS + p)),
                  pl.BlockSpec((2, CA_TILES, ATT_BLOCK, ATT_BLOCK), lambda p, i: (p, 0, 0, 0)),
                  pl.BlockSpec((CA_ROWS, 128), lambda p, i: (i, p))],
        out_specs=[pl.BlockSpec((CA_ROWS, 128), lambda p, i: (i, p)),
                   pl.BlockSpec((s, 128), lambda p, i: (0, p)),
                   pl.BlockSpec((s, 128), lambda p, i: (0, p)),
                   pl.BlockSpec((2, CA_TILES, ATT_BLOCK, ATT_BLOCK), lambda p, i: (p, 0, 0, 0))],
        out_shape=[jax.ShapeDtypeStruct((s, ATT_WIDTH), BF16)] * 3
                  + [jax.ShapeDtypeStruct((2 * N_PAIRS, CA_TILES, ATT_BLOCK, ATT_BLOCK), F32)],
        scratch_shapes=[pltpu.VMEM((s, 128), F32), pltpu.VMEM((s, 128), F32)],
        compiler_params=_params(("arbitrary", "arbitrary")),
    )


def rel_bias_grad(db_tiles):
    heads = db_tiles.shape[0]

    def body(db_ref, out_ref):
        row = lax.broadcasted_iota(jnp.int32, (ATT_BLOCK, ATT_BLOCK), 0)
        col = lax.broadcasted_iota(jnp.int32, (ATT_BLOCK, ATT_BLOCK), 1)
        flip = (row + col == ATT_BLOCK - 1).astype(F32)
        wrapped = col < row
        cp = lax.broadcasted_iota(jnp.int32, (ATT_BLOCK, REL_PAD), 0)
        bins = lax.broadcasted_iota(jnp.int32, (ATT_BLOCK, REL_PAD), 1)
        total = jnp.zeros((heads, REL_PAD), F32)
        for j in range(CA_TILES):
            sums_neg, sums_pos = [], []
            for h in range(heads):
                tile = jnp.dot(db_ref[h, j], flip, preferred_element_type=F32, precision=lax.Precision.HIGHEST)
                tile = pltpu.roll(tile, 0, 1, stride=1, stride_axis=0)
                sums_neg.append(jnp.sum(jnp.where(wrapped, 0.0, tile), axis=0, keepdims=True))
                sums_pos.append(jnp.sum(jnp.where(wrapped, tile, 0.0), axis=0, keepdims=True))
            neg = jnp.concatenate(sums_neg, axis=0)
            pos = jnp.concatenate(sums_pos, axis=0)
            hot_neg = (jnp.clip(ATT_BLOCK * j + cp - (ATT_BLOCK - 1), -REL_CLIP, REL_CLIP) + REL_CLIP == bins)
            hot_pos = (jnp.clip(ATT_BLOCK * j + cp + 1, -REL_CLIP, REL_CLIP) + REL_CLIP == bins)
            total = total + jnp.dot(neg, hot_neg.astype(F32), preferred_element_type=F32,
                                    precision=lax.Precision.HIGHEST)
            total = total + jnp.dot(pos, hot_pos.astype(F32), preferred_element_type=F32,
                                    precision=lax.Precision.HIGHEST)
        out_ref[...] = total

    return pl.pallas_call(
        body, name="rel_bias_grad",
        out_shape=jax.ShapeDtypeStruct((heads, REL_PAD), F32),
        compiler_params=_params(),
    )(db_tiles)


def _assemble_cols(dst_ref, src_ref):
    w = src_ref.shape[2]
    for j in range(N_DEV):
        dst_ref[:, w * j:w * (j + 1)] = src_ref[j]


def _merge(o_sb, o_ca, lg, bg, wsb, wca):
    y_sb = _dot(o_sb.astype(BF16), wsb)
    y_ca = _dot(o_ca.astype(BF16), wca)
    gates = jax.nn.sigmoid(lg + bg)
    g_sb, g_ca = gates[:, :D_MODEL], gates[:, D_MODEL:]
    return y_sb, y_ca, g_sb, g_ca, g_sb * y_sb + g_ca * y_ca


def mix_fwd(x, o_sb, o_ca, lg, b_gate, wg_sb, wg_ca, w_out, tm=256):
    s = x.shape[0]

    def body(x_ref, osb_ref, oca_ref, lg_ref, bg_ref, wsb_ref, wca_ref, wout_ref, u_ref, wsb_s, wca_s):
        @pl.when(pl.program_id(0) == 0)
        def _():
            _assemble_cols(wsb_s, wsb_ref)
            _assemble_cols(wca_s, wca_ref)

        merged = _merge(osb_ref[...], oca_ref[...], lg_ref[...], bg_ref[...], wsb_s[...], wca_s[...])[4]
        u_ref[...] = ALPHA * x_ref[...] + _dot(merged.astype(BF16), wout_ref[...])

    rows = lambda w: pl.BlockSpec((tm, w), lambda i: (i, 0))
    whole = lambda shape: pl.BlockSpec(shape, lambda i: (0,) * len(shape))
    return pl.pallas_call(
        body, name="mix_fwd", grid=(s // tm,),
        in_specs=[rows(D_MODEL), rows(ATT_WIDTH), rows(ATT_WIDTH), rows(GATE_COLS), whole((1, GATE_COLS)),
                  whole((N_DEV, ATT_WIDTH, PROJ_SHARD)), whole((N_DEV, ATT_WIDTH, PROJ_SHARD)),
                  whole((D_MODEL, D_MODEL))],
        out_specs=rows(D_MODEL),
        out_shape=jax.ShapeDtypeStruct((s, D_MODEL), F32),
        scratch_shapes=[pltpu.VMEM((ATT_WIDTH, D_MODEL), BF16), pltpu.VMEM((ATT_WIDTH, D_MODEL), BF16)],
        compiler_params=_params(("arbitrary",)),
    )(x, o_sb, o_ca, lg, _row(b_gate), wg_sb, wg_ca, w_out)


def mix_bwd(du1, o_sb, o_ca, lg, b_gate, wg_sb, wg_ca, w_out, tm=256, guest=None):
    s = du1.shape[0]

    def body(du_ref, osb_ref, oca_ref, lg_ref, bg_ref, wsb_ref, wca_ref, wout_ref,
             dlg_ref, dosb_ref, doca_ref, dysb_ref, dyca_ref, mg_ref, dbg_ref, wsb_s, wca_s):
        @pl.when(pl.program_id(0) == 0)
        def _():
            _assemble_cols(wsb_s, wsb_ref)
            _assemble_cols(wca_s, wca_ref)
            dbg_ref[...] = jnp.zeros_like(dbg_ref)

        y_sb, y_ca, g_sb, g_ca, merged = _merge(
            osb_ref[...], oca_ref[...], lg_ref[...], bg_ref[...], wsb_s[...], wca_s[...])
        mg_ref[...] = merged.astype(BF16)
        dm = _dot_nt(du_ref[...].astype(BF16), wout_ref[...])
        dl_sb = dm * y_sb * g_sb * (1.0 - g_sb)
        dl_ca = dm * y_ca * g_ca * (1.0 - g_ca)
        dlg_ref[:, :D_MODEL] = dl_sb.astype(BF16)
        dlg_ref[:, D_MODEL:] = dl_ca.astype(BF16)
        dbg_ref[:, :D_MODEL] += jnp.sum(dl_sb, axis=0, keepdims=True)
        dbg_ref[:, D_MODEL:] += jnp.sum(dl_ca, axis=0, keepdims=True)
        dy_sb = (dm * g_sb).astype(BF16)
        dy_ca = (dm * g_ca).astype(BF16)
        dysb_ref[...] = dy_sb
        dyca_ref[...] = dy_ca
        dosb_ref[...] = _dot_nt(dy_sb, wsb_s[...]).astype(BF16)
        doca_ref[...] = _dot_nt(dy_ca, wca_s[...]).astype(BF16)

    rows = lambda w: pl.BlockSpec((tm, w), lambda i: (i, 0))
    whole = lambda shape: pl.BlockSpec(shape, lambda i: (0,) * len(shape))
    return _pallas(
        body, guest, (du1, o_sb, o_ca, lg, _row(b_gate), wg_sb, wg_ca, w_out), name="mix_bwd", grid=(s // tm,),
        in_specs=[rows(D_MODEL), rows(ATT_WIDTH), rows(ATT_WIDTH), rows(GATE_COLS), whole((1, GATE_COLS)),
                  whole((N_DEV, ATT_WIDTH, PROJ_SHARD)), whole((N_DEV, ATT_WIDTH, PROJ_SHARD)),
                  whole((D_MODEL, D_MODEL))],
        out_specs=[rows(GATE_COLS), rows(ATT_WIDTH), rows(ATT_WIDTH), rows(D_MODEL), rows(D_MODEL),
                   rows(D_MODEL), whole((1, GATE_COLS))],
        out_shape=[jax.ShapeDtypeStruct((s, GATE_COLS), BF16),
                   jax.ShapeDtypeStruct((s, ATT_WIDTH), BF16), jax.ShapeDtypeStruct((s, ATT_WIDTH), BF16),
                   jax.ShapeDtypeStruct((s, D_MODEL), BF16), jax.ShapeDtypeStruct((s, D_MODEL), BF16),
                   jax.ShapeDtypeStruct((s, D_MODEL), BF16), jax.ShapeDtypeStruct((1, GATE_COLS), F32)],
        scratch_shapes=[pltpu.VMEM((ATT_WIDTH, D_MODEL), BF16), pltpu.VMEM((ATT_WIDTH, D_MODEL), BF16)],
        compiler_params=_params(("arbitrary",)),
    )


def _ln_stats(u):
    mu = jnp.mean(u, axis=1, keepdims=True)
    cen = u - mu
    var = jnp.mean(cen * cen, axis=1, keepdims=True)
    rstd = lax.rsqrt(var + LN_EPS)
    return cen * rstd, rstd


def _ln_bwd(dy, xhat, rstd, gain):
    dyg = dy * gain
    m1 = jnp.mean(dyg, axis=1, keepdims=True)
    m2 = jnp.mean(dyg * xhat, axis=1, keepdims=True)
    return rstd * (dyg - m1 - xhat * m2)


def _load_mlp_weights(win_hbm, wout_hbm, win_s, wout_s, sems):
    copies = [pltpu.make_async_copy(win_hbm.at[k], win_s.at[:, pl.ds(FF_SHARD * k, FF_SHARD)], sems.at[k])
              for k in range(N_DEV)]
    copies.append(pltpu.make_async_copy(wout_hbm, wout_s, sems.at[N_DEV]))
    for cp in copies:
        cp.start()
    return copies


MLP_WEIGHT_SCRATCH = [pltpu.VMEM((D_MODEL, D_FF), BF16), pltpu.VMEM((D_FF, D_MODEL), BF16),
                      pltpu.SemaphoreType.DMA((N_DEV + 1,))]


def mlp_fwd(u1, target, wg_in, w_out, ln1_g, ln1_b, ln2_g, ln2_b, tm=256):
    s = u1.shape[0]

    def body(u_ref, t_ref, win_hbm, wout_hbm, g1_ref, b1_ref, g2_ref, b2_ref,
             hm_ref, a_ref, x1b_ref, du2_ref, du2b_ref, loss_ref, dg2_ref, db2_ref, win_s, wout_s, sems):
        first = pl.program_id(0) == 0

        @pl.when(first)
        def _():
            copies = _load_mlp_weights(win_hbm, wout_hbm, win_s, wout_s, sems)
            loss_ref[...] = jnp.zeros_like(loss_ref)
            dg2_ref[...] = jnp.zeros_like(dg2_ref)
            db2_ref[...] = jnp.zeros_like(db2_ref)
            for cp in copies:
                cp.wait()

        xhat, _ = _ln_stats(u_ref[...])
        x1 = xhat * g1_ref[...] + b1_ref[...]
        x1b = x1.astype(BF16)
        x1b_ref[...] = x1b
        for k in range(N_DEV):
            cols = slice(FF_SHARD * k, FF_SHARD * (k + 1))
            hm = _dot(x1b, win_s[:, cols])
            hm_ref[:, cols] = hm
            r = jnp.maximum(hm, 0.0)
            a_ref[:, cols] = (r * r).astype(BF16)
        xhat, rstd = _ln_stats(ALPHA * x1 + _dot(a_ref[...], wout_s[...]))
        diff = xhat * g2_ref[...] + b2_ref[...] - t_ref[...]
        per_token = jnp.mean(diff * diff, axis=1, keepdims=True)
        loss_ref[...] += 0.5 * jnp.sum(per_token, axis=0, keepdims=True)
        dy = diff * (1.0 / D_MODEL)
        dg2_ref[...] += jnp.sum(dy * xhat, axis=0, keepdims=True)
        db2_ref[...] += jnp.sum(dy, axis=0, keepdims=True)
        du2 = _ln_bwd(dy, xhat, rstd, g2_ref[...])
        du2_ref[...] = du2
        du2b_ref[...] = du2.astype(BF16)

    rows = pl.BlockSpec((tm, D_MODEL), lambda i: (i, 0))
    vec = pl.BlockSpec((1, D_MODEL), lambda i: (0, 0))
    ff = pl.BlockSpec((tm, D_FF), lambda i: (i, 0))
    return pl.pallas_call(
        body, name="mlp_fwd", grid=(s // tm,),
        in_specs=[rows, rows, HBM_SPEC, HBM_SPEC, vec, vec, vec, vec],
        out_specs=[ff, ff, rows, rows, rows, pl.BlockSpec((1, 1), lambda i: (0, 0)), vec, vec],
        out_shape=[jax.ShapeDtypeStruct((s, D_FF), F32), jax.ShapeDtypeStruct((s, D_FF), BF16),
                   jax.ShapeDtypeStruct((s, D_MODEL), BF16), jax.ShapeDtypeStruct((s, D_MODEL), F32),
                   jax.ShapeDtypeStruct((s, D_MODEL), BF16), jax.ShapeDtypeStruct((1, 1), F32),
                   jax.ShapeDtypeStruct((1, D_MODEL), F32), jax.ShapeDtypeStruct((1, D_MODEL), F32)],
        scratch_shapes=MLP_WEIGHT_SCRATCH,
        compiler_params=_params(("arbitrary",)),
    )(u1, target, wg_in, w_out, _row(ln1_g), _row(ln1_b), _row(ln2_g), _row(ln2_b))


def mlp_bwd(du2, hm, u1, wg_in, w_out, ln1_g, tm=256):
    s = du2.shape[0]

    def body(du2_ref, hm_ref, u_ref, win_hbm, wout_hbm, g1_ref,
             dhm_ref, du1_ref, dg1_ref, db1_ref, win_s, wout_s, sems):
        first = pl.program_id(0) == 0

        @pl.when(first)
        def _():
            copies = _load_mlp_weights(win_hbm, wout_hbm, win_s, wout_s, sems)
            dg1_ref[...] = jnp.zeros_like(dg1_ref)
            db1_ref[...] = jnp.zeros_like(db1_ref)
            for cp in copies:
                cp.wait()

        du2 = du2_ref[...]
        du2b = du2.astype(BF16)
        for k in range(N_DEV):
            cols = slice(FF_SHARD * k, FF_SHARD * (k + 1))
            da = _dot_nt(du2b, wout_s[cols, :])
            dhm_ref[:, cols] = (2.0 * jnp.maximum(hm_ref[:, cols], 0.0) * da).astype(BF16)
        dx1 = ALPHA * du2 + _dot_nt(dhm_ref[...], win_s[...])
        xhat, rstd = _ln_stats(u_ref[...])
        dg1_ref[...] += jnp.sum(dx1 * xhat, axis=0, keepdims=True)
        db1_ref[...] += jnp.sum(dx1, axis=0, keepdims=True)
        du1_ref[...] = _ln_bwd(dx1, xhat, rstd, g1_ref[...])

    rows = pl.BlockSpec((tm, D_MODEL), lambda i: (i, 0))
    vec = pl.BlockSpec((1, D_MODEL), lambda i: (0, 0))
    ff = pl.BlockSpec((tm, D_FF), lambda i: (i, 0))
    return pl.pallas_call(
        body, name="mlp_bwd", grid=(s // tm,),
        in_specs=[rows, ff, rows, HBM_SPEC, HBM_SPEC, vec],
        out_specs=[ff, rows, vec, vec],
        out_shape=[jax.ShapeDtypeStruct((s, D_FF), BF16), jax.ShapeDtypeStruct((s, D_MODEL), F32),
                   jax.ShapeDtypeStruct((1, D_MODEL), F32), jax.ShapeDtypeStruct((1, D_MODEL), F32)],
        scratch_shapes=MLP_WEIGHT_SCRATCH,
        compiler_params=_params(("arbitrary",)),
    )(du2, hm, u1, wg_in, w_out, _row(ln1_g))


def dh_slab(pieces, tm=512):
    s = pieces[0].shape[0]
    widths = [p.shape[1] for p in pieces]
    assert sum(widths) == IN_COLS
    n = len(pieces)

    def body(*refs):
        dh_ref = refs[n]
        off = 0
        for ref, w in zip(refs[:n], widths):
            dh_ref[:, off:off + w] = ref[...].astype(BF16)
            off += w

    rows = lambda w: pl.BlockSpec((tm, w), lambda i: (i, 0))
    return pl.pallas_call(
        body, name="dh_slab", grid=(s // tm,),
        in_specs=[rows(w) for w in widths], out_specs=rows(IN_COLS),
        out_shape=jax.ShapeDtypeStruct((s, IN_COLS), BF16),
        compiler_params=_params(("arbitrary",)),
    )(*pieces)


def in_bwd(du1, dh, wg_in, tm=512, guest=None):
    s = du1.shape[0]

    def body(du_ref, dh_ref, w_hbm, gx_ref, pairs_s, sems):
        @pl.when(pl.program_id(0) == 0)
        def _():
            _load_in_pairs(w_hbm, pairs_s, sems)

        acc = ALPHA * du_ref[...]
        for j in range(N_DEV // 2):
            acc = acc + _dot_nt(dh_ref[:, IN_PAIR * j:IN_PAIR * (j + 1)], pairs_s[j])
        gx_ref[...] = acc

    rows = lambda w: pl.BlockSpec((tm, w), lambda i: (i, 0))
    return _pallas(
        body, guest, (du1, dh, wg_in), name="in_bwd", grid=(s // tm,), scratch_shapes=IN_PAIR_SCRATCH,
        in_specs=[rows(D_MODEL), rows(IN_COLS), HBM_SPEC],
        out_specs=[rows(D_MODEL)],
        out_shape=[jax.ShapeDtypeStruct((s, D_MODEL), F32)],
        compiler_params=_params(("arbitrary",)),
    )


def wgrad(name, a, b, tm, tn, shard_cols=None, tk=2048):
    kk, m = a.shape
    n = b.shape[1]
    while kk % tk:
        tk //= 2
    nk = kk // tk
    w = shard_cols

    def body(a_ref, b_ref, out_ref, acc_s):
        k = pl.program_id(2)

        @pl.when(k == 0)
        def _():
            acc_s[...] = jnp.zeros_like(acc_s)

        acc_s[...] += _dot_tn(a_ref[...].astype(BF16), b_ref[...].astype(BF16))

        @pl.when(k == nk - 1)
        def _():
            if w is None:
                out_ref[...] = acc_s[...]
            else:
                for j in range(tn // w):
                    out_ref[j] = acc_s[:, w * j:w * (j + 1)]

    if w is None:
        out_spec = pl.BlockSpec((tm, tn), lambda i, j, k: (i, j))
        out_shape = jax.ShapeDtypeStruct((m, n), F32)
    else:
        out_spec = pl.BlockSpec((tn // w, tm, w), lambda i, j, k: (j, i, 0))
        out_shape = jax.ShapeDtypeStruct((n // w, m, w), F32)
    return pl.pallas_call(
        body, name=name, grid=(m // tm, n // tn, nk),
        in_specs=[pl.BlockSpec((tk, tm), lambda i, j, k: (k, i)),
                  pl.BlockSpec((tk, tn), lambda i, j, k: (k, j))],
        out_specs=out_spec, out_shape=out_shape,
        scratch_shapes=[pltpu.VMEM((tm, tn), F32)],
        compiler_params=_params(("arbitrary", "arbitrary", "arbitrary")),
    )(a, b)


def adamw(name, w, m, v, parts, tm=None, guest=None):
    rows, cols = w.shape
    tm = rows if tm is None else min(tm, rows)
    n = len(parts)
    c1 = 1.0 - ADAM_B1 ** ADAM_STEP
    c2 = 1.0 - ADAM_B2 ** ADAM_STEP

    def body(*refs):
        w_ref, m_ref, v_ref = refs[:3]
        part_refs = refs[3:3 + n]
        g_ref, d_ref, nm_ref, nv_ref = refs[3 + n:]
        g = None
        for ref, (_, index) in zip(part_refs, parts):
            term = (ref[...] if index is None else ref[0]).astype(F32)
            g = term if g is None else g + term
        new_m = ADAM_B1 * m_ref[...] + (1.0 - ADAM_B1) * g
        new_v = ADAM_B2 * v_ref[...] + (1.0 - ADAM_B2) * (g * g)
        m_hat = new_m / c1
        v_hat = new_v / c2
        g_ref[...] = g
        d_ref[...] = -ADAM_LR * (m_hat / (jnp.sqrt(v_hat) + ADAM_EPS) + ADAM_WD * w_ref[...])
        nm_ref[...] = new_m
        nv_ref[...] = new_v

    spec = pl.BlockSpec((tm, cols), lambda i: (i, 0))
    shape = jax.ShapeDtypeStruct((rows, cols), F32)

    def part_spec(index):
        if index is None:
            return spec
        return pl.BlockSpec((1, tm, cols), lambda i: (index, i, 0))

    return _pallas(
        body, guest, (w, m, v, *[array for array, _ in parts]), name=name, grid=(rows // tm,),
        in_specs=[spec] * 3 + [part_spec(index) for _, index in parts],
        out_specs=[spec] * 4, out_shape=[shape] * 4,
        compiler_params=_params(("arbitrary",)),
    )


def _place():
    x, y, c = lax.axis_index("x"), lax.axis_index("y"), lax.axis_index("c")
    other_chips = [(1 - x, y), (x, 1 - y), (1 - x, 1 - y)]
    return x, y, c, other_chips


def place_shards(arrays):
    n = len(arrays)

    def body(*refs):
        ins, outs, stage, sems = refs[:n], refs[n:2 * n], refs[2 * n:3 * n], refs[3 * n]
        x, y, c, _ = _place()
        copies = []
        for a in range(n):
            stage[a][...] = ins[a][...].astype(BF16)
            copies.append(pltpu.make_async_copy(stage[a], outs[a].at[4 * x + 2 * y + c], sems.at[a]))
            copies[-1].start()
        for cp in copies:
            cp.wait()

    return pl.pallas_call(
        body, name="place_shards",
        in_specs=[pl.BlockSpec(memory_space=pltpu.VMEM)] * n, out_specs=[HBM_SPEC] * n,
        out_shape=[jax.ShapeDtypeStruct((N_DEV,) + a.shape, BF16) for a in arrays],
        scratch_shapes=[pltpu.VMEM(a.shape, BF16) for a in arrays] + [pltpu.SemaphoreType.DMA((n,))],
        compiler_params=_params(),
    )(*arrays)


def all_gather(name, arrays):
    n = len(arrays)

    def body(*refs):
        ins, outs = refs[:n], refs[n:2 * n]
        send_sems, recv_sems, local_sems = refs[2 * n:]
        x, y, c, chips = _place()
        me = 4 * x + 2 * y + c
        sibling = (x, y, 1 - c)

        def copy(a, k, block, to, src=None):
            return pltpu.make_async_remote_copy(
                src_ref=outs[a].at[block] if src is None else src, dst_ref=outs[a].at[block],
                send_sem=send_sems.at[a, k], recv_sem=recv_sems.at[a, k],
                device_id=to, device_id_type=MESH)

        started = []
        local = [pltpu.make_async_copy(ins[a], outs[a].at[me], local_sems.at[a]) for a in range(n)]
        for a in range(n):
            local[a].start()
            sends = [copy(a, 0, me, sibling, src=ins[a])]
            sends += [copy(a, 1 + j, me, (cx, cy, c), src=ins[a]) for j, (cx, cy) in enumerate(chips)]
            for cp in sends:
                cp.start()
            started += sends
        for a in range(n):
            for j, (cx, cy) in enumerate(chips):
                block = 4 * cx + 2 * cy + c
                copy(a, 1 + j, block, sibling).wait_recv()
                passed = copy(a, 4 + j, block, sibling)
                passed.start()
                started.append(passed)
        for a in range(n):
            copy(a, 0, 4 * x + 2 * y + (1 - c), sibling).wait_recv()
            for j, (cx, cy) in enumerate(chips):
                copy(a, 4 + j, 4 * cx + 2 * cy + (1 - c), sibling).wait_recv()
        for cp in started:
            cp.wait_send()
        for cp in local:
            cp.wait()

    return pl.pallas_call(
        body, name=name,
        in_specs=[pl.BlockSpec(memory_space=pltpu.VMEM)] * n, out_specs=[HBM_SPEC] * n,
        out_shape=[jax.ShapeDtypeStruct((N_DEV,) + a.shape, a.dtype) for a in arrays],
        scratch_shapes=[pltpu.SemaphoreType.DMA((n, 7)), pltpu.SemaphoreType.DMA((n, 7)),
                        pltpu.SemaphoreType.DMA((n,))],
    )(*arrays)


def _gather_first_plan(n):
    def plan(zones):
        x, y, c, chips = _place()
        me = 4 * x + 2 * y + c
        copies = []
        for zone in zones[:n]:
            copies.append((zone.at[me], zone.at[me], (x, y, 1 - c), zone.at[4 * x + 2 * y + (1 - c)]))
            copies += [(zone.at[me], zone.at[me], (cx, cy, c), zone.at[4 * cx + 2 * cy + c]) for cx, cy in chips]
        return copies
    return plan


def _gather_second_plan(n):
    def plan(zones):
        x, y, c, chips = _place()
        return [(zone.at[4 * cx + 2 * cy + c], zone.at[4 * cx + 2 * cy + c], (x, y, 1 - c),
                 zone.at[4 * cx + 2 * cy + (1 - c)]) for zone in zones[:n] for cx, cy in chips]
    return plan


def _both_plans(first, n_first, second):
    return lambda bufs: first(bufs[:n_first]) + second(bufs[n_first:])


def _sibling_plan(n):
    def plan(bufs):
        x, y, c, _ = _place()
        return [(bufs[a].at[:, 1 - c], bufs[n + a], (x, y, 1 - c), bufs[n + a]) for a in range(n)]
    return plan


def _chips_plan(n):
    def plan(bufs):
        _, _, c, chips = _place()
        return [(bufs[a].at[2 * cx + cy], bufs[n + a].at[j], (cx, cy, c), bufs[n + a].at[j])
                for a in range(n) for j, (cx, cy) in enumerate(chips)]
    return plan


def chip_sum(name, grad, landed, place, tr=256):
    _, _, rows, cols = grad.shape
    tr = min(tr, rows)

    def body(place_ref, g_ref, l_ref, own_ref, out_ref):
        total = g_ref[0, 0] + l_ref[0]
        out_ref[0] = total.astype(BF16)

        @pl.when(pl.program_id(1) == place_ref[1])
        def _():
            own_ref[...] = total

    return pl.pallas_call(
        body, name=name,
        grid_spec=pltpu.PrefetchScalarGridSpec(
            num_scalar_prefetch=1, grid=(rows // tr, 4),
            in_specs=[pl.BlockSpec((1, 1, tr, cols), lambda i, ch, pos: (ch, pos[0], i, 0)),
                      pl.BlockSpec((1, tr, cols), lambda i, ch, pos: (ch, i, 0))],
            out_specs=[pl.BlockSpec((tr, cols), lambda i, ch, pos: (i, 0)),
                       pl.BlockSpec((1, tr, cols), lambda i, ch, pos: (ch, i, 0))]),
        out_shape=[jax.ShapeDtypeStruct((rows, cols), F32), jax.ShapeDtypeStruct((4, rows, cols), BF16)],
        compiler_params=_params(("arbitrary", "arbitrary")),
    )(place, grad, landed)


BIG =("w_in", "w_sb_proj", "w_ca_proj", "w_out", "w_mlp_in", "w_mlp_out")
SMALL = ("b_gate", "rel_bias", "ln1_g", "ln1_b", "ln2_g", "ln2_b")
NAMES = ("w_in", "b_gate", "w_sb_proj", "w_ca_proj", "rel_bias", "w_out", "ln1_g", "ln1_b",
         "w_mlp_in", "w_mlp_out", "ln2_g", "ln2_b")
REL_COLS = 2 * REL_CLIP + 1


def _pack_small(t, scalar=None):
    rel = t["rel_bias"]
    if rel.shape[1] != REL_PAD:
        rel = jnp.pad(rel, ((0, 0), (0, REL_PAD - rel.shape[1])))
    last = jnp.zeros((128,), F32) if scalar is None else jnp.pad(scalar.reshape(1), (0, 127))
    flat = [t["b_gate"].reshape(-1), rel.reshape(-1)] + [t[n].reshape(-1) for n in SMALL[2:]] + [last]
    return jnp.concatenate(flat).reshape(-1, 128)


def _unpack_small(p):
    flat = p.reshape(-1)
    out, off = {}, 0
    for n, size in (("b_gate", GATE_COLS), ("rel_bias", 8 * REL_PAD)) + tuple((n, D_MODEL) for n in SMALL[2:]):
        out[n] = flat[off:off + size]
        off += size
    out["rel_bias"] = out["rel_bias"].reshape(8, REL_PAD)[:, :REL_COLS]
    return out


def kernel(x, w_in, b_gate, w_sb_proj, w_ca_proj, rel_bias, w_out, ln1_g, ln1_b, w_mlp_in, w_mlp_out, ln2_g, ln2_b, loss_target, m_w_in, m_b_gate, m_w_sb_proj, m_w_ca_proj, m_rel_bias, m_w_out, m_ln1_g, m_ln1_b, m_w_mlp_in, m_w_mlp_out, m_ln2_g, m_ln2_b, v_w_in, v_b_gate, v_w_sb_proj, v_w_ca_proj, v_rel_bias, v_w_out, v_ln1_g, v_ln1_b, v_w_mlp_in, v_w_mlp_out, v_ln2_g, v_ln2_b):
    w = dict(w_in=w_in, b_gate=b_gate, w_sb_proj=w_sb_proj, w_ca_proj=w_ca_proj, rel_bias=rel_bias, w_out=w_out,
             ln1_g=ln1_g, ln1_b=ln1_b, w_mlp_in=w_mlp_in, w_mlp_out=w_mlp_out, ln2_g=ln2_g, ln2_b=ln2_b)
    m = dict(w_in=m_w_in, b_gate=m_b_gate, w_sb_proj=m_w_sb_proj, w_ca_proj=m_w_ca_proj, rel_bias=m_rel_bias,
             w_out=m_w_out, ln1_g=m_ln1_g, ln1_b=m_ln1_b, w_mlp_in=m_w_mlp_in, w_mlp_out=m_w_mlp_out,
             ln2_g=m_ln2_g, ln2_b=m_ln2_b)
    v = dict(w_in=v_w_in, b_gate=v_b_gate, w_sb_proj=v_w_sb_proj, w_ca_proj=v_w_ca_proj, rel_bias=v_rel_bias,
             w_out=v_w_out, ln1_g=v_ln1_g, ln1_b=v_ln1_b, w_mlp_in=v_w_mlp_in, w_mlp_out=v_w_mlp_out,
             ln2_g=v_ln2_g, ln2_b=v_ln2_b)

    x, target = x[0], loss_target[0]
    c = lax.axis_index("c")
    place = jnp.stack([c, 2 * lax.axis_index("x") + lax.axis_index("y")]).astype(jnp.int32)
    grads, deltas, new_m, new_v = {}, {}, {}, {}

    def by_owner(n, g):
        return g.reshape((4, 2) + w[n].shape)

    def finish(n, own, arrived, guest=None):
        out = adamw("adamw_" + n, w[n], m[n], v[n], [(own, None), (arrived, 0), (arrived, 1), (arrived, 2)],
                    tm=256, guest=guest)
        out, bufs = (out, None) if guest is None else out
        grads[n], deltas[n], new_m[n], new_v[n] = out
        return bufs

    def siblings(names, g):
        bufs = [by_owner(n, g[n]) for n in names] + [lax.empty((4,) + w[n].shape, F32) for n in names]
        return Exchange(tuple(bufs), _sibling_plan(len(names)), len(names))

    def chip_sums(names, bufs):
        k = len(names)
        own, sums = zip(*[chip_sum("chip_sum_" + n, bufs[a], bufs[k + a], place) for a, n in enumerate(names)])
        zones = [lax.empty((3,) + w[n].shape, BF16) for n in names]
        return own, Exchange(tuple(sums) + tuple(zones), _chips_plan(k), 3 * k)

    proj, ffn = ("w_sb_proj", "w_ca_proj", "w_out"), ("w_mlp_in", "w_mlp_out")
    zones = place_shards([w[n] for n in proj + ffn])
    proj_zones, ffn_zones = tuple(zones[:len(proj)]), tuple(zones[len(proj):])
    wg_in = all_gather("gather_w_in", [w_in.astype(BF16)])[0]
    (qkv, lg, xb), proj_zones = in_proj(
        x, wg_in, guest=Exchange(proj_zones, _gather_first_plan(len(proj)), 4 * len(proj)))
    bias_tiles = ca_bias_tiles(rel_bias)
    both = Exchange(ffn_zones + tuple(proj_zones),
                    _both_plans(_gather_first_plan(len(ffn)), len(ffn), _gather_second_plan(len(proj))),
                    4 * len(ffn) + 3 * len(proj))
    (o_sb,), zones = sb_fwd(qkv, both)
    ffn_zones, proj_zones = tuple(zones[:len(ffn)]), zones[len(ffn):]
    (o_ca,), ffn_zones = ca_fwd(qkv, bias_tiles, Exchange(ffn_zones, _gather_second_plan(len(ffn)), 3 * len(ffn)))
    wg = dict(zip(proj + ffn, list(proj_zones) + list(ffn_zones)))
    wg_sb, wg_ca, wg_mi = wg["w_sb_proj"], wg["w_ca_proj"], wg["w_mlp_in"]
    wg_mo = wg["w_mlp_out"].reshape(D_FF, D_MODEL)
    w_out_b = wg["w_out"].reshape(D_MODEL, D_MODEL)

    u1 = mix_fwd(x, o_sb, o_ca, lg, b_gate, wg_sb, wg_ca, w_out_b)
    hm, a, x1b, du2, du2b, loss, dg2, db2 = mlp_fwd(u1, target, wg_mi, wg_mo, ln1_g, ln1_b, ln2_g, ln2_b)
    dhm, du1, dg1, db1 = mlp_bwd(du2, hm, u1, wg_mi, wg_mo, ln1_g)
    g = {}
    g["w_mlp_in"] = wgrad("wgrad_mlp_in", x1b, dhm, tm=D_MODEL, tn=FF_SHARD, shard_cols=FF_SHARD)
    g["w_mlp_out"] = wgrad("wgrad_mlp_out", a, du2b, tm=FF_SHARD, tn=D_MODEL)

    mlp = ("w_mlp_in", "w_mlp_out")
    (dlg, do_sb, do_ca, dy_sb, dy_ca, merged, dbg), bufs = mix_bwd(
        du1, o_sb, o_ca, lg, b_gate, wg_sb, wg_ca, w_out_b, guest=siblings(mlp, g))
    own_mlp, mlp_chips = chip_sums(mlp, bufs)
    g["w_out"] = wgrad("wgrad_out", merged, du1, tm=D_MODEL, tn=D_MODEL)
    g["w_sb_proj"] = wgrad("wgrad_sb_proj", o_sb, dy_sb, tm=ATT_WIDTH, tn=D_MODEL, shard_cols=PROJ_SHARD)
    g["w_ca_proj"] = wgrad("wgrad_ca_proj", o_ca, dy_ca, tm=ATT_WIDTH, tn=D_MODEL, shard_cols=PROJ_SHARD)

    att = ("w_out", "w_sb_proj", "w_ca_proj")
    att_siblings = siblings(att, g)
    both = Exchange(mlp_chips.bufs + att_siblings.bufs,
                    _both_plans(mlp_chips.plan, len(mlp_chips.bufs), att_siblings.plan),
                    mlp_chips.copies + att_siblings.copies)
    (dq_sb, dk_sb, dv_sb), bufs = sb_bwd(qkv, o_sb, do_sb, guest=both)
    arrived_mlp, bufs = bufs[len(mlp):len(mlp_chips.bufs)], bufs[len(mlp_chips.bufs):]
    own_att, att_chips = chip_sums(att, bufs)
    (dq_ca, dk_ca, dv_ca, db_tiles), bufs = ca_bwd(qkv, bias_tiles, do_ca, guest=att_chips)
    arrived_att = bufs[len(att):]
    g_rel = rel_bias_grad(db_tiles)

    dh = dh_slab([dq_sb, dk_sb, dv_sb, dq_ca, dk_ca, dv_ca, dlg])
    g["w_in"] = wgrad("wgrad_in", xb, dh, tm=D_MODEL, tn=2 * IN_SHARD, shard_cols=IN_SHARD)
    last = ("w_in",)
    bufs = finish(mlp[0], own_mlp[0], arrived_mlp[0], guest=siblings(last, g))
    finish(mlp[1], own_mlp[1], arrived_mlp[1])
    for n, own, arrived in zip(att, own_att, arrived_att):
        finish(n, own, arrived)
    own_in, in_chips = chip_sums(last, bufs)
    (grad_x,), bufs = in_bwd(du1, dh, wg_in, guest=in_chips)
    finish(last[0], own_in[0], bufs[1])

    small = dict(b_gate=dbg, rel_bias=g_rel, ln1_g=dg1, ln1_b=db1, ln2_g=dg2, ln2_b=db2)
    parts = all_gather("gather_small_grads", [_pack_small(small, loss[0, 0])])[0]
    packed = adamw("adamw_small", _pack_small(w), _pack_small(m), _pack_small(v),
                   [(parts, d) for d in range(N_DEV)])
    loss = packed[0][-1, 0]
    for out, p in zip((grads, deltas, new_m, new_v), packed):
        for n, val in _unpack_small(p).items():
            out[n] = val.reshape(w[n].shape)

    return (loss, grad_x[None], *[grads[n] for n in NAMES], *[deltas[n] for n in NAMES],
            *[new_m[n] for n in NAMES], *[new_v[n] for n in NAMES])
```

```python
import functools
from typing import NamedTuple

import jax
import jax.numpy as jnp
from jax import lax
from jax.experimental import pallas as pl
from jax.experimental.pallas import tpu as pltpu

F32 = jnp.float32
BF16 = jnp.bfloat16
MESH = pl.DeviceIdType.MESH

N_DEV = 8
D_MODEL = 1024
HEAD_DIM = 64
ATT_WIDTH = 512
N_PAIRS = ATT_WIDTH // 128
QKV_COLS = 6 * ATT_WIDTH
GATE_COLS = 2 * D_MODEL
IN_COLS = QKV_COLS + GATE_COLS
IN_SHARD = IN_COLS // N_DEV
D_FF = 4 * D_MODEL
FF_SHARD = D_FF // N_DEV
PROJ_SHARD = D_MODEL // N_DEV
ATT_BLOCK = 128
CA_TILES = 5
CA_ROWS = 1024
CHUNK = 64
CA_PREV_CHUNKS = 8
REL_CLIP = 256
REL_PAD = 640
ALPHA = 2.0 ** 0.25
LN_EPS = 1e-5
QK_SCALE = HEAD_DIM ** -0.5
NEG_BIG = -1e30
VMEM_LIMIT = 56 * 1024 * 1024

ADAM_LR = 0.001
ADAM_B1 = 0.9
ADAM_B2 = 0.999
ADAM_EPS = 1e-08
ADAM_WD = 0.01
ADAM_STEP = 10

_NT = (((1,), (1,)), ((), ()))
_TN = (((0,), (0,)), ((), ()))


def _dot(a, b):
    return jnp.dot(a, b, preferred_element_type=F32)


def _dot_nt(a, b):
    return lax.dot_general(a, b, _NT, preferred_element_type=F32)


def _dot_tn(a, b):
    return lax.dot_general(a, b, _TN, preferred_element_type=F32)


def _params(semantics=None):
    return pltpu.CompilerParams(dimension_semantics=semantics, vmem_limit_bytes=VMEM_LIMIT)


def _row(v):
    return v.reshape(1, -1)


HBM_SPEC = pl.BlockSpec(memory_space=pl.ANY)


class Exchange(NamedTuple):
    bufs: tuple
    plan: object
    copies: int


def _pallas(body, guest, args, *, grid, in_specs, out_specs, out_shape, scratch_shapes=(), **kw):
    out_specs, out_shape, scratch_shapes = list(out_specs), list(out_shape), list(scratch_shapes)
    if guest is None:
        return pl.pallas_call(body, grid=grid, in_specs=in_specs, out_specs=out_specs, out_shape=out_shape,
                              scratch_shapes=scratch_shapes, **kw)(*args)
    nb, n_own = len(guest.bufs), len(in_specs) + len(out_specs)

    def hosting(*refs):
        zones = refs[nb + n_own:2 * nb + n_own]
        send_sems, recv_sems = refs[-2], refs[-1]
        steps = [pl.program_id(axis) for axis in range(len(grid))]
        first = functools.reduce(jnp.logical_and, [step == 0 for step in steps])
        last = functools.reduce(jnp.logical_and, [step == n - 1 for step, n in zip(steps, grid)])

        def copy(k, src, dst, peer):
            return pltpu.make_async_remote_copy(src_ref=src, dst_ref=dst, send_sem=send_sems.at[k],
                                                recv_sem=recv_sems.at[k], device_id=peer, device_id_type=MESH)

        @pl.when(first)
        def _():
            for k, (src, dst, peer, _) in enumerate(guest.plan(zones)):
                copy(k, src, dst, peer).start()

        body(*refs[nb:nb + n_own], *refs[2 * nb + n_own:-2])

        @pl.when(last)
        def _():
            for k, (src, dst, peer, landed) in enumerate(guest.plan(zones)):
                copy(k, src, dst, peer).wait_send()
                copy(k, src, landed, peer).wait_recv()

    sems = pltpu.SemaphoreType.DMA((guest.copies,))
    results = pl.pallas_call(
        hosting, grid=grid, in_specs=[HBM_SPEC] * nb + list(in_specs), out_specs=out_specs + [HBM_SPEC] * nb,
        out_shape=out_shape + [jax.ShapeDtypeStruct(b.shape, b.dtype) for b in guest.bufs],
        input_output_aliases={k: len(out_specs) + k for k in range(nb)},
        scratch_shapes=scratch_shapes + [sems, sems], **kw)(*guest.bufs, *args)
    return results[:len(out_specs)], results[len(out_specs):]


IN_PAIR = 2 * IN_SHARD
IN_PAIR_SCRATCH = [pltpu.VMEM((N_DEV // 2, D_MODEL, IN_PAIR), BF16), pltpu.SemaphoreType.DMA((N_DEV,))]


def _load_in_pairs(w_hbm, pairs_s, sems):
    copies = [pltpu.make_async_copy(w_hbm.at[j], pairs_s.at[j // 2, :, pl.ds(IN_SHARD * (j % 2), IN_SHARD)],
                                    sems.at[j]) for j in range(N_DEV)]
    for cp in copies:
        cp.start()
    for cp in copies:
        cp.wait()


def in_proj(x, wg_in, tm=512, guest=None):
    s = x.shape[0]

    def body(x_ref, w_hbm, qkv_ref, lg_ref, xb_ref, pairs_s, sems):
        @pl.when(pl.program_id(0) == 0)
        def _():
            _load_in_pairs(w_hbm, pairs_s, sems)

        xb = x_ref[...].astype(BF16)
        xb_ref[...] = xb
        for j in range(N_DEV // 2):
            acc = _dot(xb, pairs_s[j])
            lo, hi = IN_PAIR * j, IN_PAIR * (j + 1)
            if hi <= QKV_COLS:
                qkv_ref[:, lo:hi] = acc.astype(BF16)
            elif lo >= QKV_COLS:
                lg_ref[:, lo - QKV_COLS:hi - QKV_COLS] = acc
            else:
                qkv_ref[:, lo:QKV_COLS] = acc[:, :QKV_COLS - lo].astype(BF16)
                lg_ref[:, 0:hi - QKV_COLS] = acc[:, QKV_COLS - lo:]

    return _pallas(
        body, guest, (x, wg_in), name="in_proj", grid=(s // tm,), scratch_shapes=IN_PAIR_SCRATCH,
        in_specs=[pl.BlockSpec((tm, D_MODEL), lambda i: (i, 0)), HBM_SPEC],
        out_specs=[pl.BlockSpec((tm, QKV_COLS), lambda i: (i, 0)),
                   pl.BlockSpec((tm, GATE_COLS), lambda i: (i, 0)),
                   pl.BlockSpec((tm, D_MODEL), lambda i: (i, 0))],
        out_shape=[jax.ShapeDtypeStruct((s, QKV_COLS), BF16),
                   jax.ShapeDtypeStruct((s, GATE_COLS), F32),
                   jax.ShapeDtypeStruct((s, D_MODEL), BF16)],
        compiler_params=_params(("arbitrary",)),
    )


def _head_masks():
    lane = lax.broadcasted_iota(jnp.int32, (1, 128), 1)
    first = lane < HEAD_DIM
    return first, jnp.logical_not(first)


SB_CHUNK = 512
SB_SUB = SB_CHUNK // ATT_BLOCK


def _tri_pair():
    row = lax.broadcasted_iota(jnp.int32, (4 * ATT_BLOCK, 2 * ATT_BLOCK), 0) & (2 * ATT_BLOCK - 1)
    col = lax.broadcasted_iota(jnp.int32, (4 * ATT_BLOCK, 2 * ATT_BLOCK), 1)
    same = (row >= ATT_BLOCK) == (col >= ATT_BLOCK)
    return jnp.logical_and(same, row >= col).astype(BF16)


def _pair_bf16(pieces):
    return jnp.concatenate(
        [jnp.concatenate([p.astype(BF16) for p in row], axis=1) for row in pieces], axis=0)


def _suffix_sums(pieces, tri_pair):
    hi = [[p.astype(BF16) for p in row] for row in pieces]
    lo = [[p - h.astype(F32) for p, h in zip(row, hrow)] for row, hrow in zip(pieces, hi)]
    return _dot(jnp.concatenate([_pair_bf16(hi), _pair_bf16(lo)], axis=1), tri_pair)


def _first_row_piece(diag, j):
    return j if diag else 0


def _piece(v, r, h):
    return v[ATT_BLOCK * r:ATT_BLOCK * (r + 1), ATT_BLOCK * h:ATT_BLOCK * (h + 1)]


def _all_row_pieces(slabs):
    return [[slab[ATT_BLOCK * r:ATT_BLOCK * (r + 1)] for r in range(SB_SUB)] for slab in slabs]


def _add_rows(total, part, r0):
    if r0 == 0:
        return total + part
    return jnp.concatenate([total[:ATT_BLOCK * r0], total[ATT_BLOCK * r0:] + part], axis=0)


def _row_totals(csum, r0, h):
    return jnp.concatenate(
        [jnp.broadcast_to(_piece(csum, at, h)[:, 0:1], (ATT_BLOCK, ATT_BLOCK)) for at in range(SB_SUB - r0)], axis=0)


SIGN_BIT = 0x80000000


def _drop(z):
    minus_abs = lax.bitcast_convert_type(lax.bitcast_convert_type(z, jnp.uint32) | jnp.uint32(SIGN_BIT), F32)
    return jnp.maximum(z, 0.0) + jnp.log(1.0 + jnp.exp(minus_abs))


def _sb_weights(z, runs, r0, masked, strict, tri_pair):
    drops = []
    for r in range(r0, SB_SUB):
        row = [_drop(_piece(z, r - r0, h)) for h in range(2)]
        drops.append([jnp.where(strict, drop, 0.0) for drop in row] if r == masked else row)
    csum = _suffix_sums(drops, tri_pair)
    weights = []
    for r in range(r0, SB_SUB):
        row = [jnp.exp(_piece(z, r - r0, h) - (_piece(csum, r - r0, h) + runs[h][r])) for h in range(2)]
        weights.append([jnp.where(strict, a, 0.0) for a in row] if r == masked else row)
    return drops, weights, csum


EXP_UNDERFLOW = 104.0


SB_STEP = 2


def _sweep_left(span, i, dropped_ref):
    def more(at):
        return jnp.logical_and(at >= 0, jnp.min(dropped_ref[...]) < EXP_UNDERFLOW)

    def step(at):
        span(at * (SB_STEP * ATT_BLOCK), SB_STEP, False)
        return at - 1

    lax.while_loop(more, step, i * (SB_SUB // SB_STEP) - 1)


def _by_head(blk, masks):
    zero = jnp.zeros_like(blk)
    return jnp.concatenate([jnp.where(m, blk, zero) for m in masks], axis=0)


def sb_fwd(qkv, guest=None):
    s = qkv.shape[0]
    nq = s // SB_CHUNK

    def body(q_ref, k_ref, v_ref, o_ref, run_s):
        i = pl.program_id(1)
        masks = _head_masks()
        q = q_ref[...] * QK_SCALE
        row = lax.broadcasted_iota(jnp.int32, (ATT_BLOCK, ATT_BLOCK), 0)
        col = lax.broadcasted_iota(jnp.int32, (ATT_BLOCK, ATT_BLOCK), 1)
        strict = col < row
        tri_pair = _tri_pair()
        run_s[...] = jnp.zeros_like(run_s)
        o_ref[...] = jnp.zeros_like(o_ref)

        def span(start, pieces, diag):
            runs = [run_s[0], run_s[1]]
            acc = jnp.zeros((SB_CHUNK, ATT_BLOCK), F32)
            for j in reversed(range(pieces)):
                r0 = _first_row_piece(diag, j)
                keys = pl.ds(pl.multiple_of(start + ATT_BLOCK * j, ATT_BLOCK), ATT_BLOCK)
                z = _dot_nt(q[ATT_BLOCK * r0:], _by_head(k_ref[keys, :], masks))
                _, weights, csum = _sb_weights(z, _all_row_pieces(runs), r0, j if diag else None, strict, tri_pair)
                part = _dot(_pair_bf16(weights), _by_head(v_ref[keys, :], masks))
                acc = _add_rows(acc, part, r0)
                runs = [_add_rows(runs[h], _row_totals(csum, r0, h), r0) for h in range(2)]
            run_s[0], run_s[1] = runs
            o_ref[...] += acc

        span(i * SB_CHUNK, SB_SUB, True)
        _sweep_left(span, i, run_s)

    return _pallas(
        body, guest, (qkv, qkv, qkv), name="sb_fwd", grid=(N_PAIRS, nq),
        in_specs=[pl.BlockSpec((SB_CHUNK, 128), lambda p, i: (i, p)),
                  pl.BlockSpec((s, 128), lambda p, i: (0, N_PAIRS + p)),
                  pl.BlockSpec((s, 128), lambda p, i: (0, 2 * N_PAIRS + p))],
        out_specs=[pl.BlockSpec((SB_CHUNK, 128), lambda p, i: (i, p))],
        out_shape=[jax.ShapeDtypeStruct((s, ATT_WIDTH), F32)],
        scratch_shapes=[pltpu.VMEM((2, SB_CHUNK, 128), F32)],
        compiler_params=_params(("arbitrary", "arbitrary")),
    )


def sb_bwd(qkv, o_sb, do_sb, guest=None):
    s = qkv.shape[0]
    nq = s // SB_CHUNK

    def body(q_ref, k_ref, v_ref, o_ref, do_ref, dq_ref, dk_ref, dv_ref, run_l_s, run_e_s, etot_s, dq_s, dk_s, dv_s):
        i = pl.program_id(1)

        @pl.when(i == 0)
        def _():
            dk_s[...] = jnp.zeros_like(dk_s)
            dv_s[...] = jnp.zeros_like(dv_s)

        masks = _head_masks()
        q = q_ref[...] * QK_SCALE
        do = do_ref[...]
        prod = do.astype(F32) * o_ref[...]
        for h in range(2):
            total = jnp.sum(jnp.where(masks[h], prod, 0.0), axis=1, keepdims=True)
            etot_s[h] = jnp.broadcast_to(total, (SB_CHUNK, ATT_BLOCK))
        row = lax.broadcasted_iota(jnp.int32, (ATT_BLOCK, ATT_BLOCK), 0)
        col = lax.broadcasted_iota(jnp.int32, (ATT_BLOCK, ATT_BLOCK), 1)
        strict = col < row
        tri_pair = _tri_pair()
        run_l_s[...] = jnp.zeros_like(run_l_s)
        run_e_s[...] = jnp.zeros_like(run_e_s)
        dq_s[...] = jnp.zeros_like(dq_s)

        def span(start, pieces, diag):
            slabs_l = [run_l_s[0], run_l_s[1]]
            slabs_e = [run_e_s[0], run_e_s[1]]
            e_tot = _all_row_pieces([etot_s[0], etot_s[1]])
            dq = jnp.zeros((SB_CHUNK, ATT_BLOCK), F32)
            for j in reversed(range(pieces)):
                runs_l, runs_e = _all_row_pieces(slabs_l), _all_row_pieces(slabs_e)
                r0 = _first_row_piece(diag, j)
                masked = j if diag else None
                keys = pl.ds(pl.multiple_of(start + ATT_BLOCK * j, ATT_BLOCK), ATT_BLOCK)
                kk = _by_head(k_ref[keys, :], masks)
                vv = _by_head(v_ref[keys, :], masks)
                q_rows, do_rows = q[ATT_BLOCK * r0:], do[ATT_BLOCK * r0:]
                z = _dot_nt(q_rows, kk)
                da = _dot_nt(do_rows, vv)
                drops, weights, csum = _sb_weights(z, runs_l, r0, masked, strict, tri_pair)
                ab = [[a.astype(BF16) for a in wrow] for wrow in weights]
                es = [[a.astype(F32) * _piece(da, at, h) for h, a in enumerate(arow)] for at, arow in enumerate(ab)]
                esum = _suffix_sums(es, tri_pair)
                dzs = []
                for at, r in enumerate(range(r0, SB_SUB)):
                    dz_row = []
                    for h in range(2):
                        sig = jnp.exp(_piece(z, at, h) - drops[at][h])
                        after = _piece(esum, at, h) + runs_e[h][r] - es[at][h]
                        dz = es[at][h] - sig * (e_tot[h][r] - after)
                        dz_row.append(jnp.where(strict, dz, 0.0) if r == masked else dz)
                    dzs.append(dz_row)
                dz = _pair_bf16(dzs)
                dq = _add_rows(dq, _dot(dz, kk), r0)
                dk = _dot_tn(dz, q_rows)
                dv = _dot_tn(_pair_bf16(ab), do_rows)
                dk_s[keys, :] += jnp.where(masks[0], dk[:ATT_BLOCK], dk[ATT_BLOCK:])
                dv_s[keys, :] += jnp.where(masks[0], dv[:ATT_BLOCK], dv[ATT_BLOCK:])
                slabs_l =[_add_rows(slabs_l[h], _row_totals(csum, r0, h), r0) for h in range(2)]
                slabs_e = [_add_rows(slabs_e[h], _row_totals(esum, r0, h), r0) for h in range(2)]
            run_l_s[0], run_l_s[1] = slabs_l
            run_e_s[0], run_e_s[1] = slabs_e
            dq_s[...] += dq

        span(i * SB_CHUNK, SB_SUB, True)
        _sweep_left(span, i, run_l_s)
        dq_ref[...] = (dq_s[...] * QK_SCALE).astype(BF16)

        @pl.when(i == nq - 1)
        def _():
            dk_ref[...] = dk_s[...].astype(BF16)
            dv_ref[...] = dv_s[...].astype(BF16)

    tile = lambda p, i: (i, p)
    state = pltpu.VMEM((2, SB_CHUNK, 128), F32)
    sums = pltpu.VMEM((s, 128), F32)
    return _pallas(
        body, guest, (qkv, qkv, qkv, o_sb, do_sb), name="sb_bwd", grid=(N_PAIRS, nq),
        in_specs=[pl.BlockSpec((SB_CHUNK, 128), tile),
                  pl.BlockSpec((s, 128), lambda p, i: (0, N_PAIRS + p)),
                  pl.BlockSpec((s, 128), lambda p, i: (0, 2 * N_PAIRS + p)),
                  pl.BlockSpec((SB_CHUNK, 128), tile),
                  pl.BlockSpec((SB_CHUNK, 128), tile)],
        out_specs=[pl.BlockSpec((SB_CHUNK, 128), tile),
                   pl.BlockSpec((s, 128), lambda p, i: (0, p)),
                   pl.BlockSpec((s, 128), lambda p, i: (0, p))],
        out_shape=[jax.ShapeDtypeStruct((s, ATT_WIDTH), BF16)] * 3,
        scratch_shapes=[state, state, state, pltpu.VMEM((SB_CHUNK, 128), F32), sums, sums],
        compiler_params=_params(("arbitrary", "arbitrary")),
    )


def ca_bias_tiles(rel_bias):
    heads = rel_bias.shape[0]
    wide = 2 * ATT_BLOCK

    def body(rel_ref, out_ref):
        bins = lax.broadcasted_iota(jnp.int32, (REL_PAD, wide), 0)
        pos = lax.broadcasted_iota(jnp.int32, (REL_PAD, wide), 1)
        r = lax.broadcasted_iota(jnp.int32, (ATT_BLOCK, ATT_BLOCK), 0)
        c = lax.broadcasted_iota(jnp.int32, (ATT_BLOCK, ATT_BLOCK), 1)
        for j in range(CA_TILES):
            hot = (jnp.clip(ATT_BLOCK * (j + 1) - 1 - pos, -REL_CLIP, REL_CLIP) + REL_CLIP == bins).astype(F32)
            lines = jnp.dot(rel_ref[...], hot, preferred_element_type=F32, precision=lax.Precision.HIGHEST)
            back = 2 * j + (r >> 6) - (c >> 6)
            ok = jnp.logical_and(back >= 0, back <= CA_PREV_CHUNKS)
            for h in range(heads):
                tile = jnp.broadcast_to(lines[h:h + 1, :], (ATT_BLOCK, wide))
                tile = pltpu.roll(tile, wide - (ATT_BLOCK - 1), 1, stride=1, stride_axis=0)
                out_ref[h, j] = jnp.where(ok, tile[:, :ATT_BLOCK], NEG_BIG)

    padded = jnp.pad(rel_bias, ((0, 0), (0, REL_PAD - rel_bias.shape[1])))
    return pl.pallas_call(
        body, name="ca_bias_tiles",
        out_shape=jax.ShapeDtypeStruct((heads, CA_TILES, ATT_BLOCK, ATT_BLOCK), F32),
        compiler_params=_params(),
    )(padded)


CA_BLOCKS = CA_ROWS // ATT_BLOCK
CA_KEY_BLOCKS = CA_BLOCKS + CA_TILES - 1


def _ca_users(t):
    return [r for r in range(CA_BLOCKS) if 0 <= r + CA_TILES - 1 - t < CA_TILES]


def _ca_rows(v, t, r):
    at = _ca_users(t).index(r)
    return v[ATT_BLOCK * at:ATT_BLOCK * (at + 1)]


def _ca_window(step, k_ref, v_ref, masks):
    kks, vvs, inside, rows = [], [], [], []
    for t in range(CA_KEY_BLOCKS):
        block = step * CA_BLOCKS - (CA_TILES - 1) + t
        keys = pl.ds(pl.multiple_of(jnp.maximum(block, 0) * ATT_BLOCK, ATT_BLOCK), ATT_BLOCK)
        kks.append(_by_head(k_ref[keys, :], masks))
        vvs.append(_by_head(v_ref[keys, :], masks))
        inside.append(block >= 0)
        rows.append(keys)
    return kks, vvs, inside, rows


def _ca_probs(q, kks, inside, bias_ref):
    scores = [[[None] * CA_TILES for _ in range(2)] for _ in range(CA_BLOCKS)]
    for t in range(CA_KEY_BLOCKS):
        users = _ca_users(t)
        z = _dot_nt(q[ATT_BLOCK * users[0]:ATT_BLOCK * (users[-1] + 1)], kks[t])
        for r in users:
            j = r + CA_TILES - 1 - t
            for h in range(2):
                zz = _ca_rows(z, t, r)[:, ATT_BLOCK * h:ATT_BLOCK * (h + 1)] + bias_ref[h, j]
                scores[r][h][j] = jnp.where(inside[t], zz, NEG_BIG)
    probs = [[None, None] for _ in range(CA_BLOCKS)]
    for r in range(CA_BLOCKS):
        for h in range(2):
            m = jnp.max(functools.reduce(jnp.maximum, scores[r][h]), axis=1, keepdims=True)
            ex = [jnp.exp(z - m) for z in scores[r][h]]
            inv = 1.0 / jnp.sum(functools.reduce(jnp.add, ex), axis=1, keepdims=True)
            probs[r][h] = [e * inv for e in ex]
    return probs


def _ca_stack(tiles, t):
    return jnp.concatenate(
        [jnp.concatenate([tiles[r][h][r + CA_TILES - 1 - t].astype(BF16) for h in range(2)], axis=1)
         for r in _ca_users(t)], axis=0)


def ca_fwd(qkv, bias_tiles, guest=None):
    s = qkv.shape[0]
    nq = s // CA_ROWS

    def body(q_ref, k_ref, v_ref, bias_ref, o_ref):
        masks = _head_masks()
        kks, vvs, inside, _ = _ca_window(pl.program_id(1), k_ref, v_ref, masks)
        probs = _ca_probs(q_ref[...] * QK_SCALE, kks, inside, bias_ref)
        out = [None] * CA_BLOCKS
        for t in range(CA_KEY_BLOCKS):
            part = _dot(_ca_stack(probs, t), vvs[t])
            for r in _ca_users(t):
                piece = _ca_rows(part, t, r)
                out[r] = piece if out[r] is None else out[r] + piece
        o_ref[...] = jnp.concatenate(out, axis=0).astype(BF16)

    return _pallas(
        body, guest, (qkv, qkv, qkv, bias_tiles), name="ca_fwd", grid=(N_PAIRS, nq),
        in_specs=[pl.BlockSpec((CA_ROWS, 128), lambda p, i: (i, 3 * N_PAIRS + p)),
                  pl.BlockSpec((s, 128), lambda p, i: (0, 4 * N_PAIRS + p)),
                  pl.BlockSpec((s, 128), lambda p, i: (0, 5 * N_PAIRS + p)),
                  pl.BlockSpec((2, CA_TILES, ATT_BLOCK, ATT_BLOCK), lambda p, i: (p, 0, 0, 0))],
        out_specs=[pl.BlockSpec((CA_ROWS, 128), lambda p, i: (i, p))],
        out_shape=[jax.ShapeDtypeStruct((s, ATT_WIDTH), BF16)],
        compiler_params=_params(("arbitrary", "arbitrary")),
    )


def ca_bwd(qkv, bias_tiles, do_ca, guest=None):
    s = qkv.shape[0]
    nq = s // CA_ROWS

    def body(q_ref, k_ref, v_ref, bias_ref, do_ref, dq_ref, dk_ref, dv_ref, db_ref, dk_s, dv_s):
        big = pl.program_id(1)

        @pl.when(big == 0)
        def _():
            dk_s[...] = jnp.zeros_like(dk_s)
            dv_s[...] = jnp.zeros_like(dv_s)
            db_ref[...] = jnp.zeros_like(db_ref)

        masks = _head_masks()
        kks, vvs, inside, key_rows = _ca_window(big, k_ref, v_ref, masks)
        q = q_ref[...] * QK_SCALE
        do = do_ref[...]
        probs = _ca_probs(q, kks, inside, bias_ref)
        dps = [[[None] * CA_TILES for _ in range(2)] for _ in range(CA_BLOCKS)]
        for t in range(CA_KEY_BLOCKS):
            users = _ca_users(t)
            dp = _dot_nt(do[ATT_BLOCK * users[0]:ATT_BLOCK * (users[-1] + 1)], vvs[t])
            for r in users:
                for h in range(2):
                    dps[r][h][r + CA_TILES - 1 - t] = _ca_rows(dp, t, r)[:, ATT_BLOCK * h:ATT_BLOCK * (h + 1)]
        dss = [[None, None] for _ in range(CA_BLOCKS)]
        for r in range(CA_BLOCKS):
            for h in range(2):
                delta = jnp.sum(functools.reduce(jnp.add, [p * dp for p, dp in zip(probs[r][h], dps[r][h])]),
                                axis=1, keepdims=True)
                dss[r][h] = [p * (dp - delta) for p, dp in zip(probs[r][h], dps[r][h])]
        for h in range(2):
            for j in range(CA_TILES):
                db_ref[h, j] += functools.reduce(jnp.add, [dss[r][h][j] for r in range(CA_BLOCKS)])
        dq = [None] * CA_BLOCKS
        for t in range(CA_KEY_BLOCKS):
            users = _ca_users(t)
            rows = slice(ATT_BLOCK * users[0], ATT_BLOCK * (users[-1] + 1))
            ds = _ca_stack(dss, t)
            part = _dot(ds, kks[t])
            for r in users:
                piece = _ca_rows(part, t, r)
                dq[r] = piece if dq[r] is None else dq[r] + piece
            dk = _dot_tn(ds, q[rows])
            dv = _dot_tn(_ca_stack(probs, t), do[rows])
            dk_s[key_rows[t], :] += jnp.where(masks[0], dk[:ATT_BLOCK], dk[ATT_BLOCK:])
            dv_s[key_rows[t], :] += jnp.where(masks[0], dv[:ATT_BLOCK], dv[ATT_BLOCK:])
        dq_ref[...] = (jnp.concatenate(dq, axis=0) * QK_SCALE).astype(BF16)

        @pl.when(big == nq - 1)
        def _():
            dk_ref[...] = dk_s[...].astype(BF16)
            dv_ref[...] = dv_s[...].astype(BF16)

    return _pallas(
        body, guest, (qkv, qkv, qkv, bias_tiles, do_ca), name="ca_bwd", grid=(N_PAIRS, nq),
        in_specs=[pl.BlockSpec((CA_ROWS, 128), lambda p, i: (i, 3 * N_PAIRS + p)),
                  pl.BlockSpec((s, 128), lambda p, i: (0, 4 * N_PAIRS + p)),
                  pl.BlockSpec((s, 128), lambda p, i: (0, 5 * N_PAIRS + p)),
                  pl.BlockSpec((2, CA_TILES, ATT_BLOCK, ATT_BLOCK), lambda p, i: (p, 0, 0, 0)),
                  pl.BlockSpec((CA_ROWS, 128), lambda p, i: (i, p))],
        out_specs=[pl.BlockSpec((CA_ROWS, 128), lambda p, i: (i, p)),
                   pl.BlockSpec((s, 128), lambda p, i: (0, p)),
                   pl.BlockSpec((s, 128), lambda p, i: (0, p)),
                   pl.BlockSpec((2, CA_TILES, ATT_BLOCK, ATT_BLOCK), lambda p, i: (p, 0, 0, 0))],
        out_shape=[jax.ShapeDtypeStruct((s, ATT_WIDTH), BF16)] * 3
                  + [jax.ShapeDtypeStruct((2 * N_PAIRS, CA_TILES, ATT_BLOCK, ATT_BLOCK), F32)],
        scratch_shapes=[pltpu.VMEM((s, 128), F32), pltpu.VMEM((s, 128), F32)],
        compiler_params=_params(("arbitrary", "arbitrary")),
    )


def rel_bias_grad(db_tiles):
    heads = db_tiles.shape[0]

    def body(db_ref, out_ref):
        row = lax.broadcasted_iota(jnp.int32, (ATT_BLOCK, ATT_BLOCK), 0)
        col = lax.broadcasted_iota(jnp.int32, (ATT_BLOCK, ATT_BLOCK), 1)
        flip = (row + col == ATT_BLOCK - 1).astype(F32)
        wrapped = col < row
        cp = lax.broadcasted_iota(jnp.int32, (ATT_BLOCK, REL_PAD), 0)
        bins = lax.broadcasted_iota(jnp.int32, (ATT_BLOCK, REL_PAD), 1)
        total = jnp.zeros((heads, REL_PAD), F32)
        for j in range(CA_TILES):
            sums_neg, sums_pos = [], []
            for h in range(heads):
                tile = jnp.dot(db_ref[h, j], flip, preferred_element_type=F32, precision=lax.Precision.HIGHEST)
                tile = pltpu.roll(tile, 0, 1, stride=1, stride_axis=0)
                sums_neg.append(jnp.sum(jnp.where(wrapped, 0.0, tile), axis=0, keepdims=True))
                sums_pos.append(jnp.sum(jnp.where(wrapped, tile, 0.0), axis=0, keepdims=True))
            neg = jnp.concatenate(sums_neg, axis=0)
            pos = jnp.concatenate(sums_pos, axis=0)
            hot_neg = (jnp.clip(ATT_BLOCK * j + cp - (ATT_BLOCK - 1), -REL_CLIP, REL_CLIP) + REL_CLIP == bins)
            hot_pos = (jnp.clip(ATT_BLOCK * j + cp + 1, -REL_CLIP, REL_CLIP) + REL_CLIP == bins)
            total = total + jnp.dot(neg, hot_neg.astype(F32), preferred_element_type=F32,
                                    precision=lax.Precision.HIGHEST)
            total = total + jnp.dot(pos, hot_pos.astype(F32), preferred_element_type=F32,
                                    precision=lax.Precision.HIGHEST)
        out_ref[...] = total

    return pl.pallas_call(
        body, name="rel_bias_grad",
        out_shape=jax.ShapeDtypeStruct((heads, REL_PAD), F32),
        compiler_params=_params(),
    )(db_tiles)


def _assemble_cols(dst_ref, src_ref):
    w = src_ref.shape[2]
    for j in range(N_DEV):
        dst_ref[:, w * j:w * (j + 1)] = src_ref[j]


def _merge(o_sb, o_ca, lg, bg, wsb, wca):
    y_sb = _dot(o_sb.astype(BF16), wsb)
    y_ca = _dot(o_ca.astype(BF16), wca)
    gates = jax.nn.sigmoid(lg + bg)
    g_sb, g_ca = gates[:, :D_MODEL], gates[:, D_MODEL:]
    return y_sb, y_ca, g_sb, g_ca, g_sb * y_sb + g_ca * y_ca


def mix_fwd(x, o_sb, o_ca, lg, b_gate, wg_sb, wg_ca, w_out, tm=256):
    s = x.shape[0]

    def body(x_ref, osb_ref, oca_ref, lg_ref, bg_ref, wsb_ref, wca_ref, wout_ref, u_ref, wsb_s, wca_s):
        @pl.when(pl.program_id(0) == 0)
        def _():
            _assemble_cols(wsb_s, wsb_ref)
            _assemble_cols(wca_s, wca_ref)

        merged = _merge(osb_ref[...], oca_ref[...], lg_ref[...], bg_ref[...], wsb_s[...], wca_s[...])[4]
        u_ref[...] = ALPHA * x_ref[...] + _dot(merged.astype(BF16), wout_ref[...])

    rows = lambda w: pl.BlockSpec((tm, w), lambda i: (i, 0))
    whole = lambda shape: pl.BlockSpec(shape, lambda i: (0,) * len(shape))
    return pl.pallas_call(
        body, name="mix_fwd", grid=(s // tm,),
        in_specs=[rows(D_MODEL), rows(ATT_WIDTH), rows(ATT_WIDTH), rows(GATE_COLS), whole((1, GATE_COLS)),
                  whole((N_DEV, ATT_WIDTH, PROJ_SHARD)), whole((N_DEV, ATT_WIDTH, PROJ_SHARD)),
                  whole((D_MODEL, D_MODEL))],
        out_specs=rows(D_MODEL),
        out_shape=jax.ShapeDtypeStruct((s, D_MODEL), F32),
        scratch_shapes=[pltpu.VMEM((ATT_WIDTH, D_MODEL), BF16), pltpu.VMEM((ATT_WIDTH, D_MODEL), BF16)],
        compiler_params=_params(("arbitrary",)),
    )(x, o_sb, o_ca, lg, _row(b_gate), wg_sb, wg_ca, w_out)


def mix_bwd(du1, o_sb, o_ca, lg, b_gate, wg_sb, wg_ca, w_out, tm=256, guest=None):
    s = du1.shape[0]

    def body(du_ref, osb_ref, oca_ref, lg_ref, bg_ref, wsb_ref, wca_ref, wout_ref,
             dlg_ref, dosb_ref, doca_ref, dysb_ref, dyca_ref, mg_ref, dbg_ref, wsb_s, wca_s):
        @pl.when(pl.program_id(0) == 0)
        def _():
            _assemble_cols(wsb_s, wsb_ref)
            _assemble_cols(wca_s, wca_ref)
            dbg_ref[...] = jnp.zeros_like(dbg_ref)

        y_sb, y_ca, g_sb, g_ca, merged = _merge(
            osb_ref[...], oca_ref[...], lg_ref[...], bg_ref[...], wsb_s[...], wca_s[...])
        mg_ref[...] = merged.astype(BF16)
        dm = _dot_nt(du_ref[...].astype(BF16), wout_ref[...])
        dl_sb = dm * y_sb * g_sb * (1.0 - g_sb)
        dl_ca = dm * y_ca * g_ca * (1.0 - g_ca)
        dlg_ref[:, :D_MODEL] = dl_sb.astype(BF16)
        dlg_ref[:, D_MODEL:] = dl_ca.astype(BF16)
        dbg_ref[:, :D_MODEL] += jnp.sum(dl_sb, axis=0, keepdims=True)
        dbg_ref[:, D_MODEL:] += jnp.sum(dl_ca, axis=0, keepdims=True)
        dy_sb = (dm * g_sb).astype(BF16)
        dy_ca = (dm * g_ca).astype(BF16)
        dysb_ref[...] = dy_sb
        dyca_ref[...] = dy_ca
        dosb_ref[...] = _dot_nt(dy_sb, wsb_s[...]).astype(BF16)
        doca_ref[...] = _dot_nt(dy_ca, wca_s[...]).astype(BF16)

    rows = lambda w: pl.BlockSpec((tm, w), lambda i: (i, 0))
    whole = lambda shape: pl.BlockSpec(shape, lambda i: (0,) * len(shape))
    return _pallas(
        body, guest, (du1, o_sb, o_ca, lg, _row(b_gate), wg_sb, wg_ca, w_out), name="mix_bwd", grid=(s // tm,),
        in_specs=[rows(D_MODEL), rows(ATT_WIDTH), rows(ATT_WIDTH), rows(GATE_COLS), whole((1, GATE_COLS)),
                  whole((N_DEV, ATT_WIDTH, PROJ_SHARD)), whole((N_DEV, ATT_WIDTH, PROJ_SHARD)),
                  whole((D_MODEL, D_MODEL))],
        out_specs=[rows(GATE_COLS), rows(ATT_WIDTH), rows(ATT_WIDTH), rows(D_MODEL), rows(D_MODEL),
                   rows(D_MODEL), whole((1, GATE_COLS))],
        out_shape=[jax.ShapeDtypeStruct((s, GATE_COLS), BF16),
                   jax.ShapeDtypeStruct((s, ATT_WIDTH), BF16), jax.ShapeDtypeStruct((s, ATT_WIDTH), BF16),
                   jax.ShapeDtypeStruct((s, D_MODEL), BF16), jax.ShapeDtypeStruct((s, D_MODEL), BF16),
                   jax.ShapeDtypeStruct((s, D_MODEL), BF16), jax.ShapeDtypeStruct((1, GATE_COLS), F32)],
        scratch_shapes=[pltpu.VMEM((ATT_WIDTH, D_MODEL), BF16), pltpu.VMEM((ATT_WIDTH, D_MODEL), BF16)],
        compiler_params=_params(("arbitrary",)),
    )


def _ln_stats(u):
    mu = jnp.mean(u, axis=1, keepdims=True)
    cen = u - mu
    var = jnp.mean(cen * cen, axis=1, keepdims=True)
    rstd = lax.rsqrt(var + LN_EPS)
    return cen * rstd, rstd


def _ln_bwd(dy, xhat, rstd, gain):
    dyg = dy * gain
    m1 = jnp.mean(dyg, axis=1, keepdims=True)
    m2 = jnp.mean(dyg * xhat, axis=1, keepdims=True)
    return rstd * (dyg - m1 - xhat * m2)


def _load_mlp_weights(win_hbm, wout_hbm, win_s, wout_s, sems):
    copies = [pltpu.make_async_copy(win_hbm.at[k], win_s.at[:, pl.ds(FF_SHARD * k, FF_SHARD)], sems.at[k])
              for k in range(N_DEV)]
    copies.append(pltpu.make_async_copy(wout_hbm, wout_s, sems.at[N_DEV]))
    for cp in copies:
        cp.start()
    return copies


MLP_WEIGHT_SCRATCH = [pltpu.VMEM((D_MODEL, D_FF), BF16), pltpu.VMEM((D_FF, D_MODEL), BF16),
                      pltpu.SemaphoreType.DMA((N_DEV + 1,))]


def mlp_block(u1, target, wg_in, w_out, ln1_g, ln1_b, ln2_g, ln2_b, tm=256):
    s = u1.shape[0]

    def body(u_ref, t_ref, win_hbm, wout_hbm, g1_ref, b1_ref, g2_ref, b2_ref,
             a_ref, x1b_ref, dhm_ref, du2b_ref, du1_ref, loss_ref, dg2_ref, db2_ref, dg1_ref, db1_ref,
             win_s, wout_s, sems, relu_s):
        @pl.when(pl.program_id(0) == 0)
        def _():
            copies = _load_mlp_weights(win_hbm, wout_hbm, win_s, wout_s, sems)
            for ref in (loss_ref, dg2_ref, db2_ref, dg1_ref, db1_ref):
                ref[...] = jnp.zeros_like(ref)
            for cp in copies:
                cp.wait()

        xhat1, rstd1 = _ln_stats(u_ref[...])
        x1 = xhat1 * g1_ref[...] + b1_ref[...]
        x1b = x1.astype(BF16)
        x1b_ref[...] = x1b
        for k in range(N_DEV):
            cols = slice(FF_SHARD * k, FF_SHARD * (k + 1))
            r = jnp.maximum(_dot(x1b, win_s[:, cols]), 0.0)
            relu_s[:, cols] = r
            a_ref[:, cols] = (r * r).astype(BF16)
        xhat2, rstd2 = _ln_stats(ALPHA * x1 + _dot(a_ref[...], wout_s[...]))
        diff = xhat2 * g2_ref[...] + b2_ref[...] - t_ref[...]
        per_token = jnp.mean(diff * diff, axis=1, keepdims=True)
        loss_ref[...] += 0.5 * jnp.sum(per_token, axis=0, keepdims=True)
        dy = diff * (1.0 / D_MODEL)
        dg2_ref[...] += jnp.sum(dy * xhat2, axis=0, keepdims=True)
        db2_ref[...] += jnp.sum(dy, axis=0, keepdims=True)
        du2 = _ln_bwd(dy, xhat2, rstd2, g2_ref[...])
        du2b = du2.astype(BF16)
        du2b_ref[...] = du2b
        for k in range(N_DEV):
            cols = slice(FF_SHARD * k, FF_SHARD * (k + 1))
            da = _dot_nt(du2b, wout_s[cols, :])
            dhm_ref[:, cols] = (2.0 * relu_s[:, cols] * da).astype(BF16)
        dx1 = ALPHA * du2 + _dot_nt(dhm_ref[...], win_s[...])
        dg1_ref[...] += jnp.sum(dx1 * xhat1, axis=0, keepdims=True)
        db1_ref[...] += jnp.sum(dx1, axis=0, keepdims=True)
        du1_ref[...] = _ln_bwd(dx1, xhat1, rstd1, g1_ref[...])

    rows = pl.BlockSpec((tm, D_MODEL), lambda i: (i, 0))
    vec = pl.BlockSpec((1, D_MODEL), lambda i: (0, 0))
    ff = pl.BlockSpec((tm, D_FF), lambda i: (i, 0))
    wide = jax.ShapeDtypeStruct((s, D_FF), BF16)
    narrow = jax.ShapeDtypeStruct((s, D_MODEL), BF16)
    stat = jax.ShapeDtypeStruct((1, D_MODEL), F32)
    return pl.pallas_call(
        body, name="mlp", grid=(s // tm,),
        in_specs=[rows, rows, HBM_SPEC, HBM_SPEC, vec, vec, vec, vec],
        out_specs=[ff, rows, ff, rows, rows, pl.BlockSpec((1, 1), lambda i: (0, 0)), vec, vec, vec, vec],
        out_shape=[wide, narrow, wide, narrow, jax.ShapeDtypeStruct((s, D_MODEL), F32),
                   jax.ShapeDtypeStruct((1, 1), F32), stat, stat, stat, stat],
        scratch_shapes=MLP_WEIGHT_SCRATCH + [pltpu.VMEM((tm, D_FF), F32)],
        compiler_params=_params(("arbitrary",)),
    )(u1, target, wg_in, w_out, _row(ln1_g), _row(ln1_b), _row(ln2_g), _row(ln2_b))


def dh_slab(pieces, tm=512):
    s = pieces[0].shape[0]
    widths = [p.shape[1] for p in pieces]
    assert sum(widths) == IN_COLS
    n = len(pieces)

    def body(*refs):
        dh_ref = refs[n]
        off = 0
        for ref, w in zip(refs[:n], widths):
            dh_ref[:, off:off + w] = ref[...].astype(BF16)
            off += w

    rows = lambda w: pl.BlockSpec((tm, w), lambda i: (i, 0))
    return pl.pallas_call(
        body, name="dh_slab", grid=(s // tm,),
        in_specs=[rows(w) for w in widths], out_specs=rows(IN_COLS),
        out_shape=jax.ShapeDtypeStruct((s, IN_COLS), BF16),
        compiler_params=_params(("arbitrary",)),
    )(*pieces)


def in_bwd(du1, dh, wg_in, tm=512, guest=None):
    s = du1.shape[0]

    def body(du_ref, dh_ref, w_hbm, gx_ref, pairs_s, sems):
        @pl.when(pl.program_id(0) == 0)
        def _():
            _load_in_pairs(w_hbm, pairs_s, sems)

        acc = ALPHA * du_ref[...]
        for j in range(N_DEV // 2):
            acc = acc + _dot_nt(dh_ref[:, IN_PAIR * j:IN_PAIR * (j + 1)], pairs_s[j])
        gx_ref[...] = acc

    rows = lambda w: pl.BlockSpec((tm, w), lambda i: (i, 0))
    return _pallas(
        body, guest, (du1, dh, wg_in), name="in_bwd", grid=(s // tm,), scratch_shapes=IN_PAIR_SCRATCH,
        in_specs=[rows(D_MODEL), rows(IN_COLS), HBM_SPEC],
        out_specs=[rows(D_MODEL)],
        out_shape=[jax.ShapeDtypeStruct((s, D_MODEL), F32)],
        compiler_params=_params(("arbitrary",)),
    )


def wgrad(name, a, b, tm, tn, shard_cols=None, tk=2048):
    kk, m = a.shape
    n = b.shape[1]
    while kk % tk:
        tk //= 2
    nk = kk // tk
    w = shard_cols

    def body(a_ref, b_ref, out_ref, acc_s):
        k = pl.program_id(2)

        @pl.when(k == 0)
        def _():
            acc_s[...] = jnp.zeros_like(acc_s)

        acc_s[...] += _dot_tn(a_ref[...].astype(BF16), b_ref[...].astype(BF16))

        @pl.when(k == nk - 1)
        def _():
            if w is None:
                out_ref[...] = acc_s[...]
            else:
                for j in range(tn // w):
                    out_ref[j] = acc_s[:, w * j:w * (j + 1)]

    if w is None:
        out_spec = pl.BlockSpec((tm, tn), lambda i, j, k: (i, j))
        out_shape = jax.ShapeDtypeStruct((m, n), F32)
    else:
        out_spec = pl.BlockSpec((tn // w, tm, w), lambda i, j, k: (j, i, 0))
        out_shape = jax.ShapeDtypeStruct((n // w, m, w), F32)
    return pl.pallas_call(
        body, name=name, grid=(m // tm, n // tn, nk),
        in_specs=[pl.BlockSpec((tk, tm), lambda i, j, k: (k, i)),
                  pl.BlockSpec((tk, tn), lambda i, j, k: (k, j))],
        out_specs=out_spec, out_shape=out_shape,
        scratch_shapes=[pltpu.VMEM((tm, tn), F32)],
        compiler_params=_params(("arbitrary", "arbitrary", "arbitrary")),
    )(a, b)


def adamw(name, w, m, v, parts, tm=None, guest=None):
    rows, cols = w.shape
    tm = rows if tm is None else min(tm, rows)
    n = len(parts)
    c1 = 1.0 - ADAM_B1 ** ADAM_STEP
    c2 = 1.0 - ADAM_B2 ** ADAM_STEP

    def body(*refs):
        w_ref, m_ref, v_ref = refs[:3]
        part_refs = refs[3:3 + n]
        g_ref, d_ref, nm_ref, nv_ref = refs[3 + n:]
        g = None
        for ref, (_, index) in zip(part_refs, parts):
            term = (ref[...] if index is None else ref[0]).astype(F32)
            g = term if g is None else g + term
        new_m = ADAM_B1 * m_ref[...] + (1.0 - ADAM_B1) * g
        new_v = ADAM_B2 * v_ref[...] + (1.0 - ADAM_B2) * (g * g)
        m_hat = new_m / c1
        v_hat = new_v / c2
        g_ref[...] = g
        d_ref[...] = -ADAM_LR * (m_hat / (jnp.sqrt(v_hat) + ADAM_EPS) + ADAM_WD * w_ref[...])
        nm_ref[...] = new_m
        nv_ref[...] = new_v

    spec = pl.BlockSpec((tm, cols), lambda i: (i, 0))
    shape = jax.ShapeDtypeStruct((rows, cols), F32)

    def part_spec(index):
        if index is None:
            return spec
        return pl.BlockSpec((1, tm, cols), lambda i: (index, i, 0))

    return _pallas(
        body, guest, (w, m, v, *[array for array, _ in parts]), name=name, grid=(rows // tm,),
        in_specs=[spec] * 3 + [part_spec(index) for _, index in parts],
        out_specs=[spec] * 4, out_shape=[shape] * 4,
        compiler_params=_params(("arbitrary",)),
    )


def _place():
    x, y, c = lax.axis_index("x"), lax.axis_index("y"), lax.axis_index("c")
    other_chips = [(1 - x, y), (x, 1 - y), (1 - x, 1 - y)]
    return x, y, c, other_chips


def place_shards(arrays):
    n = len(arrays)

    def body(*refs):
        ins, outs, stage, sems = refs[:n], refs[n:2 * n], refs[2 * n:3 * n], refs[3 * n]
        x, y, c, _ = _place()
        copies = []
        for a in range(n):
            stage[a][...] = ins[a][...].astype(BF16)
            copies.append(pltpu.make_async_copy(stage[a], outs[a].at[4 * x + 2 * y + c], sems.at[a]))
            copies[-1].start()
        for cp in copies:
            cp.wait()

    return pl.pallas_call(
        body, name="place_shards",
        in_specs=[pl.BlockSpec(memory_space=pltpu.VMEM)] * n, out_specs=[HBM_SPEC] * n,
        out_shape=[jax.ShapeDtypeStruct((N_DEV,) + a.shape, BF16) for a in arrays],
        scratch_shapes=[pltpu.VMEM(a.shape, BF16) for a in arrays] + [pltpu.SemaphoreType.DMA((n,))],
        compiler_params=_params(),
    )(*arrays)


def all_gather(name, arrays):
    n = len(arrays)

    def body(*refs):
        ins, outs = refs[:n], refs[n:2 * n]
        send_sems, recv_sems, local_sems = refs[2 * n:]
        x, y, c, chips = _place()
        me = 4 * x + 2 * y + c
        sibling = (x, y, 1 - c)

        def copy(a, k, block, to, src=None):
            return pltpu.make_async_remote_copy(
                src_ref=outs[a].at[block] if src is None else src, dst_ref=outs[a].at[block],
                send_sem=send_sems.at[a, k], recv_sem=recv_sems.at[a, k],
                device_id=to, device_id_type=MESH)

        started = []
        local = [pltpu.make_async_copy(ins[a], outs[a].at[me], local_sems.at[a]) for a in range(n)]
        for a in range(n):
            local[a].start()
            sends = [copy(a, 0, me, sibling, src=ins[a])]
            sends += [copy(a, 1 + j, me, (cx, cy, c), src=ins[a]) for j, (cx, cy) in enumerate(chips)]
            for cp in sends:
                cp.start()
            started += sends
        for a in range(n):
            for j, (cx, cy) in enumerate(chips):
                block = 4 * cx + 2 * cy + c
                copy(a, 1 + j, block, sibling).wait_recv()
                passed = copy(a, 4 + j, block, sibling)
                passed.start()
                started.append(passed)
        for a in range(n):
            copy(a, 0, 4 * x + 2 * y + (1 - c), sibling).wait_recv()
            for j, (cx, cy) in enumerate(chips):
                copy(a, 4 + j, 4 * cx + 2 * cy + (1 - c), sibling).wait_recv()
        for cp in started:
            cp.wait_send()
        for cp in local:
            cp.wait()

    return pl.pallas_call(
        body, name=name,
        in_specs=[pl.BlockSpec(memory_space=pltpu.VMEM)] * n, out_specs=[HBM_SPEC] * n,
        out_shape=[jax.ShapeDtypeStruct((N_DEV,) + a.shape, a.dtype) for a in arrays],
        scratch_shapes=[pltpu.SemaphoreType.DMA((n, 7)), pltpu.SemaphoreType.DMA((n, 7)),
                        pltpu.SemaphoreType.DMA((n,))],
    )(*arrays)


def _gather_first_plan(n):
    def plan(zones):
        x, y, c, chips = _place()
        me = 4 * x + 2 * y + c
        copies = []
        for zone in zones[:n]:
            copies.append((zone.at[me], zone.at[me], (x, y, 1 - c), zone.at[4 * x + 2 * y + (1 - c)]))
            copies += [(zone.at[me], zone.at[me], (cx, cy, c), zone.at[4 * cx + 2 * cy + c]) for cx, cy in chips]
        return copies
    return plan


def _gather_second_plan(n):
    def plan(zones):
        x, y, c, chips = _place()
        return [(zone.at[4 * cx + 2 * cy + c], zone.at[4 * cx + 2 * cy + c], (x, y, 1 - c),
                 zone.at[4 * cx + 2 * cy + (1 - c)]) for zone in zones[:n] for cx, cy in chips]
    return plan


def _both_plans(first, n_first, second):
    return lambda bufs: first(bufs[:n_first]) + second(bufs[n_first:])


def _sibling_plan(n):
    def plan(bufs):
        x, y, c, _ = _place()
        return [(bufs[a].at[:, 1 - c], bufs[n + a], (x, y, 1 - c), bufs[n + a]) for a in range(n)]
    return plan


def _chips_plan(n):
    def plan(bufs):
        _, _, c, chips = _place()
        return [(bufs[a].at[2 * cx + cy], bufs[n + a].at[j], (cx, cy, c), bufs[n + a].at[j])
                for a in range(n) for j, (cx, cy) in enumerate(chips)]
    return plan


def chip_sum(name, grad, landed, place, tr=256):
    _, _, rows, cols = grad.shape
    tr = min(tr, rows)

    def body(place_ref, g_ref, l_ref, own_ref, out_ref):
        total = g_ref[0, 0] + l_ref[0]
        out_ref[0] = total.astype(BF16)

        @pl.when(pl.program_id(1) == place_ref[1])
        def _():
            own_ref[...] = total

    return pl.pallas_call(
        body, name=name,
        grid_spec=pltpu.PrefetchScalarGridSpec(
            num_scalar_prefetch=1, grid=(rows // tr, 4),
            in_specs=[pl.BlockSpec((1, 1, tr, cols), lambda i, ch, pos: (ch, pos[0], i, 0)),
                      pl.BlockSpec((1, tr, cols), lambda i, ch, pos: (ch, i, 0))],
            out_specs=[pl.BlockSpec((tr, cols), lambda i, ch, pos: (i, 0)),
                       pl.BlockSpec((1, tr, cols), lambda i, ch, pos: (ch, i, 0))]),
        out_shape=[jax.ShapeDtypeStruct((rows, cols), F32), jax.ShapeDtypeStruct((4, rows, cols), BF16)],
        compiler_params=_params(("arbitrary", "arbitrary")),
    )(place, grad, landed)


BIG =("w_in", "w_sb_proj", "w_ca_proj", "w_out", "w_mlp_in", "w_mlp_out")
SMALL = ("b_gate", "rel_bias", "ln1_g", "ln1_b", "ln2_g", "ln2_b")
NAMES = ("w_in", "b_gate", "w_sb_proj", "w_ca_proj", "rel_bias", "w_out", "ln1_g", "ln1_b",
         "w_mlp_in", "w_mlp_out", "ln2_g", "ln2_b")
REL_COLS = 2 * REL_CLIP + 1


def _pack_small(t, scalar=None):
    rel = t["rel_bias"]
    if rel.shape[1] != REL_PAD:
        rel = jnp.pad(rel, ((0, 0), (0, REL_PAD - rel.shape[1])))
    last = jnp.zeros((128,), F32) if scalar is None else jnp.pad(scalar.reshape(1), (0, 127))
    flat = [t["b_gate"].reshape(-1), rel.reshape(-1)] + [t[n].reshape(-1) for n in SMALL[2:]] + [last]
    return jnp.concatenate(flat).reshape(-1, 128)


def _unpack_small(p):
    flat = p.reshape(-1)
    out, off = {}, 0
    for n, size in (("b_gate", GATE_COLS), ("rel_bias", 8 * REL_PAD)) + tuple((n, D_MODEL) for n in SMALL[2:]):
        out[n] = flat[off:off + size]
        off += size
    out["rel_bias"] = out["rel_bias"].reshape(8, REL_PAD)[:, :REL_COLS]
    return out


def kernel(x, w_in, b_gate, w_sb_proj, w_ca_proj, rel_bias, w_out, ln1_g, ln1_b, w_mlp_in, w_mlp_out, ln2_g, ln2_b, loss_target, m_w_in, m_b_gate, m_w_sb_proj, m_w_ca_proj, m_rel_bias, m_w_out, m_ln1_g, m_ln1_b, m_w_mlp_in, m_w_mlp_out, m_ln2_g, m_ln2_b, v_w_in, v_b_gate, v_w_sb_proj, v_w_ca_proj, v_rel_bias, v_w_out, v_ln1_g, v_ln1_b, v_w_mlp_in, v_w_mlp_out, v_ln2_g, v_ln2_b):
    w = dict(w_in=w_in, b_gate=b_gate, w_sb_proj=w_sb_proj, w_ca_proj=w_ca_proj, rel_bias=rel_bias, w_out=w_out,
             ln1_g=ln1_g, ln1_b=ln1_b, w_mlp_in=w_mlp_in, w_mlp_out=w_mlp_out, ln2_g=ln2_g, ln2_b=ln2_b)
    m = dict(w_in=m_w_in, b_gate=m_b_gate, w_sb_proj=m_w_sb_proj, w_ca_proj=m_w_ca_proj, rel_bias=m_rel_bias,
             w_out=m_w_out, ln1_g=m_ln1_g, ln1_b=m_ln1_b, w_mlp_in=m_w_mlp_in, w_mlp_out=m_w_mlp_out,
             ln2_g=m_ln2_g, ln2_b=m_ln2_b)
    v = dict(w_in=v_w_in, b_gate=v_b_gate, w_sb_proj=v_w_sb_proj, w_ca_proj=v_w_ca_proj, rel_bias=v_rel_bias,
             w_out=v_w_out, ln1_g=v_ln1_g, ln1_b=v_ln1_b, w_mlp_in=v_w_mlp_in, w_mlp_out=v_w_mlp_out,
             ln2_g=v_ln2_g, ln2_b=v_ln2_b)

    x, target = x[0], loss_target[0]
    c = lax.axis_index("c")
    place = jnp.stack([c, 2 * lax.axis_index("x") + lax.axis_index("y")]).astype(jnp.int32)
    grads, deltas, new_m, new_v = {}, {}, {}, {}

    def by_owner(n, g):
        return g.reshape((4, 2) + w[n].shape)

    def finish(n, own, arrived, guest=None):
        out = adamw("adamw_" + n, w[n], m[n], v[n], [(own, None), (arrived, 0), (arrived, 1), (arrived, 2)],
                    tm=256, guest=guest)
        out, bufs = (out, None) if guest is None else out
        grads[n], deltas[n], new_m[n], new_v[n] = out
        return bufs

    def siblings(names, g):
        bufs = [by_owner(n, g[n]) for n in names] + [lax.empty((4,) + w[n].shape, F32) for n in names]
        return Exchange(tuple(bufs), _sibling_plan(len(names)), len(names))

    def chip_sums(names, bufs):
        k = len(names)
        own, sums = zip(*[chip_sum("chip_sum_" + n, bufs[a], bufs[k + a], place) for a, n in enumerate(names)])
        zones = [lax.empty((3,) + w[n].shape, BF16) for n in names]
        return own, Exchange(tuple(sums) + tuple(zones), _chips_plan(k), 3 * k)

    proj, ffn = ("w_sb_proj", "w_ca_proj", "w_out"), ("w_mlp_in", "w_mlp_out")
    zones = place_shards([w[n] for n in proj + ffn])
    proj_zones, ffn_zones = tuple(zones[:len(proj)]), tuple(zones[len(proj):])
    wg_in = all_gather("gather_w_in", [w_in.astype(BF16)])[0]
    (qkv, lg, xb), proj_zones = in_proj(
        x, wg_in, guest=Exchange(proj_zones, _gather_first_plan(len(proj)), 4 * len(proj)))
    bias_tiles = ca_bias_tiles(rel_bias)
    both = Exchange(ffn_zones + tuple(proj_zones),
                    _both_plans(_gather_first_plan(len(ffn)), len(ffn), _gather_second_plan(len(proj))),
                    4 * len(ffn) + 3 * len(proj))
    (o_sb,), zones = sb_fwd(qkv, both)
    ffn_zones, proj_zones = tuple(zones[:len(ffn)]), zones[len(ffn):]
    (o_ca,), ffn_zones = ca_fwd(qkv, bias_tiles, Exchange(ffn_zones, _gather_second_plan(len(ffn)), 3 * len(ffn)))
    wg = dict(zip(proj + ffn, list(proj_zones) + list(ffn_zones)))
    wg_sb, wg_ca, wg_mi = wg["w_sb_proj"], wg["w_ca_proj"], wg["w_mlp_in"]
    wg_mo = wg["w_mlp_out"].reshape(D_FF, D_MODEL)
    w_out_b = wg["w_out"].reshape(D_MODEL, D_MODEL)

    u1 = mix_fwd(x, o_sb, o_ca, lg, b_gate, wg_sb, wg_ca, w_out_b)
    a, x1b, dhm, du2b, du1, loss, dg2, db2, dg1, db1 = mlp_block(
        u1, target, wg_mi, wg_mo, ln1_g, ln1_b, ln2_g, ln2_b)
    g = {}
    g["w_mlp_in"] = wgrad("wgrad_mlp_in", x1b, dhm, tm=D_MODEL, tn=FF_SHARD, shard_cols=FF_SHARD)
    g["w_mlp_out"] = wgrad("wgrad_mlp_out", a, du2b, tm=FF_SHARD, tn=D_MODEL)

    mlp = ("w_mlp_in", "w_mlp_out")
    (dlg, do_sb, do_ca, dy_sb, dy_ca, merged, dbg), bufs = mix_bwd(
        du1, o_sb, o_ca, lg, b_gate, wg_sb, wg_ca, w_out_b, guest=siblings(mlp, g))
    own_mlp, mlp_chips = chip_sums(mlp, bufs)
    g["w_out"] = wgrad("wgrad_out", merged, du1, tm=D_MODEL, tn=D_MODEL)
    g["w_sb_proj"] = wgrad("wgrad_sb_proj", o_sb, dy_sb, tm=ATT_WIDTH, tn=D_MODEL, shard_cols=PROJ_SHARD)
    g["w_ca_proj"] = wgrad("wgrad_ca_proj", o_ca, dy_ca, tm=ATT_WIDTH, tn=D_MODEL, shard_cols=PROJ_SHARD)

    att = ("w_out", "w_sb_proj", "w_ca_proj")
    att_siblings = siblings(att, g)
    both = Exchange(mlp_chips.bufs + att_siblings.bufs,
                    _both_plans(mlp_chips.plan, len(mlp_chips.bufs), att_siblings.plan),
                    mlp_chips.copies + att_siblings.copies)
    (dq_sb, dk_sb, dv_sb), bufs = sb_bwd(qkv, o_sb, do_sb, guest=both)
    arrived_mlp, bufs = bufs[len(mlp):len(mlp_chips.bufs)], bufs[len(mlp_chips.bufs):]
    own_att, att_chips = chip_sums(att, bufs)
    (dq_ca, dk_ca, dv_ca, db_tiles), bufs = ca_bwd(qkv, bias_tiles, do_ca, guest=att_chips)
    arrived_att = bufs[len(att):]
    g_rel = rel_bias_grad(db_tiles)

    dh = dh_slab([dq_sb, dk_sb, dv_sb, dq_ca, dk_ca, dv_ca, dlg])
    g["w_in"] = wgrad("wgrad_in", xb, dh, tm=D_MODEL, tn=2 * IN_SHARD, shard_cols=IN_SHARD)
    last = ("w_in",)
    bufs = finish(mlp[0], own_mlp[0], arrived_mlp[0], guest=siblings(last, g))
    finish(mlp[1], own_mlp[1], arrived_mlp[1])
    for n, own, arrived in zip(att, own_att, arrived_att):
        finish(n, own, arrived)
    own_in, in_chips = chip_sums(last, bufs)
    (grad_x,), bufs = in_bwd(du1, dh, wg_in, guest=in_chips)
    finish(last[0], own_in[0], bufs[1])

    small = dict(b_gate=dbg, rel_bias=g_rel, ln1_g=dg1, ln1_b=db1, ln2_g=dg2, ln2_b=db2)
    parts = all_gather("gather_small_grads", [_pack_small(small, loss[0, 0])])[0]
    packed = adamw("adamw_small", _pack_small(w), _pack_small(m), _pack_small(v),
                   [(parts, d) for d in range(N_DEV)])
    loss = packed[0][-1, 0]
    for out, p in zip((grads, deltas, new_m, new_v), packed):
        for n, val in _unpack_small(p).items():
            out[n] = val.reshape(w[n].shape)

    return (loss, grad_x[None], *[grads[n] for n in NAMES], *[deltas[n] for n in NAMES],
            *[new_m[n] for n in NAMES], *[new_v[n] for n in NAMES])
```

```python
import functools
from typing import NamedTuple

import jax
import jax.numpy as jnp
from jax import lax
from jax.experimental import pallas as pl
from jax.experimental.pallas import tpu as pltpu

F32 = jnp.float32
BF16 = jnp.bfloat16
MESH = pl.DeviceIdType.MESH

N_DEV = 8
D_MODEL = 1024
HEAD_DIM = 64
ATT_WIDTH = 512
N_PAIRS = ATT_WIDTH // 128
QKV_COLS = 6 * ATT_WIDTH
GATE_COLS = 2 * D_MODEL
IN_COLS = QKV_COLS + GATE_COLS
IN_SHARD = IN_COLS // N_DEV
D_FF = 4 * D_MODEL
FF_SHARD = D_FF // N_DEV
PROJ_SHARD = D_MODEL // N_DEV
ATT_BLOCK = 128
CA_TILES = 5
CA_ROWS = 1024
CHUNK = 64
CA_PREV_CHUNKS = 8
REL_CLIP = 256
REL_PAD = 640
ALPHA = 2.0 ** 0.25
LN_EPS = 1e-5
QK_SCALE = HEAD_DIM ** -0.5
NEG_BIG = -1e30
VMEM_LIMIT = 56 * 1024 * 1024

ADAM_LR = 0.001
ADAM_B1 = 0.9
ADAM_B2 = 0.999
ADAM_EPS = 1e-08
ADAM_WD = 0.01
ADAM_STEP = 10

_NT = (((1,), (1,)), ((), ()))
_TN = (((0,), (0,)), ((), ()))


def _dot(a, b):
    return jnp.dot(a, b, preferred_element_type=F32)


def _dot_nt(a, b):
    return lax.dot_general(a, b, _NT, preferred_element_type=F32)


def _dot_tn(a, b):
    return lax.dot_general(a, b, _TN, preferred_element_type=F32)


def _params(semantics=None):
    return pltpu.CompilerParams(dimension_semantics=semantics, vmem_limit_bytes=VMEM_LIMIT)


def _row(v):
    return v.reshape(1, -1)


HBM_SPEC = pl.BlockSpec(memory_space=pl.ANY)


class Exchange(NamedTuple):
    bufs: tuple
    plan: object
    copies: int


def _pallas(body, guest, args, *, grid, in_specs, out_specs, out_shape, scratch_shapes=(), **kw):
    out_specs, out_shape, scratch_shapes = list(out_specs), list(out_shape), list(scratch_shapes)
    if guest is None:
        return pl.pallas_call(body, grid=grid, in_specs=in_specs, out_specs=out_specs, out_shape=out_shape,
                              scratch_shapes=scratch_shapes, **kw)(*args)
    nb, n_own = len(guest.bufs), len(in_specs) + len(out_specs)

    def hosting(*refs):
        zones = refs[nb + n_own:2 * nb + n_own]
        send_sems, recv_sems = refs[-2], refs[-1]
        steps = [pl.program_id(axis) for axis in range(len(grid))]
        first = functools.reduce(jnp.logical_and, [step == 0 for step in steps])
        last = functools.reduce(jnp.logical_and, [step == n - 1 for step, n in zip(steps, grid)])

        def copy(k, src, dst, peer):
            return pltpu.make_async_remote_copy(src_ref=src, dst_ref=dst, send_sem=send_sems.at[k],
                                                recv_sem=recv_sems.at[k], device_id=peer, device_id_type=MESH)

        @pl.when(first)
        def _():
            for k, (src, dst, peer, _) in enumerate(guest.plan(zones)):
                copy(k, src, dst, peer).start()

        body(*refs[nb:nb + n_own], *refs[2 * nb + n_own:-2])

        @pl.when(last)
        def _():
            for k, (src, dst, peer, landed) in enumerate(guest.plan(zones)):
                copy(k, src, dst, peer).wait_send()
                copy(k, src, landed, peer).wait_recv()

    sems = pltpu.SemaphoreType.DMA((guest.copies,))
    results = pl.pallas_call(
        hosting, grid=grid, in_specs=[HBM_SPEC] * nb + list(in_specs), out_specs=out_specs + [HBM_SPEC] * nb,
        out_shape=out_shape + [jax.ShapeDtypeStruct(b.shape, b.dtype) for b in guest.bufs],
        input_output_aliases={k: len(out_specs) + k for k in range(nb)},
        scratch_shapes=scratch_shapes + [sems, sems], **kw)(*guest.bufs, *args)
    return results[:len(out_specs)], results[len(out_specs):]


IN_PAIR = 2 * IN_SHARD
IN_PAIR_SCRATCH = [pltpu.VMEM((N_DEV // 2, D_MODEL, IN_PAIR), BF16), pltpu.SemaphoreType.DMA((N_DEV,))]


def _load_in_pairs(w_hbm, pairs_s, sems):
    copies = [pltpu.make_async_copy(w_hbm.at[j], pairs_s.at[j // 2, :, pl.ds(IN_SHARD * (j % 2), IN_SHARD)],
                                    sems.at[j]) for j in range(N_DEV)]
    for cp in copies:
        cp.start()
    for cp in copies:
        cp.wait()


def in_proj(x, wg_in, tm=512, guest=None):
    s = x.shape[0]

    def body(x_ref, w_hbm, qkv_ref, lg_ref, xb_ref, pairs_s, sems):
        @pl.when(pl.program_id(0) == 0)
        def _():
            _load_in_pairs(w_hbm, pairs_s, sems)

        xb = x_ref[...].astype(BF16)
        xb_ref[...] = xb
        for j in range(N_DEV // 2):
            acc = _dot(xb, pairs_s[j])
            lo, hi = IN_PAIR * j, IN_PAIR * (j + 1)
            if hi <= QKV_COLS:
                qkv_ref[:, lo:hi] = acc.astype(BF16)
            elif lo >= QKV_COLS:
                lg_ref[:, lo - QKV_COLS:hi - QKV_COLS] = acc
            else:
                qkv_ref[:, lo:QKV_COLS] = acc[:, :QKV_COLS - lo].astype(BF16)
                lg_ref[:, 0:hi - QKV_COLS] = acc[:, QKV_COLS - lo:]

    return _pallas(
        body, guest, (x, wg_in), name="in_proj", grid=(s // tm,), scratch_shapes=IN_PAIR_SCRATCH,
        in_specs=[pl.BlockSpec((tm, D_MODEL), lambda i: (i, 0)), HBM_SPEC],
        out_specs=[pl.BlockSpec((tm, QKV_COLS), lambda i: (i, 0)),
                   pl.BlockSpec((tm, GATE_COLS), lambda i: (i, 0)),
                   pl.BlockSpec((tm, D_MODEL), lambda i: (i, 0))],
        out_shape=[jax.ShapeDtypeStruct((s, QKV_COLS), BF16),
                   jax.ShapeDtypeStruct((s, GATE_COLS), F32),
                   jax.ShapeDtypeStruct((s, D_MODEL), BF16)],
        compiler_params=_params(("arbitrary",)),
    )


def _head_masks():
    lane = lax.broadcasted_iota(jnp.int32, (1, 128), 1)
    first = lane < HEAD_DIM
    return first, jnp.logical_not(first)


SB_CHUNK = 512
SB_SUB = SB_CHUNK // ATT_BLOCK


def _tri_pair():
    row = lax.broadcasted_iota(jnp.int32, (4 * ATT_BLOCK, 2 * ATT_BLOCK), 0) & (2 * ATT_BLOCK - 1)
    col = lax.broadcasted_iota(jnp.int32, (4 * ATT_BLOCK, 2 * ATT_BLOCK), 1)
    same = (row >= ATT_BLOCK) == (col >= ATT_BLOCK)
    return jnp.logical_and(same, row >= col).astype(BF16)


def _pair_bf16(pieces):
    return jnp.concatenate(
        [jnp.concatenate([p.astype(BF16) for p in row], axis=1) for row in pieces], axis=0)


def _suffix_sums(pieces, tri_pair):
    hi = [[p.astype(BF16) for p in row] for row in pieces]
    lo = [[p - h.astype(F32) for p, h in zip(row, hrow)] for row, hrow in zip(pieces, hi)]
    return _dot(jnp.concatenate([_pair_bf16(hi), _pair_bf16(lo)], axis=1), tri_pair)


def _first_row_piece(diag, j):
    return j if diag else 0


def _piece(v, r, h):
    return v[ATT_BLOCK * r:ATT_BLOCK * (r + 1), ATT_BLOCK * h:ATT_BLOCK * (h + 1)]


def _all_row_pieces(slabs):
    return [[slab[ATT_BLOCK * r:ATT_BLOCK * (r + 1)] for r in range(SB_SUB)] for slab in slabs]


def _add_rows(total, part, r0):
    if r0 == 0:
        return total + part
    return jnp.concatenate([total[:ATT_BLOCK * r0], total[ATT_BLOCK * r0:] + part], axis=0)


def _row_totals(csum, r0, h):
    return jnp.concatenate(
        [jnp.broadcast_to(_piece(csum, at, h)[:, 0:1], (ATT_BLOCK, ATT_BLOCK)) for at in range(SB_SUB - r0)], axis=0)


SIGN_BIT = 0x80000000


def _drop(z):
    minus_abs = lax.bitcast_convert_type(lax.bitcast_convert_type(z, jnp.uint32) | jnp.uint32(SIGN_BIT), F32)
    return jnp.maximum(z, 0.0) + jnp.log(1.0 + jnp.exp(minus_abs))


def _sb_weights(z, runs, r0, masked, strict, tri_pair):
    drops = []
    for r in range(r0, SB_SUB):
        row = [_drop(_piece(z, r - r0, h)) for h in range(2)]
        drops.append([jnp.where(strict, drop, 0.0) for drop in row] if r == masked else row)
    csum = _suffix_sums(drops, tri_pair)
    weights = []
    for r in range(r0, SB_SUB):
        row = [jnp.exp(_piece(z, r - r0, h) - (_piece(csum, r - r0, h) + runs[h][r])) for h in range(2)]
        weights.append([jnp.where(strict, a, 0.0) for a in row] if r == masked else row)
    return drops, weights, csum


EXP_UNDERFLOW = 104.0


SB_STEP = 2


def _sweep_left(span, i, dropped_ref):
    def more(at):
        return jnp.logical_and(at >= 0, jnp.min(dropped_ref[...]) < EXP_UNDERFLOW)

    def step(at):
        span(at * (SB_STEP * ATT_BLOCK), SB_STEP, False)
        return at - 1

    lax.while_loop(more, step, i * (SB_SUB // SB_STEP) - 1)


def _by_head(blk, masks):
    zero = jnp.zeros_like(blk)
    return jnp.concatenate([jnp.where(m, blk, zero) for m in masks], axis=0)


def sb_fwd(qkv, guest=None):
    s = qkv.shape[0]
    nq = s // SB_CHUNK

    def body(q_ref, k_ref, v_ref, o_ref, run_s):
        i = pl.program_id(1)
        masks = _head_masks()
        q = q_ref[...] * QK_SCALE
        row = lax.broadcasted_iota(jnp.int32, (ATT_BLOCK, ATT_BLOCK), 0)
        col = lax.broadcasted_iota(jnp.int32, (ATT_BLOCK, ATT_BLOCK), 1)
        strict = col < row
        tri_pair = _tri_pair()
        run_s[...] = jnp.zeros_like(run_s)
        o_ref[...] = jnp.zeros_like(o_ref)

        def span(start, pieces, diag):
            runs = [run_s[0], run_s[1]]
            acc = jnp.zeros((SB_CHUNK, ATT_BLOCK), F32)
            for j in reversed(range(pieces)):
                r0 = _first_row_piece(diag, j)
                keys = pl.ds(pl.multiple_of(start + ATT_BLOCK * j, ATT_BLOCK), ATT_BLOCK)
                z = _dot_nt(q[ATT_BLOCK * r0:], _by_head(k_ref[keys, :], masks))
                _, weights, csum = _sb_weights(z, _all_row_pieces(runs), r0, j if diag else None, strict, tri_pair)
                part = _dot(_pair_bf16(weights), _by_head(v_ref[keys, :], masks))
                acc = _add_rows(acc, part, r0)
                runs = [_add_rows(runs[h], _row_totals(csum, r0, h), r0) for h in range(2)]
            run_s[0], run_s[1] = runs
            o_ref[...] += acc

        span(i * SB_CHUNK, SB_SUB, True)
        _sweep_left(span, i, run_s)

    return _pallas(
        body, guest, (qkv, qkv, qkv), name="sb_fwd", grid=(N_PAIRS, nq),
        in_specs=[pl.BlockSpec((SB_CHUNK, 128), lambda p, i: (i, p)),
                  pl.BlockSpec((s, 128), lambda p, i: (0, N_PAIRS + p)),
                  pl.BlockSpec((s, 128), lambda p, i: (0, 2 * N_PAIRS + p))],
        out_specs=[pl.BlockSpec((SB_CHUNK, 128), lambda p, i: (i, p))],
        out_shape=[jax.ShapeDtypeStruct((s, ATT_WIDTH), F32)],
        scratch_shapes=[pltpu.VMEM((2, SB_CHUNK, 128), F32)],
        compiler_params=_params(("arbitrary", "arbitrary")),
    )


def sb_bwd(qkv, o_sb, do_sb, guest=None):
    s = qkv.shape[0]
    nq = s // SB_CHUNK

    def body(q_ref, k_ref, v_ref, o_ref, do_ref, dq_ref, dk_ref, dv_ref, run_l_s, run_e_s, etot_s, dq_s, dk_s, dv_s):
        i = pl.program_id(1)

        @pl.when(i == 0)
        def _():
            dk_s[...] = jnp.zeros_like(dk_s)
            dv_s[...] = jnp.zeros_like(dv_s)

        masks = _head_masks()
        q = q_ref[...] * QK_SCALE
        do = do_ref[...]
        prod = do.astype(F32) * o_ref[...]
        for h in range(2):
            total = jnp.sum(jnp.where(masks[h], prod, 0.0), axis=1, keepdims=True)
            etot_s[h] = jnp.broadcast_to(total, (SB_CHUNK, ATT_BLOCK))
        row = lax.broadcasted_iota(jnp.int32, (ATT_BLOCK, ATT_BLOCK), 0)
        col = lax.broadcasted_iota(jnp.int32, (ATT_BLOCK, ATT_BLOCK), 1)
        strict = col < row
        tri_pair = _tri_pair()
        run_l_s[...] = jnp.zeros_like(run_l_s)
        run_e_s[...] = jnp.zeros_like(run_e_s)
        dq_s[...] = jnp.zeros_like(dq_s)

        def span(start, pieces, diag):
            slabs_l = [run_l_s[0], run_l_s[1]]
            slabs_e = [run_e_s[0], run_e_s[1]]
            e_tot = _all_row_pieces([etot_s[0], etot_s[1]])
            dq = jnp.zeros((SB_CHUNK, ATT_BLOCK), F32)
            for j in reversed(range(pieces)):
                runs_l, runs_e = _all_row_pieces(slabs_l), _all_row_pieces(slabs_e)
                r0 = _first_row_piece(diag, j)
                masked = j if diag else None
                keys = pl.ds(pl.multiple_of(start + ATT_BLOCK * j, ATT_BLOCK), ATT_BLOCK)
                kk = _by_head(k_ref[keys, :], masks)
                vv = _by_head(v_ref[keys, :], masks)
                q_rows, do_rows = q[ATT_BLOCK * r0:], do[ATT_BLOCK * r0:]
                z = _dot_nt(q_rows, kk)
                da = _dot_nt(do_rows, vv)
                drops, weights, csum = _sb_weights(z, runs_l, r0, masked, strict, tri_pair)
                ab = [[a.astype(BF16) for a in wrow] for wrow in weights]
                es = [[a.astype(F32) * _piece(da, at, h) for h, a in enumerate(arow)] for at, arow in enumerate(ab)]
                esum = _suffix_sums(es, tri_pair)
                dzs = []
                for at, r in enumerate(range(r0, SB_SUB)):
                    dz_row = []
                    for h in range(2):
                        sig = jnp.exp(_piece(z, at, h) - drops[at][h])
                        after = _piece(esum, at, h) + runs_e[h][r] - es[at][h]
                        dz = es[at][h] - sig * (e_tot[h][r] - after)
                        dz_row.append(jnp.where(strict, dz, 0.0) if r == masked else dz)
                    dzs.append(dz_row)
                dz = _pair_bf16(dzs)
                dq = _add_rows(dq, _dot(dz, kk), r0)
                dk = _dot_tn(dz, q_rows)
                dv = _dot_tn(_pair_bf16(ab), do_rows)
                dk_s[keys, :] += jnp.where(masks[0], dk[:ATT_BLOCK], dk[ATT_BLOCK:])
                dv_s[keys, :] += jnp.where(masks[0], dv[:ATT_BLOCK], dv[ATT_BLOCK:])
                slabs_l =[_add_rows(slabs_l[h], _row_totals(csum, r0, h), r0) for h in range(2)]
                slabs_e = [_add_rows(slabs_e[h], _row_totals(esum, r0, h), r0) for h in range(2)]
            run_l_s[0], run_l_s[1] = slabs_l
            run_e_s[0], run_e_s[1] = slabs_e
            dq_s[...] += dq

        span(i * SB_CHUNK, SB_SUB, True)
        _sweep_left(span, i, run_l_s)
        dq_ref[...] = (dq_s[...] * QK_SCALE).astype(BF16)

        @pl.when(i == nq - 1)
        def _():
            dk_ref[...] = dk_s[...].astype(BF16)
            dv_ref[...] = dv_s[...].astype(BF16)

    tile = lambda p, i: (i, p)
    state = pltpu.VMEM((2, SB_CHUNK, 128), F32)
    sums = pltpu.VMEM((s, 128), F32)
    return _pallas(
        body, guest, (qkv, qkv, qkv, o_sb, do_sb), name="sb_bwd", grid=(N_PAIRS, nq),
        in_specs=[pl.BlockSpec((SB_CHUNK, 128), tile),
                  pl.BlockSpec((s, 128), lambda p, i: (0, N_PAIRS + p)),
                  pl.BlockSpec((s, 128), lambda p, i: (0, 2 * N_PAIRS + p)),
                  pl.BlockSpec((SB_CHUNK, 128), tile),
                  pl.BlockSpec((SB_CHUNK, 128), tile)],
        out_specs=[pl.BlockSpec((SB_CHUNK, 128), tile),
                   pl.BlockSpec((s, 128), lambda p, i: (0, p)),
                   pl.BlockSpec((s, 128), lambda p, i: (0, p))],
        out_shape=[jax.ShapeDtypeStruct((s, ATT_WIDTH), BF16)] * 3,
        scratch_shapes=[state, state, state, pltpu.VMEM((SB_CHUNK, 128), F32), sums, sums],
        compiler_params=_params(("arbitrary", "arbitrary")),
    )


def ca_bias_tiles(rel_bias):
    heads = rel_bias.shape[0]
    wide = 2 * ATT_BLOCK

    def body(rel_ref, out_ref):
        bins = lax.broadcasted_iota(jnp.int32, (REL_PAD, wide), 0)
        pos = lax.broadcasted_iota(jnp.int32, (REL_PAD, wide), 1)
        r = lax.broadcasted_iota(jnp.int32, (ATT_BLOCK, ATT_BLOCK), 0)
        c = lax.broadcasted_iota(jnp.int32, (ATT_BLOCK, ATT_BLOCK), 1)
        for j in range(CA_TILES):
            hot = (jnp.clip(ATT_BLOCK * (j + 1) - 1 - pos, -REL_CLIP, REL_CLIP) + REL_CLIP == bins).astype(F32)
            lines = jnp.dot(rel_ref[...], hot, preferred_element_type=F32, precision=lax.Precision.HIGHEST)
            back = 2 * j + (r >> 6) - (c >> 6)
            ok = jnp.logical_and(back >= 0, back <= CA_PREV_CHUNKS)
            for h in range(heads):
                tile = jnp.broadcast_to(lines[h:h + 1, :], (ATT_BLOCK, wide))
                tile = pltpu.roll(tile, wide - (ATT_BLOCK - 1), 1, stride=1, stride_axis=0)
                out_ref[h, j] = jnp.where(ok, tile[:, :ATT_BLOCK], NEG_BIG)

    padded = jnp.pad(rel_bias, ((0, 0), (0, REL_PAD - rel_bias.shape[1])))
    return pl.pallas_call(
        body, name="ca_bias_tiles",
        out_shape=jax.ShapeDtypeStruct((heads, CA_TILES, ATT_BLOCK, ATT_BLOCK), F32),
        compiler_params=_params(),
    )(padded)


CA_BLOCKS = CA_ROWS // ATT_BLOCK
CA_KEY_BLOCKS = CA_BLOCKS + CA_TILES - 1


def _ca_users(t):
    return [r for r in range(CA_BLOCKS) if 0 <= r + CA_TILES - 1 - t < CA_TILES]


def _ca_rows(v, t, r):
    at = _ca_users(t).index(r)
    return v[ATT_BLOCK * at:ATT_BLOCK * (at + 1)]


def _ca_window(step, k_ref, v_ref, masks):
    kks, vvs, inside, rows = [], [], [], []
    for t in range(CA_KEY_BLOCKS):
        block = step * CA_BLOCKS - (CA_TILES - 1) + t
        keys = pl.ds(pl.multiple_of(jnp.maximum(block, 0) * ATT_BLOCK, ATT_BLOCK), ATT_BLOCK)
        kks.append(_by_head(k_ref[keys, :], masks))
        vvs.append(_by_head(v_ref[keys, :], masks))
        inside.append(block >= 0)
        rows.append(keys)
    return kks, vvs, inside, rows


def _ca_probs(q, kks, inside, bias_ref):
    scores = [[[None] * CA_TILES for _ in range(2)] for _ in range(CA_BLOCKS)]
    for t in range(CA_KEY_BLOCKS):
        users = _ca_users(t)
        z = _dot_nt(q[ATT_BLOCK * users[0]:ATT_BLOCK * (users[-1] + 1)], kks[t])
        for r in users:
            j = r + CA_TILES - 1 - t
            for h in range(2):
                zz = _ca_rows(z, t, r)[:, ATT_BLOCK * h:ATT_BLOCK * (h + 1)] + bias_ref[h, j]
                scores[r][h][j] = jnp.where(inside[t], zz, NEG_BIG)
    probs = [[None, None] for _ in range(CA_BLOCKS)]
    for r in range(CA_BLOCKS):
        for h in range(2):
            m = jnp.max(functools.reduce(jnp.maximum, scores[r][h]), axis=1, keepdims=True)
            ex = [jnp.exp(z - m) for z in scores[r][h]]
            inv = 1.0 / jnp.sum(functools.reduce(jnp.add, ex), axis=1, keepdims=True)
            probs[r][h] = [e * inv for e in ex]
    return probs


def _ca_stack(tiles, t):
    return jnp.concatenate(
        [jnp.concatenate([tiles[r][h][r + CA_TILES - 1 - t].astype(BF16) for h in range(2)], axis=1)
         for r in _ca_users(t)], axis=0)


def ca_fwd(qkv, bias_tiles, guest=None):
    s = qkv.shape[0]
    nq = s // CA_ROWS

    def body(q_ref, k_ref, v_ref, bias_ref, o_ref):
        masks = _head_masks()
        kks, vvs, inside, _ = _ca_window(pl.program_id(1), k_ref, v_ref, masks)
        probs = _ca_probs(q_ref[...] * QK_SCALE, kks, inside, bias_ref)
        out = [None] * CA_BLOCKS
        for t in range(CA_KEY_BLOCKS):
            part = _dot(_ca_stack(probs, t), vvs[t])
            for r in _ca_users(t):
                piece = _ca_rows(part, t, r)
                out[r] = piece if out[r] is None else out[r] + piece
        o_ref[...] = jnp.concatenate(out, axis=0).astype(BF16)

    return _pallas(
        body, guest, (qkv, qkv, qkv, bias_tiles), name="ca_fwd", grid=(N_PAIRS, nq),
        in_specs=[pl.BlockSpec((CA_ROWS, 128), lambda p, i: (i, 3 * N_PAIRS + p)),
                  pl.BlockSpec((s, 128), lambda p, i: (0, 4 * N_PAIRS + p)),
                  pl.BlockSpec((s, 128), lambda p, i: (0, 5 * N_PAIRS + p)),
                  pl.BlockSpec((2, CA_TILES, ATT_BLOCK, ATT_BLOCK), lambda p, i: (p, 0, 0, 0))],
        out_specs=[pl.BlockSpec((CA_ROWS, 128), lambda p, i: (i, p))],
        out_shape=[jax.ShapeDtypeStruct((s, ATT_WIDTH), BF16)],
        compiler_params=_params(("arbitrary", "arbitrary")),
    )


def ca_bwd(qkv, bias_tiles, do_ca, guest=None):
    s = qkv.shape[0]
    nq = s // CA_ROWS

    def body(q_ref, k_ref, v_ref, bias_ref, do_ref, dq_ref, dk_ref, dv_ref, db_ref, dk_s, dv_s):
        big = pl.program_id(1)

        @pl.when(big == 0)
        def _():
            dk_s[...] = jnp.zeros_like(dk_s)
            dv_s[...] = jnp.zeros_like(dv_s)
            db_ref[...] = jnp.zeros_like(db_ref)

        masks = _head_masks()
        kks, vvs, inside, key_rows = _ca_window(big, k_ref, v_ref, masks)
        q = q_ref[...] * QK_SCALE
        do = do_ref[...]
        probs = _ca_probs(q, kks, inside, bias_ref)
        dps = [[[None] * CA_TILES for _ in range(2)] for _ in range(CA_BLOCKS)]
        for t in range(CA_KEY_BLOCKS):
            users = _ca_users(t)
            dp = _dot_nt(do[ATT_BLOCK * users[0]:ATT_BLOCK * (users[-1] + 1)], vvs[t])
            for r in users:
                for h in range(2):
                    dps[r][h][r + CA_TILES - 1 - t] = _ca_rows(dp, t, r)[:, ATT_BLOCK * h:ATT_BLOCK * (h + 1)]
        dss = [[None, None] for _ in range(CA_BLOCKS)]
        for r in range(CA_BLOCKS):
            for h in range(2):
                delta = jnp.sum(functools.reduce(jnp.add, [p * dp for p, dp in zip(probs[r][h], dps[r][h])]),
                                axis=1, keepdims=True)
                dss[r][h] = [p * (dp - delta) for p, dp in zip(probs[r][h], dps[r][h])]
        for h in range(2):
            for j in range(CA_TILES):
                db_ref[h, j] += functools.reduce(jnp.add, [dss[r][h][j] for r in range(CA_BLOCKS)])
        dq = [None] * CA_BLOCKS
        for t in range(CA_KEY_BLOCKS):
            users = _ca_users(t)
            rows = slice(ATT_BLOCK * users[0], ATT_BLOCK * (users[-1] + 1))
            ds = _ca_stack(dss, t)
            part = _dot(ds, kks[t])
            for r in users:
                piece = _ca_rows(part, t, r)
                dq[r] = piece if dq[r] is None else dq[r] + piece
            dk = _dot_tn(ds, q[rows])
            dv = _dot_tn(_ca_stack(probs, t), do[rows])
            dk_s[key_rows[t], :] += jnp.where(masks[0], dk[:ATT_BLOCK], dk[ATT_BLOCK:])
            dv_s[key_rows[t], :] += jnp.where(masks[0], dv[:ATT_BLOCK], dv[ATT_BLOCK:])
        dq_ref[...] = (jnp.concatenate(dq, axis=0) * QK_SCALE).astype(BF16)

        @pl.when(big == nq - 1)
        def _():
            dk_ref[...] = dk_s[...].astype(BF16)
            dv_ref[...] = dv_s[...].astype(BF16)

    return _pallas(
        body, guest, (qkv, qkv, qkv, bias_tiles, do_ca), name="ca_bwd", grid=(N_PAIRS, nq),
        in_specs=[pl.BlockSpec((CA_ROWS, 128), lambda p, i: (i, 3 * N_PAIRS + p)),
                  pl.BlockSpec((s, 128), lambda p, i: (0, 4 * N_PAIRS + p)),
                  pl.BlockSpec((s, 128), lambda p, i: (0, 5 * N_PAIRS + p)),
                  pl.BlockSpec((2, CA_TILES, ATT_BLOCK, ATT_BLOCK), lambda p, i: (p, 0, 0, 0)),
                  pl.BlockSpec((CA_ROWS, 128), lambda p, i: (i, p))],
        out_specs=[pl.BlockSpec((CA_ROWS, 128), lambda p, i: (i, p)),
                   pl.BlockSpec((s, 128), lambda p, i: (0, p)),
                   pl.BlockSpec((s, 128), lambda p, i: (0, p)),
                   pl.BlockSpec((2, CA_TILES, ATT_BLOCK, ATT_BLOCK), lambda p, i: (p, 0, 0, 0))],
        out_shape=[jax.ShapeDtypeStruct((s, ATT_WIDTH), BF16)] * 3
                  + [jax.ShapeDtypeStruct((2 * N_PAIRS, CA_TILES, ATT_BLOCK, ATT_BLOCK), F32)],
        scratch_shapes=[pltpu.VMEM((s, 128), F32), pltpu.VMEM((s, 128), F32)],
        compiler_params=_params(("arbitrary", "arbitrary")),
    )


def rel_bias_grad(db_tiles):
    heads = db_tiles.shape[0]

    def body(db_ref, out_ref):
        row = lax.broadcasted_iota(jnp.int32, (ATT_BLOCK, ATT_BLOCK), 0)
        col = lax.broadcasted_iota(jnp.int32, (ATT_BLOCK, ATT_BLOCK), 1)
        flip = (row + col == ATT_BLOCK - 1).astype(F32)
        wrapped = col < row
        cp = lax.broadcasted_iota(jnp.int32, (ATT_BLOCK, REL_PAD), 0)
        bins = lax.broadcasted_iota(jnp.int32, (ATT_BLOCK, REL_PAD), 1)
        total = jnp.zeros((heads, REL_PAD), F32)
        for j in range(CA_TILES):
            sums_neg, sums_pos = [], []
            for h in range(heads):
                tile = jnp.dot(db_ref[h, j], flip, preferred_element_type=F32, precision=lax.Precision.HIGHEST)
                tile = pltpu.roll(tile, 0, 1, stride=1, stride_axis=0)
                sums_neg.append(jnp.sum(jnp.where(wrapped, 0.0, tile), axis=0, keepdims=True))
                sums_pos.append(jnp.sum(jnp.where(wrapped, tile, 0.0), axis=0, keepdims=True))
            neg = jnp.concatenate(sums_neg, axis=0)
            pos = jnp.concatenate(sums_pos, axis=0)
            hot_neg = (jnp.clip(ATT_BLOCK * j + cp - (ATT_BLOCK - 1), -REL_CLIP, REL_CLIP) + REL_CLIP == bins)
            hot_pos = (jnp.clip(ATT_BLOCK * j + cp + 1, -REL_CLIP, REL_CLIP) + REL_CLIP == bins)
            total = total + jnp.dot(neg, hot_neg.astype(F32), preferred_element_type=F32,
                                    precision=lax.Precision.HIGHEST)
            total = total + jnp.dot(pos, hot_pos.astype(F32), preferred_element_type=F32,
                                    precision=lax.Precision.HIGHEST)
        out_ref[...] = total

    return pl.pallas_call(
        body, name="rel_bias_grad",
        out_shape=jax.ShapeDtypeStruct((heads, REL_PAD), F32),
        compiler_params=_params(),
    )(db_tiles)


def _assemble_cols(dst_ref, src_ref):
    w = src_ref.shape[2]
    for j in range(N_DEV):
        dst_ref[:, w * j:w * (j + 1)] = src_ref[j]


def _merge(o_sb, o_ca, lg, bg, wsb, wca):
    y_sb = _dot(o_sb.astype(BF16), wsb)
    y_ca = _dot(o_ca.astype(BF16), wca)
    gates = jax.nn.sigmoid(lg + bg)
    g_sb, g_ca = gates[:, :D_MODEL], gates[:, D_MODEL:]
    return y_sb, y_ca, g_sb, g_ca, g_sb * y_sb + g_ca * y_ca


def mix_fwd(x, o_sb, o_ca, lg, b_gate, wg_sb, wg_ca, w_out, tm=256):
    s = x.shape[0]

    def body(x_ref, osb_ref, oca_ref, lg_ref, bg_ref, wsb_ref, wca_ref, wout_ref, u_ref, wsb_s, wca_s):
        @pl.when(pl.program_id(0) == 0)
        def _():
            _assemble_cols(wsb_s, wsb_ref)
            _assemble_cols(wca_s, wca_ref)

        merged = _merge(osb_ref[...], oca_ref[...], lg_ref[...], bg_ref[...], wsb_s[...], wca_s[...])[4]
        u_ref[...] = ALPHA * x_ref[...] + _dot(merged.astype(BF16), wout_ref[...])

    rows = lambda w: pl.BlockSpec((tm, w), lambda i: (i, 0))
    whole = lambda shape: pl.BlockSpec(shape, lambda i: (0,) * len(shape))
    return pl.pallas_call(
        body, name="mix_fwd", grid=(s // tm,),
        in_specs=[rows(D_MODEL), rows(ATT_WIDTH), rows(ATT_WIDTH), rows(GATE_COLS), whole((1, GATE_COLS)),
                  whole((N_DEV, ATT_WIDTH, PROJ_SHARD)), whole((N_DEV, ATT_WIDTH, PROJ_SHARD)),
                  whole((D_MODEL, D_MODEL))],
        out_specs=rows(D_MODEL),
        out_shape=jax.ShapeDtypeStruct((s, D_MODEL), F32),
        scratch_shapes=[pltpu.VMEM((ATT_WIDTH, D_MODEL), BF16), pltpu.VMEM((ATT_WIDTH, D_MODEL), BF16)],
        compiler_params=_params(("arbitrary",)),
    )(x, o_sb, o_ca, lg, _row(b_gate), wg_sb, wg_ca, w_out)


def mix_bwd(du1, o_sb, o_ca, lg, b_gate, wg_sb, wg_ca, w_out, tm=256, guest=None):
    s = du1.shape[0]
    last = s // tm - 1

    def body(du_ref, osb_ref, oca_ref, lg_ref, bg_ref, wsb_ref, wca_ref, wout_ref,
             dlg_ref, dosb_ref, doca_ref, dbg_ref, gout_ref, gsb_ref, gca_ref, wsb_s, wca_s, gsb_s, gca_s):
        step = pl.program_id(0)

        @pl.when(step == 0)
        def _():
            _assemble_cols(wsb_s, wsb_ref)
            _assemble_cols(wca_s, wca_ref)
            for ref in (dbg_ref, gout_ref, gsb_s, gca_s):
                ref[...] = jnp.zeros_like(ref)

        o_sb, o_ca, du = osb_ref[...].astype(BF16), oca_ref[...].astype(BF16), du_ref[...].astype(BF16)
        y_sb, y_ca, g_sb, g_ca, merged = _merge(o_sb, o_ca, lg_ref[...], bg_ref[...], wsb_s[...], wca_s[...])
        dm = _dot_nt(du, wout_ref[...])
        dl_sb = dm * y_sb * g_sb * (1.0 - g_sb)
        dl_ca = dm * y_ca * g_ca * (1.0 - g_ca)
        dlg_ref[:, :D_MODEL] = dl_sb.astype(BF16)
        dlg_ref[:, D_MODEL:] = dl_ca.astype(BF16)
        dbg_ref[:, :D_MODEL] += jnp.sum(dl_sb, axis=0, keepdims=True)
        dbg_ref[:, D_MODEL:] += jnp.sum(dl_ca, axis=0, keepdims=True)
        dy_sb = (dm * g_sb).astype(BF16)
        dy_ca = (dm * g_ca).astype(BF16)
        dosb_ref[...] = _dot_nt(dy_sb, wsb_s[...]).astype(BF16)
        doca_ref[...] = _dot_nt(dy_ca, wca_s[...]).astype(BF16)
        gout_ref[...] += _dot_tn(merged.astype(BF16), du)
        gsb_s[...] += _dot_tn(o_sb, dy_sb)
        gca_s[...] += _dot_tn(o_ca, dy_ca)

        @pl.when(step == last)
        def _():
            for j in range(N_DEV):
                cols = slice(PROJ_SHARD * j, PROJ_SHARD * (j + 1))
                gsb_ref[j] = gsb_s[:, cols]
                gca_ref[j] = gca_s[:, cols]

    rows = lambda w: pl.BlockSpec((tm, w), lambda i: (i, 0))
    whole = lambda shape: pl.BlockSpec(shape, lambda i: (0,) * len(shape))
    stack = (N_DEV, ATT_WIDTH, PROJ_SHARD)
    return _pallas(
        body, guest, (du1, o_sb, o_ca, lg, _row(b_gate), wg_sb, wg_ca, w_out), name="mix_bwd", grid=(s // tm,),
        in_specs=[rows(D_MODEL), rows(ATT_WIDTH), rows(ATT_WIDTH), rows(GATE_COLS), whole((1, GATE_COLS)),
                  whole(stack), whole(stack), whole((D_MODEL, D_MODEL))],
        out_specs=[rows(GATE_COLS), rows(ATT_WIDTH), rows(ATT_WIDTH), whole((1, GATE_COLS)),
                   whole((D_MODEL, D_MODEL)), whole(stack), whole(stack)],
        out_shape=[jax.ShapeDtypeStruct((s, GATE_COLS), BF16),
                   jax.ShapeDtypeStruct((s, ATT_WIDTH), BF16), jax.ShapeDtypeStruct((s, ATT_WIDTH), BF16),
                   jax.ShapeDtypeStruct((1, GATE_COLS), F32), jax.ShapeDtypeStruct((D_MODEL, D_MODEL), F32),
                   jax.ShapeDtypeStruct(stack, F32), jax.ShapeDtypeStruct(stack, F32)],
        scratch_shapes=[pltpu.VMEM((ATT_WIDTH, D_MODEL), BF16), pltpu.VMEM((ATT_WIDTH, D_MODEL), BF16),
                        pltpu.VMEM((ATT_WIDTH, D_MODEL), F32), pltpu.VMEM((ATT_WIDTH, D_MODEL), F32)],
        compiler_params=_params(("arbitrary",)),
    )


def _ln_stats(u):
    mu = jnp.mean(u, axis=1, keepdims=True)
    cen = u - mu
    var = jnp.mean(cen * cen, axis=1, keepdims=True)
    rstd = lax.rsqrt(var + LN_EPS)
    return cen * rstd, rstd


def _ln_bwd(dy, xhat, rstd, gain):
    dyg = dy * gain
    m1 = jnp.mean(dyg, axis=1, keepdims=True)
    m2 = jnp.mean(dyg * xhat, axis=1, keepdims=True)
    return rstd * (dyg - m1 - xhat * m2)


def _load_mlp_weights(win_hbm, wout_hbm, win_s, wout_s, sems):
    copies = [pltpu.make_async_copy(win_hbm.at[k], win_s.at[:, pl.ds(FF_SHARD * k, FF_SHARD)], sems.at[k])
              for k in range(N_DEV)]
    copies.append(pltpu.make_async_copy(wout_hbm, wout_s, sems.at[N_DEV]))
    for cp in copies:
        cp.start()
    return copies


MLP_WEIGHT_SCRATCH = [pltpu.VMEM((D_MODEL, D_FF), BF16), pltpu.VMEM((D_FF, D_MODEL), BF16),
                      pltpu.SemaphoreType.DMA((N_DEV + 1,))]


def mlp_block(u1, target, wg_in, w_out, ln1_g, ln1_b, ln2_g, ln2_b, tm=256):
    s = u1.shape[0]

    def body(u_ref, t_ref, win_hbm, wout_hbm, g1_ref, b1_ref, g2_ref, b2_ref,
             a_ref, x1b_ref, dhm_ref, du2b_ref, du1_ref, loss_ref, dg2_ref, db2_ref, dg1_ref, db1_ref,
             win_s, wout_s, sems, relu_s):
        @pl.when(pl.program_id(0) == 0)
        def _():
            copies = _load_mlp_weights(win_hbm, wout_hbm, win_s, wout_s, sems)
            for ref in (loss_ref, dg2_ref, db2_ref, dg1_ref, db1_ref):
                ref[...] = jnp.zeros_like(ref)
            for cp in copies:
                cp.wait()

        xhat1, rstd1 = _ln_stats(u_ref[...])
        x1 = xhat1 * g1_ref[...] + b1_ref[...]
        x1b = x1.astype(BF16)
        x1b_ref[...] = x1b
        for k in range(N_DEV):
            cols = slice(FF_SHARD * k, FF_SHARD * (k + 1))
            r = jnp.maximum(_dot(x1b, win_s[:, cols]), 0.0)
            relu_s[:, cols] = r
            a_ref[:, cols] = (r * r).astype(BF16)
        xhat2, rstd2 = _ln_stats(ALPHA * x1 + _dot(a_ref[...], wout_s[...]))
        diff = xhat2 * g2_ref[...] + b2_ref[...] - t_ref[...]
        per_token = jnp.mean(diff * diff, axis=1, keepdims=True)
        loss_ref[...] += 0.5 * jnp.sum(per_token, axis=0, keepdims=True)
        dy = diff * (1.0 / D_MODEL)
        dg2_ref[...] += jnp.sum(dy * xhat2, axis=0, keepdims=True)
        db2_ref[...] += jnp.sum(dy, axis=0, keepdims=True)
        du2 = _ln_bwd(dy, xhat2, rstd2, g2_ref[...])
        du2b = du2.astype(BF16)
        du2b_ref[...] = du2b
        for k in range(N_DEV):
            cols = slice(FF_SHARD * k, FF_SHARD * (k + 1))
            da = _dot_nt(du2b, wout_s[cols, :])
            dhm_ref[:, cols] = (2.0 * relu_s[:, cols] * da).astype(BF16)
        dx1 = ALPHA * du2 + _dot_nt(dhm_ref[...], win_s[...])
        dg1_ref[...] += jnp.sum(dx1 * xhat1, axis=0, keepdims=True)
        db1_ref[...] += jnp.sum(dx1, axis=0, keepdims=True)
        du1_ref[...] = _ln_bwd(dx1, xhat1, rstd1, g1_ref[...])

    rows = pl.BlockSpec((tm, D_MODEL), lambda i: (i, 0))
    vec = pl.BlockSpec((1, D_MODEL), lambda i: (0, 0))
    ff = pl.BlockSpec((tm, D_FF), lambda i: (i, 0))
    wide = jax.ShapeDtypeStruct((s, D_FF), BF16)
    narrow = jax.ShapeDtypeStruct((s, D_MODEL), BF16)
    stat = jax.ShapeDtypeStruct((1, D_MODEL), F32)
    return pl.pallas_call(
        body, name="mlp", grid=(s // tm,),
        in_specs=[rows, rows, HBM_SPEC, HBM_SPEC, vec, vec, vec, vec],
        out_specs=[ff, rows, ff, rows, rows, pl.BlockSpec((1, 1), lambda i: (0, 0)), vec, vec, vec, vec],
        out_shape=[wide, narrow, wide, narrow, jax.ShapeDtypeStruct((s, D_MODEL), F32),
                   jax.ShapeDtypeStruct((1, 1), F32), stat, stat, stat, stat],
        scratch_shapes=MLP_WEIGHT_SCRATCH + [pltpu.VMEM((tm, D_FF), F32)],
        compiler_params=_params(("arbitrary",)),
    )(u1, target, wg_in, w_out, _row(ln1_g), _row(ln1_b), _row(ln2_g), _row(ln2_b))


def dh_slab(pieces, tm=512):
    s = pieces[0].shape[0]
    widths = [p.shape[1] for p in pieces]
    assert sum(widths) == IN_COLS
    n = len(pieces)

    def body(*refs):
        dh_ref = refs[n]
        off = 0
        for ref, w in zip(refs[:n], widths):
            dh_ref[:, off:off + w] = ref[...].astype(BF16)
            off += w

    rows = lambda w: pl.BlockSpec((tm, w), lambda i: (i, 0))
    return pl.pallas_call(
        body, name="dh_slab", grid=(s // tm,),
        in_specs=[rows(w) for w in widths], out_specs=rows(IN_COLS),
        out_shape=jax.ShapeDtypeStruct((s, IN_COLS), BF16),
        compiler_params=_params(("arbitrary",)),
    )(*pieces)


def in_bwd(du1, dh, wg_in, tm=512, guest=None):
    s = du1.shape[0]

    def body(du_ref, dh_ref, w_hbm, gx_ref, pairs_s, sems):
        @pl.when(pl.program_id(0) == 0)
        def _():
            _load_in_pairs(w_hbm, pairs_s, sems)

        acc = ALPHA * du_ref[...]
        for j in range(N_DEV // 2):
            acc = acc + _dot_nt(dh_ref[:, IN_PAIR * j:IN_PAIR * (j + 1)], pairs_s[j])
        gx_ref[...] = acc

    rows = lambda w: pl.BlockSpec((tm, w), lambda i: (i, 0))
    return _pallas(
        body, guest, (du1, dh, wg_in), name="in_bwd", grid=(s // tm,), scratch_shapes=IN_PAIR_SCRATCH,
        in_specs=[rows(D_MODEL), rows(IN_COLS), HBM_SPEC],
        out_specs=[rows(D_MODEL)],
        out_shape=[jax.ShapeDtypeStruct((s, D_MODEL), F32)],
        compiler_params=_params(("arbitrary",)),
    )


def wgrad(name, a, b, tm, tn, shard_cols=None, tk=2048):
    kk, m = a.shape
    n = b.shape[1]
    while kk % tk:
        tk //= 2
    nk = kk // tk
    w = shard_cols

    def body(a_ref, b_ref, out_ref, acc_s):
        k = pl.program_id(2)

        @pl.when(k == 0)
        def _():
            acc_s[...] = jnp.zeros_like(acc_s)

        acc_s[...] += _dot_tn(a_ref[...].astype(BF16), b_ref[...].astype(BF16))

        @pl.when(k == nk - 1)
        def _():
            if w is None:
                out_ref[...] = acc_s[...]
            else:
                for j in range(tn // w):
                    out_ref[j] = acc_s[:, w * j:w * (j + 1)]

    if w is None:
        out_spec = pl.BlockSpec((tm, tn), lambda i, j, k: (i, j))
        out_shape = jax.ShapeDtypeStruct((m, n), F32)
    else:
        out_spec = pl.BlockSpec((tn // w, tm, w), lambda i, j, k: (j, i, 0))
        out_shape = jax.ShapeDtypeStruct((n // w, m, w), F32)
    return pl.pallas_call(
        body, name=name, grid=(m // tm, n // tn, nk),
        in_specs=[pl.BlockSpec((tk, tm), lambda i, j, k: (k, i)),
                  pl.BlockSpec((tk, tn), lambda i, j, k: (k, j))],
        out_specs=out_spec, out_shape=out_shape,
        scratch_shapes=[pltpu.VMEM((tm, tn), F32)],
        compiler_params=_params(("arbitrary", "arbitrary", "arbitrary")),
    )(a, b)


def adamw(name, w, m, v, parts, tm=None, guest=None):
    rows, cols = w.shape
    tm = rows if tm is None else min(tm, rows)
    n = len(parts)
    c1 = 1.0 - ADAM_B1 ** ADAM_STEP
    c2 = 1.0 - ADAM_B2 ** ADAM_STEP

    def body(*refs):
        w_ref, m_ref, v_ref = refs[:3]
        part_refs = refs[3:3 + n]
        g_ref, d_ref, nm_ref, nv_ref = refs[3 + n:]
        g = None
        for ref, (_, index) in zip(part_refs, parts):
            term = (ref[...] if index is None else ref[0]).astype(F32)
            g = term if g is None else g + term
        new_m = ADAM_B1 * m_ref[...] + (1.0 - ADAM_B1) * g
        new_v = ADAM_B2 * v_ref[...] + (1.0 - ADAM_B2) * (g * g)
        m_hat = new_m / c1
        v_hat = new_v / c2
        g_ref[...] = g
        d_ref[...] = -ADAM_LR * (m_hat / (jnp.sqrt(v_hat) + ADAM_EPS) + ADAM_WD * w_ref[...])
        nm_ref[...] = new_m
        nv_ref[...] = new_v

    spec = pl.BlockSpec((tm, cols), lambda i: (i, 0))
    shape = jax.ShapeDtypeStruct((rows, cols), F32)

    def part_spec(index):
        if index is None:
            return spec
        return pl.BlockSpec((1, tm, cols), lambda i: (index, i, 0))

    return _pallas(
        body, guest, (w, m, v, *[array for array, _ in parts]), name=name, grid=(rows // tm,),
        in_specs=[spec] * 3 + [part_spec(index) for _, index in parts],
        out_specs=[spec] * 4, out_shape=[shape] * 4,
        compiler_params=_params(("arbitrary",)),
    )


def _place():
    x, y, c = lax.axis_index("x"), lax.axis_index("y"), lax.axis_index("c")
    other_chips = [(1 - x, y), (x, 1 - y), (1 - x, 1 - y)]
    return x, y, c, other_chips


def place_shards(arrays):
    n = len(arrays)

    def body(*refs):
        ins, outs, stage, sems = refs[:n], refs[n:2 * n], refs[2 * n:3 * n], refs[3 * n]
        x, y, c, _ = _place()
        copies = []
        for a in range(n):
            stage[a][...] = ins[a][...].astype(BF16)
            copies.append(pltpu.make_async_copy(stage[a], outs[a].at[4 * x + 2 * y + c], sems.at[a]))
            copies[-1].start()
        for cp in copies:
            cp.wait()

    return pl.pallas_call(
        body, name="place_shards",
        in_specs=[pl.BlockSpec(memory_space=pltpu.VMEM)] * n, out_specs=[HBM_SPEC] * n,
        out_shape=[jax.ShapeDtypeStruct((N_DEV,) + a.shape, BF16) for a in arrays],
        scratch_shapes=[pltpu.VMEM(a.shape, BF16) for a in arrays] + [pltpu.SemaphoreType.DMA((n,))],
        compiler_params=_params(),
    )(*arrays)


def all_gather(name, arrays):
    n = len(arrays)

    def body(*refs):
        ins, outs = refs[:n], refs[n:2 * n]
        send_sems, recv_sems, local_sems = refs[2 * n:]
        x, y, c, chips = _place()
        me = 4 * x + 2 * y + c
        sibling = (x, y, 1 - c)

        def copy(a, k, block, to, src=None):
            return pltpu.make_async_remote_copy(
                src_ref=outs[a].at[block] if src is None else src, dst_ref=outs[a].at[block],
                send_sem=send_sems.at[a, k], recv_sem=recv_sems.at[a, k],
                device_id=to, device_id_type=MESH)

        started = []
        local = [pltpu.make_async_copy(ins[a], outs[a].at[me], local_sems.at[a]) for a in range(n)]
        for a in range(n):
            local[a].start()
            sends = [copy(a, 0, me, sibling, src=ins[a])]
            sends += [copy(a, 1 + j, me, (cx, cy, c), src=ins[a]) for j, (cx, cy) in enumerate(chips)]
            for cp in sends:
                cp.start()
            started += sends
        for a in range(n):
            for j, (cx, cy) in enumerate(chips):
                block = 4 * cx + 2 * cy + c
                copy(a, 1 + j, block, sibling).wait_recv()
                passed = copy(a, 4 + j, block, sibling)
                passed.start()
                started.append(passed)
        for a in range(n):
            copy(a, 0, 4 * x + 2 * y + (1 - c), sibling).wait_recv()
            for j, (cx, cy) in enumerate(chips):
                copy(a, 4 + j, 4 * cx + 2 * cy + (1 - c), sibling).wait_recv()
        for cp in started:
            cp.wait_send()
        for cp in local:
            cp.wait()

    return pl.pallas_call(
        body, name=name,
        in_specs=[pl.BlockSpec(memory_space=pltpu.VMEM)] * n, out_specs=[HBM_SPEC] * n,
        out_shape=[jax.ShapeDtypeStruct((N_DEV,) + a.shape, a.dtype) for a in arrays],
        scratch_shapes=[pltpu.SemaphoreType.DMA((n, 7)), pltpu.SemaphoreType.DMA((n, 7)),
                        pltpu.SemaphoreType.DMA((n,))],
    )(*arrays)


def _gather_first_plan(n):
    def plan(zones):
        x, y, c, chips = _place()
        me = 4 * x + 2 * y + c
        copies = []
        for zone in zones[:n]:
            copies.append((zone.at[me], zone.at[me], (x, y, 1 - c), zone.at[4 * x + 2 * y + (1 - c)]))
            copies += [(zone.at[me], zone.at[me], (cx, cy, c), zone.at[4 * cx + 2 * cy + c]) for cx, cy in chips]
        return copies
    return plan


def _gather_second_plan(n):
    def plan(zones):
        x, y, c, chips = _place()
        return [(zone.at[4 * cx + 2 * cy + c], zone.at[4 * cx + 2 * cy + c], (x, y, 1 - c),
                 zone.at[4 * cx + 2 * cy + (1 - c)]) for zone in zones[:n] for cx, cy in chips]
    return plan


def _both_plans(first, n_first, second):
    return lambda bufs: first(bufs[:n_first]) + second(bufs[n_first:])


def _sibling_plan(n):
    def plan(bufs):
        x, y, c, _ = _place()
        return [(bufs[a].at[:, 1 - c], bufs[n + a], (x, y, 1 - c), bufs[n + a]) for a in range(n)]
    return plan


def _chips_plan(n):
    def plan(bufs):
        _, _, c, chips = _place()
        return [(bufs[a].at[2 * cx + cy], bufs[n + a].at[j], (cx, cy, c), bufs[n + a].at[j])
                for a in range(n) for j, (cx, cy) in enumerate(chips)]
    return plan


def chip_sum(name, grad, landed, place, tr=256):
    _, _, rows, cols = grad.shape
    tr = min(tr, rows)

    def body(place_ref, g_ref, l_ref, own_ref, out_ref):
        total = g_ref[0, 0] + l_ref[0]
        out_ref[0] = total.astype(BF16)

        @pl.when(pl.program_id(1) == place_ref[1])
        def _():
            own_ref[...] = total

    return pl.pallas_call(
        body, name=name,
        grid_spec=pltpu.PrefetchScalarGridSpec(
            num_scalar_prefetch=1, grid=(rows // tr, 4),
            in_specs=[pl.BlockSpec((1, 1, tr, cols), lambda i, ch, pos: (ch, pos[0], i, 0)),
                      pl.BlockSpec((1, tr, cols), lambda i, ch, pos: (ch, i, 0))],
            out_specs=[pl.BlockSpec((tr, cols), lambda i, ch, pos: (i, 0)),
                       pl.BlockSpec((1, tr, cols), lambda i, ch, pos: (ch, i, 0))]),
        out_shape=[jax.ShapeDtypeStruct((rows, cols), F32), jax.ShapeDtypeStruct((4, rows, cols), BF16)],
        compiler_params=_params(("arbitrary", "arbitrary")),
    )(place, grad, landed)


BIG =("w_in", "w_sb_proj", "w_ca_proj", "w_out", "w_mlp_in", "w_mlp_out")
SMALL = ("b_gate", "rel_bias", "ln1_g", "ln1_b", "ln2_g", "ln2_b")
NAMES = ("w_in", "b_gate", "w_sb_proj", "w_ca_proj", "rel_bias", "w_out", "ln1_g", "ln1_b",
         "w_mlp_in", "w_mlp_out", "ln2_g", "ln2_b")
REL_COLS = 2 * REL_CLIP + 1


def _pack_small(t, scalar=None):
    rel = t["rel_bias"]
    if rel.shape[1] != REL_PAD:
        rel = jnp.pad(rel, ((0, 0), (0, REL_PAD - rel.shape[1])))
    last = jnp.zeros((128,), F32) if scalar is None else jnp.pad(scalar.reshape(1), (0, 127))
    flat = [t["b_gate"].reshape(-1), rel.reshape(-1)] + [t[n].reshape(-1) for n in SMALL[2:]] + [last]
    return jnp.concatenate(flat).reshape(-1, 128)


def _unpack_small(p):
    flat = p.reshape(-1)
    out, off = {}, 0
    for n, size in (("b_gate", GATE_COLS), ("rel_bias", 8 * REL_PAD)) + tuple((n, D_MODEL) for n in SMALL[2:]):
        out[n] = flat[off:off + size]
        off += size
    out["rel_bias"] = out["rel_bias"].reshape(8, REL_PAD)[:, :REL_COLS]
    return out


def kernel(x, w_in, b_gate, w_sb_proj, w_ca_proj, rel_bias, w_out, ln1_g, ln1_b, w_mlp_in, w_mlp_out, ln2_g, ln2_b, loss_target, m_w_in, m_b_gate, m_w_sb_proj, m_w_ca_proj, m_rel_bias, m_w_out, m_ln1_g, m_ln1_b, m_w_mlp_in, m_w_mlp_out, m_ln2_g, m_ln2_b, v_w_in, v_b_gate, v_w_sb_proj, v_w_ca_proj, v_rel_bias, v_w_out, v_ln1_g, v_ln1_b, v_w_mlp_in, v_w_mlp_out, v_ln2_g, v_ln2_b):
    w = dict(w_in=w_in, b_gate=b_gate, w_sb_proj=w_sb_proj, w_ca_proj=w_ca_proj, rel_bias=rel_bias, w_out=w_out,
             ln1_g=ln1_g, ln1_b=ln1_b, w_mlp_in=w_mlp_in, w_mlp_out=w_mlp_out, ln2_g=ln2_g, ln2_b=ln2_b)
    m = dict(w_in=m_w_in, b_gate=m_b_gate, w_sb_proj=m_w_sb_proj, w_ca_proj=m_w_ca_proj, rel_bias=m_rel_bias,
             w_out=m_w_out, ln1_g=m_ln1_g, ln1_b=m_ln1_b, w_mlp_in=m_w_mlp_in, w_mlp_out=m_w_mlp_out,
             ln2_g=m_ln2_g, ln2_b=m_ln2_b)
    v = dict(w_in=v_w_in, b_gate=v_b_gate, w_sb_proj=v_w_sb_proj, w_ca_proj=v_w_ca_proj, rel_bias=v_rel_bias,
             w_out=v_w_out, ln1_g=v_ln1_g, ln1_b=v_ln1_b, w_mlp_in=v_w_mlp_in, w_mlp_out=v_w_mlp_out,
             ln2_g=v_ln2_g, ln2_b=v_ln2_b)

    x, target = x[0], loss_target[0]
    c = lax.axis_index("c")
    place = jnp.stack([c, 2 * lax.axis_index("x") + lax.axis_index("y")]).astype(jnp.int32)
    grads, deltas, new_m, new_v = {}, {}, {}, {}

    def by_owner(n, g):
        return g.reshape((4, 2) + w[n].shape)

    def finish(n, own, arrived, guest=None):
        out = adamw("adamw_" + n, w[n], m[n], v[n], [(own, None), (arrived, 0), (arrived, 1), (arrived, 2)],
                    tm=256, guest=guest)
        out, bufs = (out, None) if guest is None else out
        grads[n], deltas[n], new_m[n], new_v[n] = out
        return bufs

    def siblings(names, g):
        bufs = [by_owner(n, g[n]) for n in names] + [lax.empty((4,) + w[n].shape, F32) for n in names]
        return Exchange(tuple(bufs), _sibling_plan(len(names)), len(names))

    def chip_sums(names, bufs):
        k = len(names)
        own, sums = zip(*[chip_sum("chip_sum_" + n, bufs[a], bufs[k + a], place) for a, n in enumerate(names)])
        zones = [lax.empty((3,) + w[n].shape, BF16) for n in names]
        return own, Exchange(tuple(sums) + tuple(zones), _chips_plan(k), 3 * k)

    proj, ffn = ("w_sb_proj", "w_ca_proj", "w_out"), ("w_mlp_in", "w_mlp_out")
    zones = place_shards([w[n] for n in proj + ffn])
    proj_zones, ffn_zones = tuple(zones[:len(proj)]), tuple(zones[len(proj):])
    wg_in = all_gather("gather_w_in", [w_in.astype(BF16)])[0]
    (qkv, lg, xb), proj_zones = in_proj(
        x, wg_in, guest=Exchange(proj_zones, _gather_first_plan(len(proj)), 4 * len(proj)))
    bias_tiles = ca_bias_tiles(rel_bias)
    both = Exchange(ffn_zones + tuple(proj_zones),
                    _both_plans(_gather_first_plan(len(ffn)), len(ffn), _gather_second_plan(len(proj))),
                    4 * len(ffn) + 3 * len(proj))
    (o_sb,), zones = sb_fwd(qkv, both)
    ffn_zones, proj_zones = tuple(zones[:len(ffn)]), zones[len(ffn):]
    (o_ca,), ffn_zones = ca_fwd(qkv, bias_tiles, Exchange(ffn_zones, _gather_second_plan(len(ffn)), 3 * len(ffn)))
    wg = dict(zip(proj + ffn, list(proj_zones) + list(ffn_zones)))
    wg_sb, wg_ca, wg_mi = wg["w_sb_proj"], wg["w_ca_proj"], wg["w_mlp_in"]
    wg_mo = wg["w_mlp_out"].reshape(D_FF, D_MODEL)
    w_out_b = wg["w_out"].reshape(D_MODEL, D_MODEL)

    u1 = mix_fwd(x, o_sb, o_ca, lg, b_gate, wg_sb, wg_ca, w_out_b)
    a, x1b, dhm, du2b, du1, loss, dg2, db2, dg1, db1 = mlp_block(
        u1, target, wg_mi, wg_mo, ln1_g, ln1_b, ln2_g, ln2_b)
    g = {}
    g["w_mlp_in"] = wgrad("wgrad_mlp_in", x1b, dhm, tm=D_MODEL, tn=FF_SHARD, shard_cols=FF_SHARD)
    g["w_mlp_out"] = wgrad("wgrad_mlp_out", a, du2b, tm=FF_SHARD, tn=D_MODEL)

    mlp = ("w_mlp_in", "w_mlp_out")
    (dlg, do_sb, do_ca, dbg, g["w_out"], g["w_sb_proj"], g["w_ca_proj"]), bufs = mix_bwd(
        du1, o_sb, o_ca, lg, b_gate, wg_sb, wg_ca, w_out_b, guest=siblings(mlp, g))
    own_mlp, mlp_chips = chip_sums(mlp, bufs)

    att = ("w_out", "w_sb_proj", "w_ca_proj")
    att_siblings = siblings(att, g)
    both = Exchange(mlp_chips.bufs + att_siblings.bufs,
                    _both_plans(mlp_chips.plan, len(mlp_chips.bufs), att_siblings.plan),
                    mlp_chips.copies + att_siblings.copies)
    (dq_sb, dk_sb, dv_sb), bufs = sb_bwd(qkv, o_sb, do_sb, guest=both)
    arrived_mlp, bufs = bufs[len(mlp):len(mlp_chips.bufs)], bufs[len(mlp_chips.bufs):]
    own_att, att_chips = chip_sums(att, bufs)
    (dq_ca, dk_ca, dv_ca, db_tiles), bufs = ca_bwd(qkv, bias_tiles, do_ca, guest=att_chips)
    arrived_att = bufs[len(att):]
    g_rel = rel_bias_grad(db_tiles)

    dh = dh_slab([dq_sb, dk_sb, dv_sb, dq_ca, dk_ca, dv_ca, dlg])
    g["w_in"] = wgrad("wgrad_in", xb, dh, tm=D_MODEL, tn=2 * IN_SHARD, shard_cols=IN_SHARD)
    last = ("w_in",)
    bufs = finish(mlp[0], own_mlp[0], arrived_mlp[0], guest=siblings(last, g))
    finish(mlp[1], own_mlp[1], arrived_mlp[1])
    for n, own, arrived in zip(att, own_att, arrived_att):
        finish(n, own, arrived)
    own_in, in_chips = chip_sums(last, bufs)
    (grad_x,), bufs = in_bwd(du1, dh, wg_in, guest=in_chips)
    finish(last[0], own_in[0], bufs[1])

    small = dict(b_gate=dbg, rel_bias=g_rel, ln1_g=dg1, ln1_b=db1, ln2_g=dg2, ln2_b=db2)
    parts = all_gather("gather_small_grads", [_pack_small(small, loss[0, 0])])[0]
    packed = adamw("adamw_small", _pack_small(w), _pack_small(m), _pack_small(v),
                   [(parts, d) for d in range(N_DEV)])
    loss = packed[0][-1, 0]
    for out, p in zip((grads, deltas, new_m, new_v), packed):
        for n, val in _unpack_small(p).items():
            out[n] = val.reshape(w[n].shape)

    return (loss, grad_x[None], *[grads[n] for n in NAMES], *[deltas[n] for n in NAMES],
            *[new_m[n] for n in NAMES], *[new_v[n] for n in NAMES])
```

```python
import functools
from typing import NamedTuple

import jax
import jax.numpy as jnp
from jax import lax
from jax.experimental import pallas as pl
from jax.experimental.pallas import tpu as pltpu

F32 = jnp.float32
BF16 = jnp.bfloat16
MESH = pl.DeviceIdType.MESH

N_DEV = 8
D_MODEL = 1024
HEAD_DIM = 64
ATT_WIDTH = 512
N_PAIRS = ATT_WIDTH // 128
QKV_COLS = 6 * ATT_WIDTH
GATE_COLS = 2 * D_MODEL
IN_COLS = QKV_COLS + GATE_COLS
IN_SHARD = IN_COLS // N_DEV
D_FF = 4 * D_MODEL
FF_SHARD = D_FF // N_DEV
PROJ_SHARD = D_MODEL // N_DEV
ATT_BLOCK = 128
CA_TILES = 5
CA_ROWS = 1024
CHUNK = 64
CA_PREV_CHUNKS = 8
REL_CLIP = 256
REL_PAD = 640
ALPHA = 2.0 ** 0.25
LN_EPS = 1e-5
QK_SCALE = HEAD_DIM ** -0.5
NEG_BIG = -1e30
VMEM_LIMIT = 56 * 1024 * 1024

ADAM_LR = 0.001
ADAM_B1 = 0.9
ADAM_B2 = 0.999
ADAM_EPS = 1e-08
ADAM_WD = 0.01
ADAM_STEP = 10

_NT = (((1,), (1,)), ((), ()))
_TN = (((0,), (0,)), ((), ()))


def _dot(a, b):
    return jnp.dot(a, b, preferred_element_type=F32)


def _dot_nt(a, b):
    return lax.dot_general(a, b, _NT, preferred_element_type=F32)


def _dot_tn(a, b):
    return lax.dot_general(a, b, _TN, preferred_element_type=F32)


def _params(semantics=None):
    return pltpu.CompilerParams(dimension_semantics=semantics, vmem_limit_bytes=VMEM_LIMIT)


def _row(v):
    return v.reshape(1, -1)


HBM_SPEC = pl.BlockSpec(memory_space=pl.ANY)


class Exchange(NamedTuple):
    bufs: tuple
    plan: object
    copies: int


def _pallas(body, guest, args, *, grid, in_specs, out_specs, out_shape, scratch_shapes=(), **kw):
    out_specs, out_shape, scratch_shapes = list(out_specs), list(out_shape), list(scratch_shapes)
    if guest is None:
        return pl.pallas_call(body, grid=grid, in_specs=in_specs, out_specs=out_specs, out_shape=out_shape,
                              scratch_shapes=scratch_shapes, **kw)(*args)
    nb, n_own = len(guest.bufs), len(in_specs) + len(out_specs)

    def hosting(*refs):
        zones = refs[nb + n_own:2 * nb + n_own]
        send_sems, recv_sems = refs[-2], refs[-1]
        steps = [pl.program_id(axis) for axis in range(len(grid))]
        first = functools.reduce(jnp.logical_and, [step == 0 for step in steps])
        last = functools.reduce(jnp.logical_and, [step == n - 1 for step, n in zip(steps, grid)])

        def copy(k, src, dst, peer):
            return pltpu.make_async_remote_copy(src_ref=src, dst_ref=dst, send_sem=send_sems.at[k],
                                                recv_sem=recv_sems.at[k], device_id=peer, device_id_type=MESH)

        @pl.when(first)
        def _():
            for k, (src, dst, peer, _) in enumerate(guest.plan(zones)):
                copy(k, src, dst, peer).start()

        body(*refs[nb:nb + n_own], *refs[2 * nb + n_own:-2])

        @pl.when(last)
        def _():
            for k, (src, dst, peer, landed) in enumerate(guest.plan(zones)):
                copy(k, src, dst, peer).wait_send()
                copy(k, src, landed, peer).wait_recv()

    sems = pltpu.SemaphoreType.DMA((guest.copies,))
    results = pl.pallas_call(
        hosting, grid=grid, in_specs=[HBM_SPEC] * nb + list(in_specs), out_specs=out_specs + [HBM_SPEC] * nb,
        out_shape=out_shape + [jax.ShapeDtypeStruct(b.shape, b.dtype) for b in guest.bufs],
        input_output_aliases={k: len(out_specs) + k for k in range(nb)},
        scratch_shapes=scratch_shapes + [sems, sems], **kw)(*guest.bufs, *args)
    return results[:len(out_specs)], results[len(out_specs):]


IN_PAIR = 2 * IN_SHARD
IN_PAIR_SCRATCH = [pltpu.VMEM((N_DEV // 2, D_MODEL, IN_PAIR), BF16), pltpu.SemaphoreType.DMA((N_DEV,))]


def _load_in_pairs(w_hbm, pairs_s, sems):
    copies = [pltpu.make_async_copy(w_hbm.at[j], pairs_s.at[j // 2, :, pl.ds(IN_SHARD * (j % 2), IN_SHARD)],
                                    sems.at[j]) for j in range(N_DEV)]
    for cp in copies:
        cp.start()
    for cp in copies:
        cp.wait()


def in_proj(x, wg_in, tm=512, guest=None):
    s = x.shape[0]

    def body(x_ref, w_hbm, qkv_ref, lg_ref, xb_ref, pairs_s, sems):
        @pl.when(pl.program_id(0) == 0)
        def _():
            _load_in_pairs(w_hbm, pairs_s, sems)

        xb = x_ref[...].astype(BF16)
        xb_ref[...] = xb
        for j in range(N_DEV // 2):
            acc = _dot(xb, pairs_s[j])
            lo, hi = IN_PAIR * j, IN_PAIR * (j + 1)
            if hi <= QKV_COLS:
                qkv_ref[:, lo:hi] = acc.astype(BF16)
            elif lo >= QKV_COLS:
                lg_ref[:, lo - QKV_COLS:hi - QKV_COLS] = acc
            else:
                qkv_ref[:, lo:QKV_COLS] = acc[:, :QKV_COLS - lo].astype(BF16)
                lg_ref[:, 0:hi - QKV_COLS] = acc[:, QKV_COLS - lo:]

    return _pallas(
        body, guest, (x, wg_in), name="in_proj", grid=(s // tm,), scratch_shapes=IN_PAIR_SCRATCH,
        in_specs=[pl.BlockSpec((tm, D_MODEL), lambda i: (i, 0)), HBM_SPEC],
        out_specs=[pl.BlockSpec((tm, QKV_COLS), lambda i: (i, 0)),
                   pl.BlockSpec((tm, GATE_COLS), lambda i: (i, 0)),
                   pl.BlockSpec((tm, D_MODEL), lambda i: (i, 0))],
        out_shape=[jax.ShapeDtypeStruct((s, QKV_COLS), BF16),
                   jax.ShapeDtypeStruct((s, GATE_COLS), F32),
                   jax.ShapeDtypeStruct((s, D_MODEL), BF16)],
        compiler_params=_params(("arbitrary",)),
    )


def _head_masks():
    lane = lax.broadcasted_iota(jnp.int32, (1, 128), 1)
    first = lane < HEAD_DIM
    return first, jnp.logical_not(first)


SB_CHUNK = 512
SB_SUB = SB_CHUNK // ATT_BLOCK


def _tri_pair():
    row = lax.broadcasted_iota(jnp.int32, (4 * ATT_BLOCK, 2 * ATT_BLOCK), 0) & (2 * ATT_BLOCK - 1)
    col = lax.broadcasted_iota(jnp.int32, (4 * ATT_BLOCK, 2 * ATT_BLOCK), 1)
    same = (row >= ATT_BLOCK) == (col >= ATT_BLOCK)
    return jnp.logical_and(same, row >= col).astype(BF16)


def _pair_bf16(pieces):
    return jnp.concatenate(
        [jnp.concatenate([p.astype(BF16) for p in row], axis=1) for row in pieces], axis=0)


def _suffix_sums(pieces, tri_pair):
    hi = [[p.astype(BF16) for p in row] for row in pieces]
    lo = [[p - h.astype(F32) for p, h in zip(row, hrow)] for row, hrow in zip(pieces, hi)]
    return _dot(jnp.concatenate([_pair_bf16(hi), _pair_bf16(lo)], axis=1), tri_pair)


def _first_row_piece(diag, j):
    return j if diag else 0


def _piece(v, r, h):
    return v[ATT_BLOCK * r:ATT_BLOCK * (r + 1), ATT_BLOCK * h:ATT_BLOCK * (h + 1)]


def _all_row_pieces(slabs):
    return [[slab[ATT_BLOCK * r:ATT_BLOCK * (r + 1)] for r in range(SB_SUB)] for slab in slabs]


def _add_rows(total, part, r0):
    if r0 == 0:
        return total + part
    return jnp.concatenate([total[:ATT_BLOCK * r0], total[ATT_BLOCK * r0:] + part], axis=0)


def _row_totals(csum, r0, h):
    return jnp.concatenate(
        [jnp.broadcast_to(_piece(csum, at, h)[:, 0:1], (ATT_BLOCK, ATT_BLOCK)) for at in range(SB_SUB - r0)], axis=0)


SIGN_BIT = 0x80000000


def _drop(z):
    minus_abs = lax.bitcast_convert_type(lax.bitcast_convert_type(z, jnp.uint32) | jnp.uint32(SIGN_BIT), F32)
    return jnp.maximum(z, 0.0) + jnp.log(1.0 + jnp.exp(minus_abs))


def _sb_weights(z, runs, r0, masked, strict, tri_pair):
    drops = []
    for r in range(r0, SB_SUB):
        row = [_drop(_piece(z, r - r0, h)) for h in range(2)]
        drops.append([jnp.where(strict, drop, 0.0) for drop in row] if r == masked else row)
    csum = _suffix_sums(drops, tri_pair)
    weights = []
    for r in range(r0, SB_SUB):
        row = [jnp.exp(_piece(z, r - r0, h) - (_piece(csum, r - r0, h) + runs[h][r])) for h in range(2)]
        weights.append([jnp.where(strict, a, 0.0) for a in row] if r == masked else row)
    return drops, weights, csum


EXP_UNDERFLOW = 104.0


SB_STEP = 2


def _sweep_left(span, i, dropped_ref):
    def more(at):
        return jnp.logical_and(at >= 0, jnp.min(dropped_ref[...]) < EXP_UNDERFLOW)

    def step(at):
        span(at * (SB_STEP * ATT_BLOCK), SB_STEP, False)
        return at - 1

    lax.while_loop(more, step, i * (SB_SUB // SB_STEP) - 1)


def _by_head(blk, masks):
    zero = jnp.zeros_like(blk)
    return jnp.concatenate([jnp.where(m, blk, zero) for m in masks], axis=0)


def sb_fwd(qkv, guest=None):
    s = qkv.shape[0]
    nq = s // SB_CHUNK

    def body(q_ref, k_ref, v_ref, o_ref, run_s):
        i = pl.program_id(1)
        masks = _head_masks()
        q = q_ref[...] * QK_SCALE
        row = lax.broadcasted_iota(jnp.int32, (ATT_BLOCK, ATT_BLOCK), 0)
        col = lax.broadcasted_iota(jnp.int32, (ATT_BLOCK, ATT_BLOCK), 1)
        strict = col < row
        tri_pair = _tri_pair()
        run_s[...] = jnp.zeros_like(run_s)
        o_ref[...] = jnp.zeros_like(o_ref)

        def span(start, pieces, diag):
            runs = [run_s[0], run_s[1]]
            acc = jnp.zeros((SB_CHUNK, ATT_BLOCK), F32)
            for j in reversed(range(pieces)):
                r0 = _first_row_piece(diag, j)
                keys = pl.ds(pl.multiple_of(start + ATT_BLOCK * j, ATT_BLOCK), ATT_BLOCK)
                z = _dot_nt(q[ATT_BLOCK * r0:], _by_head(k_ref[keys, :], masks))
                _, weights, csum = _sb_weights(z, _all_row_pieces(runs), r0, j if diag else None, strict, tri_pair)
                part = _dot(_pair_bf16(weights), _by_head(v_ref[keys, :], masks))
                acc = _add_rows(acc, part, r0)
                runs = [_add_rows(runs[h], _row_totals(csum, r0, h), r0) for h in range(2)]
            run_s[0], run_s[1] = runs
            o_ref[...] += acc

        span(i * SB_CHUNK, SB_SUB, True)
        _sweep_left(span, i, run_s)

    return _pallas(
        body, guest, (qkv, qkv, qkv), name="sb_fwd", grid=(N_PAIRS, nq),
        in_specs=[pl.BlockSpec((SB_CHUNK, 128), lambda p, i: (i, p)),
                  pl.BlockSpec((s, 128), lambda p, i: (0, N_PAIRS + p)),
                  pl.BlockSpec((s, 128), lambda p, i: (0, 2 * N_PAIRS + p))],
        out_specs=[pl.BlockSpec((SB_CHUNK, 128), lambda p, i: (i, p))],
        out_shape=[jax.ShapeDtypeStruct((s, ATT_WIDTH), F32)],
        scratch_shapes=[pltpu.VMEM((2, SB_CHUNK, 128), F32)],
        compiler_params=_params(("arbitrary", "arbitrary")),
    )


def sb_bwd(qkv, o_sb, do_sb, guest=None):
    s = qkv.shape[0]
    nq = s // SB_CHUNK

    def body(q_ref, k_ref, v_ref, o_ref, do_ref, dq_ref, dk_ref, dv_ref, run_l_s, run_e_s, etot_s, dq_s, dk_s, dv_s):
        i = pl.program_id(1)

        @pl.when(i == 0)
        def _():
            dk_s[...] = jnp.zeros_like(dk_s)
            dv_s[...] = jnp.zeros_like(dv_s)

        masks = _head_masks()
        q = q_ref[...] * QK_SCALE
        do = do_ref[...]
        prod = do.astype(F32) * o_ref[...]
        for h in range(2):
            total = jnp.sum(jnp.where(masks[h], prod, 0.0), axis=1, keepdims=True)
            etot_s[h] = jnp.broadcast_to(total, (SB_CHUNK, ATT_BLOCK))
        row = lax.broadcasted_iota(jnp.int32, (ATT_BLOCK, ATT_BLOCK), 0)
        col = lax.broadcasted_iota(jnp.int32, (ATT_BLOCK, ATT_BLOCK), 1)
        strict = col < row
        tri_pair = _tri_pair()
        run_l_s[...] = jnp.zeros_like(run_l_s)
        run_e_s[...] = jnp.zeros_like(run_e_s)
        dq_s[...] = jnp.zeros_like(dq_s)

        def span(start, pieces, diag):
            slabs_l = [run_l_s[0], run_l_s[1]]
            slabs_e = [run_e_s[0], run_e_s[1]]
            e_tot = _all_row_pieces([etot_s[0], etot_s[1]])
            dq = jnp.zeros((SB_CHUNK, ATT_BLOCK), F32)
            for j in reversed(range(pieces)):
                runs_l, runs_e = _all_row_pieces(slabs_l), _all_row_pieces(slabs_e)
                r0 = _first_row_piece(diag, j)
                masked = j if diag else None
                keys = pl.ds(pl.multiple_of(start + ATT_BLOCK * j, ATT_BLOCK), ATT_BLOCK)
                kk = _by_head(k_ref[keys, :], masks)
                vv = _by_head(v_ref[keys, :], masks)
                q_rows, do_rows = q[ATT_BLOCK * r0:], do[ATT_BLOCK * r0:]
                z = _dot_nt(q_rows, kk)
                da = _dot_nt(do_rows, vv)
                drops, weights, csum = _sb_weights(z, runs_l, r0, masked, strict, tri_pair)
                ab = [[a.astype(BF16) for a in wrow] for wrow in weights]
                es = [[a.astype(F32) * _piece(da, at, h) for h, a in enumerate(arow)] for at, arow in enumerate(ab)]
                esum = _suffix_sums(es, tri_pair)
                dzs = []
                for at, r in enumerate(range(r0, SB_SUB)):
                    dz_row = []
                    for h in range(2):
                        sig = jnp.exp(_piece(z, at, h) - drops[at][h])
                        after = _piece(esum, at, h) + runs_e[h][r] - es[at][h]
                        dz = es[at][h] - sig * (e_tot[h][r] - after)
                        dz_row.append(jnp.where(strict, dz, 0.0) if r == masked else dz)
                    dzs.append(dz_row)
                dz = _pair_bf16(dzs)
                dq = _add_rows(dq, _dot(dz, kk), r0)
                dk = _dot_tn(dz, q_rows)
                dv = _dot_tn(_pair_bf16(ab), do_rows)
                dk_s[keys, :] += jnp.where(masks[0], dk[:ATT_BLOCK], dk[ATT_BLOCK:])
                dv_s[keys, :] += jnp.where(masks[0], dv[:ATT_BLOCK], dv[ATT_BLOCK:])
                slabs_l =[_add_rows(slabs_l[h], _row_totals(csum, r0, h), r0) for h in range(2)]
                slabs_e = [_add_rows(slabs_e[h], _row_totals(esum, r0, h), r0) for h in range(2)]
            run_l_s[0], run_l_s[1] = slabs_l
            run_e_s[0], run_e_s[1] = slabs_e
            dq_s[...] += dq

        span(i * SB_CHUNK, SB_SUB, True)
        _sweep_left(span, i, run_l_s)
        dq_ref[...] = (dq_s[...] * QK_SCALE).astype(BF16)

        @pl.when(i == nq - 1)
        def _():
            dk_ref[...] = dk_s[...].astype(BF16)
            dv_ref[...] = dv_s[...].astype(BF16)

    tile = lambda p, i: (i, p)
    state = pltpu.VMEM((2, SB_CHUNK, 128), F32)
    sums = pltpu.VMEM((s, 128), F32)
    return _pallas(
        body, guest, (qkv, qkv, qkv, o_sb, do_sb), name="sb_bwd", grid=(N_PAIRS, nq),
        in_specs=[pl.BlockSpec((SB_CHUNK, 128), tile),
                  pl.BlockSpec((s, 128), lambda p, i: (0, N_PAIRS + p)),
                  pl.BlockSpec((s, 128), lambda p, i: (0, 2 * N_PAIRS + p)),
                  pl.BlockSpec((SB_CHUNK, 128), tile),
                  pl.BlockSpec((SB_CHUNK, 128), tile)],
        out_specs=[pl.BlockSpec((SB_CHUNK, 128), tile),
                   pl.BlockSpec((s, 128), lambda p, i: (0, p)),
                   pl.BlockSpec((s, 128), lambda p, i: (0, p))],
        out_shape=[jax.ShapeDtypeStruct((s, ATT_WIDTH), BF16)] * 3,
        scratch_shapes=[state, state, state, pltpu.VMEM((SB_CHUNK, 128), F32), sums, sums],
        compiler_params=_params(("arbitrary", "arbitrary")),
    )


def ca_bias_tiles(rel_bias):
    heads = rel_bias.shape[0]
    wide = 2 * ATT_BLOCK

    def body(rel_ref, out_ref):
        bins = lax.broadcasted_iota(jnp.int32, (REL_PAD, wide), 0)
        pos = lax.broadcasted_iota(jnp.int32, (REL_PAD, wide), 1)
        r = lax.broadcasted_iota(jnp.int32, (ATT_BLOCK, ATT_BLOCK), 0)
        c = lax.broadcasted_iota(jnp.int32, (ATT_BLOCK, ATT_BLOCK), 1)
        for j in range(CA_TILES):
            hot = (jnp.clip(ATT_BLOCK * (j + 1) - 1 - pos, -REL_CLIP, REL_CLIP) + REL_CLIP == bins).astype(F32)
            lines = jnp.dot(rel_ref[...], hot, preferred_element_type=F32, precision=lax.Precision.HIGHEST)
            back = 2 * j + (r >> 6) - (c >> 6)
            ok = jnp.logical_and(back >= 0, back <= CA_PREV_CHUNKS)
            for h in range(heads):
                tile = jnp.broadcast_to(lines[h:h + 1, :], (ATT_BLOCK, wide))
                tile = pltpu.roll(tile, wide - (ATT_BLOCK - 1), 1, stride=1, stride_axis=0)
                out_ref[h, j] = jnp.where(ok, tile[:, :ATT_BLOCK], NEG_BIG)

    padded = jnp.pad(rel_bias, ((0, 0), (0, REL_PAD - rel_bias.shape[1])))
    return pl.pallas_call(
        body, name="ca_bias_tiles",
        out_shape=jax.ShapeDtypeStruct((heads, CA_TILES, ATT_BLOCK, ATT_BLOCK), F32),
        compiler_params=_params(),
    )(padded)


CA_BLOCKS = CA_ROWS // ATT_BLOCK
CA_KEY_BLOCKS = CA_BLOCKS + CA_TILES - 1


def _ca_users(t):
    return [r for r in range(CA_BLOCKS) if 0 <= r + CA_TILES - 1 - t < CA_TILES]


def _ca_rows(v, t, r):
    at = _ca_users(t).index(r)
    return v[ATT_BLOCK * at:ATT_BLOCK * (at + 1)]


def _ca_window(step, k_ref, v_ref, masks):
    kks, vvs, inside, rows = [], [], [], []
    for t in range(CA_KEY_BLOCKS):
        block = step * CA_BLOCKS - (CA_TILES - 1) + t
        keys = pl.ds(pl.multiple_of(jnp.maximum(block, 0) * ATT_BLOCK, ATT_BLOCK), ATT_BLOCK)
        kks.append(_by_head(k_ref[keys, :], masks))
        vvs.append(_by_head(v_ref[keys, :], masks))
        inside.append(block >= 0)
        rows.append(keys)
    return kks, vvs, inside, rows


def _ca_probs(q, kks, inside, bias_ref):
    scores = [[[None] * CA_TILES for _ in range(2)] for _ in range(CA_BLOCKS)]
    for t in range(CA_KEY_BLOCKS):
        users = _ca_users(t)
        z = _dot_nt(q[ATT_BLOCK * users[0]:ATT_BLOCK * (users[-1] + 1)], kks[t])
        for r in users:
            j = r + CA_TILES - 1 - t
            for h in range(2):
                zz = _ca_rows(z, t, r)[:, ATT_BLOCK * h:ATT_BLOCK * (h + 1)] + bias_ref[h, j]
                scores[r][h][j] = jnp.where(inside[t], zz, NEG_BIG)
    probs = [[None, None] for _ in range(CA_BLOCKS)]
    for r in range(CA_BLOCKS):
        for h in range(2):
            m = jnp.max(functools.reduce(jnp.maximum, scores[r][h]), axis=1, keepdims=True)
            ex = [jnp.exp(z - m) for z in scores[r][h]]
            inv = 1.0 / jnp.sum(functools.reduce(jnp.add, ex), axis=1, keepdims=True)
            probs[r][h] = [e * inv for e in ex]
    return probs


def _ca_stack(tiles, t):
    return jnp.concatenate(
        [jnp.concatenate([tiles[r][h][r + CA_TILES - 1 - t].astype(BF16) for h in range(2)], axis=1)
         for r in _ca_users(t)], axis=0)


def ca_fwd(qkv, bias_tiles, guest=None):
    s = qkv.shape[0]
    nq = s // CA_ROWS

    def body(q_ref, k_ref, v_ref, bias_ref, o_ref):
        masks = _head_masks()
        kks, vvs, inside, _ = _ca_window(pl.program_id(1), k_ref, v_ref, masks)
        probs = _ca_probs(q_ref[...] * QK_SCALE, kks, inside, bias_ref)
        out = [None] * CA_BLOCKS
        for t in range(CA_KEY_BLOCKS):
            part = _dot(_ca_stack(probs, t), vvs[t])
            for r in _ca_users(t):
                piece = _ca_rows(part, t, r)
                out[r] = piece if out[r] is None else out[r] + piece
        o_ref[...] = jnp.concatenate(out, axis=0).astype(BF16)

    return _pallas(
        body, guest, (qkv, qkv, qkv, bias_tiles), name="ca_fwd", grid=(N_PAIRS, nq),
        in_specs=[pl.BlockSpec((CA_ROWS, 128), lambda p, i: (i, 3 * N_PAIRS + p)),
                  pl.BlockSpec((s, 128), lambda p, i: (0, 4 * N_PAIRS + p)),
                  pl.BlockSpec((s, 128), lambda p, i: (0, 5 * N_PAIRS + p)),
                  pl.BlockSpec((2, CA_TILES, ATT_BLOCK, ATT_BLOCK), lambda p, i: (p, 0, 0, 0))],
        out_specs=[pl.BlockSpec((CA_ROWS, 128), lambda p, i: (i, p))],
        out_shape=[jax.ShapeDtypeStruct((s, ATT_WIDTH), BF16)],
        compiler_params=_params(("arbitrary", "arbitrary")),
    )


def ca_bwd(qkv, bias_tiles, do_ca, guest=None):
    s = qkv.shape[0]
    nq = s // CA_ROWS

    def body(q_ref, k_ref, v_ref, bias_ref, do_ref, dq_ref, dk_ref, dv_ref, db_ref, dk_s, dv_s):
        big = pl.program_id(1)

        @pl.when(big == 0)
        def _():
            dk_s[...] = jnp.zeros_like(dk_s)
            dv_s[...] = jnp.zeros_like(dv_s)
            db_ref[...] = jnp.zeros_like(db_ref)

        masks = _head_masks()
        kks, vvs, inside, key_rows = _ca_window(big, k_ref, v_ref, masks)
        q = q_ref[...] * QK_SCALE
        do = do_ref[...]
        probs = _ca_probs(q, kks, inside, bias_ref)
        dps = [[[None] * CA_TILES for _ in range(2)] for _ in range(CA_BLOCKS)]
        for t in range(CA_KEY_BLOCKS):
            users = _ca_users(t)
            dp = _dot_nt(do[ATT_BLOCK * users[0]:ATT_BLOCK * (users[-1] + 1)], vvs[t])
            for r in users:
                for h in range(2):
                    dps[r][h][r + CA_TILES - 1 - t] = _ca_rows(dp, t, r)[:, ATT_BLOCK * h:ATT_BLOCK * (h + 1)]
        dss = [[None, None] for _ in range(CA_BLOCKS)]
        for r in range(CA_BLOCKS):
            for h in range(2):
                delta = jnp.sum(functools.reduce(jnp.add, [p * dp for p, dp in zip(probs[r][h], dps[r][h])]),
                                axis=1, keepdims=True)
                dss[r][h] = [p * (dp - delta) for p, dp in zip(probs[r][h], dps[r][h])]
        for h in range(2):
            for j in range(CA_TILES):
                db_ref[h, j] += functools.reduce(jnp.add, [dss[r][h][j] for r in range(CA_BLOCKS)])
        dq = [None] * CA_BLOCKS
        for t in range(CA_KEY_BLOCKS):
            users = _ca_users(t)
            rows = slice(ATT_BLOCK * users[0], ATT_BLOCK * (users[-1] + 1))
            ds = _ca_stack(dss, t)
            part = _dot(ds, kks[t])
            for r in users:
                piece = _ca_rows(part, t, r)
                dq[r] = piece if dq[r] is None else dq[r] + piece
            dk = _dot_tn(ds, q[rows])
            dv = _dot_tn(_ca_stack(probs, t), do[rows])
            dk_s[key_rows[t], :] += jnp.where(masks[0], dk[:ATT_BLOCK], dk[ATT_BLOCK:])
            dv_s[key_rows[t], :] += jnp.where(masks[0], dv[:ATT_BLOCK], dv[ATT_BLOCK:])
        dq_ref[...] = (jnp.concatenate(dq, axis=0) * QK_SCALE).astype(BF16)

        @pl.when(big == nq - 1)
        def _():
            dk_ref[...] = dk_s[...].astype(BF16)
            dv_ref[...] = dv_s[...].astype(BF16)

    return _pallas(
        body, guest, (qkv, qkv, qkv, bias_tiles, do_ca), name="ca_bwd", grid=(N_PAIRS, nq),
        in_specs=[pl.BlockSpec((CA_ROWS, 128), lambda p, i: (i, 3 * N_PAIRS + p)),
                  pl.BlockSpec((s, 128), lambda p, i: (0, 4 * N_PAIRS + p)),
                  pl.BlockSpec((s, 128), lambda p, i: (0, 5 * N_PAIRS + p)),
                  pl.BlockSpec((2, CA_TILES, ATT_BLOCK, ATT_BLOCK), lambda p, i: (p, 0, 0, 0)),
                  pl.BlockSpec((CA_ROWS, 128), lambda p, i: (i, p))],
        out_specs=[pl.BlockSpec((CA_ROWS, 128), lambda p, i: (i, p)),
                   pl.BlockSpec((s, 128), lambda p, i: (0, p)),
                   pl.BlockSpec((s, 128), lambda p, i: (0, p)),
                   pl.BlockSpec((2, CA_TILES, ATT_BLOCK, ATT_BLOCK), lambda p, i: (p, 0, 0, 0))],
        out_shape=[jax.ShapeDtypeStruct((s, ATT_WIDTH), BF16)] * 3
                  + [jax.ShapeDtypeStruct((2 * N_PAIRS, CA_TILES, ATT_BLOCK, ATT_BLOCK), F32)],
        scratch_shapes=[pltpu.VMEM((s, 128), F32), pltpu.VMEM((s, 128), F32)],
        compiler_params=_params(("arbitrary", "arbitrary")),
    )


def rel_bias_grad(db_tiles):
    heads = db_tiles.shape[0]

    def body(db_ref, out_ref):
        row = lax.broadcasted_iota(jnp.int32, (ATT_BLOCK, ATT_BLOCK), 0)
        col = lax.broadcasted_iota(jnp.int32, (ATT_BLOCK, ATT_BLOCK), 1)
        flip = (row + col == ATT_BLOCK - 1).astype(F32)
        wrapped = col < row
        cp = lax.broadcasted_iota(jnp.int32, (ATT_BLOCK, REL_PAD), 0)
        bins = lax.broadcasted_iota(jnp.int32, (ATT_BLOCK, REL_PAD), 1)
        total = jnp.zeros((heads, REL_PAD), F32)
        for j in range(CA_TILES):
            sums_neg, sums_pos = [], []
            for h in range(heads):
                tile = jnp.dot(db_ref[h, j], flip, preferred_element_type=F32, precision=lax.Precision.HIGHEST)
                tile = pltpu.roll(tile, 0, 1, stride=1, stride_axis=0)
                sums_neg.append(jnp.sum(jnp.where(wrapped, 0.0, tile), axis=0, keepdims=True))
                sums_pos.append(jnp.sum(jnp.where(wrapped, tile, 0.0), axis=0, keepdims=True))
            neg = jnp.concatenate(sums_neg, axis=0)
            pos = jnp.concatenate(sums_pos, axis=0)
            hot_neg = (jnp.clip(ATT_BLOCK * j + cp - (ATT_BLOCK - 1), -REL_CLIP, REL_CLIP) + REL_CLIP == bins)
            hot_pos = (jnp.clip(ATT_BLOCK * j + cp + 1, -REL_CLIP, REL_CLIP) + REL_CLIP == bins)
            total = total + jnp.dot(neg, hot_neg.astype(F32), preferred_element_type=F32,
                                    precision=lax.Precision.HIGHEST)
            total = total + jnp.dot(pos, hot_pos.astype(F32), preferred_element_type=F32,
                                    precision=lax.Precision.HIGHEST)
        out_ref[...] = total

    return pl.pallas_call(
        body, name="rel_bias_grad",
        out_shape=jax.ShapeDtypeStruct((heads, REL_PAD), F32),
        compiler_params=_params(),
    )(db_tiles)


def _assemble_cols(dst_ref, src_ref):
    w = src_ref.shape[2]
    for j in range(N_DEV):
        dst_ref[:, w * j:w * (j + 1)] = src_ref[j]


def _merge(o_sb, o_ca, lg, bg, wsb, wca):
    y_sb = _dot(o_sb.astype(BF16), wsb)
    y_ca = _dot(o_ca.astype(BF16), wca)
    gates = jax.nn.sigmoid(lg + bg)
    g_sb, g_ca = gates[:, :D_MODEL], gates[:, D_MODEL:]
    return y_sb, y_ca, g_sb, g_ca, g_sb * y_sb + g_ca * y_ca


def mix_bwd(du1, o_sb, o_ca, lg, b_gate, wg_sb, wg_ca, w_out, tm=256, guest=None):
    s = du1.shape[0]
    last = s // tm - 1

    def body(du_ref, osb_ref, oca_ref, lg_ref, bg_ref, wsb_ref, wca_ref, wout_ref,
             dlg_ref, dosb_ref, doca_ref, dbg_ref, gout_ref, gsb_ref, gca_ref, wsb_s, wca_s, gsb_s, gca_s):
        step = pl.program_id(0)

        @pl.when(step == 0)
        def _():
            _assemble_cols(wsb_s, wsb_ref)
            _assemble_cols(wca_s, wca_ref)
            for ref in (dbg_ref, gout_ref, gsb_s, gca_s):
                ref[...] = jnp.zeros_like(ref)

        o_sb, o_ca, du = osb_ref[...].astype(BF16), oca_ref[...].astype(BF16), du_ref[...].astype(BF16)
        y_sb, y_ca, g_sb, g_ca, merged = _merge(o_sb, o_ca, lg_ref[...], bg_ref[...], wsb_s[...], wca_s[...])
        dm = _dot_nt(du, wout_ref[...])
        dl_sb = dm * y_sb * g_sb * (1.0 - g_sb)
        dl_ca = dm * y_ca * g_ca * (1.0 - g_ca)
        dlg_ref[:, :D_MODEL] = dl_sb.astype(BF16)
        dlg_ref[:, D_MODEL:] = dl_ca.astype(BF16)
        dbg_ref[:, :D_MODEL] += jnp.sum(dl_sb, axis=0, keepdims=True)
        dbg_ref[:, D_MODEL:] += jnp.sum(dl_ca, axis=0, keepdims=True)
        dy_sb = (dm * g_sb).astype(BF16)
        dy_ca = (dm * g_ca).astype(BF16)
        dosb_ref[...] = _dot_nt(dy_sb, wsb_s[...]).astype(BF16)
        doca_ref[...] = _dot_nt(dy_ca, wca_s[...]).astype(BF16)
        gout_ref[...] += _dot_tn(merged.astype(BF16), du)
        gsb_s[...] += _dot_tn(o_sb, dy_sb)
        gca_s[...] += _dot_tn(o_ca, dy_ca)

        @pl.when(step == last)
        def _():
            for j in range(N_DEV):
                cols = slice(PROJ_SHARD * j, PROJ_SHARD * (j + 1))
                gsb_ref[j] = gsb_s[:, cols]
                gca_ref[j] = gca_s[:, cols]

    rows = lambda w: pl.BlockSpec((tm, w), lambda i: (i, 0))
    whole = lambda shape: pl.BlockSpec(shape, lambda i: (0,) * len(shape))
    stack = (N_DEV, ATT_WIDTH, PROJ_SHARD)
    return _pallas(
        body, guest, (du1, o_sb, o_ca, lg, _row(b_gate), wg_sb, wg_ca, w_out), name="mix_bwd", grid=(s // tm,),
        in_specs=[rows(D_MODEL), rows(ATT_WIDTH), rows(ATT_WIDTH), rows(GATE_COLS), whole((1, GATE_COLS)),
                  whole(stack), whole(stack), whole((D_MODEL, D_MODEL))],
        out_specs=[rows(GATE_COLS), rows(ATT_WIDTH), rows(ATT_WIDTH), whole((1, GATE_COLS)),
                   whole((D_MODEL, D_MODEL)), whole(stack), whole(stack)],
        out_shape=[jax.ShapeDtypeStruct((s, GATE_COLS), BF16),
                   jax.ShapeDtypeStruct((s, ATT_WIDTH), BF16), jax.ShapeDtypeStruct((s, ATT_WIDTH), BF16),
                   jax.ShapeDtypeStruct((1, GATE_COLS), F32), jax.ShapeDtypeStruct((D_MODEL, D_MODEL), F32),
                   jax.ShapeDtypeStruct(stack, F32), jax.ShapeDtypeStruct(stack, F32)],
        scratch_shapes=[pltpu.VMEM((ATT_WIDTH, D_MODEL), BF16), pltpu.VMEM((ATT_WIDTH, D_MODEL), BF16),
                        pltpu.VMEM((ATT_WIDTH, D_MODEL), F32), pltpu.VMEM((ATT_WIDTH, D_MODEL), F32)],
        compiler_params=_params(("arbitrary",)),
    )


def _ln_stats(u):
    mu = jnp.mean(u, axis=1, keepdims=True)
    cen = u - mu
    var = jnp.mean(cen * cen, axis=1, keepdims=True)
    rstd = lax.rsqrt(var + LN_EPS)
    return cen * rstd, rstd


def _ln_bwd(dy, xhat, rstd, gain):
    dyg = dy * gain
    m1 = jnp.mean(dyg, axis=1, keepdims=True)
    m2 = jnp.mean(dyg * xhat, axis=1, keepdims=True)
    return rstd * (dyg - m1 - xhat * m2)


def _load_mlp_weights(win_hbm, wout_hbm, win_s, wout_s, sems):
    copies = [pltpu.make_async_copy(win_hbm.at[k], win_s.at[:, pl.ds(FF_SHARD * k, FF_SHARD)], sems.at[k])
              for k in range(N_DEV)]
    copies.append(pltpu.make_async_copy(wout_hbm, wout_s, sems.at[N_DEV]))
    for cp in copies:
        cp.start()
    return copies


MLP_WEIGHT_SCRATCH = [pltpu.VMEM((D_MODEL, D_FF), BF16), pltpu.VMEM((D_FF, D_MODEL), BF16),
                      pltpu.SemaphoreType.DMA((N_DEV + 1,))]


def mlp_block(x, o_sb, o_ca, lg, b_gate, wg_sb, wg_ca, w_proj, target, wg_in, w_out,
              ln1_g, ln1_b, ln2_g, ln2_b, tm=256):
    s = x.shape[0]

    def body(x_ref, osb_ref, oca_ref, lg_ref, bg_ref, wsb_hbm, wca_hbm, wproj_hbm, t_ref, win_hbm, wout_hbm,
             g1_ref, b1_ref, g2_ref, b2_ref,
             a_ref, x1b_ref, dhm_ref, du2b_ref, du1_ref, loss_ref, dg2_ref, db2_ref, dg1_ref, db1_ref,
             win_s, wout_s, sems, relu_s, wsb_s, wca_s, wproj_s, proj_sems):
        @pl.when(pl.program_id(0) == 0)
        def _():
            copies = _load_mlp_weights(win_hbm, wout_hbm, win_s, wout_s, sems)
            for j in range(N_DEV):
                cols = pl.ds(PROJ_SHARD * j, PROJ_SHARD)
                copies.append(pltpu.make_async_copy(wsb_hbm.at[j], wsb_s.at[:, cols], proj_sems.at[j]))
                copies.append(pltpu.make_async_copy(wca_hbm.at[j], wca_s.at[:, cols], proj_sems.at[N_DEV + j]))
            copies.append(pltpu.make_async_copy(wproj_hbm, wproj_s, proj_sems.at[2 * N_DEV]))
            for cp in copies[N_DEV + 1:]:
                cp.start()
            for ref in (loss_ref, dg2_ref, db2_ref, dg1_ref, db1_ref):
                ref[...] = jnp.zeros_like(ref)
            for cp in copies:
                cp.wait()

        merged = _merge(osb_ref[...], oca_ref[...], lg_ref[...], bg_ref[...], wsb_s[...], wca_s[...])[4]
        u1 = ALPHA * x_ref[...] + _dot(merged.astype(BF16), wproj_s[...])
        xhat1, rstd1 = _ln_stats(u1)
        x1 = xhat1 * g1_ref[...] + b1_ref[...]
        x1b = x1.astype(BF16)
        x1b_ref[...] = x1b
        for k in range(N_DEV):
            cols = slice(FF_SHARD * k, FF_SHARD * (k + 1))
            r = jnp.maximum(_dot(x1b, win_s[:, cols]), 0.0)
            relu_s[:, cols] = r
            a_ref[:, cols] = (r * r).astype(BF16)
        xhat2, rstd2 = _ln_stats(ALPHA * x1 + _dot(a_ref[...], wout_s[...]))
        diff = xhat2 * g2_ref[...] + b2_ref[...] - t_ref[...]
        per_token = jnp.mean(diff * diff, axis=1, keepdims=True)
        loss_ref[...] += 0.5 * jnp.sum(per_token, axis=0, keepdims=True)
        dy = diff * (1.0 / D_MODEL)
        dg2_ref[...] += jnp.sum(dy * xhat2, axis=0, keepdims=True)
        db2_ref[...] += jnp.sum(dy, axis=0, keepdims=True)
        du2 = _ln_bwd(dy, xhat2, rstd2, g2_ref[...])
        du2b = du2.astype(BF16)
        du2b_ref[...] = du2b
        for k in range(N_DEV):
            cols = slice(FF_SHARD * k, FF_SHARD * (k + 1))
            da = _dot_nt(du2b, wout_s[cols, :])
            dhm_ref[:, cols] = (2.0 * relu_s[:, cols] * da).astype(BF16)
        dx1 = ALPHA * du2 + _dot_nt(dhm_ref[...], win_s[...])
        dg1_ref[...] += jnp.sum(dx1 * xhat1, axis=0, keepdims=True)
        db1_ref[...] += jnp.sum(dx1, axis=0, keepdims=True)
        du1_ref[...] = _ln_bwd(dx1, xhat1, rstd1, g1_ref[...])

    rows = pl.BlockSpec((tm, D_MODEL), lambda i: (i, 0))
    vec = pl.BlockSpec((1, D_MODEL), lambda i: (0, 0))
    ff = pl.BlockSpec((tm, D_FF), lambda i: (i, 0))
    wide = jax.ShapeDtypeStruct((s, D_FF), BF16)
    narrow = jax.ShapeDtypeStruct((s, D_MODEL), BF16)
    stat = jax.ShapeDtypeStruct((1, D_MODEL), F32)
    tile = lambda w: pl.BlockSpec((tm, w), lambda i: (i, 0))
    return pl.pallas_call(
        body, name="mlp", grid=(s // tm,),
        in_specs=[rows, tile(ATT_WIDTH), tile(ATT_WIDTH), tile(GATE_COLS), pl.BlockSpec((1, GATE_COLS), lambda i: (0, 0)),
                  HBM_SPEC, HBM_SPEC, HBM_SPEC, rows, HBM_SPEC, HBM_SPEC, vec, vec, vec, vec],
        out_specs=[ff, rows, ff, rows, rows, pl.BlockSpec((1, 1), lambda i: (0, 0)), vec, vec, vec, vec],
        out_shape=[wide, narrow, wide, narrow, jax.ShapeDtypeStruct((s, D_MODEL), F32),
                   jax.ShapeDtypeStruct((1, 1), F32), stat, stat, stat, stat],
        scratch_shapes=MLP_WEIGHT_SCRATCH + [
            pltpu.VMEM((tm, D_FF), F32), pltpu.VMEM((ATT_WIDTH, D_MODEL), BF16), pltpu.VMEM((ATT_WIDTH, D_MODEL), BF16),
            pltpu.VMEM((D_MODEL, D_MODEL), BF16), pltpu.SemaphoreType.DMA((2 * N_DEV + 1,))],
        compiler_params=_params(("arbitrary",)),
    )(x, o_sb, o_ca, lg, _row(b_gate), wg_sb, wg_ca, w_proj, target, wg_in, w_out,
      _row(ln1_g), _row(ln1_b), _row(ln2_g), _row(ln2_b))


def dh_slab(pieces, tm=512):
    s = pieces[0].shape[0]
    widths = [p.shape[1] for p in pieces]
    assert sum(widths) == IN_COLS
    n = len(pieces)

    def body(*refs):
        dh_ref = refs[n]
        off = 0
        for ref, w in zip(refs[:n], widths):
            dh_ref[:, off:off + w] = ref[...].astype(BF16)
            off += w

    rows = lambda w: pl.BlockSpec((tm, w), lambda i: (i, 0))
    return pl.pallas_call(
        body, name="dh_slab", grid=(s // tm,),
        in_specs=[rows(w) for w in widths], out_specs=rows(IN_COLS),
        out_shape=jax.ShapeDtypeStruct((s, IN_COLS), BF16),
        compiler_params=_params(("arbitrary",)),
    )(*pieces)


def in_bwd(du1, dh, wg_in, tm=512, guest=None):
    s = du1.shape[0]

    def body(du_ref, dh_ref, w_hbm, gx_ref, pairs_s, sems):
        @pl.when(pl.program_id(0) == 0)
        def _():
            _load_in_pairs(w_hbm, pairs_s, sems)

        acc = ALPHA * du_ref[...]
        for j in range(N_DEV // 2):
            acc = acc + _dot_nt(dh_ref[:, IN_PAIR * j:IN_PAIR * (j + 1)], pairs_s[j])
        gx_ref[...] = acc

    rows = lambda w: pl.BlockSpec((tm, w), lambda i: (i, 0))
    return _pallas(
        body, guest, (du1, dh, wg_in), name="in_bwd", grid=(s // tm,), scratch_shapes=IN_PAIR_SCRATCH,
        in_specs=[rows(D_MODEL), rows(IN_COLS), HBM_SPEC],
        out_specs=[rows(D_MODEL)],
        out_shape=[jax.ShapeDtypeStruct((s, D_MODEL), F32)],
        compiler_params=_params(("arbitrary",)),
    )


def wgrad(name, a, b, tm, tn, shard_cols=None, tk=2048):
    kk, m = a.shape
    n = b.shape[1]
    while kk % tk:
        tk //= 2
    nk = kk // tk
    w = shard_cols

    def body(a_ref, b_ref, out_ref, acc_s):
        k = pl.program_id(2)

        @pl.when(k == 0)
        def _():
            acc_s[...] = jnp.zeros_like(acc_s)

        acc_s[...] += _dot_tn(a_ref[...].astype(BF16), b_ref[...].astype(BF16))

        @pl.when(k == nk - 1)
        def _():
            if w is None:
                out_ref[...] = acc_s[...]
            else:
                for j in range(tn // w):
                    out_ref[j] = acc_s[:, w * j:w * (j + 1)]

    if w is None:
        out_spec = pl.BlockSpec((tm, tn), lambda i, j, k: (i, j))
        out_shape = jax.ShapeDtypeStruct((m, n), F32)
    else:
        out_spec = pl.BlockSpec((tn // w, tm, w), lambda i, j, k: (j, i, 0))
        out_shape = jax.ShapeDtypeStruct((n // w, m, w), F32)
    return pl.pallas_call(
        body, name=name, grid=(m // tm, n // tn, nk),
        in_specs=[pl.BlockSpec((tk, tm), lambda i, j, k: (k, i)),
                  pl.BlockSpec((tk, tn), lambda i, j, k: (k, j))],
        out_specs=out_spec, out_shape=out_shape,
        scratch_shapes=[pltpu.VMEM((tm, tn), F32)],
        compiler_params=_params(("arbitrary", "arbitrary", "arbitrary")),
    )(a, b)


def adamw(name, w, m, v, parts, tm=None, guest=None):
    rows, cols = w.shape
    tm = rows if tm is None else min(tm, rows)
    n = len(parts)
    c1 = 1.0 - ADAM_B1 ** ADAM_STEP
    c2 = 1.0 - ADAM_B2 ** ADAM_STEP

    def body(*refs):
        w_ref, m_ref, v_ref = refs[:3]
        part_refs = refs[3:3 + n]
        g_ref, d_ref, nm_ref, nv_ref = refs[3 + n:]
        g = None
        for ref, (_, index) in zip(part_refs, parts):
            term = (ref[...] if index is None else ref[0]).astype(F32)
            g = term if g is None else g + term
        new_m = ADAM_B1 * m_ref[...] + (1.0 - ADAM_B1) * g
        new_v = ADAM_B2 * v_ref[...] + (1.0 - ADAM_B2) * (g * g)
        m_hat = new_m / c1
        v_hat = new_v / c2
        g_ref[...] = g
        d_ref[...] = -ADAM_LR * (m_hat / (jnp.sqrt(v_hat) + ADAM_EPS) + ADAM_WD * w_ref[...])
        nm_ref[...] = new_m
        nv_ref[...] = new_v

    spec = pl.BlockSpec((tm, cols), lambda i: (i, 0))
    shape = jax.ShapeDtypeStruct((rows, cols), F32)

    def part_spec(index):
        if index is None:
            return spec
        return pl.BlockSpec((1, tm, cols), lambda i: (index, i, 0))

    return _pallas(
        body, guest, (w, m, v, *[array for array, _ in parts]), name=name, grid=(rows // tm,),
        in_specs=[spec] * 3 + [part_spec(index) for _, index in parts],
        out_specs=[spec] * 4, out_shape=[shape] * 4,
        compiler_params=_params(("arbitrary",)),
    )


def _place():
    x, y, c = lax.axis_index("x"), lax.axis_index("y"), lax.axis_index("c")
    other_chips = [(1 - x, y), (x, 1 - y), (1 - x, 1 - y)]
    return x, y, c, other_chips


def place_shards(arrays):
    n = len(arrays)

    def body(*refs):
        ins, outs, stage, sems = refs[:n], refs[n:2 * n], refs[2 * n:3 * n], refs[3 * n]
        x, y, c, _ = _place()
        copies = []
        for a in range(n):
            stage[a][...] = ins[a][...].astype(BF16)
            copies.append(pltpu.make_async_copy(stage[a], outs[a].at[4 * x + 2 * y + c], sems.at[a]))
            copies[-1].start()
        for cp in copies:
            cp.wait()

    return pl.pallas_call(
        body, name="place_shards",
        in_specs=[pl.BlockSpec(memory_space=pltpu.VMEM)] * n, out_specs=[HBM_SPEC] * n,
        out_shape=[jax.ShapeDtypeStruct((N_DEV,) + a.shape, BF16) for a in arrays],
        scratch_shapes=[pltpu.VMEM(a.shape, BF16) for a in arrays] + [pltpu.SemaphoreType.DMA((n,))],
        compiler_params=_params(),
    )(*arrays)


def all_gather(name, arrays):
    n = len(arrays)

    def body(*refs):
        ins, outs = refs[:n], refs[n:2 * n]
        send_sems, recv_sems, local_sems = refs[2 * n:]
        x, y, c, chips = _place()
        me = 4 * x + 2 * y + c
        sibling = (x, y, 1 - c)

        def copy(a, k, block, to, src=None):
            return pltpu.make_async_remote_copy(
                src_ref=outs[a].at[block] if src is None else src, dst_ref=outs[a].at[block],
                send_sem=send_sems.at[a, k], recv_sem=recv_sems.at[a, k],
                device_id=to, device_id_type=MESH)

        started = []
        local = [pltpu.make_async_copy(ins[a], outs[a].at[me], local_sems.at[a]) for a in range(n)]
        for a in range(n):
            local[a].start()
            sends = [copy(a, 0, me, sibling, src=ins[a])]
            sends += [copy(a, 1 + j, me, (cx, cy, c), src=ins[a]) for j, (cx, cy) in enumerate(chips)]
            for cp in sends:
                cp.start()
            started += sends
        for a in range(n):
            for j, (cx, cy) in enumerate(chips):
                block = 4 * cx + 2 * cy + c
                copy(a, 1 + j, block, sibling).wait_recv()
                passed = copy(a, 4 + j, block, sibling)
                passed.start()
                started.append(passed)
        for a in range(n):
            copy(a, 0, 4 * x + 2 * y + (1 - c), sibling).wait_recv()
            for j, (cx, cy) in enumerate(chips):
                copy(a, 4 + j, 4 * cx + 2 * cy + (1 - c), sibling).wait_recv()
        for cp in started:
            cp.wait_send()
        for cp in local:
            cp.wait()

    return pl.pallas_call(
        body, name=name,
        in_specs=[pl.BlockSpec(memory_space=pltpu.VMEM)] * n, out_specs=[HBM_SPEC] * n,
        out_shape=[jax.ShapeDtypeStruct((N_DEV,) + a.shape, a.dtype) for a in arrays],
        scratch_shapes=[pltpu.SemaphoreType.DMA((n, 7)), pltpu.SemaphoreType.DMA((n, 7)),
                        pltpu.SemaphoreType.DMA((n,))],
    )(*arrays)


def _gather_first_plan(n):
    def plan(zones):
        x, y, c, chips = _place()
        me = 4 * x + 2 * y + c
        copies = []
        for zone in zones[:n]:
            copies.append((zone.at[me], zone.at[me], (x, y, 1 - c), zone.at[4 * x + 2 * y + (1 - c)]))
            copies += [(zone.at[me], zone.at[me], (cx, cy, c), zone.at[4 * cx + 2 * cy + c]) for cx, cy in chips]
        return copies
    return plan


def _gather_second_plan(n):
    def plan(zones):
        x, y, c, chips = _place()
        return [(zone.at[4 * cx + 2 * cy + c], zone.at[4 * cx + 2 * cy + c], (x, y, 1 - c),
                 zone.at[4 * cx + 2 * cy + (1 - c)]) for zone in zones[:n] for cx, cy in chips]
    return plan


def _both_plans(first, n_first, second):
    return lambda bufs: first(bufs[:n_first]) + second(bufs[n_first:])


def _sibling_plan(n):
    def plan(bufs):
        x, y, c, _ = _place()
        return [(bufs[a].at[:, 1 - c], bufs[n + a], (x, y, 1 - c), bufs[n + a]) for a in range(n)]
    return plan


def _chips_plan(n):
    def plan(bufs):
        _, _, c, chips = _place()
        return [(bufs[a].at[2 * cx + cy], bufs[n + a].at[j], (cx, cy, c), bufs[n + a].at[j])
                for a in range(n) for j, (cx, cy) in enumerate(chips)]
    return plan


def chip_sum(name, grad, landed, place, tr=256):
    _, _, rows, cols = grad.shape
    tr = min(tr, rows)

    def body(place_ref, g_ref, l_ref, own_ref, out_ref):
        total = g_ref[0, 0] + l_ref[0]
        out_ref[0] = total.astype(BF16)

        @pl.when(pl.program_id(1) == place_ref[1])
        def _():
            own_ref[...] = total

    return pl.pallas_call(
        body, name=name,
        grid_spec=pltpu.PrefetchScalarGridSpec(
            num_scalar_prefetch=1, grid=(rows // tr, 4),
            in_specs=[pl.BlockSpec((1, 1, tr, cols), lambda i, ch, pos: (ch, pos[0], i, 0)),
                      pl.BlockSpec((1, tr, cols), lambda i, ch, pos: (ch, i, 0))],
            out_specs=[pl.BlockSpec((tr, cols), lambda i, ch, pos: (i, 0)),
                       pl.BlockSpec((1, tr, cols), lambda i, ch, pos: (ch, i, 0))]),
        out_shape=[jax.ShapeDtypeStruct((rows, cols), F32), jax.ShapeDtypeStruct((4, rows, cols), BF16)],
        compiler_params=_params(("arbitrary", "arbitrary")),
    )(place, grad, landed)


BIG =("w_in", "w_sb_proj", "w_ca_proj", "w_out", "w_mlp_in", "w_mlp_out")
SMALL = ("b_gate", "rel_bias", "ln1_g", "ln1_b", "ln2_g", "ln2_b")
NAMES = ("w_in", "b_gate", "w_sb_proj", "w_ca_proj", "rel_bias", "w_out", "ln1_g", "ln1_b",
         "w_mlp_in", "w_mlp_out", "ln2_g", "ln2_b")
REL_COLS = 2 * REL_CLIP + 1


def _pack_small(t, scalar=None):
    rel = t["rel_bias"]
    if rel.shape[1] != REL_PAD:
        rel = jnp.pad(rel, ((0, 0), (0, REL_PAD - rel.shape[1])))
    last = jnp.zeros((128,), F32) if scalar is None else jnp.pad(scalar.reshape(1), (0, 127))
    flat = [t["b_gate"].reshape(-1), rel.reshape(-1)] + [t[n].reshape(-1) for n in SMALL[2:]] + [last]
    return jnp.concatenate(flat).reshape(-1, 128)


def _unpack_small(p):
    flat = p.reshape(-1)
    out, off = {}, 0
    for n, size in (("b_gate", GATE_COLS), ("rel_bias", 8 * REL_PAD)) + tuple((n, D_MODEL) for n in SMALL[2:]):
        out[n] = flat[off:off + size]
        off += size
    out["rel_bias"] = out["rel_bias"].reshape(8, REL_PAD)[:, :REL_COLS]
    return out


def kernel(x, w_in, b_gate, w_sb_proj, w_ca_proj, rel_bias, w_out, ln1_g, ln1_b, w_mlp_in, w_mlp_out, ln2_g, ln2_b, loss_target, m_w_in, m_b_gate, m_w_sb_proj, m_w_ca_proj, m_rel_bias, m_w_out, m_ln1_g, m_ln1_b, m_w_mlp_in, m_w_mlp_out, m_ln2_g, m_ln2_b, v_w_in, v_b_gate, v_w_sb_proj, v_w_ca_proj, v_rel_bias, v_w_out, v_ln1_g, v_ln1_b, v_w_mlp_in, v_w_mlp_out, v_ln2_g, v_ln2_b):
    w = dict(w_in=w_in, b_gate=b_gate, w_sb_proj=w_sb_proj, w_ca_proj=w_ca_proj, rel_bias=rel_bias, w_out=w_out,
             ln1_g=ln1_g, ln1_b=ln1_b, w_mlp_in=w_mlp_in, w_mlp_out=w_mlp_out, ln2_g=ln2_g, ln2_b=ln2_b)
    m = dict(w_in=m_w_in, b_gate=m_b_gate, w_sb_proj=m_w_sb_proj, w_ca_proj=m_w_ca_proj, rel_bias=m_rel_bias,
             w_out=m_w_out, ln1_g=m_ln1_g, ln1_b=m_ln1_b, w_mlp_in=m_w_mlp_in, w_mlp_out=m_w_mlp_out,
             ln2_g=m_ln2_g, ln2_b=m_ln2_b)
    v = dict(w_in=v_w_in, b_gate=v_b_gate, w_sb_proj=v_w_sb_proj, w_ca_proj=v_w_ca_proj, rel_bias=v_rel_bias,
             w_out=v_w_out, ln1_g=v_ln1_g, ln1_b=v_ln1_b, w_mlp_in=v_w_mlp_in, w_mlp_out=v_w_mlp_out,
             ln2_g=v_ln2_g, ln2_b=v_ln2_b)

    x, target = x[0], loss_target[0]
    c = lax.axis_index("c")
    place = jnp.stack([c, 2 * lax.axis_index("x") + lax.axis_index("y")]).astype(jnp.int32)
    grads, deltas, new_m, new_v = {}, {}, {}, {}

    def by_owner(n, g):
        return g.reshape((4, 2) + w[n].shape)

    def finish(n, own, arrived, guest=None):
        out = adamw("adamw_" + n, w[n], m[n], v[n], [(own, None), (arrived, 0), (arrived, 1), (arrived, 2)],
                    tm=256, guest=guest)
        out, bufs = (out, None) if guest is None else out
        grads[n], deltas[n], new_m[n], new_v[n] = out
        return bufs

    def siblings(names, g):
        bufs = [by_owner(n, g[n]) for n in names] + [lax.empty((4,) + w[n].shape, F32) for n in names]
        return Exchange(tuple(bufs), _sibling_plan(len(names)), len(names))

    def chip_sums(names, bufs):
        k = len(names)
        own, sums = zip(*[chip_sum("chip_sum_" + n, bufs[a], bufs[k + a], place) for a, n in enumerate(names)])
        zones = [lax.empty((3,) + w[n].shape, BF16) for n in names]
        return own, Exchange(tuple(sums) + tuple(zones), _chips_plan(k), 3 * k)

    proj, ffn = ("w_sb_proj", "w_ca_proj", "w_out"), ("w_mlp_in", "w_mlp_out")
    zones = place_shards([w[n] for n in proj + ffn])
    proj_zones, ffn_zones = tuple(zones[:len(proj)]), tuple(zones[len(proj):])
    wg_in = all_gather("gather_w_in", [w_in.astype(BF16)])[0]
    (qkv, lg, xb), proj_zones = in_proj(
        x, wg_in, guest=Exchange(proj_zones, _gather_first_plan(len(proj)), 4 * len(proj)))
    bias_tiles = ca_bias_tiles(rel_bias)
    both = Exchange(ffn_zones + tuple(proj_zones),
                    _both_plans(_gather_first_plan(len(ffn)), len(ffn), _gather_second_plan(len(proj))),
                    4 * len(ffn) + 3 * len(proj))
    (o_sb,), zones = sb_fwd(qkv, both)
    ffn_zones, proj_zones = tuple(zones[:len(ffn)]), zones[len(ffn):]
    (o_ca,), ffn_zones = ca_fwd(qkv, bias_tiles, Exchange(ffn_zones, _gather_second_plan(len(ffn)), 3 * len(ffn)))
    wg = dict(zip(proj + ffn, list(proj_zones) + list(ffn_zones)))
    wg_sb, wg_ca, wg_mi = wg["w_sb_proj"], wg["w_ca_proj"], wg["w_mlp_in"]
    wg_mo = wg["w_mlp_out"].reshape(D_FF, D_MODEL)
    w_out_b = wg["w_out"].reshape(D_MODEL, D_MODEL)

    a, x1b, dhm, du2b, du1, loss, dg2, db2, dg1, db1 = mlp_block(
        x, o_sb, o_ca, lg, b_gate, wg_sb, wg_ca, w_out_b, target, wg_mi, wg_mo, ln1_g, ln1_b, ln2_g, ln2_b)
    g = {}
    g["w_mlp_in"] = wgrad("wgrad_mlp_in", x1b, dhm, tm=D_MODEL, tn=FF_SHARD, shard_cols=FF_SHARD)
    g["w_mlp_out"] = wgrad("wgrad_mlp_out", a, du2b, tm=FF_SHARD, tn=D_MODEL)

    mlp = ("w_mlp_in", "w_mlp_out")
    (dlg, do_sb, do_ca, dbg, g["w_out"], g["w_sb_proj"], g["w_ca_proj"]), bufs = mix_bwd(
        du1, o_sb, o_ca, lg, b_gate, wg_sb, wg_ca, w_out_b, guest=siblings(mlp, g))
    own_mlp, mlp_chips = chip_sums(mlp, bufs)

    att = ("w_out", "w_sb_proj", "w_ca_proj")
    att_siblings = siblings(att, g)
    both = Exchange(mlp_chips.bufs + att_siblings.bufs,
                    _both_plans(mlp_chips.plan, len(mlp_chips.bufs), att_siblings.plan),
                    mlp_chips.copies + att_siblings.copies)
    (dq_sb, dk_sb, dv_sb), bufs = sb_bwd(qkv, o_sb, do_sb, guest=both)
    arrived_mlp, bufs = bufs[len(mlp):len(mlp_chips.bufs)], bufs[len(mlp_chips.bufs):]
    own_att, att_chips = chip_sums(att, bufs)
    (dq_ca, dk_ca, dv_ca, db_tiles), bufs = ca_bwd(qkv, bias_tiles, do_ca, guest=att_chips)
    arrived_att = bufs[len(att):]
    g_rel = rel_bias_grad(db_tiles)

    dh = dh_slab([dq_sb, dk_sb, dv_sb, dq_ca, dk_ca, dv_ca, dlg])
    g["w_in"] = wgrad("wgrad_in", xb, dh, tm=D_MODEL, tn=2 * IN_SHARD, shard_cols=IN_SHARD)
    last = ("w_in",)
    bufs = finish(mlp[0], own_mlp[0], arrived_mlp[0], guest=siblings(last, g))
    finish(mlp[1], own_mlp[1], arrived_mlp[1])
    for n, own, arrived in zip(att, own_att, arrived_att):
        finish(n, own, arrived)
    own_in, in_chips = chip_sums(last, bufs)
    (grad_x,), bufs = in_bwd(du1, dh, wg_in, guest=in_chips)
    finish(last[0], own_in[0], bufs[1])

    small = dict(b_gate=dbg, rel_bias=g_rel, ln1_g=dg1, ln1_b=db1, ln2_g=dg2, ln2_b=db2)
    parts = all_gather("gather_small_grads", [_pack_small(small, loss[0, 0])])[0]
    packed = adamw("adamw_small", _pack_small(w), _pack_small(m), _pack_small(v),
                   [(parts, d) for d in range(N_DEV)])
    loss = packed[0][-1, 0]
    for out, p in zip((grads, deltas, new_m, new_v), packed):
        for n, val in _unpack_small(p).items():
            out[n] = val.reshape(w[n].shape)

    return (loss, grad_x[None], *[grads[n] for n in NAMES], *[deltas[n] for n in NAMES],
            *[new_m[n] for n in NAMES], *[new_v[n] for n in NAMES])
```

```python
import functools
from typing import NamedTuple

import jax
import jax.numpy as jnp
from jax import lax
from jax.experimental import pallas as pl
from jax.experimental.pallas import tpu as pltpu

F32 = jnp.float32
BF16 = jnp.bfloat16
MESH = pl.DeviceIdType.MESH

N_DEV = 8
D_MODEL = 1024
HEAD_DIM = 64
ATT_WIDTH = 512
N_PAIRS = ATT_WIDTH // 128
QKV_COLS = 6 * ATT_WIDTH
GATE_COLS = 2 * D_MODEL
IN_COLS = QKV_COLS + GATE_COLS
IN_SHARD = IN_COLS // N_DEV
D_FF = 4 * D_MODEL
FF_SHARD = D_FF // N_DEV
PROJ_SHARD = D_MODEL // N_DEV
ATT_BLOCK = 128
CA_TILES = 5
CA_ROWS = 1024
CHUNK = 64
CA_PREV_CHUNKS = 8
REL_CLIP = 256
REL_PAD = 640
ALPHA = 2.0 ** 0.25
LN_EPS = 1e-5
QK_SCALE = HEAD_DIM ** -0.5
NEG_BIG = -1e30
VMEM_LIMIT = 56 * 1024 * 1024

ADAM_LR = 0.001
ADAM_B1 = 0.9
ADAM_B2 = 0.999
ADAM_EPS = 1e-08
ADAM_WD = 0.01
ADAM_STEP = 10

_NT = (((1,), (1,)), ((), ()))
_TN = (((0,), (0,)), ((), ()))


def _dot(a, b):
    return jnp.dot(a, b, preferred_element_type=F32)


def _dot_nt(a, b):
    return lax.dot_general(a, b, _NT, preferred_element_type=F32)


def _dot_tn(a, b):
    return lax.dot_general(a, b, _TN, preferred_element_type=F32)


def _params(semantics=None):
    return pltpu.CompilerParams(dimension_semantics=semantics, vmem_limit_bytes=VMEM_LIMIT)


def _row(v):
    return v.reshape(1, -1)


HBM_SPEC = pl.BlockSpec(memory_space=pl.ANY)


class Exchange(NamedTuple):
    bufs: tuple
    plan: object
    copies: int


def _pallas(body, guest, args, *, grid, in_specs, out_specs, out_shape, scratch_shapes=(), **kw):
    out_specs, out_shape, scratch_shapes = list(out_specs), list(out_shape), list(scratch_shapes)
    if guest is None:
        return pl.pallas_call(body, grid=grid, in_specs=in_specs, out_specs=out_specs, out_shape=out_shape,
                              scratch_shapes=scratch_shapes, **kw)(*args)
    nb, n_own = len(guest.bufs), len(in_specs) + len(out_specs)

    def hosting(*refs):
        zones = refs[nb + n_own:2 * nb + n_own]
        send_sems, recv_sems = refs[-2], refs[-1]
        steps = [pl.program_id(axis) for axis in range(len(grid))]
        first = functools.reduce(jnp.logical_and, [step == 0 for step in steps])
        last = functools.reduce(jnp.logical_and, [step == n - 1 for step, n in zip(steps, grid)])

        def copy(k, src, dst, peer):
            return pltpu.make_async_remote_copy(src_ref=src, dst_ref=dst, send_sem=send_sems.at[k],
                                                recv_sem=recv_sems.at[k], device_id=peer, device_id_type=MESH)

        @pl.when(first)
        def _():
            for k, (src, dst, peer, _) in enumerate(guest.plan(zones)):
                copy(k, src, dst, peer).start()

        body(*refs[nb:nb + n_own], *refs[2 * nb + n_own:-2])

        @pl.when(last)
        def _():
            for k, (src, dst, peer, landed) in enumerate(guest.plan(zones)):
                copy(k, src, dst, peer).wait_send()
                copy(k, src, landed, peer).wait_recv()

    sems = pltpu.SemaphoreType.DMA((guest.copies,))
    results = pl.pallas_call(
        hosting, grid=grid, in_specs=[HBM_SPEC] * nb + list(in_specs), out_specs=out_specs + [HBM_SPEC] * nb,
        out_shape=out_shape + [jax.ShapeDtypeStruct(b.shape, b.dtype) for b in guest.bufs],
        input_output_aliases={k: len(out_specs) + k for k in range(nb)},
        scratch_shapes=scratch_shapes + [sems, sems], **kw)(*guest.bufs, *args)
    return results[:len(out_specs)], results[len(out_specs):]


IN_PAIR = 2 * IN_SHARD
IN_PAIR_SCRATCH = [pltpu.VMEM((N_DEV // 2, D_MODEL, IN_PAIR), BF16), pltpu.SemaphoreType.DMA((N_DEV,))]


def _load_in_pairs(w_hbm, pairs_s, sems):
    copies = [pltpu.make_async_copy(w_hbm.at[j], pairs_s.at[j // 2, :, pl.ds(IN_SHARD * (j % 2), IN_SHARD)],
                                    sems.at[j]) for j in range(N_DEV)]
    for cp in copies:
        cp.start()
    for cp in copies:
        cp.wait()


def in_proj(x, wg_in, tm=512, guest=None):
    s = x.shape[0]

    def body(x_ref, w_hbm, qkv_ref, lg_ref, xb_ref, pairs_s, sems):
        @pl.when(pl.program_id(0) == 0)
        def _():
            _load_in_pairs(w_hbm, pairs_s, sems)

        xb = x_ref[...].astype(BF16)
        xb_ref[...] = xb
        for j in range(N_DEV // 2):
            acc = _dot(xb, pairs_s[j])
            lo, hi = IN_PAIR * j, IN_PAIR * (j + 1)
            if hi <= QKV_COLS:
                qkv_ref[:, lo:hi] = acc.astype(BF16)
            elif lo >= QKV_COLS:
                lg_ref[:, lo - QKV_COLS:hi - QKV_COLS] = acc
            else:
                qkv_ref[:, lo:QKV_COLS] = acc[:, :QKV_COLS - lo].astype(BF16)
                lg_ref[:, 0:hi - QKV_COLS] = acc[:, QKV_COLS - lo:]

    return _pallas(
        body, guest, (x, wg_in), name="in_proj", grid=(s // tm,), scratch_shapes=IN_PAIR_SCRATCH,
        in_specs=[pl.BlockSpec((tm, D_MODEL), lambda i: (i, 0)), HBM_SPEC],
        out_specs=[pl.BlockSpec((tm, QKV_COLS), lambda i: (i, 0)),
                   pl.BlockSpec((tm, GATE_COLS), lambda i: (i, 0)),
                   pl.BlockSpec((tm, D_MODEL), lambda i: (i, 0))],
        out_shape=[jax.ShapeDtypeStruct((s, QKV_COLS), BF16),
                   jax.ShapeDtypeStruct((s, GATE_COLS), F32),
                   jax.ShapeDtypeStruct((s, D_MODEL), BF16)],
        compiler_params=_params(("arbitrary",)),
    )


def _head_masks():
    lane = lax.broadcasted_iota(jnp.int32, (1, 128), 1)
    first = lane < HEAD_DIM
    return first, jnp.logical_not(first)


SB_CHUNK = 512
SB_SUB = SB_CHUNK // ATT_BLOCK


def _tri_pair():
    row = lax.broadcasted_iota(jnp.int32, (4 * ATT_BLOCK, 2 * ATT_BLOCK), 0) & (2 * ATT_BLOCK - 1)
    col = lax.broadcasted_iota(jnp.int32, (4 * ATT_BLOCK, 2 * ATT_BLOCK), 1)
    same = (row >= ATT_BLOCK) == (col >= ATT_BLOCK)
    return jnp.logical_and(same, row >= col).astype(BF16)


def _pair_bf16(pieces):
    return jnp.concatenate(
        [jnp.concatenate([p.astype(BF16) for p in row], axis=1) for row in pieces], axis=0)


def _suffix_sums(pieces, tri_pair):
    hi = [[p.astype(BF16) for p in row] for row in pieces]
    lo = [[p - h.astype(F32) for p, h in zip(row, hrow)] for row, hrow in zip(pieces, hi)]
    return _dot(jnp.concatenate([_pair_bf16(hi), _pair_bf16(lo)], axis=1), tri_pair)


def _first_row_piece(diag, j):
    return j if diag else 0


def _piece(v, r, h):
    return v[ATT_BLOCK * r:ATT_BLOCK * (r + 1), ATT_BLOCK * h:ATT_BLOCK * (h + 1)]


def _all_row_pieces(slabs):
    return [[slab[ATT_BLOCK * r:ATT_BLOCK * (r + 1)] for r in range(SB_SUB)] for slab in slabs]


def _add_rows(total, part, r0):
    if r0 == 0:
        return total + part
    return jnp.concatenate([total[:ATT_BLOCK * r0], total[ATT_BLOCK * r0:] + part], axis=0)


def _row_totals(csum, r0, h):
    return jnp.concatenate(
        [jnp.broadcast_to(_piece(csum, at, h)[:, 0:1], (ATT_BLOCK, ATT_BLOCK)) for at in range(SB_SUB - r0)], axis=0)


SIGN_BIT = 0x80000000


def _drop(z):
    minus_abs = lax.bitcast_convert_type(lax.bitcast_convert_type(z, jnp.uint32) | jnp.uint32(SIGN_BIT), F32)
    return jnp.maximum(z, 0.0) + jnp.log(1.0 + jnp.exp(minus_abs))


def _sb_weights(z, runs, r0, masked, strict, tri_pair):
    drops = []
    for r in range(r0, SB_SUB):
        row = [_drop(_piece(z, r - r0, h)) for h in range(2)]
        drops.append([jnp.where(strict, drop, 0.0) for drop in row] if r == masked else row)
    csum = _suffix_sums(drops, tri_pair)
    weights = []
    for r in range(r0, SB_SUB):
        row = [jnp.exp(_piece(z, r - r0, h) - (_piece(csum, r - r0, h) + runs[h][r])) for h in range(2)]
        weights.append([jnp.where(strict, a, 0.0) for a in row] if r == masked else row)
    return drops, weights, csum


EXP_UNDERFLOW = 104.0


SB_STEP = 2


def _sweep_left(span, i, dropped_ref):
    def more(at):
        return jnp.logical_and(at >= 0, jnp.min(dropped_ref[...]) < EXP_UNDERFLOW)

    def step(at):
        span(at * (SB_STEP * ATT_BLOCK), SB_STEP, False)
        return at - 1

    lax.while_loop(more, step, i * (SB_SUB // SB_STEP) - 1)


def _by_head(blk, masks):
    zero = jnp.zeros_like(blk)
    return jnp.concatenate([jnp.where(m, blk, zero) for m in masks], axis=0)


def sb_fwd(qkv, guest=None):
    s = qkv.shape[0]
    nq = s // SB_CHUNK

    def body(q_ref, k_ref, v_ref, o_ref, run_s):
        i = pl.program_id(1)
        masks = _head_masks()
        q = q_ref[...] * QK_SCALE
        row = lax.broadcasted_iota(jnp.int32, (ATT_BLOCK, ATT_BLOCK), 0)
        col = lax.broadcasted_iota(jnp.int32, (ATT_BLOCK, ATT_BLOCK), 1)
        strict = col < row
        tri_pair = _tri_pair()
        run_s[...] = jnp.zeros_like(run_s)
        o_ref[...] = jnp.zeros_like(o_ref)

        def span(start, pieces, diag):
            runs = [run_s[0], run_s[1]]
            acc = jnp.zeros((SB_CHUNK, ATT_BLOCK), F32)
            for j in reversed(range(pieces)):
                r0 = _first_row_piece(diag, j)
                keys = pl.ds(pl.multiple_of(start + ATT_BLOCK * j, ATT_BLOCK), ATT_BLOCK)
                z = _dot_nt(q[ATT_BLOCK * r0:], _by_head(k_ref[keys, :], masks))
                _, weights, csum = _sb_weights(z, _all_row_pieces(runs), r0, j if diag else None, strict, tri_pair)
                part = _dot(_pair_bf16(weights), _by_head(v_ref[keys, :], masks))
                acc = _add_rows(acc, part, r0)
                runs = [_add_rows(runs[h], _row_totals(csum, r0, h), r0) for h in range(2)]
            run_s[0], run_s[1] = runs
            o_ref[...] += acc

        span(i * SB_CHUNK, SB_SUB, True)
        _sweep_left(span, i, run_s)

    return _pallas(
        body, guest, (qkv, qkv, qkv), name="sb_fwd", grid=(N_PAIRS, nq),
        in_specs=[pl.BlockSpec((SB_CHUNK, 128), lambda p, i: (i, p)),
                  pl.BlockSpec((s, 128), lambda p, i: (0, N_PAIRS + p)),
                  pl.BlockSpec((s, 128), lambda p, i: (0, 2 * N_PAIRS + p))],
        out_specs=[pl.BlockSpec((SB_CHUNK, 128), lambda p, i: (i, p))],
        out_shape=[jax.ShapeDtypeStruct((s, ATT_WIDTH), F32)],
        scratch_shapes=[pltpu.VMEM((2, SB_CHUNK, 128), F32)],
        compiler_params=_params(("arbitrary", "arbitrary")),
    )


def sb_bwd(qkv, o_sb, do_sb, guest=None):
    s = qkv.shape[0]
    nq = s // SB_CHUNK

    def body(q_ref, k_ref, v_ref, o_ref, do_ref, dq_ref, dk_ref, dv_ref, run_l_s, run_e_s, etot_s, dq_s, dk_s, dv_s):
        i = pl.program_id(1)

        @pl.when(i == 0)
        def _():
            dk_s[...] = jnp.zeros_like(dk_s)
            dv_s[...] = jnp.zeros_like(dv_s)

        masks = _head_masks()
        q = q_ref[...] * QK_SCALE
        do = do_ref[...]
        prod = do.astype(F32) * o_ref[...]
        for h in range(2):
            total = jnp.sum(jnp.where(masks[h], prod, 0.0), axis=1, keepdims=True)
            etot_s[h] = jnp.broadcast_to(total, (SB_CHUNK, ATT_BLOCK))
        row = lax.broadcasted_iota(jnp.int32, (ATT_BLOCK, ATT_BLOCK), 0)
        col = lax.broadcasted_iota(jnp.int32, (ATT_BLOCK, ATT_BLOCK), 1)
        strict = col < row
        tri_pair = _tri_pair()
        run_l_s[...] = jnp.zeros_like(run_l_s)
        run_e_s[...] = jnp.zeros_like(run_e_s)
        dq_s[...] = jnp.zeros_like(dq_s)

        def span(start, pieces, diag):
            slabs_l = [run_l_s[0], run_l_s[1]]
            slabs_e = [run_e_s[0], run_e_s[1]]
            e_tot = _all_row_pieces([etot_s[0], etot_s[1]])
            dq = jnp.zeros((SB_CHUNK, ATT_BLOCK), F32)
            for j in reversed(range(pieces)):
                runs_l, runs_e = _all_row_pieces(slabs_l), _all_row_pieces(slabs_e)
                r0 = _first_row_piece(diag, j)
                masked = j if diag else None
                keys = pl.ds(pl.multiple_of(start + ATT_BLOCK * j, ATT_BLOCK), ATT_BLOCK)
                kk = _by_head(k_ref[keys, :], masks)
                vv = _by_head(v_ref[keys, :], masks)
                q_rows, do_rows = q[ATT_BLOCK * r0:], do[ATT_BLOCK * r0:]
                z = _dot_nt(q_rows, kk)
                da = _dot_nt(do_rows, vv)
                drops, weights, csum = _sb_weights(z, runs_l, r0, masked, strict, tri_pair)
                ab = [[a.astype(BF16) for a in wrow] for wrow in weights]
                es = [[a.astype(F32) * _piece(da, at, h) for h, a in enumerate(arow)] for at, arow in enumerate(ab)]
                esum = _suffix_sums(es, tri_pair)
                dzs = []
                for at, r in enumerate(range(r0, SB_SUB)):
                    dz_row = []
                    for h in range(2):
                        sig = jnp.exp(_piece(z, at, h) - drops[at][h])
                        after = _piece(esum, at, h) + runs_e[h][r] - es[at][h]
                        dz = es[at][h] - sig * (e_tot[h][r] - after)
                        dz_row.append(jnp.where(strict, dz, 0.0) if r == masked else dz)
                    dzs.append(dz_row)
                dz = _pair_bf16(dzs)
                dq = _add_rows(dq, _dot(dz, kk), r0)
                dk = _dot_tn(dz, q_rows)
                dv = _dot_tn(_pair_bf16(ab), do_rows)
                dk_s[keys, :] += jnp.where(masks[0], dk[:ATT_BLOCK], dk[ATT_BLOCK:])
                dv_s[keys, :] += jnp.where(masks[0], dv[:ATT_BLOCK], dv[ATT_BLOCK:])
                slabs_l =[_add_rows(slabs_l[h], _row_totals(csum, r0, h), r0) for h in range(2)]
                slabs_e = [_add_rows(slabs_e[h], _row_totals(esum, r0, h), r0) for h in range(2)]
            run_l_s[0], run_l_s[1] = slabs_l
            run_e_s[0], run_e_s[1] = slabs_e
            dq_s[...] += dq

        span(i * SB_CHUNK, SB_SUB, True)
        _sweep_left(span, i, run_l_s)
        dq_ref[...] = (dq_s[...] * QK_SCALE).astype(BF16)

        @pl.when(i == nq - 1)
        def _():
            dk_ref[...] = dk_s[...].astype(BF16)
            dv_ref[...] = dv_s[...].astype(BF16)

    tile = lambda p, i: (i, p)
    state = pltpu.VMEM((2, SB_CHUNK, 128), F32)
    sums = pltpu.VMEM((s, 128), F32)
    return _pallas(
        body, guest, (qkv, qkv, qkv, o_sb, do_sb), name="sb_bwd", grid=(N_PAIRS, nq),
        in_specs=[pl.BlockSpec((SB_CHUNK, 128), tile),
                  pl.BlockSpec((s, 128), lambda p, i: (0, N_PAIRS + p)),
                  pl.BlockSpec((s, 128), lambda p, i: (0, 2 * N_PAIRS + p)),
                  pl.BlockSpec((SB_CHUNK, 128), tile),
                  pl.BlockSpec((SB_CHUNK, 128), tile)],
        out_specs=[pl.BlockSpec((SB_CHUNK, 128), tile),
                   pl.BlockSpec((s, 128), lambda p, i: (0, p)),
                   pl.BlockSpec((s, 128), lambda p, i: (0, p))],
        out_shape=[jax.ShapeDtypeStruct((s, ATT_WIDTH), BF16)] * 3,
        scratch_shapes=[state, state, state, pltpu.VMEM((SB_CHUNK, 128), F32), sums, sums],
        compiler_params=_params(("arbitrary", "arbitrary")),
    )


def ca_bias_tiles(rel_bias):
    heads = rel_bias.shape[0]
    wide = 2 * ATT_BLOCK

    def body(rel_ref, out_ref):
        bins = lax.broadcasted_iota(jnp.int32, (REL_PAD, wide), 0)
        pos = lax.broadcasted_iota(jnp.int32, (REL_PAD, wide), 1)
        r = lax.broadcasted_iota(jnp.int32, (ATT_BLOCK, ATT_BLOCK), 0)
        c = lax.broadcasted_iota(jnp.int32, (ATT_BLOCK, ATT_BLOCK), 1)
        for j in range(CA_TILES):
            hot = (jnp.clip(ATT_BLOCK * (j + 1) - 1 - pos, -REL_CLIP, REL_CLIP) + REL_CLIP == bins).astype(F32)
            lines = jnp.dot(rel_ref[...], hot, preferred_element_type=F32, precision=lax.Precision.HIGHEST)
            back = 2 * j + (r >> 6) - (c >> 6)
            ok = jnp.logical_and(back >= 0, back <= CA_PREV_CHUNKS)
            for h in range(heads):
                tile = jnp.broadcast_to(lines[h:h + 1, :], (ATT_BLOCK, wide))
                tile = pltpu.roll(tile, wide - (ATT_BLOCK - 1), 1, stride=1, stride_axis=0)
                out_ref[h, j] = jnp.where(ok, tile[:, :ATT_BLOCK], NEG_BIG)

    padded = jnp.pad(rel_bias, ((0, 0), (0, REL_PAD - rel_bias.shape[1])))
    return pl.pallas_call(
        body, name="ca_bias_tiles",
        out_shape=jax.ShapeDtypeStruct((heads, CA_TILES, ATT_BLOCK, ATT_BLOCK), F32),
        compiler_params=_params(),
    )(padded)


CA_BLOCKS = CA_ROWS // ATT_BLOCK
CA_KEY_BLOCKS = CA_BLOCKS + CA_TILES - 1


def _ca_users(t):
    return [r for r in range(CA_BLOCKS) if 0 <= r + CA_TILES - 1 - t < CA_TILES]


def _ca_rows(v, t, r):
    at = _ca_users(t).index(r)
    return v[ATT_BLOCK * at:ATT_BLOCK * (at + 1)]


def _ca_window(step, k_ref, v_ref, masks):
    kks, vvs, inside, rows = [], [], [], []
    for t in range(CA_KEY_BLOCKS):
        block = step * CA_BLOCKS - (CA_TILES - 1) + t
        keys = pl.ds(pl.multiple_of(jnp.maximum(block, 0) * ATT_BLOCK, ATT_BLOCK), ATT_BLOCK)
        kks.append(_by_head(k_ref[keys, :], masks))
        vvs.append(_by_head(v_ref[keys, :], masks))
        inside.append(block >= 0)
        rows.append(keys)
    return kks, vvs, inside, rows


def _ca_probs(q, kks, inside, bias_ref):
    scores = [[[None] * CA_TILES for _ in range(2)] for _ in range(CA_BLOCKS)]
    for t in range(CA_KEY_BLOCKS):
        users = _ca_users(t)
        z = _dot_nt(q[ATT_BLOCK * users[0]:ATT_BLOCK * (users[-1] + 1)], kks[t])
        for r in users:
            j = r + CA_TILES - 1 - t
            for h in range(2):
                zz = _ca_rows(z, t, r)[:, ATT_BLOCK * h:ATT_BLOCK * (h + 1)] + bias_ref[h, j]
                scores[r][h][j] = jnp.where(inside[t], zz, NEG_BIG)
    probs = [[None, None] for _ in range(CA_BLOCKS)]
    for r in range(CA_BLOCKS):
        for h in range(2):
            m = jnp.max(functools.reduce(jnp.maximum, scores[r][h]), axis=1, keepdims=True)
            ex = [jnp.exp(z - m) for z in scores[r][h]]
            inv = 1.0 / jnp.sum(functools.reduce(jnp.add, ex), axis=1, keepdims=True)
            probs[r][h] = [e * inv for e in ex]
    return probs


def _ca_stack(tiles, t):
    return jnp.concatenate(
        [jnp.concatenate([tiles[r][h][r + CA_TILES - 1 - t].astype(BF16) for h in range(2)], axis=1)
         for r in _ca_users(t)], axis=0)


def ca_fwd(qkv, bias_tiles, guest=None):
    s = qkv.shape[0]
    nq = s // CA_ROWS

    def body(q_ref, k_ref, v_ref, bias_ref, o_ref):
        masks = _head_masks()
        kks, vvs, inside, _ = _ca_window(pl.program_id(1), k_ref, v_ref, masks)
        probs = _ca_probs(q_ref[...] * QK_SCALE, kks, inside, bias_ref)
        out = [None] * CA_BLOCKS
        for t in range(CA_KEY_BLOCKS):
            part = _dot(_ca_stack(probs, t), vvs[t])
            for r in _ca_users(t):
                piece = _ca_rows(part, t, r)
                out[r] = piece if out[r] is None else out[r] + piece
        o_ref[...] = jnp.concatenate(out, axis=0).astype(BF16)

    return _pallas(
        body, guest, (qkv, qkv, qkv, bias_tiles), name="ca_fwd", grid=(N_PAIRS, nq),
        in_specs=[pl.BlockSpec((CA_ROWS, 128), lambda p, i: (i, 3 * N_PAIRS + p)),
                  pl.BlockSpec((s, 128), lambda p, i: (0, 4 * N_PAIRS + p)),
                  pl.BlockSpec((s, 128), lambda p, i: (0, 5 * N_PAIRS + p)),
                  pl.BlockSpec((2, CA_TILES, ATT_BLOCK, ATT_BLOCK), lambda p, i: (p, 0, 0, 0))],
        out_specs=[pl.BlockSpec((CA_ROWS, 128), lambda p, i: (i, p))],
        out_shape=[jax.ShapeDtypeStruct((s, ATT_WIDTH), BF16)],
        compiler_params=_params(("arbitrary", "arbitrary")),
    )


def ca_bwd(qkv, bias_tiles, do_ca, guest=None):
    s = qkv.shape[0]
    nq = s // CA_ROWS

    def body(q_ref, k_ref, v_ref, bias_ref, do_ref, dq_ref, dk_ref, dv_ref, db_ref, dk_s, dv_s):
        big = pl.program_id(1)

        @pl.when(big == 0)
        def _():
            dk_s[...] = jnp.zeros_like(dk_s)
            dv_s[...] = jnp.zeros_like(dv_s)
            db_ref[...] = jnp.zeros_like(db_ref)

        masks = _head_masks()
        kks, vvs, inside, key_rows = _ca_window(big, k_ref, v_ref, masks)
        q = q_ref[...] * QK_SCALE
        do = do_ref[...]
        probs = _ca_probs(q, kks, inside, bias_ref)
        dps = [[[None] * CA_TILES for _ in range(2)] for _ in range(CA_BLOCKS)]
        for t in range(CA_KEY_BLOCKS):
            users = _ca_users(t)
            dp = _dot_nt(do[ATT_BLOCK * users[0]:ATT_BLOCK * (users[-1] + 1)], vvs[t])
            for r in users:
                for h in range(2):
                    dps[r][h][r + CA_TILES - 1 - t] = _ca_rows(dp, t, r)[:, ATT_BLOCK * h:ATT_BLOCK * (h + 1)]
        dss = [[None, None] for _ in range(CA_BLOCKS)]
        for r in range(CA_BLOCKS):
            for h in range(2):
                delta = jnp.sum(functools.reduce(jnp.add, [p * dp for p, dp in zip(probs[r][h], dps[r][h])]),
                                axis=1, keepdims=True)
                dss[r][h] = [p * (dp - delta) for p, dp in zip(probs[r][h], dps[r][h])]
        for h in range(2):
            for j in range(CA_TILES):
                db_ref[h, j] += functools.reduce(jnp.add, [dss[r][h][j] for r in range(CA_BLOCKS)])
        dq = [None] * CA_BLOCKS
        for t in range(CA_KEY_BLOCKS):
            users = _ca_users(t)
            rows = slice(ATT_BLOCK * users[0], ATT_BLOCK * (users[-1] + 1))
            ds = _ca_stack(dss, t)
            part = _dot(ds, kks[t])
            for r in users:
                piece = _ca_rows(part, t, r)
                dq[r] = piece if dq[r] is None else dq[r] + piece
            dk = _dot_tn(ds, q[rows])
            dv = _dot_tn(_ca_stack(probs, t), do[rows])
            dk_s[key_rows[t], :] += jnp.where(masks[0], dk[:ATT_BLOCK], dk[ATT_BLOCK:])
            dv_s[key_rows[t], :] += jnp.where(masks[0], dv[:ATT_BLOCK], dv[ATT_BLOCK:])
        dq_ref[...] = (jnp.concatenate(dq, axis=0) * QK_SCALE).astype(BF16)

        @pl.when(big == nq - 1)
        def _():
            dk_ref[...] = dk_s[...].astype(BF16)
            dv_ref[...] = dv_s[...].astype(BF16)

    return _pallas(
        body, guest, (qkv, qkv, qkv, bias_tiles, do_ca), name="ca_bwd", grid=(N_PAIRS, nq),
        in_specs=[pl.BlockSpec((CA_ROWS, 128), lambda p, i: (i, 3 * N_PAIRS + p)),
                  pl.BlockSpec((s, 128), lambda p, i: (0, 4 * N_PAIRS + p)),
                  pl.BlockSpec((s, 128), lambda p, i: (0, 5 * N_PAIRS + p)),
                  pl.BlockSpec((2, CA_TILES, ATT_BLOCK, ATT_BLOCK), lambda p, i: (p, 0, 0, 0)),
                  pl.BlockSpec((CA_ROWS, 128), lambda p, i: (i, p))],
        out_specs=[pl.BlockSpec((CA_ROWS, 128), lambda p, i: (i, p)),
                   pl.BlockSpec((s, 128), lambda p, i: (0, p)),
                   pl.BlockSpec((s, 128), lambda p, i: (0, p)),
                   pl.BlockSpec((2, CA_TILES, ATT_BLOCK, ATT_BLOCK), lambda p, i: (p, 0, 0, 0))],
        out_shape=[jax.ShapeDtypeStruct((s, ATT_WIDTH), BF16)] * 3
                  + [jax.ShapeDtypeStruct((2 * N_PAIRS, CA_TILES, ATT_BLOCK, ATT_BLOCK), F32)],
        scratch_shapes=[pltpu.VMEM((s, 128), F32), pltpu.VMEM((s, 128), F32)],
        compiler_params=_params(("arbitrary", "arbitrary")),
    )


def rel_bias_grad(db_tiles):
    heads = db_tiles.shape[0]

    def body(db_ref, out_ref):
        row = lax.broadcasted_iota(jnp.int32, (ATT_BLOCK, ATT_BLOCK), 0)
        col = lax.broadcasted_iota(jnp.int32, (ATT_BLOCK, ATT_BLOCK), 1)
        flip = (row + col == ATT_BLOCK - 1).astype(F32)
        wrapped = col < row
        cp = lax.broadcasted_iota(jnp.int32, (ATT_BLOCK, REL_PAD), 0)
        bins = lax.broadcasted_iota(jnp.int32, (ATT_BLOCK, REL_PAD), 1)
        total = jnp.zeros((heads, REL_PAD), F32)
        for j in range(CA_TILES):
            sums_neg, sums_pos = [], []
            for h in range(heads):
                tile = jnp.dot(db_ref[h, j], flip, preferred_element_type=F32, precision=lax.Precision.HIGHEST)
                tile = pltpu.roll(tile, 0, 1, stride=1, stride_axis=0)
                sums_neg.append(jnp.sum(jnp.where(wrapped, 0.0, tile), axis=0, keepdims=True))
                sums_pos.append(jnp.sum(jnp.where(wrapped, tile, 0.0), axis=0, keepdims=True))
            neg = jnp.concatenate(sums_neg, axis=0)
            pos = jnp.concatenate(sums_pos, axis=0)
            hot_neg = (jnp.clip(ATT_BLOCK * j + cp - (ATT_BLOCK - 1), -REL_CLIP, REL_CLIP) + REL_CLIP == bins)
            hot_pos = (jnp.clip(ATT_BLOCK * j + cp + 1, -REL_CLIP, REL_CLIP) + REL_CLIP == bins)
            total = total + jnp.dot(neg, hot_neg.astype(F32), preferred_element_type=F32,
                                    precision=lax.Precision.HIGHEST)
            total = total + jnp.dot(pos, hot_pos.astype(F32), preferred_element_type=F32,
                                    precision=lax.Precision.HIGHEST)
        out_ref[...] = total

    return pl.pallas_call(
        body, name="rel_bias_grad",
        out_shape=jax.ShapeDtypeStruct((heads, REL_PAD), F32),
        compiler_params=_params(),
    )(db_tiles)


def _assemble_cols(dst_ref, src_ref):
    w = src_ref.shape[2]
    for j in range(N_DEV):
        dst_ref[:, w * j:w * (j + 1)] = src_ref[j]


def _merge(o_sb, o_ca, lg, bg, wsb, wca):
    y_sb = _dot(o_sb.astype(BF16), wsb)
    y_ca = _dot(o_ca.astype(BF16), wca)
    gates = jax.nn.sigmoid(lg + bg)
    g_sb, g_ca = gates[:, :D_MODEL], gates[:, D_MODEL:]
    return y_sb, y_ca, g_sb, g_ca, g_sb * y_sb + g_ca * y_ca


def mix_bwd(du1, o_sb, o_ca, lg, b_gate, wg_sb, wg_ca, w_out, tm=256, guest=None):
    s = du1.shape[0]
    last = s // tm - 1

    def body(du_ref, osb_ref, oca_ref, lg_ref, bg_ref, wsb_ref, wca_ref, wout_ref,
             dlg_ref, dosb_ref, doca_ref, dbg_ref, gout_ref, gsb_ref, gca_ref, wsb_s, wca_s, gsb_s, gca_s):
        step = pl.program_id(0)

        @pl.when(step == 0)
        def _():
            _assemble_cols(wsb_s, wsb_ref)
            _assemble_cols(wca_s, wca_ref)
            for ref in (dbg_ref, gout_ref, gsb_s, gca_s):
                ref[...] = jnp.zeros_like(ref)

        o_sb, o_ca, du = osb_ref[...].astype(BF16), oca_ref[...].astype(BF16), du_ref[...].astype(BF16)
        y_sb, y_ca, g_sb, g_ca, merged = _merge(o_sb, o_ca, lg_ref[...], bg_ref[...], wsb_s[...], wca_s[...])
        dm = _dot_nt(du, wout_ref[...])
        dl_sb = dm * y_sb * g_sb * (1.0 - g_sb)
        dl_ca = dm * y_ca * g_ca * (1.0 - g_ca)
        dlg_ref[:, :D_MODEL] = dl_sb.astype(BF16)
        dlg_ref[:, D_MODEL:] = dl_ca.astype(BF16)
        dbg_ref[:, :D_MODEL] += jnp.sum(dl_sb, axis=0, keepdims=True)
        dbg_ref[:, D_MODEL:] += jnp.sum(dl_ca, axis=0, keepdims=True)
        dy_sb = (dm * g_sb).astype(BF16)
        dy_ca = (dm * g_ca).astype(BF16)
        dosb_ref[...] = _dot_nt(dy_sb, wsb_s[...]).astype(BF16)
        doca_ref[...] = _dot_nt(dy_ca, wca_s[...]).astype(BF16)
        gout_ref[...] += _dot_tn(merged.astype(BF16), du)
        gsb_s[...] += _dot_tn(o_sb, dy_sb)
        gca_s[...] += _dot_tn(o_ca, dy_ca)

        @pl.when(step == last)
        def _():
            for j in range(N_DEV):
                cols = slice(PROJ_SHARD * j, PROJ_SHARD * (j + 1))
                gsb_ref[j] = gsb_s[:, cols]
                gca_ref[j] = gca_s[:, cols]

    rows = lambda w: pl.BlockSpec((tm, w), lambda i: (i, 0))
    whole = lambda shape: pl.BlockSpec(shape, lambda i: (0,) * len(shape))
    stack = (N_DEV, ATT_WIDTH, PROJ_SHARD)
    return _pallas(
        body, guest, (du1, o_sb, o_ca, lg, _row(b_gate), wg_sb, wg_ca, w_out), name="mix_bwd", grid=(s // tm,),
        in_specs=[rows(D_MODEL), rows(ATT_WIDTH), rows(ATT_WIDTH), rows(GATE_COLS), whole((1, GATE_COLS)),
                  whole(stack), whole(stack), whole((D_MODEL, D_MODEL))],
        out_specs=[rows(GATE_COLS), rows(ATT_WIDTH), rows(ATT_WIDTH), whole((1, GATE_COLS)),
                   whole((D_MODEL, D_MODEL)), whole(stack), whole(stack)],
        out_shape=[jax.ShapeDtypeStruct((s, GATE_COLS), BF16),
                   jax.ShapeDtypeStruct((s, ATT_WIDTH), BF16), jax.ShapeDtypeStruct((s, ATT_WIDTH), BF16),
                   jax.ShapeDtypeStruct((1, GATE_COLS), F32), jax.ShapeDtypeStruct((D_MODEL, D_MODEL), F32),
                   jax.ShapeDtypeStruct(stack, F32), jax.ShapeDtypeStruct(stack, F32)],
        scratch_shapes=[pltpu.VMEM((ATT_WIDTH, D_MODEL), BF16), pltpu.VMEM((ATT_WIDTH, D_MODEL), BF16),
                        pltpu.VMEM((ATT_WIDTH, D_MODEL), F32), pltpu.VMEM((ATT_WIDTH, D_MODEL), F32)],
        compiler_params=_params(("arbitrary",)),
    )


def _ln_stats(u):
    mu = jnp.mean(u, axis=1, keepdims=True)
    cen = u - mu
    var = jnp.mean(cen * cen, axis=1, keepdims=True)
    rstd = lax.rsqrt(var + LN_EPS)
    return cen * rstd, rstd


def _ln_bwd(dy, xhat, rstd, gain):
    dyg = dy * gain
    m1 = jnp.mean(dyg, axis=1, keepdims=True)
    m2 = jnp.mean(dyg * xhat, axis=1, keepdims=True)
    return rstd * (dyg - m1 - xhat * m2)


def _load_mlp_weights(win_hbm, wout_hbm, win_s, wout_s, sems):
    copies = [pltpu.make_async_copy(win_hbm.at[k], win_s.at[:, pl.ds(FF_SHARD * k, FF_SHARD)], sems.at[k])
              for k in range(N_DEV)]
    copies.append(pltpu.make_async_copy(wout_hbm, wout_s, sems.at[N_DEV]))
    for cp in copies:
        cp.start()
    return copies


MLP_WEIGHT_SCRATCH = [pltpu.VMEM((D_MODEL, D_FF), BF16), pltpu.VMEM((D_FF, D_MODEL), BF16),
                      pltpu.SemaphoreType.DMA((N_DEV + 1,))]


def mlp_block(x, o_sb, o_ca, lg, b_gate, wg_sb, wg_ca, w_proj, target, wg_in, w_out,
              ln1_g, ln1_b, ln2_g, ln2_b, tm=256):
    s = x.shape[0]

    def body(x_ref, osb_ref, oca_ref, lg_ref, bg_ref, wsb_hbm, wca_hbm, wproj_hbm, t_ref, win_hbm, wout_hbm,
             g1_ref, b1_ref, g2_ref, b2_ref,
             a_ref, x1b_ref, dhm_ref, du2b_ref, du1_ref, loss_ref, dg2_ref, db2_ref, dg1_ref, db1_ref,
             win_s, wout_s, sems, relu_s, wsb_s, wca_s, wproj_s, proj_sems):
        @pl.when(pl.program_id(0) == 0)
        def _():
            copies = _load_mlp_weights(win_hbm, wout_hbm, win_s, wout_s, sems)
            for j in range(N_DEV):
                cols = pl.ds(PROJ_SHARD * j, PROJ_SHARD)
                copies.append(pltpu.make_async_copy(wsb_hbm.at[j], wsb_s.at[:, cols], proj_sems.at[j]))
                copies.append(pltpu.make_async_copy(wca_hbm.at[j], wca_s.at[:, cols], proj_sems.at[N_DEV + j]))
            copies.append(pltpu.make_async_copy(wproj_hbm, wproj_s, proj_sems.at[2 * N_DEV]))
            for cp in copies[N_DEV + 1:]:
                cp.start()
            for ref in (loss_ref, dg2_ref, db2_ref, dg1_ref, db1_ref):
                ref[...] = jnp.zeros_like(ref)
            for cp in copies:
                cp.wait()

        merged = _merge(osb_ref[...], oca_ref[...], lg_ref[...], bg_ref[...], wsb_s[...], wca_s[...])[4]
        u1 = ALPHA * x_ref[...] + _dot(merged.astype(BF16), wproj_s[...])
        xhat1, rstd1 = _ln_stats(u1)
        x1 = xhat1 * g1_ref[...] + b1_ref[...]
        x1b = x1.astype(BF16)
        x1b_ref[...] = x1b
        for k in range(N_DEV):
            cols = slice(FF_SHARD * k, FF_SHARD * (k + 1))
            r = jnp.maximum(_dot(x1b, win_s[:, cols]), 0.0)
            relu_s[:, cols] = r
            a_ref[:, cols] = (r * r).astype(BF16)
        xhat2, rstd2 = _ln_stats(ALPHA * x1 + _dot(a_ref[...], wout_s[...]))
        diff = xhat2 * g2_ref[...] + b2_ref[...] - t_ref[...]
        per_token = jnp.mean(diff * diff, axis=1, keepdims=True)
        loss_ref[...] += 0.5 * jnp.sum(per_token, axis=0, keepdims=True)
        dy = diff * (1.0 / D_MODEL)
        dg2_ref[...] += jnp.sum(dy * xhat2, axis=0, keepdims=True)
        db2_ref[...] += jnp.sum(dy, axis=0, keepdims=True)
        du2 = _ln_bwd(dy, xhat2, rstd2, g2_ref[...])
        du2b = du2.astype(BF16)
        du2b_ref[...] = du2b
        for k in range(N_DEV):
            cols = slice(FF_SHARD * k, FF_SHARD * (k + 1))
            da = _dot_nt(du2b, wout_s[cols, :])
            dhm_ref[:, cols] = (2.0 * relu_s[:, cols] * da).astype(BF16)
        dx1 = ALPHA * du2 + _dot_nt(dhm_ref[...], win_s[...])
        dg1_ref[...] += jnp.sum(dx1 * xhat1, axis=0, keepdims=True)
        db1_ref[...] += jnp.sum(dx1, axis=0, keepdims=True)
        du1_ref[...] = _ln_bwd(dx1, xhat1, rstd1, g1_ref[...])

    rows = pl.BlockSpec((tm, D_MODEL), lambda i: (i, 0))
    vec = pl.BlockSpec((1, D_MODEL), lambda i: (0, 0))
    ff = pl.BlockSpec((tm, D_FF), lambda i: (i, 0))
    wide = jax.ShapeDtypeStruct((s, D_FF), BF16)
    narrow = jax.ShapeDtypeStruct((s, D_MODEL), BF16)
    stat = jax.ShapeDtypeStruct((1, D_MODEL), F32)
    tile = lambda w: pl.BlockSpec((tm, w), lambda i: (i, 0))
    return pl.pallas_call(
        body, name="mlp", grid=(s // tm,),
        in_specs=[rows, tile(ATT_WIDTH), tile(ATT_WIDTH), tile(GATE_COLS), pl.BlockSpec((1, GATE_COLS), lambda i: (0, 0)),
                  HBM_SPEC, HBM_SPEC, HBM_SPEC, rows, HBM_SPEC, HBM_SPEC, vec, vec, vec, vec],
        out_specs=[ff, rows, ff, rows, rows, pl.BlockSpec((1, 1), lambda i: (0, 0)), vec, vec, vec, vec],
        out_shape=[wide, narrow, wide, narrow, jax.ShapeDtypeStruct((s, D_MODEL), F32),
                   jax.ShapeDtypeStruct((1, 1), F32), stat, stat, stat, stat],
        scratch_shapes=MLP_WEIGHT_SCRATCH + [
            pltpu.VMEM((tm, D_FF), F32), pltpu.VMEM((ATT_WIDTH, D_MODEL), BF16), pltpu.VMEM((ATT_WIDTH, D_MODEL), BF16),
            pltpu.VMEM((D_MODEL, D_MODEL), BF16), pltpu.SemaphoreType.DMA((2 * N_DEV + 1,))],
        compiler_params=_params(("arbitrary",)),
    )(x, o_sb, o_ca, lg, _row(b_gate), wg_sb, wg_ca, w_proj, target, wg_in, w_out,
      _row(ln1_g), _row(ln1_b), _row(ln2_g), _row(ln2_b))


def dh_slab(pieces, tm=512):
    s = pieces[0].shape[0]
    widths = [p.shape[1] for p in pieces]
    assert sum(widths) == IN_COLS
    n = len(pieces)

    def body(*refs):
        dh_ref = refs[n]
        off = 0
        for ref, w in zip(refs[:n], widths):
            dh_ref[:, off:off + w] = ref[...].astype(BF16)
            off += w

    rows = lambda w: pl.BlockSpec((tm, w), lambda i: (i, 0))
    return pl.pallas_call(
        body, name="dh_slab", grid=(s // tm,),
        in_specs=[rows(w) for w in widths], out_specs=rows(IN_COLS),
        out_shape=jax.ShapeDtypeStruct((s, IN_COLS), BF16),
        compiler_params=_params(("arbitrary",)),
    )(*pieces)


def in_bwd(du1, dh, wg_in, tm=512, guest=None):
    s = du1.shape[0]

    def body(du_ref, dh_ref, w_hbm, gx_ref, pairs_s, sems):
        @pl.when(pl.program_id(0) == 0)
        def _():
            _load_in_pairs(w_hbm, pairs_s, sems)

        acc = ALPHA * du_ref[...]
        for j in range(N_DEV // 2):
            acc = acc + _dot_nt(dh_ref[:, IN_PAIR * j:IN_PAIR * (j + 1)], pairs_s[j])
        gx_ref[...] = acc

    rows = lambda w: pl.BlockSpec((tm, w), lambda i: (i, 0))
    return _pallas(
        body, guest, (du1, dh, wg_in), name="in_bwd", grid=(s // tm,), scratch_shapes=IN_PAIR_SCRATCH,
        in_specs=[rows(D_MODEL), rows(IN_COLS), HBM_SPEC],
        out_specs=[rows(D_MODEL)],
        out_shape=[jax.ShapeDtypeStruct((s, D_MODEL), F32)],
        compiler_params=_params(("arbitrary",)),
    )


def wgrad(name, a, b, tm, tn, shard_cols=None, tk=2048):
    kk, m = a.shape
    n = b.shape[1]
    while kk % tk:
        tk //= 2
    nk = kk // tk
    w = shard_cols

    def body(a_ref, b_ref, out_ref, acc_s):
        k = pl.program_id(2)

        @pl.when(k == 0)
        def _():
            acc_s[...] = jnp.zeros_like(acc_s)

        acc_s[...] += _dot_tn(a_ref[...].astype(BF16), b_ref[...].astype(BF16))

        @pl.when(k == nk - 1)
        def _():
            if w is None:
                out_ref[...] = acc_s[...]
            else:
                for j in range(tn // w):
                    out_ref[j] = acc_s[:, w * j:w * (j + 1)]

    if w is None:
        out_spec = pl.BlockSpec((tm, tn), lambda i, j, k: (i, j))
        out_shape = jax.ShapeDtypeStruct((m, n), F32)
    else:
        out_spec = pl.BlockSpec((tn // w, tm, w), lambda i, j, k: (j, i, 0))
        out_shape = jax.ShapeDtypeStruct((n // w, m, w), F32)
    return pl.pallas_call(
        body, name=name, grid=(m // tm, n // tn, nk),
        in_specs=[pl.BlockSpec((tk, tm), lambda i, j, k: (k, i)),
                  pl.BlockSpec((tk, tn), lambda i, j, k: (k, j))],
        out_specs=out_spec, out_shape=out_shape,
        scratch_shapes=[pltpu.VMEM((tm, tn), F32)],
        compiler_params=_params(("arbitrary", "arbitrary", "arbitrary")),
    )(a, b)


def adamw(name, w, m, v, parts, tm=None, guest=None):
    rows, cols = w.shape
    tm = rows if tm is None else min(tm, rows)
    n = len(parts)
    c1 = 1.0 - ADAM_B1 ** ADAM_STEP
    c2 = 1.0 - ADAM_B2 ** ADAM_STEP

    def body(*refs):
        w_ref, m_ref, v_ref = refs[:3]
        part_refs = refs[3:3 + n]
        g_ref, d_ref, nm_ref, nv_ref = refs[3 + n:]
        g = None
        for ref, (_, index) in zip(part_refs, parts):
            term = (ref[...] if index is None else ref[0]).astype(F32)
            g = term if g is None else g + term
        new_m = ADAM_B1 * m_ref[...] + (1.0 - ADAM_B1) * g
        new_v = ADAM_B2 * v_ref[...] + (1.0 - ADAM_B2) * (g * g)
        m_hat = new_m / c1
        v_hat = new_v / c2
        g_ref[...] = g
        d_ref[...] = -ADAM_LR * (m_hat / (jnp.sqrt(v_hat) + ADAM_EPS) + ADAM_WD * w_ref[...])
        nm_ref[...] = new_m
        nv_ref[...] = new_v

    spec = pl.BlockSpec((tm, cols), lambda i: (i, 0))
    shape = jax.ShapeDtypeStruct((rows, cols), F32)

    def part_spec(index):
        if index is None:
            return spec
        return pl.BlockSpec((1, tm, cols), lambda i: (index, i, 0))

    return _pallas(
        body, guest, (w, m, v, *[array for array, _ in parts]), name=name, grid=(rows // tm,),
        in_specs=[spec] * 3 + [part_spec(index) for _, index in parts],
        out_specs=[spec] * 4, out_shape=[shape] * 4,
        compiler_params=_params(("arbitrary",)),
    )


def _place():
    x, y, c = lax.axis_index("x"), lax.axis_index("y"), lax.axis_index("c")
    other_chips = [(1 - x, y), (x, 1 - y), (1 - x, 1 - y)]
    return x, y, c, other_chips


def place_shards(arrays):
    n = len(arrays)

    def body(*refs):
        ins, outs, stage, sems = refs[:n], refs[n:2 * n], refs[2 * n:3 * n], refs[3 * n]
        x, y, c, _ = _place()
        copies = []
        for a in range(n):
            stage[a][...] = ins[a][...].astype(BF16)
            copies.append(pltpu.make_async_copy(stage[a], outs[a].at[4 * x + 2 * y + c], sems.at[a]))
            copies[-1].start()
        for cp in copies:
            cp.wait()

    return pl.pallas_call(
        body, name="place_shards",
        in_specs=[pl.BlockSpec(memory_space=pltpu.VMEM)] * n, out_specs=[HBM_SPEC] * n,
        out_shape=[jax.ShapeDtypeStruct((N_DEV,) + a.shape, BF16) for a in arrays],
        scratch_shapes=[pltpu.VMEM(a.shape, BF16) for a in arrays] + [pltpu.SemaphoreType.DMA((n,))],
        compiler_params=_params(),
    )(*arrays)


def all_gather(name, arrays):
    n = len(arrays)

    def body(*refs):
        ins, outs = refs[:n], refs[n:2 * n]
        send_sems, recv_sems, local_sems = refs[2 * n:]
        x, y, c, chips = _place()
        me = 4 * x + 2 * y + c
        sibling = (x, y, 1 - c)

        def copy(a, k, block, to, src=None):
            return pltpu.make_async_remote_copy(
                src_ref=outs[a].at[block] if src is None else src, dst_ref=outs[a].at[block],
                send_sem=send_sems.at[a, k], recv_sem=recv_sems.at[a, k],
                device_id=to, device_id_type=MESH)

        started = []
        local = [pltpu.make_async_copy(ins[a], outs[a].at[me], local_sems.at[a]) for a in range(n)]
        for a in range(n):
            local[a].start()
            sends = [copy(a, 0, me, sibling, src=ins[a])]
            sends += [copy(a, 1 + j, me, (cx, cy, c), src=ins[a]) for j, (cx, cy) in enumerate(chips)]
            for cp in sends:
                cp.start()
            started += sends
        for a in range(n):
            for j, (cx, cy) in enumerate(chips):
                block = 4 * cx + 2 * cy + c
                copy(a, 1 + j, block, sibling).wait_recv()
                passed = copy(a, 4 + j, block, sibling)
                passed.start()
                started.append(passed)
        for a in range(n):
            copy(a, 0, 4 * x + 2 * y + (1 - c), sibling).wait_recv()
            for j, (cx, cy) in enumerate(chips):
                copy(a, 4 + j, 4 * cx + 2 * cy + (1 - c), sibling).wait_recv()
        for cp in started:
            cp.wait_send()
        for cp in local:
            cp.wait()

    return pl.pallas_call(
        body, name=name,
        in_specs=[pl.BlockSpec(memory_space=pltpu.VMEM)] * n, out_specs=[HBM_SPEC] * n,
        out_shape=[jax.ShapeDtypeStruct((N_DEV,) + a.shape, a.dtype) for a in arrays],
        scratch_shapes=[pltpu.SemaphoreType.DMA((n, 7)), pltpu.SemaphoreType.DMA((n, 7)),
                        pltpu.SemaphoreType.DMA((n,))],
    )(*arrays)


def _gather_first_plan(n):
    def plan(zones):
        x, y, c, chips = _place()
        me = 4 * x + 2 * y + c
        copies = []
        for zone in zones[:n]:
            copies.append((zone.at[me], zone.at[me], (x, y, 1 - c), zone.at[4 * x + 2 * y + (1 - c)]))
            copies += [(zone.at[me], zone.at[me], (cx, cy, c), zone.at[4 * cx + 2 * cy + c]) for cx, cy in chips]
        return copies
    return plan


def _gather_second_plan(n):
    def plan(zones):
        x, y, c, chips = _place()
        return [(zone.at[4 * cx + 2 * cy + c], zone.at[4 * cx + 2 * cy + c], (x, y, 1 - c),
                 zone.at[4 * cx + 2 * cy + (1 - c)]) for zone in zones[:n] for cx, cy in chips]
    return plan


def _both_plans(first, n_first, second):
    return lambda bufs: first(bufs[:n_first]) + second(bufs[n_first:])


def _sibling_plan(n):
    def plan(bufs):
        x, y, c, _ = _place()
        return [(bufs[a].at[:, 1 - c], bufs[n + a], (x, y, 1 - c), bufs[n + a]) for a in range(n)]
    return plan


def _chips_plan(n):
    def plan(bufs):
        _, _, c, chips = _place()
        return [(bufs[a].at[2 * cx + cy], bufs[n + a].at[j], (cx, cy, c), bufs[n + a].at[j])
                for a in range(n) for j, (cx, cy) in enumerate(chips)]
    return plan


def chip_sum(name, grad, landed, place, tr=256):
    _, _, rows, cols = grad.shape
    tr = min(tr, rows)

    def body(place_ref, g_ref, l_ref, own_ref, out_ref):
        total = g_ref[0, 0] + l_ref[0]
        out_ref[0] = total.astype(BF16)

        @pl.when(pl.program_id(1) == place_ref[1])
        def _():
            own_ref[...] = total

    return pl.pallas_call(
        body, name=name,
        grid_spec=pltpu.PrefetchScalarGridSpec(
            num_scalar_prefetch=1, grid=(rows // tr, 4),
            in_specs=[pl.BlockSpec((1, 1, tr, cols), lambda i, ch, pos: (ch, pos[0], i, 0)),
                      pl.BlockSpec((1, tr, cols), lambda i, ch, pos: (ch, i, 0))],
            out_specs=[pl.BlockSpec((tr, cols), lambda i, ch, pos: (i, 0)),
                       pl.BlockSpec((1, tr, cols), lambda i, ch, pos: (ch, i, 0))]),
        out_shape=[jax.ShapeDtypeStruct((rows, cols), F32), jax.ShapeDtypeStruct((4, rows, cols), BF16)],
        compiler_params=_params(("arbitrary", "arbitrary")),
    )(place, grad, landed)


BIG =("w_in", "w_sb_proj", "w_ca_proj", "w_out", "w_mlp_in", "w_mlp_out")
SMALL = ("b_gate", "rel_bias", "ln1_g", "ln1_b", "ln2_g", "ln2_b")
NAMES = ("w_in", "b_gate", "w_sb_proj", "w_ca_proj", "rel_bias", "w_out", "ln1_g", "ln1_b",
         "w_mlp_in", "w_mlp_out", "ln2_g", "ln2_b")
REL_COLS = 2 * REL_CLIP + 1


def _pack_small(t, scalar=None):
    rel = t["rel_bias"]
    if rel.shape[1] != REL_PAD:
        rel = jnp.pad(rel, ((0, 0), (0, REL_PAD - rel.shape[1])))
    last = jnp.zeros((128,), F32) if scalar is None else jnp.pad(scalar.reshape(1), (0, 127))
    flat = [t["b_gate"].reshape(-1), rel.reshape(-1)] + [t[n].reshape(-1) for n in SMALL[2:]] + [last]
    return jnp.concatenate(flat).reshape(-1, 128)


def _unpack_small(p):
    flat = p.reshape(-1)
    out, off = {}, 0
    for n, size in (("b_gate", GATE_COLS), ("rel_bias", 8 * REL_PAD)) + tuple((n, D_MODEL) for n in SMALL[2:]):
        out[n] = flat[off:off + size]
        off += size
    out["rel_bias"] = out["rel_bias"].reshape(8, REL_PAD)[:, :REL_COLS]
    return out


def kernel(x, w_in, b_gate, w_sb_proj, w_ca_proj, rel_bias, w_out, ln1_g, ln1_b, w_mlp_in, w_mlp_out, ln2_g, ln2_b, loss_target, m_w_in, m_b_gate, m_w_sb_proj, m_w_ca_proj, m_rel_bias, m_w_out, m_ln1_g, m_ln1_b, m_w_mlp_in, m_w_mlp_out, m_ln2_g, m_ln2_b, v_w_in, v_b_gate, v_w_sb_proj, v_w_ca_proj, v_rel_bias, v_w_out, v_ln1_g, v_ln1_b, v_w_mlp_in, v_w_mlp_out, v_ln2_g, v_ln2_b):
    w = dict(w_in=w_in, b_gate=b_gate, w_sb_proj=w_sb_proj, w_ca_proj=w_ca_proj, rel_bias=rel_bias, w_out=w_out,
             ln1_g=ln1_g, ln1_b=ln1_b, w_mlp_in=w_mlp_in, w_mlp_out=w_mlp_out, ln2_g=ln2_g, ln2_b=ln2_b)
    m = dict(w_in=m_w_in, b_gate=m_b_gate, w_sb_proj=m_w_sb_proj, w_ca_proj=m_w_ca_proj, rel_bias=m_rel_bias,
             w_out=m_w_out, ln1_g=m_ln1_g, ln1_b=m_ln1_b, w_mlp_in=m_w_mlp_in, w_mlp_out=m_w_mlp_out,
             ln2_g=m_ln2_g, ln2_b=m_ln2_b)
    v = dict(w_in=v_w_in, b_gate=v_b_gate, w_sb_proj=v_w_sb_proj, w_ca_proj=v_w_ca_proj, rel_bias=v_rel_bias,
             w_out=v_w_out, ln1_g=v_ln1_g, ln1_b=v_ln1_b, w_mlp_in=v_w_mlp_in, w_mlp_out=v_w_mlp_out,
             ln2_g=v_ln2_g, ln2_b=v_ln2_b)

    x, target = x[0], loss_target[0]
    c = lax.axis_index("c")
    place = jnp.stack([c, 2 * lax.axis_index("x") + lax.axis_index("y")]).astype(jnp.int32)
    grads, deltas, new_m, new_v = {}, {}, {}, {}

    def by_owner(n, g):
        return g.reshape((4, 2) + w[n].shape)

    def finish(n, own, arrived, guest=None):
        out = adamw("adamw_" + n, w[n], m[n], v[n], [(own, None), (arrived, 0), (arrived, 1), (arrived, 2)],
                    tm=256, guest=guest)
        out, bufs = (out, None) if guest is None else out
        grads[n], deltas[n], new_m[n], new_v[n] = out
        return bufs

    def siblings(names, g):
        bufs = [by_owner(n, g[n]) for n in names] + [lax.empty((4,) + w[n].shape, F32) for n in names]
        return Exchange(tuple(bufs), _sibling_plan(len(names)), len(names))

    def chip_sums(names, bufs):
        k = len(names)
        own, sums = zip(*[chip_sum("chip_sum_" + n, bufs[a], bufs[k + a], place) for a, n in enumerate(names)])
        zones = [lax.empty((3,) + w[n].shape, BF16) for n in names]
        return own, Exchange(tuple(sums) + tuple(zones), _chips_plan(k), 3 * k)

    proj, ffn = ("w_sb_proj", "w_ca_proj", "w_out"), ("w_mlp_in", "w_mlp_out")
    zones = place_shards([w[n] for n in proj + ffn])
    proj_zones, ffn_zones = tuple(zones[:len(proj)]), tuple(zones[len(proj):])
    wg_in = all_gather("gather_w_in", [w_in.astype(BF16)])[0]
    (qkv, lg, xb), proj_zones = in_proj(
        x, wg_in, guest=Exchange(proj_zones, _gather_first_plan(len(proj)), 4 * len(proj)))
    bias_tiles = ca_bias_tiles(rel_bias)
    both = Exchange(ffn_zones + tuple(proj_zones),
                    _both_plans(_gather_first_plan(len(ffn)), len(ffn), _gather_second_plan(len(proj))),
                    4 * len(ffn) + 3 * len(proj))
    (o_sb,), zones = sb_fwd(qkv, both)
    ffn_zones, proj_zones = tuple(zones[:len(ffn)]), zones[len(ffn):]
    (o_ca,), ffn_zones = ca_fwd(qkv, bias_tiles, Exchange(ffn_zones, _gather_second_plan(len(ffn)), 3 * len(ffn)))
    wg = dict(zip(proj + ffn, list(proj_zones) + list(ffn_zones)))
    wg_sb, wg_ca, wg_mi = wg["w_sb_proj"], wg["w_ca_proj"], wg["w_mlp_in"]
    wg_mo = wg["w_mlp_out"].reshape(D_FF, D_MODEL)
    w_out_b = wg["w_out"].reshape(D_MODEL, D_MODEL)

    a, x1b, dhm, du2b, du1, loss, dg2, db2, dg1, db1 = mlp_block(
        x, o_sb, o_ca, lg, b_gate, wg_sb, wg_ca, w_out_b, target, wg_mi, wg_mo, ln1_g, ln1_b, ln2_g, ln2_b)
    g = {}
    g["w_mlp_in"] = wgrad("wgrad_mlp_in", x1b, dhm, tm=D_MODEL, tn=FF_SHARD, shard_cols=FF_SHARD, tk=x.shape[0])
    g["w_mlp_out"] = wgrad("wgrad_mlp_out", a, du2b, tm=FF_SHARD, tn=D_MODEL, tk=x.shape[0])

    mlp = ("w_mlp_in", "w_mlp_out")
    (dlg, do_sb, do_ca, dbg, g["w_out"], g["w_sb_proj"], g["w_ca_proj"]), bufs = mix_bwd(
        du1, o_sb, o_ca, lg, b_gate, wg_sb, wg_ca, w_out_b, guest=siblings(mlp, g))
    own_mlp, mlp_chips = chip_sums(mlp, bufs)

    att = ("w_out", "w_sb_proj", "w_ca_proj")
    att_siblings = siblings(att, g)
    both = Exchange(mlp_chips.bufs + att_siblings.bufs,
                    _both_plans(mlp_chips.plan, len(mlp_chips.bufs), att_siblings.plan),
                    mlp_chips.copies + att_siblings.copies)
    (dq_sb, dk_sb, dv_sb), bufs = sb_bwd(qkv, o_sb, do_sb, guest=both)
    arrived_mlp, bufs = bufs[len(mlp):len(mlp_chips.bufs)], bufs[len(mlp_chips.bufs):]
    own_att, att_chips = chip_sums(att, bufs)
    (dq_ca, dk_ca, dv_ca, db_tiles), bufs = ca_bwd(qkv, bias_tiles, do_ca, guest=att_chips)
    arrived_att = bufs[len(att):]
    g_rel = rel_bias_grad(db_tiles)

    dh = dh_slab([dq_sb, dk_sb, dv_sb, dq_ca, dk_ca, dv_ca, dlg])
    g["w_in"] = wgrad("wgrad_in", xb, dh, tm=D_MODEL, tn=2 * IN_SHARD, shard_cols=IN_SHARD)
    last = ("w_in",)
    bufs = finish(mlp[0], own_mlp[0], arrived_mlp[0], guest=siblings(last, g))
    finish(mlp[1], own_mlp[1], arrived_mlp[1])
    for n, own, arrived in zip(att, own_att, arrived_att):
        finish(n, own, arrived)
    own_in, in_chips = chip_sums(last, bufs)
    (grad_x,), bufs = in_bwd(du1, dh, wg_in, guest=in_chips)
    finish(last[0], own_in[0], bufs[1])

    small = dict(b_gate=dbg, rel_bias=g_rel, ln1_g=dg1, ln1_b=db1, ln2_g=dg2, ln2_b=db2)
    parts = all_gather("gather_small_grads", [_pack_small(small, loss[0, 0])])[0]
    packed = adamw("adamw_small", _pack_small(w), _pack_small(m), _pack_small(v),
                   [(parts, d) for d in range(N_DEV)])
    loss = packed[0][-1, 0]
    for out, p in zip((grads, deltas, new_m, new_v), packed):
        for n, val in _unpack_small(p).items():
            out[n] = val.reshape(w[n].shape)

    return (loss, grad_x[None], *[grads[n] for n in NAMES], *[deltas[n] for n in NAMES],
            *[new_m[n] for n in NAMES], *[new_v[n] for n in NAMES])
```

```python
import functools
from typing import NamedTuple

import jax
import jax.numpy as jnp
from jax import lax
from jax.experimental import pallas as pl
from jax.experimental.pallas import tpu as pltpu

F32 = jnp.float32
BF16 = jnp.bfloat16
MESH = pl.DeviceIdType.MESH

N_DEV = 8
D_MODEL = 1024
HEAD_DIM = 64
ATT_WIDTH = 512
N_PAIRS = ATT_WIDTH // 128
QKV_COLS = 6 * ATT_WIDTH
GATE_COLS = 2 * D_MODEL
IN_COLS = QKV_COLS + GATE_COLS
IN_SHARD = IN_COLS // N_DEV
D_FF = 4 * D_MODEL
FF_SHARD = D_FF // N_DEV
PROJ_SHARD = D_MODEL // N_DEV
ATT_BLOCK = 128
CA_TILES = 5
CA_ROWS = 2048
CHUNK = 64
CA_PREV_CHUNKS = 8
REL_CLIP = 256
REL_PAD = 640
ALPHA = 2.0 ** 0.25
LN_EPS = 1e-5
QK_SCALE = HEAD_DIM ** -0.5
NEG_BIG = -1e30
VMEM_LIMIT = 56 * 1024 * 1024

ADAM_LR = 0.001
ADAM_B1 = 0.9
ADAM_B2 = 0.999
ADAM_EPS = 1e-08
ADAM_WD = 0.01
ADAM_STEP = 10

_NT = (((1,), (1,)), ((), ()))
_TN = (((0,), (0,)), ((), ()))


def _dot(a, b):
    return jnp.dot(a, b, preferred_element_type=F32)


def _dot_nt(a, b):
    return lax.dot_general(a, b, _NT, preferred_element_type=F32)


def _dot_tn(a, b):
    return lax.dot_general(a, b, _TN, preferred_element_type=F32)


def _params(semantics=None):
    return pltpu.CompilerParams(dimension_semantics=semantics, vmem_limit_bytes=VMEM_LIMIT)


def _row(v):
    return v.reshape(1, -1)


HBM_SPEC = pl.BlockSpec(memory_space=pl.ANY)


class Exchange(NamedTuple):
    bufs: tuple
    plan: object
    copies: int


def _pallas(body, guest, args, *, grid, in_specs, out_specs, out_shape, scratch_shapes=(), **kw):
    out_specs, out_shape, scratch_shapes = list(out_specs), list(out_shape), list(scratch_shapes)
    if guest is None:
        return pl.pallas_call(body, grid=grid, in_specs=in_specs, out_specs=out_specs, out_shape=out_shape,
                              scratch_shapes=scratch_shapes, **kw)(*args)
    nb, n_own = len(guest.bufs), len(in_specs) + len(out_specs)

    def hosting(*refs):
        zones = refs[nb + n_own:2 * nb + n_own]
        send_sems, recv_sems = refs[-2], refs[-1]
        steps = [pl.program_id(axis) for axis in range(len(grid))]
        first = functools.reduce(jnp.logical_and, [step == 0 for step in steps])
        last = functools.reduce(jnp.logical_and, [step == n - 1 for step, n in zip(steps, grid)])

        def copy(k, src, dst, peer):
            return pltpu.make_async_remote_copy(src_ref=src, dst_ref=dst, send_sem=send_sems.at[k],
                                                recv_sem=recv_sems.at[k], device_id=peer, device_id_type=MESH)

        @pl.when(first)
        def _():
            for k, (src, dst, peer, _) in enumerate(guest.plan(zones)):
                copy(k, src, dst, peer).start()

        body(*refs[nb:nb + n_own], *refs[2 * nb + n_own:-2])

        @pl.when(last)
        def _():
            for k, (src, dst, peer, landed) in enumerate(guest.plan(zones)):
                copy(k, src, dst, peer).wait_send()
                copy(k, src, landed, peer).wait_recv()

    sems = pltpu.SemaphoreType.DMA((guest.copies,))
    results = pl.pallas_call(
        hosting, grid=grid, in_specs=[HBM_SPEC] * nb + list(in_specs), out_specs=out_specs + [HBM_SPEC] * nb,
        out_shape=out_shape + [jax.ShapeDtypeStruct(b.shape, b.dtype) for b in guest.bufs],
        input_output_aliases={k: len(out_specs) + k for k in range(nb)},
        scratch_shapes=scratch_shapes + [sems, sems], **kw)(*guest.bufs, *args)
    return results[:len(out_specs)], results[len(out_specs):]


IN_PAIR = 2 * IN_SHARD
IN_PAIR_SCRATCH = [pltpu.VMEM((N_DEV // 2, D_MODEL, IN_PAIR), BF16), pltpu.SemaphoreType.DMA((N_DEV,))]


def _load_in_pairs(w_hbm, pairs_s, sems):
    copies = [pltpu.make_async_copy(w_hbm.at[j], pairs_s.at[j // 2, :, pl.ds(IN_SHARD * (j % 2), IN_SHARD)],
                                    sems.at[j]) for j in range(N_DEV)]
    for cp in copies:
        cp.start()
    for cp in copies:
        cp.wait()


def in_proj(x, wg_in, tm=512, guest=None):
    s = x.shape[0]

    def body(x_ref, w_hbm, qkv_ref, lg_ref, xb_ref, pairs_s, sems):
        @pl.when(pl.program_id(0) == 0)
        def _():
            _load_in_pairs(w_hbm, pairs_s, sems)

        xb = x_ref[...].astype(BF16)
        xb_ref[...] = xb
        for j in range(N_DEV // 2):
            acc = _dot(xb, pairs_s[j])
            lo, hi = IN_PAIR * j, IN_PAIR * (j + 1)
            if hi <= QKV_COLS:
                qkv_ref[:, lo:hi] = acc.astype(BF16)
            elif lo >= QKV_COLS:
                lg_ref[:, lo - QKV_COLS:hi - QKV_COLS] = acc
            else:
                qkv_ref[:, lo:QKV_COLS] = acc[:, :QKV_COLS - lo].astype(BF16)
                lg_ref[:, 0:hi - QKV_COLS] = acc[:, QKV_COLS - lo:]

    return _pallas(
        body, guest, (x, wg_in), name="in_proj", grid=(s // tm,), scratch_shapes=IN_PAIR_SCRATCH,
        in_specs=[pl.BlockSpec((tm, D_MODEL), lambda i: (i, 0)), HBM_SPEC],
        out_specs=[pl.BlockSpec((tm, QKV_COLS), lambda i: (i, 0)),
                   pl.BlockSpec((tm, GATE_COLS), lambda i: (i, 0)),
                   pl.BlockSpec((tm, D_MODEL), lambda i: (i, 0))],
        out_shape=[jax.ShapeDtypeStruct((s, QKV_COLS), BF16),
                   jax.ShapeDtypeStruct((s, GATE_COLS), F32),
                   jax.ShapeDtypeStruct((s, D_MODEL), BF16)],
        compiler_params=_params(("arbitrary",)),
    )


def _head_masks():
    lane = lax.broadcasted_iota(jnp.int32, (1, 128), 1)
    first = lane < HEAD_DIM
    return first, jnp.logical_not(first)


SB_CHUNK = 512
SB_SUB = SB_CHUNK // ATT_BLOCK


def _tri_pair():
    row = lax.broadcasted_iota(jnp.int32, (4 * ATT_BLOCK, 2 * ATT_BLOCK), 0) & (2 * ATT_BLOCK - 1)
    col = lax.broadcasted_iota(jnp.int32, (4 * ATT_BLOCK, 2 * ATT_BLOCK), 1)
    same = (row >= ATT_BLOCK) == (col >= ATT_BLOCK)
    return jnp.logical_and(same, row >= col).astype(BF16)


def _pair_bf16(pieces):
    return jnp.concatenate(
        [jnp.concatenate([p.astype(BF16) for p in row], axis=1) for row in pieces], axis=0)


def _suffix_sums(pieces, tri_pair):
    hi = [[p.astype(BF16) for p in row] for row in pieces]
    lo = [[p - h.astype(F32) for p, h in zip(row, hrow)] for row, hrow in zip(pieces, hi)]
    return _dot(jnp.concatenate([_pair_bf16(hi), _pair_bf16(lo)], axis=1), tri_pair)


def _first_row_piece(diag, j):
    return j if diag else 0


def _piece(v, r, h):
    return v[ATT_BLOCK * r:ATT_BLOCK * (r + 1), ATT_BLOCK * h:ATT_BLOCK * (h + 1)]


def _all_row_pieces(slabs):
    return [[slab[ATT_BLOCK * r:ATT_BLOCK * (r + 1)] for r in range(SB_SUB)] for slab in slabs]


def _add_rows(total, part, r0):
    if r0 == 0:
        return total + part
    return jnp.concatenate([total[:ATT_BLOCK * r0], total[ATT_BLOCK * r0:] + part], axis=0)


def _row_totals(csum, r0, h):
    return jnp.concatenate(
        [jnp.broadcast_to(_piece(csum, at, h)[:, 0:1], (ATT_BLOCK, ATT_BLOCK)) for at in range(SB_SUB - r0)], axis=0)


SIGN_BIT = 0x80000000


def _drop(z):
    minus_abs = lax.bitcast_convert_type(lax.bitcast_convert_type(z, jnp.uint32) | jnp.uint32(SIGN_BIT), F32)
    return jnp.maximum(z, 0.0) + jnp.log(1.0 + jnp.exp(minus_abs))


def _sb_weights(z, runs, r0, masked, strict, tri_pair):
    drops = []
    for r in range(r0, SB_SUB):
        row = [_drop(_piece(z, r - r0, h)) for h in range(2)]
        drops.append([jnp.where(strict, drop, 0.0) for drop in row] if r == masked else row)
    csum = _suffix_sums(drops, tri_pair)
    weights = []
    for r in range(r0, SB_SUB):
        row = [jnp.exp(_piece(z, r - r0, h) - (_piece(csum, r - r0, h) + runs[h][r])) for h in range(2)]
        weights.append([jnp.where(strict, a, 0.0) for a in row] if r == masked else row)
    return drops, weights, csum


EXP_UNDERFLOW = 104.0


SB_STEP = 2


def _sweep_left(span, i, dropped_ref):
    def more(at):
        return jnp.logical_and(at >= 0, jnp.min(dropped_ref[...]) < EXP_UNDERFLOW)

    def step(at):
        span(at * (SB_STEP * ATT_BLOCK), SB_STEP, False)
        return at - 1

    lax.while_loop(more, step, i * (SB_SUB // SB_STEP) - 1)


def _by_head(blk, masks):
    zero = jnp.zeros_like(blk)
    return jnp.concatenate([jnp.where(m, blk, zero) for m in masks], axis=0)


def sb_fwd(qkv, guest=None):
    s = qkv.shape[0]
    nq = s // SB_CHUNK

    def body(q_ref, k_ref, v_ref, o_ref, run_s):
        i = pl.program_id(1)
        masks = _head_masks()
        q = q_ref[...] * QK_SCALE
        row = lax.broadcasted_iota(jnp.int32, (ATT_BLOCK, ATT_BLOCK), 0)
        col = lax.broadcasted_iota(jnp.int32, (ATT_BLOCK, ATT_BLOCK), 1)
        strict = col < row
        tri_pair = _tri_pair()
        run_s[...] = jnp.zeros_like(run_s)
        o_ref[...] = jnp.zeros_like(o_ref)

        def span(start, pieces, diag):
            runs = [run_s[0], run_s[1]]
            acc = jnp.zeros((SB_CHUNK, ATT_BLOCK), F32)
            for j in reversed(range(pieces)):
                r0 = _first_row_piece(diag, j)
                keys = pl.ds(pl.multiple_of(start + ATT_BLOCK * j, ATT_BLOCK), ATT_BLOCK)
                z = _dot_nt(q[ATT_BLOCK * r0:], _by_head(k_ref[keys, :], masks))
                _, weights, csum = _sb_weights(z, _all_row_pieces(runs), r0, j if diag else None, strict, tri_pair)
                part = _dot(_pair_bf16(weights), _by_head(v_ref[keys, :], masks))
                acc = _add_rows(acc, part, r0)
                runs = [_add_rows(runs[h], _row_totals(csum, r0, h), r0) for h in range(2)]
            run_s[0], run_s[1] = runs
            o_ref[...] += acc

        span(i * SB_CHUNK, SB_SUB, True)
        _sweep_left(span, i, run_s)

    return _pallas(
        body, guest, (qkv, qkv, qkv), name="sb_fwd", grid=(N_PAIRS, nq),
        in_specs=[pl.BlockSpec((SB_CHUNK, 128), lambda p, i: (i, p)),
                  pl.BlockSpec((s, 128), lambda p, i: (0, N_PAIRS + p)),
                  pl.BlockSpec((s, 128), lambda p, i: (0, 2 * N_PAIRS + p))],
        out_specs=[pl.BlockSpec((SB_CHUNK, 128), lambda p, i: (i, p))],
        out_shape=[jax.ShapeDtypeStruct((s, ATT_WIDTH), F32)],
        scratch_shapes=[pltpu.VMEM((2, SB_CHUNK, 128), F32)],
        compiler_params=_params(("arbitrary", "arbitrary")),
    )


def sb_bwd(qkv, o_sb, do_sb, guest=None):
    s = qkv.shape[0]
    nq = s // SB_CHUNK

    def body(q_ref, k_ref, v_ref, o_ref, do_ref, dq_ref, dk_ref, dv_ref, run_l_s, run_e_s, etot_s, dq_s, dk_s, dv_s):
        i = pl.program_id(1)

        @pl.when(i == 0)
        def _():
            dk_s[...] = jnp.zeros_like(dk_s)
            dv_s[...] = jnp.zeros_like(dv_s)

        masks = _head_masks()
        q = q_ref[...] * QK_SCALE
        do = do_ref[...]
        prod = do.astype(F32) * o_ref[...]
        for h in range(2):
            total = jnp.sum(jnp.where(masks[h], prod, 0.0), axis=1, keepdims=True)
            etot_s[h] = jnp.broadcast_to(total, (SB_CHUNK, ATT_BLOCK))
        row = lax.broadcasted_iota(jnp.int32, (ATT_BLOCK, ATT_BLOCK), 0)
        col = lax.broadcasted_iota(jnp.int32, (ATT_BLOCK, ATT_BLOCK), 1)
        strict = col < row
        tri_pair = _tri_pair()
        run_l_s[...] = jnp.zeros_like(run_l_s)
        run_e_s[...] = jnp.zeros_like(run_e_s)
        dq_s[...] = jnp.zeros_like(dq_s)

        def span(start, pieces, diag):
            slabs_l = [run_l_s[0], run_l_s[1]]
            slabs_e = [run_e_s[0], run_e_s[1]]
            e_tot = _all_row_pieces([etot_s[0], etot_s[1]])
            dq = jnp.zeros((SB_CHUNK, ATT_BLOCK), F32)
            for j in reversed(range(pieces)):
                runs_l, runs_e = _all_row_pieces(slabs_l), _all_row_pieces(slabs_e)
                r0 = _first_row_piece(diag, j)
                masked = j if diag else None
                keys = pl.ds(pl.multiple_of(start + ATT_BLOCK * j, ATT_BLOCK), ATT_BLOCK)
                kk = _by_head(k_ref[keys, :], masks)
                vv = _by_head(v_ref[keys, :], masks)
                q_rows, do_rows = q[ATT_BLOCK * r0:], do[ATT_BLOCK * r0:]
                z = _dot_nt(q_rows, kk)
                da = _dot_nt(do_rows, vv)
                drops, weights, csum = _sb_weights(z, runs_l, r0, masked, strict, tri_pair)
                ab = [[a.astype(BF16) for a in wrow] for wrow in weights]
                es = [[a.astype(F32) * _piece(da, at, h) for h, a in enumerate(arow)] for at, arow in enumerate(ab)]
                esum = _suffix_sums(es, tri_pair)
                dzs = []
                for at, r in enumerate(range(r0, SB_SUB)):
                    dz_row = []
                    for h in range(2):
                        sig = jnp.exp(_piece(z, at, h) - drops[at][h])
                        after = _piece(esum, at, h) + runs_e[h][r] - es[at][h]
                        dz = es[at][h] - sig * (e_tot[h][r] - after)
                        dz_row.append(jnp.where(strict, dz, 0.0) if r == masked else dz)
                    dzs.append(dz_row)
                dz = _pair_bf16(dzs)
                dq = _add_rows(dq, _dot(dz, kk), r0)
                dk = _dot_tn(dz, q_rows)
                dv = _dot_tn(_pair_bf16(ab), do_rows)
                dk_s[keys, :] += jnp.where(masks[0], dk[:ATT_BLOCK], dk[ATT_BLOCK:])
                dv_s[keys, :] += jnp.where(masks[0], dv[:ATT_BLOCK], dv[ATT_BLOCK:])
                slabs_l =[_add_rows(slabs_l[h], _row_totals(csum, r0, h), r0) for h in range(2)]
                slabs_e = [_add_rows(slabs_e[h], _row_totals(esum, r0, h), r0) for h in range(2)]
            run_l_s[0], run_l_s[1] = slabs_l
            run_e_s[0], run_e_s[1] = slabs_e
            dq_s[...] += dq

        span(i * SB_CHUNK, SB_SUB, True)
        _sweep_left(span, i, run_l_s)
        dq_ref[...] = (dq_s[...] * QK_SCALE).astype(BF16)

        @pl.when(i == nq - 1)
        def _():
            dk_ref[...] = dk_s[...].astype(BF16)
            dv_ref[...] = dv_s[...].astype(BF16)

    tile = lambda p, i: (i, p)
    state = pltpu.VMEM((2, SB_CHUNK, 128), F32)
    sums = pltpu.VMEM((s, 128), F32)
    return _pallas(
        body, guest, (qkv, qkv, qkv, o_sb, do_sb), name="sb_bwd", grid=(N_PAIRS, nq),
        in_specs=[pl.BlockSpec((SB_CHUNK, 128), tile),
                  pl.BlockSpec((s, 128), lambda p, i: (0, N_PAIRS + p)),
                  pl.BlockSpec((s, 128), lambda p, i: (0, 2 * N_PAIRS + p)),
                  pl.BlockSpec((SB_CHUNK, 128), tile),
                  pl.BlockSpec((SB_CHUNK, 128), tile)],
        out_specs=[pl.BlockSpec((SB_CHUNK, 128), tile),
                   pl.BlockSpec((s, 128), lambda p, i: (0, p)),
                   pl.BlockSpec((s, 128), lambda p, i: (0, p))],
        out_shape=[jax.ShapeDtypeStruct((s, ATT_WIDTH), BF16)] * 3,
        scratch_shapes=[state, state, state, pltpu.VMEM((SB_CHUNK, 128), F32), sums, sums],
        compiler_params=_params(("arbitrary", "arbitrary")),
    )


def ca_bias_tiles(rel_bias):
    heads = rel_bias.shape[0]
    wide = 2 * ATT_BLOCK

    def body(rel_ref, out_ref):
        bins = lax.broadcasted_iota(jnp.int32, (REL_PAD, wide), 0)
        pos = lax.broadcasted_iota(jnp.int32, (REL_PAD, wide), 1)
        r = lax.broadcasted_iota(jnp.int32, (ATT_BLOCK, ATT_BLOCK), 0)
        c = lax.broadcasted_iota(jnp.int32, (ATT_BLOCK, ATT_BLOCK), 1)
        for j in range(CA_TILES):
            hot = (jnp.clip(ATT_BLOCK * (j + 1) - 1 - pos, -REL_CLIP, REL_CLIP) + REL_CLIP == bins).astype(F32)
            lines = jnp.dot(rel_ref[...], hot, preferred_element_type=F32, precision=lax.Precision.HIGHEST)
            back = 2 * j + (r >> 6) - (c >> 6)
            ok = jnp.logical_and(back >= 0, back <= CA_PREV_CHUNKS)
            for h in range(heads):
                tile = jnp.broadcast_to(lines[h:h + 1, :], (ATT_BLOCK, wide))
                tile = pltpu.roll(tile, wide - (ATT_BLOCK - 1), 1, stride=1, stride_axis=0)
                out_ref[h, j] = jnp.where(ok, tile[:, :ATT_BLOCK], NEG_BIG)

    padded = jnp.pad(rel_bias, ((0, 0), (0, REL_PAD - rel_bias.shape[1])))
    return pl.pallas_call(
        body, name="ca_bias_tiles",
        out_shape=jax.ShapeDtypeStruct((heads, CA_TILES, ATT_BLOCK, ATT_BLOCK), F32),
        compiler_params=_params(),
    )(padded)


CA_BLOCKS = CA_ROWS // ATT_BLOCK
CA_KEY_BLOCKS = CA_BLOCKS + CA_TILES - 1


def _ca_users(t):
    return [r for r in range(CA_BLOCKS) if 0 <= r + CA_TILES - 1 - t < CA_TILES]


def _ca_rows(v, t, r):
    at = _ca_users(t).index(r)
    return v[ATT_BLOCK * at:ATT_BLOCK * (at + 1)]


def _ca_window(step, k_ref, v_ref, masks):
    kks, vvs, inside, rows = [], [], [], []
    for t in range(CA_KEY_BLOCKS):
        block = step * CA_BLOCKS - (CA_TILES - 1) + t
        keys = pl.ds(pl.multiple_of(jnp.maximum(block, 0) * ATT_BLOCK, ATT_BLOCK), ATT_BLOCK)
        kks.append(_by_head(k_ref[keys, :], masks))
        vvs.append(_by_head(v_ref[keys, :], masks))
        inside.append(block >= 0)
        rows.append(keys)
    return kks, vvs, inside, rows


def _ca_probs(q, kks, inside, bias_ref):
    scores = [[[None] * CA_TILES for _ in range(2)] for _ in range(CA_BLOCKS)]
    for t in range(CA_KEY_BLOCKS):
        users = _ca_users(t)
        z = _dot_nt(q[ATT_BLOCK * users[0]:ATT_BLOCK * (users[-1] + 1)], kks[t])
        for r in users:
            j = r + CA_TILES - 1 - t
            for h in range(2):
                zz = _ca_rows(z, t, r)[:, ATT_BLOCK * h:ATT_BLOCK * (h + 1)] + bias_ref[h, j]
                scores[r][h][j] = jnp.where(inside[t], zz, NEG_BIG)
    probs = [[None, None] for _ in range(CA_BLOCKS)]
    for r in range(CA_BLOCKS):
        for h in range(2):
            m = jnp.max(functools.reduce(jnp.maximum, scores[r][h]), axis=1, keepdims=True)
            ex = [jnp.exp(z - m) for z in scores[r][h]]
            inv = 1.0 / jnp.sum(functools.reduce(jnp.add, ex), axis=1, keepdims=True)
            probs[r][h] = [e * inv for e in ex]
    return probs


def _ca_stack(tiles, t):
    return jnp.concatenate(
        [jnp.concatenate([tiles[r][h][r + CA_TILES - 1 - t].astype(BF16) for h in range(2)], axis=1)
         for r in _ca_users(t)], axis=0)


def ca_fwd(qkv, bias_tiles, guest=None):
    s = qkv.shape[0]
    nq = s // CA_ROWS

    def body(q_ref, k_ref, v_ref, bias_ref, o_ref):
        masks = _head_masks()
        kks, vvs, inside, _ = _ca_window(pl.program_id(1), k_ref, v_ref, masks)
        probs = _ca_probs(q_ref[...] * QK_SCALE, kks, inside, bias_ref)
        out = [None] * CA_BLOCKS
        for t in range(CA_KEY_BLOCKS):
            part = _dot(_ca_stack(probs, t), vvs[t])
            for r in _ca_users(t):
                piece = _ca_rows(part, t, r)
                out[r] = piece if out[r] is None else out[r] + piece
        o_ref[...] = jnp.concatenate(out, axis=0).astype(BF16)

    return _pallas(
        body, guest, (qkv, qkv, qkv, bias_tiles), name="ca_fwd", grid=(N_PAIRS, nq),
        in_specs=[pl.BlockSpec((CA_ROWS, 128), lambda p, i: (i, 3 * N_PAIRS + p)),
                  pl.BlockSpec((s, 128), lambda p, i: (0, 4 * N_PAIRS + p)),
                  pl.BlockSpec((s, 128), lambda p, i: (0, 5 * N_PAIRS + p)),
                  pl.BlockSpec((2, CA_TILES, ATT_BLOCK, ATT_BLOCK), lambda p, i: (p, 0, 0, 0))],
        out_specs=[pl.BlockSpec((CA_ROWS, 128), lambda p, i: (i, p))],
        out_shape=[jax.ShapeDtypeStruct((s, ATT_WIDTH), BF16)],
        compiler_params=_params(("arbitrary", "arbitrary")),
    )


def ca_bwd(qkv, bias_tiles, do_ca, guest=None):
    s = qkv.shape[0]
    nq = s // CA_ROWS

    def body(q_ref, k_ref, v_ref, bias_ref, do_ref, dq_ref, dk_ref, dv_ref, db_ref, dk_s, dv_s):
        big = pl.program_id(1)

        @pl.when(big == 0)
        def _():
            dk_s[...] = jnp.zeros_like(dk_s)
            dv_s[...] = jnp.zeros_like(dv_s)
            db_ref[...] = jnp.zeros_like(db_ref)

        masks = _head_masks()
        kks, vvs, inside, key_rows = _ca_window(big, k_ref, v_ref, masks)
        q = q_ref[...] * QK_SCALE
        do = do_ref[...]
        probs = _ca_probs(q, kks, inside, bias_ref)
        dps = [[[None] * CA_TILES for _ in range(2)] for _ in range(CA_BLOCKS)]
        for t in range(CA_KEY_BLOCKS):
            users = _ca_users(t)
            dp = _dot_nt(do[ATT_BLOCK * users[0]:ATT_BLOCK * (users[-1] + 1)], vvs[t])
            for r in users:
                for h in range(2):
                    dps[r][h][r + CA_TILES - 1 - t] = _ca_rows(dp, t, r)[:, ATT_BLOCK * h:ATT_BLOCK * (h + 1)]
        dss = [[None, None] for _ in range(CA_BLOCKS)]
        for r in range(CA_BLOCKS):
            for h in range(2):
                delta = jnp.sum(functools.reduce(jnp.add, [p * dp for p, dp in zip(probs[r][h], dps[r][h])]),
                                axis=1, keepdims=True)
                dss[r][h] = [p * (dp - delta) for p, dp in zip(probs[r][h], dps[r][h])]
        for h in range(2):
            for j in range(CA_TILES):
                db_ref[h, j] += functools.reduce(jnp.add, [dss[r][h][j] for r in range(CA_BLOCKS)])
        dq = [None] * CA_BLOCKS
        for t in range(CA_KEY_BLOCKS):
            users = _ca_users(t)
            rows = slice(ATT_BLOCK * users[0], ATT_BLOCK * (users[-1] + 1))
            ds = _ca_stack(dss, t)
            part = _dot(ds, kks[t])
            for r in users:
                piece = _ca_rows(part, t, r)
                dq[r] = piece if dq[r] is None else dq[r] + piece
            dk = _dot_tn(ds, q[rows])
            dv = _dot_tn(_ca_stack(probs, t), do[rows])
            dk_s[key_rows[t], :] += jnp.where(masks[0], dk[:ATT_BLOCK], dk[ATT_BLOCK:])
            dv_s[key_rows[t], :] += jnp.where(masks[0], dv[:ATT_BLOCK], dv[ATT_BLOCK:])
        dq_ref[...] = (jnp.concatenate(dq, axis=0) * QK_SCALE).astype(BF16)

        @pl.when(big == nq - 1)
        def _():
            dk_ref[...] = dk_s[...].astype(BF16)
            dv_ref[...] = dv_s[...].astype(BF16)

    return _pallas(
        body, guest, (qkv, qkv, qkv, bias_tiles, do_ca), name="ca_bwd", grid=(N_PAIRS, nq),
        in_specs=[pl.BlockSpec((CA_ROWS, 128), lambda p, i: (i, 3 * N_PAIRS + p)),
                  pl.BlockSpec((s, 128), lambda p, i: (0, 4 * N_PAIRS + p)),
                  pl.BlockSpec((s, 128), lambda p, i: (0, 5 * N_PAIRS + p)),
                  pl.BlockSpec((2, CA_TILES, ATT_BLOCK, ATT_BLOCK), lambda p, i: (p, 0, 0, 0)),
                  pl.BlockSpec((CA_ROWS, 128), lambda p, i: (i, p))],
        out_specs=[pl.BlockSpec((CA_ROWS, 128), lambda p, i: (i, p)),
                   pl.BlockSpec((s, 128), lambda p, i: (0, p)),
                   pl.BlockSpec((s, 128), lambda p, i: (0, p)),
                   pl.BlockSpec((2, CA_TILES, ATT_BLOCK, ATT_BLOCK), lambda p, i: (p, 0, 0, 0))],
        out_shape=[jax.ShapeDtypeStruct((s, ATT_WIDTH), BF16)] * 3
                  + [jax.ShapeDtypeStruct((2 * N_PAIRS, CA_TILES, ATT_BLOCK, ATT_BLOCK), F32)],
        scratch_shapes=[pltpu.VMEM((s, 128), F32), pltpu.VMEM((s, 128), F32)],
        compiler_params=_params(("arbitrary", "arbitrary")),
    )


def rel_bias_grad(db_tiles):
    heads = db_tiles.shape[0]

    def body(db_ref, out_ref):
        row = lax.broadcasted_iota(jnp.int32, (ATT_BLOCK, ATT_BLOCK), 0)
        col = lax.broadcasted_iota(jnp.int32, (ATT_BLOCK, ATT_BLOCK), 1)
        flip = (row + col == ATT_BLOCK - 1).astype(F32)
        wrapped = col < row
        cp = lax.broadcasted_iota(jnp.int32, (ATT_BLOCK, REL_PAD), 0)
        bins = lax.broadcasted_iota(jnp.int32, (ATT_BLOCK, REL_PAD), 1)
        total = jnp.zeros((heads, REL_PAD), F32)
        for j in range(CA_TILES):
            sums_neg, sums_pos = [], []
            for h in range(heads):
                tile = jnp.dot(db_ref[h, j], flip, preferred_element_type=F32, precision=lax.Precision.HIGHEST)
                tile = pltpu.roll(tile, 0, 1, stride=1, stride_axis=0)
                sums_neg.append(jnp.sum(jnp.where(wrapped, 0.0, tile), axis=0, keepdims=True))
                sums_pos.append(jnp.sum(jnp.where(wrapped, tile, 0.0), axis=0, keepdims=True))
            neg = jnp.concatenate(sums_neg, axis=0)
            pos = jnp.concatenate(sums_pos, axis=0)
            hot_neg = (jnp.clip(ATT_BLOCK * j + cp - (ATT_BLOCK - 1), -REL_CLIP, REL_CLIP) + REL_CLIP == bins)
            hot_pos = (jnp.clip(ATT_BLOCK * j + cp + 1, -REL_CLIP, REL_CLIP) + REL_CLIP == bins)
            total = total + jnp.dot(neg, hot_neg.astype(F32), preferred_element_type=F32,
                                    precision=lax.Precision.HIGHEST)
            total = total + jnp.dot(pos, hot_pos.astype(F32), preferred_element_type=F32,
                                    precision=lax.Precision.HIGHEST)
        out_ref[...] = total

    return pl.pallas_call(
        body, name="rel_bias_grad",
        out_shape=jax.ShapeDtypeStruct((heads, REL_PAD), F32),
        compiler_params=_params(),
    )(db_tiles)


def _assemble_cols(dst_ref, src_ref):
    w = src_ref.shape[2]
    for j in range(N_DEV):
        dst_ref[:, w * j:w * (j + 1)] = src_ref[j]


def _merge(o_sb, o_ca, lg, bg, wsb, wca):
    y_sb = _dot(o_sb.astype(BF16), wsb)
    y_ca = _dot(o_ca.astype(BF16), wca)
    gates = jax.nn.sigmoid(lg + bg)
    g_sb, g_ca = gates[:, :D_MODEL], gates[:, D_MODEL:]
    return y_sb, y_ca, g_sb, g_ca, g_sb * y_sb + g_ca * y_ca


def mix_bwd(du1, o_sb, o_ca, lg, b_gate, wg_sb, wg_ca, w_out, tm=256, guest=None):
    s = du1.shape[0]
    last = s // tm - 1

    def body(du_ref, osb_ref, oca_ref, lg_ref, bg_ref, wsb_ref, wca_ref, wout_ref,
             dlg_ref, dosb_ref, doca_ref, dbg_ref, gout_ref, gsb_ref, gca_ref, wsb_s, wca_s, gsb_s, gca_s):
        step = pl.program_id(0)

        @pl.when(step == 0)
        def _():
            _assemble_cols(wsb_s, wsb_ref)
            _assemble_cols(wca_s, wca_ref)
            for ref in (dbg_ref, gout_ref, gsb_s, gca_s):
                ref[...] = jnp.zeros_like(ref)

        o_sb, o_ca, du = osb_ref[...].astype(BF16), oca_ref[...].astype(BF16), du_ref[...].astype(BF16)
        y_sb, y_ca, g_sb, g_ca, merged = _merge(o_sb, o_ca, lg_ref[...], bg_ref[...], wsb_s[...], wca_s[...])
        dm = _dot_nt(du, wout_ref[...])
        dl_sb = dm * y_sb * g_sb * (1.0 - g_sb)
        dl_ca = dm * y_ca * g_ca * (1.0 - g_ca)
        dlg_ref[:, :D_MODEL] = dl_sb.astype(BF16)
        dlg_ref[:, D_MODEL:] = dl_ca.astype(BF16)
        dbg_ref[:, :D_MODEL] += jnp.sum(dl_sb, axis=0, keepdims=True)
        dbg_ref[:, D_MODEL:] += jnp.sum(dl_ca, axis=0, keepdims=True)
        dy_sb = (dm * g_sb).astype(BF16)
        dy_ca = (dm * g_ca).astype(BF16)
        dosb_ref[...] = _dot_nt(dy_sb, wsb_s[...]).astype(BF16)
        doca_ref[...] = _dot_nt(dy_ca, wca_s[...]).astype(BF16)
        gout_ref[...] += _dot_tn(merged.astype(BF16), du)
        gsb_s[...] += _dot_tn(o_sb, dy_sb)
        gca_s[...] += _dot_tn(o_ca, dy_ca)

        @pl.when(step == last)
        def _():
            for j in range(N_DEV):
                cols = slice(PROJ_SHARD * j, PROJ_SHARD * (j + 1))
                gsb_ref[j] = gsb_s[:, cols]
                gca_ref[j] = gca_s[:, cols]

    rows = lambda w: pl.BlockSpec((tm, w), lambda i: (i, 0))
    whole = lambda shape: pl.BlockSpec(shape, lambda i: (0,) * len(shape))
    stack = (N_DEV, ATT_WIDTH, PROJ_SHARD)
    return _pallas(
        body, guest, (du1, o_sb, o_ca, lg, _row(b_gate), wg_sb, wg_ca, w_out), name="mix_bwd", grid=(s // tm,),
        in_specs=[rows(D_MODEL), rows(ATT_WIDTH), rows(ATT_WIDTH), rows(GATE_COLS), whole((1, GATE_COLS)),
                  whole(stack), whole(stack), whole((D_MODEL, D_MODEL))],
        out_specs=[rows(GATE_COLS), rows(ATT_WIDTH), rows(ATT_WIDTH), whole((1, GATE_COLS)),
                   whole((D_MODEL, D_MODEL)), whole(stack), whole(stack)],
        out_shape=[jax.ShapeDtypeStruct((s, GATE_COLS), BF16),
                   jax.ShapeDtypeStruct((s, ATT_WIDTH), BF16), jax.ShapeDtypeStruct((s, ATT_WIDTH), BF16),
                   jax.ShapeDtypeStruct((1, GATE_COLS), F32), jax.ShapeDtypeStruct((D_MODEL, D_MODEL), F32),
                   jax.ShapeDtypeStruct(stack, F32), jax.ShapeDtypeStruct(stack, F32)],
        scratch_shapes=[pltpu.VMEM((ATT_WIDTH, D_MODEL), BF16), pltpu.VMEM((ATT_WIDTH, D_MODEL), BF16),
                        pltpu.VMEM((ATT_WIDTH, D_MODEL), F32), pltpu.VMEM((ATT_WIDTH, D_MODEL), F32)],
        compiler_params=_params(("arbitrary",)),
    )


def _ln_stats(u):
    mu = jnp.mean(u, axis=1, keepdims=True)
    cen = u - mu
    var = jnp.mean(cen * cen, axis=1, keepdims=True)
    rstd = lax.rsqrt(var + LN_EPS)
    return cen * rstd, rstd


def _ln_bwd(dy, xhat, rstd, gain):
    dyg = dy * gain
    m1 = jnp.mean(dyg, axis=1, keepdims=True)
    m2 = jnp.mean(dyg * xhat, axis=1, keepdims=True)
    return rstd * (dyg - m1 - xhat * m2)


def _load_mlp_weights(win_hbm, wout_hbm, win_s, wout_s, sems):
    copies = [pltpu.make_async_copy(win_hbm.at[k], win_s.at[:, pl.ds(FF_SHARD * k, FF_SHARD)], sems.at[k])
              for k in range(N_DEV)]
    copies.append(pltpu.make_async_copy(wout_hbm, wout_s, sems.at[N_DEV]))
    for cp in copies:
        cp.start()
    return copies


MLP_WEIGHT_SCRATCH = [pltpu.VMEM((D_MODEL, D_FF), BF16), pltpu.VMEM((D_FF, D_MODEL), BF16),
                      pltpu.SemaphoreType.DMA((N_DEV + 1,))]


def mlp_block(x, o_sb, o_ca, lg, b_gate, wg_sb, wg_ca, w_proj, target, wg_in, w_out,
              ln1_g, ln1_b, ln2_g, ln2_b, tm=256):
    s = x.shape[0]

    def body(x_ref, osb_ref, oca_ref, lg_ref, bg_ref, wsb_hbm, wca_hbm, wproj_hbm, t_ref, win_hbm, wout_hbm,
             g1_ref, b1_ref, g2_ref, b2_ref,
             a_ref, x1b_ref, dhm_ref, du2b_ref, du1_ref, loss_ref, dg2_ref, db2_ref, dg1_ref, db1_ref,
             win_s, wout_s, sems, relu_s, wsb_s, wca_s, wproj_s, proj_sems):
        @pl.when(pl.program_id(0) == 0)
        def _():
            copies = _load_mlp_weights(win_hbm, wout_hbm, win_s, wout_s, sems)
            for j in range(N_DEV):
                cols = pl.ds(PROJ_SHARD * j, PROJ_SHARD)
                copies.append(pltpu.make_async_copy(wsb_hbm.at[j], wsb_s.at[:, cols], proj_sems.at[j]))
                copies.append(pltpu.make_async_copy(wca_hbm.at[j], wca_s.at[:, cols], proj_sems.at[N_DEV + j]))
            copies.append(pltpu.make_async_copy(wproj_hbm, wproj_s, proj_sems.at[2 * N_DEV]))
            for cp in copies[N_DEV + 1:]:
                cp.start()
            for ref in (loss_ref, dg2_ref, db2_ref, dg1_ref, db1_ref):
                ref[...] = jnp.zeros_like(ref)
            for cp in copies:
                cp.wait()

        merged = _merge(osb_ref[...], oca_ref[...], lg_ref[...], bg_ref[...], wsb_s[...], wca_s[...])[4]
        u1 = ALPHA * x_ref[...] + _dot(merged.astype(BF16), wproj_s[...])
        xhat1, rstd1 = _ln_stats(u1)
        x1 = xhat1 * g1_ref[...] + b1_ref[...]
        x1b = x1.astype(BF16)
        x1b_ref[...] = x1b
        for k in range(N_DEV):
            cols = slice(FF_SHARD * k, FF_SHARD * (k + 1))
            r = jnp.maximum(_dot(x1b, win_s[:, cols]), 0.0)
            relu_s[:, cols] = r
            a_ref[:, cols] = (r * r).astype(BF16)
        xhat2, rstd2 = _ln_stats(ALPHA * x1 + _dot(a_ref[...], wout_s[...]))
        diff = xhat2 * g2_ref[...] + b2_ref[...] - t_ref[...]
        per_token = jnp.mean(diff * diff, axis=1, keepdims=True)
        loss_ref[...] += 0.5 * jnp.sum(per_token, axis=0, keepdims=True)
        dy = diff * (1.0 / D_MODEL)
        dg2_ref[...] += jnp.sum(dy * xhat2, axis=0, keepdims=True)
        db2_ref[...] += jnp.sum(dy, axis=0, keepdims=True)
        du2 = _ln_bwd(dy, xhat2, rstd2, g2_ref[...])
        du2b = du2.astype(BF16)
        du2b_ref[...] = du2b
        for k in range(N_DEV):
            cols = slice(FF_SHARD * k, FF_SHARD * (k + 1))
            da = _dot_nt(du2b, wout_s[cols, :])
            dhm_ref[:, cols] = (2.0 * relu_s[:, cols] * da).astype(BF16)
        dx1 = ALPHA * du2 + _dot_nt(dhm_ref[...], win_s[...])
        dg1_ref[...] += jnp.sum(dx1 * xhat1, axis=0, keepdims=True)
        db1_ref[...] += jnp.sum(dx1, axis=0, keepdims=True)
        du1_ref[...] = _ln_bwd(dx1, xhat1, rstd1, g1_ref[...])

    rows = pl.BlockSpec((tm, D_MODEL), lambda i: (i, 0))
    vec = pl.BlockSpec((1, D_MODEL), lambda i: (0, 0))
    ff = pl.BlockSpec((tm, D_FF), lambda i: (i, 0))
    wide = jax.ShapeDtypeStruct((s, D_FF), BF16)
    narrow = jax.ShapeDtypeStruct((s, D_MODEL), BF16)
    stat = jax.ShapeDtypeStruct((1, D_MODEL), F32)
    tile = lambda w: pl.BlockSpec((tm, w), lambda i: (i, 0))
    return pl.pallas_call(
        body, name="mlp", grid=(s // tm,),
        in_specs=[rows, tile(ATT_WIDTH), tile(ATT_WIDTH), tile(GATE_COLS), pl.BlockSpec((1, GATE_COLS), lambda i: (0, 0)),
                  HBM_SPEC, HBM_SPEC, HBM_SPEC, rows, HBM_SPEC, HBM_SPEC, vec, vec, vec, vec],
        out_specs=[ff, rows, ff, rows, rows, pl.BlockSpec((1, 1), lambda i: (0, 0)), vec, vec, vec, vec],
        out_shape=[wide, narrow, wide, narrow, jax.ShapeDtypeStruct((s, D_MODEL), F32),
                   jax.ShapeDtypeStruct((1, 1), F32), stat, stat, stat, stat],
        scratch_shapes=MLP_WEIGHT_SCRATCH + [
            pltpu.VMEM((tm, D_FF), F32), pltpu.VMEM((ATT_WIDTH, D_MODEL), BF16), pltpu.VMEM((ATT_WIDTH, D_MODEL), BF16),
            pltpu.VMEM((D_MODEL, D_MODEL), BF16), pltpu.SemaphoreType.DMA((2 * N_DEV + 1,))],
        compiler_params=_params(("arbitrary",)),
    )(x, o_sb, o_ca, lg, _row(b_gate), wg_sb, wg_ca, w_proj, target, wg_in, w_out,
      _row(ln1_g), _row(ln1_b), _row(ln2_g), _row(ln2_b))


def dh_slab(pieces, tm=512):
    s = pieces[0].shape[0]
    widths = [p.shape[1] for p in pieces]
    assert sum(widths) == IN_COLS
    n = len(pieces)

    def body(*refs):
        dh_ref = refs[n]
        off = 0
        for ref, w in zip(refs[:n], widths):
            dh_ref[:, off:off + w] = ref[...].astype(BF16)
            off += w

    rows = lambda w: pl.BlockSpec((tm, w), lambda i: (i, 0))
    return pl.pallas_call(
        body, name="dh_slab", grid=(s // tm,),
        in_specs=[rows(w) for w in widths], out_specs=rows(IN_COLS),
        out_shape=jax.ShapeDtypeStruct((s, IN_COLS), BF16),
        compiler_params=_params(("arbitrary",)),
    )(*pieces)


def in_bwd(du1, dh, wg_in, tm=512, guest=None):
    s = du1.shape[0]

    def body(du_ref, dh_ref, w_hbm, gx_ref, pairs_s, sems):
        @pl.when(pl.program_id(0) == 0)
        def _():
            _load_in_pairs(w_hbm, pairs_s, sems)

        acc = ALPHA * du_ref[...]
        for j in range(N_DEV // 2):
            acc = acc + _dot_nt(dh_ref[:, IN_PAIR * j:IN_PAIR * (j + 1)], pairs_s[j])
        gx_ref[...] = acc

    rows = lambda w: pl.BlockSpec((tm, w), lambda i: (i, 0))
    return _pallas(
        body, guest, (du1, dh, wg_in), name="in_bwd", grid=(s // tm,), scratch_shapes=IN_PAIR_SCRATCH,
        in_specs=[rows(D_MODEL), rows(IN_COLS), HBM_SPEC],
        out_specs=[rows(D_MODEL)],
        out_shape=[jax.ShapeDtypeStruct((s, D_MODEL), F32)],
        compiler_params=_params(("arbitrary",)),
    )


def wgrad(name, a, b, tm, tn, shard_cols=None, tk=2048):
    kk, m = a.shape
    n = b.shape[1]
    while kk % tk:
        tk //= 2
    nk = kk // tk
    w = shard_cols

    def body(a_ref, b_ref, out_ref, acc_s):
        k = pl.program_id(2)

        @pl.when(k == 0)
        def _():
            acc_s[...] = jnp.zeros_like(acc_s)

        acc_s[...] += _dot_tn(a_ref[...].astype(BF16), b_ref[...].astype(BF16))

        @pl.when(k == nk - 1)
        def _():
            if w is None:
                out_ref[...] = acc_s[...]
            else:
                for j in range(tn // w):
                    out_ref[j] = acc_s[:, w * j:w * (j + 1)]

    if w is None:
        out_spec = pl.BlockSpec((tm, tn), lambda i, j, k: (i, j))
        out_shape = jax.ShapeDtypeStruct((m, n), F32)
    else:
        out_spec = pl.BlockSpec((tn // w, tm, w), lambda i, j, k: (j, i, 0))
        out_shape = jax.ShapeDtypeStruct((n // w, m, w), F32)
    return pl.pallas_call(
        body, name=name, grid=(m // tm, n // tn, nk),
        in_specs=[pl.BlockSpec((tk, tm), lambda i, j, k: (k, i)),
                  pl.BlockSpec((tk, tn), lambda i, j, k: (k, j))],
        out_specs=out_spec, out_shape=out_shape,
        scratch_shapes=[pltpu.VMEM((tm, tn), F32)],
        compiler_params=_params(("arbitrary", "arbitrary", "arbitrary")),
    )(a, b)


def adamw(name, w, m, v, parts, tm=None, guest=None):
    rows, cols = w.shape
    tm = rows if tm is None else min(tm, rows)
    n = len(parts)
    c1 = 1.0 - ADAM_B1 ** ADAM_STEP
    c2 = 1.0 - ADAM_B2 ** ADAM_STEP

    def body(*refs):
        w_ref, m_ref, v_ref = refs[:3]
        part_refs = refs[3:3 + n]
        g_ref, d_ref, nm_ref, nv_ref = refs[3 + n:]
        g = None
        for ref, (_, index) in zip(part_refs, parts):
            term = (ref[...] if index is None else ref[0]).astype(F32)
            g = term if g is None else g + term
        new_m = ADAM_B1 * m_ref[...] + (1.0 - ADAM_B1) * g
        new_v = ADAM_B2 * v_ref[...] + (1.0 - ADAM_B2) * (g * g)
        m_hat = new_m / c1
        v_hat = new_v / c2
        g_ref[...] = g
        d_ref[...] = -ADAM_LR * (m_hat / (jnp.sqrt(v_hat) + ADAM_EPS) + ADAM_WD * w_ref[...])
        nm_ref[...] = new_m
        nv_ref[...] = new_v

    spec = pl.BlockSpec((tm, cols), lambda i: (i, 0))
    shape = jax.ShapeDtypeStruct((rows, cols), F32)

    def part_spec(index):
        if index is None:
            return spec
        return pl.BlockSpec((1, tm, cols), lambda i: (index, i, 0))

    return _pallas(
        body, guest, (w, m, v, *[array for array, _ in parts]), name=name, grid=(rows // tm,),
        in_specs=[spec] * 3 + [part_spec(index) for _, index in parts],
        out_specs=[spec] * 4, out_shape=[shape] * 4,
        compiler_params=_params(("arbitrary",)),
    )


def _place():
    x, y, c = lax.axis_index("x"), lax.axis_index("y"), lax.axis_index("c")
    other_chips = [(1 - x, y), (x, 1 - y), (1 - x, 1 - y)]
    return x, y, c, other_chips


def place_shards(arrays):
    n = len(arrays)

    def body(*refs):
        ins, outs, stage, sems = refs[:n], refs[n:2 * n], refs[2 * n:3 * n], refs[3 * n]
        x, y, c, _ = _place()
        copies = []
        for a in range(n):
            stage[a][...] = ins[a][...].astype(BF16)
            copies.append(pltpu.make_async_copy(stage[a], outs[a].at[4 * x + 2 * y + c], sems.at[a]))
            copies[-1].start()
        for cp in copies:
            cp.wait()

    return pl.pallas_call(
        body, name="place_shards",
        in_specs=[pl.BlockSpec(memory_space=pltpu.VMEM)] * n, out_specs=[HBM_SPEC] * n,
        out_shape=[jax.ShapeDtypeStruct((N_DEV,) + a.shape, BF16) for a in arrays],
        scratch_shapes=[pltpu.VMEM(a.shape, BF16) for a in arrays] + [pltpu.SemaphoreType.DMA((n,))],
        compiler_params=_params(),
    )(*arrays)


def all_gather(name, arrays):
    n = len(arrays)

    def body(*refs):
        ins, outs = refs[:n], refs[n:2 * n]
        send_sems, recv_sems, local_sems = refs[2 * n:]
        x, y, c, chips = _place()
        me = 4 * x + 2 * y + c
        sibling = (x, y, 1 - c)

        def copy(a, k, block, to, src=None):
            return pltpu.make_async_remote_copy(
                src_ref=outs[a].at[block] if src is None else src, dst_ref=outs[a].at[block],
                send_sem=send_sems.at[a, k], recv_sem=recv_sems.at[a, k],
                device_id=to, device_id_type=MESH)

        started = []
        local = [pltpu.make_async_copy(ins[a], outs[a].at[me], local_sems.at[a]) for a in range(n)]
        for a in range(n):
            local[a].start()
            sends = [copy(a, 0, me, sibling, src=ins[a])]
            sends += [copy(a, 1 + j, me, (cx, cy, c), src=ins[a]) for j, (cx, cy) in enumerate(chips)]
            for cp in sends:
                cp.start()
            started += sends
        for a in range(n):
            for j, (cx, cy) in enumerate(chips):
                block = 4 * cx + 2 * cy + c
                copy(a, 1 + j, block, sibling).wait_recv()
                passed = copy(a, 4 + j, block, sibling)
                passed.start()
                started.append(passed)
        for a in range(n):
            copy(a, 0, 4 * x + 2 * y + (1 - c), sibling).wait_recv()
            for j, (cx, cy) in enumerate(chips):
                copy(a, 4 + j, 4 * cx + 2 * cy + (1 - c), sibling).wait_recv()
        for cp in started:
            cp.wait_send()
        for cp in local:
            cp.wait()

    return pl.pallas_call(
        body, name=name,
        in_specs=[pl.BlockSpec(memory_space=pltpu.VMEM)] * n, out_specs=[HBM_SPEC] * n,
        out_shape=[jax.ShapeDtypeStruct((N_DEV,) + a.shape, a.dtype) for a in arrays],
        scratch_shapes=[pltpu.SemaphoreType.DMA((n, 7)), pltpu.SemaphoreType.DMA((n, 7)),
                        pltpu.SemaphoreType.DMA((n,))],
    )(*arrays)


def _gather_first_plan(n):
    def plan(zones):
        x, y, c, chips = _place()
        me = 4 * x + 2 * y + c
        copies = []
        for zone in zones[:n]:
            copies.append((zone.at[me], zone.at[me], (x, y, 1 - c), zone.at[4 * x + 2 * y + (1 - c)]))
            copies += [(zone.at[me], zone.at[me], (cx, cy, c), zone.at[4 * cx + 2 * cy + c]) for cx, cy in chips]
        return copies
    return plan


def _gather_second_plan(n):
    def plan(zones):
        x, y, c, chips = _place()
        return [(zone.at[4 * cx + 2 * cy + c], zone.at[4 * cx + 2 * cy + c], (x, y, 1 - c),
                 zone.at[4 * cx + 2 * cy + (1 - c)]) for zone in zones[:n] for cx, cy in chips]
    return plan


def _both_plans(first, n_first, second):
    return lambda bufs: first(bufs[:n_first]) + second(bufs[n_first:])


def _sibling_plan(n):
    def plan(bufs):
        x, y, c, _ = _place()
        return [(bufs[a].at[:, 1 - c], bufs[n + a], (x, y, 1 - c), bufs[n + a]) for a in range(n)]
    return plan


def _chips_plan(n):
    def plan(bufs):
        _, _, c, chips = _place()
        return [(bufs[a].at[2 * cx + cy], bufs[n + a].at[j], (cx, cy, c), bufs[n + a].at[j])
                for a in range(n) for j, (cx, cy) in enumerate(chips)]
    return plan


def chip_sum(name, grad, landed, place, tr=256):
    _, _, rows, cols = grad.shape
    tr = min(tr, rows)

    def body(place_ref, g_ref, l_ref, own_ref, out_ref):
        total = g_ref[0, 0] + l_ref[0]
        out_ref[0] = total.astype(BF16)

        @pl.when(pl.program_id(1) == place_ref[1])
        def _():
            own_ref[...] = total

    return pl.pallas_call(
        body, name=name,
        grid_spec=pltpu.PrefetchScalarGridSpec(
            num_scalar_prefetch=1, grid=(rows // tr, 4),
            in_specs=[pl.BlockSpec((1, 1, tr, cols), lambda i, ch, pos: (ch, pos[0], i, 0)),
                      pl.BlockSpec((1, tr, cols), lambda i, ch, pos: (ch, i, 0))],
            out_specs=[pl.BlockSpec((tr, cols), lambda i, ch, pos: (i, 0)),
                       pl.BlockSpec((1, tr, cols), lambda i, ch, pos: (ch, i, 0))]),
        out_shape=[jax.ShapeDtypeStruct((rows, cols), F32), jax.ShapeDtypeStruct((4, rows, cols), BF16)],
        compiler_params=_params(("arbitrary", "arbitrary")),
    )(place, grad, landed)


BIG =("w_in", "w_sb_proj", "w_ca_proj", "w_out", "w_mlp_in", "w_mlp_out")
SMALL = ("b_gate", "rel_bias", "ln1_g", "ln1_b", "ln2_g", "ln2_b")
NAMES = ("w_in", "b_gate", "w_sb_proj", "w_ca_proj", "rel_bias", "w_out", "ln1_g", "ln1_b",
         "w_mlp_in", "w_mlp_out", "ln2_g", "ln2_b")
REL_COLS = 2 * REL_CLIP + 1


def _pack_small(t, scalar=None):
    rel = t["rel_bias"]
    if rel.shape[1] != REL_PAD:
        rel = jnp.pad(rel, ((0, 0), (0, REL_PAD - rel.shape[1])))
    last = jnp.zeros((128,), F32) if scalar is None else jnp.pad(scalar.reshape(1), (0, 127))
    flat = [t["b_gate"].reshape(-1), rel.reshape(-1)] + [t[n].reshape(-1) for n in SMALL[2:]] + [last]
    return jnp.concatenate(flat).reshape(-1, 128)


def _unpack_small(p):
    flat = p.reshape(-1)
    out, off = {}, 0
    for n, size in (("b_gate", GATE_COLS), ("rel_bias", 8 * REL_PAD)) + tuple((n, D_MODEL) for n in SMALL[2:]):
        out[n] = flat[off:off + size]
        off += size
    out["rel_bias"] = out["rel_bias"].reshape(8, REL_PAD)[:, :REL_COLS]
    return out


def kernel(x, w_in, b_gate, w_sb_proj, w_ca_proj, rel_bias, w_out, ln1_g, ln1_b, w_mlp_in, w_mlp_out, ln2_g, ln2_b, loss_target, m_w_in, m_b_gate, m_w_sb_proj, m_w_ca_proj, m_rel_bias, m_w_out, m_ln1_g, m_ln1_b, m_w_mlp_in, m_w_mlp_out, m_ln2_g, m_ln2_b, v_w_in, v_b_gate, v_w_sb_proj, v_w_ca_proj, v_rel_bias, v_w_out, v_ln1_g, v_ln1_b, v_w_mlp_in, v_w_mlp_out, v_ln2_g, v_ln2_b):
    w = dict(w_in=w_in, b_gate=b_gate, w_sb_proj=w_sb_proj, w_ca_proj=w_ca_proj, rel_bias=rel_bias, w_out=w_out,
             ln1_g=ln1_g, ln1_b=ln1_b, w_mlp_in=w_mlp_in, w_mlp_out=w_mlp_out, ln2_g=ln2_g, ln2_b=ln2_b)
    m = dict(w_in=m_w_in, b_gate=m_b_gate, w_sb_proj=m_w_sb_proj, w_ca_proj=m_w_ca_proj, rel_bias=m_rel_bias,
             w_out=m_w_out, ln1_g=m_ln1_g, ln1_b=m_ln1_b, w_mlp_in=m_w_mlp_in, w_mlp_out=m_w_mlp_out,
             ln2_g=m_ln2_g, ln2_b=m_ln2_b)
    v = dict(w_in=v_w_in, b_gate=v_b_gate, w_sb_proj=v_w_sb_proj, w_ca_proj=v_w_ca_proj, rel_bias=v_rel_bias,
             w_out=v_w_out, ln1_g=v_ln1_g, ln1_b=v_ln1_b, w_mlp_in=v_w_mlp_in, w_mlp_out=v_w_mlp_out,
             ln2_g=v_ln2_g, ln2_b=v_ln2_b)

    x, target = x[0], loss_target[0]
    c = lax.axis_index("c")
    place = jnp.stack([c, 2 * lax.axis_index("x") + lax.axis_index("y")]).astype(jnp.int32)
    grads, deltas, new_m, new_v = {}, {}, {}, {}

    def by_owner(n, g):
        return g.reshape((4, 2) + w[n].shape)

    def finish(n, own, arrived, guest=None):
        out = adamw("adamw_" + n, w[n], m[n], v[n], [(own, None), (arrived, 0), (arrived, 1), (arrived, 2)],
                    tm=256, guest=guest)
        out, bufs = (out, None) if guest is None else out
        grads[n], deltas[n], new_m[n], new_v[n] = out
        return bufs

    def siblings(names, g):
        bufs = [by_owner(n, g[n]) for n in names] + [lax.empty((4,) + w[n].shape, F32) for n in names]
        return Exchange(tuple(bufs), _sibling_plan(len(names)), len(names))

    def chip_sums(names, bufs):
        k = len(names)
        own, sums = zip(*[chip_sum("chip_sum_" + n, bufs[a], bufs[k + a], place) for a, n in enumerate(names)])
        zones = [lax.empty((3,) + w[n].shape, BF16) for n in names]
        return own, Exchange(tuple(sums) + tuple(zones), _chips_plan(k), 3 * k)

    proj, ffn = ("w_sb_proj", "w_ca_proj", "w_out"), ("w_mlp_in", "w_mlp_out")
    zones = place_shards([w[n] for n in proj + ffn])
    proj_zones, ffn_zones = tuple(zones[:len(proj)]), tuple(zones[len(proj):])
    wg_in = all_gather("gather_w_in", [w_in.astype(BF16)])[0]
    (qkv, lg, xb), proj_zones = in_proj(
        x, wg_in, guest=Exchange(proj_zones, _gather_first_plan(len(proj)), 4 * len(proj)))
    bias_tiles = ca_bias_tiles(rel_bias)
    both = Exchange(ffn_zones + tuple(proj_zones),
                    _both_plans(_gather_first_plan(len(ffn)), len(ffn), _gather_second_plan(len(proj))),
                    4 * len(ffn) + 3 * len(proj))
    (o_sb,), zones = sb_fwd(qkv, both)
    ffn_zones, proj_zones = tuple(zones[:len(ffn)]), zones[len(ffn):]
    (o_ca,), ffn_zones = ca_fwd(qkv, bias_tiles, Exchange(ffn_zones, _gather_second_plan(len(ffn)), 3 * len(ffn)))
    wg = dict(zip(proj + ffn, list(proj_zones) + list(ffn_zones)))
    wg_sb, wg_ca, wg_mi = wg["w_sb_proj"], wg["w_ca_proj"], wg["w_mlp_in"]
    wg_mo = wg["w_mlp_out"].reshape(D_FF, D_MODEL)
    w_out_b = wg["w_out"].reshape(D_MODEL, D_MODEL)

    a, x1b, dhm, du2b, du1, loss, dg2, db2, dg1, db1 = mlp_block(
        x, o_sb, o_ca, lg, b_gate, wg_sb, wg_ca, w_out_b, target, wg_mi, wg_mo, ln1_g, ln1_b, ln2_g, ln2_b)
    g = {}
    g["w_mlp_in"] = wgrad("wgrad_mlp_in", x1b, dhm, tm=D_MODEL, tn=FF_SHARD, shard_cols=FF_SHARD, tk=x.shape[0])
    g["w_mlp_out"] = wgrad("wgrad_mlp_out", a, du2b, tm=FF_SHARD, tn=D_MODEL, tk=x.shape[0])

    mlp = ("w_mlp_in", "w_mlp_out")
    (dlg, do_sb, do_ca, dbg, g["w_out"], g["w_sb_proj"], g["w_ca_proj"]), bufs = mix_bwd(
        du1, o_sb, o_ca, lg, b_gate, wg_sb, wg_ca, w_out_b, guest=siblings(mlp, g))
    own_mlp, mlp_chips = chip_sums(mlp, bufs)

    att = ("w_out", "w_sb_proj", "w_ca_proj")
    att_siblings = siblings(att, g)
    both = Exchange(mlp_chips.bufs + att_siblings.bufs,
                    _both_plans(mlp_chips.plan, len(mlp_chips.bufs), att_siblings.plan),
                    mlp_chips.copies + att_siblings.copies)
    (dq_sb, dk_sb, dv_sb), bufs = sb_bwd(qkv, o_sb, do_sb, guest=both)
    arrived_mlp, bufs = bufs[len(mlp):len(mlp_chips.bufs)], bufs[len(mlp_chips.bufs):]
    own_att, att_chips = chip_sums(att, bufs)
    (dq_ca, dk_ca, dv_ca, db_tiles), bufs = ca_bwd(qkv, bias_tiles, do_ca, guest=att_chips)
    arrived_att = bufs[len(att):]
    g_rel = rel_bias_grad(db_tiles)

    dh = dh_slab([dq_sb, dk_sb, dv_sb, dq_ca, dk_ca, dv_ca, dlg])
    g["w_in"] = wgrad("wgrad_in", xb, dh, tm=D_MODEL, tn=2 * IN_SHARD, shard_cols=IN_SHARD)
    last = ("w_in",)
    bufs = finish(mlp[0], own_mlp[0], arrived_mlp[0], guest=siblings(last, g))
    finish(mlp[1], own_mlp[1], arrived_mlp[1])
    for n, own, arrived in zip(att, own_att, arrived_att):
        finish(n, own, arrived)
    own_in, in_chips = chip_sums(last, bufs)
    (grad_x,), bufs = in_bwd(du1, dh, wg_in, guest=in_chips)
    finish(last[0], own_in[0], bufs[1])

    small = dict(b_gate=dbg, rel_bias=g_rel, ln1_g=dg1, ln1_b=db1, ln2_g=dg2, ln2_b=db2)
    parts = all_gather("gather_small_grads", [_pack_small(small, loss[0, 0])])[0]
    packed = adamw("adamw_small", _pack_small(w), _pack_small(m), _pack_small(v),
                   [(parts, d) for d in range(N_DEV)])
    loss = packed[0][-1, 0]
    for out, p in zip((grads, deltas, new_m, new_v), packed):
        for n, val in _unpack_small(p).items():
            out[n] = val.reshape(w[n].shape)

    return (loss, grad_x[None], *[grads[n] for n in NAMES], *[deltas[n] for n in NAMES],
            *[new_m[n] for n in NAMES], *[new_v[n] for n in NAMES])
```

```python
import functools
from typing import NamedTuple

import jax
import jax.numpy as jnp
from jax import lax
from jax.experimental import pallas as pl
from jax.experimental.pallas import tpu as pltpu

F32 = jnp.float32
BF16 = jnp.bfloat16
MESH = pl.DeviceIdType.MESH

N_DEV = 8
D_MODEL = 1024
HEAD_DIM = 64
ATT_WIDTH = 512
N_PAIRS = ATT_WIDTH // 128
QKV_COLS = 6 * ATT_WIDTH
GATE_COLS = 2 * D_MODEL
IN_COLS = QKV_COLS + GATE_COLS
IN_SHARD = IN_COLS // N_DEV
D_FF = 4 * D_MODEL
FF_SHARD = D_FF // N_DEV
PROJ_SHARD = D_MODEL // N_DEV
ATT_BLOCK = 128
CA_TILES = 5
CA_ROWS = 2048
CHUNK = 64
CA_PREV_CHUNKS = 8
REL_CLIP = 256
REL_PAD = 640
ALPHA = 2.0 ** 0.25
LN_EPS = 1e-5
QK_SCALE = HEAD_DIM ** -0.5
NEG_BIG = -1e30
VMEM_LIMIT = 56 * 1024 * 1024

ADAM_LR = 0.001
ADAM_B1 = 0.9
ADAM_B2 = 0.999
ADAM_EPS = 1e-08
ADAM_WD = 0.01
ADAM_STEP = 10

_NT = (((1,), (1,)), ((), ()))
_TN = (((0,), (0,)), ((), ()))


def _dot(a, b):
    return jnp.dot(a, b, preferred_element_type=F32)


def _dot_nt(a, b):
    return lax.dot_general(a, b, _NT, preferred_element_type=F32)


def _dot_tn(a, b):
    return lax.dot_general(a, b, _TN, preferred_element_type=F32)


def _params(semantics=None):
    return pltpu.CompilerParams(dimension_semantics=semantics, vmem_limit_bytes=VMEM_LIMIT)


def _row(v):
    return v.reshape(1, -1)


HBM_SPEC = pl.BlockSpec(memory_space=pl.ANY)


class Exchange(NamedTuple):
    bufs: tuple
    plan: object
    copies: int


def _pallas(body, guest, args, *, grid, in_specs, out_specs, out_shape, scratch_shapes=(), **kw):
    out_specs, out_shape, scratch_shapes = list(out_specs), list(out_shape), list(scratch_shapes)
    if guest is None:
        return pl.pallas_call(body, grid=grid, in_specs=in_specs, out_specs=out_specs, out_shape=out_shape,
                              scratch_shapes=scratch_shapes, **kw)(*args)
    nb, n_own = len(guest.bufs), len(in_specs) + len(out_specs)

    def hosting(*refs):
        zones = refs[nb + n_own:2 * nb + n_own]
        send_sems, recv_sems = refs[-2], refs[-1]
        steps = [pl.program_id(axis) for axis in range(len(grid))]
        first = functools.reduce(jnp.logical_and, [step == 0 for step in steps])
        last = functools.reduce(jnp.logical_and, [step == n - 1 for step, n in zip(steps, grid)])

        def copy(k, src, dst, peer):
            return pltpu.make_async_remote_copy(src_ref=src, dst_ref=dst, send_sem=send_sems.at[k],
                                                recv_sem=recv_sems.at[k], device_id=peer, device_id_type=MESH)

        @pl.when(first)
        def _():
            for k, (src, dst, peer, _) in enumerate(guest.plan(zones)):
                copy(k, src, dst, peer).start()

        body(*refs[nb:nb + n_own], *refs[2 * nb + n_own:-2])

        @pl.when(last)
        def _():
            for k, (src, dst, peer, landed) in enumerate(guest.plan(zones)):
                copy(k, src, dst, peer).wait_send()
                copy(k, src, landed, peer).wait_recv()

    sems = pltpu.SemaphoreType.DMA((guest.copies,))
    results = pl.pallas_call(
        hosting, grid=grid, in_specs=[HBM_SPEC] * nb + list(in_specs), out_specs=out_specs + [HBM_SPEC] * nb,
        out_shape=out_shape + [jax.ShapeDtypeStruct(b.shape, b.dtype) for b in guest.bufs],
        input_output_aliases={k: len(out_specs) + k for k in range(nb)},
        scratch_shapes=scratch_shapes + [sems, sems], **kw)(*guest.bufs, *args)
    return results[:len(out_specs)], results[len(out_specs):]


IN_PAIR = 2 * IN_SHARD
IN_PAIR_SCRATCH = [pltpu.VMEM((N_DEV // 2, D_MODEL, IN_PAIR), BF16), pltpu.SemaphoreType.DMA((N_DEV,))]


def _load_in_pairs(w_hbm, pairs_s, sems):
    copies = [pltpu.make_async_copy(w_hbm.at[j], pairs_s.at[j // 2, :, pl.ds(IN_SHARD * (j % 2), IN_SHARD)],
                                    sems.at[j]) for j in range(N_DEV)]
    for cp in copies:
        cp.start()
    for cp in copies:
        cp.wait()


def in_proj(x, wg_in, tm=512, guest=None):
    s = x.shape[0]

    def body(x_ref, w_hbm, qkv_ref, lg_ref, xb_ref, pairs_s, sems):
        @pl.when(pl.program_id(0) == 0)
        def _():
            _load_in_pairs(w_hbm, pairs_s, sems)

        xb = x_ref[...].astype(BF16)
        xb_ref[...] = xb
        for j in range(N_DEV // 2):
            acc = _dot(xb, pairs_s[j])
            lo, hi = IN_PAIR * j, IN_PAIR * (j + 1)
            if hi <= QKV_COLS:
                qkv_ref[:, lo:hi] = acc.astype(BF16)
            elif lo >= QKV_COLS:
                lg_ref[:, lo - QKV_COLS:hi - QKV_COLS] = acc
            else:
                qkv_ref[:, lo:QKV_COLS] = acc[:, :QKV_COLS - lo].astype(BF16)
                lg_ref[:, 0:hi - QKV_COLS] = acc[:, QKV_COLS - lo:]

    return _pallas(
        body, guest, (x, wg_in), name="in_proj", grid=(s // tm,), scratch_shapes=IN_PAIR_SCRATCH,
        in_specs=[pl.BlockSpec((tm, D_MODEL), lambda i: (i, 0)), HBM_SPEC],
        out_specs=[pl.BlockSpec((tm, QKV_COLS), lambda i: (i, 0)),
                   pl.BlockSpec((tm, GATE_COLS), lambda i: (i, 0)),
                   pl.BlockSpec((tm, D_MODEL), lambda i: (i, 0))],
        out_shape=[jax.ShapeDtypeStruct((s, QKV_COLS), BF16),
                   jax.ShapeDtypeStruct((s, GATE_COLS), F32),
                   jax.ShapeDtypeStruct((s, D_MODEL), BF16)],
        compiler_params=_params(("arbitrary",)),
    )


def _head_masks():
    lane = lax.broadcasted_iota(jnp.int32, (1, 128), 1)
    first = lane < HEAD_DIM
    return first, jnp.logical_not(first)


SB_CHUNK = 512
SB_SUB = SB_CHUNK // ATT_BLOCK


def _tri_pair():
    row = lax.broadcasted_iota(jnp.int32, (4 * ATT_BLOCK, 2 * ATT_BLOCK), 0) & (2 * ATT_BLOCK - 1)
    col = lax.broadcasted_iota(jnp.int32, (4 * ATT_BLOCK, 2 * ATT_BLOCK), 1)
    same = (row >= ATT_BLOCK) == (col >= ATT_BLOCK)
    return jnp.logical_and(same, row >= col).astype(BF16)


def _pair_bf16(pieces):
    return jnp.concatenate(
        [jnp.concatenate([p.astype(BF16) for p in row], axis=1) for row in pieces], axis=0)


def _suffix_sums(pieces, tri_pair):
    hi = [[p.astype(BF16) for p in row] for row in pieces]
    lo = [[p - h.astype(F32) for p, h in zip(row, hrow)] for row, hrow in zip(pieces, hi)]
    return _dot(jnp.concatenate([_pair_bf16(hi), _pair_bf16(lo)], axis=1), tri_pair)


def _first_row_piece(diag, j):
    return j if diag else 0


def _piece(v, r, h):
    return v[ATT_BLOCK * r:ATT_BLOCK * (r + 1), ATT_BLOCK * h:ATT_BLOCK * (h + 1)]


def _all_row_pieces(slabs):
    return [[slab[ATT_BLOCK * r:ATT_BLOCK * (r + 1)] for r in range(SB_SUB)] for slab in slabs]


def _add_rows(total, part, r0):
    if r0 == 0:
        return total + part
    return jnp.concatenate([total[:ATT_BLOCK * r0], total[ATT_BLOCK * r0:] + part], axis=0)


def _row_totals(csum, r0, h):
    return jnp.concatenate(
        [jnp.broadcast_to(_piece(csum, at, h)[:, 0:1], (ATT_BLOCK, ATT_BLOCK)) for at in range(SB_SUB - r0)], axis=0)


SIGN_BIT = 0x80000000


def _drop(z):
    minus_abs = lax.bitcast_convert_type(lax.bitcast_convert_type(z, jnp.uint32) | jnp.uint32(SIGN_BIT), F32)
    return jnp.maximum(z, 0.0) + jnp.log(1.0 + jnp.exp(minus_abs))


def _sb_weights(z, runs, r0, masked, strict, tri_pair):
    drops = []
    for r in range(r0, SB_SUB):
        row = [_drop(_piece(z, r - r0, h)) for h in range(2)]
        drops.append([jnp.where(strict, drop, 0.0) for drop in row] if r == masked else row)
    csum = _suffix_sums(drops, tri_pair)
    weights = []
    for r in range(r0, SB_SUB):
        row = [jnp.exp(_piece(z, r - r0, h) - (_piece(csum, r - r0, h) + runs[h][r])) for h in range(2)]
        weights.append([jnp.where(strict, a, 0.0) for a in row] if r == masked else row)
    return drops, weights, csum


EXP_UNDERFLOW = 104.0


SB_STEP = 2


def _sweep_left(span, i, dropped_ref):
    def more(at):
        return jnp.logical_and(at >= 0, jnp.min(dropped_ref[...]) < EXP_UNDERFLOW)

    def step(at):
        span(at * (SB_STEP * ATT_BLOCK), SB_STEP, False)
        return at - 1

    lax.while_loop(more, step, i * (SB_SUB // SB_STEP) - 1)


def _by_head(blk, masks):
    zero = jnp.zeros_like(blk)
    return jnp.concatenate([jnp.where(m, blk, zero) for m in masks], axis=0)


def sb_fwd(qkv, guest=None):
    s = qkv.shape[0]
    nq = s // SB_CHUNK

    def body(q_ref, k_ref, v_ref, o_ref, run_s):
        i = pl.program_id(1)
        masks = _head_masks()
        q = q_ref[...] * QK_SCALE
        row = lax.broadcasted_iota(jnp.int32, (ATT_BLOCK, ATT_BLOCK), 0)
        col = lax.broadcasted_iota(jnp.int32, (ATT_BLOCK, ATT_BLOCK), 1)
        strict = col < row
        tri_pair = _tri_pair()
        run_s[...] = jnp.zeros_like(run_s)
        o_ref[...] = jnp.zeros_like(o_ref)

        def span(start, pieces, diag):
            runs = [run_s[0], run_s[1]]
            acc = jnp.zeros((SB_CHUNK, ATT_BLOCK), F32)
            for j in reversed(range(pieces)):
                r0 = _first_row_piece(diag, j)
                keys = pl.ds(pl.multiple_of(start + ATT_BLOCK * j, ATT_BLOCK), ATT_BLOCK)
                z = _dot_nt(q[ATT_BLOCK * r0:], _by_head(k_ref[keys, :], masks))
                _, weights, csum = _sb_weights(z, _all_row_pieces(runs), r0, j if diag else None, strict, tri_pair)
                part = _dot(_pair_bf16(weights), _by_head(v_ref[keys, :], masks))
                acc = _add_rows(acc, part, r0)
                runs = [_add_rows(runs[h], _row_totals(csum, r0, h), r0) for h in range(2)]
            run_s[0], run_s[1] = runs
            o_ref[...] += acc

        span(i * SB_CHUNK, SB_SUB, True)
        _sweep_left(span, i, run_s)

    return _pallas(
        body, guest, (qkv, qkv, qkv), name="sb_fwd", grid=(N_PAIRS, nq),
        in_specs=[pl.BlockSpec((SB_CHUNK, 128), lambda p, i: (i, p)),
                  pl.BlockSpec((s, 128), lambda p, i: (0, N_PAIRS + p)),
                  pl.BlockSpec((s, 128), lambda p, i: (0, 2 * N_PAIRS + p))],
        out_specs=[pl.BlockSpec((SB_CHUNK, 128), lambda p, i: (i, p))],
        out_shape=[jax.ShapeDtypeStruct((s, ATT_WIDTH), F32)],
        scratch_shapes=[pltpu.VMEM((2, SB_CHUNK, 128), F32)],
        compiler_params=_params(("arbitrary", "arbitrary")),
    )


def sb_bwd(qkv, o_sb, do_sb, guest=None):
    s = qkv.shape[0]
    nq = s // SB_CHUNK

    def body(q_ref, k_ref, v_ref, o_ref, do_ref, dq_ref, dk_ref, dv_ref, run_l_s, run_e_s, etot_s, dq_s, dk_s, dv_s):
        i = pl.program_id(1)

        @pl.when(i == 0)
        def _():
            dk_s[...] = jnp.zeros_like(dk_s)
            dv_s[...] = jnp.zeros_like(dv_s)

        masks = _head_masks()
        q = q_ref[...] * QK_SCALE
        do = do_ref[...]
        prod = do.astype(F32) * o_ref[...]
        for h in range(2):
            total = jnp.sum(jnp.where(masks[h], prod, 0.0), axis=1, keepdims=True)
            etot_s[h] = jnp.broadcast_to(total, (SB_CHUNK, ATT_BLOCK))
        row = lax.broadcasted_iota(jnp.int32, (ATT_BLOCK, ATT_BLOCK), 0)
        col = lax.broadcasted_iota(jnp.int32, (ATT_BLOCK, ATT_BLOCK), 1)
        strict = col < row
        tri_pair = _tri_pair()
        run_l_s[...] = jnp.zeros_like(run_l_s)
        run_e_s[...] = jnp.zeros_like(run_e_s)
        dq_s[...] = jnp.zeros_like(dq_s)

        def span(start, pieces, diag):
            slabs_l = [run_l_s[0], run_l_s[1]]
            slabs_e = [run_e_s[0], run_e_s[1]]
            e_tot = _all_row_pieces([etot_s[0], etot_s[1]])
            dq = jnp.zeros((SB_CHUNK, ATT_BLOCK), F32)
            for j in reversed(range(pieces)):
                runs_l, runs_e = _all_row_pieces(slabs_l), _all_row_pieces(slabs_e)
                r0 = _first_row_piece(diag, j)
                masked = j if diag else None
                keys = pl.ds(pl.multiple_of(start + ATT_BLOCK * j, ATT_BLOCK), ATT_BLOCK)
                kk = _by_head(k_ref[keys, :], masks)
                vv = _by_head(v_ref[keys, :], masks)
                q_rows, do_rows = q[ATT_BLOCK * r0:], do[ATT_BLOCK * r0:]
                z = _dot_nt(q_rows, kk)
                da = _dot_nt(do_rows, vv)
                drops, weights, csum = _sb_weights(z, runs_l, r0, masked, strict, tri_pair)
                ab = [[a.astype(BF16) for a in wrow] for wrow in weights]
                es = [[a.astype(F32) * _piece(da, at, h) for h, a in enumerate(arow)] for at, arow in enumerate(ab)]
                esum = _suffix_sums(es, tri_pair)
                dzs = []
                for at, r in enumerate(range(r0, SB_SUB)):
                    dz_row = []
                    for h in range(2):
                        sig = jnp.exp(_piece(z, at, h) - drops[at][h])
                        after = _piece(esum, at, h) + runs_e[h][r] - es[at][h]
                        dz = es[at][h] - sig * (e_tot[h][r] - after)
                        dz_row.append(jnp.where(strict, dz, 0.0) if r == masked else dz)
                    dzs.append(dz_row)
                dz = _pair_bf16(dzs)
                dq = _add_rows(dq, _dot(dz, kk), r0)
                dk = _dot_tn(dz, q_rows)
                dv = _dot_tn(_pair_bf16(ab), do_rows)
                dk_s[keys, :] += jnp.where(masks[0], dk[:ATT_BLOCK], dk[ATT_BLOCK:])
                dv_s[keys, :] += jnp.where(masks[0], dv[:ATT_BLOCK], dv[ATT_BLOCK:])
                slabs_l =[_add_rows(slabs_l[h], _row_totals(csum, r0, h), r0) for h in range(2)]
                slabs_e = [_add_rows(slabs_e[h], _row_totals(esum, r0, h), r0) for h in range(2)]
            run_l_s[0], run_l_s[1] = slabs_l
            run_e_s[0], run_e_s[1] = slabs_e
            dq_s[...] += dq

        span(i * SB_CHUNK, SB_SUB, True)
        _sweep_left(span, i, run_l_s)
        dq_ref[...] = (dq_s[...] * QK_SCALE).astype(BF16)

        @pl.when(i == nq - 1)
        def _():
            dk_ref[...] = dk_s[...].astype(BF16)
            dv_ref[...] = dv_s[...].astype(BF16)

    tile = lambda p, i: (i, p)
    state = pltpu.VMEM((2, SB_CHUNK, 128), F32)
    sums = pltpu.VMEM((s, 128), F32)
    return _pallas(
        body, guest, (qkv, qkv, qkv, o_sb, do_sb), name="sb_bwd", grid=(N_PAIRS, nq),
        in_specs=[pl.BlockSpec((SB_CHUNK, 128), tile),
                  pl.BlockSpec((s, 128), lambda p, i: (0, N_PAIRS + p)),
                  pl.BlockSpec((s, 128), lambda p, i: (0, 2 * N_PAIRS + p)),
                  pl.BlockSpec((SB_CHUNK, 128), tile),
                  pl.BlockSpec((SB_CHUNK, 128), tile)],
        out_specs=[pl.BlockSpec((SB_CHUNK, 128), tile),
                   pl.BlockSpec((s, 128), lambda p, i: (0, p)),
                   pl.BlockSpec((s, 128), lambda p, i: (0, p))],
        out_shape=[jax.ShapeDtypeStruct((s, ATT_WIDTH), BF16)] * 3,
        scratch_shapes=[state, state, state, pltpu.VMEM((SB_CHUNK, 128), F32), sums, sums],
        compiler_params=_params(("arbitrary", "arbitrary")),
    )


def ca_bias_tiles(rel_bias):
    heads = rel_bias.shape[0]
    wide = 2 * ATT_BLOCK

    def body(rel_ref, out_ref):
        bins = lax.broadcasted_iota(jnp.int32, (REL_PAD, wide), 0)
        pos = lax.broadcasted_iota(jnp.int32, (REL_PAD, wide), 1)
        r = lax.broadcasted_iota(jnp.int32, (ATT_BLOCK, ATT_BLOCK), 0)
        c = lax.broadcasted_iota(jnp.int32, (ATT_BLOCK, ATT_BLOCK), 1)
        for j in range(CA_TILES):
            hot = (jnp.clip(ATT_BLOCK * (j + 1) - 1 - pos, -REL_CLIP, REL_CLIP) + REL_CLIP == bins).astype(F32)
            lines = jnp.dot(rel_ref[...], hot, preferred_element_type=F32, precision=lax.Precision.HIGHEST)
            back = 2 * j + (r >> 6) - (c >> 6)
            ok = jnp.logical_and(back >= 0, back <= CA_PREV_CHUNKS)
            for h in range(heads):
                tile = jnp.broadcast_to(lines[h:h + 1, :], (ATT_BLOCK, wide))
                tile = pltpu.roll(tile, wide - (ATT_BLOCK - 1), 1, stride=1, stride_axis=0)
                out_ref[h, j] = jnp.where(ok, tile[:, :ATT_BLOCK], NEG_BIG)

    padded = jnp.pad(rel_bias, ((0, 0), (0, REL_PAD - rel_bias.shape[1])))
    return pl.pallas_call(
        body, name="ca_bias_tiles",
        out_shape=jax.ShapeDtypeStruct((heads, CA_TILES, ATT_BLOCK, ATT_BLOCK), F32),
        compiler_params=_params(),
    )(padded)


CA_BLOCKS = CA_ROWS // ATT_BLOCK
CA_KEY_BLOCKS = CA_BLOCKS + CA_TILES - 1


def _ca_users(t):
    return [r for r in range(CA_BLOCKS) if 0 <= r + CA_TILES - 1 - t < CA_TILES]


def _ca_rows(v, t, r):
    at = _ca_users(t).index(r)
    return v[ATT_BLOCK * at:ATT_BLOCK * (at + 1)]


def _ca_window(step, k_ref, v_ref, masks):
    kks, vvs, inside, rows = [], [], [], []
    for t in range(CA_KEY_BLOCKS):
        block = step * CA_BLOCKS - (CA_TILES - 1) + t
        keys = pl.ds(pl.multiple_of(jnp.maximum(block, 0) * ATT_BLOCK, ATT_BLOCK), ATT_BLOCK)
        kks.append(_by_head(k_ref[keys, :], masks))
        vvs.append(_by_head(v_ref[keys, :], masks))
        inside.append(block >= 0)
        rows.append(keys)
    return kks, vvs, inside, rows


def _ca_probs(q, kks, inside, bias_ref):
    scores = [[[None] * CA_TILES for _ in range(2)] for _ in range(CA_BLOCKS)]
    for t in range(CA_KEY_BLOCKS):
        users = _ca_users(t)
        z = _dot_nt(q[ATT_BLOCK * users[0]:ATT_BLOCK * (users[-1] + 1)], kks[t])
        for r in users:
            j = r + CA_TILES - 1 - t
            for h in range(2):
                zz = _ca_rows(z, t, r)[:, ATT_BLOCK * h:ATT_BLOCK * (h + 1)] + bias_ref[h, j]
                scores[r][h][j] = jnp.where(inside[t], zz, NEG_BIG)
    probs = [[None, None] for _ in range(CA_BLOCKS)]
    for r in range(CA_BLOCKS):
        for h in range(2):
            m = jnp.max(functools.reduce(jnp.maximum, scores[r][h]), axis=1, keepdims=True)
            ex = [jnp.exp(z - m) for z in scores[r][h]]
            inv = 1.0 / jnp.sum(functools.reduce(jnp.add, ex), axis=1, keepdims=True)
            probs[r][h] = [e * inv for e in ex]
    return probs


def _ca_stack(tiles, t):
    return jnp.concatenate(
        [jnp.concatenate([tiles[r][h][r + CA_TILES - 1 - t].astype(BF16) for h in range(2)], axis=1)
         for r in _ca_users(t)], axis=0)


def ca_fwd(qkv, bias_tiles, guest=None):
    s = qkv.shape[0]
    nq = s // CA_ROWS

    def body(q_ref, k_ref, v_ref, bias_ref, o_ref):
        masks = _head_masks()
        kks, vvs, inside, _ = _ca_window(pl.program_id(1), k_ref, v_ref, masks)
        probs = _ca_probs(q_ref[...] * QK_SCALE, kks, inside, bias_ref)
        out = [None] * CA_BLOCKS
        for t in range(CA_KEY_BLOCKS):
            part = _dot(_ca_stack(probs, t), vvs[t])
            for r in _ca_users(t):
                piece = _ca_rows(part, t, r)
                out[r] = piece if out[r] is None else out[r] + piece
        o_ref[...] = jnp.concatenate(out, axis=0).astype(BF16)

    return _pallas(
        body, guest, (qkv, qkv, qkv, bias_tiles), name="ca_fwd", grid=(N_PAIRS, nq),
        in_specs=[pl.BlockSpec((CA_ROWS, 128), lambda p, i: (i, 3 * N_PAIRS + p)),
                  pl.BlockSpec((s, 128), lambda p, i: (0, 4 * N_PAIRS + p)),
                  pl.BlockSpec((s, 128), lambda p, i: (0, 5 * N_PAIRS + p)),
                  pl.BlockSpec((2, CA_TILES, ATT_BLOCK, ATT_BLOCK), lambda p, i: (p, 0, 0, 0))],
        out_specs=[pl.BlockSpec((CA_ROWS, 128), lambda p, i: (i, p))],
        out_shape=[jax.ShapeDtypeStruct((s, ATT_WIDTH), BF16)],
        compiler_params=_params(("arbitrary", "arbitrary")),
    )


def ca_bwd(qkv, bias_tiles, do_ca, guest=None):
    s = qkv.shape[0]
    nq = s // CA_ROWS

    def body(q_ref, k_ref, v_ref, bias_ref, do_ref, dq_ref, dk_ref, dv_ref, db_ref, dk_s, dv_s):
        big = pl.program_id(1)

        @pl.when(big == 0)
        def _():
            dk_s[...] = jnp.zeros_like(dk_s)
            dv_s[...] = jnp.zeros_like(dv_s)
            db_ref[...] = jnp.zeros_like(db_ref)

        masks = _head_masks()
        kks, vvs, inside, key_rows = _ca_window(big, k_ref, v_ref, masks)
        q = q_ref[...] * QK_SCALE
        do = do_ref[...]
        probs = _ca_probs(q, kks, inside, bias_ref)
        dps = [[[None] * CA_TILES for _ in range(2)] for _ in range(CA_BLOCKS)]
        for t in range(CA_KEY_BLOCKS):
            users = _ca_users(t)
            dp = _dot_nt(do[ATT_BLOCK * users[0]:ATT_BLOCK * (users[-1] + 1)], vvs[t])
            for r in users:
                for h in range(2):
                    dps[r][h][r + CA_TILES - 1 - t] = _ca_rows(dp, t, r)[:, ATT_BLOCK * h:ATT_BLOCK * (h + 1)]
        dss = [[None, None] for _ in range(CA_BLOCKS)]
        for r in range(CA_BLOCKS):
            for h in range(2):
                delta = jnp.sum(functools.reduce(jnp.add, [p * dp for p, dp in zip(probs[r][h], dps[r][h])]),
                                axis=1, keepdims=True)
                dss[r][h] = [p * (dp - delta) for p, dp in zip(probs[r][h], dps[r][h])]
        for h in range(2):
            for j in range(CA_TILES):
                db_ref[h, j] += functools.reduce(jnp.add, [dss[r][h][j] for r in range(CA_BLOCKS)])
        dq = [None] * CA_BLOCKS
        for t in range(CA_KEY_BLOCKS):
            users = _ca_users(t)
            rows = slice(ATT_BLOCK * users[0], ATT_BLOCK * (users[-1] + 1))
            ds = _ca_stack(dss, t)
            part = _dot(ds, kks[t])
            for r in users:
                piece = _ca_rows(part, t, r)
                dq[r] = piece if dq[r] is None else dq[r] + piece
            dk = _dot_tn(ds, q[rows])
            dv = _dot_tn(_ca_stack(probs, t), do[rows])
            dk_s[key_rows[t], :] += jnp.where(masks[0], dk[:ATT_BLOCK], dk[ATT_BLOCK:])
            dv_s[key_rows[t], :] += jnp.where(masks[0], dv[:ATT_BLOCK], dv[ATT_BLOCK:])
        dq_ref[...] = (jnp.concatenate(dq, axis=0) * QK_SCALE).astype(BF16)

        @pl.when(big == nq - 1)
        def _():
            dk_ref[...] = dk_s[...].astype(BF16)
            dv_ref[...] = dv_s[...].astype(BF16)

    return _pallas(
        body, guest, (qkv, qkv, qkv, bias_tiles, do_ca), name="ca_bwd", grid=(N_PAIRS, nq),
        in_specs=[pl.BlockSpec((CA_ROWS, 128), lambda p, i: (i, 3 * N_PAIRS + p)),
                  pl.BlockSpec((s, 128), lambda p, i: (0, 4 * N_PAIRS + p)),
                  pl.BlockSpec((s, 128), lambda p, i: (0, 5 * N_PAIRS + p)),
                  pl.BlockSpec((2, CA_TILES, ATT_BLOCK, ATT_BLOCK), lambda p, i: (p, 0, 0, 0)),
                  pl.BlockSpec((CA_ROWS, 128), lambda p, i: (i, p))],
        out_specs=[pl.BlockSpec((CA_ROWS, 128), lambda p, i: (i, p)),
                   pl.BlockSpec((s, 128), lambda p, i: (0, p)),
                   pl.BlockSpec((s, 128), lambda p, i: (0, p)),
                   pl.BlockSpec((2, CA_TILES, ATT_BLOCK, ATT_BLOCK), lambda p, i: (p, 0, 0, 0))],
        out_shape=[jax.ShapeDtypeStruct((s, ATT_WIDTH), BF16)] * 3
                  + [jax.ShapeDtypeStruct((2 * N_PAIRS, CA_TILES, ATT_BLOCK, ATT_BLOCK), F32)],
        scratch_shapes=[pltpu.VMEM((s, 128), F32), pltpu.VMEM((s, 128), F32)],
        compiler_params=_params(("arbitrary", "arbitrary")),
    )


def rel_bias_grad(db_tiles):
    heads = db_tiles.shape[0]

    def body(db_ref, out_ref):
        row = lax.broadcasted_iota(jnp.int32, (ATT_BLOCK, ATT_BLOCK), 0)
        col = lax.broadcasted_iota(jnp.int32, (ATT_BLOCK, ATT_BLOCK), 1)
        flip = (row + col == ATT_BLOCK - 1).astype(F32)
        wrapped = col < row
        cp = lax.broadcasted_iota(jnp.int32, (ATT_BLOCK, REL_PAD), 0)
        bins = lax.broadcasted_iota(jnp.int32, (ATT_BLOCK, REL_PAD), 1)
        total = jnp.zeros((heads, REL_PAD), F32)
        for j in range(CA_TILES):
            sums_neg, sums_pos = [], []
            for h in range(heads):
                tile = jnp.dot(db_ref[h, j], flip, preferred_element_type=F32, precision=lax.Precision.HIGHEST)
                tile = pltpu.roll(tile, 0, 1, stride=1, stride_axis=0)
                sums_neg.append(jnp.sum(jnp.where(wrapped, 0.0, tile), axis=0, keepdims=True))
                sums_pos.append(jnp.sum(jnp.where(wrapped, tile, 0.0), axis=0, keepdims=True))
            neg = jnp.concatenate(sums_neg, axis=0)
            pos = jnp.concatenate(sums_pos, axis=0)
            hot_neg = (jnp.clip(ATT_BLOCK * j + cp - (ATT_BLOCK - 1), -REL_CLIP, REL_CLIP) + REL_CLIP == bins)
            hot_pos = (jnp.clip(ATT_BLOCK * j + cp + 1, -REL_CLIP, REL_CLIP) + REL_CLIP == bins)
            total = total + jnp.dot(neg, hot_neg.astype(F32), preferred_element_type=F32,
                                    precision=lax.Precision.HIGHEST)
            total = total + jnp.dot(pos, hot_pos.astype(F32), preferred_element_type=F32,
                                    precision=lax.Precision.HIGHEST)
        out_ref[...] = total

    return pl.pallas_call(
        body, name="rel_bias_grad",
        out_shape=jax.ShapeDtypeStruct((heads, REL_PAD), F32),
        compiler_params=_params(),
    )(db_tiles)


def _assemble_cols(dst_ref, src_ref):
    w = src_ref.shape[2]
    for j in range(N_DEV):
        dst_ref[:, w * j:w * (j + 1)] = src_ref[j]


def _merge(o_sb, o_ca, lg, bg, wsb, wca):
    y_sb = _dot(o_sb.astype(BF16), wsb)
    y_ca = _dot(o_ca.astype(BF16), wca)
    gates = jax.nn.sigmoid(lg + bg)
    g_sb, g_ca = gates[:, :D_MODEL], gates[:, D_MODEL:]
    return y_sb, y_ca, g_sb, g_ca, g_sb * y_sb + g_ca * y_ca


def mix_bwd(du1, o_sb, o_ca, lg, b_gate, wg_sb, wg_ca, w_out, tm=256, guest=None):
    s = du1.shape[0]
    last = s // tm - 1

    def body(du_ref, osb_ref, oca_ref, lg_ref, bg_ref, wsb_ref, wca_ref, wout_ref,
             dlg_ref, dosb_ref, doca_ref, dbg_ref, gout_ref, gsb_ref, gca_ref, wsb_s, wca_s, gsb_s, gca_s):
        step = pl.program_id(0)

        @pl.when(step == 0)
        def _():
            _assemble_cols(wsb_s, wsb_ref)
            _assemble_cols(wca_s, wca_ref)
            for ref in (dbg_ref, gout_ref, gsb_s, gca_s):
                ref[...] = jnp.zeros_like(ref)

        o_sb, o_ca, du = osb_ref[...].astype(BF16), oca_ref[...].astype(BF16), du_ref[...].astype(BF16)
        y_sb, y_ca, g_sb, g_ca, merged = _merge(o_sb, o_ca, lg_ref[...], bg_ref[...], wsb_s[...], wca_s[...])
        dm = _dot_nt(du, wout_ref[...])
        dl_sb = dm * y_sb * g_sb * (1.0 - g_sb)
        dl_ca = dm * y_ca * g_ca * (1.0 - g_ca)
        dlg_ref[:, :D_MODEL] = dl_sb.astype(BF16)
        dlg_ref[:, D_MODEL:] = dl_ca.astype(BF16)
        dbg_ref[:, :D_MODEL] += jnp.sum(dl_sb, axis=0, keepdims=True)
        dbg_ref[:, D_MODEL:] += jnp.sum(dl_ca, axis=0, keepdims=True)
        dy_sb = (dm * g_sb).astype(BF16)
        dy_ca = (dm * g_ca).astype(BF16)
        dosb_ref[...] = _dot_nt(dy_sb, wsb_s[...]).astype(BF16)
        doca_ref[...] = _dot_nt(dy_ca, wca_s[...]).astype(BF16)
        gout_ref[...] += _dot_tn(merged.astype(BF16), du)
        gsb_s[...] += _dot_tn(o_sb, dy_sb)
        gca_s[...] += _dot_tn(o_ca, dy_ca)

        @pl.when(step == last)
        def _():
            for j in range(N_DEV):
                cols = slice(PROJ_SHARD * j, PROJ_SHARD * (j + 1))
                gsb_ref[j] = gsb_s[:, cols]
                gca_ref[j] = gca_s[:, cols]

    rows = lambda w: pl.BlockSpec((tm, w), lambda i: (i, 0))
    whole = lambda shape: pl.BlockSpec(shape, lambda i: (0,) * len(shape))
    stack = (N_DEV, ATT_WIDTH, PROJ_SHARD)
    return _pallas(
        body, guest, (du1, o_sb, o_ca, lg, _row(b_gate), wg_sb, wg_ca, w_out), name="mix_bwd", grid=(s // tm,),
        in_specs=[rows(D_MODEL), rows(ATT_WIDTH), rows(ATT_WIDTH), rows(GATE_COLS), whole((1, GATE_COLS)),
                  whole(stack), whole(stack), whole((D_MODEL, D_MODEL))],
        out_specs=[rows(GATE_COLS), rows(ATT_WIDTH), rows(ATT_WIDTH), whole((1, GATE_COLS)),
                   whole((D_MODEL, D_MODEL)), whole(stack), whole(stack)],
        out_shape=[jax.ShapeDtypeStruct((s, GATE_COLS), BF16),
                   jax.ShapeDtypeStruct((s, ATT_WIDTH), BF16), jax.ShapeDtypeStruct((s, ATT_WIDTH), BF16),
                   jax.ShapeDtypeStruct((1, GATE_COLS), F32), jax.ShapeDtypeStruct((D_MODEL, D_MODEL), F32),
                   jax.ShapeDtypeStruct(stack, F32), jax.ShapeDtypeStruct(stack, F32)],
        scratch_shapes=[pltpu.VMEM((ATT_WIDTH, D_MODEL), BF16), pltpu.VMEM((ATT_WIDTH, D_MODEL), BF16),
                        pltpu.VMEM((ATT_WIDTH, D_MODEL), F32), pltpu.VMEM((ATT_WIDTH, D_MODEL), F32)],
        compiler_params=_params(("arbitrary",)),
    )


def _ln_stats(u):
    mu = jnp.mean(u, axis=1, keepdims=True)
    cen = u - mu
    var = jnp.mean(cen * cen, axis=1, keepdims=True)
    rstd = lax.rsqrt(var + LN_EPS)
    return cen * rstd, rstd


def _ln_bwd(dy, xhat, rstd, gain):
    dyg = dy * gain
    m1 = jnp.mean(dyg, axis=1, keepdims=True)
    m2 = jnp.mean(dyg * xhat, axis=1, keepdims=True)
    return rstd * (dyg - m1 - xhat * m2)


def _load_mlp_weights(win_hbm, wout_hbm, win_s, wout_s, sems):
    copies = [pltpu.make_async_copy(win_hbm.at[k], win_s.at[:, pl.ds(FF_SHARD * k, FF_SHARD)], sems.at[k])
              for k in range(N_DEV)]
    copies.append(pltpu.make_async_copy(wout_hbm, wout_s, sems.at[N_DEV]))
    for cp in copies:
        cp.start()
    return copies


MLP_WEIGHT_SCRATCH = [pltpu.VMEM((D_MODEL, D_FF), BF16), pltpu.VMEM((D_FF, D_MODEL), BF16),
                      pltpu.SemaphoreType.DMA((N_DEV + 1,))]


def mlp_block(x, o_sb, o_ca, lg, b_gate, wg_sb, wg_ca, w_proj, target, wg_in, w_out,
              ln1_g, ln1_b, ln2_g, ln2_b, tm=256):
    s = x.shape[0]

    def body(x_ref, osb_ref, oca_ref, lg_ref, bg_ref, wsb_hbm, wca_hbm, wproj_hbm, t_ref, win_hbm, wout_hbm,
             g1_ref, b1_ref, g2_ref, b2_ref,
             a_ref, x1b_ref, dhm_ref, du2b_ref, du1_ref, loss_ref, dg2_ref, db2_ref, dg1_ref, db1_ref,
             win_s, wout_s, sems, relu_s, wsb_s, wca_s, wproj_s, proj_sems):
        @pl.when(pl.program_id(0) == 0)
        def _():
            copies = _load_mlp_weights(win_hbm, wout_hbm, win_s, wout_s, sems)
            for j in range(N_DEV):
                cols = pl.ds(PROJ_SHARD * j, PROJ_SHARD)
                copies.append(pltpu.make_async_copy(wsb_hbm.at[j], wsb_s.at[:, cols], proj_sems.at[j]))
                copies.append(pltpu.make_async_copy(wca_hbm.at[j], wca_s.at[:, cols], proj_sems.at[N_DEV + j]))
            copies.append(pltpu.make_async_copy(wproj_hbm, wproj_s, proj_sems.at[2 * N_DEV]))
            for cp in copies[N_DEV + 1:]:
                cp.start()
            for ref in (loss_ref, dg2_ref, db2_ref, dg1_ref, db1_ref):
                ref[...] = jnp.zeros_like(ref)
            for cp in copies:
                cp.wait()

        merged = _merge(osb_ref[...], oca_ref[...], lg_ref[...], bg_ref[...], wsb_s[...], wca_s[...])[4]
        u1 = ALPHA * x_ref[...] + _dot(merged.astype(BF16), wproj_s[...])
        xhat1, rstd1 = _ln_stats(u1)
        x1 = xhat1 * g1_ref[...] + b1_ref[...]
        x1b = x1.astype(BF16)
        x1b_ref[...] = x1b
        for k in range(N_DEV):
            cols = slice(FF_SHARD * k, FF_SHARD * (k + 1))
            r = jnp.maximum(_dot(x1b, win_s[:, cols]), 0.0)
            relu_s[:, cols] = r
            a_ref[:, cols] = (r * r).astype(BF16)
        xhat2, rstd2 = _ln_stats(ALPHA * x1 + _dot(a_ref[...], wout_s[...]))
        diff = xhat2 * g2_ref[...] + b2_ref[...] - t_ref[...]
        per_token = jnp.mean(diff * diff, axis=1, keepdims=True)
        loss_ref[...] += 0.5 * jnp.sum(per_token, axis=0, keepdims=True)
        dy = diff * (1.0 / D_MODEL)
        dg2_ref[...] += jnp.sum(dy * xhat2, axis=0, keepdims=True)
        db2_ref[...] += jnp.sum(dy, axis=0, keepdims=True)
        du2 = _ln_bwd(dy, xhat2, rstd2, g2_ref[...])
        du2b = du2.astype(BF16)
        du2b_ref[...] = du2b
        for k in range(N_DEV):
            cols = slice(FF_SHARD * k, FF_SHARD * (k + 1))
            da = _dot_nt(du2b, wout_s[cols, :])
            dhm_ref[:, cols] = (2.0 * relu_s[:, cols] * da).astype(BF16)
        dx1 = ALPHA * du2 + _dot_nt(dhm_ref[...], win_s[...])
        dg1_ref[...] += jnp.sum(dx1 * xhat1, axis=0, keepdims=True)
        db1_ref[...] += jnp.sum(dx1, axis=0, keepdims=True)
        du1_ref[...] = _ln_bwd(dx1, xhat1, rstd1, g1_ref[...])

    rows = pl.BlockSpec((tm, D_MODEL), lambda i: (i, 0))
    vec = pl.BlockSpec((1, D_MODEL), lambda i: (0, 0))
    ff = pl.BlockSpec((tm, D_FF), lambda i: (i, 0))
    wide = jax.ShapeDtypeStruct((s, D_FF), BF16)
    narrow = jax.ShapeDtypeStruct((s, D_MODEL), BF16)
    stat = jax.ShapeDtypeStruct((1, D_MODEL), F32)
    tile = lambda w: pl.BlockSpec((tm, w), lambda i: (i, 0))
    return pl.pallas_call(
        body, name="mlp", grid=(s // tm,),
        in_specs=[rows, tile(ATT_WIDTH), tile(ATT_WIDTH), tile(GATE_COLS), pl.BlockSpec((1, GATE_COLS), lambda i: (0, 0)),
                  HBM_SPEC, HBM_SPEC, HBM_SPEC, rows, HBM_SPEC, HBM_SPEC, vec, vec, vec, vec],
        out_specs=[ff, rows, ff, rows, rows, pl.BlockSpec((1, 1), lambda i: (0, 0)), vec, vec, vec, vec],
        out_shape=[wide, narrow, wide, narrow, jax.ShapeDtypeStruct((s, D_MODEL), F32),
                   jax.ShapeDtypeStruct((1, 1), F32), stat, stat, stat, stat],
        scratch_shapes=MLP_WEIGHT_SCRATCH + [
            pltpu.VMEM((tm, D_FF), F32), pltpu.VMEM((ATT_WIDTH, D_MODEL), BF16), pltpu.VMEM((ATT_WIDTH, D_MODEL), BF16),
            pltpu.VMEM((D_MODEL, D_MODEL), BF16), pltpu.SemaphoreType.DMA((2 * N_DEV + 1,))],
        compiler_params=_params(("arbitrary",)),
    )(x, o_sb, o_ca, lg, _row(b_gate), wg_sb, wg_ca, w_proj, target, wg_in, w_out,
      _row(ln1_g), _row(ln1_b), _row(ln2_g), _row(ln2_b))


def dh_slab(pieces, tm=512):
    s = pieces[0].shape[0]
    widths = [p.shape[1] for p in pieces]
    assert sum(widths) == IN_COLS
    n = len(pieces)

    def body(*refs):
        dh_ref = refs[n]
        off = 0
        for ref, w in zip(refs[:n], widths):
            dh_ref[:, off:off + w] = ref[...].astype(BF16)
            off += w

    rows = lambda w: pl.BlockSpec((tm, w), lambda i: (i, 0))
    return pl.pallas_call(
        body, name="dh_slab", grid=(s // tm,),
        in_specs=[rows(w) for w in widths], out_specs=rows(IN_COLS),
        out_shape=jax.ShapeDtypeStruct((s, IN_COLS), BF16),
        compiler_params=_params(("arbitrary",)),
    )(*pieces)


def in_bwd(du1, dh, wg_in, tm=512, guest=None):
    s = du1.shape[0]

    def body(du_ref, dh_ref, w_hbm, gx_ref, pairs_s, sems):
        @pl.when(pl.program_id(0) == 0)
        def _():
            _load_in_pairs(w_hbm, pairs_s, sems)

        acc = ALPHA * du_ref[...]
        for j in range(N_DEV // 2):
            acc = acc + _dot_nt(dh_ref[:, IN_PAIR * j:IN_PAIR * (j + 1)], pairs_s[j])
        gx_ref[...] = acc

    rows = lambda w: pl.BlockSpec((tm, w), lambda i: (i, 0))
    return _pallas(
        body, guest, (du1, dh, wg_in), name="in_bwd", grid=(s // tm,), scratch_shapes=IN_PAIR_SCRATCH,
        in_specs=[rows(D_MODEL), rows(IN_COLS), HBM_SPEC],
        out_specs=[rows(D_MODEL)],
        out_shape=[jax.ShapeDtypeStruct((s, D_MODEL), F32)],
        compiler_params=_params(("arbitrary",)),
    )


def wgrad(name, a, b, tm, tn, shard_cols=None, tk=2048):
    kk, m = a.shape
    n = b.shape[1]
    while kk % tk:
        tk //= 2
    nk = kk // tk
    w = shard_cols

    def body(a_ref, b_ref, out_ref, acc_s):
        k = pl.program_id(2)

        @pl.when(k == 0)
        def _():
            acc_s[...] = jnp.zeros_like(acc_s)

        acc_s[...] += _dot_tn(a_ref[...].astype(BF16), b_ref[...].astype(BF16))

        @pl.when(k == nk - 1)
        def _():
            if w is None:
                out_ref[...] = acc_s[...]
            else:
                for j in range(tn // w):
                    out_ref[j] = acc_s[:, w * j:w * (j + 1)]

    if w is None:
        out_spec = pl.BlockSpec((tm, tn), lambda i, j, k: (i, j))
        out_shape = jax.ShapeDtypeStruct((m, n), F32)
    else:
        out_spec = pl.BlockSpec((tn // w, tm, w), lambda i, j, k: (j, i, 0))
        out_shape = jax.ShapeDtypeStruct((n // w, m, w), F32)
    return pl.pallas_call(
        body, name=name, grid=(m // tm, n // tn, nk),
        in_specs=[pl.BlockSpec((tk, tm), lambda i, j, k: (k, i)),
                  pl.BlockSpec((tk, tn), lambda i, j, k: (k, j))],
        out_specs=out_spec, out_shape=out_shape,
        scratch_shapes=[pltpu.VMEM((tm, tn), F32)],
        compiler_params=_params(("arbitrary", "arbitrary", "arbitrary")),
    )(a, b)


def adamw(name, w, m, v, parts, tm=None, guest=None):
    rows, cols = w.shape
    tm = rows if tm is None else min(tm, rows)
    n = len(parts)
    c1 = 1.0 - ADAM_B1 ** ADAM_STEP
    c2 = 1.0 - ADAM_B2 ** ADAM_STEP

    def body(*refs):
        w_ref, m_ref, v_ref = refs[:3]
        part_refs = refs[3:3 + n]
        g_ref, d_ref, nm_ref, nv_ref = refs[3 + n:]
        g = None
        for ref, (_, index) in zip(part_refs, parts):
            term = (ref[...] if index is None else ref[0]).astype(F32)
            g = term if g is None else g + term
        new_m = ADAM_B1 * m_ref[...] + (1.0 - ADAM_B1) * g
        new_v = ADAM_B2 * v_ref[...] + (1.0 - ADAM_B2) * (g * g)
        m_hat = new_m / c1
        v_hat = new_v / c2
        g_ref[...] = g
        d_ref[...] = -ADAM_LR * (m_hat / (jnp.sqrt(v_hat) + ADAM_EPS) + ADAM_WD * w_ref[...])
        nm_ref[...] = new_m
        nv_ref[...] = new_v

    spec = pl.BlockSpec((tm, cols), lambda i: (i, 0))
    shape = jax.ShapeDtypeStruct((rows, cols), F32)

    def part_spec(index):
        if index is None:
            return spec
        return pl.BlockSpec((1, tm, cols), lambda i: (index, i, 0))

    return _pallas(
        body, guest, (w, m, v, *[array for array, _ in parts]), name=name, grid=(rows // tm,),
        in_specs=[spec] * 3 + [part_spec(index) for _, index in parts],
        out_specs=[spec] * 4, out_shape=[shape] * 4,
        compiler_params=_params(("arbitrary",)),
    )


def _place():
    x, y, c = lax.axis_index("x"), lax.axis_index("y"), lax.axis_index("c")
    other_chips = [(1 - x, y), (x, 1 - y), (1 - x, 1 - y)]
    return x, y, c, other_chips


def place_shards(arrays):
    n = len(arrays)

    def body(*refs):
        ins, outs, stage, sems = refs[:n], refs[n:2 * n], refs[2 * n:3 * n], refs[3 * n]
        x, y, c, _ = _place()
        copies = []
        for a in range(n):
            stage[a][...] = ins[a][...].astype(BF16)
            copies.append(pltpu.make_async_copy(stage[a], outs[a].at[4 * x + 2 * y + c], sems.at[a]))
            copies[-1].start()
        for cp in copies:
            cp.wait()

    return pl.pallas_call(
        body, name="place_shards",
        in_specs=[pl.BlockSpec(memory_space=pltpu.VMEM)] * n, out_specs=[HBM_SPEC] * n,
        out_shape=[jax.ShapeDtypeStruct((N_DEV,) + a.shape, BF16) for a in arrays],
        scratch_shapes=[pltpu.VMEM(a.shape, BF16) for a in arrays] + [pltpu.SemaphoreType.DMA((n,))],
        compiler_params=_params(),
    )(*arrays)


def all_gather(name, arrays):
    n = len(arrays)

    def body(*refs):
        ins, outs = refs[:n], refs[n:2 * n]
        send_sems, recv_sems, local_sems = refs[2 * n:]
        x, y, c, chips = _place()
        me = 4 * x + 2 * y + c
        sibling = (x, y, 1 - c)

        def copy(a, k, block, to, src=None):
            return pltpu.make_async_remote_copy(
                src_ref=outs[a].at[block] if src is None else src, dst_ref=outs[a].at[block],
                send_sem=send_sems.at[a, k], recv_sem=recv_sems.at[a, k],
                device_id=to, device_id_type=MESH)

        started = []
        local = [pltpu.make_async_copy(ins[a], outs[a].at[me], local_sems.at[a]) for a in range(n)]
        for a in range(n):
            local[a].start()
            sends = [copy(a, 0, me, sibling, src=ins[a])]
            sends += [copy(a, 1 + j, me, (cx, cy, c), src=ins[a]) for j, (cx, cy) in enumerate(chips)]
            for cp in sends:
                cp.start()
            started += sends
        for a in range(n):
            for j, (cx, cy) in enumerate(chips):
                block = 4 * cx + 2 * cy + c
                copy(a, 1 + j, block, sibling).wait_recv()
                passed = copy(a, 4 + j, block, sibling)
                passed.start()
                started.append(passed)
        for a in range(n):
            copy(a, 0, 4 * x + 2 * y + (1 - c), sibling).wait_recv()
            for j, (cx, cy) in enumerate(chips):
                copy(a, 4 + j, 4 * cx + 2 * cy + (1 - c), sibling).wait_recv()
        for cp in started:
            cp.wait_send()
        for cp in local:
            cp.wait()

    return pl.pallas_call(
        body, name=name,
        in_specs=[pl.BlockSpec(memory_space=pltpu.VMEM)] * n, out_specs=[HBM_SPEC] * n,
        out_shape=[jax.ShapeDtypeStruct((N_DEV,) + a.shape, a.dtype) for a in arrays],
        scratch_shapes=[pltpu.SemaphoreType.DMA((n, 7)), pltpu.SemaphoreType.DMA((n, 7)),
                        pltpu.SemaphoreType.DMA((n,))],
    )(*arrays)


def _gather_first_plan(n):
    def plan(zones):
        x, y, c, chips = _place()
        me = 4 * x + 2 * y + c
        copies = []
        for zone in zones[:n]:
            copies.append((zone.at[me], zone.at[me], (x, y, 1 - c), zone.at[4 * x + 2 * y + (1 - c)]))
            copies += [(zone.at[me], zone.at[me], (cx, cy, c), zone.at[4 * cx + 2 * cy + c]) for cx, cy in chips]
        return copies
    return plan


def _gather_second_plan(n):
    def plan(zones):
        x, y, c, chips = _place()
        return [(zone.at[4 * cx + 2 * cy + c], zone.at[4 * cx + 2 * cy + c], (x, y, 1 - c),
                 zone.at[4 * cx + 2 * cy + (1 - c)]) for zone in zones[:n] for cx, cy in chips]
    return plan


def _both_plans(first, n_first, second):
    return lambda bufs: first(bufs[:n_first]) + second(bufs[n_first:])


def _sibling_plan(n):
    def plan(bufs):
        x, y, c, _ = _place()
        return [(bufs[a].at[:, 1 - c], bufs[n + a], (x, y, 1 - c), bufs[n + a]) for a in range(n)]
    return plan


def _chips_plan(n):
    def plan(bufs):
        _, _, c, chips = _place()
        return [(bufs[a].at[2 * cx + cy], bufs[n + a].at[j], (cx, cy, c), bufs[n + a].at[j])
                for a in range(n) for j, (cx, cy) in enumerate(chips)]
    return plan


def chip_sum(name, grad, landed, place, tr=512):
    _, _, rows, cols = grad.shape
    tr = min(tr, rows)

    def body(place_ref, g_ref, l_ref, own_ref, out_ref):
        total = g_ref[0, 0] + l_ref[0]
        out_ref[0] = total.astype(BF16)

        @pl.when(pl.program_id(1) == place_ref[1])
        def _():
            own_ref[...] = total

    return pl.pallas_call(
        body, name=name,
        grid_spec=pltpu.PrefetchScalarGridSpec(
            num_scalar_prefetch=1, grid=(rows // tr, 4),
            in_specs=[pl.BlockSpec((1, 1, tr, cols), lambda i, ch, pos: (ch, pos[0], i, 0)),
                      pl.BlockSpec((1, tr, cols), lambda i, ch, pos: (ch, i, 0))],
            out_specs=[pl.BlockSpec((tr, cols), lambda i, ch, pos: (i, 0)),
                       pl.BlockSpec((1, tr, cols), lambda i, ch, pos: (ch, i, 0))]),
        out_shape=[jax.ShapeDtypeStruct((rows, cols), F32), jax.ShapeDtypeStruct((4, rows, cols), BF16)],
        compiler_params=_params(("arbitrary", "arbitrary")),
    )(place, grad, landed)


BIG =("w_in", "w_sb_proj", "w_ca_proj", "w_out", "w_mlp_in", "w_mlp_out")
SMALL = ("b_gate", "rel_bias", "ln1_g", "ln1_b", "ln2_g", "ln2_b")
NAMES = ("w_in", "b_gate", "w_sb_proj", "w_ca_proj", "rel_bias", "w_out", "ln1_g", "ln1_b",
         "w_mlp_in", "w_mlp_out", "ln2_g", "ln2_b")
REL_COLS = 2 * REL_CLIP + 1


def _pack_small(t, scalar=None):
    rel = t["rel_bias"]
    if rel.shape[1] != REL_PAD:
        rel = jnp.pad(rel, ((0, 0), (0, REL_PAD - rel.shape[1])))
    last = jnp.zeros((128,), F32) if scalar is None else jnp.pad(scalar.reshape(1), (0, 127))
    flat = [t["b_gate"].reshape(-1), rel.reshape(-1)] + [t[n].reshape(-1) for n in SMALL[2:]] + [last]
    return jnp.concatenate(flat).reshape(-1, 128)


def _unpack_small(p):
    flat = p.reshape(-1)
    out, off = {}, 0
    for n, size in (("b_gate", GATE_COLS), ("rel_bias", 8 * REL_PAD)) + tuple((n, D_MODEL) for n in SMALL[2:]):
        out[n] = flat[off:off + size]
        off += size
    out["rel_bias"] = out["rel_bias"].reshape(8, REL_PAD)[:, :REL_COLS]
    return out


def kernel(x, w_in, b_gate, w_sb_proj, w_ca_proj, rel_bias, w_out, ln1_g, ln1_b, w_mlp_in, w_mlp_out, ln2_g, ln2_b, loss_target, m_w_in, m_b_gate, m_w_sb_proj, m_w_ca_proj, m_rel_bias, m_w_out, m_ln1_g, m_ln1_b, m_w_mlp_in, m_w_mlp_out, m_ln2_g, m_ln2_b, v_w_in, v_b_gate, v_w_sb_proj, v_w_ca_proj, v_rel_bias, v_w_out, v_ln1_g, v_ln1_b, v_w_mlp_in, v_w_mlp_out, v_ln2_g, v_ln2_b):
    w = dict(w_in=w_in, b_gate=b_gate, w_sb_proj=w_sb_proj, w_ca_proj=w_ca_proj, rel_bias=rel_bias, w_out=w_out,
             ln1_g=ln1_g, ln1_b=ln1_b, w_mlp_in=w_mlp_in, w_mlp_out=w_mlp_out, ln2_g=ln2_g, ln2_b=ln2_b)
    m = dict(w_in=m_w_in, b_gate=m_b_gate, w_sb_proj=m_w_sb_proj, w_ca_proj=m_w_ca_proj, rel_bias=m_rel_bias,
             w_out=m_w_out, ln1_g=m_ln1_g, ln1_b=m_ln1_b, w_mlp_in=m_w_mlp_in, w_mlp_out=m_w_mlp_out,
             ln2_g=m_ln2_g, ln2_b=m_ln2_b)
    v = dict(w_in=v_w_in, b_gate=v_b_gate, w_sb_proj=v_w_sb_proj, w_ca_proj=v_w_ca_proj, rel_bias=v_rel_bias,
             w_out=v_w_out, ln1_g=v_ln1_g, ln1_b=v_ln1_b, w_mlp_in=v_w_mlp_in, w_mlp_out=v_w_mlp_out,
             ln2_g=v_ln2_g, ln2_b=v_ln2_b)

    x, target = x[0], loss_target[0]
    c = lax.axis_index("c")
    place = jnp.stack([c, 2 * lax.axis_index("x") + lax.axis_index("y")]).astype(jnp.int32)
    grads, deltas, new_m, new_v = {}, {}, {}, {}

    def by_owner(n, g):
        return g.reshape((4, 2) + w[n].shape)

    def finish(n, own, arrived, guest=None):
        out = adamw("adamw_" + n, w[n], m[n], v[n], [(own, None), (arrived, 0), (arrived, 1), (arrived, 2)],
                    tm=512, guest=guest)
        out, bufs = (out, None) if guest is None else out
        grads[n], deltas[n], new_m[n], new_v[n] = out
        return bufs

    def siblings(names, g):
        bufs = [by_owner(n, g[n]) for n in names] + [lax.empty((4,) + w[n].shape, F32) for n in names]
        return Exchange(tuple(bufs), _sibling_plan(len(names)), len(names))

    def chip_sums(names, bufs):
        k = len(names)
        own, sums = zip(*[chip_sum("chip_sum_" + n, bufs[a], bufs[k + a], place) for a, n in enumerate(names)])
        zones = [lax.empty((3,) + w[n].shape, BF16) for n in names]
        return own, Exchange(tuple(sums) + tuple(zones), _chips_plan(k), 3 * k)

    proj, ffn = ("w_sb_proj", "w_ca_proj", "w_out"), ("w_mlp_in", "w_mlp_out")
    zones = place_shards([w[n] for n in proj + ffn])
    proj_zones, ffn_zones = tuple(zones[:len(proj)]), tuple(zones[len(proj):])
    wg_in = all_gather("gather_w_in", [w_in.astype(BF16)])[0]
    (qkv, lg, xb), proj_zones = in_proj(
        x, wg_in, guest=Exchange(proj_zones, _gather_first_plan(len(proj)), 4 * len(proj)))
    bias_tiles = ca_bias_tiles(rel_bias)
    both = Exchange(ffn_zones + tuple(proj_zones),
                    _both_plans(_gather_first_plan(len(ffn)), len(ffn), _gather_second_plan(len(proj))),
                    4 * len(ffn) + 3 * len(proj))
    (o_sb,), zones = sb_fwd(qkv, both)
    ffn_zones, proj_zones = tuple(zones[:len(ffn)]), zones[len(ffn):]
    (o_ca,), ffn_zones = ca_fwd(qkv, bias_tiles, Exchange(ffn_zones, _gather_second_plan(len(ffn)), 3 * len(ffn)))
    wg = dict(zip(proj + ffn, list(proj_zones) + list(ffn_zones)))
    wg_sb, wg_ca, wg_mi = wg["w_sb_proj"], wg["w_ca_proj"], wg["w_mlp_in"]
    wg_mo = wg["w_mlp_out"].reshape(D_FF, D_MODEL)
    w_out_b = wg["w_out"].reshape(D_MODEL, D_MODEL)

    a, x1b, dhm, du2b, du1, loss, dg2, db2, dg1, db1 = mlp_block(
        x, o_sb, o_ca, lg, b_gate, wg_sb, wg_ca, w_out_b, target, wg_mi, wg_mo, ln1_g, ln1_b, ln2_g, ln2_b)
    g = {}
    g["w_mlp_in"] = wgrad("wgrad_mlp_in", x1b, dhm, tm=D_MODEL, tn=FF_SHARD, shard_cols=FF_SHARD, tk=x.shape[0])
    g["w_mlp_out"] = wgrad("wgrad_mlp_out", a, du2b, tm=FF_SHARD, tn=D_MODEL, tk=x.shape[0])

    mlp = ("w_mlp_in", "w_mlp_out")
    (dlg, do_sb, do_ca, dbg, g["w_out"], g["w_sb_proj"], g["w_ca_proj"]), bufs = mix_bwd(
        du1, o_sb, o_ca, lg, b_gate, wg_sb, wg_ca, w_out_b, guest=siblings(mlp, g))
    own_mlp, mlp_chips = chip_sums(mlp, bufs)

    att = ("w_out", "w_sb_proj", "w_ca_proj")
    att_siblings = siblings(att, g)
    both = Exchange(mlp_chips.bufs + att_siblings.bufs,
                    _both_plans(mlp_chips.plan, len(mlp_chips.bufs), att_siblings.plan),
                    mlp_chips.copies + att_siblings.copies)
    (dq_sb, dk_sb, dv_sb), bufs = sb_bwd(qkv, o_sb, do_sb, guest=both)
    arrived_mlp, bufs = bufs[len(mlp):len(mlp_chips.bufs)], bufs[len(mlp_chips.bufs):]
    own_att, att_chips = chip_sums(att, bufs)
    (dq_ca, dk_ca, dv_ca, db_tiles), bufs = ca_bwd(qkv, bias_tiles, do_ca, guest=att_chips)
    arrived_att = bufs[len(att):]
    g_rel = rel_bias_grad(db_tiles)

    dh = dh_slab([dq_sb, dk_sb, dv_sb, dq_ca, dk_ca, dv_ca, dlg])
    g["w_in"] = wgrad("wgrad_in", xb, dh, tm=D_MODEL, tn=2 * IN_SHARD, shard_cols=IN_SHARD)
    last = ("w_in",)
    bufs = finish(mlp[0], own_mlp[0], arrived_mlp[0], guest=siblings(last, g))
    finish(mlp[1], own_mlp[1], arrived_mlp[1])
    for n, own, arrived in zip(att, own_att, arrived_att):
        finish(n, own, arrived)
    own_in, in_chips = chip_sums(last, bufs)
    (grad_x,), bufs = in_bwd(du1, dh, wg_in, guest=in_chips)
    finish(last[0], own_in[0], bufs[1])

    small = dict(b_gate=dbg, rel_bias=g_rel, ln1_g=dg1, ln1_b=db1, ln2_g=dg2, ln2_b=db2)
    parts = all_gather("gather_small_grads", [_pack_small(small, loss[0, 0])])[0]
    packed = adamw("adamw_small", _pack_small(w), _pack_small(m), _pack_small(v),
                   [(parts, d) for d in range(N_DEV)])
    loss = packed[0][-1, 0]
    for out, p in zip((grads, deltas, new_m, new_v), packed):
        for n, val in _unpack_small(p).items():
            out[n] = val.reshape(w[n].shape)

    return (loss, grad_x[None], *[grads[n] for n in NAMES], *[deltas[n] for n in NAMES],
            *[new_m[n] for n in NAMES], *[new_v[n] for n in NAMES])
```

```python
import functools
from typing import NamedTuple

import jax
import jax.numpy as jnp
from jax import lax
from jax.experimental import pallas as pl
from jax.experimental.pallas import tpu as pltpu

F32 = jnp.float32
BF16 = jnp.bfloat16
MESH = pl.DeviceIdType.MESH

N_DEV = 8
D_MODEL = 1024
HEAD_DIM = 64
ATT_WIDTH = 512
N_PAIRS = ATT_WIDTH // 128
QKV_COLS = 6 * ATT_WIDTH
GATE_COLS = 2 * D_MODEL
IN_COLS = QKV_COLS + GATE_COLS
IN_SHARD = IN_COLS // N_DEV
D_FF = 4 * D_MODEL
FF_SHARD = D_FF // N_DEV
PROJ_SHARD = D_MODEL // N_DEV
ATT_BLOCK = 128
CA_TILES = 5
CA_ROWS = 2048
CHUNK = 64
CA_PREV_CHUNKS = 8
REL_CLIP = 256
REL_PAD = 640
ALPHA = 2.0 ** 0.25
LN_EPS = 1e-5
QK_SCALE = HEAD_DIM ** -0.5
NEG_BIG = -1e30
VMEM_LIMIT = 56 * 1024 * 1024

ADAM_LR = 0.001
ADAM_B1 = 0.9
ADAM_B2 = 0.999
ADAM_EPS = 1e-08
ADAM_WD = 0.01
ADAM_STEP = 10

_NT = (((1,), (1,)), ((), ()))
_TN = (((0,), (0,)), ((), ()))


def _dot(a, b):
    return jnp.dot(a, b, preferred_element_type=F32)


def _dot_nt(a, b):
    return lax.dot_general(a, b, _NT, preferred_element_type=F32)


def _dot_tn(a, b):
    return lax.dot_general(a, b, _TN, preferred_element_type=F32)


def _params(semantics=None):
    return pltpu.CompilerParams(dimension_semantics=semantics, vmem_limit_bytes=VMEM_LIMIT)


def _row(v):
    return v.reshape(1, -1)


HBM_SPEC = pl.BlockSpec(memory_space=pl.ANY)


class Exchange(NamedTuple):
    bufs: tuple
    plan: object
    copies: int


def _pallas(body, guest, args, *, grid, in_specs, out_specs, out_shape, scratch_shapes=(), **kw):
    out_specs, out_shape, scratch_shapes = list(out_specs), list(out_shape), list(scratch_shapes)
    if guest is None:
        return pl.pallas_call(body, grid=grid, in_specs=in_specs, out_specs=out_specs, out_shape=out_shape,
                              scratch_shapes=scratch_shapes, **kw)(*args)
    nb, n_own = len(guest.bufs), len(in_specs) + len(out_specs)

    def hosting(*refs):
        zones = refs[nb + n_own:2 * nb + n_own]
        send_sems, recv_sems = refs[-2], refs[-1]
        steps = [pl.program_id(axis) for axis in range(len(grid))]
        first = functools.reduce(jnp.logical_and, [step == 0 for step in steps])
        last = functools.reduce(jnp.logical_and, [step == n - 1 for step, n in zip(steps, grid)])

        def copy(k, src, dst, peer):
            return pltpu.make_async_remote_copy(src_ref=src, dst_ref=dst, send_sem=send_sems.at[k],
                                                recv_sem=recv_sems.at[k], device_id=peer, device_id_type=MESH)

        @pl.when(first)
        def _():
            for k, (src, dst, peer, _) in enumerate(guest.plan(zones)):
                copy(k, src, dst, peer).start()

        body(*refs[nb:nb + n_own], *refs[2 * nb + n_own:-2])

        @pl.when(last)
        def _():
            for k, (src, dst, peer, landed) in enumerate(guest.plan(zones)):
                copy(k, src, dst, peer).wait_send()
                copy(k, src, landed, peer).wait_recv()

    sems = pltpu.SemaphoreType.DMA((guest.copies,))
    results = pl.pallas_call(
        hosting, grid=grid, in_specs=[HBM_SPEC] * nb + list(in_specs), out_specs=out_specs + [HBM_SPEC] * nb,
        out_shape=out_shape + [jax.ShapeDtypeStruct(b.shape, b.dtype) for b in guest.bufs],
        input_output_aliases={k: len(out_specs) + k for k in range(nb)},
        scratch_shapes=scratch_shapes + [sems, sems], **kw)(*guest.bufs, *args)
    return results[:len(out_specs)], results[len(out_specs):]


IN_PAIR = 2 * IN_SHARD
IN_PAIR_SCRATCH = [pltpu.VMEM((N_DEV // 2, D_MODEL, IN_PAIR), BF16), pltpu.SemaphoreType.DMA((N_DEV,))]


def _load_in_pairs(w_hbm, pairs_s, sems):
    copies = [pltpu.make_async_copy(w_hbm.at[j], pairs_s.at[j // 2, :, pl.ds(IN_SHARD * (j % 2), IN_SHARD)],
                                    sems.at[j]) for j in range(N_DEV)]
    for cp in copies:
        cp.start()
    for cp in copies:
        cp.wait()


def in_proj(x, wg_in, tm=512, guest=None):
    s = x.shape[0]

    def body(x_ref, w_hbm, qkv_ref, lg_ref, xb_ref, pairs_s, sems):
        @pl.when(pl.program_id(0) == 0)
        def _():
            _load_in_pairs(w_hbm, pairs_s, sems)

        xb = x_ref[...].astype(BF16)
        xb_ref[...] = xb
        for j in range(N_DEV // 2):
            acc = _dot(xb, pairs_s[j])
            lo, hi = IN_PAIR * j, IN_PAIR * (j + 1)
            if hi <= QKV_COLS:
                qkv_ref[:, lo:hi] = acc.astype(BF16)
            elif lo >= QKV_COLS:
                lg_ref[:, lo - QKV_COLS:hi - QKV_COLS] = acc
            else:
                qkv_ref[:, lo:QKV_COLS] = acc[:, :QKV_COLS - lo].astype(BF16)
                lg_ref[:, 0:hi - QKV_COLS] = acc[:, QKV_COLS - lo:]

    return _pallas(
        body, guest, (x, wg_in), name="in_proj", grid=(s // tm,), scratch_shapes=IN_PAIR_SCRATCH,
        in_specs=[pl.BlockSpec((tm, D_MODEL), lambda i: (i, 0)), HBM_SPEC],
        out_specs=[pl.BlockSpec((tm, QKV_COLS), lambda i: (i, 0)),
                   pl.BlockSpec((tm, GATE_COLS), lambda i: (i, 0)),
                   pl.BlockSpec((tm, D_MODEL), lambda i: (i, 0))],
        out_shape=[jax.ShapeDtypeStruct((s, QKV_COLS), BF16),
                   jax.ShapeDtypeStruct((s, GATE_COLS), F32),
                   jax.ShapeDtypeStruct((s, D_MODEL), BF16)],
        compiler_params=_params(("arbitrary",)),
    )


def _head_masks():
    lane = lax.broadcasted_iota(jnp.int32, (1, 128), 1)
    first = lane < HEAD_DIM
    return first, jnp.logical_not(first)


SB_CHUNK = 512
SB_SUB = SB_CHUNK // ATT_BLOCK


def _tri_pair():
    row = lax.broadcasted_iota(jnp.int32, (4 * ATT_BLOCK, 2 * ATT_BLOCK), 0) & (2 * ATT_BLOCK - 1)
    col = lax.broadcasted_iota(jnp.int32, (4 * ATT_BLOCK, 2 * ATT_BLOCK), 1)
    same = (row >= ATT_BLOCK) == (col >= ATT_BLOCK)
    return jnp.logical_and(same, row >= col).astype(BF16)


def _pair_bf16(pieces):
    return jnp.concatenate(
        [jnp.concatenate([p.astype(BF16) for p in row], axis=1) for row in pieces], axis=0)


def _suffix_sums(pieces, tri_pair):
    hi = [[p.astype(BF16) for p in row] for row in pieces]
    lo = [[p - h.astype(F32) for p, h in zip(row, hrow)] for row, hrow in zip(pieces, hi)]
    return _dot(jnp.concatenate([_pair_bf16(hi), _pair_bf16(lo)], axis=1), tri_pair)


def _first_row_piece(diag, j):
    return j if diag else 0


def _piece(v, r, h):
    return v[ATT_BLOCK * r:ATT_BLOCK * (r + 1), ATT_BLOCK * h:ATT_BLOCK * (h + 1)]


def _all_row_pieces(slabs):
    return [[slab[ATT_BLOCK * r:ATT_BLOCK * (r + 1)] for r in range(SB_SUB)] for slab in slabs]


def _add_rows(total, part, r0):
    if r0 == 0:
        return total + part
    return jnp.concatenate([total[:ATT_BLOCK * r0], total[ATT_BLOCK * r0:] + part], axis=0)


def _row_totals(csum, r0, h):
    return jnp.concatenate(
        [jnp.broadcast_to(_piece(csum, at, h)[:, 0:1], (ATT_BLOCK, ATT_BLOCK)) for at in range(SB_SUB - r0)], axis=0)


SIGN_BIT = 0x80000000


def _drop(z):
    minus_abs = lax.bitcast_convert_type(lax.bitcast_convert_type(z, jnp.uint32) | jnp.uint32(SIGN_BIT), F32)
    return jnp.maximum(z, 0.0) + jnp.log(1.0 + jnp.exp(minus_abs))


def _sb_weights(z, runs, r0, masked, strict, tri_pair):
    drops = []
    for r in range(r0, SB_SUB):
        row = [_drop(_piece(z, r - r0, h)) for h in range(2)]
        drops.append([jnp.where(strict, drop, 0.0) for drop in row] if r == masked else row)
    csum = _suffix_sums(drops, tri_pair)
    weights = []
    for r in range(r0, SB_SUB):
        row = [jnp.exp(_piece(z, r - r0, h) - (_piece(csum, r - r0, h) + runs[h][r])) for h in range(2)]
        weights.append([jnp.where(strict, a, 0.0) for a in row] if r == masked else row)
    return drops, weights, csum


EXP_UNDERFLOW = 104.0


SB_STEP = 2


def _sweep_left(span, i, dropped_ref):
    def more(at):
        return jnp.logical_and(at >= 0, jnp.min(dropped_ref[...]) < EXP_UNDERFLOW)

    def step(at):
        span(at * (SB_STEP * ATT_BLOCK), SB_STEP, False)
        return at - 1

    lax.while_loop(more, step, i * (SB_SUB // SB_STEP) - 1)


def _by_head(blk, masks):
    zero = jnp.zeros_like(blk)
    return jnp.concatenate([jnp.where(m, blk, zero) for m in masks], axis=0)


def sb_fwd(qkv, guest=None):
    s = qkv.shape[0]
    nq = s // SB_CHUNK

    def body(q_ref, k_ref, v_ref, o_ref, run_s):
        i = pl.program_id(1)
        masks = _head_masks()
        q = q_ref[...] * QK_SCALE
        row = lax.broadcasted_iota(jnp.int32, (ATT_BLOCK, ATT_BLOCK), 0)
        col = lax.broadcasted_iota(jnp.int32, (ATT_BLOCK, ATT_BLOCK), 1)
        strict = col < row
        tri_pair = _tri_pair()
        run_s[...] = jnp.zeros_like(run_s)
        o_ref[...] = jnp.zeros_like(o_ref)

        def span(start, pieces, diag):
            runs = [run_s[0], run_s[1]]
            acc = jnp.zeros((SB_CHUNK, ATT_BLOCK), F32)
            for j in reversed(range(pieces)):
                r0 = _first_row_piece(diag, j)
                keys = pl.ds(pl.multiple_of(start + ATT_BLOCK * j, ATT_BLOCK), ATT_BLOCK)
                z = _dot_nt(q[ATT_BLOCK * r0:], _by_head(k_ref[keys, :], masks))
                _, weights, csum = _sb_weights(z, _all_row_pieces(runs), r0, j if diag else None, strict, tri_pair)
                part = _dot(_pair_bf16(weights), _by_head(v_ref[keys, :], masks))
                acc = _add_rows(acc, part, r0)
                runs = [_add_rows(runs[h], _row_totals(csum, r0, h), r0) for h in range(2)]
            run_s[0], run_s[1] = runs
            o_ref[...] += acc

        span(i * SB_CHUNK, SB_SUB, True)
        _sweep_left(span, i, run_s)

    return _pallas(
        body, guest, (qkv, qkv, qkv), name="sb_fwd", grid=(N_PAIRS, nq),
        in_specs=[pl.BlockSpec((SB_CHUNK, 128), lambda p, i: (i, p)),
                  pl.BlockSpec((s, 128), lambda p, i: (0, N_PAIRS + p)),
                  pl.BlockSpec((s, 128), lambda p, i: (0, 2 * N_PAIRS + p))],
        out_specs=[pl.BlockSpec((SB_CHUNK, 128), lambda p, i: (i, p))],
        out_shape=[jax.ShapeDtypeStruct((s, ATT_WIDTH), F32)],
        scratch_shapes=[pltpu.VMEM((2, SB_CHUNK, 128), F32)],
        compiler_params=_params(("arbitrary", "arbitrary")),
    )


def sb_bwd(qkv, o_sb, do_sb, guest=None):
    s = qkv.shape[0]
    nq = s // SB_CHUNK

    def body(q_ref, k_ref, v_ref, o_ref, do_ref, dq_ref, dk_ref, dv_ref, run_l_s, run_e_s, etot_s, dq_s, dk_s, dv_s):
        i = pl.program_id(1)

        @pl.when(i == 0)
        def _():
            dk_s[...] = jnp.zeros_like(dk_s)
            dv_s[...] = jnp.zeros_like(dv_s)

        masks = _head_masks()
        q = q_ref[...] * QK_SCALE
        do = do_ref[...]
        prod = do.astype(F32) * o_ref[...]
        for h in range(2):
            total = jnp.sum(jnp.where(masks[h], prod, 0.0), axis=1, keepdims=True)
            etot_s[h] = jnp.broadcast_to(total, (SB_CHUNK, ATT_BLOCK))
        row = lax.broadcasted_iota(jnp.int32, (ATT_BLOCK, ATT_BLOCK), 0)
        col = lax.broadcasted_iota(jnp.int32, (ATT_BLOCK, ATT_BLOCK), 1)
        strict = col < row
        tri_pair = _tri_pair()
        run_l_s[...] = jnp.zeros_like(run_l_s)
        run_e_s[...] = jnp.zeros_like(run_e_s)
        dq_s[...] = jnp.zeros_like(dq_s)

        def span(start, pieces, diag):
            slabs_l = [run_l_s[0], run_l_s[1]]
            slabs_e = [run_e_s[0], run_e_s[1]]
            e_tot = _all_row_pieces([etot_s[0], etot_s[1]])
            dq = jnp.zeros((SB_CHUNK, ATT_BLOCK), F32)
            for j in reversed(range(pieces)):
                runs_l, runs_e = _all_row_pieces(slabs_l), _all_row_pieces(slabs_e)
                r0 = _first_row_piece(diag, j)
                masked = j if diag else None
                keys = pl.ds(pl.multiple_of(start + ATT_BLOCK * j, ATT_BLOCK), ATT_BLOCK)
                kk = _by_head(k_ref[keys, :], masks)
                vv = _by_head(v_ref[keys, :], masks)
                q_rows, do_rows = q[ATT_BLOCK * r0:], do[ATT_BLOCK * r0:]
                z = _dot_nt(q_rows, kk)
                da = _dot_nt(do_rows, vv)
                drops, weights, csum = _sb_weights(z, runs_l, r0, masked, strict, tri_pair)
                ab = [[a.astype(BF16) for a in wrow] for wrow in weights]
                es = [[a.astype(F32) * _piece(da, at, h) for h, a in enumerate(arow)] for at, arow in enumerate(ab)]
                esum = _suffix_sums(es, tri_pair)
                dzs = []
                for at, r in enumerate(range(r0, SB_SUB)):
                    dz_row = []
                    for h in range(2):
                        sig = jnp.exp(_piece(z, at, h) - drops[at][h])
                        after = _piece(esum, at, h) + runs_e[h][r] - es[at][h]
                        dz = es[at][h] - sig * (e_tot[h][r] - after)
                        dz_row.append(jnp.where(strict, dz, 0.0) if r == masked else dz)
                    dzs.append(dz_row)
                dz = _pair_bf16(dzs)
                dq = _add_rows(dq, _dot(dz, kk), r0)
                dk = _dot_tn(dz, q_rows)
                dv = _dot_tn(_pair_bf16(ab), do_rows)
                dk_s[keys, :] += jnp.where(masks[0], dk[:ATT_BLOCK], dk[ATT_BLOCK:])
                dv_s[keys, :] += jnp.where(masks[0], dv[:ATT_BLOCK], dv[ATT_BLOCK:])
                slabs_l =[_add_rows(slabs_l[h], _row_totals(csum, r0, h), r0) for h in range(2)]
                slabs_e = [_add_rows(slabs_e[h], _row_totals(esum, r0, h), r0) for h in range(2)]
            run_l_s[0], run_l_s[1] = slabs_l
            run_e_s[0], run_e_s[1] = slabs_e
            dq_s[...] += dq

        span(i * SB_CHUNK, SB_SUB, True)
        _sweep_left(span, i, run_l_s)
        dq_ref[...] = (dq_s[...] * QK_SCALE).astype(BF16)

        @pl.when(i == nq - 1)
        def _():
            dk_ref[...] = dk_s[...].astype(BF16)
            dv_ref[...] = dv_s[...].astype(BF16)

    tile = lambda p, i: (i, p)
    state = pltpu.VMEM((2, SB_CHUNK, 128), F32)
    sums = pltpu.VMEM((s, 128), F32)
    return _pallas(
        body, guest, (qkv, qkv, qkv, o_sb, do_sb), name="sb_bwd", grid=(N_PAIRS, nq),
        in_specs=[pl.BlockSpec((SB_CHUNK, 128), tile),
                  pl.BlockSpec((s, 128), lambda p, i: (0, N_PAIRS + p)),
                  pl.BlockSpec((s, 128), lambda p, i: (0, 2 * N_PAIRS + p)),
                  pl.BlockSpec((SB_CHUNK, 128), tile),
                  pl.BlockSpec((SB_CHUNK, 128), tile)],
        out_specs=[pl.BlockSpec((SB_CHUNK, 128), tile),
                   pl.BlockSpec((s, 128), lambda p, i: (0, p)),
                   pl.BlockSpec((s, 128), lambda p, i: (0, p))],
        out_shape=[jax.ShapeDtypeStruct((s, ATT_WIDTH), BF16)] * 3,
        scratch_shapes=[state, state, state, pltpu.VMEM((SB_CHUNK, 128), F32), sums, sums],
        compiler_params=_params(("arbitrary", "arbitrary")),
    )


def ca_bias_tiles(rel_bias):
    heads = rel_bias.shape[0]
    wide = 2 * ATT_BLOCK

    def body(rel_ref, out_ref):
        bins = lax.broadcasted_iota(jnp.int32, (REL_PAD, wide), 0)
        pos = lax.broadcasted_iota(jnp.int32, (REL_PAD, wide), 1)
        r = lax.broadcasted_iota(jnp.int32, (ATT_BLOCK, ATT_BLOCK), 0)
        c = lax.broadcasted_iota(jnp.int32, (ATT_BLOCK, ATT_BLOCK), 1)
        for j in range(CA_TILES):
            hot = (jnp.clip(ATT_BLOCK * (j + 1) - 1 - pos, -REL_CLIP, REL_CLIP) + REL_CLIP == bins).astype(F32)
            lines = jnp.dot(rel_ref[...], hot, preferred_element_type=F32, precision=lax.Precision.HIGHEST)
            back = 2 * j + (r >> 6) - (c >> 6)
            ok = jnp.logical_and(back >= 0, back <= CA_PREV_CHUNKS)
            for h in range(heads):
                tile = jnp.broadcast_to(lines[h:h + 1, :], (ATT_BLOCK, wide))
                tile = pltpu.roll(tile, wide - (ATT_BLOCK - 1), 1, stride=1, stride_axis=0)
                out_ref[h, j] = jnp.where(ok, tile[:, :ATT_BLOCK], NEG_BIG)

    padded = jnp.pad(rel_bias, ((0, 0), (0, REL_PAD - rel_bias.shape[1])))
    return pl.pallas_call(
        body, name="ca_bias_tiles",
        out_shape=jax.ShapeDtypeStruct((heads, CA_TILES, ATT_BLOCK, ATT_BLOCK), F32),
        compiler_params=_params(),
    )(padded)


CA_BLOCKS = CA_ROWS // ATT_BLOCK
CA_KEY_BLOCKS = CA_BLOCKS + CA_TILES - 1


def _ca_users(t):
    return [r for r in range(CA_BLOCKS) if 0 <= r + CA_TILES - 1 - t < CA_TILES]


def _ca_rows(v, t, r):
    at = _ca_users(t).index(r)
    return v[ATT_BLOCK * at:ATT_BLOCK * (at + 1)]


def _ca_window(step, k_ref, v_ref, masks):
    kks, vvs, inside, rows = [], [], [], []
    for t in range(CA_KEY_BLOCKS):
        block = step * CA_BLOCKS - (CA_TILES - 1) + t
        keys = pl.ds(pl.multiple_of(jnp.maximum(block, 0) * ATT_BLOCK, ATT_BLOCK), ATT_BLOCK)
        kks.append(_by_head(k_ref[keys, :], masks))
        vvs.append(_by_head(v_ref[keys, :], masks))
        inside.append(block >= 0)
        rows.append(keys)
    return kks, vvs, inside, rows


def _ca_probs(q, kks, inside, bias_ref):
    scores = [[[None] * CA_TILES for _ in range(2)] for _ in range(CA_BLOCKS)]
    for t in range(CA_KEY_BLOCKS):
        users = _ca_users(t)
        z = _dot_nt(q[ATT_BLOCK * users[0]:ATT_BLOCK * (users[-1] + 1)], kks[t])
        for r in users:
            j = r + CA_TILES - 1 - t
            for h in range(2):
                zz = _ca_rows(z, t, r)[:, ATT_BLOCK * h:ATT_BLOCK * (h + 1)] + bias_ref[h, j]
                scores[r][h][j] = jnp.where(inside[t], zz, NEG_BIG)
    probs = [[None, None] for _ in range(CA_BLOCKS)]
    for r in range(CA_BLOCKS):
        for h in range(2):
            m = jnp.max(functools.reduce(jnp.maximum, scores[r][h]), axis=1, keepdims=True)
            ex = [jnp.exp(z - m) for z in scores[r][h]]
            inv = 1.0 / jnp.sum(functools.reduce(jnp.add, ex), axis=1, keepdims=True)
            probs[r][h] = [e * inv for e in ex]
    return probs


def _ca_stack(tiles, t):
    return jnp.concatenate(
        [jnp.concatenate([tiles[r][h][r + CA_TILES - 1 - t].astype(BF16) for h in range(2)], axis=1)
         for r in _ca_users(t)], axis=0)


def ca_fwd(qkv, bias_tiles, guest=None):
    s = qkv.shape[0]
    nq = s // CA_ROWS

    def body(q_ref, k_ref, v_ref, bias_ref, o_ref):
        masks = _head_masks()
        kks, vvs, inside, _ = _ca_window(pl.program_id(1), k_ref, v_ref, masks)
        probs = _ca_probs(q_ref[...] * QK_SCALE, kks, inside, bias_ref)
        out = [None] * CA_BLOCKS
        for t in range(CA_KEY_BLOCKS):
            part = _dot(_ca_stack(probs, t), vvs[t])
            for r in _ca_users(t):
                piece = _ca_rows(part, t, r)
                out[r] = piece if out[r] is None else out[r] + piece
        o_ref[...] = jnp.concatenate(out, axis=0).astype(BF16)

    return _pallas(
        body, guest, (qkv, qkv, qkv, bias_tiles), name="ca_fwd", grid=(N_PAIRS, nq),
        in_specs=[pl.BlockSpec((CA_ROWS, 128), lambda p, i: (i, 3 * N_PAIRS + p)),
                  pl.BlockSpec((s, 128), lambda p, i: (0, 4 * N_PAIRS + p)),
                  pl.BlockSpec((s, 128), lambda p, i: (0, 5 * N_PAIRS + p)),
                  pl.BlockSpec((2, CA_TILES, ATT_BLOCK, ATT_BLOCK), lambda p, i: (p, 0, 0, 0))],
        out_specs=[pl.BlockSpec((CA_ROWS, 128), lambda p, i: (i, p))],
        out_shape=[jax.ShapeDtypeStruct((s, ATT_WIDTH), BF16)],
        compiler_params=_params(("arbitrary", "arbitrary")),
    )


def ca_bwd(qkv, bias_tiles, do_ca, guest=None):
    s = qkv.shape[0]
    nq = s // CA_ROWS

    def body(q_ref, k_ref, v_ref, bias_ref, do_ref, dq_ref, dk_ref, dv_ref, db_ref, dk_s, dv_s):
        big = pl.program_id(1)

        @pl.when(big == 0)
        def _():
            dk_s[...] = jnp.zeros_like(dk_s)
            dv_s[...] = jnp.zeros_like(dv_s)
            db_ref[...] = jnp.zeros_like(db_ref)

        masks = _head_masks()
        kks, vvs, inside, key_rows = _ca_window(big, k_ref, v_ref, masks)
        q = q_ref[...] * QK_SCALE
        do = do_ref[...]
        probs = _ca_probs(q, kks, inside, bias_ref)
        dps = [[[None] * CA_TILES for _ in range(2)] for _ in range(CA_BLOCKS)]
        for t in range(CA_KEY_BLOCKS):
            users = _ca_users(t)
            dp = _dot_nt(do[ATT_BLOCK * users[0]:ATT_BLOCK * (users[-1] + 1)], vvs[t])
            for r in users:
                for h in range(2):
                    dps[r][h][r + CA_TILES - 1 - t] = _ca_rows(dp, t, r)[:, ATT_BLOCK * h:ATT_BLOCK * (h + 1)]
        dss = [[None, None] for _ in range(CA_BLOCKS)]
        for r in range(CA_BLOCKS):
            for h in range(2):
                delta = jnp.sum(functools.reduce(jnp.add, [p * dp for p, dp in zip(probs[r][h], dps[r][h])]),
                                axis=1, keepdims=True)
                dss[r][h] = [p * (dp - delta) for p, dp in zip(probs[r][h], dps[r][h])]
        for h in range(2):
            for j in range(CA_TILES):
                db_ref[h, j] += functools.reduce(jnp.add, [dss[r][h][j] for r in range(CA_BLOCKS)])
        dq = [None] * CA_BLOCKS
        for t in range(CA_KEY_BLOCKS):
            users = _ca_users(t)
            rows = slice(ATT_BLOCK * users[0], ATT_BLOCK * (users[-1] + 1))
            ds = _ca_stack(dss, t)
            part = _dot(ds, kks[t])
            for r in users:
                piece = _ca_rows(part, t, r)
                dq[r] = piece if dq[r] is None else dq[r] + piece
            dk = _dot_tn(ds, q[rows])
            dv = _dot_tn(_ca_stack(probs, t), do[rows])
            dk_s[key_rows[t], :] += jnp.where(masks[0], dk[:ATT_BLOCK], dk[ATT_BLOCK:])
            dv_s[key_rows[t], :] += jnp.where(masks[0], dv[:ATT_BLOCK], dv[ATT_BLOCK:])
        dq_ref[...] = (jnp.concatenate(dq, axis=0) * QK_SCALE).astype(BF16)

        @pl.when(big == nq - 1)
        def _():
            dk_ref[...] = dk_s[...].astype(BF16)
            dv_ref[...] = dv_s[...].astype(BF16)

    return _pallas(
        body, guest, (qkv, qkv, qkv, bias_tiles, do_ca), name="ca_bwd", grid=(N_PAIRS, nq),
        in_specs=[pl.BlockSpec((CA_ROWS, 128), lambda p, i: (i, 3 * N_PAIRS + p)),
                  pl.BlockSpec((s, 128), lambda p, i: (0, 4 * N_PAIRS + p)),
                  pl.BlockSpec((s, 128), lambda p, i: (0, 5 * N_PAIRS + p)),
                  pl.BlockSpec((2, CA_TILES, ATT_BLOCK, ATT_BLOCK), lambda p, i: (p, 0, 0, 0)),
                  pl.BlockSpec((CA_ROWS, 128), lambda p, i: (i, p))],
        out_specs=[pl.BlockSpec((CA_ROWS, 128), lambda p, i: (i, p)),
                   pl.BlockSpec((s, 128), lambda p, i: (0, p)),
                   pl.BlockSpec((s, 128), lambda p, i: (0, p)),
                   pl.BlockSpec((2, CA_TILES, ATT_BLOCK, ATT_BLOCK), lambda p, i: (p, 0, 0, 0))],
        out_shape=[jax.ShapeDtypeStruct((s, ATT_WIDTH), BF16)] * 3
                  + [jax.ShapeDtypeStruct((2 * N_PAIRS, CA_TILES, ATT_BLOCK, ATT_BLOCK), F32)],
        scratch_shapes=[pltpu.VMEM((s, 128), F32), pltpu.VMEM((s, 128), F32)],
        compiler_params=_params(("arbitrary", "arbitrary")),
    )


def rel_bias_grad(db_tiles):
    heads = db_tiles.shape[0]

    def body(db_ref, out_ref):
        row = lax.broadcasted_iota(jnp.int32, (ATT_BLOCK, ATT_BLOCK), 0)
        col = lax.broadcasted_iota(jnp.int32, (ATT_BLOCK, ATT_BLOCK), 1)
        flip = (row + col == ATT_BLOCK - 1).astype(F32)
        wrapped = col < row
        cp = lax.broadcasted_iota(jnp.int32, (ATT_BLOCK, REL_PAD), 0)
        bins = lax.broadcasted_iota(jnp.int32, (ATT_BLOCK, REL_PAD), 1)
        total = jnp.zeros((heads, REL_PAD), F32)
        for j in range(CA_TILES):
            sums_neg, sums_pos = [], []
            for h in range(heads):
                tile = jnp.dot(db_ref[h, j], flip, preferred_element_type=F32, precision=lax.Precision.HIGHEST)
                tile = pltpu.roll(tile, 0, 1, stride=1, stride_axis=0)
                sums_neg.append(jnp.sum(jnp.where(wrapped, 0.0, tile), axis=0, keepdims=True))
                sums_pos.append(jnp.sum(jnp.where(wrapped, tile, 0.0), axis=0, keepdims=True))
            neg = jnp.concatenate(sums_neg, axis=0)
            pos = jnp.concatenate(sums_pos, axis=0)
            hot_neg = (jnp.clip(ATT_BLOCK * j + cp - (ATT_BLOCK - 1), -REL_CLIP, REL_CLIP) + REL_CLIP == bins)
            hot_pos = (jnp.clip(ATT_BLOCK * j + cp + 1, -REL_CLIP, REL_CLIP) + REL_CLIP == bins)
            total = total + jnp.dot(neg, hot_neg.astype(F32), preferred_element_type=F32,
                                    precision=lax.Precision.HIGHEST)
            total = total + jnp.dot(pos, hot_pos.astype(F32), preferred_element_type=F32,
                                    precision=lax.Precision.HIGHEST)
        out_ref[...] = total

    return pl.pallas_call(
        body, name="rel_bias_grad",
        out_shape=jax.ShapeDtypeStruct((heads, REL_PAD), F32),
        compiler_params=_params(),
    )(db_tiles)


def _assemble_cols(dst_ref, src_ref):
    w = src_ref.shape[2]
    for j in range(N_DEV):
        dst_ref[:, w * j:w * (j + 1)] = src_ref[j]


def _merge(o_sb, o_ca, lg, bg, wsb, wca):
    y_sb = _dot(o_sb.astype(BF16), wsb)
    y_ca = _dot(o_ca.astype(BF16), wca)
    gates = jax.nn.sigmoid(lg + bg)
    g_sb, g_ca = gates[:, :D_MODEL], gates[:, D_MODEL:]
    return y_sb, y_ca, g_sb, g_ca, g_sb * y_sb + g_ca * y_ca


def mix_bwd(du1, o_sb, o_ca, lg, b_gate, wg_sb, wg_ca, w_out, tm=256, guest=None):
    s = du1.shape[0]
    last = s // tm - 1

    def body(du_ref, osb_ref, oca_ref, lg_ref, bg_ref, wsb_ref, wca_ref, wout_ref,
             dlg_ref, dosb_ref, doca_ref, dbg_ref, gout_ref, gsb_ref, gca_ref, wsb_s, wca_s, gsb_s, gca_s):
        step = pl.program_id(0)

        @pl.when(step == 0)
        def _():
            _assemble_cols(wsb_s, wsb_ref)
            _assemble_cols(wca_s, wca_ref)
            for ref in (dbg_ref, gout_ref, gsb_s, gca_s):
                ref[...] = jnp.zeros_like(ref)

        o_sb, o_ca, du = osb_ref[...].astype(BF16), oca_ref[...].astype(BF16), du_ref[...].astype(BF16)
        y_sb, y_ca, g_sb, g_ca, merged = _merge(o_sb, o_ca, lg_ref[...], bg_ref[...], wsb_s[...], wca_s[...])
        dm = _dot_nt(du, wout_ref[...])
        dl_sb = dm * y_sb * g_sb * (1.0 - g_sb)
        dl_ca = dm * y_ca * g_ca * (1.0 - g_ca)
        dlg_ref[:, :D_MODEL] = dl_sb.astype(BF16)
        dlg_ref[:, D_MODEL:] = dl_ca.astype(BF16)
        dbg_ref[:, :D_MODEL] += jnp.sum(dl_sb, axis=0, keepdims=True)
        dbg_ref[:, D_MODEL:] += jnp.sum(dl_ca, axis=0, keepdims=True)
        dy_sb = (dm * g_sb).astype(BF16)
        dy_ca = (dm * g_ca).astype(BF16)
        dosb_ref[...] = _dot_nt(dy_sb, wsb_s[...]).astype(BF16)
        doca_ref[...] = _dot_nt(dy_ca, wca_s[...]).astype(BF16)
        gout_ref[...] += _dot_tn(merged.astype(BF16), du)
        gsb_s[...] += _dot_tn(o_sb, dy_sb)
        gca_s[...] += _dot_tn(o_ca, dy_ca)

        @pl.when(step == last)
        def _():
            for j in range(N_DEV):
                cols = slice(PROJ_SHARD * j, PROJ_SHARD * (j + 1))
                gsb_ref[j] = gsb_s[:, cols]
                gca_ref[j] = gca_s[:, cols]

    rows = lambda w: pl.BlockSpec((tm, w), lambda i: (i, 0))
    whole = lambda shape: pl.BlockSpec(shape, lambda i: (0,) * len(shape))
    stack = (N_DEV, ATT_WIDTH, PROJ_SHARD)
    return _pallas(
        body, guest, (du1, o_sb, o_ca, lg, _row(b_gate), wg_sb, wg_ca, w_out), name="mix_bwd", grid=(s // tm,),
        in_specs=[rows(D_MODEL), rows(ATT_WIDTH), rows(ATT_WIDTH), rows(GATE_COLS), whole((1, GATE_COLS)),
                  whole(stack), whole(stack), whole((D_MODEL, D_MODEL))],
        out_specs=[rows(GATE_COLS), rows(ATT_WIDTH), rows(ATT_WIDTH), whole((1, GATE_COLS)),
                   whole((D_MODEL, D_MODEL)), whole(stack), whole(stack)],
        out_shape=[jax.ShapeDtypeStruct((s, GATE_COLS), BF16),
                   jax.ShapeDtypeStruct((s, ATT_WIDTH), BF16), jax.ShapeDtypeStruct((s, ATT_WIDTH), BF16),
                   jax.ShapeDtypeStruct((1, GATE_COLS), F32), jax.ShapeDtypeStruct((D_MODEL, D_MODEL), F32),
                   jax.ShapeDtypeStruct(stack, F32), jax.ShapeDtypeStruct(stack, F32)],
        scratch_shapes=[pltpu.VMEM((ATT_WIDTH, D_MODEL), BF16), pltpu.VMEM((ATT_WIDTH, D_MODEL), BF16),
                        pltpu.VMEM((ATT_WIDTH, D_MODEL), F32), pltpu.VMEM((ATT_WIDTH, D_MODEL), F32)],
        compiler_params=_params(("arbitrary",)),
    )


def _ln_stats(u):
    mu = jnp.mean(u, axis=1, keepdims=True)
    cen = u - mu
    var = jnp.mean(cen * cen, axis=1, keepdims=True)
    rstd = lax.rsqrt(var + LN_EPS)
    return cen * rstd, rstd


def _ln_bwd(dy, xhat, rstd, gain):
    dyg = dy * gain
    m1 = jnp.mean(dyg, axis=1, keepdims=True)
    m2 = jnp.mean(dyg * xhat, axis=1, keepdims=True)
    return rstd * (dyg - m1 - xhat * m2)


def _load_mlp_weights(win_hbm, wout_hbm, win_s, wout_s, sems):
    copies = [pltpu.make_async_copy(win_hbm.at[k], win_s.at[:, pl.ds(FF_SHARD * k, FF_SHARD)], sems.at[k])
              for k in range(N_DEV)]
    copies.append(pltpu.make_async_copy(wout_hbm, wout_s, sems.at[N_DEV]))
    for cp in copies:
        cp.start()
    return copies


MLP_WEIGHT_SCRATCH = [pltpu.VMEM((D_MODEL, D_FF), BF16), pltpu.VMEM((D_FF, D_MODEL), BF16),
                      pltpu.SemaphoreType.DMA((N_DEV + 1,))]


def mlp_block(x, o_sb, o_ca, lg, b_gate, wg_sb, wg_ca, w_proj, target, wg_in, w_out,
              ln1_g, ln1_b, ln2_g, ln2_b, tm=256):
    s = x.shape[0]

    def body(x_ref, osb_ref, oca_ref, lg_ref, bg_ref, wsb_hbm, wca_hbm, wproj_hbm, t_ref, win_hbm, wout_hbm,
             g1_ref, b1_ref, g2_ref, b2_ref,
             a_ref, x1b_ref, dhm_ref, du2b_ref, du1_ref, loss_ref, dg2_ref, db2_ref, dg1_ref, db1_ref,
             win_s, wout_s, sems, relu_s, wsb_s, wca_s, wproj_s, proj_sems):
        @pl.when(pl.program_id(0) == 0)
        def _():
            copies = _load_mlp_weights(win_hbm, wout_hbm, win_s, wout_s, sems)
            for j in range(N_DEV):
                cols = pl.ds(PROJ_SHARD * j, PROJ_SHARD)
                copies.append(pltpu.make_async_copy(wsb_hbm.at[j], wsb_s.at[:, cols], proj_sems.at[j]))
                copies.append(pltpu.make_async_copy(wca_hbm.at[j], wca_s.at[:, cols], proj_sems.at[N_DEV + j]))
            copies.append(pltpu.make_async_copy(wproj_hbm, wproj_s, proj_sems.at[2 * N_DEV]))
            for cp in copies[N_DEV + 1:]:
                cp.start()
            for ref in (loss_ref, dg2_ref, db2_ref, dg1_ref, db1_ref):
                ref[...] = jnp.zeros_like(ref)
            for cp in copies:
                cp.wait()

        merged = _merge(osb_ref[...], oca_ref[...], lg_ref[...], bg_ref[...], wsb_s[...], wca_s[...])[4]
        u1 = ALPHA * x_ref[...] + _dot(merged.astype(BF16), wproj_s[...])
        xhat1, rstd1 = _ln_stats(u1)
        x1 = xhat1 * g1_ref[...] + b1_ref[...]
        x1b = x1.astype(BF16)
        x1b_ref[...] = x1b
        for k in range(N_DEV):
            cols = slice(FF_SHARD * k, FF_SHARD * (k + 1))
            r = jnp.maximum(_dot(x1b, win_s[:, cols]), 0.0)
            relu_s[:, cols] = r
            a_ref[:, cols] = (r * r).astype(BF16)
        xhat2, rstd2 = _ln_stats(ALPHA * x1 + _dot(a_ref[...], wout_s[...]))
        diff = xhat2 * g2_ref[...] + b2_ref[...] - t_ref[...]
        per_token = jnp.mean(diff * diff, axis=1, keepdims=True)
        loss_ref[...] += 0.5 * jnp.sum(per_token, axis=0, keepdims=True)
        dy = diff * (1.0 / D_MODEL)
        dg2_ref[...] += jnp.sum(dy * xhat2, axis=0, keepdims=True)
        db2_ref[...] += jnp.sum(dy, axis=0, keepdims=True)
        du2 = _ln_bwd(dy, xhat2, rstd2, g2_ref[...])
        du2b = du2.astype(BF16)
        du2b_ref[...] = du2b
        for k in range(N_DEV):
            cols = slice(FF_SHARD * k, FF_SHARD * (k + 1))
            da = _dot_nt(du2b, wout_s[cols, :])
            dhm_ref[:, cols] = (2.0 * relu_s[:, cols] * da).astype(BF16)
        dx1 = ALPHA * du2 + _dot_nt(dhm_ref[...], win_s[...])
        dg1_ref[...] += jnp.sum(dx1 * xhat1, axis=0, keepdims=True)
        db1_ref[...] += jnp.sum(dx1, axis=0, keepdims=True)
        du1_ref[...] = _ln_bwd(dx1, xhat1, rstd1, g1_ref[...])

    rows = pl.BlockSpec((tm, D_MODEL), lambda i: (i, 0))
    vec = pl.BlockSpec((1, D_MODEL), lambda i: (0, 0))
    ff = pl.BlockSpec((tm, D_FF), lambda i: (i, 0))
    wide = jax.ShapeDtypeStruct((s, D_FF), BF16)
    narrow = jax.ShapeDtypeStruct((s, D_MODEL), BF16)
    stat = jax.ShapeDtypeStruct((1, D_MODEL), F32)
    tile = lambda w: pl.BlockSpec((tm, w), lambda i: (i, 0))
    return pl.pallas_call(
        body, name="mlp", grid=(s // tm,),
        in_specs=[rows, tile(ATT_WIDTH), tile(ATT_WIDTH), tile(GATE_COLS), pl.BlockSpec((1, GATE_COLS), lambda i: (0, 0)),
                  HBM_SPEC, HBM_SPEC, HBM_SPEC, rows, HBM_SPEC, HBM_SPEC, vec, vec, vec, vec],
        out_specs=[ff, rows, ff, rows, rows, pl.BlockSpec((1, 1), lambda i: (0, 0)), vec, vec, vec, vec],
        out_shape=[wide, narrow, wide, narrow, jax.ShapeDtypeStruct((s, D_MODEL), F32),
                   jax.ShapeDtypeStruct((1, 1), F32), stat, stat, stat, stat],
        scratch_shapes=MLP_WEIGHT_SCRATCH + [
            pltpu.VMEM((tm, D_FF), F32), pltpu.VMEM((ATT_WIDTH, D_MODEL), BF16), pltpu.VMEM((ATT_WIDTH, D_MODEL), BF16),
            pltpu.VMEM((D_MODEL, D_MODEL), BF16), pltpu.SemaphoreType.DMA((2 * N_DEV + 1,))],
        compiler_params=_params(("arbitrary",)),
    )(x, o_sb, o_ca, lg, _row(b_gate), wg_sb, wg_ca, w_proj, target, wg_in, w_out,
      _row(ln1_g), _row(ln1_b), _row(ln2_g), _row(ln2_b))


def dh_slab(pieces, tm=512):
    s = pieces[0].shape[0]
    widths = [p.shape[1] for p in pieces]
    assert sum(widths) == IN_COLS
    n = len(pieces)

    def body(*refs):
        dh_ref = refs[n]
        off = 0
        for ref, w in zip(refs[:n], widths):
            dh_ref[:, off:off + w] = ref[...].astype(BF16)
            off += w

    rows = lambda w: pl.BlockSpec((tm, w), lambda i: (i, 0))
    return pl.pallas_call(
        body, name="dh_slab", grid=(s // tm,),
        in_specs=[rows(w) for w in widths], out_specs=rows(IN_COLS),
        out_shape=jax.ShapeDtypeStruct((s, IN_COLS), BF16),
        compiler_params=_params(("arbitrary",)),
    )(*pieces)


def in_bwd(du1, dh, wg_in, tm=512, guest=None):
    s = du1.shape[0]

    def body(du_ref, dh_ref, w_hbm, gx_ref, pairs_s, sems):
        @pl.when(pl.program_id(0) == 0)
        def _():
            _load_in_pairs(w_hbm, pairs_s, sems)

        acc = ALPHA * du_ref[...]
        for j in range(N_DEV // 2):
            acc = acc + _dot_nt(dh_ref[:, IN_PAIR * j:IN_PAIR * (j + 1)], pairs_s[j])
        gx_ref[...] = acc

    rows = lambda w: pl.BlockSpec((tm, w), lambda i: (i, 0))
    return _pallas(
        body, guest, (du1, dh, wg_in), name="in_bwd", grid=(s // tm,), scratch_shapes=IN_PAIR_SCRATCH,
        in_specs=[rows(D_MODEL), rows(IN_COLS), HBM_SPEC],
        out_specs=[rows(D_MODEL)],
        out_shape=[jax.ShapeDtypeStruct((s, D_MODEL), F32)],
        compiler_params=_params(("arbitrary",)),
    )


def wgrad(name, a, b, tm, tn, shard_cols=None, tk=2048):
    kk, m = a.shape
    n = b.shape[1]
    while kk % tk:
        tk //= 2
    nk = kk // tk
    w = shard_cols

    def body(a_ref, b_ref, out_ref, acc_s):
        k = pl.program_id(2)

        @pl.when(k == 0)
        def _():
            acc_s[...] = jnp.zeros_like(acc_s)

        acc_s[...] += _dot_tn(a_ref[...].astype(BF16), b_ref[...].astype(BF16))

        @pl.when(k == nk - 1)
        def _():
            if w is None:
                out_ref[...] = acc_s[...]
            else:
                for j in range(tn // w):
                    out_ref[j] = acc_s[:, w * j:w * (j + 1)]

    if w is None:
        out_spec = pl.BlockSpec((tm, tn), lambda i, j, k: (i, j))
        out_shape = jax.ShapeDtypeStruct((m, n), F32)
    else:
        out_spec = pl.BlockSpec((tn // w, tm, w), lambda i, j, k: (j, i, 0))
        out_shape = jax.ShapeDtypeStruct((n // w, m, w), F32)
    return pl.pallas_call(
        body, name=name, grid=(m // tm, n // tn, nk),
        in_specs=[pl.BlockSpec((tk, tm), lambda i, j, k: (k, i)),
                  pl.BlockSpec((tk, tn), lambda i, j, k: (k, j))],
        out_specs=out_spec, out_shape=out_shape,
        scratch_shapes=[pltpu.VMEM((tm, tn), F32)],
        compiler_params=_params(("arbitrary", "arbitrary", "arbitrary")),
    )(a, b)


def adamw(name, w, m, v, parts, tm=None, guest=None):
    rows, cols = w.shape
    tm = rows if tm is None else min(tm, rows)
    n = len(parts)
    c1 = 1.0 - ADAM_B1 ** ADAM_STEP
    c2 = 1.0 - ADAM_B2 ** ADAM_STEP

    def body(*refs):
        w_ref, m_ref, v_ref = refs[:3]
        part_refs = refs[3:3 + n]
        g_ref, d_ref, nm_ref, nv_ref = refs[3 + n:]
        g = None
        for ref, (_, index) in zip(part_refs, parts):
            term = (ref[...] if index is None else ref[0]).astype(F32)
            g = term if g is None else g + term
        new_m = ADAM_B1 * m_ref[...] + (1.0 - ADAM_B1) * g
        new_v = ADAM_B2 * v_ref[...] + (1.0 - ADAM_B2) * (g * g)
        m_hat = new_m / c1
        v_hat = new_v / c2
        g_ref[...] = g
        d_ref[...] = -ADAM_LR * (m_hat / (jnp.sqrt(v_hat) + ADAM_EPS) + ADAM_WD * w_ref[...])
        nm_ref[...] = new_m
        nv_ref[...] = new_v

    spec = pl.BlockSpec((tm, cols), lambda i: (i, 0))
    shape = jax.ShapeDtypeStruct((rows, cols), F32)

    def part_spec(index):
        if index is None:
            return spec
        return pl.BlockSpec((1, tm, cols), lambda i: (index, i, 0))

    return _pallas(
        body, guest, (w, m, v, *[array for array, _ in parts]), name=name, grid=(rows // tm,),
        in_specs=[spec] * 3 + [part_spec(index) for _, index in parts],
        out_specs=[spec] * 4, out_shape=[shape] * 4,
        compiler_params=_params(("arbitrary",)),
    )


def _place():
    x, y, c = lax.axis_index("x"), lax.axis_index("y"), lax.axis_index("c")
    other_chips = [(1 - x, y), (x, 1 - y), (1 - x, 1 - y)]
    return x, y, c, other_chips


def place_shards(arrays):
    n = len(arrays)

    def body(*refs):
        ins, outs, stage, sems = refs[:n], refs[n:2 * n], refs[2 * n:3 * n], refs[3 * n]
        x, y, c, _ = _place()
        copies = []
        for a in range(n):
            stage[a][...] = ins[a][...].astype(BF16)
            copies.append(pltpu.make_async_copy(stage[a], outs[a].at[4 * x + 2 * y + c], sems.at[a]))
            copies[-1].start()
        for cp in copies:
            cp.wait()

    return pl.pallas_call(
        body, name="place_shards",
        in_specs=[pl.BlockSpec(memory_space=pltpu.VMEM)] * n, out_specs=[HBM_SPEC] * n,
        out_shape=[jax.ShapeDtypeStruct((N_DEV,) + a.shape, BF16) for a in arrays],
        scratch_shapes=[pltpu.VMEM(a.shape, BF16) for a in arrays] + [pltpu.SemaphoreType.DMA((n,))],
        compiler_params=_params(),
    )(*arrays)


def all_gather(name, arrays):
    n = len(arrays)

    def body(*refs):
        ins, outs = refs[:n], refs[n:2 * n]
        send_sems, recv_sems, local_sems = refs[2 * n:]
        x, y, c, chips = _place()
        me = 4 * x + 2 * y + c
        sibling = (x, y, 1 - c)

        def copy(a, k, block, to, src=None):
            return pltpu.make_async_remote_copy(
                src_ref=outs[a].at[block] if src is None else src, dst_ref=outs[a].at[block],
                send_sem=send_sems.at[a, k], recv_sem=recv_sems.at[a, k],
                device_id=to, device_id_type=MESH)

        started = []
        local = [pltpu.make_async_copy(ins[a], outs[a].at[me], local_sems.at[a]) for a in range(n)]
        for a in range(n):
            local[a].start()
            sends = [copy(a, 0, me, sibling, src=ins[a])]
            sends += [copy(a, 1 + j, me, (cx, cy, c), src=ins[a]) for j, (cx, cy) in enumerate(chips)]
            for cp in sends:
                cp.start()
            started += sends
        for a in range(n):
            for j, (cx, cy) in enumerate(chips):
                block = 4 * cx + 2 * cy + c
                copy(a, 1 + j, block, sibling).wait_recv()
                passed = copy(a, 4 + j, block, sibling)
                passed.start()
                started.append(passed)
        for a in range(n):
            copy(a, 0, 4 * x + 2 * y + (1 - c), sibling).wait_recv()
            for j, (cx, cy) in enumerate(chips):
                copy(a, 4 + j, 4 * cx + 2 * cy + (1 - c), sibling).wait_recv()
        for cp in started:
            cp.wait_send()
        for cp in local:
            cp.wait()

    return pl.pallas_call(
        body, name=name,
        in_specs=[pl.BlockSpec(memory_space=pltpu.VMEM)] * n, out_specs=[HBM_SPEC] * n,
        out_shape=[jax.ShapeDtypeStruct((N_DEV,) + a.shape, a.dtype) for a in arrays],
        scratch_shapes=[pltpu.SemaphoreType.DMA((n, 7)), pltpu.SemaphoreType.DMA((n, 7)),
                        pltpu.SemaphoreType.DMA((n,))],
    )(*arrays)


def _gather_first_plan(n):
    def plan(zones):
        x, y, c, chips = _place()
        me = 4 * x + 2 * y + c
        copies = []
        for zone in zones[:n]:
            copies.append((zone.at[me], zone.at[me], (x, y, 1 - c), zone.at[4 * x + 2 * y + (1 - c)]))
            copies += [(zone.at[me], zone.at[me], (cx, cy, c), zone.at[4 * cx + 2 * cy + c]) for cx, cy in chips]
        return copies
    return plan


def _gather_second_plan(n):
    def plan(zones):
        x, y, c, chips = _place()
        return [(zone.at[4 * cx + 2 * cy + c], zone.at[4 * cx + 2 * cy + c], (x, y, 1 - c),
                 zone.at[4 * cx + 2 * cy + (1 - c)]) for zone in zones[:n] for cx, cy in chips]
    return plan


def _both_plans(first, n_first, second):
    return lambda bufs: first(bufs[:n_first]) + second(bufs[n_first:])


def _sibling_plan(n, rows=None):
    def plan(bufs):
        x, y, c, _ = _place()
        if rows is None:
            return [(bufs[a].at[:, 1 - c], bufs[n + a], (x, y, 1 - c), bufs[n + a]) for a in range(n)]
        return [(bufs[a].at[:, 1 - c, rows], bufs[n + a].at[:, rows], (x, y, 1 - c), bufs[n + a].at[:, rows])
                for a in range(n)]
    return plan


def _chips_plan(n):
    def plan(bufs):
        _, _, c, chips = _place()
        return [(bufs[a].at[2 * cx + cy], bufs[n + a].at[j], (cx, cy, c), bufs[n + a].at[j])
                for a in range(n) for j, (cx, cy) in enumerate(chips)]
    return plan


def chip_sum(name, grad, landed, place, tr=512):
    _, _, rows, cols = grad.shape
    tr = min(tr, rows)

    def body(place_ref, g_ref, l_ref, own_ref, out_ref):
        total = g_ref[0, 0] + l_ref[0]
        out_ref[0] = total.astype(BF16)

        @pl.when(pl.program_id(1) == place_ref[1])
        def _():
            own_ref[...] = total

    return pl.pallas_call(
        body, name=name,
        grid_spec=pltpu.PrefetchScalarGridSpec(
            num_scalar_prefetch=1, grid=(rows // tr, 4),
            in_specs=[pl.BlockSpec((1, 1, tr, cols), lambda i, ch, pos: (ch, pos[0], i, 0)),
                      pl.BlockSpec((1, tr, cols), lambda i, ch, pos: (ch, i, 0))],
            out_specs=[pl.BlockSpec((tr, cols), lambda i, ch, pos: (i, 0)),
                       pl.BlockSpec((1, tr, cols), lambda i, ch, pos: (ch, i, 0))]),
        out_shape=[jax.ShapeDtypeStruct((rows, cols), F32), jax.ShapeDtypeStruct((4, rows, cols), BF16)],
        compiler_params=_params(("arbitrary", "arbitrary")),
    )(place, grad, landed)


BIG =("w_in", "w_sb_proj", "w_ca_proj", "w_out", "w_mlp_in", "w_mlp_out")
SMALL = ("b_gate", "rel_bias", "ln1_g", "ln1_b", "ln2_g", "ln2_b")
NAMES = ("w_in", "b_gate", "w_sb_proj", "w_ca_proj", "rel_bias", "w_out", "ln1_g", "ln1_b",
         "w_mlp_in", "w_mlp_out", "ln2_g", "ln2_b")
REL_COLS = 2 * REL_CLIP + 1


def _pack_small(t, scalar=None):
    rel = t["rel_bias"]
    if rel.shape[1] != REL_PAD:
        rel = jnp.pad(rel, ((0, 0), (0, REL_PAD - rel.shape[1])))
    last = jnp.zeros((128,), F32) if scalar is None else jnp.pad(scalar.reshape(1), (0, 127))
    flat = [t["b_gate"].reshape(-1), rel.reshape(-1)] + [t[n].reshape(-1) for n in SMALL[2:]] + [last]
    return jnp.concatenate(flat).reshape(-1, 128)


def _unpack_small(p):
    flat = p.reshape(-1)
    out, off = {}, 0
    for n, size in (("b_gate", GATE_COLS), ("rel_bias", 8 * REL_PAD)) + tuple((n, D_MODEL) for n in SMALL[2:]):
        out[n] = flat[off:off + size]
        off += size
    out["rel_bias"] = out["rel_bias"].reshape(8, REL_PAD)[:, :REL_COLS]
    return out


def kernel(x, w_in, b_gate, w_sb_proj, w_ca_proj, rel_bias, w_out, ln1_g, ln1_b, w_mlp_in, w_mlp_out, ln2_g, ln2_b, loss_target, m_w_in, m_b_gate, m_w_sb_proj, m_w_ca_proj, m_rel_bias, m_w_out, m_ln1_g, m_ln1_b, m_w_mlp_in, m_w_mlp_out, m_ln2_g, m_ln2_b, v_w_in, v_b_gate, v_w_sb_proj, v_w_ca_proj, v_rel_bias, v_w_out, v_ln1_g, v_ln1_b, v_w_mlp_in, v_w_mlp_out, v_ln2_g, v_ln2_b):
    w = dict(w_in=w_in, b_gate=b_gate, w_sb_proj=w_sb_proj, w_ca_proj=w_ca_proj, rel_bias=rel_bias, w_out=w_out,
             ln1_g=ln1_g, ln1_b=ln1_b, w_mlp_in=w_mlp_in, w_mlp_out=w_mlp_out, ln2_g=ln2_g, ln2_b=ln2_b)
    m = dict(w_in=m_w_in, b_gate=m_b_gate, w_sb_proj=m_w_sb_proj, w_ca_proj=m_w_ca_proj, rel_bias=m_rel_bias,
             w_out=m_w_out, ln1_g=m_ln1_g, ln1_b=m_ln1_b, w_mlp_in=m_w_mlp_in, w_mlp_out=m_w_mlp_out,
             ln2_g=m_ln2_g, ln2_b=m_ln2_b)
    v = dict(w_in=v_w_in, b_gate=v_b_gate, w_sb_proj=v_w_sb_proj, w_ca_proj=v_w_ca_proj, rel_bias=v_rel_bias,
             w_out=v_w_out, ln1_g=v_ln1_g, ln1_b=v_ln1_b, w_mlp_in=v_w_mlp_in, w_mlp_out=v_w_mlp_out,
             ln2_g=v_ln2_g, ln2_b=v_ln2_b)

    x, target = x[0], loss_target[0]
    c = lax.axis_index("c")
    place = jnp.stack([c, 2 * lax.axis_index("x") + lax.axis_index("y")]).astype(jnp.int32)
    grads, deltas, new_m, new_v = {}, {}, {}, {}

    def by_owner(n, g):
        return g.reshape((4, 2) + w[n].shape)

    def finish(n, own, arrived, guest=None):
        out = adamw("adamw_" + n, w[n], m[n], v[n], [(own, None), (arrived, 0), (arrived, 1), (arrived, 2)],
                    tm=512, guest=guest)
        out, bufs = (out, None) if guest is None else out
        grads[n], deltas[n], new_m[n], new_v[n] = out
        return bufs

    def siblings(names, g):
        bufs = [by_owner(n, g[n]) for n in names] + [lax.empty((4,) + w[n].shape, F32) for n in names]
        return Exchange(tuple(bufs), _sibling_plan(len(names)), len(names))

    def chip_sums(names, bufs):
        k = len(names)
        own, sums = zip(*[chip_sum("chip_sum_" + n, bufs[a], bufs[k + a], place) for a, n in enumerate(names)])
        zones = [lax.empty((3,) + w[n].shape, BF16) for n in names]
        return own, Exchange(tuple(sums) + tuple(zones), _chips_plan(k), 3 * k)

    proj, ffn = ("w_sb_proj", "w_ca_proj", "w_out"), ("w_mlp_in", "w_mlp_out")
    zones = place_shards([w[n] for n in proj + ffn])
    proj_zones, ffn_zones = tuple(zones[:len(proj)]), tuple(zones[len(proj):])
    wg_in = all_gather("gather_w_in", [w_in.astype(BF16)])[0]
    (qkv, lg, xb), proj_zones = in_proj(
        x, wg_in, guest=Exchange(proj_zones, _gather_first_plan(len(proj)), 4 * len(proj)))
    bias_tiles = ca_bias_tiles(rel_bias)
    both = Exchange(ffn_zones + tuple(proj_zones),
                    _both_plans(_gather_first_plan(len(ffn)), len(ffn), _gather_second_plan(len(proj))),
                    4 * len(ffn) + 3 * len(proj))
    (o_sb,), zones = sb_fwd(qkv, both)
    ffn_zones, proj_zones = tuple(zones[:len(ffn)]), zones[len(ffn):]
    (o_ca,), ffn_zones = ca_fwd(qkv, bias_tiles, Exchange(ffn_zones, _gather_second_plan(len(ffn)), 3 * len(ffn)))
    wg = dict(zip(proj + ffn, list(proj_zones) + list(ffn_zones)))
    wg_sb, wg_ca, wg_mi = wg["w_sb_proj"], wg["w_ca_proj"], wg["w_mlp_in"]
    wg_mo = wg["w_mlp_out"].reshape(D_FF, D_MODEL)
    w_out_b = wg["w_out"].reshape(D_MODEL, D_MODEL)

    a, x1b, dhm, du2b, du1, loss, dg2, db2, dg1, db1 = mlp_block(
        x, o_sb, o_ca, lg, b_gate, wg_sb, wg_ca, w_out_b, target, wg_mi, wg_mo, ln1_g, ln1_b, ln2_g, ln2_b)
    g = {}
    g["w_mlp_in"] = wgrad("wgrad_mlp_in", x1b, dhm, tm=D_MODEL, tn=FF_SHARD, shard_cols=FF_SHARD, tk=x.shape[0])
    g["w_mlp_out"] = wgrad("wgrad_mlp_out", a, du2b, tm=FF_SHARD, tn=D_MODEL, tk=x.shape[0])

    mlp = ("w_mlp_in", "w_mlp_out")
    (dlg, do_sb, do_ca, dbg, g["w_out"], g["w_sb_proj"], g["w_ca_proj"]), bufs = mix_bwd(
        du1, o_sb, o_ca, lg, b_gate, wg_sb, wg_ca, w_out_b, guest=siblings(mlp, g))
    own_mlp, mlp_chips = chip_sums(mlp, bufs)

    att = ("w_out", "w_sb_proj", "w_ca_proj")
    att_siblings = siblings(att, g)
    both = Exchange(mlp_chips.bufs + att_siblings.bufs,
                    _both_plans(mlp_chips.plan, len(mlp_chips.bufs), att_siblings.plan),
                    mlp_chips.copies + att_siblings.copies)
    (dq_sb, dk_sb, dv_sb), bufs = sb_bwd(qkv, o_sb, do_sb, guest=both)
    arrived_mlp, bufs = bufs[len(mlp):len(mlp_chips.bufs)], bufs[len(mlp_chips.bufs):]
    own_att, att_chips = chip_sums(att, bufs)
    (dq_ca, dk_ca, dv_ca, db_tiles), bufs = ca_bwd(qkv, bias_tiles, do_ca, guest=att_chips)
    arrived_att = bufs[len(att):]
    g_rel = rel_bias_grad(db_tiles)

    dh = dh_slab([dq_sb, dk_sb, dv_sb, dq_ca, dk_ca, dv_ca, dlg])
    g["w_in"] = wgrad("wgrad_in", xb, dh, tm=D_MODEL, tn=2 * IN_SHARD, shard_cols=IN_SHARD)
    last = ("w_in",)
    half = w["w_in"].shape[0] // 2
    bufs = siblings(last, g).bufs
    for at, n in enumerate(mlp):
        swap = Exchange(tuple(bufs), _sibling_plan(len(last), pl.ds(half * at, half)), len(last))
        bufs = finish(n, own_mlp[at], arrived_mlp[at], guest=swap)
    for n, own, arrived in zip(att, own_att, arrived_att):
        finish(n, own, arrived)
    own_in, in_chips = chip_sums(last, bufs)
    (grad_x,), bufs = in_bwd(du1, dh, wg_in, guest=in_chips)
    finish(last[0], own_in[0], bufs[1])

    small = dict(b_gate=dbg, rel_bias=g_rel, ln1_g=dg1, ln1_b=db1, ln2_g=dg2, ln2_b=db2)
    parts = all_gather("gather_small_grads", [_pack_small(small, loss[0, 0])])[0]
    packed = adamw("adamw_small", _pack_small(w), _pack_small(m), _pack_small(v),
                   [(parts, d) for d in range(N_DEV)])
    loss = packed[0][-1, 0]
    for out, p in zip((grads, deltas, new_m, new_v), packed):
        for n, val in _unpack_small(p).items():
            out[n] = val.reshape(w[n].shape)

    return (loss, grad_x[None], *[grads[n] for n in NAMES], *[deltas[n] for n in NAMES],
            *[new_m[n] for n in NAMES], *[new_v[n] for n in NAMES])
```
